```python
import math
import jax, jax.numpy as jnp
from jax import lax
import numpy as np

D_MODEL = 1024
BATCH = 8
SEQ = 8192
DEPTH = 1

GRID_W = 64
CTX_LEN = 256
N_HEADS = 8
QK_NOPE = 64
QK_ROPE = 32
V_HEAD = 64
Q_LORA = 256
KV_LORA = 128
ROPE_THETA = 10000.0
Q_BLOCK = 128
ATTN_SCALE = 1.0 / math.sqrt(QK_NOPE + QK_ROPE)
HY_WIDTH = 512
HY_ORDER = 2
HY_SHORT = 3
HY_BANDS = 8
HY_EMB = 1 + 2 * HY_BANDS
HY_FILTER_HIDDEN = 64
HY_FAST_DECAY = 0.3
HY_SLOW_DECAY = 1.5
HY_DECAY_TARGET = 1e-2
N_EXPERTS = 64
N_GROUPS = 8
TOPK_GROUPS = 4
TOP_K = 8
EXPERT_FF = 256
SHARED_FF = 256
ROUTE_SCALE = 2.5
EXPERT_BLOCK = 512
NORM_EPS = 1e-6
IN_WIDTH = Q_LORA + KV_LORA + QK_ROPE + 3 * HY_WIDTH + 2 * D_MODEL

kernel_name = "hybrid_mla_hyena_moe_diffusion_block"


def rmsnorm(x, g):
    xf = x.astype(jnp.float32)
    y = xf * lax.rsqrt(jnp.mean(xf * xf, axis=-1, keepdims=True) + NORM_EPS)
    return (y * g.astype(jnp.float32)).astype(x.dtype)


def modulate(x, shift, scale):
    return x * (1 + scale) + shift


def _rotate(x, ang):
    xf = x.astype(jnp.float32)
    x1, x2 = jnp.split(xf, 2, axis=-1)
    cos, sin = jnp.cos(ang), jnp.sin(ang)
    return jnp.concatenate([x1 * cos - x2 * sin, x1 * sin + x2 * cos], axis=-1).astype(x.dtype)


def axial_angles(n_tokens):
    rows = n_tokens // GRID_W
    row = jnp.broadcast_to(jnp.arange(rows, dtype=jnp.float32)[:, None], (rows, GRID_W)).reshape(-1)
    col = jnp.broadcast_to(jnp.arange(GRID_W, dtype=jnp.float32)[None, :], (rows, GRID_W)).reshape(-1)
    half = QK_ROPE // 2
    inv_freq = ROPE_THETA ** (-jnp.arange(0, half, 2, dtype=jnp.float32) / half)
    return row[:, None] * inv_freq, col[:, None] * inv_freq


def rope2d(x, ang_row, ang_col):
    half = QK_ROPE // 2
    return jnp.concatenate([_rotate(x[..., :half], ang_row), _rotate(x[..., half:], ang_col)], axis=-1)


def split_proj(p):
    cuts = [Q_LORA, Q_LORA + KV_LORA, Q_LORA + KV_LORA + QK_ROPE,
            Q_LORA + KV_LORA + QK_ROPE + 3 * HY_WIDTH]
    return jnp.split(p, cuts, axis=-1)


def mla_q(q_lat, q_norm, w_uq):
    B, L, _ = q_lat.shape
    q = (rmsnorm(q_lat, q_norm) @ w_uq).reshape(B, L, N_HEADS, QK_NOPE + QK_ROPE)
    return q[..., :QK_NOPE], q[..., QK_NOPE:]


def mla_kv(kv_lat, kv_norm, w_ukv):
    B, L, _ = kv_lat.shape
    kv = (rmsnorm(kv_lat, kv_norm) @ w_ukv).reshape(B, L, N_HEADS, QK_NOPE + V_HEAD)
    return kv[..., :QK_NOPE], kv[..., QK_NOPE:]


def attend(qn, qr, kn, kr, v):
    s = jnp.einsum('bqhd,bkhd->bhqk', qn, kn) + jnp.einsum('bqhr,bkr->bhqk', qr, kr)
    p = jax.nn.softmax(s.astype(jnp.float32) * ATTN_SCALE, axis=-1).astype(v.dtype)
    return jnp.einsum('bhqk,bkhd->bqhd', p, v)


def blocked_attention(qn, qr, kn, kr, v):
    B, S = qn.shape[:2]
    nb = S // Q_BLOCK

    def to_blocks(t):
        return jnp.moveaxis(t.reshape(B, nb, Q_BLOCK, *t.shape[2:]), 1, 0)

    out = lax.map(lambda qb: attend(qb[0], qb[1], kn, kr, v), (to_blocks(qn), to_blocks(qr)))
    return jnp.moveaxis(out, 0, 1).reshape(B, S, N_HEADS * V_HEAD)


def short_conv(u, w, b):
    L = u.shape[1]
    pad = HY_SHORT // 2
    up = jnp.pad(u, ((0, 0), (pad, HY_SHORT - 1 - pad), (0, 0)))
    return sum(up[:, k:k + L] * w[k] for k in range(HY_SHORT)) + b


def hyena_filters(L, w1, b1, w2, b2, w3, freq):
    t = jnp.linspace(0.0, 1.0, L, dtype=jnp.float32)[:, None]
    w = 2.0 * math.pi * jnp.arange(L, dtype=jnp.float32)[:, None] / L
    f = jnp.linspace(1e-4, HY_BANDS - 1, HY_BANDS, dtype=jnp.float32)[None, :]
    emb = jnp.concatenate([t, jnp.cos(f * w), -jnp.sin(f * w)], axis=-1)
    fr = freq.astype(jnp.float32)
    h = jnp.sin(fr * (emb @ w1.astype(jnp.float32) + b1.astype(jnp.float32)))
    h = jnp.sin(fr * (h @ w2.astype(jnp.float32) + b2.astype(jnp.float32)))
    k = (h @ w3.astype(jnp.float32)).reshape(L, 2 * HY_ORDER, HY_WIDTH)
    deltas = jnp.abs(jnp.linspace(math.log(HY_DECAY_TARGET) / HY_SLOW_DECAY,
                                  math.log(HY_DECAY_TARGET) / HY_FAST_DECAY, HY_WIDTH, dtype=jnp.float32))
    k = k * jnp.exp(-t * deltas)[:, None, :]
    fwd, bwd = k[:, 0::2], k[:, 1::2]
    full = jnp.concatenate([fwd, jnp.zeros_like(fwd[:1]), bwd[:0:-1]], axis=0)
    full = full / (jnp.sum(jnp.abs(full), axis=0, keepdims=True) + 1e-6)
    return jnp.fft.rfft(full, axis=0)


def fft_long_conv(u, kf, skip):
    L = u.shape[1]
    uf = jnp.fft.rfft(u.astype(jnp.float32), n=2 * L, axis=1)
    y = jnp.fft.irfft(uf * kf[None], n=2 * L, axis=1)[:, :L]
    return (y + u.astype(jnp.float32) * skip.astype(jnp.float32)).astype(u.dtype)


def hyena(u, kf, conv_w, conv_b, skip):
    u = short_conv(u, conv_w, conv_b)
    v, x1, x2 = jnp.split(u, 3, axis=-1)
    z = x1 * fft_long_conv(v, kf[:, 0], skip[0])
    return x2 * fft_long_conv(z, kf[:, 1], skip[1])


def merge_branches(attn_out, hy_out, gate, w_ba, w_bh, w_out):
    ga, gh = jnp.split(gate, 2, axis=-1)
    y = jax.nn.sigmoid(ga) * (attn_out @ w_ba) + jax.nn.sigmoid(gh) * (hy_out @ w_bh)
    return y @ w_out


def route(t, w_router, router_bias):
    T = t.shape[0]
    scores = jax.nn.sigmoid(t.astype(jnp.float32) @ w_router.astype(jnp.float32))
    sel = scores + router_bias.astype(jnp.float32)
    group_score = lax.top_k(sel.reshape(T, N_GROUPS, N_EXPERTS // N_GROUPS), 2)[0].sum(-1)
    _, gidx = lax.top_k(group_score, TOPK_GROUPS)
    gmask = jax.nn.one_hot(gidx, N_GROUPS, dtype=jnp.float32).sum(1)
    emask = jnp.repeat(gmask, N_EXPERTS // N_GROUPS, axis=1) > 0
    _, eidx = lax.top_k(jnp.where(emask, sel, -jnp.inf), TOP_K)
    w = jnp.take_along_axis(scores, eidx, axis=1)
    w = w / jnp.sum(w, axis=-1, keepdims=True) * ROUTE_SCALE
    return eidx, w


def routed_experts(t, eidx, w, wg, wu, wd):
    T, D = t.shape
    A = T * TOP_K
    flat_e = eidx.reshape(-1).astype(jnp.int32)
    flat_t = jnp.arange(A, dtype=jnp.int32) // TOP_K
    flat_w = w.reshape(-1)
    order = jnp.argsort(flat_e)
    se = flat_e[order]
    counts = jnp.bincount(flat_e, length=N_EXPERTS).astype(jnp.int32)
    padded = (counts + EXPERT_BLOCK - 1) // EXPERT_BLOCK * EXPERT_BLOCK
    pad_end = jnp.cumsum(padded)
    pad_start = pad_end - padded
    grp_start = jnp.cumsum(counts) - counts
    dest = pad_start[se] + (jnp.arange(A, dtype=jnp.int32) - grp_start[se])
    nblk = -(-A // EXPERT_BLOCK) + N_EXPERTS
    P = nblk * EXPERT_BLOCK
    buf_t = jnp.full((P,), T, jnp.int32).at[dest].set(flat_t[order])
    buf_w = jnp.zeros((P,), jnp.float32).at[dest].set(flat_w[order])
    blk_e = jnp.minimum(jnp.searchsorted(pad_end, jnp.arange(nblk, dtype=jnp.int32) * EXPERT_BLOCK,
                                         side='right'), N_EXPERTS - 1)
    t_pad = jnp.concatenate([t, jnp.zeros((1, D), t.dtype)], axis=0)

    def step(acc, blk):
        tok, wt, e = blk
        xb = t_pad[tok]
        a = jax.nn.silu(xb @ wg[e]) * (xb @ wu[e])
        out = (a @ wd[e]).astype(jnp.float32) * wt[:, None]
        return acc.at[tok].add(out), None

    acc, _ = lax.scan(step, jnp.zeros((T + 1, D), jnp.float32),
                      (buf_t.reshape(nblk, EXPERT_BLOCK), buf_w.reshape(nblk, EXPERT_BLOCK), blk_e))
    return acc[:T].astype(t.dtype)


def moe(h, w_router, router_bias, wg, wu, wd, wsg, wsu, wsd):
    B, L, D = h.shape
    t = h.reshape(B * L, D)
    eidx, w = route(t, w_router, router_bias)
    routed = routed_experts(t, eidx, w, wg, wu, wd)
    shared = (jax.nn.silu(t @ wsg) * (t @ wsu)) @ wsd
    return (routed + shared).reshape(B, L, D)


def setup_inputs(seed: int = 0) -> dict:
    key = jax.random.key(seed)
    ks = iter(jax.random.split(key, 40))

    def nrm(shape, scale):
        return jax.random.normal(next(ks), shape, jnp.float32) * scale

    def gain(shape):
        return 1.0 + 0.05 * jax.random.normal(next(ks), shape, jnp.float32)

    L_ = DEPTH
    return {
        "x": nrm((BATCH, SEQ, D_MODEL), 1.0),
        "c": nrm((BATCH, D_MODEL), 1.0),
        "ctx": nrm((BATCH, CTX_LEN, D_MODEL), 1.0),
        "c_ctx": nrm((D_MODEL,), 1.0),
        "w_mod": nrm((L_, D_MODEL, 6 * D_MODEL), 0.5 * D_MODEL ** -0.5),
        "b_mod": nrm((L_, 6 * D_MODEL), 0.02),
        "norm_mix": gain((L_, D_MODEL)),
        "norm_ffn": gain((L_, D_MODEL)),
        "w_in": nrm((L_, D_MODEL, IN_WIDTH), D_MODEL ** -0.5),
        "b_in": nrm((L_, IN_WIDTH), 0.02),
        "q_norm": gain((L_, Q_LORA)),
        "w_uq": nrm((L_, Q_LORA, N_HEADS * (QK_NOPE + QK_ROPE)), Q_LORA ** -0.5),
        "kv_norm": gain((L_, KV_LORA)),
        "w_ukv": nrm((L_, KV_LORA, N_HEADS * (QK_NOPE + V_HEAD)), KV_LORA ** -0.5),
        "w_branch_attn": nrm((L_, N_HEADS * V_HEAD, D_MODEL), (N_HEADS * V_HEAD) ** -0.5),
        "hy_conv_w": nrm((L_, HY_SHORT, 3 * HY_WIDTH), HY_SHORT ** -0.5),
        "hy_conv_b": nrm((L_, 3 * HY_WIDTH), 0.02),
        "hy_filt_w1": nrm((L_, HY_EMB, HY_FILTER_HIDDEN), HY_EMB ** -0.5),
        "hy_filt_b1": nrm((L_, HY_FILTER_HIDDEN), 0.1),
        "hy_filt_w2": nrm((L_, HY_FILTER_HIDDEN, HY_FILTER_HIDDEN), HY_FILTER_HIDDEN ** -0.5),
        "hy_filt_b2": nrm((L_, HY_FILTER_HIDDEN), 0.1),
        "hy_filt_w3": nrm((L_, HY_FILTER_HIDDEN, 2 * HY_ORDER * HY_WIDTH), HY_FILTER_HIDDEN ** -0.5),
        "hy_filt_freq": gain((L_, HY_FILTER_HIDDEN)),
        "hy_skip": nrm((L_, HY_ORDER, HY_WIDTH), 0.5),
        "w_branch_hyena": nrm((L_, HY_WIDTH, D_MODEL), HY_WIDTH ** -0.5),
        "w_out": nrm((L_, D_MODEL, D_MODEL), D_MODEL ** -0.5),
        "w_router": nrm((L_, D_MODEL, N_EXPERTS), D_MODEL ** -0.5),
        "router_bias": nrm((L_, N_EXPERTS), 0.01),
        "w_exp_gate": nrm((L_, N_EXPERTS, D_MODEL, EXPERT_FF), D_MODEL ** -0.5),
        "w_exp_up": nrm((L_, N_EXPERTS, D_MODEL, EXPERT_FF), D_MODEL ** -0.5),
        "w_exp_down": nrm((L_, N_EXPERTS, EXPERT_FF, D_MODEL), EXPERT_FF ** -0.5),
        "w_sh_gate": nrm((L_, D_MODEL, SHARED_FF), D_MODEL ** -0.5),
        "w_sh_up": nrm((L_, D_MODEL, SHARED_FF), D_MODEL ** -0.5),
        "w_sh_down": nrm((L_, SHARED_FF, D_MODEL), SHARED_FF ** -0.5),
        "final_norm": gain((D_MODEL,)),
    }


def reference(x, c, ctx, c_ctx, w_mod, b_mod, norm_mix, norm_ffn, w_in, b_in,
              q_norm, w_uq, kv_norm, w_ukv, w_branch_attn,
              hy_conv_w, hy_conv_b, hy_filt_w1, hy_filt_b1, hy_filt_w2, hy_filt_b2,
              hy_filt_w3, hy_filt_freq, hy_skip, w_branch_hyena, w_out,
              w_router, router_bias, w_exp_gate, w_exp_up, w_exp_down,
              w_sh_gate, w_sh_up, w_sh_down, final_norm):
    B, S, D = x.shape
    n_ctx = ctx.shape[1]
    ang_r, ang_c = axial_angles(S)
    cx = ctx
    for i in range(DEPTH):
        last = i == DEPTH - 1
        mod = (jax.nn.silu(c) @ w_mod[i] + b_mod[i]).reshape(B, 6, D)[:, :, None, :]
        modc = (jax.nn.silu(c_ctx) @ w_mod[i] + b_mod[i]).reshape(6, D)
        filt = (hy_filt_w1[i], hy_filt_b1[i], hy_filt_w2[i], hy_filt_b2[i], hy_filt_w3[i], hy_filt_freq[i])

        h = modulate(rmsnorm(x, norm_mix[i]), mod[:, 0], mod[:, 1])
        hc = modulate(rmsnorm(cx, norm_mix[i]), modc[0], modc[1])
        q_lat, kv_lat, kpe, hy_in, gate = split_proj(h @ w_in[i] + b_in[i])
        cq_lat, ckv_lat, ckpe, chy_in, cgate = split_proj(hc @ w_in[i] + b_in[i])

        qn, qr = mla_q(q_lat, q_norm[i], w_uq[i])
        qr = rope2d(qr, ang_r[:, None], ang_c[:, None])
        kn, v = mla_kv(kv_lat, kv_norm[i], w_ukv[i])
        kpe = rope2d(kpe, ang_r, ang_c)
        ckn, cv = mla_kv(ckv_lat, kv_norm[i], w_ukv[i])
        kn_all = jnp.concatenate([ckn, kn], axis=1)
        kr_all = jnp.concatenate([ckpe, kpe], axis=1)
        v_all = jnp.concatenate([cv, v], axis=1)
        attn = blocked_attention(qn, qr, kn_all, kr_all, v_all)

        hy_out = hyena(hy_in, hyena_filters(S, *filt), hy_conv_w[i], hy_conv_b[i], hy_skip[i])
        x_mixed = x + mod[:, 2] * merge_branches(attn, hy_out, gate, w_branch_attn[i],
                                                 w_branch_hyena[i], w_out[i])

        if not last:
            cqn, cqr = mla_q(cq_lat, q_norm[i], w_uq[i])
            c_attn = attend(cqn, cqr, ckn, ckpe, cv).reshape(B, n_ctx, N_HEADS * V_HEAD)
            c_hy = hyena(chy_in, hyena_filters(n_ctx, *filt), hy_conv_w[i], hy_conv_b[i], hy_skip[i])
            cx = cx + modc[2] * merge_branches(c_attn, c_hy, cgate, w_branch_attn[i],
                                               w_branch_hyena[i], w_out[i])
            hc2 = modulate(rmsnorm(cx, norm_ffn[i]), modc[3], modc[4])
            cx = cx + modc[5] * moe(hc2, w_router[i], router_bias[i], w_exp_gate[i], w_exp_up[i],
                                    w_exp_down[i], w_sh_gate[i], w_sh_up[i], w_sh_down[i])

        x = x_mixed
        h2 = modulate(rmsnorm(x, norm_ffn[i]), mod[:, 3], mod[:, 4])
        x = x + mod[:, 5] * moe(h2, w_router[i], router_bias[i], w_exp_gate[i], w_exp_up[i],
                                w_exp_down[i], w_sh_gate[i], w_sh_up[i], w_sh_down[i])
    return rmsnorm(x, final_norm)
```

```python
import functools
import math

import numpy as np
import jax
import jax.numpy as jnp
from jax import lax
from jax.experimental import pallas as pl
from jax.experimental.pallas import tpu as pltpu

GRID_W = 64
N_HEADS = 8
QK_NOPE = 64
QK_ROPE = 32
V_HEAD = 64
Q_LORA = 256
KV_LORA = 128
ROPE_THETA = 10000.0
ATTN_SCALE = 1.0 / math.sqrt(QK_NOPE + QK_ROPE)
HY_WIDTH = 512
HY_ORDER = 2
HY_SHORT = 3
HY_BANDS = 8
HY_EMB = 1 + 2 * HY_BANDS
HY_EMB_PAD = 32
HY_FAST_DECAY = 0.3
HY_SLOW_DECAY = 1.5
HY_DECAY_TARGET = 1e-2
N_EXPERTS = 64
N_GROUPS = 8
GROUP_SIZE = N_EXPERTS // N_GROUPS
TOPK_GROUPS = 4
TOP_K = 8
EXPERT_FF = 256
ROUTE_SCALE = 2.5
EXPERT_BLOCK = 512
NORM_EPS = 1e-6

HEAD_PAD = 128
LANES = 128
SUBLANES = 8
VMEM_LIMIT = 48 * 1024 * 1024

F32 = jnp.float32
BF16 = jnp.bfloat16
NT_DIMS = (((1,), (1,)), ((), ()))
NN_DIMS = (((1,), (0,)), ((), ()))


def _params(*sem):
    return pltpu.CompilerParams(dimension_semantics=sem, vmem_limit_bytes=VMEM_LIMIT)


def _dot(a, b):
    return jnp.dot(a.astype(BF16), b.astype(BF16), preferred_element_type=F32)


def _split(a):
    hi = a.astype(BF16)
    lo = (a - hi.astype(F32)).astype(BF16)
    return hi, lo


def _dot3(a, b, dims=NN_DIMS):
    ah, al = _split(a)
    bh, bl = _split(b)
    d = functools.partial(lax.dot_general, dimension_numbers=dims, preferred_element_type=F32)
    return d(ah, bh) + (d(ah, bl) + d(al, bh))


def _rms(x, g):
    return x * lax.rsqrt(jnp.mean(x * x, axis=-1, keepdims=True) + NORM_EPS) * g


def _silu(x):
    return x * jax.nn.sigmoid(x)


def _full(shape):
    nd = len(shape)
    return pl.BlockSpec(shape, lambda *_: (0,) * nd)


def _mod_kernel(c_ref, w_ref, b_ref, o_ref):
    o_ref[...] = _dot3(_silu(c_ref[...]), w_ref[...]) + b_ref[...]


def _mod_call(c_rows, w_mod, b_mod):
    r, d = c_rows.shape
    n = w_mod.shape[1]
    bn = 1024
    return pl.pallas_call(
        _mod_kernel,
        grid=(n // bn,),
        in_specs=[_full((r, d)), pl.BlockSpec((d, bn), lambda j: (0, j)), pl.BlockSpec((1, bn), lambda j: (0, j))],
        out_specs=pl.BlockSpec((r, bn), lambda j: (0, j)),
        out_shape=jax.ShapeDtypeStruct((r, n), F32),
        compiler_params=_params("arbitrary"),
        name="mod",
    )(c_rows, w_mod, b_mod.reshape(1, n))


def _prenorm(x, mod_ref, row, g):
    shift = mod_ref[0, row:row + 1, :]
    scale = mod_ref[0, row + 1:row + 2, :]
    return _rms(x, g) * (1.0 + scale) + shift


def _kv_heads(kv_lat, kpe, kvn_ref, wuk_ref, wuv_ref, k_out, v_out):
    kvn = _rms(kv_lat, kvn_ref[...]).astype(BF16)
    kk = _dot(kvn, wuk_ref[...])
    for h in range(N_HEADS):
        k_out[0, h] = (kk[:, HEAD_PAD * h:HEAD_PAD * (h + 1)] + kpe).astype(BF16)
    v_out[0] = _dot(kvn, wuv_ref[...]).astype(BF16)


def _ctx_kernel(c_ref, mod_ref, nm_ref, w_ref, b_ref, kvn_ref, wuk_ref, wuv_ref, k_out, v_out):
    h = _prenorm(c_ref[0], mod_ref, 0, nm_ref[...]).astype(BF16)
    a = _dot(h, w_ref[...]) + b_ref[...]
    _kv_heads(a[:, :KV_LORA], a[:, KV_LORA:], kvn_ref, wuk_ref, wuv_ref, k_out, v_out)


def _ctx_call(ctx, modc, norm_mix, w_c, b_c, kv_norm, w_uk, w_uv):
    bsz, n, d = ctx.shape
    return pl.pallas_call(
        _ctx_kernel,
        grid=(bsz,),
        in_specs=[pl.BlockSpec((1, n, d), lambda b: (b, 0, 0)), _full(modc.shape), _full(norm_mix.shape),
                  _full(w_c.shape), _full(b_c.shape), _full(kv_norm.shape), _full(w_uk.shape), _full(w_uv.shape)],
        out_specs=[pl.BlockSpec((1, N_HEADS, n, HEAD_PAD), lambda b: (b, 0, 0, 0)),
                   pl.BlockSpec((1, n, N_HEADS * V_HEAD), lambda b: (b, 0, 0))],
        out_shape=[jax.ShapeDtypeStruct((bsz, N_HEADS, n, HEAD_PAD), BF16),
                   jax.ShapeDtypeStruct((bsz, n, N_HEADS * V_HEAD), BF16)],
        compiler_params=_params("arbitrary"),
        name="ctx",
    )(ctx, modc, norm_mix, w_c, b_c, kv_norm, w_uk, w_uv)


def _inproj_kernel(x_ref, xp_ref, xn_ref, mod_ref, nm_ref, wa_ref, ba_ref, why_ref, bhy_ref, wg_ref, bg_ref,
                   qn_ref, wuq_ref, wuqs_ref, kvn_ref, wuk_ref, wuv_ref, cos_ref, sin_ref, cw_ref, cb_ref,
                   q_out, k_out, v_out, hv_out, hx1_out, hx2_out, g_out):
    i = pl.program_id(0)
    tm = x_ref.shape[1]
    nm = nm_ref[...]
    h = _prenorm(x_ref[0], mod_ref, 0, nm).astype(BF16)
    a = _dot(h, wa_ref[...]) + ba_ref[...]
    q_lat = a[:, :Q_LORA]
    kv_lat = a[:, Q_LORA:Q_LORA + KV_LORA]
    kpe_m = a[:, Q_LORA + KV_LORA:Q_LORA + KV_LORA + HEAD_PAD]
    kpe_s = a[:, Q_LORA + KV_LORA + HEAD_PAD:]
    cos = cos_ref[...]
    sin = sin_ref[...]
    qn = _rms(q_lat, qn_ref[...]).astype(BF16)
    qa = _dot(qn, wuq_ref[...])
    qs = _dot(qn, wuqs_ref[...])
    for hh in range(N_HEADS):
        sl = slice(HEAD_PAD * hh, HEAD_PAD * (hh + 1))
        q_out[0, hh] = (qa[:, sl] * cos + qs[:, sl] * sin).astype(BF16)
    _kv_heads(kv_lat, kpe_m * cos + kpe_s * sin, kvn_ref, wuk_ref, wuv_ref, k_out, v_out)
    g_out[0] = (_dot(h, wg_ref[...]) + bg_ref[...]).astype(BF16)

    why = why_ref[...]
    bhy = bhy_ref[...]
    hy = _dot(h, why) + bhy
    hp = _dot(_prenorm(xp_ref[0], mod_ref, 0, nm).astype(BF16), why) + bhy
    hn = _dot(_prenorm(xn_ref[0], mod_ref, 0, nm).astype(BF16), why) + bhy
    prev = jnp.where(i == 0, 0.0, hp[SUBLANES - 1:SUBLANES])
    nxt = jnp.where(i == pl.num_programs(0) - 1, 0.0, hn[0:1])
    rid = lax.broadcasted_iota(jnp.int32, (tm, 1), 0)
    up = jnp.where(rid == 0, prev, pltpu.roll(hy, 1, 0))
    dn = jnp.where(rid == tm - 1, nxt, pltpu.roll(hy, tm - 1, 0))
    u = up * cw_ref[0:1, :] + hy * cw_ref[1:2, :] + dn * cw_ref[2:3, :] + cb_ref[...]
    hv_out[0] = u[:, :HY_WIDTH]
    hx1_out[0] = u[:, HY_WIDTH:2 * HY_WIDTH]
    hx2_out[0] = u[:, 2 * HY_WIDTH:]


def _inproj_call(x, mod, norm_mix, wa, ba, why, bhy, wg, bg, q_norm, wuq, wuqs, kv_norm, wuk, wuv, cos_t, sin_t, cw,
                 cb, tm):
    bsz, s, d = x.shape
    nt = s // tm
    rb = tm // SUBLANES
    last_rb = s // SUBLANES - 1
    consts = [norm_mix, wa, ba, why, bhy, wg, bg, q_norm, wuq, wuqs, kv_norm, wuk, wuv]
    in_specs = [
        pl.BlockSpec((1, tm, d), lambda i, b: (b, i, 0)),
        pl.BlockSpec((1, SUBLANES, d), lambda i, b: (b, jnp.maximum(i * rb - 1, 0), 0)),
        pl.BlockSpec((1, SUBLANES, d), lambda i, b: (b, jnp.minimum((i + 1) * rb, last_rb), 0)),
        pl.BlockSpec((1, SUBLANES, d), lambda i, b: (b, 0, 0)),
    ] + [_full(c.shape) for c in consts] + [
        pl.BlockSpec((tm, HEAD_PAD), lambda i, b: (i, 0)),
        pl.BlockSpec((tm, HEAD_PAD), lambda i, b: (i, 0)),
        _full(cw.shape), _full(cb.shape),
    ]
    hw = HY_WIDTH
    out_specs = [
        pl.BlockSpec((1, N_HEADS, tm, HEAD_PAD), lambda i, b: (b, 0, i, 0)),
        pl.BlockSpec((1, N_HEADS, tm, HEAD_PAD), lambda i, b: (b, 0, i, 0)),
        pl.BlockSpec((1, tm, N_HEADS * V_HEAD), lambda i, b: (b, i, 0)),
        pl.BlockSpec((1, tm, hw), lambda i, b: (b, i, 0)),
        pl.BlockSpec((1, tm, hw), lambda i, b: (b, i, 0)),
        pl.BlockSpec((1, tm, hw), lambda i, b: (b, i, 0)),
        pl.BlockSpec((1, tm, 2 * d), lambda i, b: (b, i, 0)),
    ]
    out_shape = [
        jax.ShapeDtypeStruct((bsz, N_HEADS, s, HEAD_PAD), BF16),
        jax.ShapeDtypeStruct((bsz, N_HEADS, s, HEAD_PAD), BF16),
        jax.ShapeDtypeStruct((bsz, s, N_HEADS * V_HEAD), BF16),
        jax.ShapeDtypeStruct((bsz, s, hw), F32),
        jax.ShapeDtypeStruct((bsz, s, hw), F32),
        jax.ShapeDtypeStruct((bsz, s, hw), F32),
        jax.ShapeDtypeStruct((bsz, s, 2 * d), BF16),
    ]
    return pl.pallas_call(
        _inproj_kernel,
        grid=(nt, bsz),
        in_specs=in_specs,
        out_specs=out_specs,
        out_shape=out_shape,
        compiler_params=_params("arbitrary", "arbitrary"),
        name="inproj",
    )(x, x, x, mod, *consts, cos_t, sin_t, cw, cb)


def _attn_kernel(q_ref, k_ref, v_ref, ck_ref, cv_ref, o_ref, m_sc, l_sc, acc_sc):
    j = pl.program_id(2)

    def update(k_of_head, v_of_pair):
        for h in range(N_HEADS):
            s = lax.dot_general(q_ref[0, h], k_of_head(h), NT_DIMS, preferred_element_type=F32)
            m_prev = m_sc[h]
            m_new = jnp.maximum(m_prev, jnp.max(s, axis=-1, keepdims=True))
            alpha = jnp.exp(m_prev - m_new)
            p = jnp.exp(s - m_new)
            l_sc[h] = alpha * l_sc[h] + jnp.sum(p, axis=-1, keepdims=True)
            acc_sc[h] = alpha * acc_sc[h] + jnp.dot(p.astype(BF16), v_of_pair(h // 2), preferred_element_type=F32)
            m_sc[h] = m_new

    @pl.when(j == 0)
    def _():
        m_sc[...] = jnp.full(m_sc.shape, -jnp.inf, F32)
        l_sc[...] = jnp.zeros(l_sc.shape, F32)
        acc_sc[...] = jnp.zeros(acc_sc.shape, F32)
        update(lambda h: ck_ref[0, h], lambda p: cv_ref[0, :, LANES * p:LANES * (p + 1)])

    update(lambda h: k_ref[0, h], lambda p: v_ref[0, :, LANES * p:LANES * (p + 1)])

    @pl.when(j == pl.num_programs(2) - 1)
    def _():
        lane = lax.broadcasted_iota(jnp.int32, (1, LANES), 1)
        for p in range(N_HEADS // 2):
            o0 = acc_sc[2 * p] / l_sc[2 * p]
            o1 = acc_sc[2 * p + 1] / l_sc[2 * p + 1]
            o_ref[0, :, LANES * p:LANES * (p + 1)] = jnp.where(lane < V_HEAD, o0, o1).astype(o_ref.dtype)


def _attn_call(q, k, v, ck, cv, tq, tk):
    bsz, nh, s, dh = q.shape
    n_ctx = ck.shape[2]
    dv = v.shape[2]
    return pl.pallas_call(
        _attn_kernel,
        grid=(bsz, s // tq, s // tk),
        in_specs=[
            pl.BlockSpec((1, nh, tq, dh), lambda b, i, j: (b, 0, i, 0)),
            pl.BlockSpec((1, nh, tk, dh), lambda b, i, j: (b, 0, j, 0)),
            pl.BlockSpec((1, tk, dv), lambda b, i, j: (b, j, 0)),
            pl.BlockSpec((1, nh, n_ctx, dh), lambda b, i, j: (b, 0, 0, 0)),
            pl.BlockSpec((1, n_ctx, dv), lambda b, i, j: (b, 0, 0)),
        ],
        out_specs=pl.BlockSpec((1, tq, dv), lambda b, i, j: (b, i, 0)),
        out_shape=jax.ShapeDtypeStruct((bsz, s, dv), BF16),
        scratch_shapes=[pltpu.VMEM((nh, tq, 1), F32), pltpu.VMEM((nh, tq, 1), F32),
                        pltpu.VMEM((nh, tq, LANES), F32)],
        compiler_params=_params("arbitrary", "arbitrary", "arbitrary"),
        name="attn",
    )(q, k, v, ck, cv)


def _filter_kernel(emb_ref, w1_ref, b1_ref, w2_ref, b2_ref, w3_ref, fr_ref, dl_ref, full_out, asum_out, *, seq):
    r = pl.program_id(0)
    rb = emb_ref.shape[0]
    emb = emb_ref[...]
    fr = fr_ref[...]
    h = jnp.sin(fr * (_dot3(emb, w1_ref[...]) + b1_ref[...]))
    h = jnp.sin(fr * (_dot3(h, w2_ref[...]) + b2_ref[...]))
    k = _dot3(h, w3_ref[0]) * jnp.exp(-emb[:, 0:1] * dl_ref[...])
    row = r * rb + lax.broadcasted_iota(jnp.int32, (rb, 1), 0)
    k = jnp.where(row == seq, 0.0, k)
    full_out[...] = k

    @pl.when(r == 0)
    def _():
        asum_out[...] = jnp.zeros(asum_out.shape, F32)

    asum_out[...] += jnp.sum(jnp.abs(k), axis=0, keepdims=True)


def _filter_call(emb, w1, b1, w2, b2, w3sel, freq, deltas2, seq, rb):
    n2 = emb.shape[0]
    half_blocks = seq // rb
    width = w3sel.shape[2]
    return pl.pallas_call(
        functools.partial(_filter_kernel, seq=seq),
        grid=(n2 // rb,),
        in_specs=[pl.BlockSpec((rb, HY_EMB_PAD), lambda r: (r, 0)), _full(w1.shape), _full(b1.shape),
                  _full(w2.shape), _full(b2.shape),
                  pl.BlockSpec((1,) + w3sel.shape[1:], lambda r: (r // half_blocks, 0, 0)),
                  _full(freq.shape), _full(deltas2.shape)],
        out_specs=[pl.BlockSpec((rb, width), lambda r: (r, 0)), pl.BlockSpec((1, width), lambda r: (0, 0))],
        out_shape=[jax.ShapeDtypeStruct((n2, width), F32), jax.ShapeDtypeStruct((1, width), F32)],
        compiler_params=_params("arbitrary"),
        name="filt",
    )(emb, w1, b1, w2, b2, w3sel, freq, deltas2)


def _fa_kernel(u_ref, f_ref, a_out):
    two, _, hn, nb = u_ref.shape
    x = u_ref[...].reshape(two * hn, nb)
    a_out[0] = _dot(f_ref[...], x).astype(a_out.dtype)


def _fa_call(u4, fmat, nb):
    _, p, hn, width = u4.shape
    n = 2 * hn
    return pl.pallas_call(
        _fa_kernel,
        grid=(p, width // nb),
        in_specs=[pl.BlockSpec((2, 1, hn, nb), lambda q, g: (0, q, 0, g)), _full(fmat.shape)],
        out_specs=pl.BlockSpec((1, 2 * n, nb), lambda q, g: (q, 0, g)),
        out_shape=jax.ShapeDtypeStruct((p, 2 * n, width), BF16),
        compiler_params=_params("arbitrary", "arbitrary"),
        name="fa",
    )(u4, fmat)


def _fb_kernel(a_ref, g_ref, asum_ref, kf_out):
    two, _, n, c = a_ref.shape[1:]
    a = a_ref[0].reshape(two * n, c)
    x = _dot(g_ref[0], a) / (asum_ref[...] + 1e-6)
    kf_out[0] = x.reshape(two, n, c)


def _fb_call(a5, gmat, asum):
    _, _, n, _, c = a5.shape
    return pl.pallas_call(
        _fb_kernel,
        grid=(n,),
        in_specs=[pl.BlockSpec((1, 2, 1, n, c), lambda k: (0, 0, k, 0, 0)),
                  pl.BlockSpec((1, 2 * n, 2 * n), lambda k: (k, 0, 0)), _full(asum.shape)],
        out_specs=pl.BlockSpec((1, 2, n, c), lambda k: (k, 0, 0, 0)),
        out_shape=jax.ShapeDtypeStruct((n, 2, n, c), F32),
        compiler_params=_params("arbitrary"),
        name="fb",
    )(a5, gmat, asum)


def _mid_kernel(a_ref, g_ref, h_ref, kf_ref, b_out):
    two, _, n, c = a_ref.shape[1:]
    a = a_ref[0].reshape(two * n, c)
    x = _dot(g_ref[0], a)
    xr, xi = x[:n], x[n:]
    kr, ki = kf_ref[0, 0], kf_ref[0, 1]
    y = jnp.concatenate([xr * kr - xi * ki, xr * ki + xi * kr], axis=0)
    b_out[0] = _dot(h_ref[0], y).reshape(two, 1, n, c).astype(b_out.dtype)


def _mid_call(a5, gmat, hmat, kf, order):
    p, _, n, _, c = a5.shape
    return pl.pallas_call(
        _mid_kernel,
        grid=(n, p),
        in_specs=[pl.BlockSpec((1, 2, 1, n, c), lambda k, q: (q, 0, k, 0, 0)),
                  pl.BlockSpec((1, 2 * n, 2 * n), lambda k, q: (k, 0, 0)),
                  pl.BlockSpec((1, 2 * n, 2 * n), lambda k, q: (k, 0, 0)),
                  pl.BlockSpec((1, 2, n, c), lambda k, q: (k, 0, 0, order))],
        out_specs=pl.BlockSpec((1, 2, 1, n, c), lambda k, q: (q, 0, k, 0, 0)),
        out_shape=jax.ShapeDtypeStruct(a5.shape, BF16),
        compiler_params=_params("arbitrary", "arbitrary"),
        name="mid",
    )(a5, gmat, hmat, kf)


def _fc_kernel(b_ref, f_ref, u_ref, m_ref, skip_ref, o_out):
    two, _, hn, nb = u_ref.shape
    y = _dot(f_ref[...], b_ref[0]).reshape(two, 1, hn, nb)
    o_out[...] = (m_ref[...] * (y + u_ref[...] * skip_ref[...])).astype(o_out.dtype)


def _fc_call(b3, finv, u4, m4, skip_t, nb, out_dtype):
    _, p, hn, width = u4.shape
    n = 2 * hn
    blk = pl.BlockSpec((2, 1, hn, nb), lambda q, g: (0, q, 0, g))
    return pl.pallas_call(
        _fc_kernel,
        grid=(p, width // nb),
        in_specs=[pl.BlockSpec((1, 2 * n, nb), lambda q, g: (q, 0, g)), _full(finv.shape), blk, blk,
                  pl.BlockSpec((1, nb), lambda q, g: (0, g))],
        out_specs=blk,
        out_shape=jax.ShapeDtypeStruct(u4.shape, out_dtype),
        compiler_params=_params("arbitrary", "arbitrary"),
        name="fc",
    )(b3, finv, u4, m4, skip_t)


def _dft_tables(n):
    hn = n // 2
    k = np.arange(n)[:, None]
    ang = -2.0 * np.pi * (k * np.arange(n)[None, :] % n) / n
    fr, fi = np.cos(ang), np.sin(ang)
    f_data = np.block([[fr[:, :hn], -fi[:, :hn]], [fi[:, :hn], fr[:, :hn]]])
    f_filt = np.concatenate([fr, fi], axis=0)
    er, ei = fr[:hn], -fi[:hn]
    f_inv = np.block([[er, -ei], [ei, er]]) / float(n * n)
    k1 = jnp.arange(n, dtype=jnp.int32)[:, None, None]
    k2 = jnp.arange(n, dtype=jnp.int32)[None, :, None]
    m2 = jnp.arange(n, dtype=jnp.int32)[None, None, :]
    idx = (m2 * (k1 + n * k2)) % (n * n)
    ang2 = idx.astype(F32) * (-2.0 * math.pi / (n * n))
    gr, gi = jnp.cos(ang2), jnp.sin(ang2)
    g = jnp.concatenate([jnp.concatenate([gr, -gi], axis=2), jnp.concatenate([gi, gr], axis=2)], axis=1)
    h = jnp.swapaxes(g, 1, 2)
    return (jnp.asarray(f_data, BF16), jnp.asarray(f_filt, BF16), jnp.asarray(f_inv, BF16),
            g.astype(BF16), h.astype(BF16))


def _hyena_filter_tables(seq):
    t = jnp.linspace(0.0, 1.0, seq, dtype=F32)[:, None]
    w = 2.0 * math.pi * jnp.arange(seq, dtype=F32)[:, None] / seq
    f = jnp.linspace(1e-4, HY_BANDS - 1, HY_BANDS, dtype=F32)[None, :]
    emb = jnp.concatenate([t, jnp.cos(f * w), -jnp.sin(f * w)], axis=-1)
    pos = jnp.concatenate([jnp.arange(seq), jnp.array([0]), jnp.arange(seq - 1, 0, -1)])
    emb = jnp.pad(emb[pos], ((0, 0), (0, HY_EMB_PAD - HY_EMB)))
    deltas = jnp.abs(jnp.linspace(math.log(HY_DECAY_TARGET) / HY_SLOW_DECAY,
                                  math.log(HY_DECAY_TARGET) / HY_FAST_DECAY, HY_WIDTH, dtype=F32))
    return emb, jnp.tile(deltas, HY_ORDER)[None, :]


def _hyena(hv, hx1, hx2, w1, b1, w2, b2, w3, freq, skip, nb):
    bsz, seq, c = hv.shape
    n = int(round(math.sqrt(2 * seq)))
    assert n * n == 2 * seq and bsz % 2 == 0
    hn, p = n // 2, bsz // 2
    f_data, f_filt, f_inv, gmat, hmat = _dft_tables(n)

    emb, deltas2 = _hyena_filter_tables(seq)
    w1p = jnp.pad(w1, ((0, HY_EMB_PAD - HY_EMB), (0, 0)))
    w3r = w3.reshape(w3.shape[0], HY_ORDER, 2, c)
    w3sel = jnp.stack([w3r[:, :, 0, :].reshape(-1, HY_ORDER * c), w3r[:, :, 1, :].reshape(-1, HY_ORDER * c)])
    full, asum = _filter_call(emb, w1p, b1[None], w2, b2[None], w3sel, freq[None], deltas2, seq, min(512, seq))
    c2 = HY_ORDER * c
    fa = _fa_call(full.reshape(2, 1, hn, n * c2), f_filt, nb)
    kf = _fb_call(fa.reshape(1, 2, n, n, c2), gmat, asum)

    def view(t):
        return t.reshape(2, p, hn, n * c)

    def long_conv(u4, m4, order, out_dtype):
        a = _fa_call(u4, f_data, nb)
        bm = _mid_call(a.reshape(p, 2, n, n, c), gmat, hmat, kf, order)
        skip_t = jnp.tile(skip[order], n)[None, :]
        return _fc_call(bm.reshape(p, 2 * n, n * c), f_inv, u4, m4, skip_t, nb, out_dtype)

    z = long_conv(view(hv), view(hx1), 0, F32)
    out = long_conv(z, view(hx2), 1, BF16)
    return out.reshape(bsz, seq, c)


def _merge_kernel(x_ref, at_ref, hy_ref, g_ref, mod_ref, wba_ref, wbh_ref, wo_ref, o_ref):
    d = x_ref.shape[2]
    g = g_ref[0].astype(F32)
    y = (jax.nn.sigmoid(g[:, :d]) * _dot(at_ref[0], wba_ref[...])
         + jax.nn.sigmoid(g[:, d:]) * _dot(hy_ref[0], wbh_ref[...]))
    o_ref[0] = x_ref[0] + mod_ref[0, 2:3, :] * _dot(y, wo_ref[...])


def _merge_call(x, attn, hy, gate, mod, wba, wbh, wo, tm):
    bsz, s, d = x.shape

    def tok(w):
        return pl.BlockSpec((1, tm, w), lambda b, i: (b, i, 0))

    return pl.pallas_call(
        _merge_kernel,
        grid=(bsz, s // tm),
        in_specs=[tok(d), tok(attn.shape[2]), tok(hy.shape[2]), tok(2 * d),
                  pl.BlockSpec((1, SUBLANES, d), lambda b, i: (b, 0, 0)),
                  _full(wba.shape), _full(wbh.shape), _full(wo.shape)],
        out_specs=tok(d),
        out_shape=jax.ShapeDtypeStruct((bsz, s, d), F32),
        compiler_params=_params("arbitrary", "arbitrary"),
        name="merge",
    )(x, attn, hy, gate, mod, wba, wbh, wo)


def _route_kernel(xm_ref, mod_ref, nf_ref, wrt_ref, rb_ref, tri_ref, h2_out, e_out, w_out, p_out, cnt_out, carry_sc):
    step = pl.program_id(0)
    tm = xm_ref.shape[1]
    ng, gs = N_GROUPS, GROUP_SIZE

    @pl.when(step == 0)
    def _():
        carry_sc[...] = jnp.zeros(carry_sc.shape, F32)

    h2 = _prenorm(xm_ref[0], mod_ref, 3, nf_ref[...])
    h2_out[0] = h2
    scores = jax.nn.sigmoid(_dot3(wrt_ref[...], h2, NT_DIMS))
    sel = scores + rb_ref[...]
    slabs = [sel[ng * j:ng * (j + 1)] for j in range(gs)]
    sc_slabs = [scores[ng * j:ng * (j + 1)] for j in range(gs)]

    top1 = jnp.full((ng, tm), -jnp.inf, F32)
    top2 = top1
    for x in slabs:
        top2 = jnp.maximum(top2, jnp.minimum(top1, x))
        top1 = jnp.maximum(top1, x)
    gscore = top1 + top2
    gid = lax.broadcasted_iota(jnp.int32, (ng, 1), 0)
    rank = jnp.zeros((ng, tm), jnp.int32)
    for g2 in range(ng):
        row = gscore[g2:g2 + 1]
        beats = (row > gscore) | ((row == gscore) & (g2 < gid))
        rank = rank + beats.astype(jnp.int32)
    gmask = rank < TOPK_GROUPS

    cand = [jnp.where(gmask, x, -jnp.inf) for x in slabs]
    eid = [gid * gs + j for j in range(gs)]
    chosen = []
    for _ in range(TOP_K):
        best = functools.reduce(jnp.maximum, cand)
        best = jnp.max(best, axis=0, keepdims=True)
        idx = functools.reduce(jnp.minimum, [jnp.where(cand[j] == best, eid[j], N_EXPERTS) for j in range(gs)])
        idx = jnp.min(idx, axis=0, keepdims=True)
        chosen.append(idx)
        cand = [jnp.where(eid[j] == idx, -jnp.inf, cand[j]) for j in range(gs)]

    hit = [[eid[j] == idx for j in range(gs)] for idx in chosen]
    mask = [functools.reduce(jnp.logical_or, [hit[k][j] for k in range(TOP_K)]) for j in range(gs)]
    maskf = jnp.concatenate([m.astype(F32) for m in mask], axis=0)
    before = jnp.dot(maskf.astype(BF16), tri_ref[...], preferred_element_type=F32) + carry_sc[...]
    carry_sc[...] += jnp.sum(maskf, axis=1, keepdims=True)
    cnt_out[...] = jnp.broadcast_to(carry_sc[...], cnt_out.shape)
    bslabs = [before[ng * j:ng * (j + 1)] for j in range(gs)]

    def pick(k, vals):
        tot = functools.reduce(jnp.add, [jnp.where(hit[k][j], vals[j], 0.0) for j in range(gs)])
        return jnp.sum(tot, axis=0, keepdims=True)

    wk = [pick(k, sc_slabs) for k in range(TOP_K)]
    wsum = functools.reduce(jnp.add, wk)
    w_out[...] = jnp.concatenate(wk, axis=0) / wsum * ROUTE_SCALE
    p_out[...] = jnp.concatenate([pick(k, bslabs) for k in range(TOP_K)], axis=0).astype(jnp.int32)
    e_out[...] = jnp.concatenate(chosen, axis=0)


def _route_call(xm, mod, norm_ffn, wrt, rbias, tm):
    bsz, s, d = xm.shape
    t = bsz * s
    nt = s // tm
    tri = (jnp.arange(tm)[:, None] < jnp.arange(tm)[None, :]).astype(BF16)
    tok = pl.BlockSpec((TOP_K, tm), lambda i: (0, i))
    return pl.pallas_call(
        _route_kernel,
        grid=(t // tm,),
        in_specs=[pl.BlockSpec((1, tm, d), lambda i: (i // nt, i % nt, 0)),
                  pl.BlockSpec((1, SUBLANES, d), lambda i: (i // nt, 0, 0)),
                  _full(norm_ffn.shape), _full(wrt.shape), _full(rbias.shape), _full(tri.shape)],
        out_specs=[pl.BlockSpec((1, tm, d), lambda i: (i // nt, i % nt, 0)), tok, tok, tok,
                   _full((N_EXPERTS, LANES))],
        out_shape=[jax.ShapeDtypeStruct((bsz, s, d), F32), jax.ShapeDtypeStruct((TOP_K, t), jnp.int32),
                   jax.ShapeDtypeStruct((TOP_K, t), F32), jax.ShapeDtypeStruct((TOP_K, t), jnp.int32),
                   jax.ShapeDtypeStruct((N_EXPERTS, LANES), F32)],
        scratch_shapes=[pltpu.VMEM((N_EXPERTS, 1), F32)],
        compiler_params=_params("arbitrary"),
        name="route",
    )(xm, mod, norm_ffn, wrt, rbias, tri)


def _zero_kernel(o_ref):
    o_ref[...] = jnp.zeros(o_ref.shape, o_ref.dtype)


def _zero_call(rows, d):
    return pl.pallas_call(
        _zero_kernel,
        grid=(rows // EXPERT_BLOCK,),
        out_specs=pl.BlockSpec((EXPERT_BLOCK, d), lambda r: (r, 0)),
        out_shape=jax.ShapeDtypeStruct((rows, d), F32),
        compiler_params=_params("arbitrary"),
        name="zero",
    )()


def _dispatch_kernel(dest_hbm, h_ref, xs_in, xs_out, dest_sm, sem, isem):
    del xs_in
    step = pl.program_id(0)
    tm = h_ref.shape[0]
    n = TOP_K * tm
    idx_copy = pltpu.make_async_copy(dest_hbm.at[pl.ds(pl.multiple_of(step * n, n), n)], dest_sm, isem)
    idx_copy.start()
    idx_copy.wait()

    def row_copy(k, i):
        return pltpu.make_async_copy(h_ref.at[pl.ds(i, 1), :], xs_out.at[pl.ds(dest_sm[k * tm + i], 1), :], sem)

    def issue(i, c):
        for k in range(TOP_K):
            row_copy(k, i).start()
        return c

    def drain(i, c):
        for k in range(TOP_K):
            row_copy(k, i).wait()
        return c

    lax.fori_loop(0, tm, issue, 0)
    lax.fori_loop(0, tm, drain, 0)


def _dispatch_call(dest_flat, h2, xs, tm):
    t, d = h2.shape
    return pl.pallas_call(
        _dispatch_kernel,
        grid=(t // tm,),
        in_specs=[pl.BlockSpec(memory_space=pl.ANY), pl.BlockSpec((tm, d), lambda i: (i, 0)),
                  pl.BlockSpec(memory_space=pl.ANY)],
        out_specs=pl.BlockSpec(memory_space=pl.ANY),
        out_shape=jax.ShapeDtypeStruct(xs.shape, xs.dtype),
        scratch_shapes=[pltpu.SMEM((TOP_K * tm,), jnp.int32), pltpu.SemaphoreType.DMA(()),
                        pltpu.SemaphoreType.DMA(())],
        input_output_aliases={2: 0},
        compiler_params=_params("arbitrary"),
        name="dispatch",
    )(dest_flat, h2, xs)


def _expert_kernel(blk_ref, nused_ref, x_ref, wgu_ref, wd_ref, y_ref):
    used = pl.program_id(0) < nused_ref[0]

    @pl.when(used)
    def _():
        gu = _dot(x_ref[...], wgu_ref[0])
        a = _silu(gu[:, :EXPERT_FF]) * gu[:, EXPERT_FF:]
        y_ref[...] = _dot(a, wd_ref[0])

    @pl.when(jnp.logical_not(used))
    def _():
        y_ref[...] = jnp.zeros(y_ref.shape, y_ref.dtype)


def _expert_call(blk_e, nused, xs, wgu, wd):
    rows, d = xs.shape
    nblk = rows // EXPERT_BLOCK

    def row_map(i, blk, nu):
        return (jnp.minimum(i, nu[0] - 1), 0)

    return pl.pallas_call(
        _expert_kernel,
        grid_spec=pltpu.PrefetchScalarGridSpec(
            num_scalar_prefetch=2, grid=(nblk,),
            in_specs=[pl.BlockSpec((EXPERT_BLOCK, d), row_map),
                      pl.BlockSpec((1,) + wgu.shape[1:], lambda i, blk, nu: (blk[i], 0, 0)),
                      pl.BlockSpec((1,) + wd.shape[1:], lambda i, blk, nu: (blk[i], 0, 0))],
            out_specs=pl.BlockSpec((EXPERT_BLOCK, d), lambda i, blk, nu: (i, 0))),
        out_shape=jax.ShapeDtypeStruct((rows, d), F32),
        compiler_params=_params("arbitrary"),
        name="expert",
    )(blk_e, nused, xs, wgu, wd)


def _combine_kernel(dest_hbm, ys_hbm, w_ref, xm_ref, h_ref, mod_ref, wsgu_ref, wsd_ref, fn_ref, o_ref,
                    dest_sm, gbuf, sem, isem):
    step = pl.program_id(0)
    tm = xm_ref.shape[0]
    n = TOP_K * tm
    idx_copy = pltpu.make_async_copy(dest_hbm.at[pl.ds(pl.multiple_of(step * n, n), n)], dest_sm, isem)
    idx_copy.start()
    idx_copy.wait()

    def row_copy(k, i):
        return pltpu.make_async_copy(ys_hbm.at[pl.ds(dest_sm[k * tm + i], 1), :], gbuf.at[k, pl.ds(i, 1), :], sem)

    def issue(i, c):
        for k in range(TOP_K):
            row_copy(k, i).start()
        return c

    def drain(i, c):
        for k in range(TOP_K):
            row_copy(k, i).wait()
        return c

    lax.fori_loop(0, tm, issue, 0)
    h = h_ref[...]
    gu = _dot(h, wsgu_ref[...])
    ff = gu.shape[1] // 2
    shared = _dot(_silu(gu[:, :ff]) * gu[:, ff:], wsd_ref[...])
    lax.fori_loop(0, tm, drain, 0)
    w = w_ref[...]
    routed = functools.reduce(jnp.add, [w[:, k:k + 1] * gbuf[k] for k in range(TOP_K)])
    x = xm_ref[...] + mod_ref[0, 5:6, :] * (routed + shared)
    o_ref[...] = _rms(x, fn_ref[...])


def _combine_call(dest_flat, ys, w_tk, xm, h2, mod, wsgu, wsd, final_norm, tm, tiles_per_batch):
    t, d = xm.shape
    tok = pl.BlockSpec((tm, d), lambda i: (i, 0))
    return pl.pallas_call(
        _combine_kernel,
        grid=(t // tm,),
        in_specs=[pl.BlockSpec(memory_space=pl.ANY), pl.BlockSpec(memory_space=pl.ANY),
                  pl.BlockSpec((tm, TOP_K), lambda i: (i, 0)), tok, tok,
                  pl.BlockSpec((1, SUBLANES, d), lambda i: (i // tiles_per_batch, 0, 0)),
                  _full(wsgu.shape), _full(wsd.shape), _full(final_norm.shape)],
        out_specs=tok,
        out_shape=jax.ShapeDtypeStruct((t, d), F32),
        scratch_shapes=[pltpu.SMEM((TOP_K * tm,), jnp.int32), pltpu.VMEM((TOP_K, tm, d), F32),
                        pltpu.SemaphoreType.DMA(()), pltpu.SemaphoreType.DMA(())],
        compiler_params=_params("arbitrary"),
        name="combine",
    )(dest_flat, ys, w_tk, xm, h2, mod, wsgu, wsd, final_norm)


def _tile_major(a_kt, tm):
    k, t = a_kt.shape
    return a_kt.reshape(k, t // tm, tm).transpose(1, 0, 2).reshape(-1)


def _moe(xm, mod, norm_ffn, w_router, router_bias, wg, wu, wd, wsg, wsu, wsd, final_norm, tm_route, tm_disp,
         tm_comb):
    bsz, s, d = xm.shape
    t = bsz * s
    perm = (np.arange(N_EXPERTS) % N_GROUPS) * GROUP_SIZE + np.arange(N_EXPERTS) // N_GROUPS
    wrt = w_router.T[perm]
    rbias = router_bias[perm][:, None]
    h2, eidx, w_kt, pos, cnt = _route_call(xm, mod, norm_ffn, wrt, rbias, tm_route)

    inv = np.argsort(perm)
    counts = cnt[:, 0].astype(jnp.int32)[inv]
    padded = (counts + EXPERT_BLOCK - 1) // EXPERT_BLOCK * EXPERT_BLOCK
    pad_end = jnp.cumsum(padded)
    pad_start = pad_end - padded
    dest = pad_start[eidx] + pos
    nblk = -(-(t * TOP_K) // EXPERT_BLOCK) + N_EXPERTS
    rows = nblk * EXPERT_BLOCK
    blk_e = jnp.minimum(jnp.searchsorted(pad_end, jnp.arange(nblk, dtype=jnp.int32) * EXPERT_BLOCK, side='right'),
                        N_EXPERTS - 1).astype(jnp.int32)
    nused = (pad_end[-1:] // EXPERT_BLOCK).astype(jnp.int32)

    h2f = h2.reshape(t, d)
    xs = _zero_call(rows, d)
    xs = _dispatch_call(_tile_major(dest, tm_disp), h2f, xs, tm_disp)
    wgu = jnp.concatenate([wg, wu], axis=2).astype(BF16)
    ys = _expert_call(blk_e, nused, xs, wgu, wd.astype(BF16))
    wsgu = jnp.concatenate([wsg, wsu], axis=1).astype(BF16)
    out = _combine_call(_tile_major(dest, tm_comb), ys, w_kt.T, xm.reshape(t, d), h2f, mod, wsgu, wsd.astype(BF16),
                        final_norm, tm_comb, s // tm_comb)
    return out.reshape(bsz, s, d)


def _rope_tables(s):
    rows = s // GRID_W
    row = jnp.broadcast_to(jnp.arange(rows, dtype=F32)[:, None], (rows, GRID_W)).reshape(-1)
    col = jnp.broadcast_to(jnp.arange(GRID_W, dtype=F32)[None, :], (rows, GRID_W)).reshape(-1)
    half = QK_ROPE // 2
    inv_freq = ROPE_THETA ** (-jnp.arange(0, half, 2, dtype=F32) / half)
    ar, ac = row[:, None] * inv_freq, col[:, None] * inv_freq
    ones = jnp.ones((s, QK_NOPE), F32)
    tail = HEAD_PAD - QK_NOPE - QK_ROPE
    cos_t = jnp.concatenate([ones, jnp.cos(ar), jnp.cos(ar), jnp.cos(ac), jnp.cos(ac), jnp.ones((s, tail), F32)], 1)
    sin_t = jnp.concatenate([0 * ones, -jnp.sin(ar), jnp.sin(ar), -jnp.sin(ac), jnp.sin(ac),
                             jnp.zeros((s, tail), F32)], 1)
    return cos_t, sin_t


_Q4 = QK_ROPE // 4
ROPE_SWAP = np.concatenate([np.arange(_Q4, 2 * _Q4), np.arange(0, _Q4), np.arange(3 * _Q4, 4 * _Q4),
                            np.arange(2 * _Q4, 3 * _Q4)])


def _rope_slot(w, swap):
    if swap:
        w = w[..., ROPE_SWAP]
    pad = [(0, 0)] * (w.ndim - 1) + [(QK_NOPE, HEAD_PAD - QK_NOPE - QK_ROPE)]
    return jnp.pad(w, pad)


def kernel(x, c, ctx, c_ctx, w_mod, b_mod, norm_mix, norm_ffn, w_in, b_in, q_norm, w_uq, kv_norm, w_ukv, w_branch_attn, hy_conv_w, hy_conv_b, hy_filt_w1, hy_filt_b1, hy_filt_w2, hy_filt_b2, hy_filt_w3, hy_filt_freq, hy_skip, w_branch_hyena, w_out, w_router, router_bias, w_exp_gate, w_exp_up, w_exp_down, w_sh_gate, w_sh_up, w_sh_down, final_norm,
           tiles=None):
    bsz, s, d = x.shape
    tl = dict(inproj=256, tq=512, tk=1024, fft_nb=2048, merge=512, route=512, disp=512, comb=256)
    tl.update(tiles or {})
    assert w_mod.shape[0] == 1, "single-layer trunk"
    i = 0

    rows = -(-(bsz + 1) // SUBLANES) * SUBLANES
    c_rows = jnp.pad(jnp.concatenate([c, c_ctx[None]], axis=0), ((0, rows - bsz - 1), (0, 0)))
    mod_all = _mod_call(c_rows, w_mod[i], b_mod[i])
    mod_all = jnp.pad(mod_all.reshape(rows, 6, d), ((0, 0), (0, SUBLANES - 6), (0, 0)))
    mod, modc = mod_all[:bsz], mod_all[bsz:bsz + 1]

    cuts = np.cumsum([Q_LORA, KV_LORA, QK_ROPE, 3 * HY_WIDTH])
    wi, bi = w_in[i], b_in[i][None]
    w_q, w_kv, w_pe, w_hy, w_g = jnp.split(wi, cuts, axis=1)
    b_q, b_kv, b_pe, b_hy, b_g = jnp.split(bi, cuts, axis=1)
    wa = jnp.concatenate([w_q, w_kv, _rope_slot(w_pe, False), _rope_slot(w_pe, True)], axis=1).astype(BF16)
    ba = jnp.concatenate([b_q, b_kv, _rope_slot(b_pe, False), _rope_slot(b_pe, True)], axis=1)
    wq3 = w_uq[i].reshape(Q_LORA, N_HEADS, QK_NOPE + QK_ROPE) * ATTN_SCALE
    tail = ((0, 0), (0, 0), (0, HEAD_PAD - QK_NOPE))
    wuq = (jnp.pad(wq3[..., :QK_NOPE], tail) + _rope_slot(wq3[..., QK_NOPE:], False)).reshape(Q_LORA, -1).astype(BF16)
    wuqs = _rope_slot(wq3[..., QK_NOPE:], True).reshape(Q_LORA, -1).astype(BF16)
    wkv3 = w_ukv[i].reshape(KV_LORA, N_HEADS, QK_NOPE + V_HEAD)
    wuk = jnp.pad(wkv3[..., :QK_NOPE], tail).reshape(KV_LORA, -1).astype(BF16)
    wuv = wkv3[..., QK_NOPE:].reshape(KV_LORA, -1).astype(BF16)
    nm, qn, kvn = norm_mix[i][None], q_norm[i][None], kv_norm[i][None]

    w_c = jnp.concatenate([w_kv, _rope_slot(w_pe, False)], axis=1).astype(BF16)
    b_c = jnp.concatenate([b_kv, _rope_slot(b_pe, False)], axis=1)
    ck, cv = _ctx_call(ctx, modc, nm, w_c, b_c, kvn, wuk, wuv)

    cos_t, sin_t = _rope_tables(s)
    q, k, v, hv, hx1, hx2, gate = _inproj_call(
        x, mod, nm, wa, ba, w_hy.astype(BF16), b_hy, w_g.astype(BF16), b_g, qn, wuq, wuqs, kvn, wuk, wuv,
        cos_t, sin_t, hy_conv_w[i], hy_conv_b[i][None], tl["inproj"])

    attn = _attn_call(q, k, v, ck, cv, tl["tq"], tl["tk"])
    hy = _hyena(hv, hx1, hx2, hy_filt_w1[i], hy_filt_b1[i], hy_filt_w2[i], hy_filt_b2[i], hy_filt_w3[i],
                hy_filt_freq[i], hy_skip[i], tl["fft_nb"])
    xm = _merge_call(x, attn, hy, gate, mod, w_branch_attn[i].astype(BF16), w_branch_hyena[i].astype(BF16),
                     w_out[i].astype(BF16), tl["merge"])
    return _moe(xm, mod, norm_ffn[i][None], w_router[i], router_bias[i], w_exp_gate[i], w_exp_up[i], w_exp_down[i],
                w_sh_gate[i], w_sh_up[i], w_sh_down[i], final_norm[None], tl["route"], tl["disp"], tl["comb"])
```

```python
import functools
import math

import numpy as np
import jax
import jax.numpy as jnp
from jax import lax
from jax.experimental import pallas as pl
from jax.experimental.pallas import tpu as pltpu

GRID_W = 64
N_HEADS = 8
QK_NOPE = 64
QK_ROPE = 32
V_HEAD = 64
Q_LORA = 256
KV_LORA = 128
ROPE_THETA = 10000.0
ATTN_SCALE = 1.0 / math.sqrt(QK_NOPE + QK_ROPE)
HY_WIDTH = 512
HY_ORDER = 2
HY_SHORT = 3
HY_BANDS = 8
HY_EMB = 1 + 2 * HY_BANDS
HY_EMB_PAD = 32
HY_FAST_DECAY = 0.3
HY_SLOW_DECAY = 1.5
HY_DECAY_TARGET = 1e-2
N_EXPERTS = 64
N_GROUPS = 8
GROUP_SIZE = N_EXPERTS // N_GROUPS
TOPK_GROUPS = 4
TOP_K = 8
EXPERT_FF = 256
ROUTE_SCALE = 2.5
EXPERT_BLOCK = 512
NORM_EPS = 1e-6

HEAD_PAD = 128
LANES = 128
SUBLANES = 8
VMEM_LIMIT = 48 * 1024 * 1024

F32 = jnp.float32
BF16 = jnp.bfloat16
NT_DIMS = (((1,), (1,)), ((), ()))
NN_DIMS = (((1,), (0,)), ((), ()))


def _params(*sem):
    return pltpu.CompilerParams(dimension_semantics=sem, vmem_limit_bytes=VMEM_LIMIT)


def _dot(a, b):
    return jnp.dot(a.astype(BF16), b.astype(BF16), preferred_element_type=F32)


def _split(a):
    hi = a.astype(BF16)
    lo = (a - hi.astype(F32)).astype(BF16)
    return hi, lo


def _dot3(a, b, dims=NN_DIMS):
    ah, al = _split(a)
    bh, bl = _split(b)
    d = functools.partial(lax.dot_general, dimension_numbers=dims, preferred_element_type=F32)
    return d(ah, bh) + (d(ah, bl) + d(al, bh))


def _rms(x, g):
    return x * lax.rsqrt(jnp.mean(x * x, axis=-1, keepdims=True) + NORM_EPS) * g


def _silu(x):
    return x * jax.nn.sigmoid(x)


def _full(shape):
    nd = len(shape)
    return pl.BlockSpec(shape, lambda *_: (0,) * nd)


def _mod_kernel(c_ref, w_ref, b_ref, o_ref):
    o_ref[...] = _dot3(_silu(c_ref[...]), w_ref[...]) + b_ref[...]


def _mod_call(c_rows, w_mod, b_mod):
    r, d = c_rows.shape
    n = w_mod.shape[1]
    bn = 1024
    return pl.pallas_call(
        _mod_kernel,
        grid=(n // bn,),
        in_specs=[_full((r, d)), pl.BlockSpec((d, bn), lambda j: (0, j)), pl.BlockSpec((1, bn), lambda j: (0, j))],
        out_specs=pl.BlockSpec((r, bn), lambda j: (0, j)),
        out_shape=jax.ShapeDtypeStruct((r, n), F32),
        compiler_params=_params("arbitrary"),
        name="mod",
    )(c_rows, w_mod, b_mod.reshape(1, n))


def _prenorm(x, mod_ref, row, g):
    shift = mod_ref[0, row:row + 1, :]
    scale = mod_ref[0, row + 1:row + 2, :]
    return _rms(x, g) * (1.0 + scale) + shift


def _kv_heads(kv_lat, kpe, kvn_ref, wuk_ref, wuvt_ref, k_out, vt_out):
    kvn = _rms(kv_lat, kvn_ref[...]).astype(BF16)
    kk = _dot(kvn, wuk_ref[...])
    vt = lax.dot_general(wuvt_ref[...], kvn, NT_DIMS, preferred_element_type=F32)
    ones = jnp.ones((HEAD_PAD - V_HEAD, vt.shape[1]), F32)
    for h in range(N_HEADS):
        k_out[0, h] = (kk[:, HEAD_PAD * h:HEAD_PAD * (h + 1)] + kpe).astype(BF16)
        vt_out[0, h] = jnp.concatenate([vt[V_HEAD * h:V_HEAD * (h + 1)], ones], axis=0).astype(BF16)


def _ctx_kernel(c_ref, mod_ref, nm_ref, w_ref, b_ref, kvn_ref, wuk_ref, wuv_ref, k_out, v_out):
    h = _prenorm(c_ref[0], mod_ref, 0, nm_ref[...]).astype(BF16)
    a = _dot(h, w_ref[...]) + b_ref[...]
    _kv_heads(a[:, :KV_LORA], a[:, KV_LORA:], kvn_ref, wuk_ref, wuv_ref, k_out, v_out)


def _ctx_call(ctx, modc, norm_mix, w_c, b_c, kv_norm, w_uk, w_uv):
    bsz, n, d = ctx.shape
    return pl.pallas_call(
        _ctx_kernel,
        grid=(bsz,),
        in_specs=[pl.BlockSpec((1, n, d), lambda b: (b, 0, 0)), _full(modc.shape), _full(norm_mix.shape),
                  _full(w_c.shape), _full(b_c.shape), _full(kv_norm.shape), _full(w_uk.shape), _full(w_uv.shape)],
        out_specs=[pl.BlockSpec((1, N_HEADS, n, HEAD_PAD), lambda b: (b, 0, 0, 0)),
                   pl.BlockSpec((1, N_HEADS, HEAD_PAD, n), lambda b: (b, 0, 0, 0))],
        out_shape=[jax.ShapeDtypeStruct((bsz, N_HEADS, n, HEAD_PAD), BF16),
                   jax.ShapeDtypeStruct((bsz, N_HEADS, HEAD_PAD, n), BF16)],
        compiler_params=_params("arbitrary"),
        name="ctx",
    )(ctx, modc, norm_mix, w_c, b_c, kv_norm, w_uk, w_uv)


def _inproj_kernel(x_ref, xp_ref, xn_ref, mod_ref, nm_ref, wa_ref, ba_ref, why_ref, bhy_ref, wg_ref, bg_ref,
                   qn_ref, wuq_ref, wuqs_ref, kvn_ref, wuk_ref, wuv_ref, cos_ref, sin_ref, cw_ref, cb_ref,
                   q_out, k_out, v_out, hv_out, hx1_out, hx2_out, g_out):
    i = pl.program_id(0)
    tm = x_ref.shape[1]
    nm = nm_ref[...]
    h = _prenorm(x_ref[0], mod_ref, 0, nm).astype(BF16)
    a = _dot(h, wa_ref[...]) + ba_ref[...]
    q_lat = a[:, :Q_LORA]
    kv_lat = a[:, Q_LORA:Q_LORA + KV_LORA]
    kpe_m = a[:, Q_LORA + KV_LORA:Q_LORA + KV_LORA + HEAD_PAD]
    kpe_s = a[:, Q_LORA + KV_LORA + HEAD_PAD:]
    cos = cos_ref[...]
    sin = sin_ref[...]
    qn = _rms(q_lat, qn_ref[...]).astype(BF16)
    qa = _dot(qn, wuq_ref[...])
    qs = _dot(qn, wuqs_ref[...])
    for hh in range(N_HEADS):
        sl = slice(HEAD_PAD * hh, HEAD_PAD * (hh + 1))
        q_out[0, hh] = (qa[:, sl] * cos + qs[:, sl] * sin).astype(BF16)
    _kv_heads(kv_lat, kpe_m * cos + kpe_s * sin, kvn_ref, wuk_ref, wuv_ref, k_out, v_out)
    g_out[0] = (_dot(h, wg_ref[...]) + bg_ref[...]).astype(BF16)

    why = why_ref[...]
    bhy = bhy_ref[...]
    hy = _dot(h, why) + bhy
    hp = _dot(_prenorm(xp_ref[0], mod_ref, 0, nm).astype(BF16), why) + bhy
    hn = _dot(_prenorm(xn_ref[0], mod_ref, 0, nm).astype(BF16), why) + bhy
    prev = jnp.where(i == 0, 0.0, hp[SUBLANES - 1:SUBLANES])
    nxt = jnp.where(i == pl.num_programs(0) - 1, 0.0, hn[0:1])
    rid = lax.broadcasted_iota(jnp.int32, (tm, 1), 0)
    up = jnp.where(rid == 0, prev, pltpu.roll(hy, 1, 0))
    dn = jnp.where(rid == tm - 1, nxt, pltpu.roll(hy, tm - 1, 0))
    u = up * cw_ref[0:1, :] + hy * cw_ref[1:2, :] + dn * cw_ref[2:3, :] + cb_ref[...]
    hv_out[0] = u[:, :HY_WIDTH]
    hx1_out[0] = u[:, HY_WIDTH:2 * HY_WIDTH]
    hx2_out[0] = u[:, 2 * HY_WIDTH:]


def _inproj_call(x, mod, norm_mix, wa, ba, why, bhy, wg, bg, q_norm, wuq, wuqs, kv_norm, wuk, wuvt, cos_t, sin_t, cw,
                 cb, tm):
    bsz, s, d = x.shape
    nt = s // tm
    rb = tm // SUBLANES
    last_rb = s // SUBLANES - 1
    consts = [norm_mix, wa, ba, why, bhy, wg, bg, q_norm, wuq, wuqs, kv_norm, wuk, wuvt]
    in_specs = [
        pl.BlockSpec((1, tm, d), lambda i, b: (b, i, 0)),
        pl.BlockSpec((1, SUBLANES, d), lambda i, b: (b, jnp.maximum(i * rb - 1, 0), 0)),
        pl.BlockSpec((1, SUBLANES, d), lambda i, b: (b, jnp.minimum((i + 1) * rb, last_rb), 0)),
        pl.BlockSpec((1, SUBLANES, d), lambda i, b: (b, 0, 0)),
    ] + [_full(c.shape) for c in consts] + [
        pl.BlockSpec((tm, HEAD_PAD), lambda i, b: (i, 0)),
        pl.BlockSpec((tm, HEAD_PAD), lambda i, b: (i, 0)),
        _full(cw.shape), _full(cb.shape),
    ]
    hw = HY_WIDTH
    out_specs = [
        pl.BlockSpec((1, N_HEADS, tm, HEAD_PAD), lambda i, b: (b, 0, i, 0)),
        pl.BlockSpec((1, N_HEADS, tm, HEAD_PAD), lambda i, b: (b, 0, i, 0)),
        pl.BlockSpec((1, N_HEADS, HEAD_PAD, tm), lambda i, b: (b, 0, 0, i)),
        pl.BlockSpec((1, tm, hw), lambda i, b: (b, i, 0)),
        pl.BlockSpec((1, tm, hw), lambda i, b: (b, i, 0)),
        pl.BlockSpec((1, tm, hw), lambda i, b: (b, i, 0)),
        pl.BlockSpec((1, tm, 2 * d), lambda i, b: (b, i, 0)),
    ]
    out_shape = [
        jax.ShapeDtypeStruct((bsz, N_HEADS, s, HEAD_PAD), BF16),
        jax.ShapeDtypeStruct((bsz, N_HEADS, s, HEAD_PAD), BF16),
        jax.ShapeDtypeStruct((bsz, N_HEADS, HEAD_PAD, s), BF16),
        jax.ShapeDtypeStruct((bsz, s, hw), F32),
        jax.ShapeDtypeStruct((bsz, s, hw), F32),
        jax.ShapeDtypeStruct((bsz, s, hw), F32),
        jax.ShapeDtypeStruct((bsz, s, 2 * d), BF16),
    ]
    return pl.pallas_call(
        _inproj_kernel,
        grid=(nt, bsz),
        in_specs=in_specs,
        out_specs=out_specs,
        out_shape=out_shape,
        compiler_params=_params("arbitrary", "arbitrary"),
        name="inproj",
    )(x, x, x, mod, *consts, cos_t, sin_t, cw, cb)


def _attn_kernel(q_ref, k_ref, vt_ref, o_ref, m_sc, acc_sc):
    j = pl.program_id(2)

    @pl.when(j == 0)
    def _():
        m_sc[...] = jnp.full(m_sc.shape, -jnp.inf, F32)
        acc_sc[...] = jnp.zeros(acc_sc.shape, F32)

    def scores(h):
        return lax.dot_general(k_ref[0, h], q_ref[0, h], NT_DIMS, preferred_element_type=F32)

    st = scores(0)
    for h in range(N_HEADS):
        st_next = scores(h + 1) if h + 1 < N_HEADS else None
        m_prev = m_sc[h]
        m_new = jnp.maximum(m_prev, jnp.max(st, axis=0, keepdims=True))
        pt = jnp.exp2(st - m_new).astype(BF16)
        acc_sc[h] = jnp.exp2(m_prev - m_new) * acc_sc[h] + jnp.dot(vt_ref[0, h], pt, preferred_element_type=F32)
        m_sc[h] = m_new
        st = st_next

    @pl.when(j == pl.num_programs(2) - 1)
    def _():
        ot = jnp.concatenate([acc_sc[h, :V_HEAD] / acc_sc[h, V_HEAD:V_HEAD + 1] for h in range(N_HEADS)], axis=0)
        o_ref[0] = ot.T.astype(o_ref.dtype)


def _attn_call(q, k, vt, tq, tk):
    bsz, nh, s, dh = q.shape
    nk = k.shape[2]
    dv = nh * V_HEAD
    return pl.pallas_call(
        _attn_kernel,
        grid=(bsz, s // tq, nk // tk),
        in_specs=[
            pl.BlockSpec((1, nh, tq, dh), lambda b, i, j: (b, 0, i, 0)),
            pl.BlockSpec((1, nh, tk, dh), lambda b, i, j: (b, 0, j, 0)),
            pl.BlockSpec((1, nh, dh, tk), lambda b, i, j: (b, 0, 0, j)),
        ],
        out_specs=pl.BlockSpec((1, tq, dv), lambda b, i, j: (b, i, 0)),
        out_shape=jax.ShapeDtypeStruct((bsz, s, dv), BF16),
        scratch_shapes=[pltpu.VMEM((nh, 1, tq), F32), pltpu.VMEM((nh, dh, tq), F32)],
        compiler_params=_params("arbitrary", "arbitrary", "arbitrary"),
        name="attn",
    )(q, k, vt)


def _filter_kernel(emb_ref, w1_ref, b1_ref, w2_ref, b2_ref, w3_ref, fr_ref, dl_ref, full_out, asum_out, *, seq):
    r = pl.program_id(0)
    rb = emb_ref.shape[0]
    emb = emb_ref[...]
    fr = fr_ref[...]
    h = jnp.sin(fr * (_dot3(emb, w1_ref[...]) + b1_ref[...]))
    h = jnp.sin(fr * (_dot3(h, w2_ref[...]) + b2_ref[...]))
    k = _dot3(h, w3_ref[0]) * jnp.exp(-emb[:, 0:1] * dl_ref[...])
    row = r * rb + lax.broadcasted_iota(jnp.int32, (rb, 1), 0)
    k = jnp.where(row == seq, 0.0, k)
    full_out[...] = k

    @pl.when(r == 0)
    def _():
        asum_out[...] = jnp.zeros(asum_out.shape, F32)

    asum_out[...] += jnp.sum(jnp.abs(k), axis=0, keepdims=True)


def _filter_call(emb, w1, b1, w2, b2, w3sel, freq, deltas2, seq, rb):
    n2 = emb.shape[0]
    half_blocks = seq // rb
    width = w3sel.shape[2]
    return pl.pallas_call(
        functools.partial(_filter_kernel, seq=seq),
        grid=(n2 // rb,),
        in_specs=[pl.BlockSpec((rb, HY_EMB_PAD), lambda r: (r, 0)), _full(w1.shape), _full(b1.shape),
                  _full(w2.shape), _full(b2.shape),
                  pl.BlockSpec((1,) + w3sel.shape[1:], lambda r: (r // half_blocks, 0, 0)),
                  _full(freq.shape), _full(deltas2.shape)],
        out_specs=[pl.BlockSpec((rb, width), lambda r: (r, 0)), pl.BlockSpec((1, width), lambda r: (0, 0))],
        out_shape=[jax.ShapeDtypeStruct((n2, width), F32), jax.ShapeDtypeStruct((1, width), F32)],
        compiler_params=_params("arbitrary"),
        name="filt",
    )(emb, w1, b1, w2, b2, w3sel, freq, deltas2)


def _fa_kernel(u_ref, f_ref, a_out):
    two, _, hn, nb = u_ref.shape
    x = u_ref[...].reshape(two * hn, nb)
    a_out[0] = _dot(f_ref[...], x).astype(a_out.dtype)


def _fa_call(u4, fmat, nb):
    _, p, hn, width = u4.shape
    n = 2 * hn
    return pl.pallas_call(
        _fa_kernel,
        grid=(p, width // nb),
        in_specs=[pl.BlockSpec((2, 1, hn, nb), lambda q, g: (0, q, 0, g)), _full(fmat.shape)],
        out_specs=pl.BlockSpec((1, 2 * n, nb), lambda q, g: (q, 0, g)),
        out_shape=jax.ShapeDtypeStruct((p, 2 * n, width), BF16),
        compiler_params=_params("arbitrary", "arbitrary"),
        name="fa",
    )(u4, fmat)


def _fb_kernel(a_ref, g_ref, asum_ref, kf_out):
    two, _, n, c = a_ref.shape[1:]
    a = a_ref[0].reshape(two * n, c)
    x = _dot(g_ref[0], a) / (asum_ref[...] + 1e-6)
    kf_out[0] = x.reshape(two, n, c)


def _fb_call(a5, gmat, asum):
    _, _, n, _, c = a5.shape
    return pl.pallas_call(
        _fb_kernel,
        grid=(n,),
        in_specs=[pl.BlockSpec((1, 2, 1, n, c), lambda k: (0, 0, k, 0, 0)),
                  pl.BlockSpec((1, 2 * n, 2 * n), lambda k: (k, 0, 0)), _full(asum.shape)],
        out_specs=pl.BlockSpec((1, 2, n, c), lambda k: (k, 0, 0, 0)),
        out_shape=jax.ShapeDtypeStruct((n, 2, n, c), F32),
        compiler_params=_params("arbitrary"),
        name="fb",
    )(a5, gmat, asum)


def _mid_kernel(a_ref, g_ref, h_ref, kf_ref, b_out):
    two, _, n, c = a_ref.shape[1:]
    a = a_ref[0].reshape(two * n, c)
    x = _dot(g_ref[0], a)
    xr, xi = x[:n], x[n:]
    kr, ki = kf_ref[0, 0], kf_ref[0, 1]
    y = jnp.concatenate([xr * kr - xi * ki, xr * ki + xi * kr], axis=0)
    b_out[0] = _dot(h_ref[0], y).reshape(two, 1, n, c).astype(b_out.dtype)


def _mid_call(a5, gmat, hmat, kf, order):
    p, _, n, _, c = a5.shape
    return pl.pallas_call(
        _mid_kernel,
        grid=(n, p),
        in_specs=[pl.BlockSpec((1, 2, 1, n, c), lambda k, q: (q, 0, k, 0, 0)),
                  pl.BlockSpec((1, 2 * n, 2 * n), lambda k, q: (k, 0, 0)),
                  pl.BlockSpec((1, 2 * n, 2 * n), lambda k, q: (k, 0, 0)),
                  pl.BlockSpec((1, 2, n, c), lambda k, q: (k, 0, 0, order))],
        out_specs=pl.BlockSpec((1, 2, 1, n, c), lambda k, q: (q, 0, k, 0, 0)),
        out_shape=jax.ShapeDtypeStruct(a5.shape, BF16),
        compiler_params=_params("arbitrary", "arbitrary"),
        name="mid",
    )(a5, gmat, hmat, kf)


def _fc_kernel(b_ref, f_ref, u_ref, m_ref, skip_ref, o_out):
    two, _, hn, nb = u_ref.shape
    y = _dot(f_ref[...], b_ref[0]).reshape(two, 1, hn, nb)
    o_out[...] = (m_ref[...] * (y + u_ref[...] * skip_ref[...])).astype(o_out.dtype)


def _fc_call(b3, finv, u4, m4, skip_t, nb, out_dtype):
    _, p, hn, width = u4.shape
    n = 2 * hn
    blk = pl.BlockSpec((2, 1, hn, nb), lambda q, g: (0, q, 0, g))
    return pl.pallas_call(
        _fc_kernel,
        grid=(p, width // nb),
        in_specs=[pl.BlockSpec((1, 2 * n, nb), lambda q, g: (q, 0, g)), _full(finv.shape), blk, blk,
                  pl.BlockSpec((1, nb), lambda q, g: (0, g))],
        out_specs=blk,
        out_shape=jax.ShapeDtypeStruct(u4.shape, out_dtype),
        compiler_params=_params("arbitrary", "arbitrary"),
        name="fc",
    )(b3, finv, u4, m4, skip_t)


def _dft_tables(n):
    hn = n // 2
    k = np.arange(n)[:, None]
    ang = -2.0 * np.pi * (k * np.arange(n)[None, :] % n) / n
    fr, fi = np.cos(ang), np.sin(ang)
    f_data = np.block([[fr[:, :hn], -fi[:, :hn]], [fi[:, :hn], fr[:, :hn]]])
    f_filt = np.concatenate([fr, fi], axis=0)
    er, ei = fr[:hn], -fi[:hn]
    f_inv = np.block([[er, -ei], [ei, er]]) / float(n * n)
    k1 = jnp.arange(n, dtype=jnp.int32)[:, None, None]
    k2 = jnp.arange(n, dtype=jnp.int32)[None, :, None]
    m2 = jnp.arange(n, dtype=jnp.int32)[None, None, :]
    idx = (m2 * (k1 + n * k2)) % (n * n)
    ang2 = idx.astype(F32) * (-2.0 * math.pi / (n * n))
    gr, gi = jnp.cos(ang2), jnp.sin(ang2)
    g = jnp.concatenate([jnp.concatenate([gr, -gi], axis=2), jnp.concatenate([gi, gr], axis=2)], axis=1)
    h = jnp.swapaxes(g, 1, 2)
    return (jnp.asarray(f_data, BF16), jnp.asarray(f_filt, BF16), jnp.asarray(f_inv, BF16),
            g.astype(BF16), h.astype(BF16))


def _hyena_filter_tables(seq):
    t = jnp.linspace(0.0, 1.0, seq, dtype=F32)[:, None]
    w = 2.0 * math.pi * jnp.arange(seq, dtype=F32)[:, None] / seq
    f = jnp.linspace(1e-4, HY_BANDS - 1, HY_BANDS, dtype=F32)[None, :]
    emb = jnp.concatenate([t, jnp.cos(f * w), -jnp.sin(f * w)], axis=-1)
    pos = jnp.concatenate([jnp.arange(seq), jnp.array([0]), jnp.arange(seq - 1, 0, -1)])
    emb = jnp.pad(emb[pos], ((0, 0), (0, HY_EMB_PAD - HY_EMB)))
    deltas = jnp.abs(jnp.linspace(math.log(HY_DECAY_TARGET) / HY_SLOW_DECAY,
                                  math.log(HY_DECAY_TARGET) / HY_FAST_DECAY, HY_WIDTH, dtype=F32))
    return emb, jnp.tile(deltas, HY_ORDER)[None, :]


def _hyena(hv, hx1, hx2, w1, b1, w2, b2, w3, freq, skip, nb):
    bsz, seq, c = hv.shape
    n = int(round(math.sqrt(2 * seq)))
    assert n * n == 2 * seq and bsz % 2 == 0
    hn, p = n // 2, bsz // 2
    f_data, f_filt, f_inv, gmat, hmat = _dft_tables(n)

    emb, deltas2 = _hyena_filter_tables(seq)
    w1p = jnp.pad(w1, ((0, HY_EMB_PAD - HY_EMB), (0, 0)))
    w3r = w3.reshape(w3.shape[0], HY_ORDER, 2, c)
    w3sel = jnp.stack([w3r[:, :, 0, :].reshape(-1, HY_ORDER * c), w3r[:, :, 1, :].reshape(-1, HY_ORDER * c)])
    full, asum = _filter_call(emb, w1p, b1[None], w2, b2[None], w3sel, freq[None], deltas2, seq, min(512, seq))
    c2 = HY_ORDER * c
    fa = _fa_call(full.reshape(2, 1, hn, n * c2), f_filt, nb)
    kf = _fb_call(fa.reshape(1, 2, n, n, c2), gmat, asum)

    def view(t):
        return t.reshape(2, p, hn, n * c)

    def long_conv(u4, m4, order, out_dtype):
        a = _fa_call(u4, f_data, nb)
        bm = _mid_call(a.reshape(p, 2, n, n, c), gmat, hmat, kf, order)
        skip_t = jnp.tile(skip[order], n)[None, :]
        return _fc_call(bm.reshape(p, 2 * n, n * c), f_inv, u4, m4, skip_t, nb, out_dtype)

    z = long_conv(view(hv), view(hx1), 0, F32)
    out = long_conv(z, view(hx2), 1, BF16)
    return out.reshape(bsz, seq, c)


def _merge_kernel(x_ref, at_ref, hy_ref, g_ref, mod_ref, wba_ref, wbh_ref, wo_ref, o_ref):
    d = x_ref.shape[2]
    g = g_ref[0].astype(F32)
    y = (jax.nn.sigmoid(g[:, :d]) * _dot(at_ref[0], wba_ref[...])
         + jax.nn.sigmoid(g[:, d:]) * _dot(hy_ref[0], wbh_ref[...]))
    o_ref[0] = x_ref[0] + mod_ref[0, 2:3, :] * _dot(y, wo_ref[...])


def _merge_call(x, attn, hy, gate, mod, wba, wbh, wo, tm):
    bsz, s, d = x.shape

    def tok(w):
        return pl.BlockSpec((1, tm, w), lambda b, i: (b, i, 0))

    return pl.pallas_call(
        _merge_kernel,
        grid=(bsz, s // tm),
        in_specs=[tok(d), tok(attn.shape[2]), tok(hy.shape[2]), tok(2 * d),
                  pl.BlockSpec((1, SUBLANES, d), lambda b, i: (b, 0, 0)),
                  _full(wba.shape), _full(wbh.shape), _full(wo.shape)],
        out_specs=tok(d),
        out_shape=jax.ShapeDtypeStruct((bsz, s, d), F32),
        compiler_params=_params("arbitrary", "arbitrary"),
        name="merge",
    )(x, attn, hy, gate, mod, wba, wbh, wo)


def _route_kernel(xm_ref, mod_ref, nf_ref, wrt_ref, rb_ref, tri_ref, h2_out, e_out, w_out, p_out, cnt_out, carry_sc):
    step = pl.program_id(0)
    tm = xm_ref.shape[1]
    ng, gs = N_GROUPS, GROUP_SIZE

    @pl.when(step == 0)
    def _():
        carry_sc[...] = jnp.zeros(carry_sc.shape, F32)

    h2 = _prenorm(xm_ref[0], mod_ref, 3, nf_ref[...])
    h2_out[0] = h2
    scores = jax.nn.sigmoid(_dot3(wrt_ref[...], h2, NT_DIMS))
    sel = scores + rb_ref[...]
    slabs = [sel[ng * j:ng * (j + 1)] for j in range(gs)]
    sc_slabs = [scores[ng * j:ng * (j + 1)] for j in range(gs)]

    top1 = jnp.full((ng, tm), -jnp.inf, F32)
    top2 = top1
    for x in slabs:
        top2 = jnp.maximum(top2, jnp.minimum(top1, x))
        top1 = jnp.maximum(top1, x)
    gscore = top1 + top2
    gid = lax.broadcasted_iota(jnp.int32, (ng, 1), 0)
    rank = jnp.zeros((ng, tm), jnp.int32)
    for g2 in range(ng):
        row = gscore[g2:g2 + 1]
        beats = (row > gscore) | ((row == gscore) & (g2 < gid))
        rank = rank + beats.astype(jnp.int32)
    gmask = rank < TOPK_GROUPS

    cand = [jnp.where(gmask, x, -jnp.inf) for x in slabs]
    eid = [gid * gs + j for j in range(gs)]
    chosen = []
    for _ in range(TOP_K):
        best = functools.reduce(jnp.maximum, cand)
        best = jnp.max(best, axis=0, keepdims=True)
        idx = functools.reduce(jnp.minimum, [jnp.where(cand[j] == best, eid[j], N_EXPERTS) for j in range(gs)])
        idx = jnp.min(idx, axis=0, keepdims=True)
        chosen.append(idx)
        cand = [jnp.where(eid[j] == idx, -jnp.inf, cand[j]) for j in range(gs)]

    hit = [[eid[j] == idx for j in range(gs)] for idx in chosen]
    mask = [functools.reduce(jnp.logical_or, [hit[k][j] for k in range(TOP_K)]) for j in range(gs)]
    maskf = jnp.concatenate([m.astype(F32) for m in mask], axis=0)
    before = jnp.dot(maskf.astype(BF16), tri_ref[...], preferred_element_type=F32) + carry_sc[...]
    carry_sc[...] += jnp.sum(maskf, axis=1, keepdims=True)
    cnt_out[...] = jnp.broadcast_to(carry_sc[...], cnt_out.shape)
    bslabs = [before[ng * j:ng * (j + 1)] for j in range(gs)]

    def pick(k, vals):
        tot = functools.reduce(jnp.add, [jnp.where(hit[k][j], vals[j], 0.0) for j in range(gs)])
        return jnp.sum(tot, axis=0, keepdims=True)

    wk = [pick(k, sc_slabs) for k in range(TOP_K)]
    wsum = functools.reduce(jnp.add, wk)
    w_out[...] = jnp.concatenate(wk, axis=0) / wsum * ROUTE_SCALE
    p_out[...] = jnp.concatenate([pick(k, bslabs) for k in range(TOP_K)], axis=0).astype(jnp.int32)
    e_out[...] = jnp.concatenate(chosen, axis=0)


def _route_call(xm, mod, norm_ffn, wrt, rbias, tm):
    bsz, s, d = xm.shape
    t = bsz * s
    nt = s // tm
    tri = (jnp.arange(tm)[:, None] < jnp.arange(tm)[None, :]).astype(BF16)
    tok = pl.BlockSpec((TOP_K, tm), lambda i: (0, i))
    return pl.pallas_call(
        _route_kernel,
        grid=(t // tm,),
        in_specs=[pl.BlockSpec((1, tm, d), lambda i: (i // nt, i % nt, 0)),
                  pl.BlockSpec((1, SUBLANES, d), lambda i: (i // nt, 0, 0)),
                  _full(norm_ffn.shape), _full(wrt.shape), _full(rbias.shape), _full(tri.shape)],
        out_specs=[pl.BlockSpec((1, tm, d), lambda i: (i // nt, i % nt, 0)), tok, tok, tok,
                   _full((N_EXPERTS, LANES))],
        out_shape=[jax.ShapeDtypeStruct((bsz, s, d), F32), jax.ShapeDtypeStruct((TOP_K, t), jnp.int32),
                   jax.ShapeDtypeStruct((TOP_K, t), F32), jax.ShapeDtypeStruct((TOP_K, t), jnp.int32),
                   jax.ShapeDtypeStruct((N_EXPERTS, LANES), F32)],
        scratch_shapes=[pltpu.VMEM((N_EXPERTS, 1), F32)],
        compiler_params=_params("arbitrary"),
        name="route",
    )(xm, mod, norm_ffn, wrt, rbias, tri)


def _zero_kernel(o_ref):
    o_ref[...] = jnp.zeros(o_ref.shape, o_ref.dtype)


def _zero_call(rows, d):
    return pl.pallas_call(
        _zero_kernel,
        grid=(rows // EXPERT_BLOCK,),
        out_specs=pl.BlockSpec((EXPERT_BLOCK, d), lambda r: (r, 0)),
        out_shape=jax.ShapeDtypeStruct((rows, d), F32),
        compiler_params=_params("arbitrary"),
        name="zero",
    )()


def _dispatch_kernel(dest_hbm, h_ref, xs_in, xs_out, dest_sm, sem, isem):
    del xs_in
    step = pl.program_id(0)
    tm = h_ref.shape[0]
    n = TOP_K * tm
    idx_copy = pltpu.make_async_copy(dest_hbm.at[pl.ds(pl.multiple_of(step * n, n), n)], dest_sm, isem)
    idx_copy.start()
    idx_copy.wait()

    def row_copy(k, i):
        return pltpu.make_async_copy(h_ref.at[pl.ds(i, 1), :], xs_out.at[pl.ds(dest_sm[k * tm + i], 1), :], sem)

    def issue(i, c):
        for k in range(TOP_K):
            row_copy(k, i).start()
        return c

    def drain(i, c):
        for k in range(TOP_K):
            row_copy(k, i).wait()
        return c

    lax.fori_loop(0, tm, issue, 0)
    lax.fori_loop(0, tm, drain, 0)


def _dispatch_call(dest_flat, h2, xs, tm):
    t, d = h2.shape
    return pl.pallas_call(
        _dispatch_kernel,
        grid=(t // tm,),
        in_specs=[pl.BlockSpec(memory_space=pl.ANY), pl.BlockSpec((tm, d), lambda i: (i, 0)),
                  pl.BlockSpec(memory_space=pl.ANY)],
        out_specs=pl.BlockSpec(memory_space=pl.ANY),
        out_shape=jax.ShapeDtypeStruct(xs.shape, xs.dtype),
        scratch_shapes=[pltpu.SMEM((TOP_K * tm,), jnp.int32), pltpu.SemaphoreType.DMA(()),
                        pltpu.SemaphoreType.DMA(())],
        input_output_aliases={2: 0},
        compiler_params=_params("arbitrary"),
        name="dispatch",
    )(dest_flat, h2, xs)


def _expert_kernel(blk_ref, nused_ref, x_ref, wgu_ref, wd_ref, y_ref):
    used = pl.program_id(0) < nused_ref[0]

    @pl.when(used)
    def _():
        gu = _dot(x_ref[...], wgu_ref[0])
        a = _silu(gu[:, :EXPERT_FF]) * gu[:, EXPERT_FF:]
        y_ref[...] = _dot(a, wd_ref[0])

    @pl.when(jnp.logical_not(used))
    def _():
        y_ref[...] = jnp.zeros(y_ref.shape, y_ref.dtype)


def _expert_call(blk_e, nused, xs, wgu, wd):
    rows, d = xs.shape
    nblk = rows // EXPERT_BLOCK

    def row_map(i, blk, nu):
        return (jnp.minimum(i, nu[0] - 1), 0)

    return pl.pallas_call(
        _expert_kernel,
        grid_spec=pltpu.PrefetchScalarGridSpec(
            num_scalar_prefetch=2, grid=(nblk,),
            in_specs=[pl.BlockSpec((EXPERT_BLOCK, d), row_map),
                      pl.BlockSpec((1,) + wgu.shape[1:], lambda i, blk, nu: (blk[i], 0, 0)),
                      pl.BlockSpec((1,) + wd.shape[1:], lambda i, blk, nu: (blk[i], 0, 0))],
            out_specs=pl.BlockSpec((EXPERT_BLOCK, d), lambda i, blk, nu: (i, 0))),
        out_shape=jax.ShapeDtypeStruct((rows, d), F32),
        compiler_params=_params("arbitrary"),
        name="expert",
    )(blk_e, nused, xs, wgu, wd)


def _combine_kernel(dest_hbm, ys_hbm, w_ref, xm_ref, h_ref, mod_ref, wsgu_ref, wsd_ref, fn_ref, o_ref,
                    dest_sm, gbuf, sem, isem):
    step = pl.program_id(0)
    tm = xm_ref.shape[0]
    n = TOP_K * tm
    idx_copy = pltpu.make_async_copy(dest_hbm.at[pl.ds(pl.multiple_of(step * n, n), n)], dest_sm, isem)
    idx_copy.start()
    idx_copy.wait()

    def row_copy(k, i):
        return pltpu.make_async_copy(ys_hbm.at[pl.ds(dest_sm[k * tm + i], 1), :], gbuf.at[k, pl.ds(i, 1), :], sem)

    def issue(i, c):
        for k in range(TOP_K):
            row_copy(k, i).start()
        return c

    def drain(i, c):
        for k in range(TOP_K):
            row_copy(k, i).wait()
        return c

    lax.fori_loop(0, tm, issue, 0)
    h = h_ref[...]
    gu = _dot(h, wsgu_ref[...])
    ff = gu.shape[1] // 2
    shared = _dot(_silu(gu[:, :ff]) * gu[:, ff:], wsd_ref[...])
    lax.fori_loop(0, tm, drain, 0)
    w = w_ref[...]
    routed = functools.reduce(jnp.add, [w[:, k:k + 1] * gbuf[k] for k in range(TOP_K)])
    x = xm_ref[...] + mod_ref[0, 5:6, :] * (routed + shared)
    o_ref[...] = _rms(x, fn_ref[...])


def _combine_call(dest_flat, ys, w_tk, xm, h2, mod, wsgu, wsd, final_norm, tm, tiles_per_batch):
    t, d = xm.shape
    tok = pl.BlockSpec((tm, d), lambda i: (i, 0))
    return pl.pallas_call(
        _combine_kernel,
        grid=(t // tm,),
        in_specs=[pl.BlockSpec(memory_space=pl.ANY), pl.BlockSpec(memory_space=pl.ANY),
                  pl.BlockSpec((tm, TOP_K), lambda i: (i, 0)), tok, tok,
                  pl.BlockSpec((1, SUBLANES, d), lambda i: (i // tiles_per_batch, 0, 0)),
                  _full(wsgu.shape), _full(wsd.shape), _full(final_norm.shape)],
        out_specs=tok,
        out_shape=jax.ShapeDtypeStruct((t, d), F32),
        scratch_shapes=[pltpu.SMEM((TOP_K * tm,), jnp.int32), pltpu.VMEM((TOP_K, tm, d), F32),
                        pltpu.SemaphoreType.DMA(()), pltpu.SemaphoreType.DMA(())],
        compiler_params=_params("arbitrary"),
        name="combine",
    )(dest_flat, ys, w_tk, xm, h2, mod, wsgu, wsd, final_norm)


def _dest_kernel(start_ref, e_ref, p_ref, o_ref):
    e = e_ref[...]
    base = jnp.zeros(e.shape, jnp.int32)
    for x in range(N_EXPERTS):
        base = jnp.where(e == x, start_ref[x], base)
    o_ref[...] = base + p_ref[...]


def _dest_call(start, eidx, pos):
    k, t = eidx.shape
    tb = min(t, 2048)
    blk = pl.BlockSpec((k, tb), lambda i, st: (0, i))
    return pl.pallas_call(
        _dest_kernel,
        grid_spec=pltpu.PrefetchScalarGridSpec(num_scalar_prefetch=1, grid=(t // tb,), in_specs=[blk, blk],
                                               out_specs=blk),
        out_shape=jax.ShapeDtypeStruct((k, t), jnp.int32),
        compiler_params=_params("arbitrary"),
        name="dest",
    )(start, eidx, pos)


def _tile_major(a_kt, tm):
    k, t = a_kt.shape
    return a_kt.reshape(k, t // tm, tm).transpose(1, 0, 2).reshape(-1)


def _moe(xm, mod, norm_ffn, w_router, router_bias, wg, wu, wd, wsg, wsu, wsd, final_norm, tm_route, tm_disp,
         tm_comb):
    bsz, s, d = xm.shape
    t = bsz * s
    perm = (np.arange(N_EXPERTS) % N_GROUPS) * GROUP_SIZE + np.arange(N_EXPERTS) // N_GROUPS
    wrt = w_router.T[perm]
    rbias = router_bias[perm][:, None]
    h2, eidx, w_kt, pos, cnt = _route_call(xm, mod, norm_ffn, wrt, rbias, tm_route)

    inv = np.argsort(perm)
    counts = cnt[:, 0].astype(jnp.int32)[inv]
    padded = (counts + EXPERT_BLOCK - 1) // EXPERT_BLOCK * EXPERT_BLOCK
    pad_end = jnp.cumsum(padded)
    pad_start = pad_end - padded
    dest = _dest_call(pad_start.astype(jnp.int32), eidx, pos)
    nblk = -(-(t * TOP_K) // EXPERT_BLOCK) + N_EXPERTS
    rows = nblk * EXPERT_BLOCK
    blk_first = jnp.arange(nblk, dtype=jnp.int32)[:, None] * EXPERT_BLOCK
    blk_e = jnp.minimum(jnp.sum((pad_end[None, :] <= blk_first).astype(jnp.int32), axis=1), N_EXPERTS - 1)
    nused = (pad_end[-1:] // EXPERT_BLOCK).astype(jnp.int32)

    h2f = h2.reshape(t, d)
    xs = _zero_call(rows, d)
    xs = _dispatch_call(_tile_major(dest, tm_disp), h2f, xs, tm_disp)
    wgu = jnp.concatenate([wg, wu], axis=2).astype(BF16)
    ys = _expert_call(blk_e, nused, xs, wgu, wd.astype(BF16))
    wsgu = jnp.concatenate([wsg, wsu], axis=1).astype(BF16)
    out = _combine_call(_tile_major(dest, tm_comb), ys, w_kt.T, xm.reshape(t, d), h2f, mod, wsgu, wsd.astype(BF16),
                        final_norm, tm_comb, s // tm_comb)
    return out.reshape(bsz, s, d)


def _rope_tables(s):
    rows = s // GRID_W
    row = jnp.broadcast_to(jnp.arange(rows, dtype=F32)[:, None], (rows, GRID_W)).reshape(-1)
    col = jnp.broadcast_to(jnp.arange(GRID_W, dtype=F32)[None, :], (rows, GRID_W)).reshape(-1)
    half = QK_ROPE // 2
    inv_freq = ROPE_THETA ** (-jnp.arange(0, half, 2, dtype=F32) / half)
    ar, ac = row[:, None] * inv_freq, col[:, None] * inv_freq
    ones = jnp.ones((s, QK_NOPE), F32)
    tail = HEAD_PAD - QK_NOPE - QK_ROPE
    cos_t = jnp.concatenate([ones, jnp.cos(ar), jnp.cos(ar), jnp.cos(ac), jnp.cos(ac), jnp.ones((s, tail), F32)], 1)
    sin_t = jnp.concatenate([0 * ones, -jnp.sin(ar), jnp.sin(ar), -jnp.sin(ac), jnp.sin(ac),
                             jnp.zeros((s, tail), F32)], 1)
    return cos_t, sin_t


_Q4 = QK_ROPE // 4
ROPE_SWAP = np.concatenate([np.arange(_Q4, 2 * _Q4), np.arange(0, _Q4), np.arange(3 * _Q4, 4 * _Q4),
                            np.arange(2 * _Q4, 3 * _Q4)])


def _rope_slot(w, swap):
    if swap:
        w = w[..., ROPE_SWAP]
    pad = [(0, 0)] * (w.ndim - 1) + [(QK_NOPE, HEAD_PAD - QK_NOPE - QK_ROPE)]
    return jnp.pad(w, pad)


def kernel(x, c, ctx, c_ctx, w_mod, b_mod, norm_mix, norm_ffn, w_in, b_in, q_norm, w_uq, kv_norm, w_ukv, w_branch_attn, hy_conv_w, hy_conv_b, hy_filt_w1, hy_filt_b1, hy_filt_w2, hy_filt_b2, hy_filt_w3, hy_filt_freq, hy_skip, w_branch_hyena, w_out, w_router, router_bias, w_exp_gate, w_exp_up, w_exp_down, w_sh_gate, w_sh_up, w_sh_down, final_norm,
           tiles=None):
    bsz, s, d = x.shape
    tl = dict(inproj=256, tq=512, tk=1408, fft_nb=2048, merge=512, route=512, disp=512, comb=256)
    tl.update(tiles or {})
    assert w_mod.shape[0] == 1, "single-layer trunk"
    i = 0

    rows = -(-(bsz + 1) // SUBLANES) * SUBLANES
    c_rows = jnp.pad(jnp.concatenate([c, c_ctx[None]], axis=0), ((0, rows - bsz - 1), (0, 0)))
    mod_all = _mod_call(c_rows, w_mod[i], b_mod[i])
    mod_all = jnp.pad(mod_all.reshape(rows, 6, d), ((0, 0), (0, SUBLANES - 6), (0, 0)))
    mod, modc = mod_all[:bsz], mod_all[bsz:bsz + 1]

    cuts = np.cumsum([Q_LORA, KV_LORA, QK_ROPE, 3 * HY_WIDTH])
    wi, bi = w_in[i], b_in[i][None]
    w_q, w_kv, w_pe, w_hy, w_g = jnp.split(wi, cuts, axis=1)
    b_q, b_kv, b_pe, b_hy, b_g = jnp.split(bi, cuts, axis=1)
    wa = jnp.concatenate([w_q, w_kv, _rope_slot(w_pe, False), _rope_slot(w_pe, True)], axis=1).astype(BF16)
    ba = jnp.concatenate([b_q, b_kv, _rope_slot(b_pe, False), _rope_slot(b_pe, True)], axis=1)
    wq3 = w_uq[i].reshape(Q_LORA, N_HEADS, QK_NOPE + QK_ROPE) * (ATTN_SCALE * math.log2(math.e))
    tail = ((0, 0), (0, 0), (0, HEAD_PAD - QK_NOPE))
    wuq = (jnp.pad(wq3[..., :QK_NOPE], tail) + _rope_slot(wq3[..., QK_NOPE:], False)).reshape(Q_LORA, -1).astype(BF16)
    wuqs = _rope_slot(wq3[..., QK_NOPE:], True).reshape(Q_LORA, -1).astype(BF16)
    wkv3 = w_ukv[i].reshape(KV_LORA, N_HEADS, QK_NOPE + V_HEAD)
    wuk = jnp.pad(wkv3[..., :QK_NOPE], tail).reshape(KV_LORA, -1).astype(BF16)
    wuvt = wkv3[..., QK_NOPE:].reshape(KV_LORA, -1).T.astype(BF16)
    nm, qn, kvn = norm_mix[i][None], q_norm[i][None], kv_norm[i][None]

    w_c = jnp.concatenate([w_kv, _rope_slot(w_pe, False)], axis=1).astype(BF16)
    b_c = jnp.concatenate([b_kv, _rope_slot(b_pe, False)], axis=1)
    ck, cvt = _ctx_call(ctx, modc, nm, w_c, b_c, kvn, wuk, wuvt)

    cos_t, sin_t = _rope_tables(s)
    q, k, vt, hv, hx1, hx2, gate = _inproj_call(
        x, mod, nm, wa, ba, w_hy.astype(BF16), b_hy, w_g.astype(BF16), b_g, qn, wuq, wuqs, kvn, wuk, wuvt,
        cos_t, sin_t, hy_conv_w[i], hy_conv_b[i][None], tl["inproj"])

    attn = _attn_call(q, jnp.concatenate([ck, k], axis=2), jnp.concatenate([cvt, vt], axis=3), tl["tq"], tl["tk"])
    hy = _hyena(hv, hx1, hx2, hy_filt_w1[i], hy_filt_b1[i], hy_filt_w2[i], hy_filt_b2[i], hy_filt_w3[i],
                hy_filt_freq[i], hy_skip[i], tl["fft_nb"])
    xm = _merge_call(x, attn, hy, gate, mod, w_branch_attn[i].astype(BF16), w_branch_hyena[i].astype(BF16),
                     w_out[i].astype(BF16), tl["merge"])
    return _moe(xm, mod, norm_ffn[i][None], w_router[i], router_bias[i], w_exp_gate[i], w_exp_up[i], w_exp_down[i],
                w_sh_gate[i], w_sh_up[i], w_sh_down[i], final_norm[None], tl["route"], tl["disp"], tl["comb"])
```

```python
import functools
import math

import numpy as np
import jax
import jax.numpy as jnp
from jax import lax
from jax.experimental import pallas as pl
from jax.experimental.pallas import tpu as pltpu

GRID_W = 64
N_HEADS = 8
QK_NOPE = 64
QK_ROPE = 32
V_HEAD = 64
Q_LORA = 256
KV_LORA = 128
ROPE_THETA = 10000.0
ATTN_SCALE = 1.0 / math.sqrt(QK_NOPE + QK_ROPE)
HY_WIDTH = 512
HY_ORDER = 2
HY_SHORT = 3
HY_BANDS = 8
HY_EMB = 1 + 2 * HY_BANDS
HY_EMB_PAD = 32
HY_FAST_DECAY = 0.3
HY_SLOW_DECAY = 1.5
HY_DECAY_TARGET = 1e-2
N_EXPERTS = 64
N_GROUPS = 8
GROUP_SIZE = N_EXPERTS // N_GROUPS
TOPK_GROUPS = 4
TOP_K = 8
EXPERT_FF = 256
ROUTE_SCALE = 2.5
EXPERT_BLOCK = 512
RUN_ALIGN = 8
NORM_EPS = 1e-6

HEAD_PAD = 128
LANES = 128
SUBLANES = 8
VMEM_LIMIT = 48 * 1024 * 1024

F32 = jnp.float32
BF16 = jnp.bfloat16
NT_DIMS = (((1,), (1,)), ((), ()))
NN_DIMS = (((1,), (0,)), ((), ()))


def _params(*sem):
    return pltpu.CompilerParams(dimension_semantics=sem, vmem_limit_bytes=VMEM_LIMIT)


def _dot(a, b):
    return jnp.dot(a.astype(BF16), b.astype(BF16), preferred_element_type=F32)


def _split(a):
    hi = a.astype(BF16)
    lo = (a - hi.astype(F32)).astype(BF16)
    return hi, lo


def _dot3(a, b, dims=NN_DIMS):
    ah, al = _split(a)
    bh, bl = _split(b)
    d = functools.partial(lax.dot_general, dimension_numbers=dims, preferred_element_type=F32)
    return d(ah, bh) + (d(ah, bl) + d(al, bh))


def _rms(x, g):
    return x * lax.rsqrt(jnp.mean(x * x, axis=-1, keepdims=True) + NORM_EPS) * g


def _silu(x):
    return x * jax.nn.sigmoid(x)


def _full(shape):
    nd = len(shape)
    return pl.BlockSpec(shape, lambda *_: (0,) * nd)


def _mod_kernel(c_ref, w_ref, b_ref, o_ref):
    o_ref[...] = _dot3(_silu(c_ref[...]), w_ref[...]) + b_ref[...]


def _mod_call(c_rows, w_mod, b_mod):
    r, d = c_rows.shape
    n = w_mod.shape[1]
    bn = 1024
    return pl.pallas_call(
        _mod_kernel,
        grid=(n // bn,),
        in_specs=[_full((r, d)), pl.BlockSpec((d, bn), lambda j: (0, j)), pl.BlockSpec((1, bn), lambda j: (0, j))],
        out_specs=pl.BlockSpec((r, bn), lambda j: (0, j)),
        out_shape=jax.ShapeDtypeStruct((r, n), F32),
        compiler_params=_params("arbitrary"),
        name="mod",
    )(c_rows, w_mod, b_mod.reshape(1, n))


def _prenorm(x, mod_ref, row, g):
    shift = mod_ref[0, row:row + 1, :]
    scale = mod_ref[0, row + 1:row + 2, :]
    return _rms(x, g) * (1.0 + scale) + shift


def _kv_heads(kv_lat, kpe, kvn_ref, wuk_ref, wuvt_ref, k_out, vt_out):
    kvn = _rms(kv_lat, kvn_ref[...]).astype(BF16)
    kk = _dot(kvn, wuk_ref[...])
    vt = lax.dot_general(wuvt_ref[...], kvn, NT_DIMS, preferred_element_type=F32)
    ones = jnp.ones((HEAD_PAD - V_HEAD, vt.shape[1]), F32)
    for h in range(N_HEADS):
        k_out[0, h] = (kk[:, HEAD_PAD * h:HEAD_PAD * (h + 1)] + kpe).astype(BF16)
        vt_out[0, h] = jnp.concatenate([vt[V_HEAD * h:V_HEAD * (h + 1)], ones], axis=0).astype(BF16)


def _ctx_kernel(c_ref, mod_ref, nm_ref, w_ref, b_ref, kvn_ref, wuk_ref, wuv_ref, k_out, v_out):
    h = _prenorm(c_ref[0], mod_ref, 0, nm_ref[...]).astype(BF16)
    a = _dot(h, w_ref[...]) + b_ref[...]
    _kv_heads(a[:, :KV_LORA], a[:, KV_LORA:], kvn_ref, wuk_ref, wuv_ref, k_out, v_out)


def _ctx_call(ctx, modc, norm_mix, w_c, b_c, kv_norm, w_uk, w_uv):
    bsz, n, d = ctx.shape
    return pl.pallas_call(
        _ctx_kernel,
        grid=(bsz,),
        in_specs=[pl.BlockSpec((1, n, d), lambda b: (b, 0, 0)), _full(modc.shape), _full(norm_mix.shape),
                  _full(w_c.shape), _full(b_c.shape), _full(kv_norm.shape), _full(w_uk.shape), _full(w_uv.shape)],
        out_specs=[pl.BlockSpec((1, N_HEADS, n, HEAD_PAD), lambda b: (b, 0, 0, 0)),
                   pl.BlockSpec((1, N_HEADS, HEAD_PAD, n), lambda b: (b, 0, 0, 0))],
        out_shape=[jax.ShapeDtypeStruct((bsz, N_HEADS, n, HEAD_PAD), BF16),
                   jax.ShapeDtypeStruct((bsz, N_HEADS, HEAD_PAD, n), BF16)],
        compiler_params=_params("arbitrary"),
        name="ctx",
    )(ctx, modc, norm_mix, w_c, b_c, kv_norm, w_uk, w_uv)


def _inproj_kernel(x_ref, xp_ref, xn_ref, mod_ref, nm_ref, wa_ref, ba_ref, why_ref, bhy_ref, wg_ref, bg_ref,
                   qn_ref, wuq_ref, wuqs_ref, kvn_ref, wuk_ref, wuv_ref, cos_ref, sin_ref, cw_ref, cb_ref,
                   q_out, k_out, v_out, hv_out, hx1_out, hx2_out, g_out):
    i = pl.program_id(0)
    tm = x_ref.shape[1]
    nm = nm_ref[...]
    h = _prenorm(x_ref[0], mod_ref, 0, nm).astype(BF16)
    a = _dot(h, wa_ref[...]) + ba_ref[...]
    q_lat = a[:, :Q_LORA]
    kv_lat = a[:, Q_LORA:Q_LORA + KV_LORA]
    kpe_m = a[:, Q_LORA + KV_LORA:Q_LORA + KV_LORA + HEAD_PAD]
    kpe_s = a[:, Q_LORA + KV_LORA + HEAD_PAD:]
    cos = cos_ref[...]
    sin = sin_ref[...]
    qn = _rms(q_lat, qn_ref[...]).astype(BF16)
    qa = _dot(qn, wuq_ref[...])
    qs = _dot(qn, wuqs_ref[...])
    for hh in range(N_HEADS):
        sl = slice(HEAD_PAD * hh, HEAD_PAD * (hh + 1))
        q_out[0, hh] = (qa[:, sl] * cos + qs[:, sl] * sin).astype(BF16)
    _kv_heads(kv_lat, kpe_m * cos + kpe_s * sin, kvn_ref, wuk_ref, wuv_ref, k_out, v_out)
    g_out[0] = (_dot(h, wg_ref[...]) + bg_ref[...]).astype(BF16)

    why = why_ref[...]
    bhy = bhy_ref[...]
    hy = _dot(h, why) + bhy
    hp = _dot(_prenorm(xp_ref[0], mod_ref, 0, nm).astype(BF16), why) + bhy
    hn = _dot(_prenorm(xn_ref[0], mod_ref, 0, nm).astype(BF16), why) + bhy
    prev = jnp.where(i == 0, 0.0, hp[SUBLANES - 1:SUBLANES])
    nxt = jnp.where(i == pl.num_programs(0) - 1, 0.0, hn[0:1])
    rid = lax.broadcasted_iota(jnp.int32, (tm, 1), 0)
    up = jnp.where(rid == 0, prev, pltpu.roll(hy, 1, 0))
    dn = jnp.where(rid == tm - 1, nxt, pltpu.roll(hy, tm - 1, 0))
    u = up * cw_ref[0:1, :] + hy * cw_ref[1:2, :] + dn * cw_ref[2:3, :] + cb_ref[...]
    hv_out[0] = u[:, :HY_WIDTH]
    hx1_out[0] = u[:, HY_WIDTH:2 * HY_WIDTH]
    hx2_out[0] = u[:, 2 * HY_WIDTH:]


def _inproj_call(x, mod, norm_mix, wa, ba, why, bhy, wg, bg, q_norm, wuq, wuqs, kv_norm, wuk, wuvt, cos_t, sin_t, cw,
                 cb, tm):
    bsz, s, d = x.shape
    nt = s // tm
    rb = tm // SUBLANES
    last_rb = s // SUBLANES - 1
    consts = [norm_mix, wa, ba, why, bhy, wg, bg, q_norm, wuq, wuqs, kv_norm, wuk, wuvt]
    in_specs = [
        pl.BlockSpec((1, tm, d), lambda i, b: (b, i, 0)),
        pl.BlockSpec((1, SUBLANES, d), lambda i, b: (b, jnp.maximum(i * rb - 1, 0), 0)),
        pl.BlockSpec((1, SUBLANES, d), lambda i, b: (b, jnp.minimum((i + 1) * rb, last_rb), 0)),
        pl.BlockSpec((1, SUBLANES, d), lambda i, b: (b, 0, 0)),
    ] + [_full(c.shape) for c in consts] + [
        pl.BlockSpec((tm, HEAD_PAD), lambda i, b: (i, 0)),
        pl.BlockSpec((tm, HEAD_PAD), lambda i, b: (i, 0)),
        _full(cw.shape), _full(cb.shape),
    ]
    hw = HY_WIDTH
    out_specs = [
        pl.BlockSpec((1, N_HEADS, tm, HEAD_PAD), lambda i, b: (b, 0, i, 0)),
        pl.BlockSpec((1, N_HEADS, tm, HEAD_PAD), lambda i, b: (b, 0, i, 0)),
        pl.BlockSpec((1, N_HEADS, HEAD_PAD, tm), lambda i, b: (b, 0, 0, i)),
        pl.BlockSpec((1, tm, hw), lambda i, b: (b, i, 0)),
        pl.BlockSpec((1, tm, hw), lambda i, b: (b, i, 0)),
        pl.BlockSpec((1, tm, hw), lambda i, b: (b, i, 0)),
        pl.BlockSpec((1, tm, 2 * d), lambda i, b: (b, i, 0)),
    ]
    out_shape = [
        jax.ShapeDtypeStruct((bsz, N_HEADS, s, HEAD_PAD), BF16),
        jax.ShapeDtypeStruct((bsz, N_HEADS, s, HEAD_PAD), BF16),
        jax.ShapeDtypeStruct((bsz, N_HEADS, HEAD_PAD, s), BF16),
        jax.ShapeDtypeStruct((bsz, s, hw), F32),
        jax.ShapeDtypeStruct((bsz, s, hw), F32),
        jax.ShapeDtypeStruct((bsz, s, hw), F32),
        jax.ShapeDtypeStruct((bsz, s, 2 * d), BF16),
    ]
    return pl.pallas_call(
        _inproj_kernel,
        grid=(nt, bsz),
        in_specs=in_specs,
        out_specs=out_specs,
        out_shape=out_shape,
        compiler_params=_params("arbitrary", "arbitrary"),
        name="inproj",
    )(x, x, x, mod, *consts, cos_t, sin_t, cw, cb)


def _attn_kernel(q_ref, k_ref, vt_ref, o_ref, m_sc, acc_sc):
    j = pl.program_id(2)

    @pl.when(j == 0)
    def _():
        m_sc[...] = jnp.full(m_sc.shape, -jnp.inf, F32)
        acc_sc[...] = jnp.zeros(acc_sc.shape, F32)

    def scores(h):
        return lax.dot_general(k_ref[0, h], q_ref[0, h], NT_DIMS, preferred_element_type=F32)

    st = scores(0)
    for h in range(N_HEADS):
        st_next = scores(h + 1) if h + 1 < N_HEADS else None
        m_prev = m_sc[h]
        m_new = jnp.maximum(m_prev, jnp.max(st, axis=0, keepdims=True))
        pt = jnp.exp2(st - m_new).astype(BF16)
        acc_sc[h] = jnp.exp2(m_prev - m_new) * acc_sc[h] + jnp.dot(vt_ref[0, h], pt, preferred_element_type=F32)
        m_sc[h] = m_new
        st = st_next

    @pl.when(j == pl.num_programs(2) - 1)
    def _():
        ot = jnp.concatenate([acc_sc[h, :V_HEAD] / acc_sc[h, V_HEAD:V_HEAD + 1] for h in range(N_HEADS)], axis=0)
        o_ref[0] = ot.T.astype(o_ref.dtype)


def _attn_call(q, k, vt, tq, tk):
    bsz, nh, s, dh = q.shape
    nk = k.shape[2]
    dv = nh * V_HEAD
    return pl.pallas_call(
        _attn_kernel,
        grid=(bsz, s // tq, nk // tk),
        in_specs=[
            pl.BlockSpec((1, nh, tq, dh), lambda b, i, j: (b, 0, i, 0)),
            pl.BlockSpec((1, nh, tk, dh), lambda b, i, j: (b, 0, j, 0)),
            pl.BlockSpec((1, nh, dh, tk), lambda b, i, j: (b, 0, 0, j)),
        ],
        out_specs=pl.BlockSpec((1, tq, dv), lambda b, i, j: (b, i, 0)),
        out_shape=jax.ShapeDtypeStruct((bsz, s, dv), BF16),
        scratch_shapes=[pltpu.VMEM((nh, 1, tq), F32), pltpu.VMEM((nh, dh, tq), F32)],
        compiler_params=_params("arbitrary", "arbitrary", "arbitrary"),
        name="attn",
    )(q, k, vt)


def _filter_kernel(emb_ref, w1_ref, b1_ref, w2_ref, b2_ref, w3_ref, fr_ref, dl_ref, full_out, asum_out, *, seq):
    r = pl.program_id(0)
    rb = emb_ref.shape[0]
    emb = emb_ref[...]
    fr = fr_ref[...]
    h = jnp.sin(fr * (_dot3(emb, w1_ref[...]) + b1_ref[...]))
    h = jnp.sin(fr * (_dot3(h, w2_ref[...]) + b2_ref[...]))
    k = _dot3(h, w3_ref[0]) * jnp.exp(-emb[:, 0:1] * dl_ref[...])
    row = r * rb + lax.broadcasted_iota(jnp.int32, (rb, 1), 0)
    k = jnp.where(row == seq, 0.0, k)
    full_out[...] = k

    @pl.when(r == 0)
    def _():
        asum_out[...] = jnp.zeros(asum_out.shape, F32)

    asum_out[...] += jnp.sum(jnp.abs(k), axis=0, keepdims=True)


def _filter_call(emb, w1, b1, w2, b2, w3sel, freq, deltas2, seq, rb):
    n2 = emb.shape[0]
    half_blocks = seq // rb
    width = w3sel.shape[2]
    return pl.pallas_call(
        functools.partial(_filter_kernel, seq=seq),
        grid=(n2 // rb,),
        in_specs=[pl.BlockSpec((rb, HY_EMB_PAD), lambda r: (r, 0)), _full(w1.shape), _full(b1.shape),
                  _full(w2.shape), _full(b2.shape),
                  pl.BlockSpec((1,) + w3sel.shape[1:], lambda r: (r // half_blocks, 0, 0)),
                  _full(freq.shape), _full(deltas2.shape)],
        out_specs=[pl.BlockSpec((rb, width), lambda r: (r, 0)), pl.BlockSpec((1, width), lambda r: (0, 0))],
        out_shape=[jax.ShapeDtypeStruct((n2, width), F32), jax.ShapeDtypeStruct((1, width), F32)],
        compiler_params=_params("arbitrary"),
        name="filt",
    )(emb, w1, b1, w2, b2, w3sel, freq, deltas2)


def _fa_kernel(u_ref, f_ref, a_out):
    two, _, hn, nb = u_ref.shape
    x = u_ref[...].reshape(two * hn, nb)
    a_out[0] = _dot(f_ref[...], x).astype(a_out.dtype)


def _fa_call(u4, fmat, nb):
    _, p, hn, width = u4.shape
    n = 2 * hn
    return pl.pallas_call(
        _fa_kernel,
        grid=(p, width // nb),
        in_specs=[pl.BlockSpec((2, 1, hn, nb), lambda q, g: (0, q, 0, g)), _full(fmat.shape)],
        out_specs=pl.BlockSpec((1, 2 * n, nb), lambda q, g: (q, 0, g)),
        out_shape=jax.ShapeDtypeStruct((p, 2 * n, width), BF16),
        compiler_params=_params("arbitrary", "arbitrary"),
        name="fa",
    )(u4, fmat)


def _fb_kernel(a_ref, g_ref, asum_ref, kf_out):
    two, _, n, c = a_ref.shape[1:]
    a = a_ref[0].reshape(two * n, c)
    x = _dot(g_ref[0], a) / (asum_ref[...] + 1e-6)
    kf_out[0] = x.reshape(two, n, c)


def _fb_call(a5, gmat, asum):
    _, _, n, _, c = a5.shape
    return pl.pallas_call(
        _fb_kernel,
        grid=(n,),
        in_specs=[pl.BlockSpec((1, 2, 1, n, c), lambda k: (0, 0, k, 0, 0)),
                  pl.BlockSpec((1, 2 * n, 2 * n), lambda k: (k, 0, 0)), _full(asum.shape)],
        out_specs=pl.BlockSpec((1, 2, n, c), lambda k: (k, 0, 0, 0)),
        out_shape=jax.ShapeDtypeStruct((n, 2, n, c), F32),
        compiler_params=_params("arbitrary"),
        name="fb",
    )(a5, gmat, asum)


def _mid_kernel(a_ref, g_ref, h_ref, kf_ref, b_out):
    two, _, n, c = a_ref.shape[1:]
    a = a_ref[0].reshape(two * n, c)
    x = _dot(g_ref[0], a)
    xr, xi = x[:n], x[n:]
    kr, ki = kf_ref[0, 0], kf_ref[0, 1]
    y = jnp.concatenate([xr * kr - xi * ki, xr * ki + xi * kr], axis=0)
    b_out[0] = _dot(h_ref[0], y).reshape(two, 1, n, c).astype(b_out.dtype)


def _mid_call(a5, gmat, hmat, kf, order):
    p, _, n, _, c = a5.shape
    return pl.pallas_call(
        _mid_kernel,
        grid=(n, p),
        in_specs=[pl.BlockSpec((1, 2, 1, n, c), lambda k, q: (q, 0, k, 0, 0)),
                  pl.BlockSpec((1, 2 * n, 2 * n), lambda k, q: (k, 0, 0)),
                  pl.BlockSpec((1, 2 * n, 2 * n), lambda k, q: (k, 0, 0)),
                  pl.BlockSpec((1, 2, n, c), lambda k, q: (k, 0, 0, order))],
        out_specs=pl.BlockSpec((1, 2, 1, n, c), lambda k, q: (q, 0, k, 0, 0)),
        out_shape=jax.ShapeDtypeStruct(a5.shape, BF16),
        compiler_params=_params("arbitrary", "arbitrary"),
        name="mid",
    )(a5, gmat, hmat, kf)


def _fc_kernel(b_ref, f_ref, u_ref, m_ref, skip_ref, o_out):
    two, _, hn, nb = u_ref.shape
    y = _dot(f_ref[...], b_ref[0]).reshape(two, 1, hn, nb)
    o_out[...] = (m_ref[...] * (y + u_ref[...] * skip_ref[...])).astype(o_out.dtype)


def _fc_call(b3, finv, u4, m4, skip_t, nb, out_dtype):
    _, p, hn, width = u4.shape
    n = 2 * hn
    blk = pl.BlockSpec((2, 1, hn, nb), lambda q, g: (0, q, 0, g))
    return pl.pallas_call(
        _fc_kernel,
        grid=(p, width // nb),
        in_specs=[pl.BlockSpec((1, 2 * n, nb), lambda q, g: (q, 0, g)), _full(finv.shape), blk, blk,
                  pl.BlockSpec((1, nb), lambda q, g: (0, g))],
        out_specs=blk,
        out_shape=jax.ShapeDtypeStruct(u4.shape, out_dtype),
        compiler_params=_params("arbitrary", "arbitrary"),
        name="fc",
    )(b3, finv, u4, m4, skip_t)


def _dft_tables(n):
    hn = n // 2
    k = np.arange(n)[:, None]
    ang = -2.0 * np.pi * (k * np.arange(n)[None, :] % n) / n
    fr, fi = np.cos(ang), np.sin(ang)
    f_data = np.block([[fr[:, :hn], -fi[:, :hn]], [fi[:, :hn], fr[:, :hn]]])
    f_filt = np.concatenate([fr, fi], axis=0)
    er, ei = fr[:hn], -fi[:hn]
    f_inv = np.block([[er, -ei], [ei, er]]) / float(n * n)
    k1 = jnp.arange(n, dtype=jnp.int32)[:, None, None]
    k2 = jnp.arange(n, dtype=jnp.int32)[None, :, None]
    m2 = jnp.arange(n, dtype=jnp.int32)[None, None, :]
    idx = (m2 * (k1 + n * k2)) % (n * n)
    ang2 = idx.astype(F32) * (-2.0 * math.pi / (n * n))
    gr, gi = jnp.cos(ang2), jnp.sin(ang2)
    g = jnp.concatenate([jnp.concatenate([gr, -gi], axis=2), jnp.concatenate([gi, gr], axis=2)], axis=1)
    h = jnp.swapaxes(g, 1, 2)
    return (jnp.asarray(f_data, BF16), jnp.asarray(f_filt, BF16), jnp.asarray(f_inv, BF16),
            g.astype(BF16), h.astype(BF16))


def _hyena_filter_tables(seq):
    t = jnp.linspace(0.0, 1.0, seq, dtype=F32)[:, None]
    w = 2.0 * math.pi * jnp.arange(seq, dtype=F32)[:, None] / seq
    f = jnp.linspace(1e-4, HY_BANDS - 1, HY_BANDS, dtype=F32)[None, :]
    emb = jnp.concatenate([t, jnp.cos(f * w), -jnp.sin(f * w)], axis=-1)
    pos = jnp.concatenate([jnp.arange(seq), jnp.array([0]), jnp.arange(seq - 1, 0, -1)])
    emb = jnp.pad(emb[pos], ((0, 0), (0, HY_EMB_PAD - HY_EMB)))
    deltas = jnp.abs(jnp.linspace(math.log(HY_DECAY_TARGET) / HY_SLOW_DECAY,
                                  math.log(HY_DECAY_TARGET) / HY_FAST_DECAY, HY_WIDTH, dtype=F32))
    return emb, jnp.tile(deltas, HY_ORDER)[None, :]


def _hyena(hv, hx1, hx2, w1, b1, w2, b2, w3, freq, skip, nb):
    bsz, seq, c = hv.shape
    n = int(round(math.sqrt(2 * seq)))
    assert n * n == 2 * seq and bsz % 2 == 0
    hn, p = n // 2, bsz // 2
    f_data, f_filt, f_inv, gmat, hmat = _dft_tables(n)

    emb, deltas2 = _hyena_filter_tables(seq)
    w1p = jnp.pad(w1, ((0, HY_EMB_PAD - HY_EMB), (0, 0)))
    w3r = w3.reshape(w3.shape[0], HY_ORDER, 2, c)
    w3sel = jnp.stack([w3r[:, :, 0, :].reshape(-1, HY_ORDER * c), w3r[:, :, 1, :].reshape(-1, HY_ORDER * c)])
    full, asum = _filter_call(emb, w1p, b1[None], w2, b2[None], w3sel, freq[None], deltas2, seq, min(512, seq))
    c2 = HY_ORDER * c
    fa = _fa_call(full.reshape(2, 1, hn, n * c2), f_filt, nb)
    kf = _fb_call(fa.reshape(1, 2, n, n, c2), gmat, asum)

    def view(t):
        return t.reshape(2, p, hn, n * c)

    def long_conv(u4, m4, order, out_dtype):
        a = _fa_call(u4, f_data, nb)
        bm = _mid_call(a.reshape(p, 2, n, n, c), gmat, hmat, kf, order)
        skip_t = jnp.tile(skip[order], n)[None, :]
        return _fc_call(bm.reshape(p, 2 * n, n * c), f_inv, u4, m4, skip_t, nb, out_dtype)

    z = long_conv(view(hv), view(hx1), 0, F32)
    out = long_conv(z, view(hx2), 1, BF16)
    return out.reshape(bsz, seq, c)


def _merge_kernel(x_ref, at_ref, hy_ref, g_ref, mod_ref, wba_ref, wbh_ref, wo_ref, o_ref):
    d = x_ref.shape[2]
    g = g_ref[0].astype(F32)
    y = (jax.nn.sigmoid(g[:, :d]) * _dot(at_ref[0], wba_ref[...])
         + jax.nn.sigmoid(g[:, d:]) * _dot(hy_ref[0], wbh_ref[...]))
    o_ref[0] = x_ref[0] + mod_ref[0, 2:3, :] * _dot(y, wo_ref[...])


def _merge_call(x, attn, hy, gate, mod, wba, wbh, wo, tm):
    bsz, s, d = x.shape

    def tok(w):
        return pl.BlockSpec((1, tm, w), lambda b, i: (b, i, 0))

    return pl.pallas_call(
        _merge_kernel,
        grid=(bsz, s // tm),
        in_specs=[tok(d), tok(attn.shape[2]), tok(hy.shape[2]), tok(2 * d),
                  pl.BlockSpec((1, SUBLANES, d), lambda b, i: (b, 0, 0)),
                  _full(wba.shape), _full(wbh.shape), _full(wo.shape)],
        out_specs=tok(d),
        out_shape=jax.ShapeDtypeStruct((bsz, s, d), F32),
        compiler_params=_params("arbitrary", "arbitrary"),
        name="merge",
    )(x, attn, hy, gate, mod, wba, wbh, wo)


def _route_kernel(xm_ref, mod_ref, nf_ref, wrt_ref, rb_ref, tri_ref, lt_ref, h2_out, w_out, s_out, cnt_out):
    tm = xm_ref.shape[1]
    ng, gs = N_GROUPS, GROUP_SIZE

    h2 = _prenorm(xm_ref[0], mod_ref, 3, nf_ref[...])
    h2_out[0] = h2.astype(h2_out.dtype)
    scores = jax.nn.sigmoid(_dot3(wrt_ref[...], h2, NT_DIMS))
    sel = scores + rb_ref[...]
    slabs = [sel[ng * j:ng * (j + 1)] for j in range(gs)]
    sc_slabs = [scores[ng * j:ng * (j + 1)] for j in range(gs)]

    top1 = jnp.full((ng, tm), -jnp.inf, F32)
    top2 = top1
    for x in slabs:
        top2 = jnp.maximum(top2, jnp.minimum(top1, x))
        top1 = jnp.maximum(top1, x)
    gscore = top1 + top2
    gid = lax.broadcasted_iota(jnp.int32, (ng, 1), 0)
    rank = jnp.zeros((ng, tm), jnp.int32)
    for g2 in range(ng):
        row = gscore[g2:g2 + 1]
        beats = (row > gscore) | ((row == gscore) & (g2 < gid))
        rank = rank + beats.astype(jnp.int32)
    gmask = rank < TOPK_GROUPS

    cand = [jnp.where(gmask, x, -jnp.inf) for x in slabs]
    eid = [gid * gs + j for j in range(gs)]
    chosen = []
    for _ in range(TOP_K):
        best = functools.reduce(jnp.maximum, cand)
        best = jnp.max(best, axis=0, keepdims=True)
        idx = functools.reduce(jnp.minimum, [jnp.where(cand[j] == best, eid[j], N_EXPERTS) for j in range(gs)])
        idx = jnp.min(idx, axis=0, keepdims=True)
        chosen.append(idx)
        cand = [jnp.where(eid[j] == idx, -jnp.inf, cand[j]) for j in range(gs)]

    hit = [[eid[j] == idx for j in range(gs)] for idx in chosen]
    mask = [functools.reduce(jnp.logical_or, [hit[k][j] for k in range(TOP_K)]) for j in range(gs)]
    maskf = jnp.concatenate([m.astype(F32) for m in mask], axis=0)
    before = jnp.dot(maskf.astype(BF16), tri_ref[...], preferred_element_type=F32)
    cnt = jnp.sum(maskf, axis=1, keepdims=True)
    cnt_out[0] = jnp.broadcast_to(cnt, cnt_out.shape[1:])
    units = jnp.broadcast_to(jnp.floor((cnt + (RUN_ALIGN - 1)) * (1.0 / RUN_ALIGN)), (N_EXPERTS, LANES))
    start = RUN_ALIGN * jnp.dot(lt_ref[...], units.astype(BF16), preferred_element_type=F32)[:, 0:1]
    before = before + start
    bslabs = [before[ng * j:ng * (j + 1)] for j in range(gs)]

    def pick(k, vals):
        tot = functools.reduce(jnp.add, [jnp.where(hit[k][j], vals[j], 0.0) for j in range(gs)])
        return jnp.sum(tot, axis=0, keepdims=True)

    wk = [pick(k, sc_slabs) for k in range(TOP_K)]
    wsum = functools.reduce(jnp.add, wk)
    w_out[...] = jnp.concatenate(wk, axis=0) / wsum * ROUTE_SCALE
    s_out[...] = jnp.concatenate([pick(k, bslabs) for k in range(TOP_K)], axis=0).astype(jnp.int32)


def _route_call(xm, mod, norm_ffn, wrt, rbias, lower, tm):
    bsz, s, d = xm.shape
    t = bsz * s
    nt = s // tm
    tri = (jnp.arange(tm)[:, None] < jnp.arange(tm)[None, :]).astype(BF16)
    tok = pl.BlockSpec((TOP_K, tm), lambda i: (0, i))
    return pl.pallas_call(
        _route_kernel,
        grid=(t // tm,),
        in_specs=[pl.BlockSpec((1, tm, d), lambda i: (i // nt, i % nt, 0)),
                  pl.BlockSpec((1, SUBLANES, d), lambda i: (i // nt, 0, 0)),
                  _full(norm_ffn.shape), _full(wrt.shape), _full(rbias.shape), _full(tri.shape),
                  _full(lower.shape)],
        out_specs=[pl.BlockSpec((1, tm, d), lambda i: (i // nt, i % nt, 0)), tok, tok,
                   pl.BlockSpec((1, N_EXPERTS, LANES), lambda i: (i, 0, 0))],
        out_shape=[jax.ShapeDtypeStruct((bsz, s, d), BF16), jax.ShapeDtypeStruct((TOP_K, t), F32),
                   jax.ShapeDtypeStruct((TOP_K, t), jnp.int32),
                   jax.ShapeDtypeStruct((t // tm, N_EXPERTS, LANES), F32)],
        compiler_params=_params("arbitrary"),
        name="route",
    )(xm, mod, norm_ffn, wrt, rbias, tri, lower)


def _pack(x):
    w = x.shape[1] // 2
    lo = lax.bitcast_convert_type(x[:, :w].astype(BF16).astype(F32), jnp.uint32)
    hi = lax.bitcast_convert_type(x[:, w:].astype(BF16).astype(F32), jnp.uint32)
    return hi | (lo >> 16)


def _unpack(u):
    lo = lax.bitcast_convert_type(u << 16, F32).astype(BF16)
    hi = lax.bitcast_convert_type(u & jnp.uint32(0xFFFF0000), F32).astype(BF16)
    return lo, hi


def _pow2_pieces(units, limit):
    bit = 1
    while bit * 2 <= limit:
        bit *= 2
    while bit:
        yield (units & bit) != 0, units & ~(2 * bit - 1), bit
        bit //= 2


def _run_copies(vm_ref, hbm_ref, sem, n8, vm_row, hbm_row, limit, to_hbm, act):
    for on, off, size in _pow2_pieces(n8, limit):
        @pl.when(on)
        def _():
            rows = RUN_ALIGN * size
            v = vm_ref.at[pl.ds(pl.multiple_of(vm_row + RUN_ALIGN * off, RUN_ALIGN), rows), :]
            h = hbm_ref.at[pl.ds(pl.multiple_of(hbm_row + RUN_ALIGN * off, RUN_ALIGN), rows), :]
            act(pltpu.make_async_copy(v, h, sem) if to_hbm else pltpu.make_async_copy(h, v, sem))


def _dispatch_kernel(n8_ref, ls_ref, gs_ref, ts_ref, t8_ref, nu_ref, slot_ref, h_ref, xs_out, srt, zbuf, sem):
    step = pl.program_id(0)
    tm = h_ref.shape[0]
    rows = srt.shape[0]
    rid = lax.broadcasted_iota(jnp.int32, (rows, 1), 0)
    sel = functools.reduce(jnp.logical_or, [rid == slot_ref[k:k + 1, :] for k in range(TOP_K)])
    srt[...] = _pack(jnp.dot(jnp.where(sel, 1.0, 0.0).astype(BF16), h_ref[...], preferred_element_type=F32))

    def runs(act):
        def body(e, c):
            i = step * N_EXPERTS + e
            _run_copies(srt, xs_out, sem, n8_ref[i], ls_ref[i], gs_ref[i], tm // RUN_ALIGN, True, act)
            return c
        lax.fori_loop(0, N_EXPERTS, body, 0)

    runs(lambda cp: cp.start())
    runs(lambda cp: cp.wait())

    @pl.when(step == pl.num_programs(0) - 1)
    def _():
        zbuf[...] = jnp.zeros(zbuf.shape, zbuf.dtype)
        nblk = xs_out.shape[0] // EXPERT_BLOCK

        def fill(act):
            def tails(e, c):
                _run_copies(zbuf, xs_out, sem, t8_ref[e], 0, ts_ref[e], EXPERT_BLOCK // RUN_ALIGN - 1, True, act)
                return c

            def blocks(b, c):
                act(pltpu.make_async_copy(
                    zbuf, xs_out.at[pl.ds(pl.multiple_of(b * EXPERT_BLOCK, EXPERT_BLOCK), EXPERT_BLOCK), :], sem))
                return c

            lax.fori_loop(0, N_EXPERTS, tails, 0)
            lax.fori_loop(nu_ref[0], nblk, blocks, 0)

        fill(lambda cp: cp.start())
        fill(lambda cp: cp.wait())


def _dispatch_call(tables, slot_kt, h2, nblk, tm):
    t, d = h2.shape
    lrows = TOP_K * tm + N_EXPERTS * RUN_ALIGN
    return pl.pallas_call(
        _dispatch_kernel,
        grid_spec=pltpu.PrefetchScalarGridSpec(
            num_scalar_prefetch=len(tables), grid=(t // tm,),
            in_specs=[pl.BlockSpec((TOP_K, tm), lambda i, *_: (0, i)), pl.BlockSpec((tm, d), lambda i, *_: (i, 0))],
            out_specs=pl.BlockSpec(memory_space=pl.ANY),
            scratch_shapes=[pltpu.VMEM((lrows, d // 2), jnp.uint32), pltpu.VMEM((EXPERT_BLOCK, d // 2), jnp.uint32),
                            pltpu.SemaphoreType.DMA(())]),
        out_shape=jax.ShapeDtypeStruct((nblk * EXPERT_BLOCK, d // 2), jnp.uint32),
        compiler_params=_params("arbitrary"),
        name="dispatch",
    )(*tables, slot_kt, h2)


def _expert_kernel(blk_ref, nused_ref, x_ref, wgu_ref, wd_ref, y_ref):
    used = pl.program_id(0) < nused_ref[0]

    @pl.when(used)
    def _():
        lo, hi = _unpack(x_ref[...])
        half = lo.shape[1]
        gu = (jnp.dot(lo, wgu_ref[0, :half, :], preferred_element_type=F32)
              + jnp.dot(hi, wgu_ref[0, half:, :], preferred_element_type=F32))
        a = _silu(gu[:, :EXPERT_FF]) * gu[:, EXPERT_FF:]
        y_ref[...] = _pack(_dot(a, wd_ref[0]))

    @pl.when(jnp.logical_not(used))
    def _():
        y_ref[...] = jnp.zeros(y_ref.shape, y_ref.dtype)


def _expert_call(blk_e, nused, xs, wgu, wd):
    rows, d = xs.shape
    nblk = rows // EXPERT_BLOCK

    def row_map(i, blk, nu):
        return (jnp.minimum(i, nu[0] - 1), 0)

    return pl.pallas_call(
        _expert_kernel,
        grid_spec=pltpu.PrefetchScalarGridSpec(
            num_scalar_prefetch=2, grid=(nblk,),
            in_specs=[pl.BlockSpec((EXPERT_BLOCK, d), row_map),
                      pl.BlockSpec((1,) + wgu.shape[1:], lambda i, blk, nu: (blk[i], 0, 0)),
                      pl.BlockSpec((1,) + wd.shape[1:], lambda i, blk, nu: (blk[i], 0, 0))],
            out_specs=pl.BlockSpec((EXPERT_BLOCK, d), lambda i, blk, nu: (i, 0))),
        out_shape=jax.ShapeDtypeStruct((rows, d), jnp.uint32),
        compiler_params=_params("arbitrary"),
        name="expert",
    )(blk_e, nused, xs, wgu, wd)


def _combine_kernel(n8_ref, ls_ref, gs_ref, ys_hbm, slot_ref, w_ref, xm_ref, h_ref, mod_ref, wsgu_ref, wsd_ref,
                    fn_ref, o_ref, ybuf, sem):
    step = pl.program_id(0)
    tm = xm_ref.shape[0]
    rows = ybuf.shape[0]

    def runs(act):
        def body(e, c):
            i = step * N_EXPERTS + e
            _run_copies(ybuf, ys_hbm, sem, n8_ref[i], ls_ref[i], gs_ref[i], tm // RUN_ALIGN, False, act)
            return c
        lax.fori_loop(0, N_EXPERTS, body, 0)

    runs(lambda cp: cp.start())
    gu = _dot(h_ref[...], wsgu_ref[...])
    ff = gu.shape[1] // 2
    shared = _dot(_silu(gu[:, :ff]) * gu[:, ff:], wsd_ref[...])
    cid = lax.broadcasted_iota(jnp.int32, (1, rows), 1)
    slot = slot_ref[...]
    w = w_ref[...]
    mix = functools.reduce(jnp.add, [jnp.where(slot[:, k:k + 1] == cid, w[:, k:k + 1], 0.0) for k in range(TOP_K)])
    mix = mix.astype(BF16)
    runs(lambda cp: cp.wait())
    last = step * N_EXPERTS + N_EXPERTS - 1
    filled = ls_ref[last] + RUN_ALIGN * n8_ref[last]
    rid = lax.broadcasted_iota(jnp.int32, (rows, 1), 0)
    lo, hi = _unpack(jnp.where(rid < filled, ybuf[...], jnp.uint32(0)))
    routed = jnp.concatenate([jnp.dot(mix, lo, preferred_element_type=F32),
                              jnp.dot(mix, hi, preferred_element_type=F32)], axis=1)
    x = xm_ref[...] + mod_ref[0, 5:6, :] * (routed + shared)
    o_ref[...] = _rms(x, fn_ref[...])


def _combine_call(tables, ys, slot_tk, w_tk, xm, h2, mod, wsgu, wsd, final_norm, tm, tiles_per_batch):
    t, d = xm.shape
    lrows = TOP_K * tm + N_EXPERTS * RUN_ALIGN
    tok = pl.BlockSpec((tm, d), lambda i, *_: (i, 0))
    per_k = pl.BlockSpec((tm, TOP_K), lambda i, *_: (i, 0))
    return pl.pallas_call(
        _combine_kernel,
        grid_spec=pltpu.PrefetchScalarGridSpec(
            num_scalar_prefetch=len(tables), grid=(t // tm,),
            in_specs=[pl.BlockSpec(memory_space=pl.ANY), per_k, per_k, tok, tok,
                      pl.BlockSpec((1, SUBLANES, d), lambda i, *_: (i // tiles_per_batch, 0, 0)),
                      _full(wsgu.shape), _full(wsd.shape), _full(final_norm.shape)],
            out_specs=tok,
            scratch_shapes=[pltpu.VMEM((lrows, d // 2), jnp.uint32), pltpu.SemaphoreType.DMA(())]),
        out_shape=jax.ShapeDtypeStruct((t, d), F32),
        compiler_params=_params("arbitrary"),
        name="combine",
    )(*tables, ys, slot_tk, w_tk, xm, h2, mod, wsgu, wsd, final_norm)


def _moe(xm, mod, norm_ffn, w_router, router_bias, wg, wu, wd, wsg, wsu, wsd, final_norm, tm):
    bsz, s, d = xm.shape
    t = bsz * s
    nt = t // tm
    perm = (np.arange(N_EXPERTS) % N_GROUPS) * GROUP_SIZE + np.arange(N_EXPERTS) // N_GROUPS
    wrt = w_router.T[perm]
    rbias = router_bias[perm][:, None]
    lower = jnp.asarray(perm[None, :] < perm[:, None], BF16)
    h2, w_kt, slot_kt, cnt = _route_call(xm, mod, norm_ffn, wrt, rbias, lower, tm)

    inv = np.argsort(perm)
    n8 = (cnt[:, :, 0].astype(jnp.int32)[:, inv] + (RUN_ALIGN - 1)) // RUN_ALIGN
    run = RUN_ALIGN * n8
    ls = jnp.cumsum(run, axis=1) - run
    tot = jnp.sum(run, axis=0)
    padded = (tot + EXPERT_BLOCK - 1) // EXPERT_BLOCK * EXPERT_BLOCK
    pad_end = jnp.cumsum(padded)
    gs = (pad_end - padded)[None, :] + jnp.cumsum(run, axis=0) - run
    nblk = -(-(t * TOP_K + nt * N_EXPERTS * (RUN_ALIGN - 1)) // EXPERT_BLOCK) + N_EXPERTS
    blk_first = jnp.arange(nblk, dtype=jnp.int32)[:, None] * EXPERT_BLOCK
    blk_e = jnp.minimum(jnp.sum((pad_end[None, :] <= blk_first).astype(jnp.int32), axis=1), N_EXPERTS - 1)
    nused = (pad_end[-1:] // EXPERT_BLOCK).astype(jnp.int32)
    tables = [a.reshape(-1).astype(jnp.int32) for a in (n8, ls, gs)]
    tails = [(pad_end - padded + tot).astype(jnp.int32), ((padded - tot) // RUN_ALIGN).astype(jnp.int32), nused]

    h2f = h2.reshape(t, d)
    xs = _dispatch_call(tables + tails, slot_kt, h2f, nblk, tm)
    wgu = jnp.concatenate([wg, wu], axis=2).astype(BF16)
    ys = _expert_call(blk_e, nused, xs, wgu, wd.astype(BF16))
    wsgu = jnp.concatenate([wsg, wsu], axis=1).astype(BF16)
    out = _combine_call(tables, ys, slot_kt.T, w_kt.T, xm.reshape(t, d), h2f, mod, wsgu, wsd.astype(BF16),
                        final_norm, tm, s // tm)
    return out.reshape(bsz, s, d)


def _rope_tables(s):
    rows = s // GRID_W
    row = jnp.broadcast_to(jnp.arange(rows, dtype=F32)[:, None], (rows, GRID_W)).reshape(-1)
    col = jnp.broadcast_to(jnp.arange(GRID_W, dtype=F32)[None, :], (rows, GRID_W)).reshape(-1)
    half = QK_ROPE // 2
    inv_freq = ROPE_THETA ** (-jnp.arange(0, half, 2, dtype=F32) / half)
    ar, ac = row[:, None] * inv_freq, col[:, None] * inv_freq
    ones = jnp.ones((s, QK_NOPE), F32)
    tail = HEAD_PAD - QK_NOPE - QK_ROPE
    cos_t = jnp.concatenate([ones, jnp.cos(ar), jnp.cos(ar), jnp.cos(ac), jnp.cos(ac), jnp.ones((s, tail), F32)], 1)
    sin_t = jnp.concatenate([0 * ones, -jnp.sin(ar), jnp.sin(ar), -jnp.sin(ac), jnp.sin(ac),
                             jnp.zeros((s, tail), F32)], 1)
    return cos_t, sin_t


_Q4 = QK_ROPE // 4
ROPE_SWAP = np.concatenate([np.arange(_Q4, 2 * _Q4), np.arange(0, _Q4), np.arange(3 * _Q4, 4 * _Q4),
                            np.arange(2 * _Q4, 3 * _Q4)])


def _rope_slot(w, swap):
    if swap:
        w = w[..., ROPE_SWAP]
    pad = [(0, 0)] * (w.ndim - 1) + [(QK_NOPE, HEAD_PAD - QK_NOPE - QK_ROPE)]
    return jnp.pad(w, pad)


def kernel(x, c, ctx, c_ctx, w_mod, b_mod, norm_mix, norm_ffn, w_in, b_in, q_norm, w_uq, kv_norm, w_ukv, w_branch_attn, hy_conv_w, hy_conv_b, hy_filt_w1, hy_filt_b1, hy_filt_w2, hy_filt_b2, hy_filt_w3, hy_filt_freq, hy_skip, w_branch_hyena, w_out, w_router, router_bias, w_exp_gate, w_exp_up, w_exp_down, w_sh_gate, w_sh_up, w_sh_down, final_norm,
           tiles=None):
    bsz, s, d = x.shape
    tl = dict(inproj=256, tq=512, tk=1408, fft_nb=2048, merge=512, moe=256)
    tl.update(tiles or {})
    assert w_mod.shape[0] == 1, "single-layer trunk"
    i = 0

    rows = -(-(bsz + 1) // SUBLANES) * SUBLANES
    c_rows = jnp.pad(jnp.concatenate([c, c_ctx[None]], axis=0), ((0, rows - bsz - 1), (0, 0)))
    mod_all = _mod_call(c_rows, w_mod[i], b_mod[i])
    mod_all = jnp.pad(mod_all.reshape(rows, 6, d), ((0, 0), (0, SUBLANES - 6), (0, 0)))
    mod, modc = mod_all[:bsz], mod_all[bsz:bsz + 1]

    cuts = np.cumsum([Q_LORA, KV_LORA, QK_ROPE, 3 * HY_WIDTH])
    wi, bi = w_in[i], b_in[i][None]
    w_q, w_kv, w_pe, w_hy, w_g = jnp.split(wi, cuts, axis=1)
    b_q, b_kv, b_pe, b_hy, b_g = jnp.split(bi, cuts, axis=1)
    wa = jnp.concatenate([w_q, w_kv, _rope_slot(w_pe, False), _rope_slot(w_pe, True)], axis=1).astype(BF16)
    ba = jnp.concatenate([b_q, b_kv, _rope_slot(b_pe, False), _rope_slot(b_pe, True)], axis=1)
    wq3 = w_uq[i].reshape(Q_LORA, N_HEADS, QK_NOPE + QK_ROPE) * (ATTN_SCALE * math.log2(math.e))
    tail = ((0, 0), (0, 0), (0, HEAD_PAD - QK_NOPE))
    wuq = (jnp.pad(wq3[..., :QK_NOPE], tail) + _rope_slot(wq3[..., QK_NOPE:], False)).reshape(Q_LORA, -1).astype(BF16)
    wuqs = _rope_slot(wq3[..., QK_NOPE:], True).reshape(Q_LORA, -1).astype(BF16)
    wkv3 = w_ukv[i].reshape(KV_LORA, N_HEADS, QK_NOPE + V_HEAD)
    wuk = jnp.pad(wkv3[..., :QK_NOPE], tail).reshape(KV_LORA, -1).astype(BF16)
    wuvt = wkv3[..., QK_NOPE:].reshape(KV_LORA, -1).T.astype(BF16)
    nm, qn, kvn = norm_mix[i][None], q_norm[i][None], kv_norm[i][None]

    w_c = jnp.concatenate([w_kv, _rope_slot(w_pe, False)], axis=1).astype(BF16)
    b_c = jnp.concatenate([b_kv, _rope_slot(b_pe, False)], axis=1)
    ck, cvt = _ctx_call(ctx, modc, nm, w_c, b_c, kvn, wuk, wuvt)

    cos_t, sin_t = _rope_tables(s)
    q, k, vt, hv, hx1, hx2, gate = _inproj_call(
        x, mod, nm, wa, ba, w_hy.astype(BF16), b_hy, w_g.astype(BF16), b_g, qn, wuq, wuqs, kvn, wuk, wuvt,
        cos_t, sin_t, hy_conv_w[i], hy_conv_b[i][None], tl["inproj"])

    attn = _attn_call(q, jnp.concatenate([ck, k], axis=2), jnp.concatenate([cvt, vt], axis=3), tl["tq"], tl["tk"])
    hy = _hyena(hv, hx1, hx2, hy_filt_w1[i], hy_filt_b1[i], hy_filt_w2[i], hy_filt_b2[i], hy_filt_w3[i],
                hy_filt_freq[i], hy_skip[i], tl["fft_nb"])
    xm = _merge_call(x, attn, hy, gate, mod, w_branch_attn[i].astype(BF16), w_branch_hyena[i].astype(BF16),
                     w_out[i].astype(BF16), tl["merge"])
    return _moe(xm, mod, norm_ffn[i][None], w_router[i], router_bias[i], w_exp_gate[i], w_exp_up[i], w_exp_down[i],
                w_sh_gate[i], w_sh_up[i], w_sh_down[i], final_norm[None], tl["moe"])
```

```python
import functools
import math

import numpy as np
import jax
import jax.numpy as jnp
from jax import lax
from jax.experimental import pallas as pl
from jax.experimental.pallas import tpu as pltpu

GRID_W = 64
N_HEADS = 8
QK_NOPE = 64
QK_ROPE = 32
V_HEAD = 64
Q_LORA = 256
KV_LORA = 128
ROPE_THETA = 10000.0
ATTN_SCALE = 1.0 / math.sqrt(QK_NOPE + QK_ROPE)
HY_WIDTH = 512
HY_ORDER = 2
HY_SHORT = 3
HY_BANDS = 8
HY_EMB = 1 + 2 * HY_BANDS
HY_EMB_PAD = 32
HY_FAST_DECAY = 0.3
HY_SLOW_DECAY = 1.5
HY_DECAY_TARGET = 1e-2
N_EXPERTS = 64
N_GROUPS = 8
GROUP_SIZE = N_EXPERTS // N_GROUPS
TOPK_GROUPS = 4
TOP_K = 8
EXPERT_FF = 256
ROUTE_SCALE = 2.5
EXPERT_BLOCK = 512
RUN_ALIGN = 8
NORM_EPS = 1e-6

HEAD_PAD = 128
LANES = 128
SUBLANES = 8
VMEM_LIMIT = 48 * 1024 * 1024

F32 = jnp.float32
BF16 = jnp.bfloat16
NT_DIMS = (((1,), (1,)), ((), ()))
NN_DIMS = (((1,), (0,)), ((), ()))


def _params(*sem):
    return pltpu.CompilerParams(dimension_semantics=sem, vmem_limit_bytes=VMEM_LIMIT)


def _dot(a, b):
    return jnp.dot(a.astype(BF16), b.astype(BF16), preferred_element_type=F32)


def _split(a):
    hi = a.astype(BF16)
    lo = (a - hi.astype(F32)).astype(BF16)
    return hi, lo


def _dot3(a, b, dims=NN_DIMS):
    ah, al = _split(a)
    bh, bl = _split(b)
    d = functools.partial(lax.dot_general, dimension_numbers=dims, preferred_element_type=F32)
    return d(ah, bh) + (d(ah, bl) + d(al, bh))


def _rms(x, g):
    return x * lax.rsqrt(jnp.mean(x * x, axis=-1, keepdims=True) + NORM_EPS) * g


def _silu(x):
    return x * jax.nn.sigmoid(x)


def _full(shape):
    nd = len(shape)
    return pl.BlockSpec(shape, lambda *_: (0,) * nd)


def _mod_kernel(c_ref, w_ref, b_ref, o_ref):
    o_ref[...] = _dot3(_silu(c_ref[...]), w_ref[...]) + b_ref[...]


def _mod_call(c_rows, w_mod, b_mod):
    r, d = c_rows.shape
    n = w_mod.shape[1]
    bn = 1024
    return pl.pallas_call(
        _mod_kernel,
        grid=(n // bn,),
        in_specs=[_full((r, d)), pl.BlockSpec((d, bn), lambda j: (0, j)), pl.BlockSpec((1, bn), lambda j: (0, j))],
        out_specs=pl.BlockSpec((r, bn), lambda j: (0, j)),
        out_shape=jax.ShapeDtypeStruct((r, n), F32),
        compiler_params=_params("arbitrary"),
        name="mod",
    )(c_rows, w_mod, b_mod.reshape(1, n))


def _prenorm(x, mod_ref, row, g):
    shift = mod_ref[0, row:row + 1, :]
    scale = mod_ref[0, row + 1:row + 2, :]
    return _rms(x, g) * (1.0 + scale) + shift


def _kv_heads(kv_lat, kpe, kvn_ref, wuk_ref, wuvt_ref, k_out, vt_out):
    kvn = _rms(kv_lat, kvn_ref[...]).astype(BF16)
    kk = _dot(kvn, wuk_ref[...])
    vt = lax.dot_general(wuvt_ref[...], kvn, NT_DIMS, preferred_element_type=F32)
    ones = jnp.ones((HEAD_PAD - V_HEAD, vt.shape[1]), F32)
    for h in range(N_HEADS):
        k_out[0, h] = (kk[:, HEAD_PAD * h:HEAD_PAD * (h + 1)] + kpe).astype(BF16)
        vt_out[0, h] = jnp.concatenate([vt[V_HEAD * h:V_HEAD * (h + 1)], ones], axis=0).astype(BF16)


def _ctx_kernel(c_ref, mod_ref, nm_ref, w_ref, b_ref, kvn_ref, wuk_ref, wuv_ref, k_out, v_out):
    h = _prenorm(c_ref[0], mod_ref, 0, nm_ref[...]).astype(BF16)
    a = _dot(h, w_ref[...]) + b_ref[...]
    _kv_heads(a[:, :KV_LORA], a[:, KV_LORA:], kvn_ref, wuk_ref, wuv_ref, k_out, v_out)


def _ctx_call(ctx, modc, norm_mix, w_c, b_c, kv_norm, w_uk, w_uv):
    bsz, n, d = ctx.shape
    return pl.pallas_call(
        _ctx_kernel,
        grid=(bsz,),
        in_specs=[pl.BlockSpec((1, n, d), lambda b: (b, 0, 0)), _full(modc.shape), _full(norm_mix.shape),
                  _full(w_c.shape), _full(b_c.shape), _full(kv_norm.shape), _full(w_uk.shape), _full(w_uv.shape)],
        out_specs=[pl.BlockSpec((1, N_HEADS, n, HEAD_PAD), lambda b: (b, 0, 0, 0)),
                   pl.BlockSpec((1, N_HEADS, HEAD_PAD, n), lambda b: (b, 0, 0, 0))],
        out_shape=[jax.ShapeDtypeStruct((bsz, N_HEADS, n, HEAD_PAD), BF16),
                   jax.ShapeDtypeStruct((bsz, N_HEADS, HEAD_PAD, n), BF16)],
        compiler_params=_params("arbitrary"),
        name="ctx",
    )(ctx, modc, norm_mix, w_c, b_c, kv_norm, w_uk, w_uv)


def _inproj_kernel(x_ref, xp_ref, xn_ref, mod_ref, nm_ref, wa_ref, ba_ref, why_ref, bhy_ref, wg_ref, bg_ref,
                   qn_ref, wuq_ref, wuqs_ref, kvn_ref, wuk_ref, wuv_ref, cos_ref, sin_ref, cw_ref, cb_ref,
                   q_out, k_out, v_out, hv_out, hx1_out, hx2_out, g_out):
    i = pl.program_id(0)
    tm = x_ref.shape[1]
    nm = nm_ref[...]
    h = _prenorm(x_ref[0], mod_ref, 0, nm).astype(BF16)
    a = _dot(h, wa_ref[...]) + ba_ref[...]
    q_lat = a[:, :Q_LORA]
    kv_lat = a[:, Q_LORA:Q_LORA + KV_LORA]
    kpe_m = a[:, Q_LORA + KV_LORA:Q_LORA + KV_LORA + HEAD_PAD]
    kpe_s = a[:, Q_LORA + KV_LORA + HEAD_PAD:]
    cos = cos_ref[...]
    sin = sin_ref[...]
    qn = _rms(q_lat, qn_ref[...]).astype(BF16)
    qa = _dot(qn, wuq_ref[...])
    qs = _dot(qn, wuqs_ref[...])
    for hh in range(N_HEADS):
        sl = slice(HEAD_PAD * hh, HEAD_PAD * (hh + 1))
        q_out[0, hh] = (qa[:, sl] * cos + qs[:, sl] * sin).astype(BF16)
    _kv_heads(kv_lat, kpe_m * cos + kpe_s * sin, kvn_ref, wuk_ref, wuv_ref, k_out, v_out)
    g_out[0] = (_dot(h, wg_ref[...]) + bg_ref[...]).astype(BF16)

    why = why_ref[...]
    bhy = bhy_ref[...]
    hy = _dot(h, why) + bhy
    hp = _dot(_prenorm(xp_ref[0], mod_ref, 0, nm).astype(BF16), why) + bhy
    hn = _dot(_prenorm(xn_ref[0], mod_ref, 0, nm).astype(BF16), why) + bhy
    prev = jnp.where(i == 0, 0.0, hp[SUBLANES - 1:SUBLANES])
    nxt = jnp.where(i == pl.num_programs(0) - 1, 0.0, hn[0:1])
    rid = lax.broadcasted_iota(jnp.int32, (tm, 1), 0)
    up = jnp.where(rid == 0, prev, pltpu.roll(hy, 1, 0))
    dn = jnp.where(rid == tm - 1, nxt, pltpu.roll(hy, tm - 1, 0))
    u = up * cw_ref[0:1, :] + hy * cw_ref[1:2, :] + dn * cw_ref[2:3, :] + cb_ref[...]
    hv_out[0] = u[:, :HY_WIDTH]
    hx1_out[0] = u[:, HY_WIDTH:2 * HY_WIDTH]
    hx2_out[0] = u[:, 2 * HY_WIDTH:]


def _inproj_call(x, mod, norm_mix, wa, ba, why, bhy, wg, bg, q_norm, wuq, wuqs, kv_norm, wuk, wuvt, cos_t, sin_t, cw,
                 cb, tm):
    bsz, s, d = x.shape
    nt = s // tm
    rb = tm // SUBLANES
    last_rb = s // SUBLANES - 1
    consts = [norm_mix, wa, ba, why, bhy, wg, bg, q_norm, wuq, wuqs, kv_norm, wuk, wuvt]
    in_specs = [
        pl.BlockSpec((1, tm, d), lambda i, b: (b, i, 0)),
        pl.BlockSpec((1, SUBLANES, d), lambda i, b: (b, jnp.maximum(i * rb - 1, 0), 0)),
        pl.BlockSpec((1, SUBLANES, d), lambda i, b: (b, jnp.minimum((i + 1) * rb, last_rb), 0)),
        pl.BlockSpec((1, SUBLANES, d), lambda i, b: (b, 0, 0)),
    ] + [_full(c.shape) for c in consts] + [
        pl.BlockSpec((tm, HEAD_PAD), lambda i, b: (i, 0)),
        pl.BlockSpec((tm, HEAD_PAD), lambda i, b: (i, 0)),
        _full(cw.shape), _full(cb.shape),
    ]
    hw = HY_WIDTH
    out_specs = [
        pl.BlockSpec((1, N_HEADS, tm, HEAD_PAD), lambda i, b: (b, 0, i, 0)),
        pl.BlockSpec((1, N_HEADS, tm, HEAD_PAD), lambda i, b: (b, 0, i, 0)),
        pl.BlockSpec((1, N_HEADS, HEAD_PAD, tm), lambda i, b: (b, 0, 0, i)),
        pl.BlockSpec((1, tm, hw), lambda i, b: (b, i, 0)),
        pl.BlockSpec((1, tm, hw), lambda i, b: (b, i, 0)),
        pl.BlockSpec((1, tm, hw), lambda i, b: (b, i, 0)),
        pl.BlockSpec((1, tm, 2 * d), lambda i, b: (b, i, 0)),
    ]
    out_shape = [
        jax.ShapeDtypeStruct((bsz, N_HEADS, s, HEAD_PAD), BF16),
        jax.ShapeDtypeStruct((bsz, N_HEADS, s, HEAD_PAD), BF16),
        jax.ShapeDtypeStruct((bsz, N_HEADS, HEAD_PAD, s), BF16),
        jax.ShapeDtypeStruct((bsz, s, hw), F32),
        jax.ShapeDtypeStruct((bsz, s, hw), F32),
        jax.ShapeDtypeStruct((bsz, s, hw), F32),
        jax.ShapeDtypeStruct((bsz, s, 2 * d), BF16),
    ]
    return pl.pallas_call(
        _inproj_kernel,
        grid=(nt, bsz),
        in_specs=in_specs,
        out_specs=out_specs,
        out_shape=out_shape,
        compiler_params=_params("arbitrary", "arbitrary"),
        name="inproj",
    )(x, x, x, mod, *consts, cos_t, sin_t, cw, cb)


def _attn_kernel(q_ref, k_ref, vt_ref, o_ref, m_sc, acc_sc):
    j = pl.program_id(2)

    @pl.when(j == 0)
    def _():
        m_sc[...] = jnp.full(m_sc.shape, -jnp.inf, F32)
        acc_sc[...] = jnp.zeros(acc_sc.shape, F32)

    def scores(h):
        return lax.dot_general(k_ref[0, h], q_ref[0, h], NT_DIMS, preferred_element_type=F32)

    st = scores(0)
    for h in range(N_HEADS):
        st_next = scores(h + 1) if h + 1 < N_HEADS else None
        m_prev = m_sc[h]
        m_new = jnp.maximum(m_prev, jnp.max(st, axis=0, keepdims=True))
        pt = jnp.exp2(st - m_new).astype(BF16)
        acc_sc[h] = jnp.exp2(m_prev - m_new) * acc_sc[h] + jnp.dot(vt_ref[0, h], pt, preferred_element_type=F32)
        m_sc[h] = m_new
        st = st_next

    @pl.when(j == pl.num_programs(2) - 1)
    def _():
        ot = jnp.concatenate([acc_sc[h, :V_HEAD] / acc_sc[h, V_HEAD:V_HEAD + 1] for h in range(N_HEADS)], axis=0)
        o_ref[0] = ot.T.astype(o_ref.dtype)


def _attn_call(q, k, vt, tq, tk):
    bsz, nh, s, dh = q.shape
    nk = k.shape[2]
    dv = nh * V_HEAD
    return pl.pallas_call(
        _attn_kernel,
        grid=(bsz, s // tq, nk // tk),
        in_specs=[
            pl.BlockSpec((1, nh, tq, dh), lambda b, i, j: (b, 0, i, 0)),
            pl.BlockSpec((1, nh, tk, dh), lambda b, i, j: (b, 0, j, 0)),
            pl.BlockSpec((1, nh, dh, tk), lambda b, i, j: (b, 0, 0, j)),
        ],
        out_specs=pl.BlockSpec((1, tq, dv), lambda b, i, j: (b, i, 0)),
        out_shape=jax.ShapeDtypeStruct((bsz, s, dv), BF16),
        scratch_shapes=[pltpu.VMEM((nh, 1, tq), F32), pltpu.VMEM((nh, dh, tq), F32)],
        compiler_params=_params("arbitrary", "arbitrary", "arbitrary"),
        name="attn",
    )(q, k, vt)


def _filter_kernel(emb_ref, w1_ref, b1_ref, w2_ref, b2_ref, w3_ref, fr_ref, dl_ref, full_out, asum_out, *, seq):
    r = pl.program_id(0)
    rb = emb_ref.shape[0]
    emb = emb_ref[...]
    fr = fr_ref[...]
    h = jnp.sin(fr * (_dot3(emb, w1_ref[...]) + b1_ref[...]))
    h = jnp.sin(fr * (_dot3(h, w2_ref[...]) + b2_ref[...]))
    k = _dot3(h, w3_ref[0]) * jnp.exp(-emb[:, 0:1] * dl_ref[...])
    row = r * rb + lax.broadcasted_iota(jnp.int32, (rb, 1), 0)
    k = jnp.where(row == seq, 0.0, k)
    full_out[...] = k

    @pl.when(r == 0)
    def _():
        asum_out[...] = jnp.zeros(asum_out.shape, F32)

    asum_out[...] += jnp.sum(jnp.abs(k), axis=0, keepdims=True)


def _filter_call(emb, w1, b1, w2, b2, w3sel, freq, deltas2, seq, rb):
    n2 = emb.shape[0]
    half_blocks = seq // rb
    width = w3sel.shape[2]
    return pl.pallas_call(
        functools.partial(_filter_kernel, seq=seq),
        grid=(n2 // rb,),
        in_specs=[pl.BlockSpec((rb, HY_EMB_PAD), lambda r: (r, 0)), _full(w1.shape), _full(b1.shape),
                  _full(w2.shape), _full(b2.shape),
                  pl.BlockSpec((1,) + w3sel.shape[1:], lambda r: (r // half_blocks, 0, 0)),
                  _full(freq.shape), _full(deltas2.shape)],
        out_specs=[pl.BlockSpec((rb, width), lambda r: (r, 0)), pl.BlockSpec((1, width), lambda r: (0, 0))],
        out_shape=[jax.ShapeDtypeStruct((n2, width), F32), jax.ShapeDtypeStruct((1, width), F32)],
        compiler_params=_params("arbitrary"),
        name="filt",
    )(emb, w1, b1, w2, b2, w3sel, freq, deltas2)


def _fa_kernel(u_ref, f_ref, a_out):
    two, _, hn, g, c = u_ref.shape
    f = f_ref[...]
    for j in range(g):
        x = u_ref[:, 0, :, j, :].reshape(two * hn, c)
        a_out[0, :, :, j, :] = _dot(f, x).reshape(two, f.shape[0] // two, c)


def _fa_call(u5, fmat):
    _, p, hn, n, c = u5.shape
    g = SUBLANES
    return pl.pallas_call(
        _fa_kernel,
        grid=(p, n // g),
        in_specs=[pl.BlockSpec((2, 1, hn, g, c), lambda q, j: (0, q, 0, j, 0)), _full(fmat.shape)],
        out_specs=pl.BlockSpec((1, 2, n, g, c), lambda q, j: (q, 0, 0, j, 0)),
        out_shape=jax.ShapeDtypeStruct((p, 2, n, n, c), F32),
        compiler_params=_params("arbitrary", "arbitrary"),
        name="fa",
    )(u5, fmat)


def _fb_kernel(a_ref, g_ref, asum_ref, kf_out):
    two, _, n, c = a_ref.shape[1:]
    a = a_ref[0].reshape(two * n, c)
    x = _dot(g_ref[0], a) / (asum_ref[...] + 1e-6)
    kf_out[0] = x.reshape(two, n, c)


def _fb_call(a5, gmat, asum):
    _, _, n, _, c = a5.shape
    return pl.pallas_call(
        _fb_kernel,
        grid=(n,),
        in_specs=[pl.BlockSpec((1, 2, 1, n, c), lambda k: (0, 0, k, 0, 0)),
                  pl.BlockSpec((1, 2 * n, 2 * n), lambda k: (k, 0, 0)), _full(asum.shape)],
        out_specs=pl.BlockSpec((1, 2, n, c), lambda k: (k, 0, 0, 0)),
        out_shape=jax.ShapeDtypeStruct((n, 2, n, c), F32),
        compiler_params=_params("arbitrary"),
        name="fb",
    )(a5, gmat, asum)


def _mid_kernel(a_ref, g_ref, h_ref, kf_ref, b_out):
    _, two, kb, n, c = a_ref.shape
    for kk in range(kb):
        x = _dot(g_ref[kk], a_ref[0, :, kk].reshape(two * n, c))
        xr, xi = x[:n], x[n:]
        kr, ki = kf_ref[kk, 0], kf_ref[kk, 1]
        y = jnp.concatenate([xr * kr - xi * ki, xr * ki + xi * kr], axis=0)
        b_out[0, :, kk] = _dot(h_ref[kk], y).reshape(two, n, c)


def _mid_call(a5, gmat, hmat, kf, order, kb):
    p, _, n, _, c = a5.shape
    return pl.pallas_call(
        _mid_kernel,
        grid=(n // kb, p),
        in_specs=[pl.BlockSpec((1, 2, kb, n, c), lambda k, q: (q, 0, k, 0, 0)),
                  pl.BlockSpec((kb, 2 * n, 2 * n), lambda k, q: (k, 0, 0)),
                  pl.BlockSpec((kb, 2 * n, 2 * n), lambda k, q: (k, 0, 0)),
                  pl.BlockSpec((kb, 2, n, c), lambda k, q: (k, 0, 0, order))],
        out_specs=pl.BlockSpec((1, 2, kb, n, c), lambda k, q: (q, 0, k, 0, 0)),
        out_shape=jax.ShapeDtypeStruct(a5.shape, F32),
        compiler_params=_params("arbitrary", "arbitrary"),
        name="mid",
    )(a5, gmat, hmat, kf)


def _fc_kernel(b_ref, f_ref, u_ref, m_ref, skip_ref, o_out):
    two, _, hn, g, c = u_ref.shape
    f = f_ref[...]
    skip = skip_ref[...]
    for j in range(g):
        b = b_ref[0, :, :, j, :]
        y = _dot(f, b.reshape(b.shape[0] * b.shape[1], c)).reshape(two, hn, c)
        o_out[:, 0, :, j, :] = m_ref[:, 0, :, j, :] * (y + u_ref[:, 0, :, j, :] * skip)


def _fc_call(b5, finv, u5, m5, skip_row):
    _, p, hn, n, c = u5.shape
    g = SUBLANES
    blk = pl.BlockSpec((2, 1, hn, g, c), lambda q, j: (0, q, 0, j, 0))
    return pl.pallas_call(
        _fc_kernel,
        grid=(p, n // g),
        in_specs=[pl.BlockSpec((1, 2, n, g, c), lambda q, j: (q, 0, 0, j, 0)), _full(finv.shape), blk, blk,
                  _full(skip_row.shape)],
        out_specs=blk,
        out_shape=jax.ShapeDtypeStruct(u5.shape, F32),
        compiler_params=_params("arbitrary", "arbitrary"),
        name="fc",
    )(b5, finv, u5, m5, skip_row)


def _dft_tables(n):
    hn = n // 2
    k = np.arange(n)[:, None]
    ang = -2.0 * np.pi * (k * np.arange(n)[None, :] % n) / n
    fr, fi = np.cos(ang), np.sin(ang)
    f_data = np.block([[fr[:, :hn], -fi[:, :hn]], [fi[:, :hn], fr[:, :hn]]])
    f_filt = np.concatenate([fr, fi], axis=0)
    er, ei = fr[:hn], -fi[:hn]
    f_inv = np.block([[er, -ei], [ei, er]]) / float(n * n)
    k1 = jnp.arange(n, dtype=jnp.int32)[:, None, None]
    k2 = jnp.arange(n, dtype=jnp.int32)[None, :, None]
    m2 = jnp.arange(n, dtype=jnp.int32)[None, None, :]
    idx = (m2 * (k1 + n * k2)) % (n * n)
    ang2 = idx.astype(F32) * (-2.0 * math.pi / (n * n))
    gr, gi = jnp.cos(ang2), jnp.sin(ang2)
    g = jnp.concatenate([jnp.concatenate([gr, -gi], axis=2), jnp.concatenate([gi, gr], axis=2)], axis=1)
    h = jnp.swapaxes(g, 1, 2)
    return (jnp.asarray(f_data, BF16), jnp.asarray(f_filt, BF16), jnp.asarray(f_inv, BF16),
            g.astype(BF16), h.astype(BF16))


def _hyena_filter_tables(seq):
    t = jnp.linspace(0.0, 1.0, seq, dtype=F32)[:, None]
    w = 2.0 * math.pi * jnp.arange(seq, dtype=F32)[:, None] / seq
    f = jnp.linspace(1e-4, HY_BANDS - 1, HY_BANDS, dtype=F32)[None, :]
    emb = jnp.concatenate([t, jnp.cos(f * w), -jnp.sin(f * w)], axis=-1)
    pos = jnp.concatenate([jnp.arange(seq), jnp.array([0]), jnp.arange(seq - 1, 0, -1)])
    emb = jnp.pad(emb[pos], ((0, 0), (0, HY_EMB_PAD - HY_EMB)))
    deltas = jnp.abs(jnp.linspace(math.log(HY_DECAY_TARGET) / HY_SLOW_DECAY,
                                  math.log(HY_DECAY_TARGET) / HY_FAST_DECAY, HY_WIDTH, dtype=F32))
    return emb, jnp.tile(deltas, HY_ORDER)[None, :]


def _hyena(hv, hx1, hx2, w1, b1, w2, b2, w3, freq, skip, kb):
    bsz, seq, c = hv.shape
    n = int(round(math.sqrt(2 * seq)))
    assert n * n == 2 * seq and bsz % 2 == 0
    hn, p = n // 2, bsz // 2
    f_data, f_filt, f_inv, gmat, hmat = _dft_tables(n)

    emb, deltas2 = _hyena_filter_tables(seq)
    w1p = jnp.pad(w1, ((0, HY_EMB_PAD - HY_EMB), (0, 0)))
    w3r = w3.reshape(w3.shape[0], HY_ORDER, 2, c)
    w3sel = jnp.stack([w3r[:, :, 0, :].reshape(-1, HY_ORDER * c), w3r[:, :, 1, :].reshape(-1, HY_ORDER * c)])
    full, asum = _filter_call(emb, w1p, b1[None], w2, b2[None], w3sel, freq[None], deltas2, seq, min(512, seq))
    c2 = HY_ORDER * c
    kf = _fb_call(_fa_call(full.reshape(2, 1, hn, n, c2), f_filt), gmat, asum)

    def view(t):
        return t.reshape(2, p, hn, n, c)

    def long_conv(u5, m5, order):
        bm = _mid_call(_fa_call(u5, f_data), gmat, hmat, kf, order, kb)
        return _fc_call(bm, f_inv, u5, m5, skip[order][None, :])

    z = long_conv(view(hv), view(hx1), 0)
    return long_conv(z, view(hx2), 1).reshape(bsz, seq, c)


def _merge_kernel(x_ref, at_ref, hy_ref, g_ref, mod_ref, wba_ref, wbh_ref, wo_ref, o_ref):
    d = x_ref.shape[2]
    g = g_ref[0].astype(F32)
    y = (jax.nn.sigmoid(g[:, :d]) * _dot(at_ref[0], wba_ref[...])
         + jax.nn.sigmoid(g[:, d:]) * _dot(hy_ref[0], wbh_ref[...]))
    o_ref[0] = x_ref[0] + mod_ref[0, 2:3, :] * _dot(y, wo_ref[...])


def _merge_call(x, attn, hy, gate, mod, wba, wbh, wo, tm):
    bsz, s, d = x.shape

    def tok(w):
        return pl.BlockSpec((1, tm, w), lambda b, i: (b, i, 0))

    return pl.pallas_call(
        _merge_kernel,
        grid=(bsz, s // tm),
        in_specs=[tok(d), tok(attn.shape[2]), tok(hy.shape[2]), tok(2 * d),
                  pl.BlockSpec((1, SUBLANES, d), lambda b, i: (b, 0, 0)),
                  _full(wba.shape), _full(wbh.shape), _full(wo.shape)],
        out_specs=tok(d),
        out_shape=jax.ShapeDtypeStruct((bsz, s, d), F32),
        compiler_params=_params("arbitrary", "arbitrary"),
        name="merge",
    )(x, attn, hy, gate, mod, wba, wbh, wo)


def _route_kernel(xm_ref, mod_ref, nf_ref, wrt_ref, rb_ref, tri_ref, lt_ref, h2_out, w_out, s_out, cnt_out):
    tm = xm_ref.shape[1]
    ng, gs = N_GROUPS, GROUP_SIZE

    h2 = _prenorm(xm_ref[0], mod_ref, 3, nf_ref[...])
    h2_out[0] = h2.astype(h2_out.dtype)
    scores = jax.nn.sigmoid(_dot3(wrt_ref[...], h2, NT_DIMS))
    sel = scores + rb_ref[...]
    slabs = [sel[ng * j:ng * (j + 1)] for j in range(gs)]
    sc_slabs = [scores[ng * j:ng * (j + 1)] for j in range(gs)]

    top1 = jnp.full((ng, tm), -jnp.inf, F32)
    top2 = top1
    for x in slabs:
        top2 = jnp.maximum(top2, jnp.minimum(top1, x))
        top1 = jnp.maximum(top1, x)
    gscore = top1 + top2
    gid = lax.broadcasted_iota(jnp.int32, (ng, 1), 0)
    rank = jnp.zeros((ng, tm), jnp.int32)
    for g2 in range(ng):
        row = gscore[g2:g2 + 1]
        beats = (row > gscore) | ((row == gscore) & (g2 < gid))
        rank = rank + beats.astype(jnp.int32)
    gmask = rank < TOPK_GROUPS

    cand = [jnp.where(gmask, x, -jnp.inf) for x in slabs]
    eid = [gid * gs + j for j in range(gs)]
    chosen = []
    for _ in range(TOP_K):
        best = functools.reduce(jnp.maximum, cand)
        best = jnp.max(best, axis=0, keepdims=True)
        idx = functools.reduce(jnp.minimum, [jnp.where(cand[j] == best, eid[j], N_EXPERTS) for j in range(gs)])
        idx = jnp.min(idx, axis=0, keepdims=True)
        chosen.append(idx)
        cand = [jnp.where(eid[j] == idx, -jnp.inf, cand[j]) for j in range(gs)]

    hit = [[eid[j] == idx for j in range(gs)] for idx in chosen]
    mask = [functools.reduce(jnp.logical_or, [hit[k][j] for k in range(TOP_K)]) for j in range(gs)]
    maskf = jnp.concatenate([m.astype(F32) for m in mask], axis=0)
    before = jnp.dot(maskf.astype(BF16), tri_ref[...], preferred_element_type=F32)
    cnt = jnp.sum(maskf, axis=1, keepdims=True)
    cnt_out[0] = jnp.broadcast_to(cnt, cnt_out.shape[1:])
    units = jnp.broadcast_to(jnp.floor((cnt + (RUN_ALIGN - 1)) * (1.0 / RUN_ALIGN)), (N_EXPERTS, LANES))
    start = RUN_ALIGN * jnp.dot(lt_ref[...], units.astype(BF16), preferred_element_type=F32)[:, 0:1]
    before = before + start
    bslabs = [before[ng * j:ng * (j + 1)] for j in range(gs)]

    def pick(k, vals):
        tot = functools.reduce(jnp.add, [jnp.where(hit[k][j], vals[j], 0.0) for j in range(gs)])
        return jnp.sum(tot, axis=0, keepdims=True)

    wk = [pick(k, sc_slabs) for k in range(TOP_K)]
    wsum = functools.reduce(jnp.add, wk)
    w_out[...] = jnp.concatenate(wk, axis=0) / wsum * ROUTE_SCALE
    s_out[...] = jnp.concatenate([pick(k, bslabs) for k in range(TOP_K)], axis=0).astype(jnp.int32)


def _route_call(xm, mod, norm_ffn, wrt, rbias, lower, tm):
    bsz, s, d = xm.shape
    t = bsz * s
    nt = s // tm
    tri = (jnp.arange(tm)[:, None] < jnp.arange(tm)[None, :]).astype(BF16)
    tok = pl.BlockSpec((TOP_K, tm), lambda i: (0, i))
    return pl.pallas_call(
        _route_kernel,
        grid=(t // tm,),
        in_specs=[pl.BlockSpec((1, tm, d), lambda i: (i // nt, i % nt, 0)),
                  pl.BlockSpec((1, SUBLANES, d), lambda i: (i // nt, 0, 0)),
                  _full(norm_ffn.shape), _full(wrt.shape), _full(rbias.shape), _full(tri.shape),
                  _full(lower.shape)],
        out_specs=[pl.BlockSpec((1, tm, d), lambda i: (i // nt, i % nt, 0)), tok, tok,
                   pl.BlockSpec((1, N_EXPERTS, LANES), lambda i: (i, 0, 0))],
        out_shape=[jax.ShapeDtypeStruct((bsz, s, d), BF16), jax.ShapeDtypeStruct((TOP_K, t), F32),
                   jax.ShapeDtypeStruct((TOP_K, t), jnp.int32),
                   jax.ShapeDtypeStruct((t // tm, N_EXPERTS, LANES), F32)],
        compiler_params=_params("arbitrary"),
        name="route",
    )(xm, mod, norm_ffn, wrt, rbias, tri, lower)


def _pack(x):
    w = x.shape[1] // 2
    lo = lax.bitcast_convert_type(x[:, :w].astype(BF16).astype(F32), jnp.uint32)
    hi = lax.bitcast_convert_type(x[:, w:].astype(BF16).astype(F32), jnp.uint32)
    return hi | (lo >> 16)


def _unpack(u):
    lo = lax.bitcast_convert_type(u << 16, F32).astype(BF16)
    hi = lax.bitcast_convert_type(u & jnp.uint32(0xFFFF0000), F32).astype(BF16)
    return lo, hi


def _pow2_pieces(units, limit):
    bit = 1
    while bit * 2 <= limit:
        bit *= 2
    while bit:
        yield (units & bit) != 0, units & ~(2 * bit - 1), bit
        bit //= 2


def _run_copies(vm_ref, hbm_ref, sem, n8, vm_row, hbm_row, limit, to_hbm, act):
    for on, off, size in _pow2_pieces(n8, limit):
        @pl.when(on)
        def _():
            rows = RUN_ALIGN * size
            v = vm_ref.at[pl.ds(pl.multiple_of(vm_row + RUN_ALIGN * off, RUN_ALIGN), rows), :]
            h = hbm_ref.at[pl.ds(pl.multiple_of(hbm_row + RUN_ALIGN * off, RUN_ALIGN), rows), :]
            act(pltpu.make_async_copy(v, h, sem) if to_hbm else pltpu.make_async_copy(h, v, sem))


def _dispatch_kernel(n8_ref, ls_ref, gs_ref, ts_ref, t8_ref, nu_ref, slot_ref, h_ref, xs_out, srt, zbuf, sem):
    step = pl.program_id(0)
    tm = h_ref.shape[0]
    rows = srt.shape[0]
    rid = lax.broadcasted_iota(jnp.int32, (rows, 1), 0)
    sel = functools.reduce(jnp.logical_or, [rid == slot_ref[k:k + 1, :] for k in range(TOP_K)])
    srt[...] = _pack(jnp.dot(jnp.where(sel, 1.0, 0.0).astype(BF16), h_ref[...], preferred_element_type=F32))

    def runs(act):
        def body(e, c):
            i = step * N_EXPERTS + e
            _run_copies(srt, xs_out, sem, n8_ref[i], ls_ref[i], gs_ref[i], tm // RUN_ALIGN, True, act)
            return c
        lax.fori_loop(0, N_EXPERTS, body, 0)

    runs(lambda cp: cp.start())
    runs(lambda cp: cp.wait())

    @pl.when(step == pl.num_programs(0) - 1)
    def _():
        zbuf[...] = jnp.zeros(zbuf.shape, zbuf.dtype)
        nblk = xs_out.shape[0] // EXPERT_BLOCK

        def fill(act):
            def tails(e, c):
                _run_copies(zbuf, xs_out, sem, t8_ref[e], 0, ts_ref[e], EXPERT_BLOCK // RUN_ALIGN - 1, True, act)
                return c

            def blocks(b, c):
                act(pltpu.make_async_copy(
                    zbuf, xs_out.at[pl.ds(pl.multiple_of(b * EXPERT_BLOCK, EXPERT_BLOCK), EXPERT_BLOCK), :], sem))
                return c

            lax.fori_loop(0, N_EXPERTS, tails, 0)
            lax.fori_loop(nu_ref[0], nblk, blocks, 0)

        fill(lambda cp: cp.start())
        fill(lambda cp: cp.wait())


def _dispatch_call(tables, slot_kt, h2, nblk, tm):
    t, d = h2.shape
    lrows = TOP_K * tm + N_EXPERTS * RUN_ALIGN
    return pl.pallas_call(
        _dispatch_kernel,
        grid_spec=pltpu.PrefetchScalarGridSpec(
            num_scalar_prefetch=len(tables), grid=(t // tm,),
            in_specs=[pl.BlockSpec((TOP_K, tm), lambda i, *_: (0, i)), pl.BlockSpec((tm, d), lambda i, *_: (i, 0))],
            out_specs=pl.BlockSpec(memory_space=pl.ANY),
            scratch_shapes=[pltpu.VMEM((lrows, d // 2), jnp.uint32), pltpu.VMEM((EXPERT_BLOCK, d // 2), jnp.uint32),
                            pltpu.SemaphoreType.DMA(())]),
        out_shape=jax.ShapeDtypeStruct((nblk * EXPERT_BLOCK, d // 2), jnp.uint32),
        compiler_params=_params("arbitrary"),
        name="dispatch",
    )(*tables, slot_kt, h2)


def _expert_kernel(blk_ref, nused_ref, x_ref, wgu_ref, wd_ref, y_ref):
    used = pl.program_id(0) < nused_ref[0]

    @pl.when(used)
    def _():
        lo, hi = _unpack(x_ref[...])
        half = lo.shape[1]
        gu = (jnp.dot(lo, wgu_ref[0, :half, :], preferred_element_type=F32)
              + jnp.dot(hi, wgu_ref[0, half:, :], preferred_element_type=F32))
        a = _silu(gu[:, :EXPERT_FF]) * gu[:, EXPERT_FF:]
        y_ref[...] = _pack(_dot(a, wd_ref[0]))

    @pl.when(jnp.logical_not(used))
    def _():
        y_ref[...] = jnp.zeros(y_ref.shape, y_ref.dtype)


def _expert_call(blk_e, nused, xs, wgu, wd):
    rows, d = xs.shape
    nblk = rows // EXPERT_BLOCK

    def row_map(i, blk, nu):
        return (jnp.minimum(i, nu[0] - 1), 0)

    return pl.pallas_call(
        _expert_kernel,
        grid_spec=pltpu.PrefetchScalarGridSpec(
            num_scalar_prefetch=2, grid=(nblk,),
            in_specs=[pl.BlockSpec((EXPERT_BLOCK, d), row_map),
                      pl.BlockSpec((1,) + wgu.shape[1:], lambda i, blk, nu: (blk[i], 0, 0)),
                      pl.BlockSpec((1,) + wd.shape[1:], lambda i, blk, nu: (blk[i], 0, 0))],
            out_specs=pl.BlockSpec((EXPERT_BLOCK, d), lambda i, blk, nu: (i, 0))),
        out_shape=jax.ShapeDtypeStruct((rows, d), jnp.uint32),
        compiler_params=_params("arbitrary"),
        name="expert",
    )(blk_e, nused, xs, wgu, wd)


def _combine_kernel(n8_ref, ls_ref, gs_ref, ys_hbm, slot_ref, w_ref, xm_ref, h_ref, mod_ref, wsgu_ref, wsd_ref,
                    fn_ref, o_ref, ybuf, sem):
    step = pl.program_id(0)
    tm = xm_ref.shape[0]
    rows = ybuf.shape[0]

    def runs(act):
        def body(e, c):
            i = step * N_EXPERTS + e
            _run_copies(ybuf, ys_hbm, sem, n8_ref[i], ls_ref[i], gs_ref[i], tm // RUN_ALIGN, False, act)
            return c
        lax.fori_loop(0, N_EXPERTS, body, 0)

    runs(lambda cp: cp.start())
    gu = _dot(h_ref[...], wsgu_ref[...])
    ff = gu.shape[1] // 2
    shared = _dot(_silu(gu[:, :ff]) * gu[:, ff:], wsd_ref[...])
    cid = lax.broadcasted_iota(jnp.int32, (1, rows), 1)
    slot = slot_ref[...]
    w = w_ref[...]
    mix = functools.reduce(jnp.add, [jnp.where(slot[:, k:k + 1] == cid, w[:, k:k + 1], 0.0) for k in range(TOP_K)])
    mix = mix.astype(BF16)
    runs(lambda cp: cp.wait())
    last = step * N_EXPERTS + N_EXPERTS - 1
    filled = ls_ref[last] + RUN_ALIGN * n8_ref[last]
    rid = lax.broadcasted_iota(jnp.int32, (rows, 1), 0)
    lo, hi = _unpack(jnp.where(rid < filled, ybuf[...], jnp.uint32(0)))
    routed = jnp.concatenate([jnp.dot(mix, lo, preferred_element_type=F32),
                              jnp.dot(mix, hi, preferred_element_type=F32)], axis=1)
    x = xm_ref[...] + mod_ref[0, 5:6, :] * (routed + shared)
    o_ref[...] = _rms(x, fn_ref[...])


def _combine_call(tables, ys, slot_tk, w_tk, xm, h2, mod, wsgu, wsd, final_norm, tm, tiles_per_batch):
    t, d = xm.shape
    lrows = TOP_K * tm + N_EXPERTS * RUN_ALIGN
    tok = pl.BlockSpec((tm, d), lambda i, *_: (i, 0))
    per_k = pl.BlockSpec((tm, TOP_K), lambda i, *_: (i, 0))
    return pl.pallas_call(
        _combine_kernel,
        grid_spec=pltpu.PrefetchScalarGridSpec(
            num_scalar_prefetch=len(tables), grid=(t // tm,),
            in_specs=[pl.BlockSpec(memory_space=pl.ANY), per_k, per_k, tok, tok,
                      pl.BlockSpec((1, SUBLANES, d), lambda i, *_: (i // tiles_per_batch, 0, 0)),
                      _full(wsgu.shape), _full(wsd.shape), _full(final_norm.shape)],
            out_specs=tok,
            scratch_shapes=[pltpu.VMEM((lrows, d // 2), jnp.uint32), pltpu.SemaphoreType.DMA(())]),
        out_shape=jax.ShapeDtypeStruct((t, d), F32),
        compiler_params=_params("arbitrary"),
        name="combine",
    )(*tables, ys, slot_tk, w_tk, xm, h2, mod, wsgu, wsd, final_norm)


def _moe(xm, mod, norm_ffn, w_router, router_bias, wg, wu, wd, wsg, wsu, wsd, final_norm, tm):
    bsz, s, d = xm.shape
    t = bsz * s
    nt = t // tm
    perm = (np.arange(N_EXPERTS) % N_GROUPS) * GROUP_SIZE + np.arange(N_EXPERTS) // N_GROUPS
    wrt = w_router.T[perm]
    rbias = router_bias[perm][:, None]
    lower = jnp.asarray(perm[None, :] < perm[:, None], BF16)
    h2, w_kt, slot_kt, cnt = _route_call(xm, mod, norm_ffn, wrt, rbias, lower, tm)

    inv = np.argsort(perm)
    n8 = (cnt[:, :, 0].astype(jnp.int32)[:, inv] + (RUN_ALIGN - 1)) // RUN_ALIGN
    run = RUN_ALIGN * n8
    ls = jnp.cumsum(run, axis=1) - run
    tot = jnp.sum(run, axis=0)
    padded = (tot + EXPERT_BLOCK - 1) // EXPERT_BLOCK * EXPERT_BLOCK
    pad_end = jnp.cumsum(padded)
    gs = (pad_end - padded)[None, :] + jnp.cumsum(run, axis=0) - run
    nblk = -(-(t * TOP_K + nt * N_EXPERTS * (RUN_ALIGN - 1)) // EXPERT_BLOCK) + N_EXPERTS
    blk_first = jnp.arange(nblk, dtype=jnp.int32)[:, None] * EXPERT_BLOCK
    blk_e = jnp.minimum(jnp.sum((pad_end[None, :] <= blk_first).astype(jnp.int32), axis=1), N_EXPERTS - 1)
    nused = (pad_end[-1:] // EXPERT_BLOCK).astype(jnp.int32)
    tables = [a.reshape(-1).astype(jnp.int32) for a in (n8, ls, gs)]
    tails = [(pad_end - padded + tot).astype(jnp.int32), ((padded - tot) // RUN_ALIGN).astype(jnp.int32), nused]

    h2f = h2.reshape(t, d)
    xs = _dispatch_call(tables + tails, slot_kt, h2f, nblk, tm)
    wgu = jnp.concatenate([wg, wu], axis=2).astype(BF16)
    ys = _expert_call(blk_e, nused, xs, wgu, wd.astype(BF16))
    wsgu = jnp.concatenate([wsg, wsu], axis=1).astype(BF16)
    out = _combine_call(tables, ys, slot_kt.T, w_kt.T, xm.reshape(t, d), h2f, mod, wsgu, wsd.astype(BF16),
                        final_norm, tm, s // tm)
    return out.reshape(bsz, s, d)


def _rope_tables(s):
    rows = s // GRID_W
    row = jnp.broadcast_to(jnp.arange(rows, dtype=F32)[:, None], (rows, GRID_W)).reshape(-1)
    col = jnp.broadcast_to(jnp.arange(GRID_W, dtype=F32)[None, :], (rows, GRID_W)).reshape(-1)
    half = QK_ROPE // 2
    inv_freq = ROPE_THETA ** (-jnp.arange(0, half, 2, dtype=F32) / half)
    ar, ac = row[:, None] * inv_freq, col[:, None] * inv_freq
    ones = jnp.ones((s, QK_NOPE), F32)
    tail = HEAD_PAD - QK_NOPE - QK_ROPE
    cos_t = jnp.concatenate([ones, jnp.cos(ar), jnp.cos(ar), jnp.cos(ac), jnp.cos(ac), jnp.ones((s, tail), F32)], 1)
    sin_t = jnp.concatenate([0 * ones, -jnp.sin(ar), jnp.sin(ar), -jnp.sin(ac), jnp.sin(ac),
                             jnp.zeros((s, tail), F32)], 1)
    return cos_t, sin_t


_Q4 = QK_ROPE // 4
ROPE_SWAP = np.concatenate([np.arange(_Q4, 2 * _Q4), np.arange(0, _Q4), np.arange(3 * _Q4, 4 * _Q4),
                            np.arange(2 * _Q4, 3 * _Q4)])


def _rope_slot(w, swap):
    if swap:
        w = w[..., ROPE_SWAP]
    pad = [(0, 0)] * (w.ndim - 1) + [(QK_NOPE, HEAD_PAD - QK_NOPE - QK_ROPE)]
    return jnp.pad(w, pad)


def kernel(x, c, ctx, c_ctx, w_mod, b_mod, norm_mix, norm_ffn, w_in, b_in, q_norm, w_uq, kv_norm, w_ukv, w_branch_attn, hy_conv_w, hy_conv_b, hy_filt_w1, hy_filt_b1, hy_filt_w2, hy_filt_b2, hy_filt_w3, hy_filt_freq, hy_skip, w_branch_hyena, w_out, w_router, router_bias, w_exp_gate, w_exp_up, w_exp_down, w_sh_gate, w_sh_up, w_sh_down, final_norm,
           tiles=None):
    bsz, s, d = x.shape
    tl = dict(inproj=256, tq=512, tk=1408, fft_kb=4, merge=512, moe=256)
    tl.update(tiles or {})
    assert w_mod.shape[0] == 1, "single-layer trunk"
    i = 0

    rows = -(-(bsz + 1) // SUBLANES) * SUBLANES
    c_rows = jnp.pad(jnp.concatenate([c, c_ctx[None]], axis=0), ((0, rows - bsz - 1), (0, 0)))
    mod_all = _mod_call(c_rows, w_mod[i], b_mod[i])
    mod_all = jnp.pad(mod_all.reshape(rows, 6, d), ((0, 0), (0, SUBLANES - 6), (0, 0)))
    mod, modc = mod_all[:bsz], mod_all[bsz:bsz + 1]

    cuts = np.cumsum([Q_LORA, KV_LORA, QK_ROPE, 3 * HY_WIDTH])
    wi, bi = w_in[i], b_in[i][None]
    w_q, w_kv, w_pe, w_hy, w_g = jnp.split(wi, cuts, axis=1)
    b_q, b_kv, b_pe, b_hy, b_g = jnp.split(bi, cuts, axis=1)
    wa = jnp.concatenate([w_q, w_kv, _rope_slot(w_pe, False), _rope_slot(w_pe, True)], axis=1).astype(BF16)
    ba = jnp.concatenate([b_q, b_kv, _rope_slot(b_pe, False), _rope_slot(b_pe, True)], axis=1)
    wq3 = w_uq[i].reshape(Q_LORA, N_HEADS, QK_NOPE + QK_ROPE) * (ATTN_SCALE * math.log2(math.e))
    tail = ((0, 0), (0, 0), (0, HEAD_PAD - QK_NOPE))
    wuq = (jnp.pad(wq3[..., :QK_NOPE], tail) + _rope_slot(wq3[..., QK_NOPE:], False)).reshape(Q_LORA, -1).astype(BF16)
    wuqs = _rope_slot(wq3[..., QK_NOPE:], True).reshape(Q_LORA, -1).astype(BF16)
    wkv3 = w_ukv[i].reshape(KV_LORA, N_HEADS, QK_NOPE + V_HEAD)
    wuk = jnp.pad(wkv3[..., :QK_NOPE], tail).reshape(KV_LORA, -1).astype(BF16)
    wuvt = wkv3[..., QK_NOPE:].reshape(KV_LORA, -1).T.astype(BF16)
    nm, qn, kvn = norm_mix[i][None], q_norm[i][None], kv_norm[i][None]

    w_c = jnp.concatenate([w_kv, _rope_slot(w_pe, False)], axis=1).astype(BF16)
    b_c = jnp.concatenate([b_kv, _rope_slot(b_pe, False)], axis=1)
    ck, cvt = _ctx_call(ctx, modc, nm, w_c, b_c, kvn, wuk, wuvt)

    cos_t, sin_t = _rope_tables(s)
    q, k, vt, hv, hx1, hx2, gate = _inproj_call(
        x, mod, nm, wa, ba, w_hy.astype(BF16), b_hy, w_g.astype(BF16), b_g, qn, wuq, wuqs, kvn, wuk, wuvt,
        cos_t, sin_t, hy_conv_w[i], hy_conv_b[i][None], tl["inproj"])

    attn = _attn_call(q, jnp.concatenate([ck, k], axis=2), jnp.concatenate([cvt, vt], axis=3), tl["tq"], tl["tk"])
    hy = _hyena(hv, hx1, hx2, hy_filt_w1[i], hy_filt_b1[i], hy_filt_w2[i], hy_filt_b2[i], hy_filt_w3[i],
                hy_filt_freq[i], hy_skip[i], tl["fft_kb"])
    xm = _merge_call(x, attn, hy, gate, mod, w_branch_attn[i].astype(BF16), w_branch_hyena[i].astype(BF16),
                     w_out[i].astype(BF16), tl["merge"])
    return _moe(xm, mod, norm_ffn[i][None], w_router[i], router_bias[i], w_exp_gate[i], w_exp_up[i], w_exp_down[i],
                w_sh_gate[i], w_sh_up[i], w_sh_down[i], final_norm[None], tl["moe"])
```

```python
import functools
import math

import numpy as np
import jax
import jax.numpy as jnp
from jax import lax
from jax.experimental import pallas as pl
from jax.experimental.pallas import tpu as pltpu

GRID_W = 64
N_HEADS = 8
QK_NOPE = 64
QK_ROPE = 32
V_HEAD = 64
Q_LORA = 256
KV_LORA = 128
ROPE_THETA = 10000.0
ATTN_SCALE = 1.0 / math.sqrt(QK_NOPE + QK_ROPE)
HY_WIDTH = 512
HY_ORDER = 2
HY_SHORT = 3
HY_BANDS = 8
HY_EMB = 1 + 2 * HY_BANDS
HY_EMB_PAD = 32
HY_FAST_DECAY = 0.3
HY_SLOW_DECAY = 1.5
HY_DECAY_TARGET = 1e-2
N_EXPERTS = 64
N_GROUPS = 8
GROUP_SIZE = N_EXPERTS // N_GROUPS
TOPK_GROUPS = 4
TOP_K = 8
EXPERT_FF = 256
ROUTE_SCALE = 2.5
EXPERT_BLOCK = 512
RUN_ALIGN = 8
LONG_RUN = 8
NORM_EPS = 1e-6

HEAD_PAD = 128
Q_CHUNK = 512
AHEAD = 2
LANES = 128
SUBLANES = 8
VMEM_LIMIT = 48 * 1024 * 1024

F32 = jnp.float32
BF16 = jnp.bfloat16
NT_DIMS = (((1,), (1,)), ((), ()))
NN_DIMS = (((1,), (0,)), ((), ()))


def _params(*sem):
    return pltpu.CompilerParams(dimension_semantics=sem, vmem_limit_bytes=VMEM_LIMIT)


def _dot(a, b):
    return jnp.dot(a.astype(BF16), b.astype(BF16), preferred_element_type=F32)


def _split(a):
    hi = a.astype(BF16)
    lo = (a - hi.astype(F32)).astype(BF16)
    return hi, lo


def _dot3(a, b, dims=NN_DIMS):
    ah, al = _split(a)
    bh, bl = _split(b)
    d = functools.partial(lax.dot_general, dimension_numbers=dims, preferred_element_type=F32)
    return d(ah, bh) + (d(ah, bl) + d(al, bh))


def _rms(x, g):
    return x * lax.rsqrt(jnp.mean(x * x, axis=-1, keepdims=True) + NORM_EPS) * g


def _silu(x):
    return x * jax.nn.sigmoid(x)


def _full(shape):
    nd = len(shape)
    return pl.BlockSpec(shape, lambda *_: (0,) * nd)


def _mod_kernel(c_ref, w_ref, b_ref, o_ref):
    o_ref[...] = _dot3(_silu(c_ref[...]), w_ref[...]) + b_ref[...]


def _mod_call(c_rows, w_mod, b_mod):
    r, d = c_rows.shape
    n = w_mod.shape[1]
    bn = 1024
    return pl.pallas_call(
        _mod_kernel,
        grid=(n // bn,),
        in_specs=[_full((r, d)), pl.BlockSpec((d, bn), lambda j: (0, j)), pl.BlockSpec((1, bn), lambda j: (0, j))],
        out_specs=pl.BlockSpec((r, bn), lambda j: (0, j)),
        out_shape=jax.ShapeDtypeStruct((r, n), F32),
        compiler_params=_params("arbitrary"),
        name="mod",
    )(c_rows, w_mod, b_mod.reshape(1, n))


def _prenorm(x, mod_ref, row, g):
    shift = mod_ref[0, row:row + 1, :]
    scale = mod_ref[0, row + 1:row + 2, :]
    return _rms(x, g) * (1.0 + scale) + shift


def _kv_heads(kv_lat, kpe, kvn_ref, wuk_ref, wuvt_ref, k_out, vt_out):
    kvn = _rms(kv_lat, kvn_ref[...]).astype(BF16)
    kk = _dot(kvn, wuk_ref[...])
    vt = lax.dot_general(wuvt_ref[...], kvn, NT_DIMS, preferred_element_type=F32)
    ones = jnp.ones((HEAD_PAD - V_HEAD, vt.shape[1]), F32)
    for h in range(N_HEADS):
        k_out[0, h] = (kk[:, HEAD_PAD * h:HEAD_PAD * (h + 1)] + kpe).astype(BF16)
        vt_out[0, h] = jnp.concatenate([vt[V_HEAD * h:V_HEAD * (h + 1)], ones], axis=0).astype(BF16)


def _ctx_kernel(c_ref, mod_ref, nm_ref, w_ref, b_ref, kvn_ref, wuk_ref, wuv_ref, k_out, v_out):
    h = _prenorm(c_ref[0], mod_ref, 0, nm_ref[...]).astype(BF16)
    a = _dot(h, w_ref[...]) + b_ref[...]
    _kv_heads(a[:, :KV_LORA], a[:, KV_LORA:], kvn_ref, wuk_ref, wuv_ref, k_out, v_out)


def _ctx_call(ctx, modc, norm_mix, w_c, b_c, kv_norm, w_uk, w_uv):
    bsz, n, d = ctx.shape
    return pl.pallas_call(
        _ctx_kernel,
        grid=(bsz,),
        in_specs=[pl.BlockSpec((1, n, d), lambda b: (b, 0, 0)), _full(modc.shape), _full(norm_mix.shape),
                  _full(w_c.shape), _full(b_c.shape), _full(kv_norm.shape), _full(w_uk.shape), _full(w_uv.shape)],
        out_specs=[pl.BlockSpec((1, N_HEADS, n, HEAD_PAD), lambda b: (b, 0, 0, 0)),
                   pl.BlockSpec((1, N_HEADS, HEAD_PAD, n), lambda b: (b, 0, 0, 0))],
        out_shape=[jax.ShapeDtypeStruct((bsz, N_HEADS, n, HEAD_PAD), BF16),
                   jax.ShapeDtypeStruct((bsz, N_HEADS, HEAD_PAD, n), BF16)],
        compiler_params=_params("arbitrary"),
        name="ctx",
    )(ctx, modc, norm_mix, w_c, b_c, kv_norm, w_uk, w_uv)


def _inproj_kernel(x_ref, xp_ref, xn_ref, mod_ref, nm_ref, wa_ref, ba_ref, why_ref, bhy_ref, wg_ref, bg_ref,
                   qn_ref, wuq_ref, wuqs_ref, kvn_ref, wuk_ref, wuv_ref, cos_ref, sin_ref, cw_ref, cb_ref,
                   q_out, k_out, v_out, hv_out, hx1_out, hx2_out, g_out):
    i = pl.program_id(0)
    tm = x_ref.shape[1]
    nm = nm_ref[...]
    h = _prenorm(x_ref[0], mod_ref, 0, nm).astype(BF16)
    a = _dot(h, wa_ref[...]) + ba_ref[...]
    q_lat = a[:, :Q_LORA]
    kv_lat = a[:, Q_LORA:Q_LORA + KV_LORA]
    kpe_m = a[:, Q_LORA + KV_LORA:Q_LORA + KV_LORA + HEAD_PAD]
    kpe_s = a[:, Q_LORA + KV_LORA + HEAD_PAD:]
    cos = cos_ref[...]
    sin = sin_ref[...]
    qn = _rms(q_lat, qn_ref[...]).astype(BF16)
    qa = _dot(qn, wuq_ref[...])
    qs = _dot(qn, wuqs_ref[...])
    for hh in range(N_HEADS):
        sl = slice(HEAD_PAD * hh, HEAD_PAD * (hh + 1))
        q_out[0, hh] = (qa[:, sl] * cos + qs[:, sl] * sin).astype(BF16)
    _kv_heads(kv_lat, kpe_m * cos + kpe_s * sin, kvn_ref, wuk_ref, wuv_ref, k_out, v_out)
    g_out[0] = (_dot(h, wg_ref[...]) + bg_ref[...]).astype(BF16)

    why = why_ref[...]
    bhy = bhy_ref[...]
    hy = _dot(h, why) + bhy
    hp = _dot(_prenorm(xp_ref[0], mod_ref, 0, nm).astype(BF16), why) + bhy
    hn = _dot(_prenorm(xn_ref[0], mod_ref, 0, nm).astype(BF16), why) + bhy
    prev = jnp.where(i == 0, 0.0, hp[SUBLANES - 1:SUBLANES])
    nxt = jnp.where(i == pl.num_programs(0) - 1, 0.0, hn[0:1])
    rid = lax.broadcasted_iota(jnp.int32, (tm, 1), 0)
    up = jnp.where(rid == 0, prev, pltpu.roll(hy, 1, 0))
    dn = jnp.where(rid == tm - 1, nxt, pltpu.roll(hy, tm - 1, 0))
    u = up * cw_ref[0:1, :] + hy * cw_ref[1:2, :] + dn * cw_ref[2:3, :] + cb_ref[...]
    hv_out[0] = u[:, :HY_WIDTH]
    hx1_out[0] = u[:, HY_WIDTH:2 * HY_WIDTH]
    hx2_out[0] = u[:, 2 * HY_WIDTH:]


def _inproj_call(x, mod, norm_mix, wa, ba, why, bhy, wg, bg, q_norm, wuq, wuqs, kv_norm, wuk, wuvt, cos_t, sin_t, cw,
                 cb, tm):
    bsz, s, d = x.shape
    nt = s // tm
    rb = tm // SUBLANES
    last_rb = s // SUBLANES - 1
    consts = [norm_mix, wa, ba, why, bhy, wg, bg, q_norm, wuq, wuqs, kv_norm, wuk, wuvt]
    in_specs = [
        pl.BlockSpec((1, tm, d), lambda i, b: (b, i, 0)),
        pl.BlockSpec((1, SUBLANES, d), lambda i, b: (b, jnp.maximum(i * rb - 1, 0), 0)),
        pl.BlockSpec((1, SUBLANES, d), lambda i, b: (b, jnp.minimum((i + 1) * rb, last_rb), 0)),
        pl.BlockSpec((1, SUBLANES, d), lambda i, b: (b, 0, 0)),
    ] + [_full(c.shape) for c in consts] + [
        pl.BlockSpec((tm, HEAD_PAD), lambda i, b: (i, 0)),
        pl.BlockSpec((tm, HEAD_PAD), lambda i, b: (i, 0)),
        _full(cw.shape), _full(cb.shape),
    ]
    hw = HY_WIDTH
    out_specs = [
        pl.BlockSpec((1, N_HEADS, tm, HEAD_PAD), lambda i, b: (b, 0, i, 0)),
        pl.BlockSpec((1, N_HEADS, tm, HEAD_PAD), lambda i, b: (b, 0, i, 0)),
        pl.BlockSpec((1, N_HEADS, HEAD_PAD, tm), lambda i, b: (b, 0, 0, i)),
        pl.BlockSpec((1, tm, hw), lambda i, b: (b, i, 0)),
        pl.BlockSpec((1, tm, hw), lambda i, b: (b, i, 0)),
        pl.BlockSpec((1, tm, hw), lambda i, b: (b, i, 0)),
        pl.BlockSpec((1, tm, 2 * d), lambda i, b: (b, i, 0)),
    ]
    out_shape = [
        jax.ShapeDtypeStruct((bsz, N_HEADS, s, HEAD_PAD), BF16),
        jax.ShapeDtypeStruct((bsz, N_HEADS, s, HEAD_PAD), BF16),
        jax.ShapeDtypeStruct((bsz, N_HEADS, HEAD_PAD, s), BF16),
        jax.ShapeDtypeStruct((bsz, s, hw), F32),
        jax.ShapeDtypeStruct((bsz, s, hw), F32),
        jax.ShapeDtypeStruct((bsz, s, hw), F32),
        jax.ShapeDtypeStruct((bsz, s, 2 * d), BF16),
    ]
    return pl.pallas_call(
        _inproj_kernel,
        grid=(nt, bsz),
        in_specs=in_specs,
        out_specs=out_specs,
        out_shape=out_shape,
        compiler_params=_params("arbitrary", "arbitrary"),
        name="inproj",
    )(x, x, x, mod, *consts, cos_t, sin_t, cw, cb)


def _attn_kernel(q_ref, k_ref, vt_ref, o_ref, m_sc, acc_sc):
    j = pl.program_id(2)

    @pl.when(j == 0)
    def _():
        m_sc[...] = jnp.full(m_sc.shape, -jnp.inf, F32)
        acc_sc[...] = jnp.zeros(acc_sc.shape, F32)

    tq = q_ref.shape[2]
    qw = min(tq, Q_CHUNK)
    units = [(h, c) for h in range(N_HEADS) for c in range(0, tq, qw)]

    def scores(u):
        h, c = units[u]
        return lax.dot_general(k_ref[0, h], q_ref[0, h, c:c + qw, :], NT_DIMS,
                               preferred_element_type=F32)

    pending = [scores(u) for u in range(AHEAD)]
    for u, (h, c) in enumerate(units):
        if u + AHEAD < len(units):
            pending.append(scores(u + AHEAD))
        st = pending.pop(0)
        m_prev = m_sc[h, :, c:c + qw]
        m_new = jnp.maximum(m_prev, jnp.max(st, axis=0, keepdims=True))
        pt = jnp.exp2(st - m_new).astype(BF16)
        acc_sc[h, :, c:c + qw] = (jnp.exp2(m_prev - m_new) * acc_sc[h, :, c:c + qw]
                                  + jnp.dot(vt_ref[0, h], pt, preferred_element_type=F32))
        m_sc[h, :, c:c + qw] = m_new

    @pl.when(j == pl.num_programs(2) - 1)
    def _():
        ot = jnp.concatenate([acc_sc[h, :V_HEAD] / acc_sc[h, V_HEAD:V_HEAD + 1] for h in range(N_HEADS)], axis=0)
        o_ref[0] = ot.T.astype(o_ref.dtype)


def _attn_call(q, k, vt, tq, tk):
    bsz, nh, s, dh = q.shape
    nk = k.shape[2]
    dv = nh * V_HEAD
    return pl.pallas_call(
        _attn_kernel,
        grid=(bsz, s // tq, nk // tk),
        in_specs=[
            pl.BlockSpec((1, nh, tq, dh), lambda b, i, j: (b, 0, i, 0)),
            pl.BlockSpec((1, nh, tk, dh), lambda b, i, j: (b, 0, j, 0)),
            pl.BlockSpec((1, nh, dh, tk), lambda b, i, j: (b, 0, 0, j)),
        ],
        out_specs=pl.BlockSpec((1, tq, dv), lambda b, i, j: (b, i, 0)),
        out_shape=jax.ShapeDtypeStruct((bsz, s, dv), BF16),
        scratch_shapes=[pltpu.VMEM((nh, 1, tq), F32), pltpu.VMEM((nh, dh, tq), F32)],
        compiler_params=_params("arbitrary", "arbitrary", "arbitrary"),
        name="attn",
    )(q, k, vt)


def _filter_kernel(emb_ref, w1_ref, b1_ref, w2_ref, b2_ref, w3_ref, fr_ref, dl_ref, full_out, asum_out, *, seq):
    r = pl.program_id(0)
    rb = emb_ref.shape[0]
    emb = emb_ref[...]
    fr = fr_ref[...]
    h = jnp.sin(fr * (_dot3(emb, w1_ref[...]) + b1_ref[...]))
    h = jnp.sin(fr * (_dot3(h, w2_ref[...]) + b2_ref[...]))
    k = _dot3(h, w3_ref[0]) * jnp.exp(-emb[:, 0:1] * dl_ref[...])
    row = r * rb + lax.broadcasted_iota(jnp.int32, (rb, 1), 0)
    k = jnp.where(row == seq, 0.0, k)
    full_out[...] = k

    @pl.when(r == 0)
    def _():
        asum_out[...] = jnp.zeros(asum_out.shape, F32)

    asum_out[...] += jnp.sum(jnp.abs(k), axis=0, keepdims=True)


def _filter_call(emb, w1, b1, w2, b2, w3sel, freq, deltas2, seq, rb):
    n2 = emb.shape[0]
    half_blocks = seq // rb
    width = w3sel.shape[2]
    return pl.pallas_call(
        functools.partial(_filter_kernel, seq=seq),
        grid=(n2 // rb,),
        in_specs=[pl.BlockSpec((rb, HY_EMB_PAD), lambda r: (r, 0)), _full(w1.shape), _full(b1.shape),
                  _full(w2.shape), _full(b2.shape),
                  pl.BlockSpec((1,) + w3sel.shape[1:], lambda r: (r // half_blocks, 0, 0)),
                  _full(freq.shape), _full(deltas2.shape)],
        out_specs=[pl.BlockSpec((rb, width), lambda r: (r, 0)), pl.BlockSpec((1, width), lambda r: (0, 0))],
        out_shape=[jax.ShapeDtypeStruct((n2, width), F32), jax.ShapeDtypeStruct((1, width), F32)],
        compiler_params=_params("arbitrary"),
        name="filt",
    )(emb, w1, b1, w2, b2, w3sel, freq, deltas2)


def _fa_kernel(u_ref, f_ref, a_out):
    two, _, hn, g, c = u_ref.shape
    a = _dot(f_ref[...], u_ref[...].reshape(two * hn * g, c))
    a_out[...] = a.reshape(a_out.shape)


def _fa_call(u5, fmat):
    _, p, hn, n, c = u5.shape
    g = SUBLANES
    return pl.pallas_call(
        _fa_kernel,
        grid=(p, n // g),
        in_specs=[pl.BlockSpec((2, 1, hn, g, c), lambda q, j: (0, q, 0, j, 0)), _full(fmat.shape)],
        out_specs=pl.BlockSpec((1, 2, n, g, c), lambda q, j: (q, 0, 0, j, 0)),
        out_shape=jax.ShapeDtypeStruct((p, 2, n, n, c), F32),
        compiler_params=_params("arbitrary", "arbitrary"),
        name="fa",
    )(u5, fmat)


def _fb_kernel(a_ref, g_ref, asum_ref, kf_out):
    two, _, n, c = a_ref.shape[1:]
    a = a_ref[0].reshape(two * n, c)
    x = _dot(g_ref[0], a) / (asum_ref[...] + 1e-6)
    kf_out[0] = x.reshape(two, n, c)


def _fb_call(a5, gmat, asum):
    _, _, n, _, c = a5.shape
    return pl.pallas_call(
        _fb_kernel,
        grid=(n,),
        in_specs=[pl.BlockSpec((1, 2, 1, n, c), lambda k: (0, 0, k, 0, 0)),
                  pl.BlockSpec((1, 2 * n, 2 * n), lambda k: (k, 0, 0)), _full(asum.shape)],
        out_specs=pl.BlockSpec((1, 2, n, c), lambda k: (k, 0, 0, 0)),
        out_shape=jax.ShapeDtypeStruct((n, 2, n, c), F32),
        compiler_params=_params("arbitrary"),
        name="fb",
    )(a5, gmat, asum)


def _mid_kernel(a_ref, g_ref, h_ref, kf_ref, b_out):
    _, two, kb, n, c = a_ref.shape
    for kk in range(kb):
        x = _dot(g_ref[kk], a_ref[0, :, kk].reshape(two * n, c))
        xr, xi = x[:n], x[n:]
        kr, ki = kf_ref[kk, 0], kf_ref[kk, 1]
        y = jnp.concatenate([xr * kr - xi * ki, xr * ki + xi * kr], axis=0)
        b_out[0, :, kk] = _dot(h_ref[kk], y).reshape(two, n, c)


def _mid_call(a5, gmat, hmat, kf, order, kb):
    p, _, n, _, c = a5.shape
    return pl.pallas_call(
        _mid_kernel,
        grid=(n // kb, p),
        in_specs=[pl.BlockSpec((1, 2, kb, n, c), lambda k, q: (q, 0, k, 0, 0)),
                  pl.BlockSpec((kb, 2 * n, 2 * n), lambda k, q: (k, 0, 0)),
                  pl.BlockSpec((kb, 2 * n, 2 * n), lambda k, q: (k, 0, 0)),
                  pl.BlockSpec((kb, 2, n, c), lambda k, q: (k, 0, 0, order))],
        out_specs=pl.BlockSpec((1, 2, kb, n, c), lambda k, q: (q, 0, k, 0, 0)),
        out_shape=jax.ShapeDtypeStruct(a5.shape, F32),
        compiler_params=_params("arbitrary", "arbitrary"),
        name="mid",
    )(a5, gmat, hmat, kf)


def _fc_kernel(b_ref, f_ref, u_ref, m_ref, skip_ref, o_out):
    _, two, n, g, c = b_ref.shape
    y = _dot(f_ref[...], b_ref[...].reshape(two * n * g, c)).reshape(u_ref.shape)
    o_out[...] = m_ref[...] * (y + u_ref[...] * skip_ref[...])


def _fc_call(b5, finv, u5, m5, skip_row):
    _, p, hn, n, c = u5.shape
    g = SUBLANES
    blk = pl.BlockSpec((2, 1, hn, g, c), lambda q, j: (0, q, 0, j, 0))
    return pl.pallas_call(
        _fc_kernel,
        grid=(p, n // g),
        in_specs=[pl.BlockSpec((1, 2, n, g, c), lambda q, j: (q, 0, 0, j, 0)), _full(finv.shape), blk, blk,
                  _full(skip_row.shape)],
        out_specs=blk,
        out_shape=jax.ShapeDtypeStruct(u5.shape, F32),
        compiler_params=_params("arbitrary", "arbitrary"),
        name="fc",
    )(b5, finv, u5, m5, skip_row)


def _dft_tables(n):
    hn = n // 2
    k = np.arange(n)[:, None]
    ang = -2.0 * np.pi * (k * np.arange(n)[None, :] % n) / n
    fr, fi = np.cos(ang), np.sin(ang)
    f_data = np.block([[fr[:, :hn], -fi[:, :hn]], [fi[:, :hn], fr[:, :hn]]])
    f_filt = np.concatenate([fr, fi], axis=0)
    er, ei = fr[:hn], -fi[:hn]
    f_inv = np.block([[er, -ei], [ei, er]]) / float(n * n)
    k1 = jnp.arange(n, dtype=jnp.int32)[:, None, None]
    k2 = jnp.arange(n, dtype=jnp.int32)[None, :, None]
    m2 = jnp.arange(n, dtype=jnp.int32)[None, None, :]
    idx = (m2 * (k1 + n * k2)) % (n * n)
    ang2 = idx.astype(F32) * (-2.0 * math.pi / (n * n))
    gr, gi = jnp.cos(ang2), jnp.sin(ang2)
    g = jnp.concatenate([jnp.concatenate([gr, -gi], axis=2), jnp.concatenate([gi, gr], axis=2)], axis=1)
    h = jnp.swapaxes(g, 1, 2)

    def widen(f):
        return jnp.kron(jnp.asarray(f, F32), jnp.eye(SUBLANES, dtype=F32)).astype(BF16)

    return widen(f_data), widen(f_filt), widen(f_inv), g.astype(BF16), h.astype(BF16)


def _hyena_filter_tables(seq):
    t = jnp.linspace(0.0, 1.0, seq, dtype=F32)[:, None]
    w = 2.0 * math.pi * jnp.arange(seq, dtype=F32)[:, None] / seq
    f = jnp.linspace(1e-4, HY_BANDS - 1, HY_BANDS, dtype=F32)[None, :]
    emb = jnp.concatenate([t, jnp.cos(f * w), -jnp.sin(f * w)], axis=-1)
    pos = jnp.concatenate([jnp.arange(seq), jnp.array([0]), jnp.arange(seq - 1, 0, -1)])
    emb = jnp.pad(emb[pos], ((0, 0), (0, HY_EMB_PAD - HY_EMB)))
    deltas = jnp.abs(jnp.linspace(math.log(HY_DECAY_TARGET) / HY_SLOW_DECAY,
                                  math.log(HY_DECAY_TARGET) / HY_FAST_DECAY, HY_WIDTH, dtype=F32))
    return emb, jnp.tile(deltas, HY_ORDER)[None, :]


def _hyena(hv, hx1, hx2, w1, b1, w2, b2, w3, freq, skip, kb):
    bsz, seq, c = hv.shape
    n = int(round(math.sqrt(2 * seq)))
    assert n * n == 2 * seq and bsz % 2 == 0
    hn, p = n // 2, bsz // 2
    f_data, f_filt, f_inv, gmat, hmat = _dft_tables(n)

    emb, deltas2 = _hyena_filter_tables(seq)
    w1p = jnp.pad(w1, ((0, HY_EMB_PAD - HY_EMB), (0, 0)))
    w3r = w3.reshape(w3.shape[0], HY_ORDER, 2, c)
    w3sel = jnp.stack([w3r[:, :, 0, :].reshape(-1, HY_ORDER * c), w3r[:, :, 1, :].reshape(-1, HY_ORDER * c)])
    full, asum = _filter_call(emb, w1p, b1[None], w2, b2[None], w3sel, freq[None], deltas2, seq, min(512, seq))
    c2 = HY_ORDER * c
    kf = _fb_call(_fa_call(full.reshape(2, 1, hn, n, c2), f_filt), gmat, asum)

    def view(t):
        return t.reshape(2, p, hn, n, c)

    def long_conv(u5, m5, order):
        bm = _mid_call(_fa_call(u5, f_data), gmat, hmat, kf, order, kb)
        return _fc_call(bm, f_inv, u5, m5, skip[order][None, :])

    z = long_conv(view(hv), view(hx1), 0)
    return long_conv(z, view(hx2), 1).reshape(bsz, seq, c)


def _merge_kernel(x_ref, at_ref, hy_ref, g_ref, mod_ref, wba_ref, wbh_ref, wo_ref, o_ref):
    d = x_ref.shape[2]
    g = g_ref[0].astype(F32)
    y = (jax.nn.sigmoid(g[:, :d]) * _dot(at_ref[0], wba_ref[...])
         + jax.nn.sigmoid(g[:, d:]) * _dot(hy_ref[0], wbh_ref[...]))
    o_ref[0] = x_ref[0] + mod_ref[0, 2:3, :] * _dot(y, wo_ref[...])


def _merge_call(x, attn, hy, gate, mod, wba, wbh, wo, tm):
    bsz, s, d = x.shape

    def tok(w):
        return pl.BlockSpec((1, tm, w), lambda b, i: (b, i, 0))

    return pl.pallas_call(
        _merge_kernel,
        grid=(bsz, s // tm),
        in_specs=[tok(d), tok(attn.shape[2]), tok(hy.shape[2]), tok(2 * d),
                  pl.BlockSpec((1, SUBLANES, d), lambda b, i: (b, 0, 0)),
                  _full(wba.shape), _full(wbh.shape), _full(wo.shape)],
        out_specs=tok(d),
        out_shape=jax.ShapeDtypeStruct((bsz, s, d), F32),
        compiler_params=_params("arbitrary", "arbitrary"),
        name="merge",
    )(x, attn, hy, gate, mod, wba, wbh, wo)


def _route_kernel(xm_ref, mod_ref, nf_ref, wrt_ref, rb_ref, tri_ref, lt_ref, h2_out, w_out, s_out, cnt_out):
    tm = xm_ref.shape[1]
    ng, gs = N_GROUPS, GROUP_SIZE

    h2 = _prenorm(xm_ref[0], mod_ref, 3, nf_ref[...])
    h2_out[0] = h2.astype(h2_out.dtype)
    scores = jax.nn.sigmoid(_dot3(wrt_ref[...], h2, NT_DIMS))
    sel = scores + rb_ref[...]
    slabs = [sel[ng * j:ng * (j + 1)] for j in range(gs)]
    sc_slabs = [scores[ng * j:ng * (j + 1)] for j in range(gs)]

    top1 = jnp.full((ng, tm), -jnp.inf, F32)
    top2 = top1
    for x in slabs:
        top2 = jnp.maximum(top2, jnp.minimum(top1, x))
        top1 = jnp.maximum(top1, x)
    gscore = top1 + top2
    gid = lax.broadcasted_iota(jnp.int32, (ng, 1), 0)
    rank = jnp.zeros((ng, tm), jnp.int32)
    for g2 in range(ng):
        row = gscore[g2:g2 + 1]
        beats = (row > gscore) | ((row == gscore) & (g2 < gid))
        rank = rank + beats.astype(jnp.int32)
    gmask = rank < TOPK_GROUPS

    cand = [jnp.where(gmask, x, -jnp.inf) for x in slabs]
    eid = [gid * gs + j for j in range(gs)]
    chosen = []
    for _ in range(TOP_K):
        best = functools.reduce(jnp.maximum, cand)
        best = jnp.max(best, axis=0, keepdims=True)
        idx = functools.reduce(jnp.minimum, [jnp.where(cand[j] == best, eid[j], N_EXPERTS) for j in range(gs)])
        idx = jnp.min(idx, axis=0, keepdims=True)
        chosen.append(idx)
        cand = [jnp.where(eid[j] == idx, -jnp.inf, cand[j]) for j in range(gs)]

    hit = [[eid[j] == idx for j in range(gs)] for idx in chosen]
    mask = [functools.reduce(jnp.logical_or, [hit[k][j] for k in range(TOP_K)]) for j in range(gs)]
    maskf = jnp.concatenate([m.astype(F32) for m in mask], axis=0)
    before = jnp.dot(maskf.astype(BF16), tri_ref[...], preferred_element_type=F32)
    cnt = jnp.sum(maskf, axis=1, keepdims=True)
    cnt_out[0] = jnp.broadcast_to(cnt, cnt_out.shape[1:])
    units = jnp.broadcast_to(jnp.floor((cnt + (RUN_ALIGN - 1)) * (1.0 / RUN_ALIGN)), (N_EXPERTS, LANES))
    start = RUN_ALIGN * jnp.dot(lt_ref[...], units.astype(BF16), preferred_element_type=F32)[:, 0:1]
    before = before + start
    bslabs = [before[ng * j:ng * (j + 1)] for j in range(gs)]

    def pick(k, vals):
        tot = functools.reduce(jnp.add, [jnp.where(hit[k][j], vals[j], 0.0) for j in range(gs)])
        return jnp.sum(tot, axis=0, keepdims=True)

    wk = [pick(k, sc_slabs) for k in range(TOP_K)]
    wsum = functools.reduce(jnp.add, wk)
    w_out[...] = jnp.concatenate(wk, axis=0) / wsum * ROUTE_SCALE
    s_out[...] = jnp.concatenate([pick(k, bslabs) for k in range(TOP_K)], axis=0).astype(jnp.int32)


def _route_call(xm, mod, norm_ffn, wrt, rbias, lower, tm):
    bsz, s, d = xm.shape
    t = bsz * s
    nt = s // tm
    tri = (jnp.arange(tm)[:, None] < jnp.arange(tm)[None, :]).astype(BF16)
    tok = pl.BlockSpec((TOP_K, tm), lambda i: (0, i))
    return pl.pallas_call(
        _route_kernel,
        grid=(t // tm,),
        in_specs=[pl.BlockSpec((1, tm, d), lambda i: (i // nt, i % nt, 0)),
                  pl.BlockSpec((1, SUBLANES, d), lambda i: (i // nt, 0, 0)),
                  _full(norm_ffn.shape), _full(wrt.shape), _full(rbias.shape), _full(tri.shape),
                  _full(lower.shape)],
        out_specs=[pl.BlockSpec((1, tm, d), lambda i: (i // nt, i % nt, 0)), tok, tok,
                   pl.BlockSpec((1, N_EXPERTS, LANES), lambda i: (i, 0, 0))],
        out_shape=[jax.ShapeDtypeStruct((bsz, s, d), BF16), jax.ShapeDtypeStruct((TOP_K, t), F32),
                   jax.ShapeDtypeStruct((TOP_K, t), jnp.int32),
                   jax.ShapeDtypeStruct((t // tm, N_EXPERTS, LANES), F32)],
        compiler_params=_params("arbitrary"),
        name="route",
    )(xm, mod, norm_ffn, wrt, rbias, tri, lower)


def _pack(x):
    w = x.shape[1] // 2
    lo = lax.bitcast_convert_type(x[:, :w].astype(BF16).astype(F32), jnp.uint32)
    hi = lax.bitcast_convert_type(x[:, w:].astype(BF16).astype(F32), jnp.uint32)
    return hi | (lo >> 16)


def _unpack(u):
    lo = lax.bitcast_convert_type(u << 16, F32).astype(BF16)
    hi = lax.bitcast_convert_type(u & jnp.uint32(0xFFFF0000), F32).astype(BF16)
    return lo, hi


def _pow2_pieces(units, limit):
    bit = 1
    while bit * 2 <= limit:
        bit *= 2
    while bit:
        yield (units & bit) != 0, units & ~(2 * bit - 1), bit
        bit //= 2


def _rows_copy(vm_ref, hbm_ref, sem, vm_row, hbm_row, rows, to_hbm):
    v = vm_ref.at[pl.ds(pl.multiple_of(vm_row, RUN_ALIGN), rows), :]
    h = hbm_ref.at[pl.ds(pl.multiple_of(hbm_row, RUN_ALIGN), rows), :]
    return pltpu.make_async_copy(v, h, sem) if to_hbm else pltpu.make_async_copy(h, v, sem)


def _run_copies(vm_ref, hbm_ref, sem, n8, vm_row, hbm_row, limit, to_hbm, act):
    def emit(pieces):
        for on, off, size in pieces:
            @pl.when(on)
            def _():
                act(_rows_copy(vm_ref, hbm_ref, sem, vm_row + RUN_ALIGN * off, hbm_row + RUN_ALIGN * off,
                               RUN_ALIGN * size, to_hbm))

    pieces = list(_pow2_pieces(n8, limit))
    long_pieces = [p for p in pieces if p[2] >= LONG_RUN]
    if long_pieces:
        pl.when(n8 >= LONG_RUN)(lambda: emit(long_pieces))
    emit([p for p in pieces if p[2] < LONG_RUN])


def _wait_rows(vm_ref, hbm_ref, sem, units, limit, to_hbm):
    for on, _, size in _pow2_pieces(units, limit):
        @pl.when(on)
        def _():
            _rows_copy(vm_ref, hbm_ref, sem, 0, 0, RUN_ALIGN * size, to_hbm).wait()


def _dispatch_kernel(n8_ref, ls_ref, gs_ref, ts_ref, t8_ref, nu_ref, slot_ref, h_ref, xs_out, srt, zbuf, sem):
    step = pl.program_id(0)
    tm = h_ref.shape[0]
    rows = srt.shape[0]
    rid = lax.broadcasted_iota(jnp.int32, (rows, 1), 0)
    sel = functools.reduce(jnp.logical_or, [rid == slot_ref[k:k + 1, :] for k in range(TOP_K)])
    srt[...] = _pack(jnp.dot(jnp.where(sel, 1.0, 0.0).astype(BF16), h_ref[...], preferred_element_type=F32))

    def runs(act):
        def body(e, c):
            i = step * N_EXPERTS + e
            _run_copies(srt, xs_out, sem, n8_ref[i], ls_ref[i], gs_ref[i], tm // RUN_ALIGN, True, act)
            return c
        lax.fori_loop(0, N_EXPERTS, body, 0)

    runs(lambda cp: cp.start())
    last = step * N_EXPERTS + N_EXPERTS - 1
    _wait_rows(srt, xs_out, sem, ls_ref[last] // RUN_ALIGN + n8_ref[last], rows // RUN_ALIGN, True)

    @pl.when(step == pl.num_programs(0) - 1)
    def _():
        zbuf[...] = jnp.zeros(zbuf.shape, zbuf.dtype)
        nblk = xs_out.shape[0] // EXPERT_BLOCK

        def fill(act):
            def tails(e, c):
                _run_copies(zbuf, xs_out, sem, t8_ref[e], 0, ts_ref[e], EXPERT_BLOCK // RUN_ALIGN - 1, True, act)
                return c

            def blocks(b, c):
                act(pltpu.make_async_copy(
                    zbuf, xs_out.at[pl.ds(pl.multiple_of(b * EXPERT_BLOCK, EXPERT_BLOCK), EXPERT_BLOCK), :], sem))
                return c

            lax.fori_loop(0, N_EXPERTS, tails, 0)
            lax.fori_loop(nu_ref[0], nblk, blocks, 0)

        fill(lambda cp: cp.start())
        fill(lambda cp: cp.wait())


def _dispatch_call(tables, slot_kt, h2, nblk, tm):
    t, d = h2.shape
    lrows = TOP_K * tm + N_EXPERTS * RUN_ALIGN
    return pl.pallas_call(
        _dispatch_kernel,
        grid_spec=pltpu.PrefetchScalarGridSpec(
            num_scalar_prefetch=len(tables), grid=(t // tm,),
            in_specs=[pl.BlockSpec((TOP_K, tm), lambda i, *_: (0, i)), pl.BlockSpec((tm, d), lambda i, *_: (i, 0))],
            out_specs=pl.BlockSpec(memory_space=pl.ANY),
            scratch_shapes=[pltpu.VMEM((lrows, d // 2), jnp.uint32), pltpu.VMEM((EXPERT_BLOCK, d // 2), jnp.uint32),
                            pltpu.SemaphoreType.DMA(())]),
        out_shape=jax.ShapeDtypeStruct((nblk * EXPERT_BLOCK, d // 2), jnp.uint32),
        compiler_params=_params("arbitrary"),
        name="dispatch",
    )(*tables, slot_kt, h2)


def _expert_kernel(blk_ref, nused_ref, x_ref, wgu_ref, wd_ref, y_ref):
    used = pl.program_id(0) < nused_ref[0]

    @pl.when(used)
    def _():
        lo, hi = _unpack(x_ref[...])
        half = lo.shape[1]
        gu = (jnp.dot(lo, wgu_ref[0, :half, :], preferred_element_type=F32)
              + jnp.dot(hi, wgu_ref[0, half:, :], preferred_element_type=F32))
        a = _silu(gu[:, :EXPERT_FF]) * gu[:, EXPERT_FF:]
        y_ref[...] = _pack(_dot(a, wd_ref[0]))

    @pl.when(jnp.logical_not(used))
    def _():
        y_ref[...] = jnp.zeros(y_ref.shape, y_ref.dtype)


def _expert_call(blk_e, nused, xs, wgu, wd):
    rows, d = xs.shape
    nblk = rows // EXPERT_BLOCK

    def row_map(i, blk, nu):
        return (jnp.minimum(i, nu[0] - 1), 0)

    return pl.pallas_call(
        _expert_kernel,
        grid_spec=pltpu.PrefetchScalarGridSpec(
            num_scalar_prefetch=2, grid=(nblk,),
            in_specs=[pl.BlockSpec((EXPERT_BLOCK, d), row_map),
                      pl.BlockSpec((1,) + wgu.shape[1:], lambda i, blk, nu: (blk[i], 0, 0)),
                      pl.BlockSpec((1,) + wd.shape[1:], lambda i, blk, nu: (blk[i], 0, 0))],
            out_specs=pl.BlockSpec((EXPERT_BLOCK, d), lambda i, blk, nu: (i, 0))),
        out_shape=jax.ShapeDtypeStruct((rows, d), jnp.uint32),
        compiler_params=_params("arbitrary"),
        name="expert",
    )(blk_e, nused, xs, wgu, wd)


def _combine_kernel(n8_ref, ls_ref, gs_ref, ys_hbm, slot_ref, w_ref, xm_ref, h_ref, mod_ref, wsgu_ref, wsd_ref,
                    fn_ref, o_ref, ybuf, sem):
    step = pl.program_id(0)
    tm = xm_ref.shape[0]
    rows = ybuf.shape[0]

    def runs(act):
        def body(e, c):
            i = step * N_EXPERTS + e
            _run_copies(ybuf, ys_hbm, sem, n8_ref[i], ls_ref[i], gs_ref[i], tm // RUN_ALIGN, False, act)
            return c
        lax.fori_loop(0, N_EXPERTS, body, 0)

    runs(lambda cp: cp.start())
    gu = _dot(h_ref[...], wsgu_ref[...])
    ff = gu.shape[1] // 2
    shared = _dot(_silu(gu[:, :ff]) * gu[:, ff:], wsd_ref[...])
    cid = lax.broadcasted_iota(jnp.int32, (1, rows), 1)
    slot = slot_ref[...]
    w = w_ref[...]
    mix = functools.reduce(jnp.add, [jnp.where(slot[:, k:k + 1] == cid, w[:, k:k + 1], 0.0) for k in range(TOP_K)])
    mix = mix.astype(BF16)
    last = step * N_EXPERTS + N_EXPERTS - 1
    filled = ls_ref[last] + RUN_ALIGN * n8_ref[last]
    _wait_rows(ybuf, ys_hbm, sem, filled // RUN_ALIGN, rows // RUN_ALIGN, False)
    rid = lax.broadcasted_iota(jnp.int32, (rows, 1), 0)
    lo, hi = _unpack(jnp.where(rid < filled, ybuf[...], jnp.uint32(0)))
    routed = jnp.concatenate([jnp.dot(mix, lo, preferred_element_type=F32),
                              jnp.dot(mix, hi, preferred_element_type=F32)], axis=1)
    x = xm_ref[...] + mod_ref[0, 5:6, :] * (routed + shared)
    o_ref[...] = _rms(x, fn_ref[...])


def _combine_call(tables, ys, slot_tk, w_tk, xm, h2, mod, wsgu, wsd, final_norm, tm, tiles_per_batch):
    t, d = xm.shape
    lrows = TOP_K * tm + N_EXPERTS * RUN_ALIGN
    tok = pl.BlockSpec((tm, d), lambda i, *_: (i, 0))
    per_k = pl.BlockSpec((tm, TOP_K), lambda i, *_: (i, 0))
    return pl.pallas_call(
        _combine_kernel,
        grid_spec=pltpu.PrefetchScalarGridSpec(
            num_scalar_prefetch=len(tables), grid=(t // tm,),
            in_specs=[pl.BlockSpec(memory_space=pl.ANY), per_k, per_k, tok, tok,
                      pl.BlockSpec((1, SUBLANES, d), lambda i, *_: (i // tiles_per_batch, 0, 0)),
                      _full(wsgu.shape), _full(wsd.shape), _full(final_norm.shape)],
            out_specs=tok,
            scratch_shapes=[pltpu.VMEM((lrows, d // 2), jnp.uint32), pltpu.SemaphoreType.DMA(())]),
        out_shape=jax.ShapeDtypeStruct((t, d), F32),
        compiler_params=_params("arbitrary"),
        name="combine",
    )(*tables, ys, slot_tk, w_tk, xm, h2, mod, wsgu, wsd, final_norm)


def _moe(xm, mod, norm_ffn, w_router, router_bias, wg, wu, wd, wsg, wsu, wsd, final_norm, tm):
    bsz, s, d = xm.shape
    t = bsz * s
    nt = t // tm
    perm = (np.arange(N_EXPERTS) % N_GROUPS) * GROUP_SIZE + np.arange(N_EXPERTS) // N_GROUPS
    wrt = w_router.T[perm]
    rbias = router_bias[perm][:, None]
    lower = jnp.asarray(perm[None, :] < perm[:, None], BF16)
    h2, w_kt, slot_kt, cnt = _route_call(xm, mod, norm_ffn, wrt, rbias, lower, tm)

    inv = np.argsort(perm)
    n8 = (cnt[:, :, 0].astype(jnp.int32)[:, inv] + (RUN_ALIGN - 1)) // RUN_ALIGN
    run = RUN_ALIGN * n8
    ls = jnp.cumsum(run, axis=1) - run
    tot = jnp.sum(run, axis=0)
    padded = (tot + EXPERT_BLOCK - 1) // EXPERT_BLOCK * EXPERT_BLOCK
    pad_end = jnp.cumsum(padded)
    gs = (pad_end - padded)[None, :] + jnp.cumsum(run, axis=0) - run
    nblk = -(-(t * TOP_K + nt * N_EXPERTS * (RUN_ALIGN - 1)) // EXPERT_BLOCK) + N_EXPERTS
    blk_first = jnp.arange(nblk, dtype=jnp.int32)[:, None] * EXPERT_BLOCK
    blk_e = jnp.minimum(jnp.sum((pad_end[None, :] <= blk_first).astype(jnp.int32), axis=1), N_EXPERTS - 1)
    nused = (pad_end[-1:] // EXPERT_BLOCK).astype(jnp.int32)
    tables = [a.reshape(-1).astype(jnp.int32) for a in (n8, ls, gs)]
    tails = [(pad_end - padded + tot).astype(jnp.int32), ((padded - tot) // RUN_ALIGN).astype(jnp.int32), nused]

    h2f = h2.reshape(t, d)
    xs = _dispatch_call(tables + tails, slot_kt, h2f, nblk, tm)
    wgu = jnp.concatenate([wg, wu], axis=2).astype(BF16)
    ys = _expert_call(blk_e, nused, xs, wgu, wd.astype(BF16))
    wsgu = jnp.concatenate([wsg, wsu], axis=1).astype(BF16)
    out = _combine_call(tables, ys, slot_kt.T, w_kt.T, xm.reshape(t, d), h2f, mod, wsgu, wsd.astype(BF16),
                        final_norm, tm, s // tm)
    return out.reshape(bsz, s, d)


def _rope_tables(s):
    rows = s // GRID_W
    row = jnp.broadcast_to(jnp.arange(rows, dtype=F32)[:, None], (rows, GRID_W)).reshape(-1)
    col = jnp.broadcast_to(jnp.arange(GRID_W, dtype=F32)[None, :], (rows, GRID_W)).reshape(-1)
    half = QK_ROPE // 2
    inv_freq = ROPE_THETA ** (-jnp.arange(0, half, 2, dtype=F32) / half)
    ar, ac = row[:, None] * inv_freq, col[:, None] * inv_freq
    ones = jnp.ones((s, QK_NOPE), F32)
    tail = HEAD_PAD - QK_NOPE - QK_ROPE
    cos_t = jnp.concatenate([ones, jnp.cos(ar), jnp.cos(ar), jnp.cos(ac), jnp.cos(ac), jnp.ones((s, tail), F32)], 1)
    sin_t = jnp.concatenate([0 * ones, -jnp.sin(ar), jnp.sin(ar), -jnp.sin(ac), jnp.sin(ac),
                             jnp.zeros((s, tail), F32)], 1)
    return cos_t, sin_t


_Q4 = QK_ROPE // 4
ROPE_SWAP = np.concatenate([np.arange(_Q4, 2 * _Q4), np.arange(0, _Q4), np.arange(3 * _Q4, 4 * _Q4),
                            np.arange(2 * _Q4, 3 * _Q4)])


def _rope_slot(w, swap):
    if swap:
        w = w[..., ROPE_SWAP]
    pad = [(0, 0)] * (w.ndim - 1) + [(QK_NOPE, HEAD_PAD - QK_NOPE - QK_ROPE)]
    return jnp.pad(w, pad)


def kernel(x, c, ctx, c_ctx, w_mod, b_mod, norm_mix, norm_ffn, w_in, b_in, q_norm, w_uq, kv_norm, w_ukv, w_branch_attn, hy_conv_w, hy_conv_b, hy_filt_w1, hy_filt_b1, hy_filt_w2, hy_filt_b2, hy_filt_w3, hy_filt_freq, hy_skip, w_branch_hyena, w_out, w_router, router_bias, w_exp_gate, w_exp_up, w_exp_down, w_sh_gate, w_sh_up, w_sh_down, final_norm,
           tiles=None):
    bsz, s, d = x.shape
    tl = dict(inproj=256, tq=1024, tk=1408, fft_kb=4, merge=512, moe=256)
    tl.update(tiles or {})
    assert w_mod.shape[0] == 1, "single-layer trunk"
    i = 0

    rows = -(-(bsz + 1) // SUBLANES) * SUBLANES
    c_rows = jnp.pad(jnp.concatenate([c, c_ctx[None]], axis=0), ((0, rows - bsz - 1), (0, 0)))
    mod_all = _mod_call(c_rows, w_mod[i], b_mod[i])
    mod_all = jnp.pad(mod_all.reshape(rows, 6, d), ((0, 0), (0, SUBLANES - 6), (0, 0)))
    mod, modc = mod_all[:bsz], mod_all[bsz:bsz + 1]

    cuts = np.cumsum([Q_LORA, KV_LORA, QK_ROPE, 3 * HY_WIDTH])
    wi, bi = w_in[i], b_in[i][None]
    w_q, w_kv, w_pe, w_hy, w_g = jnp.split(wi, cuts, axis=1)
    b_q, b_kv, b_pe, b_hy, b_g = jnp.split(bi, cuts, axis=1)
    wa = jnp.concatenate([w_q, w_kv, _rope_slot(w_pe, False), _rope_slot(w_pe, True)], axis=1).astype(BF16)
    ba = jnp.concatenate([b_q, b_kv, _rope_slot(b_pe, False), _rope_slot(b_pe, True)], axis=1)
    wq3 = w_uq[i].reshape(Q_LORA, N_HEADS, QK_NOPE + QK_ROPE) * (ATTN_SCALE * math.log2(math.e))
    tail = ((0, 0), (0, 0), (0, HEAD_PAD - QK_NOPE))
    wuq = (jnp.pad(wq3[..., :QK_NOPE], tail) + _rope_slot(wq3[..., QK_NOPE:], False)).reshape(Q_LORA, -1).astype(BF16)
    wuqs = _rope_slot(wq3[..., QK_NOPE:], True).reshape(Q_LORA, -1).astype(BF16)
    wkv3 = w_ukv[i].reshape(KV_LORA, N_HEADS, QK_NOPE + V_HEAD)
    wuk = jnp.pad(wkv3[..., :QK_NOPE], tail).reshape(KV_LORA, -1).astype(BF16)
    wuvt = wkv3[..., QK_NOPE:].reshape(KV_LORA, -1).T.astype(BF16)
    nm, qn, kvn = norm_mix[i][None], q_norm[i][None], kv_norm[i][None]

    w_c = jnp.concatenate([w_kv, _rope_slot(w_pe, False)], axis=1).astype(BF16)
    b_c = jnp.concatenate([b_kv, _rope_slot(b_pe, False)], axis=1)
    ck, cvt = _ctx_call(ctx, modc, nm, w_c, b_c, kvn, wuk, wuvt)

    cos_t, sin_t = _rope_tables(s)
    q, k, vt, hv, hx1, hx2, gate = _inproj_call(
        x, mod, nm, wa, ba, w_hy.astype(BF16), b_hy, w_g.astype(BF16), b_g, qn, wuq, wuqs, kvn, wuk, wuvt,
        cos_t, sin_t, hy_conv_w[i], hy_conv_b[i][None], tl["inproj"])

    attn = _attn_call(q, jnp.concatenate([ck, k], axis=2), jnp.concatenate([cvt, vt], axis=3), tl["tq"], tl["tk"])
    hy = _hyena(hv, hx1, hx2, hy_filt_w1[i], hy_filt_b1[i], hy_filt_w2[i], hy_filt_b2[i], hy_filt_w3[i],
                hy_filt_freq[i], hy_skip[i], tl["fft_kb"])
    xm = _merge_call(x, attn, hy, gate, mod, w_branch_attn[i].astype(BF16), w_branch_hyena[i].astype(BF16),
                     w_out[i].astype(BF16), tl["merge"])
    return _moe(xm, mod, norm_ffn[i][None], w_router[i], router_bias[i], w_exp_gate[i], w_exp_up[i], w_exp_down[i],
                w_sh_gate[i], w_sh_up[i], w_sh_down[i], final_norm[None], tl["moe"])
```

```python
import functools
import math

import numpy as np
import jax
import jax.numpy as jnp
from jax import lax
from jax.experimental import pallas as pl
from jax.experimental.pallas import tpu as pltpu

GRID_W = 64
N_HEADS = 8
QK_NOPE = 64
QK_ROPE = 32
V_HEAD = 64
Q_LORA = 256
KV_LORA = 128
ROPE_THETA = 10000.0
ATTN_SCALE = 1.0 / math.sqrt(QK_NOPE + QK_ROPE)
HY_WIDTH = 512
HY_ORDER = 2
HY_SHORT = 3
HY_BANDS = 8
HY_EMB = 1 + 2 * HY_BANDS
HY_EMB_PAD = 32
HY_FAST_DECAY = 0.3
HY_SLOW_DECAY = 1.5
HY_DECAY_TARGET = 1e-2
N_EXPERTS = 64
N_GROUPS = 8
GROUP_SIZE = N_EXPERTS // N_GROUPS
TOPK_GROUPS = 4
TOP_K = 8
EXPERT_FF = 256
ROUTE_SCALE = 2.5
EXPERT_BLOCK = 512
RUN_ALIGN = 8
LONG_RUN = 8
NORM_EPS = 1e-6

HEAD_PAD = 128
Q_CHUNK = 512
AHEAD = 2
LANES = 128
SUBLANES = 8
VMEM_LIMIT = 48 * 1024 * 1024

F32 = jnp.float32
BF16 = jnp.bfloat16
NT_DIMS = (((1,), (1,)), ((), ()))
NN_DIMS = (((1,), (0,)), ((), ()))


def _params(*sem):
    return pltpu.CompilerParams(dimension_semantics=sem, vmem_limit_bytes=VMEM_LIMIT)


def _dot(a, b):
    return jnp.dot(a.astype(BF16), b.astype(BF16), preferred_element_type=F32)


def _split(a):
    hi = a.astype(BF16)
    lo = (a - hi.astype(F32)).astype(BF16)
    return hi, lo


def _dot3(a, b, dims=NN_DIMS):
    ah, al = _split(a)
    bh, bl = _split(b)
    d = functools.partial(lax.dot_general, dimension_numbers=dims, preferred_element_type=F32)
    return d(ah, bh) + (d(ah, bl) + d(al, bh))


def _rms(x, g):
    return x * lax.rsqrt(jnp.mean(x * x, axis=-1, keepdims=True) + NORM_EPS) * g


def _silu(x):
    return x * jax.nn.sigmoid(x)


def _full(shape):
    nd = len(shape)
    return pl.BlockSpec(shape, lambda *_: (0,) * nd)


def _mod_kernel(c_ref, w_ref, b_ref, o_ref):
    o_ref[...] = _dot3(_silu(c_ref[...]), w_ref[...]) + b_ref[...]


def _mod_call(c_rows, w_mod, b_mod):
    r, d = c_rows.shape
    n = w_mod.shape[1]
    bn = 1024
    return pl.pallas_call(
        _mod_kernel,
        grid=(n // bn,),
        in_specs=[_full((r, d)), pl.BlockSpec((d, bn), lambda j: (0, j)), pl.BlockSpec((1, bn), lambda j: (0, j))],
        out_specs=pl.BlockSpec((r, bn), lambda j: (0, j)),
        out_shape=jax.ShapeDtypeStruct((r, n), F32),
        compiler_params=_params("arbitrary"),
        name="mod",
    )(c_rows, w_mod, b_mod.reshape(1, n))


def _prenorm(x, mod_ref, row, g):
    shift = mod_ref[0, row:row + 1, :]
    scale = mod_ref[0, row + 1:row + 2, :]
    return _rms(x, g) * (1.0 + scale) + shift


def _kv_heads(kv_lat, kpe, kvn_ref, wuk_ref, wuvt_ref, k_out, vt_out):
    kvn = _rms(kv_lat, kvn_ref[...]).astype(BF16)
    kk = _dot(kvn, wuk_ref[...])
    vt = lax.dot_general(wuvt_ref[...], kvn, NT_DIMS, preferred_element_type=F32)
    ones = jnp.ones((HEAD_PAD - V_HEAD, vt.shape[1]), F32)
    for h in range(N_HEADS):
        k_out[0, h] = (kk[:, HEAD_PAD * h:HEAD_PAD * (h + 1)] + kpe).astype(BF16)
        vt_out[0, h] = jnp.concatenate([vt[V_HEAD * h:V_HEAD * (h + 1)], ones], axis=0).astype(BF16)


def _ctx_kernel(c_ref, mod_ref, nm_ref, w_ref, b_ref, kvn_ref, wuk_ref, wuv_ref, k_out, v_out):
    h = _prenorm(c_ref[0], mod_ref, 0, nm_ref[...]).astype(BF16)
    a = _dot(h, w_ref[...]) + b_ref[...]
    _kv_heads(a[:, :KV_LORA], a[:, KV_LORA:], kvn_ref, wuk_ref, wuv_ref, k_out, v_out)


def _ctx_call(ctx, modc, norm_mix, w_c, b_c, kv_norm, w_uk, w_uv):
    bsz, n, d = ctx.shape
    return pl.pallas_call(
        _ctx_kernel,
        grid=(bsz,),
        in_specs=[pl.BlockSpec((1, n, d), lambda b: (b, 0, 0)), _full(modc.shape), _full(norm_mix.shape),
                  _full(w_c.shape), _full(b_c.shape), _full(kv_norm.shape), _full(w_uk.shape), _full(w_uv.shape)],
        out_specs=[pl.BlockSpec((1, N_HEADS, n, HEAD_PAD), lambda b: (b, 0, 0, 0)),
                   pl.BlockSpec((1, N_HEADS, HEAD_PAD, n), lambda b: (b, 0, 0, 0))],
        out_shape=[jax.ShapeDtypeStruct((bsz, N_HEADS, n, HEAD_PAD), BF16),
                   jax.ShapeDtypeStruct((bsz, N_HEADS, HEAD_PAD, n), BF16)],
        compiler_params=_params("arbitrary"),
        name="ctx",
    )(ctx, modc, norm_mix, w_c, b_c, kv_norm, w_uk, w_uv)


def _inproj_kernel(x_ref, xp_ref, xn_ref, mod_ref, nm_ref, wa_ref, ba_ref, why_ref, bhy_ref, wg_ref, bg_ref,
                   qn_ref, wuq_ref, wuqs_ref, kvn_ref, wuk_ref, wuv_ref, cos_ref, sin_ref, cw_ref, cb_ref,
                   q_out, k_out, v_out, hv_out, hx1_out, hx2_out, g_out):
    i = pl.program_id(0)
    tm = x_ref.shape[1]
    nm = nm_ref[...]
    h = _prenorm(x_ref[0], mod_ref, 0, nm).astype(BF16)
    a = _dot(h, wa_ref[...]) + ba_ref[...]
    q_lat = a[:, :Q_LORA]
    kv_lat = a[:, Q_LORA:Q_LORA + KV_LORA]
    kpe_m = a[:, Q_LORA + KV_LORA:Q_LORA + KV_LORA + HEAD_PAD]
    kpe_s = a[:, Q_LORA + KV_LORA + HEAD_PAD:]
    cos = cos_ref[...]
    sin = sin_ref[...]
    qn = _rms(q_lat, qn_ref[...]).astype(BF16)
    qa = _dot(qn, wuq_ref[...])
    qs = _dot(qn, wuqs_ref[...])
    for hh in range(N_HEADS):
        sl = slice(HEAD_PAD * hh, HEAD_PAD * (hh + 1))
        q_out[0, hh] = (qa[:, sl] * cos + qs[:, sl] * sin).astype(BF16)
    _kv_heads(kv_lat, kpe_m * cos + kpe_s * sin, kvn_ref, wuk_ref, wuv_ref, k_out, v_out)
    g_out[0] = (_dot(h, wg_ref[...]) + bg_ref[...]).astype(BF16)

    why = why_ref[...]
    bhy = bhy_ref[...]
    hy = _dot(h, why) + bhy
    hp = _dot(_prenorm(xp_ref[0], mod_ref, 0, nm).astype(BF16), why) + bhy
    hn = _dot(_prenorm(xn_ref[0], mod_ref, 0, nm).astype(BF16), why) + bhy
    prev = jnp.where(i == 0, 0.0, hp[SUBLANES - 1:SUBLANES])
    nxt = jnp.where(i == pl.num_programs(0) - 1, 0.0, hn[0:1])
    rid = lax.broadcasted_iota(jnp.int32, (tm, 1), 0)
    up = jnp.where(rid == 0, prev, pltpu.roll(hy, 1, 0))
    dn = jnp.where(rid == tm - 1, nxt, pltpu.roll(hy, tm - 1, 0))
    u = up * cw_ref[0:1, :] + hy * cw_ref[1:2, :] + dn * cw_ref[2:3, :] + cb_ref[...]
    hv_out[0] = u[:, :HY_WIDTH]
    hx1_out[0] = u[:, HY_WIDTH:2 * HY_WIDTH]
    hx2_out[0] = u[:, 2 * HY_WIDTH:]


def _inproj_call(x, mod, norm_mix, wa, ba, why, bhy, wg, bg, q_norm, wuq, wuqs, kv_norm, wuk, wuvt, cos_t, sin_t, cw,
                 cb, tm):
    bsz, s, d = x.shape
    nt = s // tm
    rb = tm // SUBLANES
    last_rb = s // SUBLANES - 1
    consts = [norm_mix, wa, ba, why, bhy, wg, bg, q_norm, wuq, wuqs, kv_norm, wuk, wuvt]
    in_specs = [
        pl.BlockSpec((1, tm, d), lambda i, b: (b, i, 0)),
        pl.BlockSpec((1, SUBLANES, d), lambda i, b: (b, jnp.maximum(i * rb - 1, 0), 0)),
        pl.BlockSpec((1, SUBLANES, d), lambda i, b: (b, jnp.minimum((i + 1) * rb, last_rb), 0)),
        pl.BlockSpec((1, SUBLANES, d), lambda i, b: (b, 0, 0)),
    ] + [_full(c.shape) for c in consts] + [
        pl.BlockSpec((tm, HEAD_PAD), lambda i, b: (i, 0)),
        pl.BlockSpec((tm, HEAD_PAD), lambda i, b: (i, 0)),
        _full(cw.shape), _full(cb.shape),
    ]
    hw = HY_WIDTH
    out_specs = [
        pl.BlockSpec((1, N_HEADS, tm, HEAD_PAD), lambda i, b: (b, 0, i, 0)),
        pl.BlockSpec((1, N_HEADS, tm, HEAD_PAD), lambda i, b: (b, 0, i, 0)),
        pl.BlockSpec((1, N_HEADS, HEAD_PAD, tm), lambda i, b: (b, 0, 0, i)),
        pl.BlockSpec((1, tm, hw), lambda i, b: (b, i, 0)),
        pl.BlockSpec((1, tm, hw), lambda i, b: (b, i, 0)),
        pl.BlockSpec((1, tm, hw), lambda i, b: (b, i, 0)),
        pl.BlockSpec((1, tm, 2 * d), lambda i, b: (b, i, 0)),
    ]
    out_shape = [
        jax.ShapeDtypeStruct((bsz, N_HEADS, s, HEAD_PAD), BF16),
        jax.ShapeDtypeStruct((bsz, N_HEADS, s, HEAD_PAD), BF16),
        jax.ShapeDtypeStruct((bsz, N_HEADS, HEAD_PAD, s), BF16),
        jax.ShapeDtypeStruct((bsz, s, hw), F32),
        jax.ShapeDtypeStruct((bsz, s, hw), F32),
        jax.ShapeDtypeStruct((bsz, s, hw), F32),
        jax.ShapeDtypeStruct((bsz, s, 2 * d), BF16),
    ]
    return pl.pallas_call(
        _inproj_kernel,
        grid=(nt, bsz),
        in_specs=in_specs,
        out_specs=out_specs,
        out_shape=out_shape,
        compiler_params=_params("arbitrary", "arbitrary"),
        name="inproj",
    )(x, x, x, mod, *consts, cos_t, sin_t, cw, cb)


def _attn_kernel(q_ref, k_ref, vt_ref, o_ref, m_sc, acc_sc):
    j = pl.program_id(2)

    @pl.when(j == 0)
    def _():
        m_sc[...] = jnp.full(m_sc.shape, -jnp.inf, F32)
        acc_sc[...] = jnp.zeros(acc_sc.shape, F32)

    tq = q_ref.shape[2]
    qw = min(tq, Q_CHUNK)
    units = [(h, c) for h in range(N_HEADS) for c in range(0, tq, qw)]

    def scores(u):
        h, c = units[u]
        return lax.dot_general(k_ref[0, h], q_ref[0, h, c:c + qw, :], NT_DIMS,
                               preferred_element_type=F32)

    pending = [scores(u) for u in range(AHEAD)]
    for u, (h, c) in enumerate(units):
        if u + AHEAD < len(units):
            pending.append(scores(u + AHEAD))
        st = pending.pop(0)
        m_prev = m_sc[h, :, c:c + qw]
        m_new = jnp.maximum(m_prev, jnp.max(st, axis=0, keepdims=True))
        pt = jnp.exp2(st - m_new).astype(BF16)
        acc_sc[h, :, c:c + qw] = (jnp.exp2(m_prev - m_new) * acc_sc[h, :, c:c + qw]
                                  + jnp.dot(vt_ref[0, h], pt, preferred_element_type=F32))
        m_sc[h, :, c:c + qw] = m_new

    @pl.when(j == pl.num_programs(2) - 1)
    def _():
        ot = jnp.concatenate([acc_sc[h, :V_HEAD] / acc_sc[h, V_HEAD:V_HEAD + 1] for h in range(N_HEADS)], axis=0)
        o_ref[0] = ot.T.astype(o_ref.dtype)


def _attn_call(q, k, vt, tq, tk):
    bsz, nh, s, dh = q.shape
    nk = k.shape[2]
    dv = nh * V_HEAD
    return pl.pallas_call(
        _attn_kernel,
        grid=(bsz, s // tq, nk // tk),
        in_specs=[
            pl.BlockSpec((1, nh, tq, dh), lambda b, i, j: (b, 0, i, 0)),
            pl.BlockSpec((1, nh, tk, dh), lambda b, i, j: (b, 0, j, 0)),
            pl.BlockSpec((1, nh, dh, tk), lambda b, i, j: (b, 0, 0, j)),
        ],
        out_specs=pl.BlockSpec((1, tq, dv), lambda b, i, j: (b, i, 0)),
        out_shape=jax.ShapeDtypeStruct((bsz, s, dv), BF16),
        scratch_shapes=[pltpu.VMEM((nh, 1, tq), F32), pltpu.VMEM((nh, dh, tq), F32)],
        compiler_params=_params("arbitrary", "arbitrary", "arbitrary"),
        name="attn",
    )(q, k, vt)


def _filter_kernel(emb_ref, w1_ref, b1_ref, w2_ref, b2_ref, w3_ref, fr_ref, dl_ref, full_out, asum_out, *, seq):
    r = pl.program_id(0)
    rb = emb_ref.shape[0]
    emb = emb_ref[...]
    fr = fr_ref[...]
    h = jnp.sin(fr * (_dot3(emb, w1_ref[...]) + b1_ref[...]))
    h = jnp.sin(fr * (_dot3(h, w2_ref[...]) + b2_ref[...]))
    k = _dot3(h, w3_ref[0]) * jnp.exp(-emb[:, 0:1] * dl_ref[...])
    row = r * rb + lax.broadcasted_iota(jnp.int32, (rb, 1), 0)
    k = jnp.where(row == seq, 0.0, k)
    full_out[...] = k

    @pl.when(r == 0)
    def _():
        asum_out[...] = jnp.zeros(asum_out.shape, F32)

    asum_out[...] += jnp.sum(jnp.abs(k), axis=0, keepdims=True)


def _filter_call(emb, w1, b1, w2, b2, w3sel, freq, deltas2, seq, rb):
    n2 = emb.shape[0]
    half_blocks = seq // rb
    width = w3sel.shape[2]
    return pl.pallas_call(
        functools.partial(_filter_kernel, seq=seq),
        grid=(n2 // rb,),
        in_specs=[pl.BlockSpec((rb, HY_EMB_PAD), lambda r: (r, 0)), _full(w1.shape), _full(b1.shape),
                  _full(w2.shape), _full(b2.shape),
                  pl.BlockSpec((1,) + w3sel.shape[1:], lambda r: (r // half_blocks, 0, 0)),
                  _full(freq.shape), _full(deltas2.shape)],
        out_specs=[pl.BlockSpec((rb, width), lambda r: (r, 0)), pl.BlockSpec((1, width), lambda r: (0, 0))],
        out_shape=[jax.ShapeDtypeStruct((n2, width), F32), jax.ShapeDtypeStruct((1, width), F32)],
        compiler_params=_params("arbitrary"),
        name="filt",
    )(emb, w1, b1, w2, b2, w3sel, freq, deltas2)


def _fa_kernel(u_ref, f_ref, a_out):
    two, _, hn, g, c = u_ref.shape
    a = _dot(f_ref[...], u_ref[...].reshape(two * hn * g, c))
    a_out[...] = a.reshape(a_out.shape)


def _fa_call(u5, fmat):
    _, p, hn, n, c = u5.shape
    g = SUBLANES
    return pl.pallas_call(
        _fa_kernel,
        grid=(p, n // g),
        in_specs=[pl.BlockSpec((2, 1, hn, g, c), lambda q, j: (0, q, 0, j, 0)), _full(fmat.shape)],
        out_specs=pl.BlockSpec((1, 2, n, g, c), lambda q, j: (q, 0, 0, j, 0)),
        out_shape=jax.ShapeDtypeStruct((p, 2, n, n, c), F32),
        compiler_params=_params("arbitrary", "arbitrary"),
        name="fa",
    )(u5, fmat)


def _fb_kernel(a_ref, g_ref, asum_ref, kf_out):
    two, _, n, c = a_ref.shape[1:]
    a = a_ref[0].reshape(two * n, c)
    x = _dot(g_ref[0], a) / (asum_ref[...] + 1e-6)
    kf_out[0] = x.reshape(two, n, c)


def _fb_call(a5, gmat, asum):
    _, _, n, _, c = a5.shape
    return pl.pallas_call(
        _fb_kernel,
        grid=(n,),
        in_specs=[pl.BlockSpec((1, 2, 1, n, c), lambda k: (0, 0, k, 0, 0)),
                  pl.BlockSpec((1, 2 * n, 2 * n), lambda k: (k, 0, 0)), _full(asum.shape)],
        out_specs=pl.BlockSpec((1, 2, n, c), lambda k: (k, 0, 0, 0)),
        out_shape=jax.ShapeDtypeStruct((n, 2, n, c), F32),
        compiler_params=_params("arbitrary"),
        name="fb",
    )(a5, gmat, asum)


def _mid_kernel(a_ref, g_ref, h_ref, kf_ref, b_out):
    _, two, kb, n, c = a_ref.shape
    for kk in range(kb):
        x = _dot(g_ref[kk], a_ref[0, :, kk].reshape(two * n, c))
        xr, xi = x[:n], x[n:]
        kr, ki = kf_ref[kk, 0], kf_ref[kk, 1]
        y = jnp.concatenate([xr * kr - xi * ki, xr * ki + xi * kr], axis=0)
        b_out[0, :, kk] = _dot(h_ref[kk], y).reshape(two, n, c)


def _mid_call(a5, gmat, hmat, kf, order, kb):
    p, _, n, _, c = a5.shape
    return pl.pallas_call(
        _mid_kernel,
        grid=(n // kb, p),
        in_specs=[pl.BlockSpec((1, 2, kb, n, c), lambda k, q: (q, 0, k, 0, 0)),
                  pl.BlockSpec((kb, 2 * n, 2 * n), lambda k, q: (k, 0, 0)),
                  pl.BlockSpec((kb, 2 * n, 2 * n), lambda k, q: (k, 0, 0)),
                  pl.BlockSpec((kb, 2, n, c), lambda k, q: (k, 0, 0, order))],
        out_specs=pl.BlockSpec((1, 2, kb, n, c), lambda k, q: (q, 0, k, 0, 0)),
        out_shape=jax.ShapeDtypeStruct(a5.shape, F32),
        compiler_params=_params("arbitrary", "arbitrary"),
        name="mid",
    )(a5, gmat, hmat, kf)


def _fc_kernel(b_ref, f_ref, u_ref, m_ref, skip_ref, o_out):
    _, two, n, g, c = b_ref.shape
    y = _dot(f_ref[...], b_ref[...].reshape(two * n * g, c)).reshape(u_ref.shape)
    o_out[...] = m_ref[...] * (y + u_ref[...] * skip_ref[...])


def _fc_call(b5, finv, u5, m5, skip_row):
    _, p, hn, n, c = u5.shape
    g = SUBLANES
    blk = pl.BlockSpec((2, 1, hn, g, c), lambda q, j: (0, q, 0, j, 0))
    return pl.pallas_call(
        _fc_kernel,
        grid=(p, n // g),
        in_specs=[pl.BlockSpec((1, 2, n, g, c), lambda q, j: (q, 0, 0, j, 0)), _full(finv.shape), blk, blk,
                  _full(skip_row.shape)],
        out_specs=blk,
        out_shape=jax.ShapeDtypeStruct(u5.shape, F32),
        compiler_params=_params("arbitrary", "arbitrary"),
        name="fc",
    )(b5, finv, u5, m5, skip_row)


def _dft_tables(n):
    hn = n // 2
    k = np.arange(n)[:, None]
    ang = -2.0 * np.pi * (k * np.arange(n)[None, :] % n) / n
    fr, fi = np.cos(ang), np.sin(ang)
    f_data = np.block([[fr[:, :hn], -fi[:, :hn]], [fi[:, :hn], fr[:, :hn]]])
    f_filt = np.concatenate([fr, fi], axis=0)
    er, ei = fr[:hn], -fi[:hn]
    f_inv = np.block([[er, -ei], [ei, er]]) / float(n * n)
    k1 = jnp.arange(n, dtype=jnp.int32)[:, None, None]
    k2 = jnp.arange(n, dtype=jnp.int32)[None, :, None]
    m2 = jnp.arange(n, dtype=jnp.int32)[None, None, :]
    idx = (m2 * (k1 + n * k2)) % (n * n)
    ang2 = idx.astype(F32) * (-2.0 * math.pi / (n * n))
    gr, gi = jnp.cos(ang2), jnp.sin(ang2)
    g = jnp.concatenate([jnp.concatenate([gr, -gi], axis=2), jnp.concatenate([gi, gr], axis=2)], axis=1)
    h = jnp.swapaxes(g, 1, 2)

    def widen(f):
        return jnp.asarray(np.kron(f, np.eye(SUBLANES)), BF16)

    return widen(f_data), widen(f_filt), widen(f_inv), g.astype(BF16), h.astype(BF16)


def _hyena_filter_tables(seq):
    t = jnp.linspace(0.0, 1.0, seq, dtype=F32)[:, None]
    w = 2.0 * math.pi * jnp.arange(seq, dtype=F32)[:, None] / seq
    f = jnp.linspace(1e-4, HY_BANDS - 1, HY_BANDS, dtype=F32)[None, :]
    emb = jnp.concatenate([t, jnp.cos(f * w), -jnp.sin(f * w)], axis=-1)
    pos = jnp.concatenate([jnp.arange(seq), jnp.array([0]), jnp.arange(seq - 1, 0, -1)])
    emb = jnp.pad(emb[pos], ((0, 0), (0, HY_EMB_PAD - HY_EMB)))
    deltas = jnp.abs(jnp.linspace(math.log(HY_DECAY_TARGET) / HY_SLOW_DECAY,
                                  math.log(HY_DECAY_TARGET) / HY_FAST_DECAY, HY_WIDTH, dtype=F32))
    return emb, jnp.tile(deltas, HY_ORDER)[None, :]


def _hyena(hv, hx1, hx2, w1, b1, w2, b2, w3, freq, skip, kb):
    bsz, seq, c = hv.shape
    n = int(round(math.sqrt(2 * seq)))
    assert n * n == 2 * seq and bsz % 2 == 0
    hn, p = n // 2, bsz // 2
    f_data, f_filt, f_inv, gmat, hmat = _dft_tables(n)

    emb, deltas2 = _hyena_filter_tables(seq)
    w1p = jnp.pad(w1, ((0, HY_EMB_PAD - HY_EMB), (0, 0)))
    w3r = w3.reshape(w3.shape[0], HY_ORDER, 2, c)
    w3sel = jnp.stack([w3r[:, :, 0, :].reshape(-1, HY_ORDER * c), w3r[:, :, 1, :].reshape(-1, HY_ORDER * c)])
    full, asum = _filter_call(emb, w1p, b1[None], w2, b2[None], w3sel, freq[None], deltas2, seq, min(512, seq))
    c2 = HY_ORDER * c
    kf = _fb_call(_fa_call(full.reshape(2, 1, hn, n, c2), f_filt), gmat, asum)

    def view(t):
        return t.reshape(2, p, hn, n, c)

    def long_conv(u5, m5, order):
        bm = _mid_call(_fa_call(u5, f_data), gmat, hmat, kf, order, kb)
        return _fc_call(bm, f_inv, u5, m5, skip[order][None, :])

    z = long_conv(view(hv), view(hx1), 0)
    return long_conv(z, view(hx2), 1).reshape(bsz, seq, c)


def _merge_kernel(x_ref, at_ref, hy_ref, g_ref, mod_ref, wba_ref, wbh_ref, wo_ref, o_ref):
    d = x_ref.shape[2]
    g = g_ref[0].astype(F32)
    y = (jax.nn.sigmoid(g[:, :d]) * _dot(at_ref[0], wba_ref[...])
         + jax.nn.sigmoid(g[:, d:]) * _dot(hy_ref[0], wbh_ref[...]))
    o_ref[0] = x_ref[0] + mod_ref[0, 2:3, :] * _dot(y, wo_ref[...])


def _merge_call(x, attn, hy, gate, mod, wba, wbh, wo, tm):
    bsz, s, d = x.shape

    def tok(w):
        return pl.BlockSpec((1, tm, w), lambda b, i: (b, i, 0))

    return pl.pallas_call(
        _merge_kernel,
        grid=(bsz, s // tm),
        in_specs=[tok(d), tok(attn.shape[2]), tok(hy.shape[2]), tok(2 * d),
                  pl.BlockSpec((1, SUBLANES, d), lambda b, i: (b, 0, 0)),
                  _full(wba.shape), _full(wbh.shape), _full(wo.shape)],
        out_specs=tok(d),
        out_shape=jax.ShapeDtypeStruct((bsz, s, d), F32),
        compiler_params=_params("arbitrary", "arbitrary"),
        name="merge",
    )(x, attn, hy, gate, mod, wba, wbh, wo)


def _route_kernel(xm_ref, mod_ref, nf_ref, wrt_ref, rb_ref, tri_ref, lt_ref, h2_out, w_out, p_out, col_out, row_out):
    tm = xm_ref.shape[1]
    ng, gs = N_GROUPS, GROUP_SIZE

    h2 = _prenorm(xm_ref[0], mod_ref, 3, nf_ref[...])
    h2_out[0] = h2.astype(h2_out.dtype)
    scores = jax.nn.sigmoid(_dot3(wrt_ref[...], h2, NT_DIMS))
    sel = scores + rb_ref[...]
    slabs = [sel[ng * j:ng * (j + 1)] for j in range(gs)]

    top1 = jnp.full((ng, tm), -jnp.inf, F32)
    top2 = top1
    for x in slabs:
        top2 = jnp.maximum(top2, jnp.minimum(top1, x))
        top1 = jnp.maximum(top1, x)
    gscore = top1 + top2
    gid = lax.broadcasted_iota(jnp.int32, (ng, 1), 0)
    rank = jnp.zeros((ng, tm), jnp.int32)
    for g2 in range(ng):
        row = gscore[g2:g2 + 1]
        beats = (row > gscore) | ((row == gscore) & (g2 < gid))
        rank = rank + beats.astype(jnp.int32)
    gmask = rank < TOPK_GROUPS

    cand = [jnp.where(gmask, x, -jnp.inf) for x in slabs]
    eid = [gid * gs + j for j in range(gs)]
    chosen = []
    for _ in range(TOP_K):
        best = functools.reduce(jnp.maximum, cand)
        best = jnp.max(best, axis=0, keepdims=True)
        idx = functools.reduce(jnp.minimum, [jnp.where(cand[j] == best, eid[j], N_EXPERTS) for j in range(gs)])
        idx = jnp.min(idx, axis=0, keepdims=True)
        chosen.append(idx)
        cand = [jnp.where(eid[j] == idx, -jnp.inf, cand[j]) for j in range(gs)]

    mask = [functools.reduce(jnp.logical_or, [eid[j] == idx for idx in chosen]) for j in range(gs)]
    maskb = jnp.concatenate(mask, axis=0)
    maskf = jnp.where(maskb, 1.0, 0.0)
    mask16 = maskf.astype(BF16)
    wsel = jnp.where(maskb, scores, 0.0)
    w_out[...] = wsel / jnp.sum(wsel, axis=0, keepdims=True) * ROUTE_SCALE
    before = jnp.dot(mask16, tri_ref[...], preferred_element_type=F32)
    p_out[...] = jnp.where(maskb, before, -1.0).astype(p_out.dtype)

    def extents(cnt, lower_sum):
        units = jnp.floor((cnt + (RUN_ALIGN - 1)) * (1.0 / RUN_ALIGN))
        start = RUN_ALIGN * lower_sum(units.astype(BF16))
        return start, start + RUN_ALIGN * units

    cnt_c = jnp.sum(maskf, axis=1, keepdims=True)
    start_c, end_c = extents(jnp.broadcast_to(cnt_c, (N_EXPERTS, LANES)),
                             lambda u: jnp.dot(lt_ref[...], u, preferred_element_type=F32))
    lane = lax.broadcasted_iota(jnp.int32, (N_EXPERTS, LANES), 1)
    col_out[0] = jnp.where(lane == 0, cnt_c, jnp.where(lane == 1, start_c, end_c))
    cnt_r = lax.dot_general(jnp.ones((SUBLANES, tm), BF16), mask16, NT_DIMS, preferred_element_type=F32)
    start_r, end_r = extents(cnt_r, lambda u: lax.dot_general(u, lt_ref[...], NT_DIMS, preferred_element_type=F32))
    sub = lax.broadcasted_iota(jnp.int32, (SUBLANES, N_EXPERTS), 0)
    row_out[0] = jnp.where(sub == 0, start_r, end_r)


def _route_call(xm, mod, norm_ffn, wrt, rbias, lower, tm):
    bsz, s, d = xm.shape
    t = bsz * s
    nt = s // tm
    tri = (jnp.arange(tm)[:, None] < jnp.arange(tm)[None, :]).astype(BF16)
    tok = pl.BlockSpec((N_EXPERTS, tm), lambda i: (0, i))
    return pl.pallas_call(
        _route_kernel,
        grid=(t // tm,),
        in_specs=[pl.BlockSpec((1, tm, d), lambda i: (i // nt, i % nt, 0)),
                  pl.BlockSpec((1, SUBLANES, d), lambda i: (i // nt, 0, 0)),
                  _full(norm_ffn.shape), _full(wrt.shape), _full(rbias.shape), _full(tri.shape),
                  _full(lower.shape)],
        out_specs=[pl.BlockSpec((1, tm, d), lambda i: (i // nt, i % nt, 0)), tok, tok,
                   pl.BlockSpec((1, N_EXPERTS, LANES), lambda i: (i, 0, 0)),
                   pl.BlockSpec((1, SUBLANES, N_EXPERTS), lambda i: (i, 0, 0))],
        out_shape=[jax.ShapeDtypeStruct((bsz, s, d), BF16), jax.ShapeDtypeStruct((N_EXPERTS, t), F32),
                   jax.ShapeDtypeStruct((N_EXPERTS, t), BF16),
                   jax.ShapeDtypeStruct((t // tm, N_EXPERTS, LANES), F32),
                   jax.ShapeDtypeStruct((t // tm, SUBLANES, N_EXPERTS), F32)],
        compiler_params=_params("arbitrary"),
        name="route",
    )(xm, mod, norm_ffn, wrt, rbias, tri, lower)


def _pack(x):
    w = x.shape[1] // 2
    lo = lax.bitcast_convert_type(x[:, :w].astype(BF16).astype(F32), jnp.uint32)
    hi = lax.bitcast_convert_type(x[:, w:].astype(BF16).astype(F32), jnp.uint32)
    return hi | (lo >> 16)


def _unpack(u):
    lo = lax.bitcast_convert_type(u << 16, F32).astype(BF16)
    hi = lax.bitcast_convert_type(u & jnp.uint32(0xFFFF0000), F32).astype(BF16)
    return lo, hi


def _pow2_pieces(units, limit):
    bit = 1
    while bit * 2 <= limit:
        bit *= 2
    while bit:
        yield (units & bit) != 0, units & ~(2 * bit - 1), bit
        bit //= 2


def _rows_copy(vm_ref, hbm_ref, sem, vm_row, hbm_row, rows, to_hbm):
    v = vm_ref.at[pl.ds(pl.multiple_of(vm_row, RUN_ALIGN), rows), :]
    h = hbm_ref.at[pl.ds(pl.multiple_of(hbm_row, RUN_ALIGN), rows), :]
    return pltpu.make_async_copy(v, h, sem) if to_hbm else pltpu.make_async_copy(h, v, sem)


def _run_copies(vm_ref, hbm_ref, sem, n8, vm_row, hbm_row, limit, to_hbm, act):
    def emit(pieces):
        for on, off, size in pieces:
            @pl.when(on)
            def _():
                act(_rows_copy(vm_ref, hbm_ref, sem, vm_row + RUN_ALIGN * off, hbm_row + RUN_ALIGN * off,
                               RUN_ALIGN * size, to_hbm))

    pieces = list(_pow2_pieces(n8, limit))
    long_pieces = [p for p in pieces if p[2] >= LONG_RUN]
    if long_pieces:
        pl.when(n8 >= LONG_RUN)(lambda: emit(long_pieces))
    emit([p for p in pieces if p[2] < LONG_RUN])


def _wait_rows(vm_ref, hbm_ref, sem, units, limit, to_hbm):
    for on, _, size in _pow2_pieces(units, limit):
        @pl.when(on)
        def _():
            _rows_copy(vm_ref, hbm_ref, sem, 0, 0, RUN_ALIGN * size, to_hbm).wait()


def _dispatch_kernel(n8_ref, ls_ref, gs_ref, ts_ref, t8_ref, nu_ref, pos_ref, ext_ref, h_ref, xs_out, srt, zbuf,
                     sem):
    step = pl.program_id(0)
    tm = h_ref.shape[0]
    rows = srt.shape[0]
    rid = lax.broadcasted_iota(jnp.int32, (rows, 1), 0).astype(F32)
    start = ext_ref[0, 0:1, :]
    member = jnp.where((rid >= start) & (rid < ext_ref[0, 1:2, :]), 1.0, 0.0)
    offset = rid - jnp.sum(member * start, axis=1, keepdims=True)
    pos = jnp.dot(member.astype(BF16), pos_ref[...], preferred_element_type=F32)
    sel = jnp.where(pos == offset, 1.0, 0.0).astype(BF16)
    srt[...] = _pack(jnp.dot(sel, h_ref[...], preferred_element_type=F32))

    def runs(act):
        def body(e, c):
            i = step * N_EXPERTS + e
            _run_copies(srt, xs_out, sem, n8_ref[i], ls_ref[i], gs_ref[i], tm // RUN_ALIGN, True, act)
            return c
        lax.fori_loop(0, N_EXPERTS, body, 0)

    runs(lambda cp: cp.start())
    last = step * N_EXPERTS + N_EXPERTS - 1
    _wait_rows(srt, xs_out, sem, ls_ref[last] // RUN_ALIGN + n8_ref[last], rows // RUN_ALIGN, True)

    @pl.when(step == pl.num_programs(0) - 1)
    def _():
        zbuf[...] = jnp.zeros(zbuf.shape, zbuf.dtype)
        nblk = xs_out.shape[0] // EXPERT_BLOCK

        def fill(act):
            def tails(e, c):
                _run_copies(zbuf, xs_out, sem, t8_ref[e], 0, ts_ref[e], EXPERT_BLOCK // RUN_ALIGN - 1, True, act)
                return c

            def blocks(b, c):
                act(pltpu.make_async_copy(
                    zbuf, xs_out.at[pl.ds(pl.multiple_of(b * EXPERT_BLOCK, EXPERT_BLOCK), EXPERT_BLOCK), :], sem))
                return c

            lax.fori_loop(0, N_EXPERTS, tails, 0)
            lax.fori_loop(nu_ref[0], nblk, blocks, 0)

        fill(lambda cp: cp.start())
        fill(lambda cp: cp.wait())


def _dispatch_call(tables, pos_et, ext_rows, h2, nblk, tm):
    t, d = h2.shape
    lrows = TOP_K * tm + N_EXPERTS * RUN_ALIGN
    return pl.pallas_call(
        _dispatch_kernel,
        grid_spec=pltpu.PrefetchScalarGridSpec(
            num_scalar_prefetch=len(tables), grid=(t // tm,),
            in_specs=[pl.BlockSpec((N_EXPERTS, tm), lambda i, *_: (0, i)),
                      pl.BlockSpec((1,) + ext_rows.shape[1:], lambda i, *_: (i, 0, 0)),
                      pl.BlockSpec((tm, d), lambda i, *_: (i, 0))],
            out_specs=pl.BlockSpec(memory_space=pl.ANY),
            scratch_shapes=[pltpu.VMEM((lrows, d // 2), jnp.uint32), pltpu.VMEM((EXPERT_BLOCK, d // 2), jnp.uint32),
                            pltpu.SemaphoreType.DMA(())]),
        out_shape=jax.ShapeDtypeStruct((nblk * EXPERT_BLOCK, d // 2), jnp.uint32),
        compiler_params=_params("arbitrary"),
        name="dispatch",
    )(*tables, pos_et, ext_rows, h2)


def _expert_kernel(blk_ref, nused_ref, x_ref, wgu_ref, wd_ref, y_ref):
    used = pl.program_id(0) < nused_ref[0]

    @pl.when(used)
    def _():
        lo, hi = _unpack(x_ref[...])
        half = lo.shape[1]
        gu = (jnp.dot(lo, wgu_ref[0, :half, :], preferred_element_type=F32)
              + jnp.dot(hi, wgu_ref[0, half:, :], preferred_element_type=F32))
        a = _silu(gu[:, :EXPERT_FF]) * gu[:, EXPERT_FF:]
        y_ref[...] = _pack(_dot(a, wd_ref[0]))

    @pl.when(jnp.logical_not(used))
    def _():
        y_ref[...] = jnp.zeros(y_ref.shape, y_ref.dtype)


def _expert_call(blk_e, nused, xs, wgu, wd):
    rows, d = xs.shape
    nblk = rows // EXPERT_BLOCK

    def row_map(i, blk, nu):
        return (jnp.minimum(i, nu[0] - 1), 0)

    return pl.pallas_call(
        _expert_kernel,
        grid_spec=pltpu.PrefetchScalarGridSpec(
            num_scalar_prefetch=2, grid=(nblk,),
            in_specs=[pl.BlockSpec((EXPERT_BLOCK, d), row_map),
                      pl.BlockSpec((1,) + wgu.shape[1:], lambda i, blk, nu: (blk[i], 0, 0)),
                      pl.BlockSpec((1,) + wd.shape[1:], lambda i, blk, nu: (blk[i], 0, 0))],
            out_specs=pl.BlockSpec((EXPERT_BLOCK, d), lambda i, blk, nu: (i, 0))),
        out_shape=jax.ShapeDtypeStruct((rows, d), jnp.uint32),
        compiler_params=_params("arbitrary"),
        name="expert",
    )(blk_e, nused, xs, wgu, wd)


def _combine_kernel(n8_ref, ls_ref, gs_ref, ys_hbm, pos_ref, w_ref, ext_ref, xm_ref, h_ref, mod_ref, wsgu_ref,
                    wsd_ref, fn_ref, o_ref, ybuf, sem):
    step = pl.program_id(0)
    tm = xm_ref.shape[0]
    rows = ybuf.shape[0]

    def runs(act):
        def body(e, c):
            i = step * N_EXPERTS + e
            _run_copies(ybuf, ys_hbm, sem, n8_ref[i], ls_ref[i], gs_ref[i], tm // RUN_ALIGN, False, act)
            return c
        lax.fori_loop(0, N_EXPERTS, body, 0)

    runs(lambda cp: cp.start())
    gu = _dot(h_ref[...], wsgu_ref[...])
    ff = gu.shape[1] // 2
    shared = _dot(_silu(gu[:, :ff]) * gu[:, ff:], wsd_ref[...])
    cid = lax.broadcasted_iota(jnp.int32, (1, rows), 1).astype(F32)
    start = ext_ref[0, :, 1:2]
    member = jnp.where((cid >= start) & (cid < ext_ref[0, :, 2:3]), 1.0, 0.0)
    offset = cid - jnp.sum(member * start, axis=0, keepdims=True)
    member = member.astype(BF16)
    pos = jnp.dot(pos_ref[...], member, preferred_element_type=F32)
    mix = jnp.where(pos == offset, jnp.dot(w_ref[...].astype(BF16), member, preferred_element_type=F32), 0.0)
    mix = mix.astype(BF16)
    last = step * N_EXPERTS + N_EXPERTS - 1
    filled = ls_ref[last] + RUN_ALIGN * n8_ref[last]
    _wait_rows(ybuf, ys_hbm, sem, filled // RUN_ALIGN, rows // RUN_ALIGN, False)
    rid = lax.broadcasted_iota(jnp.int32, (rows, 1), 0)
    lo, hi = _unpack(jnp.where(rid < filled, ybuf[...], jnp.uint32(0)))
    routed = jnp.concatenate([jnp.dot(mix, lo, preferred_element_type=F32),
                              jnp.dot(mix, hi, preferred_element_type=F32)], axis=1)
    x = xm_ref[...] + mod_ref[0, 5:6, :] * (routed + shared)
    o_ref[...] = _rms(x, fn_ref[...])


def _combine_call(tables, ys, pos_te, w_te, ext_cols, xm, h2, mod, wsgu, wsd, final_norm, tm, tiles_per_batch):
    t, d = xm.shape
    lrows = TOP_K * tm + N_EXPERTS * RUN_ALIGN
    tok = pl.BlockSpec((tm, d), lambda i, *_: (i, 0))
    per_e = pl.BlockSpec((tm, N_EXPERTS), lambda i, *_: (i, 0))
    return pl.pallas_call(
        _combine_kernel,
        grid_spec=pltpu.PrefetchScalarGridSpec(
            num_scalar_prefetch=len(tables), grid=(t // tm,),
            in_specs=[pl.BlockSpec(memory_space=pl.ANY), per_e, per_e,
                      pl.BlockSpec((1,) + ext_cols.shape[1:], lambda i, *_: (i, 0, 0)), tok, tok,
                      pl.BlockSpec((1, SUBLANES, d), lambda i, *_: (i // tiles_per_batch, 0, 0)),
                      _full(wsgu.shape), _full(wsd.shape), _full(final_norm.shape)],
            out_specs=tok,
            scratch_shapes=[pltpu.VMEM((lrows, d // 2), jnp.uint32), pltpu.SemaphoreType.DMA(())]),
        out_shape=jax.ShapeDtypeStruct((t, d), F32),
        compiler_params=_params("arbitrary"),
        name="combine",
    )(*tables, ys, pos_te, w_te, ext_cols, xm, h2, mod, wsgu, wsd, final_norm)


def _moe(xm, mod, norm_ffn, w_router, router_bias, wg, wu, wd, wsg, wsu, wsd, final_norm, tm):
    bsz, s, d = xm.shape
    t = bsz * s
    nt = t // tm
    perm = (np.arange(N_EXPERTS) % N_GROUPS) * GROUP_SIZE + np.arange(N_EXPERTS) // N_GROUPS
    wrt = w_router.T[perm]
    rbias = router_bias[perm][:, None]
    lower = jnp.asarray(perm[None, :] < perm[:, None], BF16)
    h2, w_et, pos_et, ext_cols, ext_rows = _route_call(xm, mod, norm_ffn, wrt, rbias, lower, tm)

    inv = np.argsort(perm)
    n8 = (ext_cols[:, :, 0].astype(jnp.int32)[:, inv] + (RUN_ALIGN - 1)) // RUN_ALIGN
    run = RUN_ALIGN * n8
    ls = jnp.cumsum(run, axis=1) - run
    tot = jnp.sum(run, axis=0)
    padded = (tot + EXPERT_BLOCK - 1) // EXPERT_BLOCK * EXPERT_BLOCK
    pad_end = jnp.cumsum(padded)
    gs = (pad_end - padded)[None, :] + jnp.cumsum(run, axis=0) - run
    nblk = -(-(t * TOP_K + nt * N_EXPERTS * (RUN_ALIGN - 1)) // EXPERT_BLOCK) + N_EXPERTS
    blk_first = jnp.arange(nblk, dtype=jnp.int32)[:, None] * EXPERT_BLOCK
    blk_e = jnp.minimum(jnp.sum((pad_end[None, :] <= blk_first).astype(jnp.int32), axis=1), N_EXPERTS - 1)
    nused = (pad_end[-1:] // EXPERT_BLOCK).astype(jnp.int32)
    tables = [a.reshape(-1).astype(jnp.int32) for a in (n8, ls, gs)]
    tails = [(pad_end - padded + tot).astype(jnp.int32), ((padded - tot) // RUN_ALIGN).astype(jnp.int32), nused]

    h2f = h2.reshape(t, d)
    xs = _dispatch_call(tables + tails, pos_et, ext_rows, h2f, nblk, tm)
    wgu = jnp.concatenate([wg, wu], axis=2).astype(BF16)
    ys = _expert_call(blk_e, nused, xs, wgu, wd.astype(BF16))
    wsgu = jnp.concatenate([wsg, wsu], axis=1).astype(BF16)
    out = _combine_call(tables, ys, pos_et.T, w_et.T, ext_cols, xm.reshape(t, d), h2f, mod, wsgu, wsd.astype(BF16),
                        final_norm, tm, s // tm)
    return out.reshape(bsz, s, d)


def _rope_tables(s):
    rows = s // GRID_W
    row = jnp.broadcast_to(jnp.arange(rows, dtype=F32)[:, None], (rows, GRID_W)).reshape(-1)
    col = jnp.broadcast_to(jnp.arange(GRID_W, dtype=F32)[None, :], (rows, GRID_W)).reshape(-1)
    half = QK_ROPE // 2
    inv_freq = ROPE_THETA ** (-jnp.arange(0, half, 2, dtype=F32) / half)
    ar, ac = row[:, None] * inv_freq, col[:, None] * inv_freq
    ones = jnp.ones((s, QK_NOPE), F32)
    tail = HEAD_PAD - QK_NOPE - QK_ROPE
    cos_t = jnp.concatenate([ones, jnp.cos(ar), jnp.cos(ar), jnp.cos(ac), jnp.cos(ac), jnp.ones((s, tail), F32)], 1)
    sin_t = jnp.concatenate([0 * ones, -jnp.sin(ar), jnp.sin(ar), -jnp.sin(ac), jnp.sin(ac),
                             jnp.zeros((s, tail), F32)], 1)
    return cos_t, sin_t


_Q4 = QK_ROPE // 4
ROPE_SWAP = np.concatenate([np.arange(_Q4, 2 * _Q4), np.arange(0, _Q4), np.arange(3 * _Q4, 4 * _Q4),
                            np.arange(2 * _Q4, 3 * _Q4)])


def _rope_slot(w, swap):
    if swap:
        w = w[..., ROPE_SWAP]
    pad = [(0, 0)] * (w.ndim - 1) + [(QK_NOPE, HEAD_PAD - QK_NOPE - QK_ROPE)]
    return jnp.pad(w, pad)


def kernel(x, c, ctx, c_ctx, w_mod, b_mod, norm_mix, norm_ffn, w_in, b_in, q_norm, w_uq, kv_norm, w_ukv, w_branch_attn, hy_conv_w, hy_conv_b, hy_filt_w1, hy_filt_b1, hy_filt_w2, hy_filt_b2, hy_filt_w3, hy_filt_freq, hy_skip, w_branch_hyena, w_out, w_router, router_bias, w_exp_gate, w_exp_up, w_exp_down, w_sh_gate, w_sh_up, w_sh_down, final_norm,
           tiles=None):
    bsz, s, d = x.shape
    tl = dict(inproj=256, tq=1024, tk=1408, fft_kb=4, merge=512, moe=256)
    tl.update(tiles or {})
    assert w_mod.shape[0] == 1, "single-layer trunk"
    i = 0

    rows = -(-(bsz + 1) // SUBLANES) * SUBLANES
    c_rows = jnp.pad(jnp.concatenate([c, c_ctx[None]], axis=0), ((0, rows - bsz - 1), (0, 0)))
    mod_all = _mod_call(c_rows, w_mod[i], b_mod[i])
    mod_all = jnp.pad(mod_all.reshape(rows, 6, d), ((0, 0), (0, SUBLANES - 6), (0, 0)))
    mod, modc = mod_all[:bsz], mod_all[bsz:bsz + 1]

    cuts = np.cumsum([Q_LORA, KV_LORA, QK_ROPE, 3 * HY_WIDTH])
    wi, bi = w_in[i], b_in[i][None]
    w_q, w_kv, w_pe, w_hy, w_g = jnp.split(wi, cuts, axis=1)
    b_q, b_kv, b_pe, b_hy, b_g = jnp.split(bi, cuts, axis=1)
    wa = jnp.concatenate([w_q, w_kv, _rope_slot(w_pe, False), _rope_slot(w_pe, True)], axis=1).astype(BF16)
    ba = jnp.concatenate([b_q, b_kv, _rope_slot(b_pe, False), _rope_slot(b_pe, True)], axis=1)
    wq3 = w_uq[i].reshape(Q_LORA, N_HEADS, QK_NOPE + QK_ROPE) * (ATTN_SCALE * math.log2(math.e))
    tail = ((0, 0), (0, 0), (0, HEAD_PAD - QK_NOPE))
    wuq = (jnp.pad(wq3[..., :QK_NOPE], tail) + _rope_slot(wq3[..., QK_NOPE:], False)).reshape(Q_LORA, -1).astype(BF16)
    wuqs = _rope_slot(wq3[..., QK_NOPE:], True).reshape(Q_LORA, -1).astype(BF16)
    wkv3 = w_ukv[i].reshape(KV_LORA, N_HEADS, QK_NOPE + V_HEAD)
    wuk = jnp.pad(wkv3[..., :QK_NOPE], tail).reshape(KV_LORA, -1).astype(BF16)
    wuvt = wkv3[..., QK_NOPE:].reshape(KV_LORA, -1).T.astype(BF16)
    nm, qn, kvn = norm_mix[i][None], q_norm[i][None], kv_norm[i][None]

    w_c = jnp.concatenate([w_kv, _rope_slot(w_pe, False)], axis=1).astype(BF16)
    b_c = jnp.concatenate([b_kv, _rope_slot(b_pe, False)], axis=1)
    ck, cvt = _ctx_call(ctx, modc, nm, w_c, b_c, kvn, wuk, wuvt)

    cos_t, sin_t = _rope_tables(s)
    q, k, vt, hv, hx1, hx2, gate = _inproj_call(
        x, mod, nm, wa, ba, w_hy.astype(BF16), b_hy, w_g.astype(BF16), b_g, qn, wuq, wuqs, kvn, wuk, wuvt,
        cos_t, sin_t, hy_conv_w[i], hy_conv_b[i][None], tl["inproj"])

    attn = _attn_call(q, jnp.concatenate([ck, k], axis=2), jnp.concatenate([cvt, vt], axis=3), tl["tq"], tl["tk"])
    hy = _hyena(hv, hx1, hx2, hy_filt_w1[i], hy_filt_b1[i], hy_filt_w2[i], hy_filt_b2[i], hy_filt_w3[i],
                hy_filt_freq[i], hy_skip[i], tl["fft_kb"])
    xm = _merge_call(x, attn, hy, gate, mod, w_branch_attn[i].astype(BF16), w_branch_hyena[i].astype(BF16),
                     w_out[i].astype(BF16), tl["merge"])
    return _moe(xm, mod, norm_ffn[i][None], w_router[i], router_bias[i], w_exp_gate[i], w_exp_up[i], w_exp_down[i],
                w_sh_gate[i], w_sh_up[i], w_sh_down[i], final_norm[None], tl["moe"])
```

```python
import functools
import math

import numpy as np
import jax
import jax.numpy as jnp
from jax import lax
from jax.experimental import pallas as pl
from jax.experimental.pallas import tpu as pltpu

GRID_W = 64
N_HEADS = 8
QK_NOPE = 64
QK_ROPE = 32
V_HEAD = 64
Q_LORA = 256
KV_LORA = 128
ROPE_THETA = 10000.0
ATTN_SCALE = 1.0 / math.sqrt(QK_NOPE + QK_ROPE)
HY_WIDTH = 512
HY_ORDER = 2
HY_SHORT = 3
HY_BANDS = 8
HY_EMB = 1 + 2 * HY_BANDS
HY_EMB_PAD = 32
HY_FAST_DECAY = 0.3
HY_SLOW_DECAY = 1.5
HY_DECAY_TARGET = 1e-2
N_EXPERTS = 64
N_GROUPS = 8
GROUP_SIZE = N_EXPERTS // N_GROUPS
TOPK_GROUPS = 4
TOP_K = 8
EXPERT_FF = 256
ROUTE_SCALE = 2.5
EXPERT_BLOCK = 512
RUN_ALIGN = 8
LONG_RUN = 8
NORM_EPS = 1e-6

HEAD_PAD = 128
Q_CHUNK = 512
AHEAD = 2
LANES = 128
SUBLANES = 8
VMEM_LIMIT = 48 * 1024 * 1024

F32 = jnp.float32
BF16 = jnp.bfloat16
NT_DIMS = (((1,), (1,)), ((), ()))
NN_DIMS = (((1,), (0,)), ((), ()))


def _params(*sem):
    return pltpu.CompilerParams(dimension_semantics=sem, vmem_limit_bytes=VMEM_LIMIT)


def _dot(a, b):
    return jnp.dot(a.astype(BF16), b.astype(BF16), preferred_element_type=F32)


def _split(a):
    hi = a.astype(BF16)
    lo = (a - hi.astype(F32)).astype(BF16)
    return hi, lo


def _dot3(a, b, dims=NN_DIMS):
    ah, al = _split(a)
    bh, bl = _split(b)
    d = functools.partial(lax.dot_general, dimension_numbers=dims, preferred_element_type=F32)
    return d(ah, bh) + (d(ah, bl) + d(al, bh))


def _rms(x, g):
    return x * lax.rsqrt(jnp.mean(x * x, axis=-1, keepdims=True) + NORM_EPS) * g


def _silu(x):
    return x * jax.nn.sigmoid(x)


def _full(shape):
    nd = len(shape)
    return pl.BlockSpec(shape, lambda *_: (0,) * nd)


def _mod_kernel(c_ref, w_ref, b_ref, o_ref):
    o_ref[...] = _dot3(_silu(c_ref[...]), w_ref[...]) + b_ref[...]


def _mod_call(c_rows, w_mod, b_mod):
    r, d = c_rows.shape
    n = w_mod.shape[1]
    bn = 1024
    return pl.pallas_call(
        _mod_kernel,
        grid=(n // bn,),
        in_specs=[_full((r, d)), pl.BlockSpec((d, bn), lambda j: (0, j)), pl.BlockSpec((1, bn), lambda j: (0, j))],
        out_specs=pl.BlockSpec((r, bn), lambda j: (0, j)),
        out_shape=jax.ShapeDtypeStruct((r, n), F32),
        compiler_params=_params("arbitrary"),
        name="mod",
    )(c_rows, w_mod, b_mod.reshape(1, n))


def _prenorm(x, mod_ref, row, g):
    shift = mod_ref[0, row:row + 1, :]
    scale = mod_ref[0, row + 1:row + 2, :]
    return _rms(x, g) * (1.0 + scale) + shift


def _kv_heads(kv_lat, kpe, kvn_ref, wuk_ref, wuvt_ref, k_out, vt_out):
    kvn = _rms(kv_lat, kvn_ref[...]).astype(BF16)
    kk = _dot(kvn, wuk_ref[...])
    vt = lax.dot_general(wuvt_ref[...], kvn, NT_DIMS, preferred_element_type=F32)
    ones = jnp.ones((HEAD_PAD - V_HEAD, vt.shape[1]), F32)
    for h in range(N_HEADS):
        k_out[0, h] = (kk[:, HEAD_PAD * h:HEAD_PAD * (h + 1)] + kpe).astype(BF16)
        vt_out[0, h] = jnp.concatenate([vt[V_HEAD * h:V_HEAD * (h + 1)], ones], axis=0).astype(BF16)


def _ctx_kernel(c_ref, mod_ref, nm_ref, w_ref, b_ref, kvn_ref, wuk_ref, wuv_ref, k_out, v_out):
    h = _prenorm(c_ref[0], mod_ref, 0, nm_ref[...]).astype(BF16)
    a = _dot(h, w_ref[...]) + b_ref[...]
    _kv_heads(a[:, :KV_LORA], a[:, KV_LORA:], kvn_ref, wuk_ref, wuv_ref, k_out, v_out)


def _ctx_call(ctx, modc, norm_mix, w_c, b_c, kv_norm, w_uk, w_uv):
    bsz, n, d = ctx.shape
    return pl.pallas_call(
        _ctx_kernel,
        grid=(bsz,),
        in_specs=[pl.BlockSpec((1, n, d), lambda b: (b, 0, 0)), _full(modc.shape), _full(norm_mix.shape),
                  _full(w_c.shape), _full(b_c.shape), _full(kv_norm.shape), _full(w_uk.shape), _full(w_uv.shape)],
        out_specs=[pl.BlockSpec((1, N_HEADS, n, HEAD_PAD), lambda b: (b, 0, 0, 0)),
                   pl.BlockSpec((1, N_HEADS, HEAD_PAD, n), lambda b: (b, 0, 0, 0))],
        out_shape=[jax.ShapeDtypeStruct((bsz, N_HEADS, n, HEAD_PAD), BF16),
                   jax.ShapeDtypeStruct((bsz, N_HEADS, HEAD_PAD, n), BF16)],
        compiler_params=_params("arbitrary"),
        name="ctx",
    )(ctx, modc, norm_mix, w_c, b_c, kv_norm, w_uk, w_uv)


def _inproj_kernel(x_ref, xp_ref, xn_ref, mod_ref, nm_ref, wa_ref, ba_ref, why_ref, bhy_ref, wg_ref, bg_ref,
                   qn_ref, wuq_ref, wuqs_ref, kvn_ref, wuk_ref, wuv_ref, cos_ref, sin_ref, cw_ref, cb_ref,
                   q_out, k_out, v_out, hv_out, hx1_out, hx2_out, g_out):
    i = pl.program_id(0)
    tm = x_ref.shape[1]
    nm = nm_ref[...]
    h = _prenorm(x_ref[0], mod_ref, 0, nm).astype(BF16)
    a = _dot(h, wa_ref[...]) + ba_ref[...]
    q_lat = a[:, :Q_LORA]
    kv_lat = a[:, Q_LORA:Q_LORA + KV_LORA]
    kpe_m = a[:, Q_LORA + KV_LORA:Q_LORA + KV_LORA + HEAD_PAD]
    kpe_s = a[:, Q_LORA + KV_LORA + HEAD_PAD:]
    cos = cos_ref[...]
    sin = sin_ref[...]
    qn = _rms(q_lat, qn_ref[...]).astype(BF16)
    qa = _dot(qn, wuq_ref[...])
    qs = _dot(qn, wuqs_ref[...])
    for hh in range(N_HEADS):
        sl = slice(HEAD_PAD * hh, HEAD_PAD * (hh + 1))
        q_out[0, hh] = (qa[:, sl] * cos + qs[:, sl] * sin).astype(BF16)
    _kv_heads(kv_lat, kpe_m * cos + kpe_s * sin, kvn_ref, wuk_ref, wuv_ref, k_out, v_out)
    g_out[0] = (_dot(h, wg_ref[...]) + bg_ref[...]).astype(BF16)

    why = why_ref[...]
    bhy = bhy_ref[...]
    hy = _dot(h, why) + bhy
    hp = _dot(_prenorm(xp_ref[0], mod_ref, 0, nm).astype(BF16), why) + bhy
    hn = _dot(_prenorm(xn_ref[0], mod_ref, 0, nm).astype(BF16), why) + bhy
    prev = jnp.where(i == 0, 0.0, hp[SUBLANES - 1:SUBLANES])
    nxt = jnp.where(i == pl.num_programs(0) - 1, 0.0, hn[0:1])
    rid = lax.broadcasted_iota(jnp.int32, (tm, 1), 0)
    up = jnp.where(rid == 0, prev, pltpu.roll(hy, 1, 0))
    dn = jnp.where(rid == tm - 1, nxt, pltpu.roll(hy, tm - 1, 0))
    u = up * cw_ref[0:1, :] + hy * cw_ref[1:2, :] + dn * cw_ref[2:3, :] + cb_ref[...]
    hv_out[0] = u[:, :HY_WIDTH]
    hx1_out[0] = u[:, HY_WIDTH:2 * HY_WIDTH]
    hx2_out[0] = u[:, 2 * HY_WIDTH:]


def _inproj_call(x, mod, norm_mix, wa, ba, why, bhy, wg, bg, q_norm, wuq, wuqs, kv_norm, wuk, wuvt, cos_t, sin_t, cw,
                 cb, tm):
    bsz, s, d = x.shape
    nt = s // tm
    rb = tm // SUBLANES
    last_rb = s // SUBLANES - 1
    consts = [norm_mix, wa, ba, why, bhy, wg, bg, q_norm, wuq, wuqs, kv_norm, wuk, wuvt]
    in_specs = [
        pl.BlockSpec((1, tm, d), lambda i, b: (b, i, 0)),
        pl.BlockSpec((1, SUBLANES, d), lambda i, b: (b, jnp.maximum(i * rb - 1, 0), 0)),
        pl.BlockSpec((1, SUBLANES, d), lambda i, b: (b, jnp.minimum((i + 1) * rb, last_rb), 0)),
        pl.BlockSpec((1, SUBLANES, d), lambda i, b: (b, 0, 0)),
    ] + [_full(c.shape) for c in consts] + [
        pl.BlockSpec((tm, HEAD_PAD), lambda i, b: (i, 0)),
        pl.BlockSpec((tm, HEAD_PAD), lambda i, b: (i, 0)),
        _full(cw.shape), _full(cb.shape),
    ]
    hw = HY_WIDTH
    out_specs = [
        pl.BlockSpec((1, N_HEADS, tm, HEAD_PAD), lambda i, b: (b, 0, i, 0)),
        pl.BlockSpec((1, N_HEADS, tm, HEAD_PAD), lambda i, b: (b, 0, i, 0)),
        pl.BlockSpec((1, N_HEADS, HEAD_PAD, tm), lambda i, b: (b, 0, 0, i)),
        pl.BlockSpec((1, tm, hw), lambda i, b: (b, i, 0)),
        pl.BlockSpec((1, tm, hw), lambda i, b: (b, i, 0)),
        pl.BlockSpec((1, tm, hw), lambda i, b: (b, i, 0)),
        pl.BlockSpec((1, tm, 2 * d), lambda i, b: (b, i, 0)),
    ]
    out_shape = [
        jax.ShapeDtypeStruct((bsz, N_HEADS, s, HEAD_PAD), BF16),
        jax.ShapeDtypeStruct((bsz, N_HEADS, s, HEAD_PAD), BF16),
        jax.ShapeDtypeStruct((bsz, N_HEADS, HEAD_PAD, s), BF16),
        jax.ShapeDtypeStruct((bsz, s, hw), F32),
        jax.ShapeDtypeStruct((bsz, s, hw), F32),
        jax.ShapeDtypeStruct((bsz, s, hw), F32),
        jax.ShapeDtypeStruct((bsz, s, 2 * d), BF16),
    ]
    return pl.pallas_call(
        _inproj_kernel,
        grid=(nt, bsz),
        in_specs=in_specs,
        out_specs=out_specs,
        out_shape=out_shape,
        compiler_params=_params("arbitrary", "arbitrary"),
        name="inproj",
    )(x, x, x, mod, *consts, cos_t, sin_t, cw, cb)


def _attn_kernel(q_ref, k_ref, vt_ref, o_ref, m_sc, acc_sc):
    j = pl.program_id(2)

    @pl.when(j == 0)
    def _():
        m_sc[...] = jnp.full(m_sc.shape, -jnp.inf, F32)
        acc_sc[...] = jnp.zeros(acc_sc.shape, F32)

    tq = q_ref.shape[2]
    qw = min(tq, Q_CHUNK)
    units = [(h, c) for h in range(N_HEADS) for c in range(0, tq, qw)]

    def scores(u):
        h, c = units[u]
        return lax.dot_general(k_ref[0, h], q_ref[0, h, c:c + qw, :], NT_DIMS,
                               preferred_element_type=F32)

    pending = [scores(u) for u in range(AHEAD)]
    for u, (h, c) in enumerate(units):
        if u + AHEAD < len(units):
            pending.append(scores(u + AHEAD))
        st = pending.pop(0)
        m_prev = m_sc[h, :, c:c + qw]
        m_new = jnp.maximum(m_prev, jnp.max(st, axis=0, keepdims=True))
        pt = jnp.exp2(st - m_new).astype(BF16)
        acc_sc[h, :, c:c + qw] = (jnp.exp2(m_prev - m_new) * acc_sc[h, :, c:c + qw]
                                  + jnp.dot(vt_ref[0, h], pt, preferred_element_type=F32))
        m_sc[h, :, c:c + qw] = m_new

    @pl.when(j == pl.num_programs(2) - 1)
    def _():
        ot = jnp.concatenate([acc_sc[h, :V_HEAD] / acc_sc[h, V_HEAD:V_HEAD + 1] for h in range(N_HEADS)], axis=0)
        o_ref[0] = ot.T.astype(o_ref.dtype)


def _attn_call(q, k, vt, tq, tk):
    bsz, nh, s, dh = q.shape
    nk = k.shape[2]
    dv = nh * V_HEAD
    return pl.pallas_call(
        _attn_kernel,
        grid=(bsz, s // tq, nk // tk),
        in_specs=[
            pl.BlockSpec((1, nh, tq, dh), lambda b, i, j: (b, 0, i, 0)),
            pl.BlockSpec((1, nh, tk, dh), lambda b, i, j: (b, 0, j, 0)),
            pl.BlockSpec((1, nh, dh, tk), lambda b, i, j: (b, 0, 0, j)),
        ],
        out_specs=pl.BlockSpec((1, tq, dv), lambda b, i, j: (b, i, 0)),
        out_shape=jax.ShapeDtypeStruct((bsz, s, dv), BF16),
        scratch_shapes=[pltpu.VMEM((nh, 1, tq), F32), pltpu.VMEM((nh, dh, tq), F32)],
        compiler_params=_params("arbitrary", "arbitrary", "arbitrary"),
        name="attn",
    )(q, k, vt)


def _filter_kernel(emb_ref, w1_ref, b1_ref, w2_ref, b2_ref, w3_ref, fr_ref, dl_ref, full_out, asum_out, *, seq):
    r = pl.program_id(0)
    rb = emb_ref.shape[0]
    emb = emb_ref[...]
    fr = fr_ref[...]
    h = jnp.sin(fr * (_dot3(emb, w1_ref[...]) + b1_ref[...]))
    h = jnp.sin(fr * (_dot3(h, w2_ref[...]) + b2_ref[...]))
    k = _dot3(h, w3_ref[0]) * jnp.exp(-emb[:, 0:1] * dl_ref[...])
    row = r * rb + lax.broadcasted_iota(jnp.int32, (rb, 1), 0)
    k = jnp.where(row == seq, 0.0, k)
    full_out[...] = k

    @pl.when(r == 0)
    def _():
        asum_out[...] = jnp.zeros(asum_out.shape, F32)

    asum_out[...] += jnp.sum(jnp.abs(k), axis=0, keepdims=True)


def _filter_call(emb, w1, b1, w2, b2, w3sel, freq, deltas2, seq, rb):
    n2 = emb.shape[0]
    half_blocks = seq // rb
    width = w3sel.shape[2]
    return pl.pallas_call(
        functools.partial(_filter_kernel, seq=seq),
        grid=(n2 // rb,),
        in_specs=[pl.BlockSpec((rb, HY_EMB_PAD), lambda r: (r, 0)), _full(w1.shape), _full(b1.shape),
                  _full(w2.shape), _full(b2.shape),
                  pl.BlockSpec((1,) + w3sel.shape[1:], lambda r: (r // half_blocks, 0, 0)),
                  _full(freq.shape), _full(deltas2.shape)],
        out_specs=[pl.BlockSpec((rb, width), lambda r: (r, 0)), pl.BlockSpec((1, width), lambda r: (0, 0))],
        out_shape=[jax.ShapeDtypeStruct((n2, width), F32), jax.ShapeDtypeStruct((1, width), F32)],
        compiler_params=_params("arbitrary"),
        name="filt",
    )(emb, w1, b1, w2, b2, w3sel, freq, deltas2)


def _fa_kernel(u_ref, f_ref, a_out):
    two, _, hn, g, c = u_ref.shape
    a = _dot(f_ref[...], u_ref[...].reshape(two * hn * g, c))
    a_out[...] = a.reshape(a_out.shape)


def _fa_call(u5, fmat):
    _, p, hn, n, c = u5.shape
    g = SUBLANES
    return pl.pallas_call(
        _fa_kernel,
        grid=(p, n // g),
        in_specs=[pl.BlockSpec((2, 1, hn, g, c), lambda q, j: (0, q, 0, j, 0)), _full(fmat.shape)],
        out_specs=pl.BlockSpec((1, 2, n, g, c), lambda q, j: (q, 0, 0, j, 0)),
        out_shape=jax.ShapeDtypeStruct((p, 2, n, n, c), F32),
        compiler_params=_params("arbitrary", "arbitrary"),
        name="fa",
    )(u5, fmat)


def _fb_kernel(a_ref, g_ref, asum_ref, kf_out):
    two, _, n, c = a_ref.shape[1:]
    a = a_ref[0].reshape(two * n, c)
    x = _dot(g_ref[0], a) / (asum_ref[...] + 1e-6)
    kf_out[0] = x.reshape(two, n, c)


def _fb_call(a5, gmat, asum):
    _, _, n, _, c = a5.shape
    return pl.pallas_call(
        _fb_kernel,
        grid=(n,),
        in_specs=[pl.BlockSpec((1, 2, 1, n, c), lambda k: (0, 0, k, 0, 0)),
                  pl.BlockSpec((1, 2 * n, 2 * n), lambda k: (k, 0, 0)), _full(asum.shape)],
        out_specs=pl.BlockSpec((1, 2, n, c), lambda k: (k, 0, 0, 0)),
        out_shape=jax.ShapeDtypeStruct((n, 2, n, c), F32),
        compiler_params=_params("arbitrary"),
        name="fb",
    )(a5, gmat, asum)


def _mid_kernel(a_ref, g_ref, h_ref, kf_ref, b_out):
    _, two, kb, n, c = a_ref.shape
    for kk in range(kb):
        x = _dot(g_ref[kk], a_ref[0, :, kk].reshape(two * n, c))
        xr, xi = x[:n], x[n:]
        kr, ki = kf_ref[kk, 0], kf_ref[kk, 1]
        y = jnp.concatenate([xr * kr - xi * ki, xr * ki + xi * kr], axis=0)
        b_out[0, :, kk] = _dot(h_ref[kk], y).reshape(two, n, c)


def _mid_call(a5, gmat, hmat, kf, order, kb):
    p, _, n, _, c = a5.shape
    return pl.pallas_call(
        _mid_kernel,
        grid=(n // kb, p),
        in_specs=[pl.BlockSpec((1, 2, kb, n, c), lambda k, q: (q, 0, k, 0, 0)),
                  pl.BlockSpec((kb, 2 * n, 2 * n), lambda k, q: (k, 0, 0)),
                  pl.BlockSpec((kb, 2 * n, 2 * n), lambda k, q: (k, 0, 0)),
                  pl.BlockSpec((kb, 2, n, c), lambda k, q: (k, 0, 0, order))],
        out_specs=pl.BlockSpec((1, 2, kb, n, c), lambda k, q: (q, 0, k, 0, 0)),
        out_shape=jax.ShapeDtypeStruct(a5.shape, F32),
        compiler_params=_params("arbitrary", "arbitrary"),
        name="mid",
    )(a5, gmat, hmat, kf)


def _fc_kernel(b_ref, f_ref, u_ref, m_ref, skip_ref, o_out):
    _, two, n, g, c = b_ref.shape
    y = _dot(f_ref[...], b_ref[...].reshape(two * n * g, c)).reshape(u_ref.shape)
    o_out[...] = m_ref[...] * (y + u_ref[...] * skip_ref[...])


def _fc_call(b5, finv, u5, m5, skip_row):
    _, p, hn, n, c = u5.shape
    g = SUBLANES
    blk = pl.BlockSpec((2, 1, hn, g, c), lambda q, j: (0, q, 0, j, 0))
    return pl.pallas_call(
        _fc_kernel,
        grid=(p, n // g),
        in_specs=[pl.BlockSpec((1, 2, n, g, c), lambda q, j: (q, 0, 0, j, 0)), _full(finv.shape), blk, blk,
                  _full(skip_row.shape)],
        out_specs=blk,
        out_shape=jax.ShapeDtypeStruct(u5.shape, F32),
        compiler_params=_params("arbitrary", "arbitrary"),
        name="fc",
    )(b5, finv, u5, m5, skip_row)


def _dft_tables(n):
    hn = n // 2
    k = np.arange(n)[:, None]
    ang = -2.0 * np.pi * (k * np.arange(n)[None, :] % n) / n
    fr, fi = np.cos(ang), np.sin(ang)
    f_data = np.block([[fr[:, :hn], -fi[:, :hn]], [fi[:, :hn], fr[:, :hn]]])
    f_filt = np.concatenate([fr, fi], axis=0)
    er, ei = fr[:hn], -fi[:hn]
    f_inv = np.block([[er, -ei], [ei, er]]) / float(n * n)
    k1 = jnp.arange(n, dtype=jnp.int32)[:, None, None]
    k2 = jnp.arange(n, dtype=jnp.int32)[None, :, None]
    m2 = jnp.arange(n, dtype=jnp.int32)[None, None, :]
    idx = (m2 * (k1 + n * k2)) % (n * n)
    ang2 = idx.astype(F32) * (-2.0 * math.pi / (n * n))
    gr, gi = jnp.cos(ang2), jnp.sin(ang2)
    g = jnp.concatenate([jnp.concatenate([gr, -gi], axis=2), jnp.concatenate([gi, gr], axis=2)], axis=1)
    h = jnp.swapaxes(g, 1, 2)

    def widen(f):
        return jnp.asarray(np.kron(f, np.eye(SUBLANES)), BF16)

    return widen(f_data), widen(f_filt), widen(f_inv), g.astype(BF16), h.astype(BF16)


def _hyena_filter_tables(seq):
    t = jnp.linspace(0.0, 1.0, seq, dtype=F32)[:, None]
    w = 2.0 * math.pi * jnp.arange(seq, dtype=F32)[:, None] / seq
    f = jnp.linspace(1e-4, HY_BANDS - 1, HY_BANDS, dtype=F32)[None, :]
    emb = jnp.concatenate([t, jnp.cos(f * w), -jnp.sin(f * w)], axis=-1)
    pos = jnp.concatenate([jnp.arange(seq), jnp.array([0]), jnp.arange(seq - 1, 0, -1)])
    emb = jnp.pad(emb[pos], ((0, 0), (0, HY_EMB_PAD - HY_EMB)))
    deltas = jnp.abs(jnp.linspace(math.log(HY_DECAY_TARGET) / HY_SLOW_DECAY,
                                  math.log(HY_DECAY_TARGET) / HY_FAST_DECAY, HY_WIDTH, dtype=F32))
    return emb, jnp.tile(deltas, HY_ORDER)[None, :]


def _hyena(hv, hx1, hx2, w1, b1, w2, b2, w3, freq, skip, kb):
    bsz, seq, c = hv.shape
    n = int(round(math.sqrt(2 * seq)))
    assert n * n == 2 * seq and bsz % 2 == 0
    hn, p = n // 2, bsz // 2
    f_data, f_filt, f_inv, gmat, hmat = _dft_tables(n)

    emb, deltas2 = _hyena_filter_tables(seq)
    w1p = jnp.pad(w1, ((0, HY_EMB_PAD - HY_EMB), (0, 0)))
    w3r = w3.reshape(w3.shape[0], HY_ORDER, 2, c)
    w3sel = jnp.stack([w3r[:, :, 0, :].reshape(-1, HY_ORDER * c), w3r[:, :, 1, :].reshape(-1, HY_ORDER * c)])
    full, asum = _filter_call(emb, w1p, b1[None], w2, b2[None], w3sel, freq[None], deltas2, seq, min(512, seq))
    c2 = HY_ORDER * c
    kf = _fb_call(_fa_call(full.reshape(2, 1, hn, n, c2), f_filt), gmat, asum)

    def view(t):
        return t.reshape(2, p, hn, n, c)

    def long_conv(u5, m5, order):
        bm = _mid_call(_fa_call(u5, f_data), gmat, hmat, kf, order, kb)
        return _fc_call(bm, f_inv, u5, m5, skip[order][None, :])

    z = long_conv(view(hv), view(hx1), 0)
    return long_conv(z, view(hx2), 1).reshape(bsz, seq, c)


def _merge_kernel(x_ref, at_ref, hy_ref, g_ref, mod_ref, wba_ref, wbh_ref, wo_ref, o_ref):
    d = x_ref.shape[2]
    g = g_ref[0].astype(F32)
    y = (jax.nn.sigmoid(g[:, :d]) * _dot(at_ref[0], wba_ref[...])
         + jax.nn.sigmoid(g[:, d:]) * _dot(hy_ref[0], wbh_ref[...]))
    o_ref[0] = x_ref[0] + mod_ref[0, 2:3, :] * _dot(y, wo_ref[...])


def _merge_call(x, attn, hy, gate, mod, wba, wbh, wo, tm):
    bsz, s, d = x.shape

    def tok(w):
        return pl.BlockSpec((1, tm, w), lambda b, i: (b, i, 0))

    return pl.pallas_call(
        _merge_kernel,
        grid=(bsz, s // tm),
        in_specs=[tok(d), tok(attn.shape[2]), tok(hy.shape[2]), tok(2 * d),
                  pl.BlockSpec((1, SUBLANES, d), lambda b, i: (b, 0, 0)),
                  _full(wba.shape), _full(wbh.shape), _full(wo.shape)],
        out_specs=tok(d),
        out_shape=jax.ShapeDtypeStruct((bsz, s, d), F32),
        compiler_params=_params("arbitrary", "arbitrary"),
        name="merge",
    )(x, attn, hy, gate, mod, wba, wbh, wo)


def _route_kernel(xm_ref, mod_ref, nf_ref, wrt_ref, rb_ref, tri_ref, lt_ref, h2_out, w_out, p_out, col_out, row_out):
    tm = xm_ref.shape[1]
    ng, gs = N_GROUPS, GROUP_SIZE

    h2 = _prenorm(xm_ref[0], mod_ref, 3, nf_ref[...])
    h2_out[0] = h2.astype(h2_out.dtype)
    scores = jax.nn.sigmoid(_dot3(wrt_ref[...], h2, NT_DIMS))
    sel = scores + rb_ref[...]
    slabs = [sel[ng * j:ng * (j + 1)] for j in range(gs)]

    top1 = jnp.full((ng, tm), -jnp.inf, F32)
    top2 = top1
    for x in slabs:
        top2 = jnp.maximum(top2, jnp.minimum(top1, x))
        top1 = jnp.maximum(top1, x)
    gscore = top1 + top2
    gid = lax.broadcasted_iota(jnp.int32, (ng, 1), 0)
    rank = jnp.zeros((ng, tm), jnp.int32)
    for g2 in range(ng):
        row = gscore[g2:g2 + 1]
        beats = (row > gscore) | ((row == gscore) & (g2 < gid))
        rank = rank + beats.astype(jnp.int32)
    gmask = rank < TOPK_GROUPS

    cand = [jnp.where(gmask, x, -jnp.inf) for x in slabs]
    eid = [gid * gs + j for j in range(gs)]
    chosen = []
    for _ in range(TOP_K):
        best = functools.reduce(jnp.maximum, cand)
        best = jnp.max(best, axis=0, keepdims=True)
        idx = functools.reduce(jnp.minimum, [jnp.where(cand[j] == best, eid[j], N_EXPERTS) for j in range(gs)])
        idx = jnp.min(idx, axis=0, keepdims=True)
        chosen.append(idx)
        cand = [jnp.where(eid[j] == idx, -jnp.inf, cand[j]) for j in range(gs)]

    mask = [functools.reduce(jnp.logical_or, [eid[j] == idx for idx in chosen]) for j in range(gs)]
    maskb = jnp.concatenate(mask, axis=0)
    maskf = jnp.where(maskb, 1.0, 0.0)
    mask16 = maskf.astype(BF16)
    wsel = jnp.where(maskb, scores, 0.0)
    w_out[...] = wsel / jnp.sum(wsel, axis=0, keepdims=True) * ROUTE_SCALE
    before = jnp.dot(mask16, tri_ref[...], preferred_element_type=F32)
    p_out[...] = jnp.where(maskb, before, -1.0).astype(p_out.dtype)

    def extents(cnt, lower_sum):
        units = jnp.floor((cnt + (RUN_ALIGN - 1)) * (1.0 / RUN_ALIGN))
        start = RUN_ALIGN * lower_sum(units.astype(BF16))
        return start, start + RUN_ALIGN * units

    cnt_c = jnp.sum(maskf, axis=1, keepdims=True)
    start_c, end_c = extents(jnp.broadcast_to(cnt_c, (N_EXPERTS, LANES)),
                             lambda u: jnp.dot(lt_ref[...], u, preferred_element_type=F32))
    lane = lax.broadcasted_iota(jnp.int32, (N_EXPERTS, LANES), 1)
    col_out[0] = jnp.where(lane == 0, cnt_c, jnp.where(lane == 1, start_c, end_c))
    cnt_r = lax.dot_general(jnp.ones((SUBLANES, tm), BF16), mask16, NT_DIMS, preferred_element_type=F32)
    start_r, end_r = extents(cnt_r, lambda u: lax.dot_general(u, lt_ref[...], NT_DIMS, preferred_element_type=F32))
    sub = lax.broadcasted_iota(jnp.int32, (SUBLANES, N_EXPERTS), 0)
    row_out[0] = jnp.where(sub == 0, start_r, end_r)


def _route_call(xm, mod, norm_ffn, wrt, rbias, lower, tm):
    bsz, s, d = xm.shape
    t = bsz * s
    nt = s // tm
    tri = (jnp.arange(tm)[:, None] < jnp.arange(tm)[None, :]).astype(BF16)
    tok = pl.BlockSpec((N_EXPERTS, tm), lambda i: (0, i))
    return pl.pallas_call(
        _route_kernel,
        grid=(t // tm,),
        in_specs=[pl.BlockSpec((1, tm, d), lambda i: (i // nt, i % nt, 0)),
                  pl.BlockSpec((1, SUBLANES, d), lambda i: (i // nt, 0, 0)),
                  _full(norm_ffn.shape), _full(wrt.shape), _full(rbias.shape), _full(tri.shape),
                  _full(lower.shape)],
        out_specs=[pl.BlockSpec((1, tm, d), lambda i: (i // nt, i % nt, 0)), tok, tok,
                   pl.BlockSpec((1, N_EXPERTS, LANES), lambda i: (i, 0, 0)),
                   pl.BlockSpec((1, SUBLANES, N_EXPERTS), lambda i: (i, 0, 0))],
        out_shape=[jax.ShapeDtypeStruct((bsz, s, d), BF16), jax.ShapeDtypeStruct((N_EXPERTS, t), F32),
                   jax.ShapeDtypeStruct((N_EXPERTS, t), BF16),
                   jax.ShapeDtypeStruct((t // tm, N_EXPERTS, LANES), F32),
                   jax.ShapeDtypeStruct((t // tm, SUBLANES, N_EXPERTS), F32)],
        compiler_params=_params("arbitrary"),
        name="route",
    )(xm, mod, norm_ffn, wrt, rbias, tri, lower)


def _pack(x):
    w = x.shape[1] // 2
    lo = lax.bitcast_convert_type(x[:, :w].astype(BF16).astype(F32), jnp.uint32)
    hi = lax.bitcast_convert_type(x[:, w:].astype(BF16).astype(F32), jnp.uint32)
    return hi | (lo >> 16)


def _unpack(u):
    lo = lax.bitcast_convert_type(u << 16, F32).astype(BF16)
    hi = lax.bitcast_convert_type(u & jnp.uint32(0xFFFF0000), F32).astype(BF16)
    return lo, hi


def _pow2_pieces(units, limit):
    bit = 1
    while bit * 2 <= limit:
        bit *= 2
    while bit:
        yield (units & bit) != 0, units & ~(2 * bit - 1), bit
        bit //= 2


def _rows_copy(vm_ref, hbm_ref, sem, vm_row, hbm_row, rows, to_hbm):
    def aligned(r):
        return r if isinstance(r, int) else pl.multiple_of(r, RUN_ALIGN)

    v = vm_ref.at[pl.ds(aligned(vm_row), rows), :]
    h = hbm_ref.at[pl.ds(aligned(hbm_row), rows), :]
    return pltpu.make_async_copy(v, h, sem) if to_hbm else pltpu.make_async_copy(h, v, sem)


def _run_copies(vm_ref, hbm_ref, sem, n8, vm_row, hbm_row, limit, to_hbm, act):
    def emit(pieces):
        for on, off, size in pieces:
            @pl.when(on)
            def _():
                act(_rows_copy(vm_ref, hbm_ref, sem, vm_row + RUN_ALIGN * off, hbm_row + RUN_ALIGN * off,
                               RUN_ALIGN * size, to_hbm))

    pieces = list(_pow2_pieces(n8, limit))
    long_pieces = [p for p in pieces if p[2] >= LONG_RUN]
    if long_pieces:
        pl.when(n8 >= LONG_RUN)(lambda: emit(long_pieces))
    emit([p for p in pieces if p[2] < LONG_RUN])


def _wait_rows(vm_ref, hbm_ref, sem, units, limit, to_hbm):
    for on, _, size in _pow2_pieces(units, limit):
        @pl.when(on)
        def _():
            _rows_copy(vm_ref, hbm_ref, sem, 0, 0, RUN_ALIGN * size, to_hbm).wait()


def _dispatch_kernel(dst_ref, fill_ref, ts_ref, t8_ref, nu_ref, pos_ref, ext_ref, h_ref, xs_out, srt, zbuf, sem):
    step = pl.program_id(0)
    rows = srt.shape[0]
    units = rows // RUN_ALIGN
    rid = lax.broadcasted_iota(jnp.int32, (rows, 1), 0).astype(F32)
    start = ext_ref[0, 0:1, :]
    member = jnp.where((rid >= start) & (rid < ext_ref[0, 1:2, :]), 1.0, 0.0)
    offset = rid - jnp.sum(member * start, axis=1, keepdims=True)
    pos = jnp.dot(member.astype(BF16), pos_ref[...], preferred_element_type=F32)
    sel = jnp.where(pos == offset, 1.0, 0.0).astype(BF16)
    srt[...] = _pack(jnp.dot(sel, h_ref[...], preferred_element_type=F32))

    def send(u, c):
        _rows_copy(srt, xs_out, sem, RUN_ALIGN * u, dst_ref[step * units + u], RUN_ALIGN, True).start()
        return c

    lax.fori_loop(0, fill_ref[step], send, 0)
    _wait_rows(srt, xs_out, sem, fill_ref[step], units, True)

    @pl.when(step == pl.num_programs(0) - 1)
    def _():
        zbuf[...] = jnp.zeros(zbuf.shape, zbuf.dtype)
        nblk = xs_out.shape[0] // EXPERT_BLOCK

        def fill(act):
            def tails(e, c):
                _run_copies(zbuf, xs_out, sem, t8_ref[e], 0, ts_ref[e], EXPERT_BLOCK // RUN_ALIGN - 1, True, act)
                return c

            def blocks(b, c):
                act(pltpu.make_async_copy(
                    zbuf, xs_out.at[pl.ds(pl.multiple_of(b * EXPERT_BLOCK, EXPERT_BLOCK), EXPERT_BLOCK), :], sem))
                return c

            lax.fori_loop(0, N_EXPERTS, tails, 0)
            lax.fori_loop(nu_ref[0], nblk, blocks, 0)

        fill(lambda cp: cp.start())
        fill(lambda cp: cp.wait())


def _dispatch_call(tables, pos_et, ext_rows, h2, nblk, tm):
    t, d = h2.shape
    lrows = TOP_K * tm + N_EXPERTS * RUN_ALIGN
    return pl.pallas_call(
        _dispatch_kernel,
        grid_spec=pltpu.PrefetchScalarGridSpec(
            num_scalar_prefetch=len(tables), grid=(t // tm,),
            in_specs=[pl.BlockSpec((N_EXPERTS, tm), lambda i, *_: (0, i)),
                      pl.BlockSpec((1,) + ext_rows.shape[1:], lambda i, *_: (i, 0, 0)),
                      pl.BlockSpec((tm, d), lambda i, *_: (i, 0))],
            out_specs=pl.BlockSpec(memory_space=pl.ANY),
            scratch_shapes=[pltpu.VMEM((lrows, d // 2), jnp.uint32), pltpu.VMEM((EXPERT_BLOCK, d // 2), jnp.uint32),
                            pltpu.SemaphoreType.DMA(())]),
        out_shape=jax.ShapeDtypeStruct((nblk * EXPERT_BLOCK, d // 2), jnp.uint32),
        compiler_params=_params("arbitrary"),
        name="dispatch",
    )(*tables, pos_et, ext_rows, h2)


def _expert_kernel(blk_ref, nused_ref, x_ref, wgu_ref, wd_ref, y_ref):
    used = pl.program_id(0) < nused_ref[0]

    @pl.when(used)
    def _():
        lo, hi = _unpack(x_ref[...])
        half = lo.shape[1]
        gu = (jnp.dot(lo, wgu_ref[0, :half, :], preferred_element_type=F32)
              + jnp.dot(hi, wgu_ref[0, half:, :], preferred_element_type=F32))
        a = _silu(gu[:, :EXPERT_FF]) * gu[:, EXPERT_FF:]
        y_ref[...] = _pack(_dot(a, wd_ref[0]))

    @pl.when(jnp.logical_not(used))
    def _():
        y_ref[...] = jnp.zeros(y_ref.shape, y_ref.dtype)


def _expert_call(blk_e, nused, xs, wgu, wd):
    rows, d = xs.shape
    nblk = rows // EXPERT_BLOCK

    def row_map(i, blk, nu):
        return (jnp.minimum(i, nu[0] - 1), 0)

    return pl.pallas_call(
        _expert_kernel,
        grid_spec=pltpu.PrefetchScalarGridSpec(
            num_scalar_prefetch=2, grid=(nblk,),
            in_specs=[pl.BlockSpec((EXPERT_BLOCK, d), row_map),
                      pl.BlockSpec((1,) + wgu.shape[1:], lambda i, blk, nu: (blk[i], 0, 0)),
                      pl.BlockSpec((1,) + wd.shape[1:], lambda i, blk, nu: (blk[i], 0, 0))],
            out_specs=pl.BlockSpec((EXPERT_BLOCK, d), lambda i, blk, nu: (i, 0))),
        out_shape=jax.ShapeDtypeStruct((rows, d), jnp.uint32),
        compiler_params=_params("arbitrary"),
        name="expert",
    )(blk_e, nused, xs, wgu, wd)


def _combine_kernel(dst_ref, fill_ref, ys_hbm, pos_ref, w_ref, ext_ref, xm_ref, h_ref, mod_ref, wsgu_ref,
                    wsd_ref, fn_ref, o_ref, ybuf, sem):
    step = pl.program_id(0)
    rows = ybuf.shape[0]
    units = rows // RUN_ALIGN
    for u in range(units):
        _rows_copy(ybuf, ys_hbm, sem, RUN_ALIGN * u, dst_ref[step * units + u], RUN_ALIGN, False).start()
    gu = _dot(h_ref[...], wsgu_ref[...])
    ff = gu.shape[1] // 2
    shared = _dot(_silu(gu[:, :ff]) * gu[:, ff:], wsd_ref[...])
    cid = lax.broadcasted_iota(jnp.int32, (1, rows), 1).astype(F32)
    start = ext_ref[0, :, 1:2]
    member = jnp.where((cid >= start) & (cid < ext_ref[0, :, 2:3]), 1.0, 0.0)
    offset = cid - jnp.sum(member * start, axis=0, keepdims=True)
    member = member.astype(BF16)
    pos = jnp.dot(pos_ref[...], member, preferred_element_type=F32)
    mix = jnp.where(pos == offset, jnp.dot(w_ref[...].astype(BF16), member, preferred_element_type=F32), 0.0)
    mix = mix.astype(BF16)
    _rows_copy(ybuf, ys_hbm, sem, 0, 0, rows, False).wait()
    filled = RUN_ALIGN * fill_ref[step]
    rid = lax.broadcasted_iota(jnp.int32, (rows, 1), 0)
    lo, hi = _unpack(jnp.where(rid < filled, ybuf[...], jnp.uint32(0)))
    routed = jnp.concatenate([jnp.dot(mix, lo, preferred_element_type=F32),
                              jnp.dot(mix, hi, preferred_element_type=F32)], axis=1)
    x = xm_ref[...] + mod_ref[0, 5:6, :] * (routed + shared)
    o_ref[...] = _rms(x, fn_ref[...])


def _combine_call(tables, ys, pos_te, w_te, ext_cols, xm, h2, mod, wsgu, wsd, final_norm, tm, tiles_per_batch):
    t, d = xm.shape
    lrows = TOP_K * tm + N_EXPERTS * RUN_ALIGN
    tok = pl.BlockSpec((tm, d), lambda i, *_: (i, 0))
    per_e = pl.BlockSpec((tm, N_EXPERTS), lambda i, *_: (i, 0))
    return pl.pallas_call(
        _combine_kernel,
        grid_spec=pltpu.PrefetchScalarGridSpec(
            num_scalar_prefetch=len(tables), grid=(t // tm,),
            in_specs=[pl.BlockSpec(memory_space=pl.ANY), per_e, per_e,
                      pl.BlockSpec((1,) + ext_cols.shape[1:], lambda i, *_: (i, 0, 0)), tok, tok,
                      pl.BlockSpec((1, SUBLANES, d), lambda i, *_: (i // tiles_per_batch, 0, 0)),
                      _full(wsgu.shape), _full(wsd.shape), _full(final_norm.shape)],
            out_specs=tok,
            scratch_shapes=[pltpu.VMEM((lrows, d // 2), jnp.uint32), pltpu.SemaphoreType.DMA(())]),
        out_shape=jax.ShapeDtypeStruct((t, d), F32),
        compiler_params=_params("arbitrary"),
        name="combine",
    )(*tables, ys, pos_te, w_te, ext_cols, xm, h2, mod, wsgu, wsd, final_norm)


def _moe(xm, mod, norm_ffn, w_router, router_bias, wg, wu, wd, wsg, wsu, wsd, final_norm, tm):
    bsz, s, d = xm.shape
    t = bsz * s
    nt = t // tm
    perm = (np.arange(N_EXPERTS) % N_GROUPS) * GROUP_SIZE + np.arange(N_EXPERTS) // N_GROUPS
    wrt = w_router.T[perm]
    rbias = router_bias[perm][:, None]
    lower = jnp.asarray(perm[None, :] < perm[:, None], BF16)
    h2, w_et, pos_et, ext_cols, ext_rows = _route_call(xm, mod, norm_ffn, wrt, rbias, lower, tm)

    inv = np.argsort(perm)
    n8 = (ext_cols[:, :, 0].astype(jnp.int32)[:, inv] + (RUN_ALIGN - 1)) // RUN_ALIGN
    run = RUN_ALIGN * n8
    ls = jnp.cumsum(run, axis=1) - run
    tot = jnp.sum(run, axis=0)
    padded = (tot + EXPERT_BLOCK - 1) // EXPERT_BLOCK * EXPERT_BLOCK
    pad_end = jnp.cumsum(padded)
    gs = (pad_end - padded)[None, :] + jnp.cumsum(run, axis=0) - run
    nblk = -(-(t * TOP_K + nt * N_EXPERTS * (RUN_ALIGN - 1)) // EXPERT_BLOCK) + N_EXPERTS
    blk_first = jnp.arange(nblk, dtype=jnp.int32)[:, None] * EXPERT_BLOCK
    blk_e = jnp.minimum(jnp.sum((pad_end[None, :] <= blk_first).astype(jnp.int32), axis=1), N_EXPERTS - 1)
    nused = (pad_end[-1:] // EXPERT_BLOCK).astype(jnp.int32)
    unit = jnp.arange((TOP_K * tm + N_EXPERTS * RUN_ALIGN) // RUN_ALIGN, dtype=jnp.int32)[None, :, None]
    ls8 = (ls // RUN_ALIGN)[:, None, :]
    inside = (ls8 <= unit) & (unit < ls8 + n8[:, None, :])
    dst = jnp.sum(jnp.where(inside, gs[:, None, :] + RUN_ALIGN * (unit - ls8), 0), axis=2)
    tables = [dst.reshape(-1).astype(jnp.int32), jnp.sum(n8, axis=1).astype(jnp.int32)]
    tails = [(pad_end - padded + tot).astype(jnp.int32), ((padded - tot) // RUN_ALIGN).astype(jnp.int32), nused]

    h2f = h2.reshape(t, d)
    xs = _dispatch_call(tables + tails, pos_et, ext_rows, h2f, nblk, tm)
    wgu = jnp.concatenate([wg, wu], axis=2).astype(BF16)
    ys = _expert_call(blk_e, nused, xs, wgu, wd.astype(BF16))
    wsgu = jnp.concatenate([wsg, wsu], axis=1).astype(BF16)
    out = _combine_call(tables, ys, pos_et.T, w_et.T, ext_cols, xm.reshape(t, d), h2f, mod, wsgu, wsd.astype(BF16),
                        final_norm, tm, s // tm)
    return out.reshape(bsz, s, d)


def _rope_tables(s):
    rows = s // GRID_W
    row = jnp.broadcast_to(jnp.arange(rows, dtype=F32)[:, None], (rows, GRID_W)).reshape(-1)
    col = jnp.broadcast_to(jnp.arange(GRID_W, dtype=F32)[None, :], (rows, GRID_W)).reshape(-1)
    half = QK_ROPE // 2
    inv_freq = ROPE_THETA ** (-jnp.arange(0, half, 2, dtype=F32) / half)
    ar, ac = row[:, None] * inv_freq, col[:, None] * inv_freq
    ones = jnp.ones((s, QK_NOPE), F32)
    tail = HEAD_PAD - QK_NOPE - QK_ROPE
    cos_t = jnp.concatenate([ones, jnp.cos(ar), jnp.cos(ar), jnp.cos(ac), jnp.cos(ac), jnp.ones((s, tail), F32)], 1)
    sin_t = jnp.concatenate([0 * ones, -jnp.sin(ar), jnp.sin(ar), -jnp.sin(ac), jnp.sin(ac),
                             jnp.zeros((s, tail), F32)], 1)
    return cos_t, sin_t


_Q4 = QK_ROPE // 4
ROPE_SWAP = np.concatenate([np.arange(_Q4, 2 * _Q4), np.arange(0, _Q4), np.arange(3 * _Q4, 4 * _Q4),
                            np.arange(2 * _Q4, 3 * _Q4)])


def _rope_slot(w, swap):
    if swap:
        w = w[..., ROPE_SWAP]
    pad = [(0, 0)] * (w.ndim - 1) + [(QK_NOPE, HEAD_PAD - QK_NOPE - QK_ROPE)]
    return jnp.pad(w, pad)


def kernel(x, c, ctx, c_ctx, w_mod, b_mod, norm_mix, norm_ffn, w_in, b_in, q_norm, w_uq, kv_norm, w_ukv, w_branch_attn, hy_conv_w, hy_conv_b, hy_filt_w1, hy_filt_b1, hy_filt_w2, hy_filt_b2, hy_filt_w3, hy_filt_freq, hy_skip, w_branch_hyena, w_out, w_router, router_bias, w_exp_gate, w_exp_up, w_exp_down, w_sh_gate, w_sh_up, w_sh_down, final_norm,
           tiles=None):
    bsz, s, d = x.shape
    tl = dict(inproj=256, tq=1024, tk=1408, fft_kb=4, merge=512, moe=256)
    tl.update(tiles or {})
    assert w_mod.shape[0] == 1, "single-layer trunk"
    i = 0

    rows = -(-(bsz + 1) // SUBLANES) * SUBLANES
    c_rows = jnp.pad(jnp.concatenate([c, c_ctx[None]], axis=0), ((0, rows - bsz - 1), (0, 0)))
    mod_all = _mod_call(c_rows, w_mod[i], b_mod[i])
    mod_all = jnp.pad(mod_all.reshape(rows, 6, d), ((0, 0), (0, SUBLANES - 6), (0, 0)))
    mod, modc = mod_all[:bsz], mod_all[bsz:bsz + 1]

    cuts = np.cumsum([Q_LORA, KV_LORA, QK_ROPE, 3 * HY_WIDTH])
    wi, bi = w_in[i], b_in[i][None]
    w_q, w_kv, w_pe, w_hy, w_g = jnp.split(wi, cuts, axis=1)
    b_q, b_kv, b_pe, b_hy, b_g = jnp.split(bi, cuts, axis=1)
    wa = jnp.concatenate([w_q, w_kv, _rope_slot(w_pe, False), _rope_slot(w_pe, True)], axis=1).astype(BF16)
    ba = jnp.concatenate([b_q, b_kv, _rope_slot(b_pe, False), _rope_slot(b_pe, True)], axis=1)
    wq3 = w_uq[i].reshape(Q_LORA, N_HEADS, QK_NOPE + QK_ROPE) * (ATTN_SCALE * math.log2(math.e))
    tail = ((0, 0), (0, 0), (0, HEAD_PAD - QK_NOPE))
    wuq = (jnp.pad(wq3[..., :QK_NOPE], tail) + _rope_slot(wq3[..., QK_NOPE:], False)).reshape(Q_LORA, -1).astype(BF16)
    wuqs = _rope_slot(wq3[..., QK_NOPE:], True).reshape(Q_LORA, -1).astype(BF16)
    wkv3 = w_ukv[i].reshape(KV_LORA, N_HEADS, QK_NOPE + V_HEAD)
    wuk = jnp.pad(wkv3[..., :QK_NOPE], tail).reshape(KV_LORA, -1).astype(BF16)
    wuvt = wkv3[..., QK_NOPE:].reshape(KV_LORA, -1).T.astype(BF16)
    nm, qn, kvn = norm_mix[i][None], q_norm[i][None], kv_norm[i][None]

    w_c = jnp.concatenate([w_kv, _rope_slot(w_pe, False)], axis=1).astype(BF16)
    b_c = jnp.concatenate([b_kv, _rope_slot(b_pe, False)], axis=1)
    ck, cvt = _ctx_call(ctx, modc, nm, w_c, b_c, kvn, wuk, wuvt)

    cos_t, sin_t = _rope_tables(s)
    q, k, vt, hv, hx1, hx2, gate = _inproj_call(
        x, mod, nm, wa, ba, w_hy.astype(BF16), b_hy, w_g.astype(BF16), b_g, qn, wuq, wuqs, kvn, wuk, wuvt,
        cos_t, sin_t, hy_conv_w[i], hy_conv_b[i][None], tl["inproj"])

    attn = _attn_call(q, jnp.concatenate([ck, k], axis=2), jnp.concatenate([cvt, vt], axis=3), tl["tq"], tl["tk"])
    hy = _hyena(hv, hx1, hx2, hy_filt_w1[i], hy_filt_b1[i], hy_filt_w2[i], hy_filt_b2[i], hy_filt_w3[i],
                hy_filt_freq[i], hy_skip[i], tl["fft_kb"])
    xm = _merge_call(x, attn, hy, gate, mod, w_branch_attn[i].astype(BF16), w_branch_hyena[i].astype(BF16),
                     w_out[i].astype(BF16), tl["merge"])
    return _moe(xm, mod, norm_ffn[i][None], w_router[i], router_bias[i], w_exp_gate[i], w_exp_up[i], w_exp_down[i],
                w_sh_gate[i], w_sh_up[i], w_sh_down[i], final_norm[None], tl["moe"])
```

```python
import functools
import math

import numpy as np
import jax
import jax.numpy as jnp
from jax import lax
from jax.experimental import pallas as pl
from jax.experimental.pallas import tpu as pltpu

GRID_W = 64
N_HEADS = 8
QK_NOPE = 64
QK_ROPE = 32
V_HEAD = 64
Q_LORA = 256
KV_LORA = 128
ROPE_THETA = 10000.0
ATTN_SCALE = 1.0 / math.sqrt(QK_NOPE + QK_ROPE)
HY_WIDTH = 512
HY_ORDER = 2
HY_SHORT = 3
HY_BANDS = 8
HY_EMB = 1 + 2 * HY_BANDS
HY_EMB_PAD = 32
HY_FAST_DECAY = 0.3
HY_SLOW_DECAY = 1.5
HY_DECAY_TARGET = 1e-2
N_EXPERTS = 64
N_GROUPS = 8
GROUP_SIZE = N_EXPERTS // N_GROUPS
TOPK_GROUPS = 4
TOP_K = 8
EXPERT_FF = 256
ROUTE_SCALE = 2.5
EXPERT_BLOCK = 1024
RUN_ALIGN = 8
LONG_RUN = 8
NORM_EPS = 1e-6

HEAD_PAD = 128
Q_CHUNK = 512
AHEAD = 2
LANES = 128
SUBLANES = 8
VMEM_LIMIT = 48 * 1024 * 1024

F32 = jnp.float32
BF16 = jnp.bfloat16
NT_DIMS = (((1,), (1,)), ((), ()))
NN_DIMS = (((1,), (0,)), ((), ()))


def _params(*sem):
    return pltpu.CompilerParams(dimension_semantics=sem, vmem_limit_bytes=VMEM_LIMIT)


def _dot(a, b):
    return jnp.dot(a.astype(BF16), b.astype(BF16), preferred_element_type=F32)


def _split(a):
    hi = a.astype(BF16)
    lo = (a - hi.astype(F32)).astype(BF16)
    return hi, lo


def _dot3(a, b, dims=NN_DIMS):
    ah, al = _split(a)
    bh, bl = _split(b)
    d = functools.partial(lax.dot_general, dimension_numbers=dims, preferred_element_type=F32)
    return d(ah, bh) + (d(ah, bl) + d(al, bh))


def _rms(x, g):
    return x * lax.rsqrt(jnp.mean(x * x, axis=-1, keepdims=True) + NORM_EPS) * g


def _silu(x):
    return x * jax.nn.sigmoid(x)


def _full(shape):
    nd = len(shape)
    return pl.BlockSpec(shape, lambda *_: (0,) * nd)


def _mod_kernel(c_ref, w_ref, b_ref, o_ref):
    o_ref[...] = _dot3(_silu(c_ref[...]), w_ref[...]) + b_ref[...]


def _mod_call(c_rows, w_mod, b_mod):
    r, d = c_rows.shape
    n = w_mod.shape[1]
    bn = 1024
    return pl.pallas_call(
        _mod_kernel,
        grid=(n // bn,),
        in_specs=[_full((r, d)), pl.BlockSpec((d, bn), lambda j: (0, j)), pl.BlockSpec((1, bn), lambda j: (0, j))],
        out_specs=pl.BlockSpec((r, bn), lambda j: (0, j)),
        out_shape=jax.ShapeDtypeStruct((r, n), F32),
        compiler_params=_params("arbitrary"),
        name="mod",
    )(c_rows, w_mod, b_mod.reshape(1, n))


def _prenorm(x, mod_ref, row, g):
    shift = mod_ref[0, row:row + 1, :]
    scale = mod_ref[0, row + 1:row + 2, :]
    return _rms(x, g) * (1.0 + scale) + shift


def _kv_heads(kv_lat, kpe, kvn_ref, wuk_ref, wuvt_ref, k_out, vt_out):
    kvn = _rms(kv_lat, kvn_ref[...]).astype(BF16)
    kk = _dot(kvn, wuk_ref[...])
    vt = lax.dot_general(wuvt_ref[...], kvn, NT_DIMS, preferred_element_type=F32)
    ones = jnp.ones((HEAD_PAD - V_HEAD, vt.shape[1]), F32)
    for h in range(N_HEADS):
        k_out[0, h] = (kk[:, HEAD_PAD * h:HEAD_PAD * (h + 1)] + kpe).astype(BF16)
        vt_out[0, h] = jnp.concatenate([vt[V_HEAD * h:V_HEAD * (h + 1)], ones], axis=0).astype(BF16)


def _ctx_kernel(c_ref, mod_ref, nm_ref, w_ref, b_ref, kvn_ref, wuk_ref, wuv_ref, k_out, v_out):
    h = _prenorm(c_ref[0], mod_ref, 0, nm_ref[...]).astype(BF16)
    a = _dot(h, w_ref[...]) + b_ref[...]
    _kv_heads(a[:, :KV_LORA], a[:, KV_LORA:], kvn_ref, wuk_ref, wuv_ref, k_out, v_out)


def _ctx_call(ctx, modc, norm_mix, w_c, b_c, kv_norm, w_uk, w_uv):
    bsz, n, d = ctx.shape
    return pl.pallas_call(
        _ctx_kernel,
        grid=(bsz,),
        in_specs=[pl.BlockSpec((1, n, d), lambda b: (b, 0, 0)), _full(modc.shape), _full(norm_mix.shape),
                  _full(w_c.shape), _full(b_c.shape), _full(kv_norm.shape), _full(w_uk.shape), _full(w_uv.shape)],
        out_specs=[pl.BlockSpec((1, N_HEADS, n, HEAD_PAD), lambda b: (b, 0, 0, 0)),
                   pl.BlockSpec((1, N_HEADS, HEAD_PAD, n), lambda b: (b, 0, 0, 0))],
        out_shape=[jax.ShapeDtypeStruct((bsz, N_HEADS, n, HEAD_PAD), BF16),
                   jax.ShapeDtypeStruct((bsz, N_HEADS, HEAD_PAD, n), BF16)],
        compiler_params=_params("arbitrary"),
        name="ctx",
    )(ctx, modc, norm_mix, w_c, b_c, kv_norm, w_uk, w_uv)


def _inproj_kernel(x_ref, xp_ref, xn_ref, mod_ref, nm_ref, wa_ref, ba_ref, why_ref, bhy_ref, wg_ref, bg_ref,
                   qn_ref, wuq_ref, wuqs_ref, kvn_ref, wuk_ref, wuv_ref, cos_ref, sin_ref, cw_ref, cb_ref,
                   q_out, k_out, v_out, hv_out, hx1_out, hx2_out, g_out):
    i = pl.program_id(0)
    tm = x_ref.shape[1]
    nm = nm_ref[...]
    h = _prenorm(x_ref[0], mod_ref, 0, nm).astype(BF16)
    a = _dot(h, wa_ref[...]) + ba_ref[...]
    q_lat = a[:, :Q_LORA]
    kv_lat = a[:, Q_LORA:Q_LORA + KV_LORA]
    kpe_m = a[:, Q_LORA + KV_LORA:Q_LORA + KV_LORA + HEAD_PAD]
    kpe_s = a[:, Q_LORA + KV_LORA + HEAD_PAD:]
    cos = cos_ref[...]
    sin = sin_ref[...]
    qn = _rms(q_lat, qn_ref[...]).astype(BF16)
    qa = _dot(qn, wuq_ref[...])
    qs = _dot(qn, wuqs_ref[...])
    for hh in range(N_HEADS):
        sl = slice(HEAD_PAD * hh, HEAD_PAD * (hh + 1))
        q_out[0, hh] = (qa[:, sl] * cos + qs[:, sl] * sin).astype(BF16)
    _kv_heads(kv_lat, kpe_m * cos + kpe_s * sin, kvn_ref, wuk_ref, wuv_ref, k_out, v_out)
    g_out[0] = (_dot(h, wg_ref[...]) + bg_ref[...]).astype(BF16)

    why = why_ref[...]
    bhy = bhy_ref[...]
    hy = _dot(h, why) + bhy
    hp = _dot(_prenorm(xp_ref[0], mod_ref, 0, nm).astype(BF16), why) + bhy
    hn = _dot(_prenorm(xn_ref[0], mod_ref, 0, nm).astype(BF16), why) + bhy
    prev = jnp.where(i == 0, 0.0, hp[SUBLANES - 1:SUBLANES])
    nxt = jnp.where(i == pl.num_programs(0) - 1, 0.0, hn[0:1])
    rid = lax.broadcasted_iota(jnp.int32, (tm, 1), 0)
    up = jnp.where(rid == 0, prev, pltpu.roll(hy, 1, 0))
    dn = jnp.where(rid == tm - 1, nxt, pltpu.roll(hy, tm - 1, 0))
    u = up * cw_ref[0:1, :] + hy * cw_ref[1:2, :] + dn * cw_ref[2:3, :] + cb_ref[...]
    hv_out[0] = u[:, :HY_WIDTH]
    hx1_out[0] = u[:, HY_WIDTH:2 * HY_WIDTH]
    hx2_out[0] = u[:, 2 * HY_WIDTH:]


def _inproj_call(x, mod, norm_mix, wa, ba, why, bhy, wg, bg, q_norm, wuq, wuqs, kv_norm, wuk, wuvt, cos_t, sin_t, cw,
                 cb, tm):
    bsz, s, d = x.shape
    nt = s // tm
    rb = tm // SUBLANES
    last_rb = s // SUBLANES - 1
    consts = [norm_mix, wa, ba, why, bhy, wg, bg, q_norm, wuq, wuqs, kv_norm, wuk, wuvt]
    in_specs = [
        pl.BlockSpec((1, tm, d), lambda i, b: (b, i, 0)),
        pl.BlockSpec((1, SUBLANES, d), lambda i, b: (b, jnp.maximum(i * rb - 1, 0), 0)),
        pl.BlockSpec((1, SUBLANES, d), lambda i, b: (b, jnp.minimum((i + 1) * rb, last_rb), 0)),
        pl.BlockSpec((1, SUBLANES, d), lambda i, b: (b, 0, 0)),
    ] + [_full(c.shape) for c in consts] + [
        pl.BlockSpec((tm, HEAD_PAD), lambda i, b: (i, 0)),
        pl.BlockSpec((tm, HEAD_PAD), lambda i, b: (i, 0)),
        _full(cw.shape), _full(cb.shape),
    ]
    hw = HY_WIDTH
    out_specs = [
        pl.BlockSpec((1, N_HEADS, tm, HEAD_PAD), lambda i, b: (b, 0, i, 0)),
        pl.BlockSpec((1, N_HEADS, tm, HEAD_PAD), lambda i, b: (b, 0, i, 0)),
        pl.BlockSpec((1, N_HEADS, HEAD_PAD, tm), lambda i, b: (b, 0, 0, i)),
        pl.BlockSpec((1, tm, hw), lambda i, b: (b, i, 0)),
        pl.BlockSpec((1, tm, hw), lambda i, b: (b, i, 0)),
        pl.BlockSpec((1, tm, hw), lambda i, b: (b, i, 0)),
        pl.BlockSpec((1, tm, 2 * d), lambda i, b: (b, i, 0)),
    ]
    out_shape = [
        jax.ShapeDtypeStruct((bsz, N_HEADS, s, HEAD_PAD), BF16),
        jax.ShapeDtypeStruct((bsz, N_HEADS, s, HEAD_PAD), BF16),
        jax.ShapeDtypeStruct((bsz, N_HEADS, HEAD_PAD, s), BF16),
        jax.ShapeDtypeStruct((bsz, s, hw), F32),
        jax.ShapeDtypeStruct((bsz, s, hw), F32),
        jax.ShapeDtypeStruct((bsz, s, hw), F32),
        jax.ShapeDtypeStruct((bsz, s, 2 * d), BF16),
    ]
    return pl.pallas_call(
        _inproj_kernel,
        grid=(nt, bsz),
        in_specs=in_specs,
        out_specs=out_specs,
        out_shape=out_shape,
        compiler_params=_params("arbitrary", "arbitrary"),
        name="inproj",
    )(x, x, x, mod, *consts, cos_t, sin_t, cw, cb)


def _attn_kernel(q_ref, k_ref, vt_ref, o_ref, m_sc, acc_sc):
    j = pl.program_id(2)

    @pl.when(j == 0)
    def _():
        m_sc[...] = jnp.full(m_sc.shape, -jnp.inf, F32)
        acc_sc[...] = jnp.zeros(acc_sc.shape, F32)

    tq = q_ref.shape[2]
    qw = min(tq, Q_CHUNK)
    units = [(h, c) for h in range(N_HEADS) for c in range(0, tq, qw)]

    def scores(u):
        h, c = units[u]
        return lax.dot_general(k_ref[0, h], q_ref[0, h, c:c + qw, :], NT_DIMS,
                               preferred_element_type=F32)

    pending = [scores(u) for u in range(AHEAD)]
    for u, (h, c) in enumerate(units):
        if u + AHEAD < len(units):
            pending.append(scores(u + AHEAD))
        st = pending.pop(0)
        m_prev = m_sc[h, :, c:c + qw]
        m_new = jnp.maximum(m_prev, jnp.max(st, axis=0, keepdims=True))
        pt = jnp.exp2(st - m_new).astype(BF16)
        acc_sc[h, :, c:c + qw] = (jnp.exp2(m_prev - m_new) * acc_sc[h, :, c:c + qw]
                                  + jnp.dot(vt_ref[0, h], pt, preferred_element_type=F32))
        m_sc[h, :, c:c + qw] = m_new

    @pl.when(j == pl.num_programs(2) - 1)
    def _():
        ot = jnp.concatenate([acc_sc[h, :V_HEAD] / acc_sc[h, V_HEAD:V_HEAD + 1] for h in range(N_HEADS)], axis=0)
        o_ref[0] = ot.T.astype(o_ref.dtype)


def _attn_call(q, k, vt, tq, tk):
    bsz, nh, s, dh = q.shape
    nk = k.shape[2]
    dv = nh * V_HEAD
    return pl.pallas_call(
        _attn_kernel,
        grid=(bsz, s // tq, nk // tk),
        in_specs=[
            pl.BlockSpec((1, nh, tq, dh), lambda b, i, j: (b, 0, i, 0)),
            pl.BlockSpec((1, nh, tk, dh), lambda b, i, j: (b, 0, j, 0)),
            pl.BlockSpec((1, nh, dh, tk), lambda b, i, j: (b, 0, 0, j)),
        ],
        out_specs=pl.BlockSpec((1, tq, dv), lambda b, i, j: (b, i, 0)),
        out_shape=jax.ShapeDtypeStruct((bsz, s, dv), BF16),
        scratch_shapes=[pltpu.VMEM((nh, 1, tq), F32), pltpu.VMEM((nh, dh, tq), F32)],
        compiler_params=_params("arbitrary", "arbitrary", "arbitrary"),
        name="attn",
    )(q, k, vt)


def _filter_kernel(emb_ref, w1_ref, b1_ref, w2_ref, b2_ref, w3_ref, fr_ref, dl_ref, full_out, asum_out, *, seq):
    r = pl.program_id(0)
    rb = emb_ref.shape[0]
    emb = emb_ref[...]
    fr = fr_ref[...]
    h = jnp.sin(fr * (_dot3(emb, w1_ref[...]) + b1_ref[...]))
    h = jnp.sin(fr * (_dot3(h, w2_ref[...]) + b2_ref[...]))
    k = _dot3(h, w3_ref[0]) * jnp.exp(-emb[:, 0:1] * dl_ref[...])
    row = r * rb + lax.broadcasted_iota(jnp.int32, (rb, 1), 0)
    k = jnp.where(row == seq, 0.0, k)
    full_out[...] = k

    @pl.when(r == 0)
    def _():
        asum_out[...] = jnp.zeros(asum_out.shape, F32)

    asum_out[...] += jnp.sum(jnp.abs(k), axis=0, keepdims=True)


def _filter_call(emb, w1, b1, w2, b2, w3sel, freq, deltas2, seq, rb):
    n2 = emb.shape[0]
    half_blocks = seq // rb
    width = w3sel.shape[2]
    return pl.pallas_call(
        functools.partial(_filter_kernel, seq=seq),
        grid=(n2 // rb,),
        in_specs=[pl.BlockSpec((rb, HY_EMB_PAD), lambda r: (r, 0)), _full(w1.shape), _full(b1.shape),
                  _full(w2.shape), _full(b2.shape),
                  pl.BlockSpec((1,) + w3sel.shape[1:], lambda r: (r // half_blocks, 0, 0)),
                  _full(freq.shape), _full(deltas2.shape)],
        out_specs=[pl.BlockSpec((rb, width), lambda r: (r, 0)), pl.BlockSpec((1, width), lambda r: (0, 0))],
        out_shape=[jax.ShapeDtypeStruct((n2, width), F32), jax.ShapeDtypeStruct((1, width), F32)],
        compiler_params=_params("arbitrary"),
        name="filt",
    )(emb, w1, b1, w2, b2, w3sel, freq, deltas2)


def _fa_kernel(u_ref, f_ref, a_out):
    two, _, hn, g, c = u_ref.shape
    a = _dot(f_ref[...], u_ref[...].reshape(two * hn * g, c))
    a_out[...] = a.reshape(a_out.shape)


def _fa_call(u5, fmat):
    _, p, hn, n, c = u5.shape
    g = SUBLANES
    return pl.pallas_call(
        _fa_kernel,
        grid=(p, n // g),
        in_specs=[pl.BlockSpec((2, 1, hn, g, c), lambda q, j: (0, q, 0, j, 0)), _full(fmat.shape)],
        out_specs=pl.BlockSpec((1, 2, n, g, c), lambda q, j: (q, 0, 0, j, 0)),
        out_shape=jax.ShapeDtypeStruct((p, 2, n, n, c), F32),
        compiler_params=_params("arbitrary", "arbitrary"),
        name="fa",
    )(u5, fmat)


def _fb_kernel(a_ref, g_ref, asum_ref, kf_out):
    two, _, n, c = a_ref.shape[1:]
    a = a_ref[0].reshape(two * n, c)
    x = _dot(g_ref[0], a) / (asum_ref[...] + 1e-6)
    kf_out[0] = x.reshape(two, n, c)


def _fb_call(a5, gmat, asum):
    _, _, n, _, c = a5.shape
    return pl.pallas_call(
        _fb_kernel,
        grid=(n,),
        in_specs=[pl.BlockSpec((1, 2, 1, n, c), lambda k: (0, 0, k, 0, 0)),
                  pl.BlockSpec((1, 2 * n, 2 * n), lambda k: (k, 0, 0)), _full(asum.shape)],
        out_specs=pl.BlockSpec((1, 2, n, c), lambda k: (k, 0, 0, 0)),
        out_shape=jax.ShapeDtypeStruct((n, 2, n, c), F32),
        compiler_params=_params("arbitrary"),
        name="fb",
    )(a5, gmat, asum)


def _mid_kernel(a_ref, g_ref, h_ref, kf_ref, b_out):
    _, two, kb, n, c = a_ref.shape
    for kk in range(kb):
        x = _dot(g_ref[kk], a_ref[0, :, kk].reshape(two * n, c))
        xr, xi = x[:n], x[n:]
        kr, ki = kf_ref[kk, 0], kf_ref[kk, 1]
        y = jnp.concatenate([xr * kr - xi * ki, xr * ki + xi * kr], axis=0)
        b_out[0, :, kk] = _dot(h_ref[kk], y).reshape(two, n, c)


def _mid_call(a5, gmat, hmat, kf, order, kb):
    p, _, n, _, c = a5.shape
    return pl.pallas_call(
        _mid_kernel,
        grid=(n // kb, p),
        in_specs=[pl.BlockSpec((1, 2, kb, n, c), lambda k, q: (q, 0, k, 0, 0)),
                  pl.BlockSpec((kb, 2 * n, 2 * n), lambda k, q: (k, 0, 0)),
                  pl.BlockSpec((kb, 2 * n, 2 * n), lambda k, q: (k, 0, 0)),
                  pl.BlockSpec((kb, 2, n, c), lambda k, q: (k, 0, 0, order))],
        out_specs=pl.BlockSpec((1, 2, kb, n, c), lambda k, q: (q, 0, k, 0, 0)),
        out_shape=jax.ShapeDtypeStruct(a5.shape, F32),
        compiler_params=_params("arbitrary", "arbitrary"),
        name="mid",
    )(a5, gmat, hmat, kf)


def _fc_kernel(b_ref, f_ref, u_ref, m_ref, skip_ref, o_out):
    _, two, n, g, c = b_ref.shape
    y = _dot(f_ref[...], b_ref[...].reshape(two * n * g, c)).reshape(u_ref.shape)
    o_out[...] = m_ref[...] * (y + u_ref[...] * skip_ref[...])


def _fc_call(b5, finv, u5, m5, skip_row):
    _, p, hn, n, c = u5.shape
    g = SUBLANES
    blk = pl.BlockSpec((2, 1, hn, g, c), lambda q, j: (0, q, 0, j, 0))
    return pl.pallas_call(
        _fc_kernel,
        grid=(p, n // g),
        in_specs=[pl.BlockSpec((1, 2, n, g, c), lambda q, j: (q, 0, 0, j, 0)), _full(finv.shape), blk, blk,
                  _full(skip_row.shape)],
        out_specs=blk,
        out_shape=jax.ShapeDtypeStruct(u5.shape, F32),
        compiler_params=_params("arbitrary", "arbitrary"),
        name="fc",
    )(b5, finv, u5, m5, skip_row)


def _dft_tables(n):
    hn = n // 2
    k = np.arange(n)[:, None]
    ang = -2.0 * np.pi * (k * np.arange(n)[None, :] % n) / n
    fr, fi = np.cos(ang), np.sin(ang)
    f_data = np.block([[fr[:, :hn], -fi[:, :hn]], [fi[:, :hn], fr[:, :hn]]])
    f_filt = np.concatenate([fr, fi], axis=0)
    er, ei = fr[:hn], -fi[:hn]
    f_inv = np.block([[er, -ei], [ei, er]]) / float(n * n)
    k1 = jnp.arange(n, dtype=jnp.int32)[:, None, None]
    k2 = jnp.arange(n, dtype=jnp.int32)[None, :, None]
    m2 = jnp.arange(n, dtype=jnp.int32)[None, None, :]
    idx = (m2 * (k1 + n * k2)) % (n * n)
    ang2 = idx.astype(F32) * (-2.0 * math.pi / (n * n))
    gr, gi = jnp.cos(ang2), jnp.sin(ang2)
    g = jnp.concatenate([jnp.concatenate([gr, -gi], axis=2), jnp.concatenate([gi, gr], axis=2)], axis=1)
    h = jnp.swapaxes(g, 1, 2)

    def widen(f):
        return jnp.asarray(np.kron(f, np.eye(SUBLANES)), BF16)

    return widen(f_data), widen(f_filt), widen(f_inv), g.astype(BF16), h.astype(BF16)


def _hyena_filter_tables(seq):
    t = jnp.linspace(0.0, 1.0, seq, dtype=F32)[:, None]
    w = 2.0 * math.pi * jnp.arange(seq, dtype=F32)[:, None] / seq
    f = jnp.linspace(1e-4, HY_BANDS - 1, HY_BANDS, dtype=F32)[None, :]
    emb = jnp.concatenate([t, jnp.cos(f * w), -jnp.sin(f * w)], axis=-1)
    pos = jnp.concatenate([jnp.arange(seq), jnp.array([0]), jnp.arange(seq - 1, 0, -1)])
    emb = jnp.pad(emb[pos], ((0, 0), (0, HY_EMB_PAD - HY_EMB)))
    deltas = jnp.abs(jnp.linspace(math.log(HY_DECAY_TARGET) / HY_SLOW_DECAY,
                                  math.log(HY_DECAY_TARGET) / HY_FAST_DECAY, HY_WIDTH, dtype=F32))
    return emb, jnp.tile(deltas, HY_ORDER)[None, :]


def _hyena(hv, hx1, hx2, w1, b1, w2, b2, w3, freq, skip, kb):
    bsz, seq, c = hv.shape
    n = int(round(math.sqrt(2 * seq)))
    assert n * n == 2 * seq and bsz % 2 == 0
    hn, p = n // 2, bsz // 2
    f_data, f_filt, f_inv, gmat, hmat = _dft_tables(n)

    emb, deltas2 = _hyena_filter_tables(seq)
    w1p = jnp.pad(w1, ((0, HY_EMB_PAD - HY_EMB), (0, 0)))
    w3r = w3.reshape(w3.shape[0], HY_ORDER, 2, c)
    w3sel = jnp.stack([w3r[:, :, 0, :].reshape(-1, HY_ORDER * c), w3r[:, :, 1, :].reshape(-1, HY_ORDER * c)])
    full, asum = _filter_call(emb, w1p, b1[None], w2, b2[None], w3sel, freq[None], deltas2, seq, min(512, seq))
    c2 = HY_ORDER * c
    kf = _fb_call(_fa_call(full.reshape(2, 1, hn, n, c2), f_filt), gmat, asum)

    def view(t):
        return t.reshape(2, p, hn, n, c)

    def long_conv(u5, m5, order):
        bm = _mid_call(_fa_call(u5, f_data), gmat, hmat, kf, order, kb)
        return _fc_call(bm, f_inv, u5, m5, skip[order][None, :])

    z = long_conv(view(hv), view(hx1), 0)
    return long_conv(z, view(hx2), 1).reshape(bsz, seq, c)


def _merge_kernel(x_ref, at_ref, hy_ref, g_ref, mod_ref, wba_ref, wbh_ref, wo_ref, o_ref):
    d = x_ref.shape[2]
    g = g_ref[0].astype(F32)
    y = (jax.nn.sigmoid(g[:, :d]) * _dot(at_ref[0], wba_ref[...])
         + jax.nn.sigmoid(g[:, d:]) * _dot(hy_ref[0], wbh_ref[...]))
    o_ref[0] = x_ref[0] + mod_ref[0, 2:3, :] * _dot(y, wo_ref[...])


def _merge_call(x, attn, hy, gate, mod, wba, wbh, wo, tm):
    bsz, s, d = x.shape

    def tok(w):
        return pl.BlockSpec((1, tm, w), lambda b, i: (b, i, 0))

    return pl.pallas_call(
        _merge_kernel,
        grid=(bsz, s // tm),
        in_specs=[tok(d), tok(attn.shape[2]), tok(hy.shape[2]), tok(2 * d),
                  pl.BlockSpec((1, SUBLANES, d), lambda b, i: (b, 0, 0)),
                  _full(wba.shape), _full(wbh.shape), _full(wo.shape)],
        out_specs=tok(d),
        out_shape=jax.ShapeDtypeStruct((bsz, s, d), F32),
        compiler_params=_params("arbitrary", "arbitrary"),
        name="merge",
    )(x, attn, hy, gate, mod, wba, wbh, wo)


def _route_kernel(xm_ref, mod_ref, nf_ref, wrt_ref, rb_ref, tri_ref, lt_ref, h2_out, w_out, p_out, col_out, row_out):
    tm = xm_ref.shape[1]
    ng, gs = N_GROUPS, GROUP_SIZE

    h2 = _prenorm(xm_ref[0], mod_ref, 3, nf_ref[...])
    h2_out[0] = h2.astype(h2_out.dtype)
    scores = jax.nn.sigmoid(_dot3(wrt_ref[...], h2, NT_DIMS))
    sel = scores + rb_ref[...]
    slabs = [sel[ng * j:ng * (j + 1)] for j in range(gs)]

    top1 = jnp.full((ng, tm), -jnp.inf, F32)
    top2 = top1
    for x in slabs:
        top2 = jnp.maximum(top2, jnp.minimum(top1, x))
        top1 = jnp.maximum(top1, x)
    gscore = top1 + top2
    gid = lax.broadcasted_iota(jnp.int32, (ng, 1), 0)
    rank = jnp.zeros((ng, tm), jnp.int32)
    for g2 in range(ng):
        row = gscore[g2:g2 + 1]
        beats = (row > gscore) | ((row == gscore) & (g2 < gid))
        rank = rank + beats.astype(jnp.int32)
    gmask = rank < TOPK_GROUPS

    cand = [jnp.where(gmask, x, -jnp.inf) for x in slabs]
    eid = [gid * gs + j for j in range(gs)]
    chosen = []
    for _ in range(TOP_K):
        best = functools.reduce(jnp.maximum, cand)
        best = jnp.max(best, axis=0, keepdims=True)
        idx = functools.reduce(jnp.minimum, [jnp.where(cand[j] == best, eid[j], N_EXPERTS) for j in range(gs)])
        idx = jnp.min(idx, axis=0, keepdims=True)
        chosen.append(idx)
        cand = [jnp.where(eid[j] == idx, -jnp.inf, cand[j]) for j in range(gs)]

    mask = [functools.reduce(jnp.logical_or, [eid[j] == idx for idx in chosen]) for j in range(gs)]
    maskb = jnp.concatenate(mask, axis=0)
    maskf = jnp.where(maskb, 1.0, 0.0)
    mask16 = maskf.astype(BF16)
    wsel = jnp.where(maskb, scores, 0.0)
    w_out[...] = wsel / jnp.sum(wsel, axis=0, keepdims=True) * ROUTE_SCALE
    before = jnp.dot(mask16, tri_ref[...], preferred_element_type=F32)
    p_out[...] = jnp.where(maskb, before, -1.0).astype(p_out.dtype)

    def extents(cnt, lower_sum):
        units = jnp.floor((cnt + (RUN_ALIGN - 1)) * (1.0 / RUN_ALIGN))
        start = RUN_ALIGN * lower_sum(units.astype(BF16))
        return start, start + RUN_ALIGN * units

    cnt_c = jnp.sum(maskf, axis=1, keepdims=True)
    start_c, end_c = extents(jnp.broadcast_to(cnt_c, (N_EXPERTS, LANES)),
                             lambda u: jnp.dot(lt_ref[...], u, preferred_element_type=F32))
    lane = lax.broadcasted_iota(jnp.int32, (N_EXPERTS, LANES), 1)
    col_out[0] = jnp.where(lane == 0, cnt_c, jnp.where(lane == 1, start_c, end_c))
    cnt_r = lax.dot_general(jnp.ones((SUBLANES, tm), BF16), mask16, NT_DIMS, preferred_element_type=F32)
    start_r, end_r = extents(cnt_r, lambda u: lax.dot_general(u, lt_ref[...], NT_DIMS, preferred_element_type=F32))
    sub = lax.broadcasted_iota(jnp.int32, (SUBLANES, N_EXPERTS), 0)
    row_out[0] = jnp.where(sub == 0, start_r, end_r)


def _route_call(xm, mod, norm_ffn, wrt, rbias, lower, tm):
    bsz, s, d = xm.shape
    t = bsz * s
    nt = s // tm
    tri = (jnp.arange(tm)[:, None] < jnp.arange(tm)[None, :]).astype(BF16)
    tok = pl.BlockSpec((N_EXPERTS, tm), lambda i: (0, i))
    return pl.pallas_call(
        _route_kernel,
        grid=(t // tm,),
        in_specs=[pl.BlockSpec((1, tm, d), lambda i: (i // nt, i % nt, 0)),
                  pl.BlockSpec((1, SUBLANES, d), lambda i: (i // nt, 0, 0)),
                  _full(norm_ffn.shape), _full(wrt.shape), _full(rbias.shape), _full(tri.shape),
                  _full(lower.shape)],
        out_specs=[pl.BlockSpec((1, tm, d), lambda i: (i // nt, i % nt, 0)), tok, tok,
                   pl.BlockSpec((1, N_EXPERTS, LANES), lambda i: (i, 0, 0)),
                   pl.BlockSpec((1, SUBLANES, N_EXPERTS), lambda i: (i, 0, 0))],
        out_shape=[jax.ShapeDtypeStruct((bsz, s, d), BF16), jax.ShapeDtypeStruct((N_EXPERTS, t), F32),
                   jax.ShapeDtypeStruct((N_EXPERTS, t), BF16),
                   jax.ShapeDtypeStruct((t // tm, N_EXPERTS, LANES), F32),
                   jax.ShapeDtypeStruct((t // tm, SUBLANES, N_EXPERTS), F32)],
        compiler_params=_params("arbitrary"),
        name="route",
    )(xm, mod, norm_ffn, wrt, rbias, tri, lower)


def _pack(x):
    w = x.shape[1] // 2
    lo = lax.bitcast_convert_type(x[:, :w].astype(BF16).astype(F32), jnp.uint32)
    hi = lax.bitcast_convert_type(x[:, w:].astype(BF16).astype(F32), jnp.uint32)
    return hi | (lo >> 16)


def _unpack(u):
    lo = lax.bitcast_convert_type(u << 16, F32).astype(BF16)
    hi = lax.bitcast_convert_type(u & jnp.uint32(0xFFFF0000), F32).astype(BF16)
    return lo, hi


def _pow2_pieces(units, limit):
    bit = 1
    while bit * 2 <= limit:
        bit *= 2
    while bit:
        yield (units & bit) != 0, units & ~(2 * bit - 1), bit
        bit //= 2


def _rows_copy(vm_ref, hbm_ref, sem, vm_row, hbm_row, rows, to_hbm):
    v = vm_ref.at[pl.ds(pl.multiple_of(vm_row, RUN_ALIGN), rows), :]
    h = hbm_ref.at[pl.ds(pl.multiple_of(hbm_row, RUN_ALIGN), rows), :]
    return pltpu.make_async_copy(v, h, sem) if to_hbm else pltpu.make_async_copy(h, v, sem)


def _run_copies(vm_ref, hbm_ref, sem, n8, vm_row, hbm_row, limit, to_hbm, act):
    def emit(pieces):
        for on, off, size in pieces:
            @pl.when(on)
            def _():
                act(_rows_copy(vm_ref, hbm_ref, sem, vm_row + RUN_ALIGN * off, hbm_row + RUN_ALIGN * off,
                               RUN_ALIGN * size, to_hbm))

    pieces = list(_pow2_pieces(n8, limit))
    long_pieces = [p for p in pieces if p[2] >= LONG_RUN]
    if long_pieces:
        pl.when(n8 >= LONG_RUN)(lambda: emit(long_pieces))
    emit([p for p in pieces if p[2] < LONG_RUN])


def _wait_rows(vm_ref, hbm_ref, sem, units, limit, to_hbm):
    for on, _, size in _pow2_pieces(units, limit):
        @pl.when(on)
        def _():
            _rows_copy(vm_ref, hbm_ref, sem, 0, 0, RUN_ALIGN * size, to_hbm).wait()


def _dispatch_kernel(n8_ref, ls_ref, gs_ref, ts_ref, t8_ref, nu_ref, pos_ref, ext_ref, h_ref, xs_out, srt2, zbuf,
                     sems):
    step = pl.program_id(0)
    tm = h_ref.shape[0]
    rows = srt2.shape[1]
    slot = step % 2
    srt, sem = srt2.at[slot], sems.at[slot]
    rid = lax.broadcasted_iota(jnp.int32, (rows, 1), 0).astype(F32)
    start = ext_ref[0, 0:1, :]
    member = jnp.where((rid >= start) & (rid < ext_ref[0, 1:2, :]), 1.0, 0.0)
    offset = rid - jnp.sum(member * start, axis=1, keepdims=True)
    pos = jnp.dot(member.astype(BF16), pos_ref[...], preferred_element_type=F32)
    sel = jnp.where(pos == offset, 1.0, 0.0).astype(BF16)
    srt[...] = _pack(jnp.dot(sel, h_ref[...], preferred_element_type=F32))

    def send(e, c):
        i = step * N_EXPERTS + e
        _run_copies(srt, xs_out, sem, n8_ref[i], ls_ref[i], gs_ref[i], tm // RUN_ALIGN, True, lambda cp: cp.start())
        return c

    lax.fori_loop(0, N_EXPERTS, send, 0)

    def wait_tile(tile, s):
        last = tile * N_EXPERTS + N_EXPERTS - 1
        _wait_rows(srt2.at[s], xs_out, sems.at[s], ls_ref[last] // RUN_ALIGN + n8_ref[last], rows // RUN_ALIGN, True)

    pl.when(step > 0)(lambda: wait_tile(step - 1, 1 - slot))

    @pl.when(step == pl.num_programs(0) - 1)
    def _():
        wait_tile(step, slot)
        zbuf[...] = jnp.zeros(zbuf.shape, zbuf.dtype)
        nblk = xs_out.shape[0] // EXPERT_BLOCK

        def fill(act):
            def tails(e, c):
                _run_copies(zbuf, xs_out, sem, t8_ref[e], 0, ts_ref[e], EXPERT_BLOCK // RUN_ALIGN - 1, True, act)
                return c

            def blocks(b, c):
                act(pltpu.make_async_copy(
                    zbuf, xs_out.at[pl.ds(pl.multiple_of(b * EXPERT_BLOCK, EXPERT_BLOCK), EXPERT_BLOCK), :], sem))
                return c

            lax.fori_loop(0, N_EXPERTS, tails, 0)
            lax.fori_loop(nu_ref[0], nblk, blocks, 0)

        fill(lambda cp: cp.start())
        fill(lambda cp: cp.wait())


def _dispatch_call(tables, pos_et, ext_rows, h2, nblk, tm):
    t, d = h2.shape
    lrows = TOP_K * tm + N_EXPERTS * RUN_ALIGN
    return pl.pallas_call(
        _dispatch_kernel,
        grid_spec=pltpu.PrefetchScalarGridSpec(
            num_scalar_prefetch=len(tables), grid=(t // tm,),
            in_specs=[pl.BlockSpec((N_EXPERTS, tm), lambda i, *_: (0, i)),
                      pl.BlockSpec((1,) + ext_rows.shape[1:], lambda i, *_: (i, 0, 0)),
                      pl.BlockSpec((tm, d), lambda i, *_: (i, 0))],
            out_specs=pl.BlockSpec(memory_space=pl.ANY),
            scratch_shapes=[pltpu.VMEM((2, lrows, d // 2), jnp.uint32),
                            pltpu.VMEM((EXPERT_BLOCK, d // 2), jnp.uint32), pltpu.SemaphoreType.DMA((2,))]),
        out_shape=jax.ShapeDtypeStruct((nblk * EXPERT_BLOCK, d // 2), jnp.uint32),
        compiler_params=_params("arbitrary"),
        name="dispatch",
    )(*tables, pos_et, ext_rows, h2)


def _expert_kernel(blk_ref, nused_ref, x_ref, wgu_ref, wd_ref, y_ref):
    used = pl.program_id(0) < nused_ref[0]

    @pl.when(used)
    def _():
        lo, hi = _unpack(x_ref[...])
        half = lo.shape[1]
        gu = (jnp.dot(lo, wgu_ref[0, :half, :], preferred_element_type=F32)
              + jnp.dot(hi, wgu_ref[0, half:, :], preferred_element_type=F32))
        a = _silu(gu[:, :EXPERT_FF]) * gu[:, EXPERT_FF:]
        y_ref[...] = _pack(_dot(a, wd_ref[0]))

    @pl.when(jnp.logical_not(used))
    def _():
        y_ref[...] = jnp.zeros(y_ref.shape, y_ref.dtype)


def _expert_call(blk_e, nused, xs, wgu, wd):
    rows, d = xs.shape
    nblk = rows // EXPERT_BLOCK

    def row_map(i, blk, nu):
        return (jnp.minimum(i, nu[0] - 1), 0)

    return pl.pallas_call(
        _expert_kernel,
        grid_spec=pltpu.PrefetchScalarGridSpec(
            num_scalar_prefetch=2, grid=(nblk,),
            in_specs=[pl.BlockSpec((EXPERT_BLOCK, d), row_map),
                      pl.BlockSpec((1,) + wgu.shape[1:], lambda i, blk, nu: (blk[i], 0, 0)),
                      pl.BlockSpec((1,) + wd.shape[1:], lambda i, blk, nu: (blk[i], 0, 0))],
            out_specs=pl.BlockSpec((EXPERT_BLOCK, d), lambda i, blk, nu: (i, 0))),
        out_shape=jax.ShapeDtypeStruct((rows, d), jnp.uint32),
        compiler_params=_params("arbitrary"),
        name="expert",
    )(blk_e, nused, xs, wgu, wd)


def _combine_kernel(n8_ref, ls_ref, gs_ref, ys_hbm, pos_ref, w_ref, ext_ref, xm_ref, h_ref, mod_ref, wsgu_ref,
                    wsd_ref, fn_ref, o_ref, ybuf2, sems):
    step = pl.program_id(0)
    tm = xm_ref.shape[0]
    rows = ybuf2.shape[1]
    slot = step % 2
    ybuf, sem = ybuf2.at[slot], sems.at[slot]

    def fetch(tile, s):
        def body(e, c):
            i = tile * N_EXPERTS + e
            _run_copies(ybuf2.at[s], ys_hbm, sems.at[s], n8_ref[i], ls_ref[i], gs_ref[i], tm // RUN_ALIGN, False,
                        lambda cp: cp.start())
            return c
        lax.fori_loop(0, N_EXPERTS, body, 0)

    pl.when(step == 0)(lambda: fetch(step, slot))
    pl.when(step + 1 < pl.num_programs(0))(lambda: fetch(step + 1, 1 - slot))
    gu = _dot(h_ref[...], wsgu_ref[...])
    ff = gu.shape[1] // 2
    shared = _dot(_silu(gu[:, :ff]) * gu[:, ff:], wsd_ref[...])
    cid = lax.broadcasted_iota(jnp.int32, (1, rows), 1).astype(F32)
    start = ext_ref[0, :, 1:2]
    member = jnp.where((cid >= start) & (cid < ext_ref[0, :, 2:3]), 1.0, 0.0)
    offset = cid - jnp.sum(member * start, axis=0, keepdims=True)
    member = member.astype(BF16)
    pos = jnp.dot(pos_ref[...], member, preferred_element_type=F32)
    mix = jnp.where(pos == offset, jnp.dot(w_ref[...].astype(BF16), member, preferred_element_type=F32), 0.0)
    mix = mix.astype(BF16)
    last = step * N_EXPERTS + N_EXPERTS - 1
    filled = ls_ref[last] + RUN_ALIGN * n8_ref[last]
    _wait_rows(ybuf, ys_hbm, sem, filled // RUN_ALIGN, rows // RUN_ALIGN, False)
    rid = lax.broadcasted_iota(jnp.int32, (rows, 1), 0)
    lo, hi = _unpack(jnp.where(rid < filled, ybuf[...], jnp.uint32(0)))
    routed = jnp.concatenate([jnp.dot(mix, lo, preferred_element_type=F32),
                              jnp.dot(mix, hi, preferred_element_type=F32)], axis=1)
    x = xm_ref[...] + mod_ref[0, 5:6, :] * (routed + shared)
    o_ref[...] = _rms(x, fn_ref[...])


def _combine_call(tables, ys, pos_te, w_te, ext_cols, xm, h2, mod, wsgu, wsd, final_norm, tm, tiles_per_batch):
    t, d = xm.shape
    lrows = TOP_K * tm + N_EXPERTS * RUN_ALIGN
    tok = pl.BlockSpec((tm, d), lambda i, *_: (i, 0))
    per_e = pl.BlockSpec((tm, N_EXPERTS), lambda i, *_: (i, 0))
    return pl.pallas_call(
        _combine_kernel,
        grid_spec=pltpu.PrefetchScalarGridSpec(
            num_scalar_prefetch=len(tables), grid=(t // tm,),
            in_specs=[pl.BlockSpec(memory_space=pl.ANY), per_e, per_e,
                      pl.BlockSpec((1,) + ext_cols.shape[1:], lambda i, *_: (i, 0, 0)), tok, tok,
                      pl.BlockSpec((1, SUBLANES, d), lambda i, *_: (i // tiles_per_batch, 0, 0)),
                      _full(wsgu.shape), _full(wsd.shape), _full(final_norm.shape)],
            out_specs=tok,
            scratch_shapes=[pltpu.VMEM((2, lrows, d // 2), jnp.uint32), pltpu.SemaphoreType.DMA((2,))]),
        out_shape=jax.ShapeDtypeStruct((t, d), F32),
        compiler_params=_params("arbitrary"),
        name="combine",
    )(*tables, ys, pos_te, w_te, ext_cols, xm, h2, mod, wsgu, wsd, final_norm)


def _moe(xm, mod, norm_ffn, w_router, router_bias, wg, wu, wd, wsg, wsu, wsd, final_norm, tm):
    bsz, s, d = xm.shape
    t = bsz * s
    nt = t // tm
    perm = (np.arange(N_EXPERTS) % N_GROUPS) * GROUP_SIZE + np.arange(N_EXPERTS) // N_GROUPS
    wrt = w_router.T[perm]
    rbias = router_bias[perm][:, None]
    lower = jnp.asarray(perm[None, :] < perm[:, None], BF16)
    h2, w_et, pos_et, ext_cols, ext_rows = _route_call(xm, mod, norm_ffn, wrt, rbias, lower, tm)

    inv = np.argsort(perm)
    n8 = (ext_cols[:, :, 0].astype(jnp.int32)[:, inv] + (RUN_ALIGN - 1)) // RUN_ALIGN
    run = RUN_ALIGN * n8
    ls = jnp.cumsum(run, axis=1) - run
    tot = jnp.sum(run, axis=0)
    padded = (tot + EXPERT_BLOCK - 1) // EXPERT_BLOCK * EXPERT_BLOCK
    pad_end = jnp.cumsum(padded)
    gs = (pad_end - padded)[None, :] + jnp.cumsum(run, axis=0) - run
    nblk = -(-(t * TOP_K + nt * N_EXPERTS * (RUN_ALIGN - 1)) // EXPERT_BLOCK) + N_EXPERTS
    blk_first = jnp.arange(nblk, dtype=jnp.int32)[:, None] * EXPERT_BLOCK
    blk_e = jnp.minimum(jnp.sum((pad_end[None, :] <= blk_first).astype(jnp.int32), axis=1), N_EXPERTS - 1)
    nused = (pad_end[-1:] // EXPERT_BLOCK).astype(jnp.int32)
    tables = [a.reshape(-1).astype(jnp.int32) for a in (n8, ls, gs)]
    tails = [(pad_end - padded + tot).astype(jnp.int32), ((padded - tot) // RUN_ALIGN).astype(jnp.int32), nused]

    h2f = h2.reshape(t, d)
    xs = _dispatch_call(tables + tails, pos_et, ext_rows, h2f, nblk, tm)
    wgu = jnp.concatenate([wg, wu], axis=2).astype(BF16)
    ys = _expert_call(blk_e, nused, xs, wgu, wd.astype(BF16))
    wsgu = jnp.concatenate([wsg, wsu], axis=1).astype(BF16)
    out = _combine_call(tables, ys, pos_et.T, w_et.T, ext_cols, xm.reshape(t, d), h2f, mod, wsgu, wsd.astype(BF16),
                        final_norm, tm, s // tm)
    return out.reshape(bsz, s, d)


def _rope_tables(s):
    rows = s // GRID_W
    row = jnp.broadcast_to(jnp.arange(rows, dtype=F32)[:, None], (rows, GRID_W)).reshape(-1)
    col = jnp.broadcast_to(jnp.arange(GRID_W, dtype=F32)[None, :], (rows, GRID_W)).reshape(-1)
    half = QK_ROPE // 2
    inv_freq = ROPE_THETA ** (-jnp.arange(0, half, 2, dtype=F32) / half)
    ar, ac = row[:, None] * inv_freq, col[:, None] * inv_freq
    ones = jnp.ones((s, QK_NOPE), F32)
    tail = HEAD_PAD - QK_NOPE - QK_ROPE
    cos_t = jnp.concatenate([ones, jnp.cos(ar), jnp.cos(ar), jnp.cos(ac), jnp.cos(ac), jnp.ones((s, tail), F32)], 1)
    sin_t = jnp.concatenate([0 * ones, -jnp.sin(ar), jnp.sin(ar), -jnp.sin(ac), jnp.sin(ac),
                             jnp.zeros((s, tail), F32)], 1)
    return cos_t, sin_t


_Q4 = QK_ROPE // 4
ROPE_SWAP = np.concatenate([np.arange(_Q4, 2 * _Q4), np.arange(0, _Q4), np.arange(3 * _Q4, 4 * _Q4),
                            np.arange(2 * _Q4, 3 * _Q4)])


def _rope_slot(w, swap):
    if swap:
        w = w[..., ROPE_SWAP]
    pad = [(0, 0)] * (w.ndim - 1) + [(QK_NOPE, HEAD_PAD - QK_NOPE - QK_ROPE)]
    return jnp.pad(w, pad)


def kernel(x, c, ctx, c_ctx, w_mod, b_mod, norm_mix, norm_ffn, w_in, b_in, q_norm, w_uq, kv_norm, w_ukv, w_branch_attn, hy_conv_w, hy_conv_b, hy_filt_w1, hy_filt_b1, hy_filt_w2, hy_filt_b2, hy_filt_w3, hy_filt_freq, hy_skip, w_branch_hyena, w_out, w_router, router_bias, w_exp_gate, w_exp_up, w_exp_down, w_sh_gate, w_sh_up, w_sh_down, final_norm,
           tiles=None):
    bsz, s, d = x.shape
    tl = dict(inproj=256, tq=1024, tk=1408, fft_kb=4, merge=512, moe=256)
    tl.update(tiles or {})
    assert w_mod.shape[0] == 1, "single-layer trunk"
    i = 0

    rows = -(-(bsz + 1) // SUBLANES) * SUBLANES
    c_rows = jnp.pad(jnp.concatenate([c, c_ctx[None]], axis=0), ((0, rows - bsz - 1), (0, 0)))
    mod_all = _mod_call(c_rows, w_mod[i], b_mod[i])
    mod_all = jnp.pad(mod_all.reshape(rows, 6, d), ((0, 0), (0, SUBLANES - 6), (0, 0)))
    mod, modc = mod_all[:bsz], mod_all[bsz:bsz + 1]

    cuts = np.cumsum([Q_LORA, KV_LORA, QK_ROPE, 3 * HY_WIDTH])
    wi, bi = w_in[i], b_in[i][None]
    w_q, w_kv, w_pe, w_hy, w_g = jnp.split(wi, cuts, axis=1)
    b_q, b_kv, b_pe, b_hy, b_g = jnp.split(bi, cuts, axis=1)
    wa = jnp.concatenate([w_q, w_kv, _rope_slot(w_pe, False), _rope_slot(w_pe, True)], axis=1).astype(BF16)
    ba = jnp.concatenate([b_q, b_kv, _rope_slot(b_pe, False), _rope_slot(b_pe, True)], axis=1)
    wq3 = w_uq[i].reshape(Q_LORA, N_HEADS, QK_NOPE + QK_ROPE) * (ATTN_SCALE * math.log2(math.e))
    tail = ((0, 0), (0, 0), (0, HEAD_PAD - QK_NOPE))
    wuq = (jnp.pad(wq3[..., :QK_NOPE], tail) + _rope_slot(wq3[..., QK_NOPE:], False)).reshape(Q_LORA, -1).astype(BF16)
    wuqs = _rope_slot(wq3[..., QK_NOPE:], True).reshape(Q_LORA, -1).astype(BF16)
    wkv3 = w_ukv[i].reshape(KV_LORA, N_HEADS, QK_NOPE + V_HEAD)
    wuk = jnp.pad(wkv3[..., :QK_NOPE], tail).reshape(KV_LORA, -1).astype(BF16)
    wuvt = wkv3[..., QK_NOPE:].reshape(KV_LORA, -1).T.astype(BF16)
    nm, qn, kvn = norm_mix[i][None], q_norm[i][None], kv_norm[i][None]

    w_c = jnp.concatenate([w_kv, _rope_slot(w_pe, False)], axis=1).astype(BF16)
    b_c = jnp.concatenate([b_kv, _rope_slot(b_pe, False)], axis=1)
    ck, cvt = _ctx_call(ctx, modc, nm, w_c, b_c, kvn, wuk, wuvt)

    cos_t, sin_t = _rope_tables(s)
    q, k, vt, hv, hx1, hx2, gate = _inproj_call(
        x, mod, nm, wa, ba, w_hy.astype(BF16), b_hy, w_g.astype(BF16), b_g, qn, wuq, wuqs, kvn, wuk, wuvt,
        cos_t, sin_t, hy_conv_w[i], hy_conv_b[i][None], tl["inproj"])

    attn = _attn_call(q, jnp.concatenate([ck, k], axis=2), jnp.concatenate([cvt, vt], axis=3), tl["tq"], tl["tk"])
    hy = _hyena(hv, hx1, hx2, hy_filt_w1[i], hy_filt_b1[i], hy_filt_w2[i], hy_filt_b2[i], hy_filt_w3[i],
                hy_filt_freq[i], hy_skip[i], tl["fft_kb"])
    xm = _merge_call(x, attn, hy, gate, mod, w_branch_attn[i].astype(BF16), w_branch_hyena[i].astype(BF16),
                     w_out[i].astype(BF16), tl["merge"])
    return _moe(xm, mod, norm_ffn[i][None], w_router[i], router_bias[i], w_exp_gate[i], w_exp_up[i], w_exp_down[i],
                w_sh_gate[i], w_sh_up[i], w_sh_down[i], final_norm[None], tl["moe"])
```

```python
import functools
import math

import numpy as np
import jax
import jax.numpy as jnp
from jax import lax
from jax.experimental import pallas as pl
from jax.experimental.pallas import tpu as pltpu

GRID_W = 64
N_HEADS = 8
QK_NOPE = 64
QK_ROPE = 32
V_HEAD = 64
Q_LORA = 256
KV_LORA = 128
ROPE_THETA = 10000.0
ATTN_SCALE = 1.0 / math.sqrt(QK_NOPE + QK_ROPE)
HY_WIDTH = 512
HY_ORDER = 2
HY_SHORT = 3
HY_BANDS = 8
HY_EMB = 1 + 2 * HY_BANDS
HY_EMB_PAD = 32
HY_FAST_DECAY = 0.3
HY_SLOW_DECAY = 1.5
HY_DECAY_TARGET = 1e-2
N_EXPERTS = 64
N_GROUPS = 8
GROUP_SIZE = N_EXPERTS // N_GROUPS
TOPK_GROUPS = 4
TOP_K = 8
EXPERT_FF = 256
ROUTE_SCALE = 2.5
EXPERT_BLOCK = 1024
RUN_ALIGN = 8
LONG_RUN = 8
NORM_EPS = 1e-6

HEAD_PAD = 128
Q_CHUNK = 512
AHEAD = 2
LANES = 128
SUBLANES = 8
VMEM_LIMIT = 48 * 1024 * 1024

F32 = jnp.float32
BF16 = jnp.bfloat16
NT_DIMS = (((1,), (1,)), ((), ()))
NN_DIMS = (((1,), (0,)), ((), ()))


def _params(*sem):
    return pltpu.CompilerParams(dimension_semantics=sem, vmem_limit_bytes=VMEM_LIMIT)


def _dot(a, b):
    return jnp.dot(a.astype(BF16), b.astype(BF16), preferred_element_type=F32)


def _split(a):
    hi = a.astype(BF16)
    lo = (a - hi.astype(F32)).astype(BF16)
    return hi, lo


def _dot3(a, b, dims=NN_DIMS):
    ah, al = _split(a)
    bh, bl = _split(b)
    d = functools.partial(lax.dot_general, dimension_numbers=dims, preferred_element_type=F32)
    return d(ah, bh) + (d(ah, bl) + d(al, bh))


def _rms(x, g):
    return x * lax.rsqrt(jnp.mean(x * x, axis=-1, keepdims=True) + NORM_EPS) * g


def _silu(x):
    return x * jax.nn.sigmoid(x)


def _full(shape):
    nd = len(shape)
    return pl.BlockSpec(shape, lambda *_: (0,) * nd)


def _mod_kernel(c_ref, w_ref, b_ref, o_ref):
    o_ref[...] = _dot3(_silu(c_ref[...]), w_ref[...]) + b_ref[...]


def _mod_call(c_rows, w_mod, b_mod):
    r, d = c_rows.shape
    n = w_mod.shape[1]
    bn = 1024
    return pl.pallas_call(
        _mod_kernel,
        grid=(n // bn,),
        in_specs=[_full((r, d)), pl.BlockSpec((d, bn), lambda j: (0, j)), pl.BlockSpec((1, bn), lambda j: (0, j))],
        out_specs=pl.BlockSpec((r, bn), lambda j: (0, j)),
        out_shape=jax.ShapeDtypeStruct((r, n), F32),
        compiler_params=_params("arbitrary"),
        name="mod",
    )(c_rows, w_mod, b_mod.reshape(1, n))


def _prenorm(x, mod_ref, row, g):
    shift = mod_ref[0, row:row + 1, :]
    scale = mod_ref[0, row + 1:row + 2, :]
    return _rms(x, g) * (1.0 + scale) + shift


def _kv_heads(kv_lat, kpe, kvn_ref, wuk_ref, wuvt_ref, k_out, vt_out):
    kvn = _rms(kv_lat, kvn_ref[...]).astype(BF16)
    kk = _dot(kvn, wuk_ref[...])
    vt = lax.dot_general(wuvt_ref[...], kvn, NT_DIMS, preferred_element_type=F32)
    ones = jnp.ones((HEAD_PAD - V_HEAD, vt.shape[1]), F32)
    for h in range(N_HEADS):
        k_out[0, h] = (kk[:, HEAD_PAD * h:HEAD_PAD * (h + 1)] + kpe).astype(BF16)
        vt_out[0, h] = jnp.concatenate([vt[V_HEAD * h:V_HEAD * (h + 1)], ones], axis=0).astype(BF16)


def _ctx_kernel(c_ref, mod_ref, nm_ref, w_ref, b_ref, kvn_ref, wuk_ref, wuv_ref, k_out, v_out):
    h = _prenorm(c_ref[0], mod_ref, 0, nm_ref[...]).astype(BF16)
    a = _dot(h, w_ref[...]) + b_ref[...]
    _kv_heads(a[:, :KV_LORA], a[:, KV_LORA:], kvn_ref, wuk_ref, wuv_ref, k_out, v_out)


def _ctx_call(ctx, modc, norm_mix, w_c, b_c, kv_norm, w_uk, w_uv):
    bsz, n, d = ctx.shape
    return pl.pallas_call(
        _ctx_kernel,
        grid=(bsz,),
        in_specs=[pl.BlockSpec((1, n, d), lambda b: (b, 0, 0)), _full(modc.shape), _full(norm_mix.shape),
                  _full(w_c.shape), _full(b_c.shape), _full(kv_norm.shape), _full(w_uk.shape), _full(w_uv.shape)],
        out_specs=[pl.BlockSpec((1, N_HEADS, n, HEAD_PAD), lambda b: (b, 0, 0, 0)),
                   pl.BlockSpec((1, N_HEADS, HEAD_PAD, n), lambda b: (b, 0, 0, 0))],
        out_shape=[jax.ShapeDtypeStruct((bsz, N_HEADS, n, HEAD_PAD), BF16),
                   jax.ShapeDtypeStruct((bsz, N_HEADS, HEAD_PAD, n), BF16)],
        compiler_params=_params("arbitrary"),
        name="ctx",
    )(ctx, modc, norm_mix, w_c, b_c, kv_norm, w_uk, w_uv)


def _inproj_kernel(x_ref, xp_ref, xn_ref, mod_ref, nm_ref, wa_ref, ba_ref, why_ref, bhy_ref, wg_ref, bg_ref,
                   qn_ref, wuq_ref, wuqs_ref, kvn_ref, wuk_ref, wuv_ref, cos_ref, sin_ref, cw_ref, cb_ref,
                   q_out, k_out, v_out, hv_out, hx1_out, hx2_out, g_out):
    i = pl.program_id(0)
    tm = x_ref.shape[1]
    nm = nm_ref[...]
    h = _prenorm(x_ref[0], mod_ref, 0, nm).astype(BF16)
    a = _dot(h, wa_ref[...]) + ba_ref[...]
    q_lat = a[:, :Q_LORA]
    kv_lat = a[:, Q_LORA:Q_LORA + KV_LORA]
    kpe_m = a[:, Q_LORA + KV_LORA:Q_LORA + KV_LORA + HEAD_PAD]
    kpe_s = a[:, Q_LORA + KV_LORA + HEAD_PAD:]
    cos = cos_ref[...]
    sin = sin_ref[...]
    qn = _rms(q_lat, qn_ref[...]).astype(BF16)
    qa = _dot(qn, wuq_ref[...])
    qs = _dot(qn, wuqs_ref[...])
    for hh in range(N_HEADS):
        sl = slice(HEAD_PAD * hh, HEAD_PAD * (hh + 1))
        q_out[0, hh] = (qa[:, sl] * cos + qs[:, sl] * sin).astype(BF16)
    _kv_heads(kv_lat, kpe_m * cos + kpe_s * sin, kvn_ref, wuk_ref, wuv_ref, k_out, v_out)
    g_out[0] = (_dot(h, wg_ref[...]) + bg_ref[...]).astype(BF16)

    why = why_ref[...]
    bhy = bhy_ref[...]
    hy = _dot(h, why) + bhy
    hp = _dot(_prenorm(xp_ref[0], mod_ref, 0, nm).astype(BF16), why) + bhy
    hn = _dot(_prenorm(xn_ref[0], mod_ref, 0, nm).astype(BF16), why) + bhy
    prev = jnp.where(i == 0, 0.0, hp[SUBLANES - 1:SUBLANES])
    nxt = jnp.where(i == pl.num_programs(0) - 1, 0.0, hn[0:1])
    rid = lax.broadcasted_iota(jnp.int32, (tm, 1), 0)
    up = jnp.where(rid == 0, prev, pltpu.roll(hy, 1, 0))
    dn = jnp.where(rid == tm - 1, nxt, pltpu.roll(hy, tm - 1, 0))
    u = up * cw_ref[0:1, :] + hy * cw_ref[1:2, :] + dn * cw_ref[2:3, :] + cb_ref[...]
    hv_out[0] = u[:, :HY_WIDTH]
    hx1_out[0] = u[:, HY_WIDTH:2 * HY_WIDTH]
    hx2_out[0] = u[:, 2 * HY_WIDTH:]


def _inproj_call(x, mod, norm_mix, wa, ba, why, bhy, wg, bg, q_norm, wuq, wuqs, kv_norm, wuk, wuvt, cos_t, sin_t, cw,
                 cb, tm):
    bsz, s, d = x.shape
    nt = s // tm
    rb = tm // SUBLANES
    last_rb = s // SUBLANES - 1
    consts = [norm_mix, wa, ba, why, bhy, wg, bg, q_norm, wuq, wuqs, kv_norm, wuk, wuvt]
    in_specs = [
        pl.BlockSpec((1, tm, d), lambda i, b: (b, i, 0)),
        pl.BlockSpec((1, SUBLANES, d), lambda i, b: (b, jnp.maximum(i * rb - 1, 0), 0)),
        pl.BlockSpec((1, SUBLANES, d), lambda i, b: (b, jnp.minimum((i + 1) * rb, last_rb), 0)),
        pl.BlockSpec((1, SUBLANES, d), lambda i, b: (b, 0, 0)),
    ] + [_full(c.shape) for c in consts] + [
        pl.BlockSpec((tm, HEAD_PAD), lambda i, b: (i, 0)),
        pl.BlockSpec((tm, HEAD_PAD), lambda i, b: (i, 0)),
        _full(cw.shape), _full(cb.shape),
    ]
    hw = HY_WIDTH
    out_specs = [
        pl.BlockSpec((1, N_HEADS, tm, HEAD_PAD), lambda i, b: (b, 0, i, 0)),
        pl.BlockSpec((1, N_HEADS, tm, HEAD_PAD), lambda i, b: (b, 0, i, 0)),
        pl.BlockSpec((1, N_HEADS, HEAD_PAD, tm), lambda i, b: (b, 0, 0, i)),
        pl.BlockSpec((1, tm, hw), lambda i, b: (b, i, 0)),
        pl.BlockSpec((1, tm, hw), lambda i, b: (b, i, 0)),
        pl.BlockSpec((1, tm, hw), lambda i, b: (b, i, 0)),
        pl.BlockSpec((1, tm, 2 * d), lambda i, b: (b, i, 0)),
    ]
    out_shape = [
        jax.ShapeDtypeStruct((bsz, N_HEADS, s, HEAD_PAD), BF16),
        jax.ShapeDtypeStruct((bsz, N_HEADS, s, HEAD_PAD), BF16),
        jax.ShapeDtypeStruct((bsz, N_HEADS, HEAD_PAD, s), BF16),
        jax.ShapeDtypeStruct((bsz, s, hw), F32),
        jax.ShapeDtypeStruct((bsz, s, hw), F32),
        jax.ShapeDtypeStruct((bsz, s, hw), F32),
        jax.ShapeDtypeStruct((bsz, s, 2 * d), BF16),
    ]
    return pl.pallas_call(
        _inproj_kernel,
        grid=(nt, bsz),
        in_specs=in_specs,
        out_specs=out_specs,
        out_shape=out_shape,
        compiler_params=_params("arbitrary", "arbitrary"),
        name="inproj",
    )(x, x, x, mod, *consts, cos_t, sin_t, cw, cb)


def _attn_kernel(q_ref, k_ref, vt_ref, o_ref, m_sc, acc_sc):
    j = pl.program_id(2)

    @pl.when(j == 0)
    def _():
        m_sc[...] = jnp.full(m_sc.shape, -jnp.inf, F32)
        acc_sc[...] = jnp.zeros(acc_sc.shape, F32)

    tq = q_ref.shape[2]
    qw = min(tq, Q_CHUNK)
    units = [(h, c) for h in range(N_HEADS) for c in range(0, tq, qw)]

    def scores(u):
        h, c = units[u]
        return lax.dot_general(k_ref[0, h], q_ref[0, h, c:c + qw, :], NT_DIMS,
                               preferred_element_type=F32)

    pending = [scores(u) for u in range(AHEAD)]
    for u, (h, c) in enumerate(units):
        if u + AHEAD < len(units):
            pending.append(scores(u + AHEAD))
        st = pending.pop(0)
        m_prev = m_sc[h, :, c:c + qw]
        m_new = jnp.maximum(m_prev, jnp.max(st, axis=0, keepdims=True))
        pt = jnp.exp2(st - m_new).astype(BF16)
        acc_sc[h, :, c:c + qw] = (jnp.exp2(m_prev - m_new) * acc_sc[h, :, c:c + qw]
                                  + jnp.dot(vt_ref[0, h], pt, preferred_element_type=F32))
        m_sc[h, :, c:c + qw] = m_new

    @pl.when(j == pl.num_programs(2) - 1)
    def _():
        ot = jnp.concatenate([acc_sc[h, :V_HEAD] / acc_sc[h, V_HEAD:V_HEAD + 1] for h in range(N_HEADS)], axis=0)
        o_ref[0] = ot.T.astype(o_ref.dtype)


def _attn_call(q, k, vt, tq, tk):
    bsz, nh, s, dh = q.shape
    nk = k.shape[2]
    dv = nh * V_HEAD
    return pl.pallas_call(
        _attn_kernel,
        grid=(bsz, s // tq, nk // tk),
        in_specs=[
            pl.BlockSpec((1, nh, tq, dh), lambda b, i, j: (b, 0, i, 0)),
            pl.BlockSpec((1, nh, tk, dh), lambda b, i, j: (b, 0, j, 0)),
            pl.BlockSpec((1, nh, dh, tk), lambda b, i, j: (b, 0, 0, j)),
        ],
        out_specs=pl.BlockSpec((1, tq, dv), lambda b, i, j: (b, i, 0)),
        out_shape=jax.ShapeDtypeStruct((bsz, s, dv), BF16),
        scratch_shapes=[pltpu.VMEM((nh, 1, tq), F32), pltpu.VMEM((nh, dh, tq), F32)],
        compiler_params=_params("arbitrary", "arbitrary", "arbitrary"),
        name="attn",
    )(q, k, vt)


def _filter_kernel(emb_ref, w1_ref, b1_ref, w2_ref, b2_ref, w3_ref, fr_ref, dl_ref, full_out, asum_out, *, seq):
    r = pl.program_id(0)
    rb = emb_ref.shape[0]
    emb = emb_ref[...]
    fr = fr_ref[...]
    h = jnp.sin(fr * (_dot3(emb, w1_ref[...]) + b1_ref[...]))
    h = jnp.sin(fr * (_dot3(h, w2_ref[...]) + b2_ref[...]))
    k = _dot3(h, w3_ref[0]) * jnp.exp(-emb[:, 0:1] * dl_ref[...])
    row = r * rb + lax.broadcasted_iota(jnp.int32, (rb, 1), 0)
    k = jnp.where(row == seq, 0.0, k)
    full_out[...] = k

    @pl.when(r == 0)
    def _():
        asum_out[...] = jnp.zeros(asum_out.shape, F32)

    asum_out[...] += jnp.sum(jnp.abs(k), axis=0, keepdims=True)


def _filter_call(emb, w1, b1, w2, b2, w3sel, freq, deltas2, seq, rb):
    n2 = emb.shape[0]
    half_blocks = seq // rb
    width = w3sel.shape[2]
    return pl.pallas_call(
        functools.partial(_filter_kernel, seq=seq),
        grid=(n2 // rb,),
        in_specs=[pl.BlockSpec((rb, HY_EMB_PAD), lambda r: (r, 0)), _full(w1.shape), _full(b1.shape),
                  _full(w2.shape), _full(b2.shape),
                  pl.BlockSpec((1,) + w3sel.shape[1:], lambda r: (r // half_blocks, 0, 0)),
                  _full(freq.shape), _full(deltas2.shape)],
        out_specs=[pl.BlockSpec((rb, width), lambda r: (r, 0)), pl.BlockSpec((1, width), lambda r: (0, 0))],
        out_shape=[jax.ShapeDtypeStruct((n2, width), F32), jax.ShapeDtypeStruct((1, width), F32)],
        compiler_params=_params("arbitrary"),
        name="filt",
    )(emb, w1, b1, w2, b2, w3sel, freq, deltas2)


def _fa_kernel(u_ref, f_ref, a_out):
    two, _, hn, g, c = u_ref.shape
    a = _dot(f_ref[...], u_ref[...].reshape(two * hn * g, c))
    a_out[...] = a.reshape(a_out.shape)


def _fa_call(u5, fmat):
    _, p, hn, n, c = u5.shape
    g = SUBLANES
    return pl.pallas_call(
        _fa_kernel,
        grid=(p, n // g),
        in_specs=[pl.BlockSpec((2, 1, hn, g, c), lambda q, j: (0, q, 0, j, 0)), _full(fmat.shape)],
        out_specs=pl.BlockSpec((1, 2, n, g, c), lambda q, j: (q, 0, 0, j, 0)),
        out_shape=jax.ShapeDtypeStruct((p, 2, n, n, c), F32),
        compiler_params=_params("arbitrary", "arbitrary"),
        name="fa",
    )(u5, fmat)


def _fb_kernel(a_ref, g_ref, asum_ref, kf_out):
    two, _, n, c = a_ref.shape[1:]
    a = a_ref[0].reshape(two * n, c)
    x = _dot(g_ref[0], a) / (asum_ref[...] + 1e-6)
    kf_out[0] = x.reshape(two, n, c)


def _fb_call(a5, gmat, asum):
    _, _, n, _, c = a5.shape
    return pl.pallas_call(
        _fb_kernel,
        grid=(n,),
        in_specs=[pl.BlockSpec((1, 2, 1, n, c), lambda k: (0, 0, k, 0, 0)),
                  pl.BlockSpec((1, 2 * n, 2 * n), lambda k: (k, 0, 0)), _full(asum.shape)],
        out_specs=pl.BlockSpec((1, 2, n, c), lambda k: (k, 0, 0, 0)),
        out_shape=jax.ShapeDtypeStruct((n, 2, n, c), F32),
        compiler_params=_params("arbitrary"),
        name="fb",
    )(a5, gmat, asum)


def _mid_kernel(a_ref, g_ref, h_ref, kf_ref, b_out):
    _, two, kb, n, c = a_ref.shape
    for kk in range(kb):
        x = _dot(g_ref[kk], a_ref[0, :, kk].reshape(two * n, c))
        xr, xi = x[:n], x[n:]
        kr, ki = kf_ref[kk, 0], kf_ref[kk, 1]
        y = jnp.concatenate([xr * kr - xi * ki, xr * ki + xi * kr], axis=0)
        b_out[0, :, kk] = _dot(h_ref[kk], y).reshape(two, n, c)


def _mid_call(a5, gmat, hmat, kf, order, kb):
    p, _, n, _, c = a5.shape
    return pl.pallas_call(
        _mid_kernel,
        grid=(n // kb, p),
        in_specs=[pl.BlockSpec((1, 2, kb, n, c), lambda k, q: (q, 0, k, 0, 0)),
                  pl.BlockSpec((kb, 2 * n, 2 * n), lambda k, q: (k, 0, 0)),
                  pl.BlockSpec((kb, 2 * n, 2 * n), lambda k, q: (k, 0, 0)),
                  pl.BlockSpec((kb, 2, n, c), lambda k, q: (k, 0, 0, order))],
        out_specs=pl.BlockSpec((1, 2, kb, n, c), lambda k, q: (q, 0, k, 0, 0)),
        out_shape=jax.ShapeDtypeStruct(a5.shape, F32),
        compiler_params=_params("arbitrary", "arbitrary"),
        name="mid",
    )(a5, gmat, hmat, kf)


def _fc_kernel(b_ref, f_ref, u_ref, m_ref, skip_ref, o_out):
    _, two, n, g, c = b_ref.shape
    y = _dot(f_ref[...], b_ref[...].reshape(two * n * g, c)).reshape(u_ref.shape)
    o_out[...] = m_ref[...] * (y + u_ref[...] * skip_ref[...])


def _fc_call(b5, finv, u5, m5, skip_row):
    _, p, hn, n, c = u5.shape
    g = SUBLANES
    blk = pl.BlockSpec((2, 1, hn, g, c), lambda q, j: (0, q, 0, j, 0))
    return pl.pallas_call(
        _fc_kernel,
        grid=(p, n // g),
        in_specs=[pl.BlockSpec((1, 2, n, g, c), lambda q, j: (q, 0, 0, j, 0)), _full(finv.shape), blk, blk,
                  _full(skip_row.shape)],
        out_specs=blk,
        out_shape=jax.ShapeDtypeStruct(u5.shape, F32),
        compiler_params=_params("arbitrary", "arbitrary"),
        name="fc",
    )(b5, finv, u5, m5, skip_row)


def _dft_tables(n):
    hn = n // 2
    k = np.arange(n)[:, None]
    ang = -2.0 * np.pi * (k * np.arange(n)[None, :] % n) / n
    fr, fi = np.cos(ang), np.sin(ang)
    f_data = np.block([[fr[:, :hn], -fi[:, :hn]], [fi[:, :hn], fr[:, :hn]]])
    f_filt = np.concatenate([fr, fi], axis=0)
    er, ei = fr[:hn], -fi[:hn]
    f_inv = np.block([[er, -ei], [ei, er]]) / float(n * n)
    k1 = jnp.arange(n, dtype=jnp.int32)[:, None, None]
    k2 = jnp.arange(n, dtype=jnp.int32)[None, :, None]
    m2 = jnp.arange(n, dtype=jnp.int32)[None, None, :]
    idx = (m2 * (k1 + n * k2)) % (n * n)
    ang2 = idx.astype(F32) * (-2.0 * math.pi / (n * n))
    gr, gi = jnp.cos(ang2), jnp.sin(ang2)
    g = jnp.concatenate([jnp.concatenate([gr, -gi], axis=2), jnp.concatenate([gi, gr], axis=2)], axis=1)
    h = jnp.swapaxes(g, 1, 2)

    def widen(f):
        return jnp.asarray(np.kron(f, np.eye(SUBLANES)), BF16)

    return widen(f_data), widen(f_filt), widen(f_inv), g.astype(BF16), h.astype(BF16)


def _hyena_filter_tables(seq):
    t = jnp.linspace(0.0, 1.0, seq, dtype=F32)[:, None]
    w = 2.0 * math.pi * jnp.arange(seq, dtype=F32)[:, None] / seq
    f = jnp.linspace(1e-4, HY_BANDS - 1, HY_BANDS, dtype=F32)[None, :]
    emb = jnp.concatenate([t, jnp.cos(f * w), -jnp.sin(f * w)], axis=-1)
    pos = jnp.concatenate([jnp.arange(seq), jnp.array([0]), jnp.arange(seq - 1, 0, -1)])
    emb = jnp.pad(emb[pos], ((0, 0), (0, HY_EMB_PAD - HY_EMB)))
    deltas = jnp.abs(jnp.linspace(math.log(HY_DECAY_TARGET) / HY_SLOW_DECAY,
                                  math.log(HY_DECAY_TARGET) / HY_FAST_DECAY, HY_WIDTH, dtype=F32))
    return emb, jnp.tile(deltas, HY_ORDER)[None, :]


def _hyena(hv, hx1, hx2, w1, b1, w2, b2, w3, freq, skip, kb):
    bsz, seq, c = hv.shape
    n = int(round(math.sqrt(2 * seq)))
    assert n * n == 2 * seq and bsz % 2 == 0
    hn, p = n // 2, bsz // 2
    f_data, f_filt, f_inv, gmat, hmat = _dft_tables(n)

    emb, deltas2 = _hyena_filter_tables(seq)
    w1p = jnp.pad(w1, ((0, HY_EMB_PAD - HY_EMB), (0, 0)))
    w3r = w3.reshape(w3.shape[0], HY_ORDER, 2, c)
    w3sel = jnp.stack([w3r[:, :, 0, :].reshape(-1, HY_ORDER * c), w3r[:, :, 1, :].reshape(-1, HY_ORDER * c)])
    full, asum = _filter_call(emb, w1p, b1[None], w2, b2[None], w3sel, freq[None], deltas2, seq, min(512, seq))
    c2 = HY_ORDER * c
    kf = _fb_call(_fa_call(full.reshape(2, 1, hn, n, c2), f_filt), gmat, asum)

    def view(t):
        return t.reshape(2, p, hn, n, c)

    def long_conv(u5, m5, order):
        bm = _mid_call(_fa_call(u5, f_data), gmat, hmat, kf, order, kb)
        return _fc_call(bm, f_inv, u5, m5, skip[order][None, :])

    z = long_conv(view(hv), view(hx1), 0)
    return long_conv(z, view(hx2), 1).reshape(bsz, seq, c)


def _merge_kernel(x_ref, at_ref, hy_ref, g_ref, mod_ref, wba_ref, wbh_ref, wo_ref, o_ref):
    d = x_ref.shape[2]
    g = g_ref[0].astype(F32)
    y = (jax.nn.sigmoid(g[:, :d]) * _dot(at_ref[0], wba_ref[...])
         + jax.nn.sigmoid(g[:, d:]) * _dot(hy_ref[0], wbh_ref[...]))
    o_ref[0] = x_ref[0] + mod_ref[0, 2:3, :] * _dot(y, wo_ref[...])


def _merge_call(x, attn, hy, gate, mod, wba, wbh, wo, tm):
    bsz, s, d = x.shape

    def tok(w):
        return pl.BlockSpec((1, tm, w), lambda b, i: (b, i, 0))

    return pl.pallas_call(
        _merge_kernel,
        grid=(bsz, s // tm),
        in_specs=[tok(d), tok(attn.shape[2]), tok(hy.shape[2]), tok(2 * d),
                  pl.BlockSpec((1, SUBLANES, d), lambda b, i: (b, 0, 0)),
                  _full(wba.shape), _full(wbh.shape), _full(wo.shape)],
        out_specs=tok(d),
        out_shape=jax.ShapeDtypeStruct((bsz, s, d), F32),
        compiler_params=_params("arbitrary", "arbitrary"),
        name="merge",
    )(x, attn, hy, gate, mod, wba, wbh, wo)


def _route_kernel(xm_ref, mod_ref, nf_ref, wrt_ref, rb_ref, tri_ref, lt_ref, h2_out, w_out, p_out, col_out, row_out):
    tm = xm_ref.shape[1]
    ng, gs = N_GROUPS, GROUP_SIZE

    h2 = _prenorm(xm_ref[0], mod_ref, 3, nf_ref[...])
    h2_out[0] = h2.astype(h2_out.dtype)
    scores = jax.nn.sigmoid(_dot3(wrt_ref[...], h2, NT_DIMS))
    sel = scores + rb_ref[...]
    slabs = [sel[ng * j:ng * (j + 1)] for j in range(gs)]

    top1 = jnp.full((ng, tm), -jnp.inf, F32)
    top2 = top1
    for x in slabs:
        top2 = jnp.maximum(top2, jnp.minimum(top1, x))
        top1 = jnp.maximum(top1, x)
    gscore = top1 + top2
    gid = lax.broadcasted_iota(jnp.int32, (ng, 1), 0)
    rank = jnp.zeros((ng, tm), jnp.int32)
    for g2 in range(ng):
        row = gscore[g2:g2 + 1]
        beats = (row > gscore) | ((row == gscore) & (g2 < gid))
        rank = rank + beats.astype(jnp.int32)
    gmask = rank < TOPK_GROUPS

    cand = [jnp.where(gmask, x, -jnp.inf) for x in slabs]
    eid = [gid * gs + j for j in range(gs)]
    chosen = []
    for _ in range(TOP_K):
        best = functools.reduce(jnp.maximum, cand)
        best = jnp.max(best, axis=0, keepdims=True)
        idx = functools.reduce(jnp.minimum, [jnp.where(cand[j] == best, eid[j], N_EXPERTS) for j in range(gs)])
        idx = jnp.min(idx, axis=0, keepdims=True)
        chosen.append(idx)
        cand = [jnp.where(eid[j] == idx, -jnp.inf, cand[j]) for j in range(gs)]

    mask = [functools.reduce(jnp.logical_or, [eid[j] == idx for idx in chosen]) for j in range(gs)]
    maskb = jnp.concatenate(mask, axis=0)
    maskf = jnp.where(maskb, 1.0, 0.0)
    mask16 = maskf.astype(BF16)
    wsel = jnp.where(maskb, scores, 0.0)
    w_out[...] = wsel / jnp.sum(wsel, axis=0, keepdims=True) * ROUTE_SCALE
    before = jnp.dot(mask16, tri_ref[...], preferred_element_type=F32)
    p_out[...] = jnp.where(maskb, before, -1.0).astype(p_out.dtype)

    def extents(cnt, lower_sum):
        units = jnp.floor((cnt + (RUN_ALIGN - 1)) * (1.0 / RUN_ALIGN))
        start = RUN_ALIGN * lower_sum(units.astype(BF16))
        return start, start + RUN_ALIGN * units

    cnt_c = jnp.sum(maskf, axis=1, keepdims=True)
    start_c, end_c = extents(jnp.broadcast_to(cnt_c, (N_EXPERTS, LANES)),
                             lambda u: jnp.dot(lt_ref[...], u, preferred_element_type=F32))
    lane = lax.broadcasted_iota(jnp.int32, (N_EXPERTS, LANES), 1)
    col_out[0] = jnp.where(lane == 0, cnt_c, jnp.where(lane == 1, start_c, end_c))
    cnt_r = lax.dot_general(jnp.ones((SUBLANES, tm), BF16), mask16, NT_DIMS, preferred_element_type=F32)
    start_r, end_r = extents(cnt_r, lambda u: lax.dot_general(u, lt_ref[...], NT_DIMS, preferred_element_type=F32))
    sub = lax.broadcasted_iota(jnp.int32, (SUBLANES, N_EXPERTS), 0)
    row_out[0] = jnp.where(sub == 0, start_r, end_r)


def _route_call(xm, mod, norm_ffn, wrt, rbias, lower, tm):
    bsz, s, d = xm.shape
    t = bsz * s
    nt = s // tm
    tri = (jnp.arange(tm)[:, None] < jnp.arange(tm)[None, :]).astype(BF16)
    tok = pl.BlockSpec((N_EXPERTS, tm), lambda i: (0, i))
    return pl.pallas_call(
        _route_kernel,
        grid=(t // tm,),
        in_specs=[pl.BlockSpec((1, tm, d), lambda i: (i // nt, i % nt, 0)),
                  pl.BlockSpec((1, SUBLANES, d), lambda i: (i // nt, 0, 0)),
                  _full(norm_ffn.shape), _full(wrt.shape), _full(rbias.shape), _full(tri.shape),
                  _full(lower.shape)],
        out_specs=[pl.BlockSpec((1, tm, d), lambda i: (i // nt, i % nt, 0)), tok, tok,
                   pl.BlockSpec((1, N_EXPERTS, LANES), lambda i: (i, 0, 0)),
                   pl.BlockSpec((1, SUBLANES, N_EXPERTS), lambda i: (i, 0, 0))],
        out_shape=[jax.ShapeDtypeStruct((bsz, s, d), BF16), jax.ShapeDtypeStruct((N_EXPERTS, t), F32),
                   jax.ShapeDtypeStruct((N_EXPERTS, t), BF16),
                   jax.ShapeDtypeStruct((t // tm, N_EXPERTS, LANES), F32),
                   jax.ShapeDtypeStruct((t // tm, SUBLANES, N_EXPERTS), F32)],
        compiler_params=_params("arbitrary"),
        name="route",
    )(xm, mod, norm_ffn, wrt, rbias, tri, lower)


def _pack(x):
    w = x.shape[1] // 2
    lo = lax.bitcast_convert_type(x[:, :w].astype(BF16).astype(F32), jnp.uint32)
    hi = lax.bitcast_convert_type(x[:, w:].astype(BF16).astype(F32), jnp.uint32)
    return hi | (lo >> 16)


def _unpack(u):
    lo = lax.bitcast_convert_type(u << 16, F32).astype(BF16)
    hi = lax.bitcast_convert_type(u & jnp.uint32(0xFFFF0000), F32).astype(BF16)
    return lo, hi


def _pow2_pieces(units, limit):
    bit = 1
    while bit * 2 <= limit:
        bit *= 2
    while bit:
        yield (units & bit) != 0, units & ~(2 * bit - 1), bit
        bit //= 2


def _rows_copy(vm_ref, hbm_ref, sem, vm_row, hbm_row, rows, to_hbm):
    v = vm_ref.at[pl.ds(pl.multiple_of(vm_row, RUN_ALIGN), rows), :]
    h = hbm_ref.at[pl.ds(pl.multiple_of(hbm_row, RUN_ALIGN), rows), :]
    return pltpu.make_async_copy(v, h, sem) if to_hbm else pltpu.make_async_copy(h, v, sem)


def _pow2_below(limit):
    return [1 << b for b in range(limit.bit_length() - 1, -1, -1)] if limit > 0 else []


def _run_copies(vm_ref, hbm_ref, sem, n8, vm_row, hbm_row, bits, to_hbm, act, guard=None):
    for bit in bits:
        on = (n8 & bit) != 0
        off = RUN_ALIGN * (n8 & ~(2 * bit - 1))

        @pl.when(on if guard is None else on & guard)
        def _():
            act(_rows_copy(vm_ref, hbm_ref, sem, vm_row + off, hbm_row + off, RUN_ALIGN * bit, to_hbm))


def _tile_runs(tables, tile, vm_ref, hbm_ref, sem, tm, to_hbm, guard, static):
    n8_ref, ls_ref, gs_ref, long_ref = tables
    limit = tm // RUN_ALIGN
    short_bits = [b for b in _pow2_below(limit) if b < LONG_RUN]
    long_bits = [b for b in _pow2_below(limit) if b >= LONG_RUN]

    def one(e, bits, g):
        i = tile * N_EXPERTS + e
        _run_copies(vm_ref, hbm_ref, sem, n8_ref[i], ls_ref[i], gs_ref[i], bits, to_hbm, lambda cp: cp.start(), g)

    def loop(bits, g):
        def body(e, c):
            one(e, bits, g)
            return c
        lax.fori_loop(0, N_EXPERTS, body, 0)

    if static:
        for e in range(N_EXPERTS):
            one(e, short_bits, guard)
    else:
        loop(short_bits, guard)
    if long_bits:
        has_long = long_ref[tile] != 0
        pl.when(has_long if guard is None else has_long & guard)(lambda: loop(long_bits, None))


def _wait_rows(vm_ref, hbm_ref, sem, units, limit, to_hbm):
    for on, _, size in _pow2_pieces(units, limit):
        @pl.when(on)
        def _():
            _rows_copy(vm_ref, hbm_ref, sem, 0, 0, RUN_ALIGN * size, to_hbm).wait()


def _dispatch_kernel(n8_ref, ls_ref, gs_ref, long_ref, ts_ref, t8_ref, nu_ref, pos_ref, ext_ref, h_ref, xs_out,
                     srt2, zbuf, sems):
    step = pl.program_id(0)
    last_step = pl.num_programs(0) - 1
    tm = h_ref.shape[0]
    rows = srt2.shape[1]
    slot = step % 2
    srt, sem = srt2.at[slot], sems.at[slot]
    tables = (n8_ref, ls_ref, gs_ref, long_ref)

    def wait_tile(tile, s):
        last = tile * N_EXPERTS + N_EXPERTS - 1
        _wait_rows(srt2.at[s], xs_out, sems.at[s], ls_ref[last] // RUN_ALIGN + n8_ref[last], rows // RUN_ALIGN, True)

    pl.when(step >= 2)(lambda: wait_tile(step - 2, slot))
    _tile_runs(tables, jnp.maximum(step - 1, 0), srt2.at[1 - slot], xs_out, sems.at[1 - slot], tm, True, step > 0, True)

    rid = lax.broadcasted_iota(jnp.int32, (rows, 1), 0).astype(F32)
    start = ext_ref[0, 0:1, :]
    member = jnp.where((rid >= start) & (rid < ext_ref[0, 1:2, :]), 1.0, 0.0)
    offset = rid - jnp.sum(member * start, axis=1, keepdims=True)
    pos = jnp.dot(member.astype(BF16), pos_ref[...], preferred_element_type=F32)
    sel = jnp.where(pos == offset, 1.0, 0.0).astype(BF16)
    srt[...] = _pack(jnp.dot(sel, h_ref[...], preferred_element_type=F32))

    @pl.when(step == last_step)
    def _():
        _tile_runs(tables, step, srt, xs_out, sem, tm, True, None, False)
        pl.when(step > 0)(lambda: wait_tile(step - 1, 1 - slot))
        wait_tile(step, slot)
        zbuf[...] = jnp.zeros(zbuf.shape, zbuf.dtype)
        nblk = xs_out.shape[0] // EXPERT_BLOCK
        tail_bits = _pow2_below(EXPERT_BLOCK // RUN_ALIGN - 1)

        def fill(act):
            def tails(e, c):
                _run_copies(zbuf, xs_out, sem, t8_ref[e], 0, ts_ref[e], tail_bits, True, act)
                return c

            def blocks(b, c):
                act(pltpu.make_async_copy(
                    zbuf, xs_out.at[pl.ds(pl.multiple_of(b * EXPERT_BLOCK, EXPERT_BLOCK), EXPERT_BLOCK), :], sem))
                return c

            lax.fori_loop(0, N_EXPERTS, tails, 0)
            lax.fori_loop(nu_ref[0], nblk, blocks, 0)

        fill(lambda cp: cp.start())
        fill(lambda cp: cp.wait())


def _dispatch_call(tables, pos_et, ext_rows, h2, nblk, tm):
    t, d = h2.shape
    lrows = TOP_K * tm + N_EXPERTS * RUN_ALIGN
    return pl.pallas_call(
        _dispatch_kernel,
        grid_spec=pltpu.PrefetchScalarGridSpec(
            num_scalar_prefetch=len(tables), grid=(t // tm,),
            in_specs=[pl.BlockSpec((N_EXPERTS, tm), lambda i, *_: (0, i)),
                      pl.BlockSpec((1,) + ext_rows.shape[1:], lambda i, *_: (i, 0, 0)),
                      pl.BlockSpec((tm, d), lambda i, *_: (i, 0))],
            out_specs=pl.BlockSpec(memory_space=pl.ANY),
            scratch_shapes=[pltpu.VMEM((2, lrows, d // 2), jnp.uint32),
                            pltpu.VMEM((EXPERT_BLOCK, d // 2), jnp.uint32), pltpu.SemaphoreType.DMA((2,))]),
        out_shape=jax.ShapeDtypeStruct((nblk * EXPERT_BLOCK, d // 2), jnp.uint32),
        compiler_params=_params("arbitrary"),
        name="dispatch",
    )(*tables, pos_et, ext_rows, h2)


def _expert_kernel(blk_ref, nused_ref, x_ref, wgu_ref, wd_ref, y_ref):
    used = pl.program_id(0) < nused_ref[0]

    @pl.when(used)
    def _():
        lo, hi = _unpack(x_ref[...])
        half = lo.shape[1]
        gu = (jnp.dot(lo, wgu_ref[0, :half, :], preferred_element_type=F32)
              + jnp.dot(hi, wgu_ref[0, half:, :], preferred_element_type=F32))
        a = _silu(gu[:, :EXPERT_FF]) * gu[:, EXPERT_FF:]
        y_ref[...] = _pack(_dot(a, wd_ref[0]))

    @pl.when(jnp.logical_not(used))
    def _():
        y_ref[...] = jnp.zeros(y_ref.shape, y_ref.dtype)


def _expert_call(blk_e, nused, xs, wgu, wd):
    rows, d = xs.shape
    nblk = rows // EXPERT_BLOCK

    def row_map(i, blk, nu):
        return (jnp.minimum(i, nu[0] - 1), 0)

    return pl.pallas_call(
        _expert_kernel,
        grid_spec=pltpu.PrefetchScalarGridSpec(
            num_scalar_prefetch=2, grid=(nblk,),
            in_specs=[pl.BlockSpec((EXPERT_BLOCK, d), row_map),
                      pl.BlockSpec((1,) + wgu.shape[1:], lambda i, blk, nu: (blk[i], 0, 0)),
                      pl.BlockSpec((1,) + wd.shape[1:], lambda i, blk, nu: (blk[i], 0, 0))],
            out_specs=pl.BlockSpec((EXPERT_BLOCK, d), lambda i, blk, nu: (i, 0))),
        out_shape=jax.ShapeDtypeStruct((rows, d), jnp.uint32),
        compiler_params=_params("arbitrary"),
        name="expert",
    )(blk_e, nused, xs, wgu, wd)


def _combine_kernel(n8_ref, ls_ref, gs_ref, long_ref, ys_hbm, pos_ref, w_ref, ext_ref, xm_ref, h_ref, mod_ref,
                    wsgu_ref, wsd_ref, fn_ref, o_ref, ybuf2, sems):
    step = pl.program_id(0)
    last_step = pl.num_programs(0) - 1
    tm = xm_ref.shape[0]
    rows = ybuf2.shape[1]
    slot = step % 2
    ybuf, sem = ybuf2.at[slot], sems.at[slot]
    tables = (n8_ref, ls_ref, gs_ref, long_ref)

    pl.when(step == 0)(lambda: _tile_runs(tables, step, ybuf, ys_hbm, sem, tm, False, None, False))
    _tile_runs(tables, jnp.minimum(step + 1, last_step), ybuf2.at[1 - slot], ys_hbm, sems.at[1 - slot], tm, False,
               step < last_step, True)
    gu = _dot(h_ref[...], wsgu_ref[...])
    ff = gu.shape[1] // 2
    shared = _dot(_silu(gu[:, :ff]) * gu[:, ff:], wsd_ref[...])
    cid = lax.broadcasted_iota(jnp.int32, (1, rows), 1).astype(F32)
    start = ext_ref[0, :, 1:2]
    member = jnp.where((cid >= start) & (cid < ext_ref[0, :, 2:3]), 1.0, 0.0)
    offset = cid - jnp.sum(member * start, axis=0, keepdims=True)
    member = member.astype(BF16)
    pos = jnp.dot(pos_ref[...], member, preferred_element_type=F32)
    mix = jnp.where(pos == offset, jnp.dot(w_ref[...].astype(BF16), member, preferred_element_type=F32), 0.0)
    mix = mix.astype(BF16)
    last = step * N_EXPERTS + N_EXPERTS - 1
    filled = ls_ref[last] + RUN_ALIGN * n8_ref[last]
    _wait_rows(ybuf, ys_hbm, sem, filled // RUN_ALIGN, rows // RUN_ALIGN, False)
    rid = lax.broadcasted_iota(jnp.int32, (rows, 1), 0)
    lo, hi = _unpack(jnp.where(rid < filled, ybuf[...], jnp.uint32(0)))
    routed = jnp.concatenate([jnp.dot(mix, lo, preferred_element_type=F32),
                              jnp.dot(mix, hi, preferred_element_type=F32)], axis=1)
    x = xm_ref[...] + mod_ref[0, 5:6, :] * (routed + shared)
    o_ref[...] = _rms(x, fn_ref[...])


def _combine_call(tables, ys, pos_te, w_te, ext_cols, xm, h2, mod, wsgu, wsd, final_norm, tm, tiles_per_batch):
    t, d = xm.shape
    lrows = TOP_K * tm + N_EXPERTS * RUN_ALIGN
    tok = pl.BlockSpec((tm, d), lambda i, *_: (i, 0))
    per_e = pl.BlockSpec((tm, N_EXPERTS), lambda i, *_: (i, 0))
    return pl.pallas_call(
        _combine_kernel,
        grid_spec=pltpu.PrefetchScalarGridSpec(
            num_scalar_prefetch=len(tables), grid=(t // tm,),
            in_specs=[pl.BlockSpec(memory_space=pl.ANY), per_e, per_e,
                      pl.BlockSpec((1,) + ext_cols.shape[1:], lambda i, *_: (i, 0, 0)), tok, tok,
                      pl.BlockSpec((1, SUBLANES, d), lambda i, *_: (i // tiles_per_batch, 0, 0)),
                      _full(wsgu.shape), _full(wsd.shape), _full(final_norm.shape)],
            out_specs=tok,
            scratch_shapes=[pltpu.VMEM((2, lrows, d // 2), jnp.uint32), pltpu.SemaphoreType.DMA((2,))]),
        out_shape=jax.ShapeDtypeStruct((t, d), F32),
        compiler_params=_params("arbitrary"),
        name="combine",
    )(*tables, ys, pos_te, w_te, ext_cols, xm, h2, mod, wsgu, wsd, final_norm)


def _moe(xm, mod, norm_ffn, w_router, router_bias, wg, wu, wd, wsg, wsu, wsd, final_norm, tm):
    bsz, s, d = xm.shape
    t = bsz * s
    nt = t // tm
    perm = (np.arange(N_EXPERTS) % N_GROUPS) * GROUP_SIZE + np.arange(N_EXPERTS) // N_GROUPS
    wrt = w_router.T[perm]
    rbias = router_bias[perm][:, None]
    lower = jnp.asarray(perm[None, :] < perm[:, None], BF16)
    h2, w_et, pos_et, ext_cols, ext_rows = _route_call(xm, mod, norm_ffn, wrt, rbias, lower, tm)

    inv = np.argsort(perm)
    n8 = (ext_cols[:, :, 0].astype(jnp.int32)[:, inv] + (RUN_ALIGN - 1)) // RUN_ALIGN
    run = RUN_ALIGN * n8
    ls = jnp.cumsum(run, axis=1) - run
    tot = jnp.sum(run, axis=0)
    padded = (tot + EXPERT_BLOCK - 1) // EXPERT_BLOCK * EXPERT_BLOCK
    pad_end = jnp.cumsum(padded)
    gs = (pad_end - padded)[None, :] + jnp.cumsum(run, axis=0) - run
    nblk = -(-(t * TOP_K + nt * N_EXPERTS * (RUN_ALIGN - 1)) // EXPERT_BLOCK) + N_EXPERTS
    blk_first = jnp.arange(nblk, dtype=jnp.int32)[:, None] * EXPERT_BLOCK
    blk_e = jnp.minimum(jnp.sum((pad_end[None, :] <= blk_first).astype(jnp.int32), axis=1), N_EXPERTS - 1)
    nused = (pad_end[-1:] // EXPERT_BLOCK).astype(jnp.int32)
    tables = [a.reshape(-1).astype(jnp.int32) for a in (n8, ls, gs, jnp.any(n8 >= LONG_RUN, axis=1))]
    tails = [(pad_end - padded + tot).astype(jnp.int32), ((padded - tot) // RUN_ALIGN).astype(jnp.int32), nused]

    h2f = h2.reshape(t, d)
    xs = _dispatch_call(tables + tails, pos_et, ext_rows, h2f, nblk, tm)
    wgu = jnp.concatenate([wg, wu], axis=2).astype(BF16)
    ys = _expert_call(blk_e, nused, xs, wgu, wd.astype(BF16))
    wsgu = jnp.concatenate([wsg, wsu], axis=1).astype(BF16)
    out = _combine_call(tables, ys, pos_et.T, w_et.T, ext_cols, xm.reshape(t, d), h2f, mod, wsgu, wsd.astype(BF16),
                        final_norm, tm, s // tm)
    return out.reshape(bsz, s, d)


def _rope_tables(s):
    rows = s // GRID_W
    row = jnp.broadcast_to(jnp.arange(rows, dtype=F32)[:, None], (rows, GRID_W)).reshape(-1)
    col = jnp.broadcast_to(jnp.arange(GRID_W, dtype=F32)[None, :], (rows, GRID_W)).reshape(-1)
    half = QK_ROPE // 2
    inv_freq = ROPE_THETA ** (-jnp.arange(0, half, 2, dtype=F32) / half)
    ar, ac = row[:, None] * inv_freq, col[:, None] * inv_freq
    ones = jnp.ones((s, QK_NOPE), F32)
    tail = HEAD_PAD - QK_NOPE - QK_ROPE
    cos_t = jnp.concatenate([ones, jnp.cos(ar), jnp.cos(ar), jnp.cos(ac), jnp.cos(ac), jnp.ones((s, tail), F32)], 1)
    sin_t = jnp.concatenate([0 * ones, -jnp.sin(ar), jnp.sin(ar), -jnp.sin(ac), jnp.sin(ac),
                             jnp.zeros((s, tail), F32)], 1)
    return cos_t, sin_t


_Q4 = QK_ROPE // 4
ROPE_SWAP = np.concatenate([np.arange(_Q4, 2 * _Q4), np.arange(0, _Q4), np.arange(3 * _Q4, 4 * _Q4),
                            np.arange(2 * _Q4, 3 * _Q4)])


def _rope_slot(w, swap):
    if swap:
        w = w[..., ROPE_SWAP]
    pad = [(0, 0)] * (w.ndim - 1) + [(QK_NOPE, HEAD_PAD - QK_NOPE - QK_ROPE)]
    return jnp.pad(w, pad)


def kernel(x, c, ctx, c_ctx, w_mod, b_mod, norm_mix, norm_ffn, w_in, b_in, q_norm, w_uq, kv_norm, w_ukv, w_branch_attn, hy_conv_w, hy_conv_b, hy_filt_w1, hy_filt_b1, hy_filt_w2, hy_filt_b2, hy_filt_w3, hy_filt_freq, hy_skip, w_branch_hyena, w_out, w_router, router_bias, w_exp_gate, w_exp_up, w_exp_down, w_sh_gate, w_sh_up, w_sh_down, final_norm,
           tiles=None):
    bsz, s, d = x.shape
    tl = dict(inproj=256, tq=1024, tk=1408, fft_kb=4, merge=512, moe=256)
    tl.update(tiles or {})
    assert w_mod.shape[0] == 1, "single-layer trunk"
    i = 0

    rows = -(-(bsz + 1) // SUBLANES) * SUBLANES
    c_rows = jnp.pad(jnp.concatenate([c, c_ctx[None]], axis=0), ((0, rows - bsz - 1), (0, 0)))
    mod_all = _mod_call(c_rows, w_mod[i], b_mod[i])
    mod_all = jnp.pad(mod_all.reshape(rows, 6, d), ((0, 0), (0, SUBLANES - 6), (0, 0)))
    mod, modc = mod_all[:bsz], mod_all[bsz:bsz + 1]

    cuts = np.cumsum([Q_LORA, KV_LORA, QK_ROPE, 3 * HY_WIDTH])
    wi, bi = w_in[i], b_in[i][None]
    w_q, w_kv, w_pe, w_hy, w_g = jnp.split(wi, cuts, axis=1)
    b_q, b_kv, b_pe, b_hy, b_g = jnp.split(bi, cuts, axis=1)
    wa = jnp.concatenate([w_q, w_kv, _rope_slot(w_pe, False), _rope_slot(w_pe, True)], axis=1).astype(BF16)
    ba = jnp.concatenate([b_q, b_kv, _rope_slot(b_pe, False), _rope_slot(b_pe, True)], axis=1)
    wq3 = w_uq[i].reshape(Q_LORA, N_HEADS, QK_NOPE + QK_ROPE) * (ATTN_SCALE * math.log2(math.e))
    tail = ((0, 0), (0, 0), (0, HEAD_PAD - QK_NOPE))
    wuq = (jnp.pad(wq3[..., :QK_NOPE], tail) + _rope_slot(wq3[..., QK_NOPE:], False)).reshape(Q_LORA, -1).astype(BF16)
    wuqs = _rope_slot(wq3[..., QK_NOPE:], True).reshape(Q_LORA, -1).astype(BF16)
    wkv3 = w_ukv[i].reshape(KV_LORA, N_HEADS, QK_NOPE + V_HEAD)
    wuk = jnp.pad(wkv3[..., :QK_NOPE], tail).reshape(KV_LORA, -1).astype(BF16)
    wuvt = wkv3[..., QK_NOPE:].reshape(KV_LORA, -1).T.astype(BF16)
    nm, qn, kvn = norm_mix[i][None], q_norm[i][None], kv_norm[i][None]

    w_c = jnp.concatenate([w_kv, _rope_slot(w_pe, False)], axis=1).astype(BF16)
    b_c = jnp.concatenate([b_kv, _rope_slot(b_pe, False)], axis=1)
    ck, cvt = _ctx_call(ctx, modc, nm, w_c, b_c, kvn, wuk, wuvt)

    cos_t, sin_t = _rope_tables(s)
    q, k, vt, hv, hx1, hx2, gate = _inproj_call(
        x, mod, nm, wa, ba, w_hy.astype(BF16), b_hy, w_g.astype(BF16), b_g, qn, wuq, wuqs, kvn, wuk, wuvt,
        cos_t, sin_t, hy_conv_w[i], hy_conv_b[i][None], tl["inproj"])

    attn = _attn_call(q, jnp.concatenate([ck, k], axis=2), jnp.concatenate([cvt, vt], axis=3), tl["tq"], tl["tk"])
    hy = _hyena(hv, hx1, hx2, hy_filt_w1[i], hy_filt_b1[i], hy_filt_w2[i], hy_filt_b2[i], hy_filt_w3[i],
                hy_filt_freq[i], hy_skip[i], tl["fft_kb"])
    xm = _merge_call(x, attn, hy, gate, mod, w_branch_attn[i].astype(BF16), w_branch_hyena[i].astype(BF16),
                     w_out[i].astype(BF16), tl["merge"])
    return _moe(xm, mod, norm_ffn[i][None], w_router[i], router_bias[i], w_exp_gate[i], w_exp_up[i], w_exp_down[i],
                w_sh_gate[i], w_sh_up[i], w_sh_down[i], final_norm[None], tl["moe"])
```

```python
import functools
import math

import numpy as np
import jax
import jax.numpy as jnp
from jax import lax
from jax.experimental import pallas as pl
from jax.experimental.pallas import tpu as pltpu

GRID_W = 64
N_HEADS = 8
QK_NOPE = 64
QK_ROPE = 32
V_HEAD = 64
Q_LORA = 256
KV_LORA = 128
ROPE_THETA = 10000.0
ATTN_SCALE = 1.0 / math.sqrt(QK_NOPE + QK_ROPE)
HY_WIDTH = 512
HY_ORDER = 2
HY_SHORT = 3
HY_BANDS = 8
HY_EMB = 1 + 2 * HY_BANDS
HY_EMB_PAD = 32
HY_FAST_DECAY = 0.3
HY_SLOW_DECAY = 1.5
HY_DECAY_TARGET = 1e-2
N_EXPERTS = 64
N_GROUPS = 8
GROUP_SIZE = N_EXPERTS // N_GROUPS
TOPK_GROUPS = 4
TOP_K = 8
EXPERT_FF = 256
ROUTE_SCALE = 2.5
EXPERT_BLOCK = 1024
RUN_ALIGN = 8
ROUTE_TILES = 4
LONG_RUN = 8
NORM_EPS = 1e-6

HEAD_PAD = 128
Q_CHUNK = 512
AHEAD = 2
LANES = 128
SUBLANES = 8
VMEM_LIMIT = 48 * 1024 * 1024

F32 = jnp.float32
BF16 = jnp.bfloat16
NT_DIMS = (((1,), (1,)), ((), ()))
NN_DIMS = (((1,), (0,)), ((), ()))


def _params(*sem):
    return pltpu.CompilerParams(dimension_semantics=sem, vmem_limit_bytes=VMEM_LIMIT)


def _dot(a, b):
    return jnp.dot(a.astype(BF16), b.astype(BF16), preferred_element_type=F32)


def _split(a):
    hi = a.astype(BF16)
    lo = (a - hi.astype(F32)).astype(BF16)
    return hi, lo


def _dot3(a, b, dims=NN_DIMS):
    ah, al = _split(a)
    bh, bl = _split(b)
    d = functools.partial(lax.dot_general, dimension_numbers=dims, preferred_element_type=F32)
    return d(ah, bh) + (d(ah, bl) + d(al, bh))


def _rms(x, g):
    return x * lax.rsqrt(jnp.mean(x * x, axis=-1, keepdims=True) + NORM_EPS) * g


def _silu(x):
    return x * jax.nn.sigmoid(x)


def _full(shape):
    nd = len(shape)
    return pl.BlockSpec(shape, lambda *_: (0,) * nd)


def _mod_kernel(c_ref, w_ref, b_ref, o_ref):
    o_ref[...] = _dot3(_silu(c_ref[...]), w_ref[...]) + b_ref[...]


def _mod_call(c_rows, w_mod, b_mod):
    r, d = c_rows.shape
    n = w_mod.shape[1]
    bn = 1024
    return pl.pallas_call(
        _mod_kernel,
        grid=(n // bn,),
        in_specs=[_full((r, d)), pl.BlockSpec((d, bn), lambda j: (0, j)), pl.BlockSpec((1, bn), lambda j: (0, j))],
        out_specs=pl.BlockSpec((r, bn), lambda j: (0, j)),
        out_shape=jax.ShapeDtypeStruct((r, n), F32),
        compiler_params=_params("arbitrary"),
        name="mod",
    )(c_rows, w_mod, b_mod.reshape(1, n))


def _prenorm(x, mod_ref, row, g):
    shift = mod_ref[0, row:row + 1, :]
    scale = mod_ref[0, row + 1:row + 2, :]
    return _rms(x, g) * (1.0 + scale) + shift


def _kv_heads(kv_lat, kpe, kvn_ref, wuk_ref, wuvt_ref, k_out, vt_out):
    kvn = _rms(kv_lat, kvn_ref[...]).astype(BF16)
    kk = _dot(kvn, wuk_ref[...])
    vt = lax.dot_general(wuvt_ref[...], kvn, NT_DIMS, preferred_element_type=F32)
    ones = jnp.ones((HEAD_PAD - V_HEAD, vt.shape[1]), F32)
    for h in range(N_HEADS):
        k_out[0, h] = (kk[:, HEAD_PAD * h:HEAD_PAD * (h + 1)] + kpe).astype(BF16)
        vt_out[0, h] = jnp.concatenate([vt[V_HEAD * h:V_HEAD * (h + 1)], ones], axis=0).astype(BF16)


def _ctx_kernel(c_ref, mod_ref, nm_ref, w_ref, b_ref, kvn_ref, wuk_ref, wuv_ref, k_out, v_out):
    h = _prenorm(c_ref[0], mod_ref, 0, nm_ref[...]).astype(BF16)
    a = _dot(h, w_ref[...]) + b_ref[...]
    _kv_heads(a[:, :KV_LORA], a[:, KV_LORA:], kvn_ref, wuk_ref, wuv_ref, k_out, v_out)


def _ctx_call(ctx, modc, norm_mix, w_c, b_c, kv_norm, w_uk, w_uv):
    bsz, n, d = ctx.shape
    return pl.pallas_call(
        _ctx_kernel,
        grid=(bsz,),
        in_specs=[pl.BlockSpec((1, n, d), lambda b: (b, 0, 0)), _full(modc.shape), _full(norm_mix.shape),
                  _full(w_c.shape), _full(b_c.shape), _full(kv_norm.shape), _full(w_uk.shape), _full(w_uv.shape)],
        out_specs=[pl.BlockSpec((1, N_HEADS, n, HEAD_PAD), lambda b: (b, 0, 0, 0)),
                   pl.BlockSpec((1, N_HEADS, HEAD_PAD, n), lambda b: (b, 0, 0, 0))],
        out_shape=[jax.ShapeDtypeStruct((bsz, N_HEADS, n, HEAD_PAD), BF16),
                   jax.ShapeDtypeStruct((bsz, N_HEADS, HEAD_PAD, n), BF16)],
        compiler_params=_params("arbitrary"),
        name="ctx",
    )(ctx, modc, norm_mix, w_c, b_c, kv_norm, w_uk, w_uv)


def _inproj_kernel(x_ref, xp_ref, xn_ref, mod_ref, nm_ref, wa_ref, ba_ref, why_ref, bhy_ref, wg_ref, bg_ref,
                   qn_ref, wuq_ref, wuqs_ref, kvn_ref, wuk_ref, wuv_ref, cos_ref, sin_ref, cw_ref, cb_ref,
                   q_out, k_out, v_out, hv_out, hx1_out, hx2_out, g_out):
    i = pl.program_id(0)
    tm = x_ref.shape[1]
    nm = nm_ref[...]
    h = _prenorm(x_ref[0], mod_ref, 0, nm).astype(BF16)
    a = _dot(h, wa_ref[...]) + ba_ref[...]
    q_lat = a[:, :Q_LORA]
    kv_lat = a[:, Q_LORA:Q_LORA + KV_LORA]
    kpe_m = a[:, Q_LORA + KV_LORA:Q_LORA + KV_LORA + HEAD_PAD]
    kpe_s = a[:, Q_LORA + KV_LORA + HEAD_PAD:]
    cos = cos_ref[...]
    sin = sin_ref[...]
    qn = _rms(q_lat, qn_ref[...]).astype(BF16)
    qa = _dot(qn, wuq_ref[...])
    qs = _dot(qn, wuqs_ref[...])
    for hh in range(N_HEADS):
        sl = slice(HEAD_PAD * hh, HEAD_PAD * (hh + 1))
        q_out[0, hh] = (qa[:, sl] * cos + qs[:, sl] * sin).astype(BF16)
    _kv_heads(kv_lat, kpe_m * cos + kpe_s * sin, kvn_ref, wuk_ref, wuv_ref, k_out, v_out)
    g_out[0] = (_dot(h, wg_ref[...]) + bg_ref[...]).astype(BF16)

    why = why_ref[...]
    bhy = bhy_ref[...]
    halo = jnp.concatenate([_prenorm(xp_ref[0], mod_ref, 0, nm), _prenorm(xn_ref[0], mod_ref, 0, nm)], axis=0)
    hy_all = _dot(jnp.concatenate([h, halo.astype(BF16)], axis=0), why) + bhy
    hy = hy_all[:tm]
    prev = jnp.where(i == 0, 0.0, hy_all[tm + SUBLANES - 1:tm + SUBLANES])
    nxt = jnp.where(i == pl.num_programs(0) - 1, 0.0, hy_all[tm + SUBLANES:tm + SUBLANES + 1])
    rid = lax.broadcasted_iota(jnp.int32, (tm, 1), 0)
    up = jnp.where(rid == 0, prev, pltpu.roll(hy, 1, 0))
    dn = jnp.where(rid == tm - 1, nxt, pltpu.roll(hy, tm - 1, 0))
    u = up * cw_ref[0:1, :] + hy * cw_ref[1:2, :] + dn * cw_ref[2:3, :] + cb_ref[...]
    hv_out[0] = u[:, :HY_WIDTH]
    hx1_out[0] = u[:, HY_WIDTH:2 * HY_WIDTH]
    hx2_out[0] = u[:, 2 * HY_WIDTH:]


def _inproj_call(x, mod, norm_mix, wa, ba, why, bhy, wg, bg, q_norm, wuq, wuqs, kv_norm, wuk, wuvt, cos_t, sin_t, cw,
                 cb, tm):
    bsz, s, d = x.shape
    nt = s // tm
    rb = tm // SUBLANES
    last_rb = s // SUBLANES - 1
    consts = [norm_mix, wa, ba, why, bhy, wg, bg, q_norm, wuq, wuqs, kv_norm, wuk, wuvt]
    in_specs = [
        pl.BlockSpec((1, tm, d), lambda i, b: (b, i, 0)),
        pl.BlockSpec((1, SUBLANES, d), lambda i, b: (b, jnp.maximum(i * rb - 1, 0), 0)),
        pl.BlockSpec((1, SUBLANES, d), lambda i, b: (b, jnp.minimum((i + 1) * rb, last_rb), 0)),
        pl.BlockSpec((1, SUBLANES, d), lambda i, b: (b, 0, 0)),
    ] + [_full(c.shape) for c in consts] + [
        pl.BlockSpec((tm, HEAD_PAD), lambda i, b: (i, 0)),
        pl.BlockSpec((tm, HEAD_PAD), lambda i, b: (i, 0)),
        _full(cw.shape), _full(cb.shape),
    ]
    hw = HY_WIDTH
    out_specs = [
        pl.BlockSpec((1, N_HEADS, tm, HEAD_PAD), lambda i, b: (b, 0, i, 0)),
        pl.BlockSpec((1, N_HEADS, tm, HEAD_PAD), lambda i, b: (b, 0, i, 0)),
        pl.BlockSpec((1, N_HEADS, HEAD_PAD, tm), lambda i, b: (b, 0, 0, i)),
        pl.BlockSpec((1, tm, hw), lambda i, b: (b, i, 0)),
        pl.BlockSpec((1, tm, hw), lambda i, b: (b, i, 0)),
        pl.BlockSpec((1, tm, hw), lambda i, b: (b, i, 0)),
        pl.BlockSpec((1, tm, 2 * d), lambda i, b: (b, i, 0)),
    ]
    out_shape = [
        jax.ShapeDtypeStruct((bsz, N_HEADS, s, HEAD_PAD), BF16),
        jax.ShapeDtypeStruct((bsz, N_HEADS, s, HEAD_PAD), BF16),
        jax.ShapeDtypeStruct((bsz, N_HEADS, HEAD_PAD, s), BF16),
        jax.ShapeDtypeStruct((bsz, s, hw), F32),
        jax.ShapeDtypeStruct((bsz, s, hw), F32),
        jax.ShapeDtypeStruct((bsz, s, hw), F32),
        jax.ShapeDtypeStruct((bsz, s, 2 * d), BF16),
    ]
    return pl.pallas_call(
        _inproj_kernel,
        grid=(nt, bsz),
        in_specs=in_specs,
        out_specs=out_specs,
        out_shape=out_shape,
        compiler_params=_params("arbitrary", "arbitrary"),
        name="inproj",
    )(x, x, x, mod, *consts, cos_t, sin_t, cw, cb)


def _attn_kernel(q_ref, k_ref, vt_ref, o_ref, m_sc, acc_sc):
    j = pl.program_id(2)

    @pl.when(j == 0)
    def _():
        m_sc[...] = jnp.full(m_sc.shape, -jnp.inf, F32)
        acc_sc[...] = jnp.zeros(acc_sc.shape, F32)

    tq = q_ref.shape[2]
    qw = min(tq, Q_CHUNK)
    units = [(h, c) for h in range(N_HEADS) for c in range(0, tq, qw)]

    def scores(u):
        h, c = units[u]
        return lax.dot_general(k_ref[0, h], q_ref[0, h, c:c + qw, :], NT_DIMS,
                               preferred_element_type=F32)

    pending = [scores(u) for u in range(AHEAD)]
    for u, (h, c) in enumerate(units):
        if u + AHEAD < len(units):
            pending.append(scores(u + AHEAD))
        st = pending.pop(0)
        m_prev = m_sc[h, :, c:c + qw]
        m_new = jnp.maximum(m_prev, jnp.max(st, axis=0, keepdims=True))
        pt = jnp.exp2(st - m_new).astype(BF16)
        acc_sc[h, :, c:c + qw] = (jnp.exp2(m_prev - m_new) * acc_sc[h, :, c:c + qw]
                                  + jnp.dot(vt_ref[0, h], pt, preferred_element_type=F32))
        m_sc[h, :, c:c + qw] = m_new

    @pl.when(j == pl.num_programs(2) - 1)
    def _():
        ot = jnp.concatenate([acc_sc[h, :V_HEAD] / acc_sc[h, V_HEAD:V_HEAD + 1] for h in range(N_HEADS)], axis=0)
        o_ref[0] = ot.T.astype(o_ref.dtype)


def _attn_call(q, k, vt, tq, tk):
    bsz, nh, s, dh = q.shape
    nk = k.shape[2]
    dv = nh * V_HEAD
    return pl.pallas_call(
        _attn_kernel,
        grid=(bsz, s // tq, nk // tk),
        in_specs=[
            pl.BlockSpec((1, nh, tq, dh), lambda b, i, j: (b, 0, i, 0)),
            pl.BlockSpec((1, nh, tk, dh), lambda b, i, j: (b, 0, j, 0)),
            pl.BlockSpec((1, nh, dh, tk), lambda b, i, j: (b, 0, 0, j)),
        ],
        out_specs=pl.BlockSpec((1, tq, dv), lambda b, i, j: (b, i, 0)),
        out_shape=jax.ShapeDtypeStruct((bsz, s, dv), BF16),
        scratch_shapes=[pltpu.VMEM((nh, 1, tq), F32), pltpu.VMEM((nh, dh, tq), F32)],
        compiler_params=_params("arbitrary", "arbitrary", "arbitrary"),
        name="attn",
    )(q, k, vt)


def _filter_kernel(emb_ref, w1_ref, b1_ref, w2_ref, b2_ref, w3_ref, fr_ref, dl_ref, full_out, asum_out, *, seq):
    r = pl.program_id(0)
    rb = emb_ref.shape[0]
    emb = emb_ref[...]
    fr = fr_ref[...]
    h = jnp.sin(fr * (_dot3(emb, w1_ref[...]) + b1_ref[...]))
    h = jnp.sin(fr * (_dot3(h, w2_ref[...]) + b2_ref[...]))
    k = _dot3(h, w3_ref[0]) * jnp.exp(-emb[:, 0:1] * dl_ref[...])
    row = r * rb + lax.broadcasted_iota(jnp.int32, (rb, 1), 0)
    k = jnp.where(row == seq, 0.0, k)
    full_out[...] = k

    @pl.when(r == 0)
    def _():
        asum_out[...] = jnp.zeros(asum_out.shape, F32)

    asum_out[...] += jnp.sum(jnp.abs(k), axis=0, keepdims=True)


def _filter_call(emb, w1, b1, w2, b2, w3sel, freq, deltas2, seq, rb):
    n2 = emb.shape[0]
    half_blocks = seq // rb
    width = w3sel.shape[2]
    return pl.pallas_call(
        functools.partial(_filter_kernel, seq=seq),
        grid=(n2 // rb,),
        in_specs=[pl.BlockSpec((rb, HY_EMB_PAD), lambda r: (r, 0)), _full(w1.shape), _full(b1.shape),
                  _full(w2.shape), _full(b2.shape),
                  pl.BlockSpec((1,) + w3sel.shape[1:], lambda r: (r // half_blocks, 0, 0)),
                  _full(freq.shape), _full(deltas2.shape)],
        out_specs=[pl.BlockSpec((rb, width), lambda r: (r, 0)), pl.BlockSpec((1, width), lambda r: (0, 0))],
        out_shape=[jax.ShapeDtypeStruct((n2, width), F32), jax.ShapeDtypeStruct((1, width), F32)],
        compiler_params=_params("arbitrary"),
        name="filt",
    )(emb, w1, b1, w2, b2, w3sel, freq, deltas2)


def _fa_kernel(u_ref, f_ref, a_out):
    two, _, hn, g, c = u_ref.shape
    a = _dot(f_ref[...], u_ref[...].reshape(two * hn * g, c))
    a_out[...] = a.reshape(a_out.shape)


def _fa_call(u5, fmat):
    _, p, hn, n, c = u5.shape
    g = SUBLANES
    return pl.pallas_call(
        _fa_kernel,
        grid=(p, n // g),
        in_specs=[pl.BlockSpec((2, 1, hn, g, c), lambda q, j: (0, q, 0, j, 0)), _full(fmat.shape)],
        out_specs=pl.BlockSpec((1, 2, n, g, c), lambda q, j: (q, 0, 0, j, 0)),
        out_shape=jax.ShapeDtypeStruct((p, 2, n, n, c), F32),
        compiler_params=_params("arbitrary", "arbitrary"),
        name="fa",
    )(u5, fmat)


def _fb_kernel(a_ref, g_ref, asum_ref, kf_out):
    two, _, n, c = a_ref.shape[1:]
    a = a_ref[0].reshape(two * n, c)
    x = _dot(g_ref[0], a) / (asum_ref[...] + 1e-6)
    kf_out[0] = x.reshape(two, n, c)


def _fb_call(a5, gmat, asum):
    _, _, n, _, c = a5.shape
    return pl.pallas_call(
        _fb_kernel,
        grid=(n,),
        in_specs=[pl.BlockSpec((1, 2, 1, n, c), lambda k: (0, 0, k, 0, 0)),
                  pl.BlockSpec((1, 2 * n, 2 * n), lambda k: (k, 0, 0)), _full(asum.shape)],
        out_specs=pl.BlockSpec((1, 2, n, c), lambda k: (k, 0, 0, 0)),
        out_shape=jax.ShapeDtypeStruct((n, 2, n, c), F32),
        compiler_params=_params("arbitrary"),
        name="fb",
    )(a5, gmat, asum)


def _mid_kernel(a_ref, g_ref, h_ref, kf_ref, b_out):
    _, two, kb, n, c = a_ref.shape
    for kk in range(kb):
        x = _dot(g_ref[kk], a_ref[0, :, kk].reshape(two * n, c))
        xr, xi = x[:n], x[n:]
        kr, ki = kf_ref[kk, 0], kf_ref[kk, 1]
        y = jnp.concatenate([xr * kr - xi * ki, xr * ki + xi * kr], axis=0)
        b_out[0, :, kk] = _dot(h_ref[kk], y).reshape(two, n, c)


def _mid_call(a5, gmat, hmat, kf, order, kb):
    p, _, n, _, c = a5.shape
    return pl.pallas_call(
        _mid_kernel,
        grid=(n // kb, p),
        in_specs=[pl.BlockSpec((1, 2, kb, n, c), lambda k, q: (q, 0, k, 0, 0)),
                  pl.BlockSpec((kb, 2 * n, 2 * n), lambda k, q: (k, 0, 0)),
                  pl.BlockSpec((kb, 2 * n, 2 * n), lambda k, q: (k, 0, 0)),
                  pl.BlockSpec((kb, 2, n, c), lambda k, q: (k, 0, 0, order))],
        out_specs=pl.BlockSpec((1, 2, kb, n, c), lambda k, q: (q, 0, k, 0, 0)),
        out_shape=jax.ShapeDtypeStruct(a5.shape, F32),
        compiler_params=_params("arbitrary", "arbitrary"),
        name="mid",
    )(a5, gmat, hmat, kf)


def _fc_kernel(b_ref, f_ref, u_ref, m_ref, skip_ref, o_out):
    _, two, n, g, c = b_ref.shape
    y = _dot(f_ref[...], b_ref[...].reshape(two * n * g, c)).reshape(u_ref.shape)
    o_out[...] = m_ref[...] * (y + u_ref[...] * skip_ref[...])


def _fc_call(b5, finv, u5, m5, skip_row):
    _, p, hn, n, c = u5.shape
    g = SUBLANES
    blk = pl.BlockSpec((2, 1, hn, g, c), lambda q, j: (0, q, 0, j, 0))
    return pl.pallas_call(
        _fc_kernel,
        grid=(p, n // g),
        in_specs=[pl.BlockSpec((1, 2, n, g, c), lambda q, j: (q, 0, 0, j, 0)), _full(finv.shape), blk, blk,
                  _full(skip_row.shape)],
        out_specs=blk,
        out_shape=jax.ShapeDtypeStruct(u5.shape, F32),
        compiler_params=_params("arbitrary", "arbitrary"),
        name="fc",
    )(b5, finv, u5, m5, skip_row)


def _dft_tables(n):
    hn = n // 2
    k = np.arange(n)[:, None]
    ang = -2.0 * np.pi * (k * np.arange(n)[None, :] % n) / n
    fr, fi = np.cos(ang), np.sin(ang)
    f_data = np.block([[fr[:, :hn], -fi[:, :hn]], [fi[:, :hn], fr[:, :hn]]])
    f_filt = np.concatenate([fr, fi], axis=0)
    er, ei = fr[:hn], -fi[:hn]
    f_inv = np.block([[er, -ei], [ei, er]]) / float(n * n)
    k1 = jnp.arange(n, dtype=jnp.int32)[:, None, None]
    k2 = jnp.arange(n, dtype=jnp.int32)[None, :, None]
    m2 = jnp.arange(n, dtype=jnp.int32)[None, None, :]
    idx = (m2 * (k1 + n * k2)) % (n * n)
    ang2 = idx.astype(F32) * (-2.0 * math.pi / (n * n))
    gr, gi = jnp.cos(ang2), jnp.sin(ang2)
    g = jnp.concatenate([jnp.concatenate([gr, -gi], axis=2), jnp.concatenate([gi, gr], axis=2)], axis=1)
    h = jnp.swapaxes(g, 1, 2)

    def widen(f):
        return jnp.asarray(np.kron(f, np.eye(SUBLANES)), BF16)

    return widen(f_data), widen(f_filt), widen(f_inv), g.astype(BF16), h.astype(BF16)


def _hyena_filter_tables(seq):
    t = jnp.linspace(0.0, 1.0, seq, dtype=F32)[:, None]
    w = 2.0 * math.pi * jnp.arange(seq, dtype=F32)[:, None] / seq
    f = jnp.linspace(1e-4, HY_BANDS - 1, HY_BANDS, dtype=F32)[None, :]
    emb = jnp.concatenate([t, jnp.cos(f * w), -jnp.sin(f * w)], axis=-1)
    emb = jnp.concatenate([emb, emb[:1], emb[:0:-1]], axis=0)
    emb = jnp.pad(emb, ((0, 0), (0, HY_EMB_PAD - HY_EMB)))
    deltas = jnp.abs(jnp.linspace(math.log(HY_DECAY_TARGET) / HY_SLOW_DECAY,
                                  math.log(HY_DECAY_TARGET) / HY_FAST_DECAY, HY_WIDTH, dtype=F32))
    return emb, jnp.tile(deltas, HY_ORDER)[None, :]


def _hyena(hv, hx1, hx2, w1, b1, w2, b2, w3, freq, skip, kb):
    bsz, seq, c = hv.shape
    n = int(round(math.sqrt(2 * seq)))
    assert n * n == 2 * seq and bsz % 2 == 0
    hn, p = n // 2, bsz // 2
    f_data, f_filt, f_inv, gmat, hmat = _dft_tables(n)

    emb, deltas2 = _hyena_filter_tables(seq)
    w1p = jnp.pad(w1, ((0, HY_EMB_PAD - HY_EMB), (0, 0)))
    w3r = w3.reshape(w3.shape[0], HY_ORDER, 2, c)
    w3sel = jnp.stack([w3r[:, :, 0, :].reshape(-1, HY_ORDER * c), w3r[:, :, 1, :].reshape(-1, HY_ORDER * c)])
    full, asum = _filter_call(emb, w1p, b1[None], w2, b2[None], w3sel, freq[None], deltas2, seq, min(512, seq))
    c2 = HY_ORDER * c
    kf = _fb_call(_fa_call(full.reshape(2, 1, hn, n, c2), f_filt), gmat, asum)

    def view(t):
        return t.reshape(2, p, hn, n, c)

    def long_conv(u5, m5, order):
        bm = _mid_call(_fa_call(u5, f_data), gmat, hmat, kf, order, kb)
        return _fc_call(bm, f_inv, u5, m5, skip[order][None, :])

    z = long_conv(view(hv), view(hx1), 0)
    return long_conv(z, view(hx2), 1).reshape(bsz, seq, c)


def _merge_kernel(x_ref, at_ref, hy_ref, g_ref, mod_ref, wba_ref, wbh_ref, wo_ref, o_ref):
    d = x_ref.shape[2]
    g = g_ref[0].astype(F32)
    y = (jax.nn.sigmoid(g[:, :d]) * _dot(at_ref[0], wba_ref[...])
         + jax.nn.sigmoid(g[:, d:]) * _dot(hy_ref[0], wbh_ref[...]))
    o_ref[0] = x_ref[0] + mod_ref[0, 2:3, :] * _dot(y, wo_ref[...])


def _merge_call(x, attn, hy, gate, mod, wba, wbh, wo, tm):
    bsz, s, d = x.shape

    def tok(w):
        return pl.BlockSpec((1, tm, w), lambda b, i: (b, i, 0))

    return pl.pallas_call(
        _merge_kernel,
        grid=(bsz, s // tm),
        in_specs=[tok(d), tok(attn.shape[2]), tok(hy.shape[2]), tok(2 * d),
                  pl.BlockSpec((1, SUBLANES, d), lambda b, i: (b, 0, 0)),
                  _full(wba.shape), _full(wbh.shape), _full(wo.shape)],
        out_specs=tok(d),
        out_shape=jax.ShapeDtypeStruct((bsz, s, d), F32),
        compiler_params=_params("arbitrary", "arbitrary"),
        name="merge",
    )(x, attn, hy, gate, mod, wba, wbh, wo)


def _route_kernel(xm_ref, mod_ref, nf_ref, wrt_ref, rb_ref, tri_ref, lt_ref, h2_out, w_out, p_out, col_out, row_out):
    tm = xm_ref.shape[1]
    ng, gs = N_GROUPS, GROUP_SIZE

    h2 = _prenorm(xm_ref[0], mod_ref, 3, nf_ref[...])
    h2_out[0] = h2.astype(h2_out.dtype)
    scores = jax.nn.sigmoid(_dot3(wrt_ref[...], h2, NT_DIMS))
    sel = scores + rb_ref[...]
    slabs = [sel[ng * j:ng * (j + 1)] for j in range(gs)]

    top1 = jnp.full((ng, tm), -jnp.inf, F32)
    top2 = top1
    for x in slabs:
        top2 = jnp.maximum(top2, jnp.minimum(top1, x))
        top1 = jnp.maximum(top1, x)
    gscore = top1 + top2
    gid = lax.broadcasted_iota(jnp.int32, (ng, 1), 0)
    rank = jnp.zeros((ng, tm), jnp.int32)
    for g2 in range(ng):
        row = gscore[g2:g2 + 1]
        beats = (row > gscore) | ((row == gscore) & (g2 < gid))
        rank = rank + beats.astype(jnp.int32)
    gmask = rank < TOPK_GROUPS

    cand = [jnp.where(gmask, x, -jnp.inf) for x in slabs]
    eid = [gid * gs + j for j in range(gs)]
    chosen = []
    for _ in range(TOP_K):
        best = functools.reduce(jnp.maximum, cand)
        best = jnp.max(best, axis=0, keepdims=True)
        idx = functools.reduce(jnp.minimum, [jnp.where(cand[j] == best, eid[j], N_EXPERTS) for j in range(gs)])
        idx = jnp.min(idx, axis=0, keepdims=True)
        chosen.append(idx)
        cand = [jnp.where(eid[j] == idx, -jnp.inf, cand[j]) for j in range(gs)]

    mask = [functools.reduce(jnp.logical_or, [eid[j] == idx for idx in chosen]) for j in range(gs)]
    maskb = jnp.concatenate(mask, axis=0)
    wsel = jnp.where(maskb, scores, 0.0)
    w_out[...] = wsel / jnp.sum(wsel, axis=0, keepdims=True) * ROUTE_SCALE

    def extents(cnt, lower_sum):
        units = jnp.floor((cnt + (RUN_ALIGN - 1)) * (1.0 / RUN_ALIGN))
        start = RUN_ALIGN * lower_sum(units.astype(BF16))
        return start, start + RUN_ALIGN * units

    lane = lax.broadcasted_iota(jnp.int32, (N_EXPERTS, LANES), 1)
    sub = lax.broadcasted_iota(jnp.int32, (SUBLANES, N_EXPERTS), 0)
    ts = tri_ref.shape[0]
    for c in range(tm // ts):
        mb = maskb[:, c * ts:(c + 1) * ts]
        maskf = jnp.where(mb, 1.0, 0.0)
        mask16 = maskf.astype(BF16)
        before = jnp.dot(mask16, tri_ref[...], preferred_element_type=F32)
        p_out[:, c * ts:(c + 1) * ts] = jnp.where(mb, before, -1.0).astype(p_out.dtype)
        cnt_c = jnp.sum(maskf, axis=1, keepdims=True)
        start_c, end_c = extents(jnp.broadcast_to(cnt_c, (N_EXPERTS, LANES)),
                                 lambda u: jnp.dot(lt_ref[...], u, preferred_element_type=F32))
        col_out[c] = jnp.where(lane == 0, cnt_c, jnp.where(lane == 1, start_c, end_c))
        cnt_r = lax.dot_general(jnp.ones((SUBLANES, ts), BF16), mask16, NT_DIMS, preferred_element_type=F32)
        start_r, end_r = extents(cnt_r,
                                 lambda u: lax.dot_general(u, lt_ref[...], NT_DIMS, preferred_element_type=F32))
        row_out[c] = jnp.where(sub == 0, start_r, end_r)


def _route_call(xm, mod, norm_ffn, wrt, rbias, lower, ts, tiles_per_step):
    bsz, s, d = xm.shape
    t = bsz * s
    tm = ts * tiles_per_step
    nt = s // tm
    tri = (jnp.arange(ts)[:, None] < jnp.arange(ts)[None, :]).astype(BF16)
    tok = pl.BlockSpec((N_EXPERTS, tm), lambda i: (0, i))
    return pl.pallas_call(
        _route_kernel,
        grid=(t // tm,),
        in_specs=[pl.BlockSpec((1, tm, d), lambda i: (i // nt, i % nt, 0)),
                  pl.BlockSpec((1, SUBLANES, d), lambda i: (i // nt, 0, 0)),
                  _full(norm_ffn.shape), _full(wrt.shape), _full(rbias.shape), _full(tri.shape),
                  _full(lower.shape)],
        out_specs=[pl.BlockSpec((1, tm, d), lambda i: (i // nt, i % nt, 0)), tok, tok,
                   pl.BlockSpec((tiles_per_step, N_EXPERTS, LANES), lambda i: (i, 0, 0)),
                   pl.BlockSpec((tiles_per_step, SUBLANES, N_EXPERTS), lambda i: (i, 0, 0))],
        out_shape=[jax.ShapeDtypeStruct((bsz, s, d), BF16), jax.ShapeDtypeStruct((N_EXPERTS, t), F32),
                   jax.ShapeDtypeStruct((N_EXPERTS, t), BF16),
                   jax.ShapeDtypeStruct((t // ts, N_EXPERTS, LANES), F32),
                   jax.ShapeDtypeStruct((t // ts, SUBLANES, N_EXPERTS), F32)],
        compiler_params=_params("arbitrary"),
        name="route",
    )(xm, mod, norm_ffn, wrt, rbias, tri, lower)


def _pack(x):
    w = x.shape[1] // 2
    lo = lax.bitcast_convert_type(x[:, :w].astype(BF16).astype(F32), jnp.uint32)
    hi = lax.bitcast_convert_type(x[:, w:].astype(BF16).astype(F32), jnp.uint32)
    return hi | (lo >> 16)


def _unpack(u):
    lo = lax.bitcast_convert_type(u << 16, F32).astype(BF16)
    hi = lax.bitcast_convert_type(u & jnp.uint32(0xFFFF0000), F32).astype(BF16)
    return lo, hi


def _pow2_pieces(units, limit):
    bit = 1
    while bit * 2 <= limit:
        bit *= 2
    while bit:
        yield (units & bit) != 0, units & ~(2 * bit - 1), bit
        bit //= 2


def _rows_copy(vm_ref, hbm_ref, sem, vm_row, hbm_row, rows, to_hbm):
    v = vm_ref.at[pl.ds(pl.multiple_of(vm_row, RUN_ALIGN), rows), :]
    h = hbm_ref.at[pl.ds(pl.multiple_of(hbm_row, RUN_ALIGN), rows), :]
    return pltpu.make_async_copy(v, h, sem) if to_hbm else pltpu.make_async_copy(h, v, sem)


def _run_copies(vm_ref, hbm_ref, sem, n8, vm_row, hbm_row, limit, to_hbm, act):
    def emit(pieces):
        for on, off, size in pieces:
            @pl.when(on)
            def _():
                act(_rows_copy(vm_ref, hbm_ref, sem, vm_row + RUN_ALIGN * off, hbm_row + RUN_ALIGN * off,
                               RUN_ALIGN * size, to_hbm))

    pieces = list(_pow2_pieces(n8, limit))
    long_pieces = [p for p in pieces if p[2] >= LONG_RUN]
    if long_pieces:
        pl.when(n8 >= LONG_RUN)(lambda: emit(long_pieces))
    emit([p for p in pieces if p[2] < LONG_RUN])


def _wait_rows(vm_ref, hbm_ref, sem, units, limit, to_hbm):
    for on, _, size in _pow2_pieces(units, limit):
        @pl.when(on)
        def _():
            _rows_copy(vm_ref, hbm_ref, sem, 0, 0, RUN_ALIGN * size, to_hbm).wait()


def _dispatch_kernel(n8_ref, ls_ref, gs_ref, ts_ref, t8_ref, nu_ref, pos_ref, ext_ref, h_ref, xs_out, srt2, zbuf,
                     sems):
    step = pl.program_id(0)
    tm = h_ref.shape[0]
    rows = srt2.shape[1]
    slot = step % 2
    srt, sem = srt2.at[slot], sems.at[slot]
    rid = lax.broadcasted_iota(jnp.int32, (rows, 1), 0).astype(F32)
    start = ext_ref[0, 0:1, :]
    member = jnp.where((rid >= start) & (rid < ext_ref[0, 1:2, :]), 1.0, 0.0)
    offset = rid - jnp.sum(member * start, axis=1, keepdims=True)
    pos = jnp.dot(member.astype(BF16), pos_ref[...], preferred_element_type=F32)
    sel = jnp.where(pos == offset, 1.0, 0.0).astype(BF16)
    srt[...] = _pack(jnp.dot(sel, h_ref[...], preferred_element_type=F32))

    def send(e, c):
        i = step * N_EXPERTS + e
        _run_copies(srt, xs_out, sem, n8_ref[i], ls_ref[i], gs_ref[i], tm // RUN_ALIGN, True, lambda cp: cp.start())
        return c

    lax.fori_loop(0, N_EXPERTS, send, 0)

    def wait_tile(tile, s):
        last = tile * N_EXPERTS + N_EXPERTS - 1
        _wait_rows(srt2.at[s], xs_out, sems.at[s], ls_ref[last] // RUN_ALIGN + n8_ref[last], rows // RUN_ALIGN, True)

    pl.when(step > 0)(lambda: wait_tile(step - 1, 1 - slot))

    @pl.when(step == pl.num_programs(0) - 1)
    def _():
        wait_tile(step, slot)
        zbuf[...] = jnp.zeros(zbuf.shape, zbuf.dtype)
        nblk = xs_out.shape[0] // EXPERT_BLOCK

        def fill(act):
            def tails(e, c):
                _run_copies(zbuf, xs_out, sem, t8_ref[e], 0, ts_ref[e], EXPERT_BLOCK // RUN_ALIGN - 1, True, act)
                return c

            def blocks(b, c):
                act(pltpu.make_async_copy(
                    zbuf, xs_out.at[pl.ds(pl.multiple_of(b * EXPERT_BLOCK, EXPERT_BLOCK), EXPERT_BLOCK), :], sem))
                return c

            lax.fori_loop(0, N_EXPERTS, tails, 0)
            lax.fori_loop(nu_ref[0], nblk, blocks, 0)

        fill(lambda cp: cp.start())
        fill(lambda cp: cp.wait())


def _dispatch_call(tables, pos_et, ext_rows, h2, nblk, tm):
    t, d = h2.shape
    lrows = TOP_K * tm + N_EXPERTS * RUN_ALIGN
    return pl.pallas_call(
        _dispatch_kernel,
        grid_spec=pltpu.PrefetchScalarGridSpec(
            num_scalar_prefetch=len(tables), grid=(t // tm,),
            in_specs=[pl.BlockSpec((N_EXPERTS, tm), lambda i, *_: (0, i)),
                      pl.BlockSpec((1,) + ext_rows.shape[1:], lambda i, *_: (i, 0, 0)),
                      pl.BlockSpec((tm, d), lambda i, *_: (i, 0))],
            out_specs=pl.BlockSpec(memory_space=pl.ANY),
            scratch_shapes=[pltpu.VMEM((2, lrows, d // 2), jnp.uint32),
                            pltpu.VMEM((EXPERT_BLOCK, d // 2), jnp.uint32), pltpu.SemaphoreType.DMA((2,))]),
        out_shape=jax.ShapeDtypeStruct((nblk * EXPERT_BLOCK, d // 2), jnp.uint32),
        compiler_params=_params("arbitrary"),
        name="dispatch",
    )(*tables, pos_et, ext_rows, h2)


def _expert_kernel(blk_ref, nused_ref, x_ref, wgu_ref, wd_ref, y_ref):
    used = pl.program_id(0) < nused_ref[0]

    @pl.when(used)
    def _():
        lo, hi = _unpack(x_ref[...])
        half = lo.shape[1]
        gu = (jnp.dot(lo, wgu_ref[0, :half, :], preferred_element_type=F32)
              + jnp.dot(hi, wgu_ref[0, half:, :], preferred_element_type=F32))
        a = _silu(gu[:, :EXPERT_FF]) * gu[:, EXPERT_FF:]
        y_ref[...] = _pack(_dot(a, wd_ref[0]))

    @pl.when(jnp.logical_not(used))
    def _():
        y_ref[...] = jnp.zeros(y_ref.shape, y_ref.dtype)


def _expert_call(blk_e, nused, xs, wgu, wd):
    rows, d = xs.shape
    nblk = rows // EXPERT_BLOCK

    def row_map(i, blk, nu):
        return (jnp.minimum(i, nu[0] - 1), 0)

    return pl.pallas_call(
        _expert_kernel,
        grid_spec=pltpu.PrefetchScalarGridSpec(
            num_scalar_prefetch=2, grid=(nblk,),
            in_specs=[pl.BlockSpec((EXPERT_BLOCK, d), row_map),
                      pl.BlockSpec((1,) + wgu.shape[1:], lambda i, blk, nu: (blk[i], 0, 0)),
                      pl.BlockSpec((1,) + wd.shape[1:], lambda i, blk, nu: (blk[i], 0, 0))],
            out_specs=pl.BlockSpec((EXPERT_BLOCK, d), lambda i, blk, nu: (i, 0))),
        out_shape=jax.ShapeDtypeStruct((rows, d), jnp.uint32),
        compiler_params=_params("arbitrary"),
        name="expert",
    )(blk_e, nused, xs, wgu, wd)


def _combine_kernel(n8_ref, ls_ref, gs_ref, ys_hbm, pos_ref, w_ref, ext_ref, xm_ref, h_ref, mod_ref, wsgu_ref,
                    wsd_ref, fn_ref, o_ref, ybuf2, sems):
    step = pl.program_id(0)
    tm = xm_ref.shape[0]
    rows = ybuf2.shape[1]
    slot = step % 2
    ybuf, sem = ybuf2.at[slot], sems.at[slot]

    def fetch(tile, s):
        def body(e, c):
            i = tile * N_EXPERTS + e
            _run_copies(ybuf2.at[s], ys_hbm, sems.at[s], n8_ref[i], ls_ref[i], gs_ref[i], tm // RUN_ALIGN, False,
                        lambda cp: cp.start())
            return c
        lax.fori_loop(0, N_EXPERTS, body, 0)

    pl.when(step == 0)(lambda: fetch(step, slot))
    pl.when(step + 1 < pl.num_programs(0))(lambda: fetch(step + 1, 1 - slot))
    gu = _dot(h_ref[...], wsgu_ref[...])
    ff = gu.shape[1] // 2
    shared = _dot(_silu(gu[:, :ff]) * gu[:, ff:], wsd_ref[...])
    cid = lax.broadcasted_iota(jnp.int32, (1, rows), 1).astype(F32)
    start = ext_ref[0, :, 1:2]
    member = jnp.where((cid >= start) & (cid < ext_ref[0, :, 2:3]), 1.0, 0.0)
    offset = cid - jnp.sum(member * start, axis=0, keepdims=True)
    member = member.astype(BF16)
    pos = jnp.dot(pos_ref[...], member, preferred_element_type=F32)
    mix = jnp.where(pos == offset, jnp.dot(w_ref[...].astype(BF16), member, preferred_element_type=F32), 0.0)
    mix = mix.astype(BF16)
    last = step * N_EXPERTS + N_EXPERTS - 1
    filled = ls_ref[last] + RUN_ALIGN * n8_ref[last]
    _wait_rows(ybuf, ys_hbm, sem, filled // RUN_ALIGN, rows // RUN_ALIGN, False)
    rid = lax.broadcasted_iota(jnp.int32, (rows, 1), 0)
    lo, hi = _unpack(jnp.where(rid < filled, ybuf[...], jnp.uint32(0)))
    routed = jnp.concatenate([jnp.dot(mix, lo, preferred_element_type=F32),
                              jnp.dot(mix, hi, preferred_element_type=F32)], axis=1)
    x = xm_ref[...] + mod_ref[0, 5:6, :] * (routed + shared)
    o_ref[...] = _rms(x, fn_ref[...])


def _combine_call(tables, ys, pos_te, w_te, ext_cols, xm, h2, mod, wsgu, wsd, final_norm, tm, tiles_per_batch):
    t, d = xm.shape
    lrows = TOP_K * tm + N_EXPERTS * RUN_ALIGN
    tok = pl.BlockSpec((tm, d), lambda i, *_: (i, 0))
    per_e = pl.BlockSpec((tm, N_EXPERTS), lambda i, *_: (i, 0))
    return pl.pallas_call(
        _combine_kernel,
        grid_spec=pltpu.PrefetchScalarGridSpec(
            num_scalar_prefetch=len(tables), grid=(t // tm,),
            in_specs=[pl.BlockSpec(memory_space=pl.ANY), per_e, per_e,
                      pl.BlockSpec((1,) + ext_cols.shape[1:], lambda i, *_: (i, 0, 0)), tok, tok,
                      pl.BlockSpec((1, SUBLANES, d), lambda i, *_: (i // tiles_per_batch, 0, 0)),
                      _full(wsgu.shape), _full(wsd.shape), _full(final_norm.shape)],
            out_specs=tok,
            scratch_shapes=[pltpu.VMEM((2, lrows, d // 2), jnp.uint32), pltpu.SemaphoreType.DMA((2,))]),
        out_shape=jax.ShapeDtypeStruct((t, d), F32),
        compiler_params=_params("arbitrary"),
        name="combine",
    )(*tables, ys, pos_te, w_te, ext_cols, xm, h2, mod, wsgu, wsd, final_norm)


def _moe(xm, mod, norm_ffn, w_router, router_bias, wg, wu, wd, wsg, wsu, wsd, final_norm, tm):
    bsz, s, d = xm.shape
    t = bsz * s
    nt = t // tm
    perm = (np.arange(N_EXPERTS) % N_GROUPS) * GROUP_SIZE + np.arange(N_EXPERTS) // N_GROUPS
    wrt = w_router.T[perm]
    rbias = router_bias[perm][:, None]
    lower = jnp.asarray(perm[None, :] < perm[:, None], BF16)
    h2, w_et, pos_et, ext_cols, ext_rows = _route_call(xm, mod, norm_ffn, wrt, rbias, lower, tm, ROUTE_TILES)

    inv = np.argsort(perm)
    n8 = (ext_cols[:, :, 0].astype(jnp.int32)[:, inv] + (RUN_ALIGN - 1)) // RUN_ALIGN
    run = RUN_ALIGN * n8
    ls = jnp.cumsum(run, axis=1) - run
    tot = jnp.sum(run, axis=0)
    padded = (tot + EXPERT_BLOCK - 1) // EXPERT_BLOCK * EXPERT_BLOCK
    pad_end = jnp.cumsum(padded)
    gs = (pad_end - padded)[None, :] + jnp.cumsum(run, axis=0) - run
    nblk = -(-(t * TOP_K + nt * N_EXPERTS * (RUN_ALIGN - 1)) // EXPERT_BLOCK) + N_EXPERTS
    blk_first = jnp.arange(nblk, dtype=jnp.int32)[:, None] * EXPERT_BLOCK
    blk_e = jnp.minimum(jnp.sum((pad_end[None, :] <= blk_first).astype(jnp.int32), axis=1), N_EXPERTS - 1)
    nused = (pad_end[-1:] // EXPERT_BLOCK).astype(jnp.int32)
    tables = [a.reshape(-1).astype(jnp.int32) for a in (n8, ls, gs)]
    tails = [(pad_end - padded + tot).astype(jnp.int32), ((padded - tot) // RUN_ALIGN).astype(jnp.int32), nused]

    h2f = h2.reshape(t, d)
    xs = _dispatch_call(tables + tails, pos_et, ext_rows, h2f, nblk, tm)
    wgu = jnp.concatenate([wg, wu], axis=2).astype(BF16)
    ys = _expert_call(blk_e, nused, xs, wgu, wd.astype(BF16))
    wsgu = jnp.concatenate([wsg, wsu], axis=1).astype(BF16)
    out = _combine_call(tables, ys, pos_et.T, w_et.T, ext_cols, xm.reshape(t, d), h2f, mod, wsgu, wsd.astype(BF16),
                        final_norm, tm, s // tm)
    return out.reshape(bsz, s, d)


def _rope_tables(s):
    rows = s // GRID_W
    row = jnp.broadcast_to(jnp.arange(rows, dtype=F32)[:, None], (rows, GRID_W)).reshape(-1)
    col = jnp.broadcast_to(jnp.arange(GRID_W, dtype=F32)[None, :], (rows, GRID_W)).reshape(-1)
    half = QK_ROPE // 2
    inv_freq = ROPE_THETA ** (-jnp.arange(0, half, 2, dtype=F32) / half)
    ar, ac = row[:, None] * inv_freq, col[:, None] * inv_freq
    ones = jnp.ones((s, QK_NOPE), F32)
    tail = HEAD_PAD - QK_NOPE - QK_ROPE
    cos_t = jnp.concatenate([ones, jnp.cos(ar), jnp.cos(ar), jnp.cos(ac), jnp.cos(ac), jnp.ones((s, tail), F32)], 1)
    sin_t = jnp.concatenate([0 * ones, -jnp.sin(ar), jnp.sin(ar), -jnp.sin(ac), jnp.sin(ac),
                             jnp.zeros((s, tail), F32)], 1)
    return cos_t, sin_t


_Q4 = QK_ROPE // 4
ROPE_SWAP = np.concatenate([np.arange(_Q4, 2 * _Q4), np.arange(0, _Q4), np.arange(3 * _Q4, 4 * _Q4),
                            np.arange(2 * _Q4, 3 * _Q4)])


def _rope_slot(w, swap):
    if swap:
        w = w[..., ROPE_SWAP]
    pad = [(0, 0)] * (w.ndim - 1) + [(QK_NOPE, HEAD_PAD - QK_NOPE - QK_ROPE)]
    return jnp.pad(w, pad)


def kernel(x, c, ctx, c_ctx, w_mod, b_mod, norm_mix, norm_ffn, w_in, b_in, q_norm, w_uq, kv_norm, w_ukv, w_branch_attn, hy_conv_w, hy_conv_b, hy_filt_w1, hy_filt_b1, hy_filt_w2, hy_filt_b2, hy_filt_w3, hy_filt_freq, hy_skip, w_branch_hyena, w_out, w_router, router_bias, w_exp_gate, w_exp_up, w_exp_down, w_sh_gate, w_sh_up, w_sh_down, final_norm,
           tiles=None):
    bsz, s, d = x.shape
    tl = dict(inproj=512, tq=1024, tk=1408, fft_kb=4, merge=512, moe=256)
    tl.update(tiles or {})
    assert w_mod.shape[0] == 1, "single-layer trunk"
    i = 0

    rows = -(-(bsz + 1) // SUBLANES) * SUBLANES
    c_rows = jnp.pad(jnp.concatenate([c, c_ctx[None]], axis=0), ((0, rows - bsz - 1), (0, 0)))
    mod_all = _mod_call(c_rows, w_mod[i], b_mod[i])
    mod_all = jnp.pad(mod_all.reshape(rows, 6, d), ((0, 0), (0, SUBLANES - 6), (0, 0)))
    mod, modc = mod_all[:bsz], mod_all[bsz:bsz + 1]

    cuts = np.cumsum([Q_LORA, KV_LORA, QK_ROPE, 3 * HY_WIDTH])
    wi, bi = w_in[i], b_in[i][None]
    w_q, w_kv, w_pe, w_hy, w_g = jnp.split(wi, cuts, axis=1)
    b_q, b_kv, b_pe, b_hy, b_g = jnp.split(bi, cuts, axis=1)
    wa = jnp.concatenate([w_q, w_kv, _rope_slot(w_pe, False), _rope_slot(w_pe, True)], axis=1).astype(BF16)
    ba = jnp.concatenate([b_q, b_kv, _rope_slot(b_pe, False), _rope_slot(b_pe, True)], axis=1)
    wq3 = w_uq[i].reshape(Q_LORA, N_HEADS, QK_NOPE + QK_ROPE) * (ATTN_SCALE * math.log2(math.e))
    tail = ((0, 0), (0, 0), (0, HEAD_PAD - QK_NOPE))
    wuq = (jnp.pad(wq3[..., :QK_NOPE], tail) + _rope_slot(wq3[..., QK_NOPE:], False)).reshape(Q_LORA, -1).astype(BF16)
    wuqs = _rope_slot(wq3[..., QK_NOPE:], True).reshape(Q_LORA, -1).astype(BF16)
    wkv3 = w_ukv[i].reshape(KV_LORA, N_HEADS, QK_NOPE + V_HEAD)
    wuk = jnp.pad(wkv3[..., :QK_NOPE], tail).reshape(KV_LORA, -1).astype(BF16)
    wuvt = wkv3[..., QK_NOPE:].reshape(KV_LORA, -1).T.astype(BF16)
    nm, qn, kvn = norm_mix[i][None], q_norm[i][None], kv_norm[i][None]

    w_c = jnp.concatenate([w_kv, _rope_slot(w_pe, False)], axis=1).astype(BF16)
    b_c = jnp.concatenate([b_kv, _rope_slot(b_pe, False)], axis=1)
    ck, cvt = _ctx_call(ctx, modc, nm, w_c, b_c, kvn, wuk, wuvt)

    cos_t, sin_t = _rope_tables(s)
    q, k, vt, hv, hx1, hx2, gate = _inproj_call(
        x, mod, nm, wa, ba, w_hy.astype(BF16), b_hy, w_g.astype(BF16), b_g, qn, wuq, wuqs, kvn, wuk, wuvt,
        cos_t, sin_t, hy_conv_w[i], hy_conv_b[i][None], tl["inproj"])

    attn = _attn_call(q, jnp.concatenate([ck, k], axis=2), jnp.concatenate([cvt, vt], axis=3), tl["tq"], tl["tk"])
    hy = _hyena(hv, hx1, hx2, hy_filt_w1[i], hy_filt_b1[i], hy_filt_w2[i], hy_filt_b2[i], hy_filt_w3[i],
                hy_filt_freq[i], hy_skip[i], tl["fft_kb"])
    xm = _merge_call(x, attn, hy, gate, mod, w_branch_attn[i].astype(BF16), w_branch_hyena[i].astype(BF16),
                     w_out[i].astype(BF16), tl["merge"])
    return _moe(xm, mod, norm_ffn[i][None], w_router[i], router_bias[i], w_exp_gate[i], w_exp_up[i], w_exp_down[i],
                w_sh_gate[i], w_sh_up[i], w_sh_down[i], final_norm[None], tl["moe"])
```

```python
import functools
import math

import numpy as np
import jax
import jax.numpy as jnp
from jax import lax
from jax.experimental import pallas as pl
from jax.experimental.pallas import tpu as pltpu

GRID_W = 64
N_HEADS = 8
QK_NOPE = 64
QK_ROPE = 32
V_HEAD = 64
Q_LORA = 256
KV_LORA = 128
ROPE_THETA = 10000.0
ATTN_SCALE = 1.0 / math.sqrt(QK_NOPE + QK_ROPE)
HY_WIDTH = 512
HY_ORDER = 2
HY_SHORT = 3
HY_BANDS = 8
HY_EMB = 1 + 2 * HY_BANDS
HY_EMB_PAD = 32
HY_FAST_DECAY = 0.3
HY_SLOW_DECAY = 1.5
HY_DECAY_TARGET = 1e-2
N_EXPERTS = 64
N_GROUPS = 8
GROUP_SIZE = N_EXPERTS // N_GROUPS
TOPK_GROUPS = 4
TOP_K = 8
EXPERT_FF = 256
ROUTE_SCALE = 2.5
EXPERT_BLOCK = 1024
RUN_ALIGN = 8
ROUTE_TILES = 4
LONG_RUN = 8
NORM_EPS = 1e-6

HEAD_PAD = 128
Q_CHUNK = 512
AHEAD = 2
LANES = 128
SUBLANES = 8
VMEM_LIMIT = 48 * 1024 * 1024

F32 = jnp.float32
BF16 = jnp.bfloat16
NT_DIMS = (((1,), (1,)), ((), ()))
NN_DIMS = (((1,), (0,)), ((), ()))


def _params(*sem):
    return pltpu.CompilerParams(dimension_semantics=sem, vmem_limit_bytes=VMEM_LIMIT)


def _dot(a, b):
    return jnp.dot(a.astype(BF16), b.astype(BF16), preferred_element_type=F32)


def _split(a):
    hi = a.astype(BF16)
    lo = (a - hi.astype(F32)).astype(BF16)
    return hi, lo


def _dot3(a, b, dims=NN_DIMS):
    ah, al = _split(a)
    bh, bl = _split(b)
    d = functools.partial(lax.dot_general, dimension_numbers=dims, preferred_element_type=F32)
    return d(ah, bh) + (d(ah, bl) + d(al, bh))


def _rms(x, g):
    return x * lax.rsqrt(jnp.mean(x * x, axis=-1, keepdims=True) + NORM_EPS) * g


def _silu(x):
    return x * jax.nn.sigmoid(x)


def _full(shape):
    nd = len(shape)
    return pl.BlockSpec(shape, lambda *_: (0,) * nd)


def _mod_kernel(c_ref, w_ref, b_ref, o_ref):
    o_ref[...] = _dot3(_silu(c_ref[...]), w_ref[...]) + b_ref[...]


def _mod_call(c_rows, w_mod, b_mod):
    r, d = c_rows.shape
    n = w_mod.shape[1]
    bn = 1024
    return pl.pallas_call(
        _mod_kernel,
        grid=(n // bn,),
        in_specs=[_full((r, d)), pl.BlockSpec((d, bn), lambda j: (0, j)), pl.BlockSpec((1, bn), lambda j: (0, j))],
        out_specs=pl.BlockSpec((r, bn), lambda j: (0, j)),
        out_shape=jax.ShapeDtypeStruct((r, n), F32),
        compiler_params=_params("arbitrary"),
        name="mod",
    )(c_rows, w_mod, b_mod.reshape(1, n))


def _prenorm(x, mod_ref, row, g):
    shift = mod_ref[0, row:row + 1, :]
    scale = mod_ref[0, row + 1:row + 2, :]
    return _rms(x, g) * (1.0 + scale) + shift


def _kv_heads(kv_lat, kpe, kvn_ref, wuk_ref, wuvt_ref, k_out, vt_out):
    kvn = _rms(kv_lat, kvn_ref[...]).astype(BF16)
    kk = _dot(kvn, wuk_ref[...])
    vt = lax.dot_general(wuvt_ref[...], kvn, NT_DIMS, preferred_element_type=F32)
    ones = jnp.ones((HEAD_PAD - V_HEAD, vt.shape[1]), F32)
    for h in range(N_HEADS):
        k_out[0, h] = (kk[:, HEAD_PAD * h:HEAD_PAD * (h + 1)] + kpe).astype(BF16)
        vt_out[0, h] = jnp.concatenate([vt[V_HEAD * h:V_HEAD * (h + 1)], ones], axis=0).astype(BF16)


def _ctx_kernel(c_ref, mod_ref, nm_ref, w_ref, b_ref, kvn_ref, wuk_ref, wuv_ref, k_out, v_out):
    h = _prenorm(c_ref[0], mod_ref, 0, nm_ref[...]).astype(BF16)
    a = _dot(h, w_ref[...]) + b_ref[...]
    _kv_heads(a[:, :KV_LORA], a[:, KV_LORA:], kvn_ref, wuk_ref, wuv_ref, k_out, v_out)


def _ctx_call(ctx, modc, norm_mix, w_c, b_c, kv_norm, w_uk, w_uv):
    bsz, n, d = ctx.shape
    return pl.pallas_call(
        _ctx_kernel,
        grid=(bsz,),
        in_specs=[pl.BlockSpec((1, n, d), lambda b: (b, 0, 0)), _full(modc.shape), _full(norm_mix.shape),
                  _full(w_c.shape), _full(b_c.shape), _full(kv_norm.shape), _full(w_uk.shape), _full(w_uv.shape)],
        out_specs=[pl.BlockSpec((1, N_HEADS, n, HEAD_PAD), lambda b: (b, 0, 0, 0)),
                   pl.BlockSpec((1, N_HEADS, HEAD_PAD, n), lambda b: (b, 0, 0, 0))],
        out_shape=[jax.ShapeDtypeStruct((bsz, N_HEADS, n, HEAD_PAD), BF16),
                   jax.ShapeDtypeStruct((bsz, N_HEADS, HEAD_PAD, n), BF16)],
        compiler_params=_params("arbitrary"),
        name="ctx",
    )(ctx, modc, norm_mix, w_c, b_c, kv_norm, w_uk, w_uv)


def _inproj_kernel(x_ref, xp_ref, xn_ref, mod_ref, nm_ref, wa_ref, ba_ref, why_ref, bhy_ref, wg_ref, bg_ref,
                   qn_ref, wuq_ref, wuqs_ref, kvn_ref, wuk_ref, wuv_ref, cos_ref, sin_ref, cw_ref, cb_ref,
                   q_out, k_out, v_out, hv_out, hx1_out, hx2_out, g_out):
    i = pl.program_id(0)
    tm = x_ref.shape[1]
    nm = nm_ref[...]
    h = _prenorm(x_ref[0], mod_ref, 0, nm).astype(BF16)
    a = _dot(h, wa_ref[...]) + ba_ref[...]
    q_lat = a[:, :Q_LORA]
    kv_lat = a[:, Q_LORA:Q_LORA + KV_LORA]
    kpe_m = a[:, Q_LORA + KV_LORA:Q_LORA + KV_LORA + HEAD_PAD]
    kpe_s = a[:, Q_LORA + KV_LORA + HEAD_PAD:]
    cos = cos_ref[...]
    sin = sin_ref[...]
    qn = _rms(q_lat, qn_ref[...]).astype(BF16)
    qa = _dot(qn, wuq_ref[...])
    qs = _dot(qn, wuqs_ref[...])
    for hh in range(N_HEADS):
        sl = slice(HEAD_PAD * hh, HEAD_PAD * (hh + 1))
        q_out[0, hh] = (qa[:, sl] * cos + qs[:, sl] * sin).astype(BF16)
    _kv_heads(kv_lat, kpe_m * cos + kpe_s * sin, kvn_ref, wuk_ref, wuv_ref, k_out, v_out)
    g_out[0] = (_dot(h, wg_ref[...]) + bg_ref[...]).astype(BF16)

    why = why_ref[...]
    bhy = bhy_ref[...]
    halo = jnp.concatenate([_prenorm(xp_ref[0], mod_ref, 0, nm), _prenorm(xn_ref[0], mod_ref, 0, nm)], axis=0)
    hy_all = _dot(jnp.concatenate([h, halo.astype(BF16)], axis=0), why) + bhy
    hy = hy_all[:tm]
    prev = jnp.where(i == 0, 0.0, hy_all[tm + SUBLANES - 1:tm + SUBLANES])
    nxt = jnp.where(i == pl.num_programs(0) - 1, 0.0, hy_all[tm + SUBLANES:tm + SUBLANES + 1])
    rid = lax.broadcasted_iota(jnp.int32, (tm, 1), 0)
    up = jnp.where(rid == 0, prev, pltpu.roll(hy, 1, 0))
    dn = jnp.where(rid == tm - 1, nxt, pltpu.roll(hy, tm - 1, 0))
    u = up * cw_ref[0:1, :] + hy * cw_ref[1:2, :] + dn * cw_ref[2:3, :] + cb_ref[...]
    hv_out[0] = u[:, :HY_WIDTH]
    hx1_out[0] = u[:, HY_WIDTH:2 * HY_WIDTH]
    hx2_out[0] = u[:, 2 * HY_WIDTH:]


def _inproj_call(x, mod, norm_mix, wa, ba, why, bhy, wg, bg, q_norm, wuq, wuqs, kv_norm, wuk, wuvt, cos_t, sin_t, cw,
                 cb, tm):
    bsz, s, d = x.shape
    nt = s // tm
    rb = tm // SUBLANES
    last_rb = s // SUBLANES - 1
    consts = [norm_mix, wa, ba, why, bhy, wg, bg, q_norm, wuq, wuqs, kv_norm, wuk, wuvt]
    in_specs = [
        pl.BlockSpec((1, tm, d), lambda i, b: (b, i, 0)),
        pl.BlockSpec((1, SUBLANES, d), lambda i, b: (b, jnp.maximum(i * rb - 1, 0), 0)),
        pl.BlockSpec((1, SUBLANES, d), lambda i, b: (b, jnp.minimum((i + 1) * rb, last_rb), 0)),
        pl.BlockSpec((1, SUBLANES, d), lambda i, b: (b, 0, 0)),
    ] + [_full(c.shape) for c in consts] + [
        pl.BlockSpec((tm, HEAD_PAD), lambda i, b: (i, 0)),
        pl.BlockSpec((tm, HEAD_PAD), lambda i, b: (i, 0)),
        _full(cw.shape), _full(cb.shape),
    ]
    hw = HY_WIDTH
    out_specs = [
        pl.BlockSpec((1, N_HEADS, tm, HEAD_PAD), lambda i, b: (b, 0, i, 0)),
        pl.BlockSpec((1, N_HEADS, tm, HEAD_PAD), lambda i, b: (b, 0, i, 0)),
        pl.BlockSpec((1, N_HEADS, HEAD_PAD, tm), lambda i, b: (b, 0, 0, i)),
        pl.BlockSpec((1, tm, hw), lambda i, b: (b, i, 0)),
        pl.BlockSpec((1, tm, hw), lambda i, b: (b, i, 0)),
        pl.BlockSpec((1, tm, hw), lambda i, b: (b, i, 0)),
        pl.BlockSpec((1, tm, 2 * d), lambda i, b: (b, i, 0)),
    ]
    out_shape = [
        jax.ShapeDtypeStruct((bsz, N_HEADS, s, HEAD_PAD), BF16),
        jax.ShapeDtypeStruct((bsz, N_HEADS, s, HEAD_PAD), BF16),
        jax.ShapeDtypeStruct((bsz, N_HEADS, HEAD_PAD, s), BF16),
        jax.ShapeDtypeStruct((bsz, s, hw), F32),
        jax.ShapeDtypeStruct((bsz, s, hw), F32),
        jax.ShapeDtypeStruct((bsz, s, hw), F32),
        jax.ShapeDtypeStruct((bsz, s, 2 * d), BF16),
    ]
    return pl.pallas_call(
        _inproj_kernel,
        grid=(nt, bsz),
        in_specs=in_specs,
        out_specs=out_specs,
        out_shape=out_shape,
        compiler_params=_params("arbitrary", "arbitrary"),
        name="inproj",
    )(x, x, x, mod, *consts, cos_t, sin_t, cw, cb)


def _attn_kernel(q_ref, k_ref, vt_ref, o_ref, m_sc, acc_sc):
    j = pl.program_id(2)

    @pl.when(j == 0)
    def _():
        m_sc[...] = jnp.full(m_sc.shape, -jnp.inf, F32)
        acc_sc[...] = jnp.zeros(acc_sc.shape, F32)

    tq = q_ref.shape[2]
    qw = min(tq, Q_CHUNK)
    units = [(h, c) for h in range(N_HEADS) for c in range(0, tq, qw)]

    def scores(u):
        h, c = units[u]
        return lax.dot_general(k_ref[0, h], q_ref[0, h, c:c + qw, :], NT_DIMS,
                               preferred_element_type=F32)

    pending = [scores(u) for u in range(AHEAD)]
    for u, (h, c) in enumerate(units):
        if u + AHEAD < len(units):
            pending.append(scores(u + AHEAD))
        st = pending.pop(0)
        m_prev = m_sc[h, :, c:c + qw]
        m_new = jnp.maximum(m_prev, jnp.max(st, axis=0, keepdims=True))
        pt = jnp.exp2(st - m_new).astype(BF16)
        acc_sc[h, :, c:c + qw] = (jnp.exp2(m_prev - m_new) * acc_sc[h, :, c:c + qw]
                                  + jnp.dot(vt_ref[0, h], pt, preferred_element_type=F32))
        m_sc[h, :, c:c + qw] = m_new

    @pl.when(j == pl.num_programs(2) - 1)
    def _():
        ot = jnp.concatenate([acc_sc[h, :V_HEAD] / acc_sc[h, V_HEAD:V_HEAD + 1] for h in range(N_HEADS)], axis=0)
        o_ref[0] = ot.T.astype(o_ref.dtype)


def _attn_call(q, k, vt, tq, tk):
    bsz, nh, s, dh = q.shape
    nk = k.shape[2]
    dv = nh * V_HEAD
    return pl.pallas_call(
        _attn_kernel,
        grid=(bsz, s // tq, nk // tk),
        in_specs=[
            pl.BlockSpec((1, nh, tq, dh), lambda b, i, j: (b, 0, i, 0)),
            pl.BlockSpec((1, nh, tk, dh), lambda b, i, j: (b, 0, j, 0)),
            pl.BlockSpec((1, nh, dh, tk), lambda b, i, j: (b, 0, 0, j)),
        ],
        out_specs=pl.BlockSpec((1, tq, dv), lambda b, i, j: (b, i, 0)),
        out_shape=jax.ShapeDtypeStruct((bsz, s, dv), BF16),
        scratch_shapes=[pltpu.VMEM((nh, 1, tq), F32), pltpu.VMEM((nh, dh, tq), F32)],
        compiler_params=_params("arbitrary", "arbitrary", "arbitrary"),
        name="attn",
    )(q, k, vt)


def _filter_kernel(emb_ref, w1_ref, b1_ref, w2_ref, b2_ref, w3_ref, fr_ref, dl_ref, full_out, asum_out, *, seq):
    r = pl.program_id(0)
    rb = emb_ref.shape[0]
    emb = emb_ref[...]
    fr = fr_ref[...]
    h = jnp.sin(fr * (_dot3(emb, w1_ref[...]) + b1_ref[...]))
    h = jnp.sin(fr * (_dot3(h, w2_ref[...]) + b2_ref[...]))
    k = _dot3(h, w3_ref[0]) * jnp.exp(-emb[:, 0:1] * dl_ref[...])
    row = r * rb + lax.broadcasted_iota(jnp.int32, (rb, 1), 0)
    k = jnp.where(row == seq, 0.0, k)
    full_out[...] = k

    @pl.when(r == 0)
    def _():
        asum_out[...] = jnp.zeros(asum_out.shape, F32)

    asum_out[...] += jnp.sum(jnp.abs(k), axis=0, keepdims=True)


def _filter_call(emb, w1, b1, w2, b2, w3sel, freq, deltas2, seq, rb):
    n2 = emb.shape[0]
    half_blocks = seq // rb
    width = w3sel.shape[2]
    return pl.pallas_call(
        functools.partial(_filter_kernel, seq=seq),
        grid=(n2 // rb,),
        in_specs=[pl.BlockSpec((rb, HY_EMB_PAD), lambda r: (r, 0)), _full(w1.shape), _full(b1.shape),
                  _full(w2.shape), _full(b2.shape),
                  pl.BlockSpec((1,) + w3sel.shape[1:], lambda r: (r // half_blocks, 0, 0)),
                  _full(freq.shape), _full(deltas2.shape)],
        out_specs=[pl.BlockSpec((rb, width), lambda r: (r, 0)), pl.BlockSpec((1, width), lambda r: (0, 0))],
        out_shape=[jax.ShapeDtypeStruct((n2, width), F32), jax.ShapeDtypeStruct((1, width), F32)],
        compiler_params=_params("arbitrary"),
        name="filt",
    )(emb, w1, b1, w2, b2, w3sel, freq, deltas2)


def _fa_kernel(u_ref, f_ref, a_out):
    two, _, hn, g, c = u_ref.shape
    a = _dot(f_ref[...], u_ref[...].reshape(two * hn * g, c))
    a_out[...] = (_pack(a) if a_out.dtype == jnp.uint32 else a).reshape(a_out.shape)


def _fa_call(u5, fmat, packed):
    _, p, hn, n, c = u5.shape
    g = SUBLANES
    co, dt = (c // 2, jnp.uint32) if packed else (c, F32)
    return pl.pallas_call(
        _fa_kernel,
        grid=(p, n // g),
        in_specs=[pl.BlockSpec((2, 1, hn, g, c), lambda q, j: (0, q, 0, j, 0)), _full(fmat.shape)],
        out_specs=pl.BlockSpec((1, 2, n, g, co), lambda q, j: (q, 0, 0, j, 0)),
        out_shape=jax.ShapeDtypeStruct((p, 2, n, n, co), dt),
        compiler_params=_params("arbitrary", "arbitrary"),
        name="fa",
    )(u5, fmat)


def _dot_packed(w, u):
    lo, hi = _unpack(u)
    return jnp.concatenate([jnp.dot(w, lo, preferred_element_type=F32), jnp.dot(w, hi, preferred_element_type=F32)],
                           axis=1)


def _fb_kernel(a_ref, g_ref, asum_ref, kf_out):
    two, _, n, c = a_ref.shape[1:]
    a = a_ref[0].reshape(two * n, c)
    x = _dot(g_ref[0], a) / (asum_ref[...] + 1e-6)
    kf_out[0] = x.reshape(two, n, c)


def _fb_call(a5, gmat, asum):
    _, _, n, _, c = a5.shape
    return pl.pallas_call(
        _fb_kernel,
        grid=(n,),
        in_specs=[pl.BlockSpec((1, 2, 1, n, c), lambda k: (0, 0, k, 0, 0)),
                  pl.BlockSpec((1, 2 * n, 2 * n), lambda k: (k, 0, 0)), _full(asum.shape)],
        out_specs=pl.BlockSpec((1, 2, n, c), lambda k: (k, 0, 0, 0)),
        out_shape=jax.ShapeDtypeStruct((n, 2, n, c), F32),
        compiler_params=_params("arbitrary"),
        name="fb",
    )(a5, gmat, asum)


def _mid_kernel(a_ref, g_ref, h_ref, kf_ref, b_out):
    _, two, kb, n, c = a_ref.shape
    for kk in range(kb):
        x = _dot_packed(g_ref[kk], a_ref[0, :, kk].reshape(two * n, c))
        xr, xi = x[:n], x[n:]
        kr, ki = kf_ref[kk, 0], kf_ref[kk, 1]
        y = jnp.concatenate([xr * kr - xi * ki, xr * ki + xi * kr], axis=0)
        b_out[0, :, kk] = _pack(_dot(h_ref[kk], y)).reshape(two, n, c)


def _mid_call(a5, gmat, hmat, kf, order, kb):
    p, _, n, _, c = a5.shape
    return pl.pallas_call(
        _mid_kernel,
        grid=(n // kb, p),
        in_specs=[pl.BlockSpec((1, 2, kb, n, c), lambda k, q: (q, 0, k, 0, 0)),
                  pl.BlockSpec((kb, 2 * n, 2 * n), lambda k, q: (k, 0, 0)),
                  pl.BlockSpec((kb, 2 * n, 2 * n), lambda k, q: (k, 0, 0)),
                  pl.BlockSpec((kb, 2, n, 2 * c), lambda k, q: (k, 0, 0, order))],
        out_specs=pl.BlockSpec((1, 2, kb, n, c), lambda k, q: (q, 0, k, 0, 0)),
        out_shape=jax.ShapeDtypeStruct(a5.shape, jnp.uint32),
        compiler_params=_params("arbitrary", "arbitrary"),
        name="mid",
    )(a5, gmat, hmat, kf)


def _fc_kernel(b_ref, f_ref, u_ref, m_ref, skip_ref, o_out):
    _, two, n, g, c = b_ref.shape
    y = _dot_packed(f_ref[...], b_ref[...].reshape(two * n * g, c)).reshape(u_ref.shape)
    o_out[...] = m_ref[...] * (y + u_ref[...] * skip_ref[...])


def _fc_call(b5, finv, u5, m5, skip_row):
    _, p, hn, n, c = u5.shape
    g = SUBLANES
    blk = pl.BlockSpec((2, 1, hn, g, c), lambda q, j: (0, q, 0, j, 0))
    return pl.pallas_call(
        _fc_kernel,
        grid=(p, n // g),
        in_specs=[pl.BlockSpec((1, 2, n, g, c // 2), lambda q, j: (q, 0, 0, j, 0)), _full(finv.shape), blk, blk,
                  _full(skip_row.shape)],
        out_specs=blk,
        out_shape=jax.ShapeDtypeStruct(u5.shape, F32),
        compiler_params=_params("arbitrary", "arbitrary"),
        name="fc",
    )(b5, finv, u5, m5, skip_row)


def _dft_tables(n):
    hn = n // 2
    k = np.arange(n)[:, None]
    ang = -2.0 * np.pi * (k * np.arange(n)[None, :] % n) / n
    fr, fi = np.cos(ang), np.sin(ang)
    f_data = np.block([[fr[:, :hn], -fi[:, :hn]], [fi[:, :hn], fr[:, :hn]]])
    f_filt = np.concatenate([fr, fi], axis=0)
    er, ei = fr[:hn], -fi[:hn]
    f_inv = np.block([[er, -ei], [ei, er]]) / float(n * n)
    k1 = jnp.arange(n, dtype=jnp.int32)[:, None, None]
    k2 = jnp.arange(n, dtype=jnp.int32)[None, :, None]
    m2 = jnp.arange(n, dtype=jnp.int32)[None, None, :]
    idx = (m2 * (k1 + n * k2)) % (n * n)
    ang2 = idx.astype(F32) * (-2.0 * math.pi / (n * n))
    gr, gi = jnp.cos(ang2), jnp.sin(ang2)
    g = jnp.concatenate([jnp.concatenate([gr, -gi], axis=2), jnp.concatenate([gi, gr], axis=2)], axis=1)
    h = jnp.swapaxes(g, 1, 2)

    def widen(f):
        return jnp.asarray(np.kron(f, np.eye(SUBLANES)), BF16)

    return widen(f_data), widen(f_filt), widen(f_inv), g.astype(BF16), h.astype(BF16)


def _hyena_filter_tables(seq):
    t = jnp.linspace(0.0, 1.0, seq, dtype=F32)[:, None]
    w = 2.0 * math.pi * jnp.arange(seq, dtype=F32)[:, None] / seq
    f = jnp.linspace(1e-4, HY_BANDS - 1, HY_BANDS, dtype=F32)[None, :]
    emb = jnp.concatenate([t, jnp.cos(f * w), -jnp.sin(f * w)], axis=-1)
    emb = jnp.concatenate([emb, emb[:1], emb[:0:-1]], axis=0)
    emb = jnp.pad(emb, ((0, 0), (0, HY_EMB_PAD - HY_EMB)))
    deltas = jnp.abs(jnp.linspace(math.log(HY_DECAY_TARGET) / HY_SLOW_DECAY,
                                  math.log(HY_DECAY_TARGET) / HY_FAST_DECAY, HY_WIDTH, dtype=F32))
    return emb, jnp.tile(deltas, HY_ORDER)[None, :]


def _hyena(hv, hx1, hx2, w1, b1, w2, b2, w3, freq, skip, kb):
    bsz, seq, c = hv.shape
    n = int(round(math.sqrt(2 * seq)))
    assert n * n == 2 * seq and bsz % 2 == 0
    hn, p = n // 2, bsz // 2
    f_data, f_filt, f_inv, gmat, hmat = _dft_tables(n)

    emb, deltas2 = _hyena_filter_tables(seq)
    w1p = jnp.pad(w1, ((0, HY_EMB_PAD - HY_EMB), (0, 0)))
    w3r = w3.reshape(w3.shape[0], HY_ORDER, 2, c)
    w3sel = jnp.stack([w3r[:, :, 0, :].reshape(-1, HY_ORDER * c), w3r[:, :, 1, :].reshape(-1, HY_ORDER * c)])
    full, asum = _filter_call(emb, w1p, b1[None], w2, b2[None], w3sel, freq[None], deltas2, seq, min(512, seq))
    c2 = HY_ORDER * c
    kf = _fb_call(_fa_call(full.reshape(2, 1, hn, n, c2), f_filt, False), gmat, asum)

    def view(t):
        return t.reshape(2, p, hn, n, c)

    def long_conv(u5, m5, order):
        bm = _mid_call(_fa_call(u5, f_data, True), gmat, hmat, kf, order, kb)
        return _fc_call(bm, f_inv, u5, m5, skip[order][None, :])

    z = long_conv(view(hv), view(hx1), 0)
    return long_conv(z, view(hx2), 1).reshape(bsz, seq, c)


def _merge_kernel(x_ref, at_ref, hy_ref, g_ref, mod_ref, wba_ref, wbh_ref, wo_ref, o_ref):
    d = x_ref.shape[2]
    g = g_ref[0].astype(F32)
    y = (jax.nn.sigmoid(g[:, :d]) * _dot(at_ref[0], wba_ref[...])
         + jax.nn.sigmoid(g[:, d:]) * _dot(hy_ref[0], wbh_ref[...]))
    o_ref[0] = x_ref[0] + mod_ref[0, 2:3, :] * _dot(y, wo_ref[...])


def _merge_call(x, attn, hy, gate, mod, wba, wbh, wo, tm):
    bsz, s, d = x.shape

    def tok(w):
        return pl.BlockSpec((1, tm, w), lambda b, i: (b, i, 0))

    return pl.pallas_call(
        _merge_kernel,
        grid=(bsz, s // tm),
        in_specs=[tok(d), tok(attn.shape[2]), tok(hy.shape[2]), tok(2 * d),
                  pl.BlockSpec((1, SUBLANES, d), lambda b, i: (b, 0, 0)),
                  _full(wba.shape), _full(wbh.shape), _full(wo.shape)],
        out_specs=tok(d),
        out_shape=jax.ShapeDtypeStruct((bsz, s, d), F32),
        compiler_params=_params("arbitrary", "arbitrary"),
        name="merge",
    )(x, attn, hy, gate, mod, wba, wbh, wo)


def _route_kernel(xm_ref, mod_ref, nf_ref, wrt_ref, rb_ref, tri_ref, lt_ref, h2_out, w_out, p_out, col_out, row_out):
    tm = xm_ref.shape[1]
    ng, gs = N_GROUPS, GROUP_SIZE

    h2 = _prenorm(xm_ref[0], mod_ref, 3, nf_ref[...])
    h2_out[0] = h2.astype(h2_out.dtype)
    scores = jax.nn.sigmoid(_dot3(wrt_ref[...], h2, NT_DIMS))
    sel = scores + rb_ref[...]
    slabs = [sel[ng * j:ng * (j + 1)] for j in range(gs)]

    top1 = jnp.full((ng, tm), -jnp.inf, F32)
    top2 = top1
    for x in slabs:
        top2 = jnp.maximum(top2, jnp.minimum(top1, x))
        top1 = jnp.maximum(top1, x)
    gscore = top1 + top2
    gid = lax.broadcasted_iota(jnp.int32, (ng, 1), 0)
    rank = jnp.zeros((ng, tm), jnp.int32)
    for g2 in range(ng):
        row = gscore[g2:g2 + 1]
        beats = (row > gscore) | ((row == gscore) & (g2 < gid))
        rank = rank + beats.astype(jnp.int32)
    gmask = rank < TOPK_GROUPS

    cand = [jnp.where(gmask, x, -jnp.inf) for x in slabs]
    eid = [gid * gs + j for j in range(gs)]
    chosen = []
    for _ in range(TOP_K):
        best = functools.reduce(jnp.maximum, cand)
        best = jnp.max(best, axis=0, keepdims=True)
        idx = functools.reduce(jnp.minimum, [jnp.where(cand[j] == best, eid[j], N_EXPERTS) for j in range(gs)])
        idx = jnp.min(idx, axis=0, keepdims=True)
        chosen.append(idx)
        cand = [jnp.where(eid[j] == idx, -jnp.inf, cand[j]) for j in range(gs)]

    mask = [functools.reduce(jnp.logical_or, [eid[j] == idx for idx in chosen]) for j in range(gs)]
    maskb = jnp.concatenate(mask, axis=0)
    wsel = jnp.where(maskb, scores, 0.0)
    w_out[...] = wsel / jnp.sum(wsel, axis=0, keepdims=True) * ROUTE_SCALE

    def extents(cnt, lower_sum):
        units = jnp.floor((cnt + (RUN_ALIGN - 1)) * (1.0 / RUN_ALIGN))
        start = RUN_ALIGN * lower_sum(units.astype(BF16))
        return start, start + RUN_ALIGN * units

    lane = lax.broadcasted_iota(jnp.int32, (N_EXPERTS, LANES), 1)
    sub = lax.broadcasted_iota(jnp.int32, (SUBLANES, N_EXPERTS), 0)
    ts = tri_ref.shape[0]
    for c in range(tm // ts):
        mb = maskb[:, c * ts:(c + 1) * ts]
        maskf = jnp.where(mb, 1.0, 0.0)
        mask16 = maskf.astype(BF16)
        before = jnp.dot(mask16, tri_ref[...], preferred_element_type=F32)
        p_out[:, c * ts:(c + 1) * ts] = jnp.where(mb, before, -1.0).astype(p_out.dtype)
        cnt_c = jnp.sum(maskf, axis=1, keepdims=True)
        start_c, end_c = extents(jnp.broadcast_to(cnt_c, (N_EXPERTS, LANES)),
                                 lambda u: jnp.dot(lt_ref[...], u, preferred_element_type=F32))
        col_out[c] = jnp.where(lane == 0, cnt_c, jnp.where(lane == 1, start_c, end_c))
        cnt_r = lax.dot_general(jnp.ones((SUBLANES, ts), BF16), mask16, NT_DIMS, preferred_element_type=F32)
        start_r, end_r = extents(cnt_r,
                                 lambda u: lax.dot_general(u, lt_ref[...], NT_DIMS, preferred_element_type=F32))
        row_out[c] = jnp.where(sub == 0, start_r, end_r)


def _route_call(xm, mod, norm_ffn, wrt, rbias, lower, ts, tiles_per_step):
    bsz, s, d = xm.shape
    t = bsz * s
    tm = ts * tiles_per_step
    nt = s // tm
    tri = (jnp.arange(ts)[:, None] < jnp.arange(ts)[None, :]).astype(BF16)
    tok = pl.BlockSpec((N_EXPERTS, tm), lambda i: (0, i))
    return pl.pallas_call(
        _route_kernel,
        grid=(t // tm,),
        in_specs=[pl.BlockSpec((1, tm, d), lambda i: (i // nt, i % nt, 0)),
                  pl.BlockSpec((1, SUBLANES, d), lambda i: (i // nt, 0, 0)),
                  _full(norm_ffn.shape), _full(wrt.shape), _full(rbias.shape), _full(tri.shape),
                  _full(lower.shape)],
        out_specs=[pl.BlockSpec((1, tm, d), lambda i: (i // nt, i % nt, 0)), tok, tok,
                   pl.BlockSpec((tiles_per_step, N_EXPERTS, LANES), lambda i: (i, 0, 0)),
                   pl.BlockSpec((tiles_per_step, SUBLANES, N_EXPERTS), lambda i: (i, 0, 0))],
        out_shape=[jax.ShapeDtypeStruct((bsz, s, d), BF16), jax.ShapeDtypeStruct((N_EXPERTS, t), F32),
                   jax.ShapeDtypeStruct((N_EXPERTS, t), BF16),
                   jax.ShapeDtypeStruct((t // ts, N_EXPERTS, LANES), F32),
                   jax.ShapeDtypeStruct((t // ts, SUBLANES, N_EXPERTS), F32)],
        compiler_params=_params("arbitrary"),
        name="route",
    )(xm, mod, norm_ffn, wrt, rbias, tri, lower)


def _pack(x):
    w = x.shape[1] // 2
    lo = lax.bitcast_convert_type(x[:, :w].astype(BF16).astype(F32), jnp.uint32)
    hi = lax.bitcast_convert_type(x[:, w:].astype(BF16).astype(F32), jnp.uint32)
    return hi | (lo >> 16)


def _unpack(u):
    lo = lax.bitcast_convert_type(u << 16, F32).astype(BF16)
    hi = lax.bitcast_convert_type(u & jnp.uint32(0xFFFF0000), F32).astype(BF16)
    return lo, hi


def _pow2_pieces(units, limit):
    bit = 1
    while bit * 2 <= limit:
        bit *= 2
    while bit:
        yield (units & bit) != 0, units & ~(2 * bit - 1), bit
        bit //= 2


def _rows_copy(vm_ref, hbm_ref, sem, vm_row, hbm_row, rows, to_hbm):
    v = vm_ref.at[pl.ds(pl.multiple_of(vm_row, RUN_ALIGN), rows), :]
    h = hbm_ref.at[pl.ds(pl.multiple_of(hbm_row, RUN_ALIGN), rows), :]
    return pltpu.make_async_copy(v, h, sem) if to_hbm else pltpu.make_async_copy(h, v, sem)


def _run_copies(vm_ref, hbm_ref, sem, n8, vm_row, hbm_row, limit, to_hbm, act):
    def emit(pieces):
        for on, off, size in pieces:
            @pl.when(on)
            def _():
                act(_rows_copy(vm_ref, hbm_ref, sem, vm_row + RUN_ALIGN * off, hbm_row + RUN_ALIGN * off,
                               RUN_ALIGN * size, to_hbm))

    pieces = list(_pow2_pieces(n8, limit))
    long_pieces = [p for p in pieces if p[2] >= LONG_RUN]
    if long_pieces:
        pl.when(n8 >= LONG_RUN)(lambda: emit(long_pieces))
    emit([p for p in pieces if p[2] < LONG_RUN])


def _wait_rows(vm_ref, hbm_ref, sem, units, limit, to_hbm):
    for on, _, size in _pow2_pieces(units, limit):
        @pl.when(on)
        def _():
            _rows_copy(vm_ref, hbm_ref, sem, 0, 0, RUN_ALIGN * size, to_hbm).wait()


def _dispatch_kernel(n8_ref, ls_ref, gs_ref, ts_ref, t8_ref, nu_ref, pos_ref, ext_ref, h_ref, xs_out, srt2, zbuf,
                     sems):
    step = pl.program_id(0)
    tm = h_ref.shape[0]
    rows = srt2.shape[1]
    slot = step % 2
    srt, sem = srt2.at[slot], sems.at[slot]
    rid = lax.broadcasted_iota(jnp.int32, (rows, 1), 0).astype(F32)
    start = ext_ref[0, 0:1, :]
    member = jnp.where((rid >= start) & (rid < ext_ref[0, 1:2, :]), 1.0, 0.0)
    offset = rid - jnp.sum(member * start, axis=1, keepdims=True)
    pos = jnp.dot(member.astype(BF16), pos_ref[...], preferred_element_type=F32)
    sel = jnp.where(pos == offset, 1.0, 0.0).astype(BF16)
    srt[...] = _pack(jnp.dot(sel, h_ref[...], preferred_element_type=F32))

    def send(e, c):
        i = step * N_EXPERTS + e
        _run_copies(srt, xs_out, sem, n8_ref[i], ls_ref[i], gs_ref[i], tm // RUN_ALIGN, True, lambda cp: cp.start())
        return c

    lax.fori_loop(0, N_EXPERTS, send, 0)

    def wait_tile(tile, s):
        last = tile * N_EXPERTS + N_EXPERTS - 1
        _wait_rows(srt2.at[s], xs_out, sems.at[s], ls_ref[last] // RUN_ALIGN + n8_ref[last], rows // RUN_ALIGN, True)

    pl.when(step > 0)(lambda: wait_tile(step - 1, 1 - slot))

    @pl.when(step == pl.num_programs(0) - 1)
    def _():
        wait_tile(step, slot)
        zbuf[...] = jnp.zeros(zbuf.shape, zbuf.dtype)
        nblk = xs_out.shape[0] // EXPERT_BLOCK

        def fill(act):
            def tails(e, c):
                _run_copies(zbuf, xs_out, sem, t8_ref[e], 0, ts_ref[e], EXPERT_BLOCK // RUN_ALIGN - 1, True, act)
                return c

            def blocks(b, c):
                act(pltpu.make_async_copy(
                    zbuf, xs_out.at[pl.ds(pl.multiple_of(b * EXPERT_BLOCK, EXPERT_BLOCK), EXPERT_BLOCK), :], sem))
                return c

            lax.fori_loop(0, N_EXPERTS, tails, 0)
            lax.fori_loop(nu_ref[0], nblk, blocks, 0)

        fill(lambda cp: cp.start())
        fill(lambda cp: cp.wait())


def _dispatch_call(tables, pos_et, ext_rows, h2, nblk, tm):
    t, d = h2.shape
    lrows = TOP_K * tm + N_EXPERTS * RUN_ALIGN
    return pl.pallas_call(
        _dispatch_kernel,
        grid_spec=pltpu.PrefetchScalarGridSpec(
            num_scalar_prefetch=len(tables), grid=(t // tm,),
            in_specs=[pl.BlockSpec((N_EXPERTS, tm), lambda i, *_: (0, i)),
                      pl.BlockSpec((1,) + ext_rows.shape[1:], lambda i, *_: (i, 0, 0)),
                      pl.BlockSpec((tm, d), lambda i, *_: (i, 0))],
            out_specs=pl.BlockSpec(memory_space=pl.ANY),
            scratch_shapes=[pltpu.VMEM((2, lrows, d // 2), jnp.uint32),
                            pltpu.VMEM((EXPERT_BLOCK, d // 2), jnp.uint32), pltpu.SemaphoreType.DMA((2,))]),
        out_shape=jax.ShapeDtypeStruct((nblk * EXPERT_BLOCK, d // 2), jnp.uint32),
        compiler_params=_params("arbitrary"),
        name="dispatch",
    )(*tables, pos_et, ext_rows, h2)


def _expert_kernel(blk_ref, nused_ref, x_ref, wgu_ref, wd_ref, y_ref):
    used = pl.program_id(0) < nused_ref[0]

    @pl.when(used)
    def _():
        lo, hi = _unpack(x_ref[...])
        half = lo.shape[1]
        gu = (jnp.dot(lo, wgu_ref[0, :half, :], preferred_element_type=F32)
              + jnp.dot(hi, wgu_ref[0, half:, :], preferred_element_type=F32))
        a = _silu(gu[:, :EXPERT_FF]) * gu[:, EXPERT_FF:]
        y_ref[...] = _pack(_dot(a, wd_ref[0]))

    @pl.when(jnp.logical_not(used))
    def _():
        y_ref[...] = jnp.zeros(y_ref.shape, y_ref.dtype)


def _expert_call(blk_e, nused, xs, wgu, wd):
    rows, d = xs.shape
    nblk = rows // EXPERT_BLOCK

    def row_map(i, blk, nu):
        return (jnp.minimum(i, nu[0] - 1), 0)

    return pl.pallas_call(
        _expert_kernel,
        grid_spec=pltpu.PrefetchScalarGridSpec(
            num_scalar_prefetch=2, grid=(nblk,),
            in_specs=[pl.BlockSpec((EXPERT_BLOCK, d), row_map),
                      pl.BlockSpec((1,) + wgu.shape[1:], lambda i, blk, nu: (blk[i], 0, 0)),
                      pl.BlockSpec((1,) + wd.shape[1:], lambda i, blk, nu: (blk[i], 0, 0))],
            out_specs=pl.BlockSpec((EXPERT_BLOCK, d), lambda i, blk, nu: (i, 0))),
        out_shape=jax.ShapeDtypeStruct((rows, d), jnp.uint32),
        compiler_params=_params("arbitrary"),
        name="expert",
    )(blk_e, nused, xs, wgu, wd)


def _combine_kernel(n8_ref, ls_ref, gs_ref, ys_hbm, pos_ref, w_ref, ext_ref, xm_ref, h_ref, mod_ref, wsgu_ref,
                    wsd_ref, fn_ref, o_ref, ybuf2, sems):
    step = pl.program_id(0)
    tm = xm_ref.shape[0]
    rows = ybuf2.shape[1]
    slot = step % 2
    ybuf, sem = ybuf2.at[slot], sems.at[slot]

    def fetch(tile, s):
        def body(e, c):
            i = tile * N_EXPERTS + e
            _run_copies(ybuf2.at[s], ys_hbm, sems.at[s], n8_ref[i], ls_ref[i], gs_ref[i], tm // RUN_ALIGN, False,
                        lambda cp: cp.start())
            return c
        lax.fori_loop(0, N_EXPERTS, body, 0)

    pl.when(step == 0)(lambda: fetch(step, slot))
    pl.when(step + 1 < pl.num_programs(0))(lambda: fetch(step + 1, 1 - slot))
    gu = _dot(h_ref[...], wsgu_ref[...])
    ff = gu.shape[1] // 2
    shared = _dot(_silu(gu[:, :ff]) * gu[:, ff:], wsd_ref[...])
    cid = lax.broadcasted_iota(jnp.int32, (1, rows), 1).astype(F32)
    start = ext_ref[0, :, 1:2]
    member = jnp.where((cid >= start) & (cid < ext_ref[0, :, 2:3]), 1.0, 0.0)
    offset = cid - jnp.sum(member * start, axis=0, keepdims=True)
    member = member.astype(BF16)
    pos = jnp.dot(pos_ref[...], member, preferred_element_type=F32)
    mix = jnp.where(pos == offset, jnp.dot(w_ref[...].astype(BF16), member, preferred_element_type=F32), 0.0)
    mix = mix.astype(BF16)
    last = step * N_EXPERTS + N_EXPERTS - 1
    filled = ls_ref[last] + RUN_ALIGN * n8_ref[last]
    _wait_rows(ybuf, ys_hbm, sem, filled // RUN_ALIGN, rows // RUN_ALIGN, False)
    rid = lax.broadcasted_iota(jnp.int32, (rows, 1), 0)
    lo, hi = _unpack(jnp.where(rid < filled, ybuf[...], jnp.uint32(0)))
    routed = jnp.concatenate([jnp.dot(mix, lo, preferred_element_type=F32),
                              jnp.dot(mix, hi, preferred_element_type=F32)], axis=1)
    x = xm_ref[...] + mod_ref[0, 5:6, :] * (routed + shared)
    o_ref[...] = _rms(x, fn_ref[...])


def _combine_call(tables, ys, pos_te, w_te, ext_cols, xm, h2, mod, wsgu, wsd, final_norm, tm, tiles_per_batch):
    t, d = xm.shape
    lrows = TOP_K * tm + N_EXPERTS * RUN_ALIGN
    tok = pl.BlockSpec((tm, d), lambda i, *_: (i, 0))
    per_e = pl.BlockSpec((tm, N_EXPERTS), lambda i, *_: (i, 0))
    return pl.pallas_call(
        _combine_kernel,
        grid_spec=pltpu.PrefetchScalarGridSpec(
            num_scalar_prefetch=len(tables), grid=(t // tm,),
            in_specs=[pl.BlockSpec(memory_space=pl.ANY), per_e, per_e,
                      pl.BlockSpec((1,) + ext_cols.shape[1:], lambda i, *_: (i, 0, 0)), tok, tok,
                      pl.BlockSpec((1, SUBLANES, d), lambda i, *_: (i // tiles_per_batch, 0, 0)),
                      _full(wsgu.shape), _full(wsd.shape), _full(final_norm.shape)],
            out_specs=tok,
            scratch_shapes=[pltpu.VMEM((2, lrows, d // 2), jnp.uint32), pltpu.SemaphoreType.DMA((2,))]),
        out_shape=jax.ShapeDtypeStruct((t, d), F32),
        compiler_params=_params("arbitrary"),
        name="combine",
    )(*tables, ys, pos_te, w_te, ext_cols, xm, h2, mod, wsgu, wsd, final_norm)


def _moe(xm, mod, norm_ffn, w_router, router_bias, wg, wu, wd, wsg, wsu, wsd, final_norm, tm):
    bsz, s, d = xm.shape
    t = bsz * s
    nt = t // tm
    perm = (np.arange(N_EXPERTS) % N_GROUPS) * GROUP_SIZE + np.arange(N_EXPERTS) // N_GROUPS
    wrt = w_router.T[perm]
    rbias = router_bias[perm][:, None]
    lower = jnp.asarray(perm[None, :] < perm[:, None], BF16)
    h2, w_et, pos_et, ext_cols, ext_rows = _route_call(xm, mod, norm_ffn, wrt, rbias, lower, tm, ROUTE_TILES)

    inv = np.argsort(perm)
    n8 = (ext_cols[:, :, 0].astype(jnp.int32)[:, inv] + (RUN_ALIGN - 1)) // RUN_ALIGN
    run = RUN_ALIGN * n8
    ls = jnp.cumsum(run, axis=1) - run
    tot = jnp.sum(run, axis=0)
    padded = (tot + EXPERT_BLOCK - 1) // EXPERT_BLOCK * EXPERT_BLOCK
    pad_end = jnp.cumsum(padded)
    gs = (pad_end - padded)[None, :] + jnp.cumsum(run, axis=0) - run
    nblk = -(-(t * TOP_K + nt * N_EXPERTS * (RUN_ALIGN - 1)) // EXPERT_BLOCK) + N_EXPERTS
    blk_first = jnp.arange(nblk, dtype=jnp.int32)[:, None] * EXPERT_BLOCK
    blk_e = jnp.minimum(jnp.sum((pad_end[None, :] <= blk_first).astype(jnp.int32), axis=1), N_EXPERTS - 1)
    nused = (pad_end[-1:] // EXPERT_BLOCK).astype(jnp.int32)
    tables = [a.reshape(-1).astype(jnp.int32) for a in (n8, ls, gs)]
    tails = [(pad_end - padded + tot).astype(jnp.int32), ((padded - tot) // RUN_ALIGN).astype(jnp.int32), nused]

    h2f = h2.reshape(t, d)
    xs = _dispatch_call(tables + tails, pos_et, ext_rows, h2f, nblk, tm)
    wgu = jnp.concatenate([wg, wu], axis=2).astype(BF16)
    ys = _expert_call(blk_e, nused, xs, wgu, wd.astype(BF16))
    wsgu = jnp.concatenate([wsg, wsu], axis=1).astype(BF16)
    out = _combine_call(tables, ys, pos_et.T, w_et.T, ext_cols, xm.reshape(t, d), h2f, mod, wsgu, wsd.astype(BF16),
                        final_norm, tm, s // tm)
    return out.reshape(bsz, s, d)


def _rope_tables(s):
    rows = s // GRID_W
    row = jnp.broadcast_to(jnp.arange(rows, dtype=F32)[:, None], (rows, GRID_W)).reshape(-1)
    col = jnp.broadcast_to(jnp.arange(GRID_W, dtype=F32)[None, :], (rows, GRID_W)).reshape(-1)
    half = QK_ROPE // 2
    inv_freq = ROPE_THETA ** (-jnp.arange(0, half, 2, dtype=F32) / half)
    ar, ac = row[:, None] * inv_freq, col[:, None] * inv_freq
    ones = jnp.ones((s, QK_NOPE), F32)
    tail = HEAD_PAD - QK_NOPE - QK_ROPE
    cos_t = jnp.concatenate([ones, jnp.cos(ar), jnp.cos(ar), jnp.cos(ac), jnp.cos(ac), jnp.ones((s, tail), F32)], 1)
    sin_t = jnp.concatenate([0 * ones, -jnp.sin(ar), jnp.sin(ar), -jnp.sin(ac), jnp.sin(ac),
                             jnp.zeros((s, tail), F32)], 1)
    return cos_t, sin_t


_Q4 = QK_ROPE // 4
ROPE_SWAP = np.concatenate([np.arange(_Q4, 2 * _Q4), np.arange(0, _Q4), np.arange(3 * _Q4, 4 * _Q4),
                            np.arange(2 * _Q4, 3 * _Q4)])


def _rope_slot(w, swap):
    if swap:
        w = w[..., ROPE_SWAP]
    pad = [(0, 0)] * (w.ndim - 1) + [(QK_NOPE, HEAD_PAD - QK_NOPE - QK_ROPE)]
    return jnp.pad(w, pad)


def kernel(x, c, ctx, c_ctx, w_mod, b_mod, norm_mix, norm_ffn, w_in, b_in, q_norm, w_uq, kv_norm, w_ukv, w_branch_attn, hy_conv_w, hy_conv_b, hy_filt_w1, hy_filt_b1, hy_filt_w2, hy_filt_b2, hy_filt_w3, hy_filt_freq, hy_skip, w_branch_hyena, w_out, w_router, router_bias, w_exp_gate, w_exp_up, w_exp_down, w_sh_gate, w_sh_up, w_sh_down, final_norm,
           tiles=None):
    bsz, s, d = x.shape
    tl = dict(inproj=512, tq=1024, tk=1408, fft_kb=8, merge=512, moe=256)
    tl.update(tiles or {})
    assert w_mod.shape[0] == 1, "single-layer trunk"
    i = 0

    rows = -(-(bsz + 1) // SUBLANES) * SUBLANES
    c_rows = jnp.pad(jnp.concatenate([c, c_ctx[None]], axis=0), ((0, rows - bsz - 1), (0, 0)))
    mod_all = _mod_call(c_rows, w_mod[i], b_mod[i])
    mod_all = jnp.pad(mod_all.reshape(rows, 6, d), ((0, 0), (0, SUBLANES - 6), (0, 0)))
    mod, modc = mod_all[:bsz], mod_all[bsz:bsz + 1]

    cuts = np.cumsum([Q_LORA, KV_LORA, QK_ROPE, 3 * HY_WIDTH])
    wi, bi = w_in[i], b_in[i][None]
    w_q, w_kv, w_pe, w_hy, w_g = jnp.split(wi, cuts, axis=1)
    b_q, b_kv, b_pe, b_hy, b_g = jnp.split(bi, cuts, axis=1)
    wa = jnp.concatenate([w_q, w_kv, _rope_slot(w_pe, False), _rope_slot(w_pe, True)], axis=1).astype(BF16)
    ba = jnp.concatenate([b_q, b_kv, _rope_slot(b_pe, False), _rope_slot(b_pe, True)], axis=1)
    wq3 = w_uq[i].reshape(Q_LORA, N_HEADS, QK_NOPE + QK_ROPE) * (ATTN_SCALE * math.log2(math.e))
    tail = ((0, 0), (0, 0), (0, HEAD_PAD - QK_NOPE))
    wuq = (jnp.pad(wq3[..., :QK_NOPE], tail) + _rope_slot(wq3[..., QK_NOPE:], False)).reshape(Q_LORA, -1).astype(BF16)
    wuqs = _rope_slot(wq3[..., QK_NOPE:], True).reshape(Q_LORA, -1).astype(BF16)
    wkv3 = w_ukv[i].reshape(KV_LORA, N_HEADS, QK_NOPE + V_HEAD)
    wuk = jnp.pad(wkv3[..., :QK_NOPE], tail).reshape(KV_LORA, -1).astype(BF16)
    wuvt = wkv3[..., QK_NOPE:].reshape(KV_LORA, -1).T.astype(BF16)
    nm, qn, kvn = norm_mix[i][None], q_norm[i][None], kv_norm[i][None]

    w_c = jnp.concatenate([w_kv, _rope_slot(w_pe, False)], axis=1).astype(BF16)
    b_c = jnp.concatenate([b_kv, _rope_slot(b_pe, False)], axis=1)
    ck, cvt = _ctx_call(ctx, modc, nm, w_c, b_c, kvn, wuk, wuvt)

    cos_t, sin_t = _rope_tables(s)
    q, k, vt, hv, hx1, hx2, gate = _inproj_call(
        x, mod, nm, wa, ba, w_hy.astype(BF16), b_hy, w_g.astype(BF16), b_g, qn, wuq, wuqs, kvn, wuk, wuvt,
        cos_t, sin_t, hy_conv_w[i], hy_conv_b[i][None], tl["inproj"])

    attn = _attn_call(q, jnp.concatenate([ck, k], axis=2), jnp.concatenate([cvt, vt], axis=3), tl["tq"], tl["tk"])
    hy = _hyena(hv, hx1, hx2, hy_filt_w1[i], hy_filt_b1[i], hy_filt_w2[i], hy_filt_b2[i], hy_filt_w3[i],
                hy_filt_freq[i], hy_skip[i], tl["fft_kb"])
    xm = _merge_call(x, attn, hy, gate, mod, w_branch_attn[i].astype(BF16), w_branch_hyena[i].astype(BF16),
                     w_out[i].astype(BF16), tl["merge"])
    return _moe(xm, mod, norm_ffn[i][None], w_router[i], router_bias[i], w_exp_gate[i], w_exp_up[i], w_exp_down[i],
                w_sh_gate[i], w_sh_up[i], w_sh_down[i], final_norm[None], tl["moe"])
```

```python
import functools
import math

import numpy as np
import jax
import jax.numpy as jnp
from jax import lax
from jax.experimental import pallas as pl
from jax.experimental.pallas import tpu as pltpu

GRID_W = 64
N_HEADS = 8
QK_NOPE = 64
QK_ROPE = 32
V_HEAD = 64
Q_LORA = 256
KV_LORA = 128
ROPE_THETA = 10000.0
ATTN_SCALE = 1.0 / math.sqrt(QK_NOPE + QK_ROPE)
HY_WIDTH = 512
HY_ORDER = 2
HY_SHORT = 3
HY_BANDS = 8
HY_EMB = 1 + 2 * HY_BANDS
HY_EMB_PAD = 32
HY_FAST_DECAY = 0.3
HY_SLOW_DECAY = 1.5
HY_DECAY_TARGET = 1e-2
N_EXPERTS = 64
N_GROUPS = 8
GROUP_SIZE = N_EXPERTS // N_GROUPS
TOPK_GROUPS = 4
TOP_K = 8
EXPERT_FF = 256
ROUTE_SCALE = 2.5
EXPERT_BLOCK = 1024
RUN_ALIGN = 8
ROUTE_TILES = 4
LONG_RUN = 8
NORM_EPS = 1e-6

HEAD_PAD = 128
Q_CHUNK = 512
AHEAD = 2
LANES = 128
SUBLANES = 8
VMEM_LIMIT = 48 * 1024 * 1024

F32 = jnp.float32
BF16 = jnp.bfloat16
NT_DIMS = (((1,), (1,)), ((), ()))
NN_DIMS = (((1,), (0,)), ((), ()))


def _params(*sem):
    return pltpu.CompilerParams(dimension_semantics=sem, vmem_limit_bytes=VMEM_LIMIT)


def _dot(a, b):
    return jnp.dot(a.astype(BF16), b.astype(BF16), preferred_element_type=F32)


def _split(a):
    hi = a.astype(BF16)
    lo = (a - hi.astype(F32)).astype(BF16)
    return hi, lo


def _dot3(a, b, dims=NN_DIMS):
    ah, al = _split(a)
    bh, bl = _split(b)
    d = functools.partial(lax.dot_general, dimension_numbers=dims, preferred_element_type=F32)
    return d(ah, bh) + (d(ah, bl) + d(al, bh))


def _rms(x, g):
    return x * lax.rsqrt(jnp.mean(x * x, axis=-1, keepdims=True) + NORM_EPS) * g


def _silu(x):
    return x * jax.nn.sigmoid(x)


def _full(shape):
    nd = len(shape)
    return pl.BlockSpec(shape, lambda *_: (0,) * nd)


def _mod_kernel(c_ref, w_ref, b_ref, o_ref):
    o_ref[...] = _dot3(_silu(c_ref[...]), w_ref[...]) + b_ref[...]


def _mod_call(c_rows, w_mod, b_mod):
    r, d = c_rows.shape
    n = w_mod.shape[1]
    bn = 1024
    return pl.pallas_call(
        _mod_kernel,
        grid=(n // bn,),
        in_specs=[_full((r, d)), pl.BlockSpec((d, bn), lambda j: (0, j)), pl.BlockSpec((1, bn), lambda j: (0, j))],
        out_specs=pl.BlockSpec((r, bn), lambda j: (0, j)),
        out_shape=jax.ShapeDtypeStruct((r, n), F32),
        compiler_params=_params("arbitrary"),
        name="mod",
    )(c_rows, w_mod, b_mod.reshape(1, n))


def _prenorm(x, mod_ref, row, g):
    shift = mod_ref[0, row:row + 1, :]
    scale = mod_ref[0, row + 1:row + 2, :]
    return _rms(x, g) * (1.0 + scale) + shift


def _kv_heads(kv_lat, kpe, kvn_ref, wuk_ref, wuvt_ref, k_out, vt_out):
    kvn = _rms(kv_lat, kvn_ref[...]).astype(BF16)
    kk = _dot(kvn, wuk_ref[...])
    vt = lax.dot_general(wuvt_ref[...], kvn, NT_DIMS, preferred_element_type=F32)
    ones = jnp.ones((HEAD_PAD - V_HEAD, vt.shape[1]), F32)
    for h in range(N_HEADS):
        k_out[0, h] = (kk[:, HEAD_PAD * h:HEAD_PAD * (h + 1)] + kpe).astype(BF16)
        vt_out[0, h] = jnp.concatenate([vt[V_HEAD * h:V_HEAD * (h + 1)], ones], axis=0).astype(BF16)


def _ctx_kernel(c_ref, mod_ref, nm_ref, w_ref, b_ref, kvn_ref, wuk_ref, wuv_ref, k_out, v_out):
    h = _prenorm(c_ref[0], mod_ref, 0, nm_ref[...]).astype(BF16)
    a = _dot(h, w_ref[...]) + b_ref[...]
    _kv_heads(a[:, :KV_LORA], a[:, KV_LORA:], kvn_ref, wuk_ref, wuv_ref, k_out, v_out)


def _ctx_call(ctx, modc, norm_mix, w_c, b_c, kv_norm, w_uk, w_uv):
    bsz, n, d = ctx.shape
    return pl.pallas_call(
        _ctx_kernel,
        grid=(bsz,),
        in_specs=[pl.BlockSpec((1, n, d), lambda b: (b, 0, 0)), _full(modc.shape), _full(norm_mix.shape),
                  _full(w_c.shape), _full(b_c.shape), _full(kv_norm.shape), _full(w_uk.shape), _full(w_uv.shape)],
        out_specs=[pl.BlockSpec((1, N_HEADS, n, HEAD_PAD), lambda b: (b, 0, 0, 0)),
                   pl.BlockSpec((1, N_HEADS, HEAD_PAD, n), lambda b: (b, 0, 0, 0))],
        out_shape=[jax.ShapeDtypeStruct((bsz, N_HEADS, n, HEAD_PAD), BF16),
                   jax.ShapeDtypeStruct((bsz, N_HEADS, HEAD_PAD, n), BF16)],
        compiler_params=_params("arbitrary"),
        name="ctx",
    )(ctx, modc, norm_mix, w_c, b_c, kv_norm, w_uk, w_uv)


def _inproj_kernel(x_ref, xp_ref, xn_ref, mod_ref, nm_ref, wa_ref, ba_ref, why_ref, bhy_ref, wg_ref, bg_ref,
                   qn_ref, wuq_ref, wuqs_ref, kvn_ref, wuk_ref, wuv_ref, cos_ref, sin_ref, cw_ref, cb_ref,
                   q_out, k_out, v_out, hv_out, hx1_out, hx2_out, g_out):
    i = pl.program_id(0)
    tm = x_ref.shape[1]
    nm = nm_ref[...]
    h = _prenorm(x_ref[0], mod_ref, 0, nm).astype(BF16)
    a = _dot(h, wa_ref[...]) + ba_ref[...]
    q_lat = a[:, :Q_LORA]
    kv_lat = a[:, Q_LORA:Q_LORA + KV_LORA]
    kpe_m = a[:, Q_LORA + KV_LORA:Q_LORA + KV_LORA + HEAD_PAD]
    kpe_s = a[:, Q_LORA + KV_LORA + HEAD_PAD:]
    cos = cos_ref[...]
    sin = sin_ref[...]
    qn = _rms(q_lat, qn_ref[...]).astype(BF16)
    qa = _dot(qn, wuq_ref[...])
    qs = _dot(qn, wuqs_ref[...])
    for hh in range(N_HEADS):
        sl = slice(HEAD_PAD * hh, HEAD_PAD * (hh + 1))
        q_out[0, hh] = (qa[:, sl] * cos + qs[:, sl] * sin).astype(BF16)
    _kv_heads(kv_lat, kpe_m * cos + kpe_s * sin, kvn_ref, wuk_ref, wuv_ref, k_out, v_out)
    g_out[0] = (_dot(h, wg_ref[...]) + bg_ref[...]).astype(BF16)

    why = why_ref[...]
    bhy = bhy_ref[...]
    halo = jnp.concatenate([_prenorm(xp_ref[0], mod_ref, 0, nm), _prenorm(xn_ref[0], mod_ref, 0, nm)], axis=0)
    hy_all = _dot(jnp.concatenate([h, halo.astype(BF16)], axis=0), why) + bhy
    hy = hy_all[:tm]
    prev = jnp.where(i == 0, 0.0, hy_all[tm + SUBLANES - 1:tm + SUBLANES])
    nxt = jnp.where(i == pl.num_programs(0) - 1, 0.0, hy_all[tm + SUBLANES:tm + SUBLANES + 1])
    rid = lax.broadcasted_iota(jnp.int32, (tm, 1), 0)
    up = jnp.where(rid == 0, prev, pltpu.roll(hy, 1, 0))
    dn = jnp.where(rid == tm - 1, nxt, pltpu.roll(hy, tm - 1, 0))
    u = up * cw_ref[0:1, :] + hy * cw_ref[1:2, :] + dn * cw_ref[2:3, :] + cb_ref[...]
    hv_out[0] = u[:, :HY_WIDTH]
    hx1_out[0] = u[:, HY_WIDTH:2 * HY_WIDTH]
    hx2_out[0] = u[:, 2 * HY_WIDTH:]


def _inproj_call(x, mod, norm_mix, wa, ba, why, bhy, wg, bg, q_norm, wuq, wuqs, kv_norm, wuk, wuvt, cos_t, sin_t, cw,
                 cb, tm):
    bsz, s, d = x.shape
    nt = s // tm
    rb = tm // SUBLANES
    last_rb = s // SUBLANES - 1
    consts = [norm_mix, wa, ba, why, bhy, wg, bg, q_norm, wuq, wuqs, kv_norm, wuk, wuvt]
    in_specs = [
        pl.BlockSpec((1, tm, d), lambda i, b: (b, i, 0)),
        pl.BlockSpec((1, SUBLANES, d), lambda i, b: (b, jnp.maximum(i * rb - 1, 0), 0)),
        pl.BlockSpec((1, SUBLANES, d), lambda i, b: (b, jnp.minimum((i + 1) * rb, last_rb), 0)),
        pl.BlockSpec((1, SUBLANES, d), lambda i, b: (b, 0, 0)),
    ] + [_full(c.shape) for c in consts] + [
        pl.BlockSpec((tm, HEAD_PAD), lambda i, b: (i, 0)),
        pl.BlockSpec((tm, HEAD_PAD), lambda i, b: (i, 0)),
        _full(cw.shape), _full(cb.shape),
    ]
    hw = HY_WIDTH
    out_specs = [
        pl.BlockSpec((1, N_HEADS, tm, HEAD_PAD), lambda i, b: (b, 0, i, 0)),
        pl.BlockSpec((1, N_HEADS, tm, HEAD_PAD), lambda i, b: (b, 0, i, 0)),
        pl.BlockSpec((1, N_HEADS, HEAD_PAD, tm), lambda i, b: (b, 0, 0, i)),
        pl.BlockSpec((1, tm, hw), lambda i, b: (b, i, 0)),
        pl.BlockSpec((1, tm, hw), lambda i, b: (b, i, 0)),
        pl.BlockSpec((1, tm, hw), lambda i, b: (b, i, 0)),
        pl.BlockSpec((1, tm, 2 * d), lambda i, b: (b, i, 0)),
    ]
    out_shape = [
        jax.ShapeDtypeStruct((bsz, N_HEADS, s, HEAD_PAD), BF16),
        jax.ShapeDtypeStruct((bsz, N_HEADS, s, HEAD_PAD), BF16),
        jax.ShapeDtypeStruct((bsz, N_HEADS, HEAD_PAD, s), BF16),
        jax.ShapeDtypeStruct((bsz, s, hw), F32),
        jax.ShapeDtypeStruct((bsz, s, hw), F32),
        jax.ShapeDtypeStruct((bsz, s, hw), F32),
        jax.ShapeDtypeStruct((bsz, s, 2 * d), BF16),
    ]
    return pl.pallas_call(
        _inproj_kernel,
        grid=(nt, bsz),
        in_specs=in_specs,
        out_specs=out_specs,
        out_shape=out_shape,
        compiler_params=_params("arbitrary", "arbitrary"),
        name="inproj",
    )(x, x, x, mod, *consts, cos_t, sin_t, cw, cb)


def _attn_kernel(q_ref, k_ref, vt_ref, o_ref, m_sc, acc_sc):
    j = pl.program_id(2)

    @pl.when(j == 0)
    def _():
        m_sc[...] = jnp.full(m_sc.shape, -jnp.inf, F32)
        acc_sc[...] = jnp.zeros(acc_sc.shape, F32)

    tq = q_ref.shape[2]
    qw = min(tq, Q_CHUNK)
    units = [(h, c) for h in range(N_HEADS) for c in range(0, tq, qw)]

    def scores(u):
        h, c = units[u]
        return lax.dot_general(k_ref[0, h], q_ref[0, h, c:c + qw, :], NT_DIMS,
                               preferred_element_type=F32)

    pending = [scores(u) for u in range(AHEAD)]
    for u, (h, c) in enumerate(units):
        if u + AHEAD < len(units):
            pending.append(scores(u + AHEAD))
        st = pending.pop(0)
        m_prev = m_sc[h, :, c:c + qw]
        m_new = jnp.maximum(m_prev, jnp.max(st, axis=0, keepdims=True))
        pt = jnp.exp2(st - m_new).astype(BF16)
        acc_sc[h, :, c:c + qw] = (jnp.exp2(m_prev - m_new) * acc_sc[h, :, c:c + qw]
                                  + jnp.dot(vt_ref[0, h], pt, preferred_element_type=F32))
        m_sc[h, :, c:c + qw] = m_new

    @pl.when(j == pl.num_programs(2) - 1)
    def _():
        ot = jnp.concatenate([acc_sc[h, :V_HEAD] / acc_sc[h, V_HEAD:V_HEAD + 1] for h in range(N_HEADS)], axis=0)
        o_ref[0] = ot.T.astype(o_ref.dtype)


def _attn_call(q, k, vt, tq, tk):
    bsz, nh, s, dh = q.shape
    nk = k.shape[2]
    dv = nh * V_HEAD
    return pl.pallas_call(
        _attn_kernel,
        grid=(bsz, s // tq, nk // tk),
        in_specs=[
            pl.BlockSpec((1, nh, tq, dh), lambda b, i, j: (b, 0, i, 0)),
            pl.BlockSpec((1, nh, tk, dh), lambda b, i, j: (b, 0, j, 0)),
            pl.BlockSpec((1, nh, dh, tk), lambda b, i, j: (b, 0, 0, j)),
        ],
        out_specs=pl.BlockSpec((1, tq, dv), lambda b, i, j: (b, i, 0)),
        out_shape=jax.ShapeDtypeStruct((bsz, s, dv), BF16),
        scratch_shapes=[pltpu.VMEM((nh, 1, tq), F32), pltpu.VMEM((nh, dh, tq), F32)],
        compiler_params=_params("arbitrary", "arbitrary", "arbitrary"),
        name="attn",
    )(q, k, vt)


def _filter_kernel(emb_ref, w1_ref, b1_ref, w2_ref, b2_ref, w3_ref, fr_ref, dl_ref, full_out, asum_out, *, seq):
    r = pl.program_id(0)
    rb = emb_ref.shape[0]
    emb = emb_ref[...]
    fr = fr_ref[...]
    h = jnp.sin(fr * (_dot3(emb, w1_ref[...]) + b1_ref[...]))
    h = jnp.sin(fr * (_dot3(h, w2_ref[...]) + b2_ref[...]))
    k = _dot3(h, w3_ref[0]) * jnp.exp(-emb[:, 0:1] * dl_ref[...])
    row = r * rb + lax.broadcasted_iota(jnp.int32, (rb, 1), 0)
    k = jnp.where(row == seq, 0.0, k)
    full_out[...] = k

    @pl.when(r == 0)
    def _():
        asum_out[...] = jnp.zeros(asum_out.shape, F32)

    asum_out[...] += jnp.sum(jnp.abs(k), axis=0, keepdims=True)


def _filter_call(emb, w1, b1, w2, b2, w3sel, freq, deltas2, seq, rb):
    n2 = emb.shape[0]
    half_blocks = seq // rb
    width = w3sel.shape[2]
    return pl.pallas_call(
        functools.partial(_filter_kernel, seq=seq),
        grid=(n2 // rb,),
        in_specs=[pl.BlockSpec((rb, HY_EMB_PAD), lambda r: (r, 0)), _full(w1.shape), _full(b1.shape),
                  _full(w2.shape), _full(b2.shape),
                  pl.BlockSpec((1,) + w3sel.shape[1:], lambda r: (r // half_blocks, 0, 0)),
                  _full(freq.shape), _full(deltas2.shape)],
        out_specs=[pl.BlockSpec((rb, width), lambda r: (r, 0)), pl.BlockSpec((1, width), lambda r: (0, 0))],
        out_shape=[jax.ShapeDtypeStruct((n2, width), F32), jax.ShapeDtypeStruct((1, width), F32)],
        compiler_params=_params("arbitrary"),
        name="filt",
    )(emb, w1, b1, w2, b2, w3sel, freq, deltas2)


def _fa_kernel(u_ref, f_ref, a_out):
    two, _, hn, g, c = u_ref.shape
    a = _dot(f_ref[...], u_ref[...].reshape(two * hn * g, c))
    a_out[...] = (_pack(a) if a_out.dtype == jnp.uint32 else a).reshape(a_out.shape)


def _fa_call(u5, fmat, packed):
    _, p, hn, n, c = u5.shape
    g = SUBLANES
    co, dt = (c // 2, jnp.uint32) if packed else (c, F32)
    return pl.pallas_call(
        _fa_kernel,
        grid=(p, n // g),
        in_specs=[pl.BlockSpec((2, 1, hn, g, c), lambda q, j: (0, q, 0, j, 0)), _full(fmat.shape)],
        out_specs=pl.BlockSpec((1, 2, n, g, co), lambda q, j: (q, 0, 0, j, 0)),
        out_shape=jax.ShapeDtypeStruct((p, 2, n, n, co), dt),
        compiler_params=_params("arbitrary", "arbitrary"),
        name="fa",
    )(u5, fmat)


def _dot_packed(w, u):
    lo, hi = _unpack(u)
    return jnp.concatenate([jnp.dot(w, lo, preferred_element_type=F32), jnp.dot(w, hi, preferred_element_type=F32)],
                           axis=1)


def _fb_kernel(a_ref, g_ref, asum_ref, kf_out):
    _, two, kb, n, c = a_ref.shape
    scale = 1.0 / (asum_ref[...] + 1e-6)
    for kk in range(kb):
        x = _dot(g_ref[kk], a_ref[0, :, kk].reshape(two * n, c)) * scale
        kf_out[kk] = x.reshape(two, n, c)


def _fb_call(a5, gmat, asum, kb):
    _, _, n, _, c = a5.shape
    return pl.pallas_call(
        _fb_kernel,
        grid=(n // kb,),
        in_specs=[pl.BlockSpec((1, 2, kb, n, c), lambda k: (0, 0, k, 0, 0)),
                  pl.BlockSpec((kb, 2 * n, 2 * n), lambda k: (k, 0, 0)), _full(asum.shape)],
        out_specs=pl.BlockSpec((kb, 2, n, c), lambda k: (k, 0, 0, 0)),
        out_shape=jax.ShapeDtypeStruct((n, 2, n, c), F32),
        compiler_params=_params("arbitrary"),
        name="fb",
    )(a5, gmat, asum)


def _mid_kernel(a_ref, g_ref, h_ref, kf_ref, b_out):
    _, two, kb, n, c = a_ref.shape
    for kk in range(kb):
        x = _dot_packed(g_ref[kk], a_ref[0, :, kk].reshape(two * n, c))
        xr, xi = x[:n], x[n:]
        kr, ki = kf_ref[kk, 0], kf_ref[kk, 1]
        y = jnp.concatenate([xr * kr - xi * ki, xr * ki + xi * kr], axis=0)
        b_out[0, :, kk] = _pack(_dot(h_ref[kk], y)).reshape(two, n, c)


def _mid_call(a5, gmat, hmat, kf, order, kb):
    p, _, n, _, c = a5.shape
    return pl.pallas_call(
        _mid_kernel,
        grid=(n // kb, p),
        in_specs=[pl.BlockSpec((1, 2, kb, n, c), lambda k, q: (q, 0, k, 0, 0)),
                  pl.BlockSpec((kb, 2 * n, 2 * n), lambda k, q: (k, 0, 0)),
                  pl.BlockSpec((kb, 2 * n, 2 * n), lambda k, q: (k, 0, 0)),
                  pl.BlockSpec((kb, 2, n, 2 * c), lambda k, q: (k, 0, 0, order))],
        out_specs=pl.BlockSpec((1, 2, kb, n, c), lambda k, q: (q, 0, k, 0, 0)),
        out_shape=jax.ShapeDtypeStruct(a5.shape, jnp.uint32),
        compiler_params=_params("arbitrary", "arbitrary"),
        name="mid",
    )(a5, gmat, hmat, kf)


def _fc_kernel(b_ref, f_ref, u_ref, m_ref, skip_ref, o_out):
    _, two, n, g, c = b_ref.shape
    y = _dot_packed(f_ref[...], b_ref[...].reshape(two * n * g, c)).reshape(u_ref.shape)
    o_out[...] = m_ref[...] * (y + u_ref[...] * skip_ref[...])


def _fc_call(b5, finv, u5, m5, skip_row):
    _, p, hn, n, c = u5.shape
    g = SUBLANES
    blk = pl.BlockSpec((2, 1, hn, g, c), lambda q, j: (0, q, 0, j, 0))
    return pl.pallas_call(
        _fc_kernel,
        grid=(p, n // g),
        in_specs=[pl.BlockSpec((1, 2, n, g, c // 2), lambda q, j: (q, 0, 0, j, 0)), _full(finv.shape), blk, blk,
                  _full(skip_row.shape)],
        out_specs=blk,
        out_shape=jax.ShapeDtypeStruct(u5.shape, F32),
        compiler_params=_params("arbitrary", "arbitrary"),
        name="fc",
    )(b5, finv, u5, m5, skip_row)


def _dft_tables(n):
    hn = n // 2
    k = np.arange(n)[:, None]
    ang = -2.0 * np.pi * (k * np.arange(n)[None, :] % n) / n
    fr, fi = np.cos(ang), np.sin(ang)
    f_data = np.block([[fr[:, :hn], -fi[:, :hn]], [fi[:, :hn], fr[:, :hn]]])
    f_filt = np.concatenate([fr, fi], axis=0)
    er, ei = fr[:hn], -fi[:hn]
    f_inv = np.block([[er, -ei], [ei, er]]) / float(n * n)
    k1 = jnp.arange(n, dtype=jnp.int32)[:, None, None]
    k2 = jnp.arange(n, dtype=jnp.int32)[None, :, None]
    m2 = jnp.arange(n, dtype=jnp.int32)[None, None, :]
    idx = (m2 * (k1 + n * k2)) % (n * n)
    ang2 = idx.astype(F32) * (-2.0 * math.pi / (n * n))
    gr, gi = jnp.cos(ang2), jnp.sin(ang2)
    g = jnp.concatenate([jnp.concatenate([gr, -gi], axis=2), jnp.concatenate([gi, gr], axis=2)], axis=1)
    h = jnp.swapaxes(g, 1, 2)

    def widen(f):
        return jnp.asarray(np.kron(f, np.eye(SUBLANES)), BF16)

    return widen(f_data), widen(f_filt), widen(f_inv), g.astype(BF16), h.astype(BF16)


def _hyena_filter_tables(seq):
    t = jnp.linspace(0.0, 1.0, seq, dtype=F32)[:, None]
    w = 2.0 * math.pi * jnp.arange(seq, dtype=F32)[:, None] / seq
    f = jnp.linspace(1e-4, HY_BANDS - 1, HY_BANDS, dtype=F32)[None, :]
    emb = jnp.concatenate([t, jnp.cos(f * w), -jnp.sin(f * w)], axis=-1)
    emb = jnp.concatenate([emb, emb[:1], emb[:0:-1]], axis=0)
    emb = jnp.pad(emb, ((0, 0), (0, HY_EMB_PAD - HY_EMB)))
    deltas = jnp.abs(jnp.linspace(math.log(HY_DECAY_TARGET) / HY_SLOW_DECAY,
                                  math.log(HY_DECAY_TARGET) / HY_FAST_DECAY, HY_WIDTH, dtype=F32))
    return emb, jnp.tile(deltas, HY_ORDER)[None, :]


def _hyena(hv, hx1, hx2, w1, b1, w2, b2, w3, freq, skip, kb):
    bsz, seq, c = hv.shape
    n = int(round(math.sqrt(2 * seq)))
    assert n * n == 2 * seq and bsz % 2 == 0
    hn, p = n // 2, bsz // 2
    f_data, f_filt, f_inv, gmat, hmat = _dft_tables(n)

    emb, deltas2 = _hyena_filter_tables(seq)
    w1p = jnp.pad(w1, ((0, HY_EMB_PAD - HY_EMB), (0, 0)))
    w3r = w3.reshape(w3.shape[0], HY_ORDER, 2, c)
    w3sel = jnp.stack([w3r[:, :, 0, :].reshape(-1, HY_ORDER * c), w3r[:, :, 1, :].reshape(-1, HY_ORDER * c)])
    full, asum = _filter_call(emb, w1p, b1[None], w2, b2[None], w3sel, freq[None], deltas2, seq, min(512, seq))
    c2 = HY_ORDER * c
    kf = _fb_call(_fa_call(full.reshape(2, 1, hn, n, c2), f_filt, False), gmat, asum, kb // HY_ORDER)

    def view(t):
        return t.reshape(2, p, hn, n, c)

    def long_conv(u5, m5, order):
        bm = _mid_call(_fa_call(u5, f_data, True), gmat, hmat, kf, order, kb)
        return _fc_call(bm, f_inv, u5, m5, skip[order][None, :])

    z = long_conv(view(hv), view(hx1), 0)
    return long_conv(z, view(hx2), 1).reshape(bsz, seq, c)


def _merge_kernel(x_ref, at_ref, hy_ref, g_ref, mod_ref, wba_ref, wbh_ref, wo_ref, o_ref):
    d = x_ref.shape[2]
    g = g_ref[0].astype(F32)
    y = (jax.nn.sigmoid(g[:, :d]) * _dot(at_ref[0], wba_ref[...])
         + jax.nn.sigmoid(g[:, d:]) * _dot(hy_ref[0], wbh_ref[...]))
    o_ref[0] = x_ref[0] + mod_ref[0, 2:3, :] * _dot(y, wo_ref[...])


def _merge_call(x, attn, hy, gate, mod, wba, wbh, wo, tm):
    bsz, s, d = x.shape

    def tok(w):
        return pl.BlockSpec((1, tm, w), lambda b, i: (b, i, 0))

    return pl.pallas_call(
        _merge_kernel,
        grid=(bsz, s // tm),
        in_specs=[tok(d), tok(attn.shape[2]), tok(hy.shape[2]), tok(2 * d),
                  pl.BlockSpec((1, SUBLANES, d), lambda b, i: (b, 0, 0)),
                  _full(wba.shape), _full(wbh.shape), _full(wo.shape)],
        out_specs=tok(d),
        out_shape=jax.ShapeDtypeStruct((bsz, s, d), F32),
        compiler_params=_params("arbitrary", "arbitrary"),
        name="merge",
    )(x, attn, hy, gate, mod, wba, wbh, wo)


def _route_kernel(xm_ref, mod_ref, nf_ref, wrt_ref, rb_ref, tri_ref, lt_ref, h2_out, w_out, p_out, col_out, row_out):
    tm = xm_ref.shape[1]
    ng, gs = N_GROUPS, GROUP_SIZE

    h2 = _prenorm(xm_ref[0], mod_ref, 3, nf_ref[...])
    h2_out[0] = h2.astype(h2_out.dtype)
    scores = jax.nn.sigmoid(_dot3(wrt_ref[...], h2, NT_DIMS))
    sel = scores + rb_ref[...]
    slabs = [sel[ng * j:ng * (j + 1)] for j in range(gs)]

    top1 = jnp.full((ng, tm), -jnp.inf, F32)
    top2 = top1
    for x in slabs:
        top2 = jnp.maximum(top2, jnp.minimum(top1, x))
        top1 = jnp.maximum(top1, x)
    gscore = top1 + top2
    gid = lax.broadcasted_iota(jnp.int32, (ng, 1), 0)
    rank = jnp.zeros((ng, tm), jnp.int32)
    for g2 in range(ng):
        row = gscore[g2:g2 + 1]
        beats = (row > gscore) | ((row == gscore) & (g2 < gid))
        rank = rank + beats.astype(jnp.int32)
    gmask = rank < TOPK_GROUPS

    cand = [jnp.where(gmask, x, -jnp.inf) for x in slabs]
    eid = [gid * gs + j for j in range(gs)]
    chosen = []
    for _ in range(TOP_K):
        best = functools.reduce(jnp.maximum, cand)
        best = jnp.max(best, axis=0, keepdims=True)
        idx = functools.reduce(jnp.minimum, [jnp.where(cand[j] == best, eid[j], N_EXPERTS) for j in range(gs)])
        idx = jnp.min(idx, axis=0, keepdims=True)
        chosen.append(idx)
        cand = [jnp.where(eid[j] == idx, -jnp.inf, cand[j]) for j in range(gs)]

    mask = [functools.reduce(jnp.logical_or, [eid[j] == idx for idx in chosen]) for j in range(gs)]
    maskb = jnp.concatenate(mask, axis=0)
    wsel = jnp.where(maskb, scores, 0.0)
    w_out[...] = wsel / jnp.sum(wsel, axis=0, keepdims=True) * ROUTE_SCALE

    def extents(cnt, lower_sum):
        units = jnp.floor((cnt + (RUN_ALIGN - 1)) * (1.0 / RUN_ALIGN))
        start = RUN_ALIGN * lower_sum(units.astype(BF16))
        return start, start + RUN_ALIGN * units

    lane = lax.broadcasted_iota(jnp.int32, (N_EXPERTS, LANES), 1)
    sub = lax.broadcasted_iota(jnp.int32, (SUBLANES, N_EXPERTS), 0)
    ts = tri_ref.shape[0]
    for c in range(tm // ts):
        mb = maskb[:, c * ts:(c + 1) * ts]
        maskf = jnp.where(mb, 1.0, 0.0)
        mask16 = maskf.astype(BF16)
        before = jnp.dot(mask16, tri_ref[...], preferred_element_type=F32)
        p_out[:, c * ts:(c + 1) * ts] = jnp.where(mb, before, -1.0).astype(p_out.dtype)
        cnt_c = jnp.sum(maskf, axis=1, keepdims=True)
        start_c, end_c = extents(jnp.broadcast_to(cnt_c, (N_EXPERTS, LANES)),
                                 lambda u: jnp.dot(lt_ref[...], u, preferred_element_type=F32))
        col_out[c] = jnp.where(lane == 0, cnt_c, jnp.where(lane == 1, start_c, end_c))
        cnt_r = lax.dot_general(jnp.ones((SUBLANES, ts), BF16), mask16, NT_DIMS, preferred_element_type=F32)
        start_r, end_r = extents(cnt_r,
                                 lambda u: lax.dot_general(u, lt_ref[...], NT_DIMS, preferred_element_type=F32))
        row_out[c] = jnp.where(sub == 0, start_r, end_r)


def _route_call(xm, mod, norm_ffn, wrt, rbias, lower, ts, tiles_per_step):
    bsz, s, d = xm.shape
    t = bsz * s
    tm = ts * tiles_per_step
    nt = s // tm
    tri = (jnp.arange(ts)[:, None] < jnp.arange(ts)[None, :]).astype(BF16)
    tok = pl.BlockSpec((N_EXPERTS, tm), lambda i: (0, i))
    return pl.pallas_call(
        _route_kernel,
        grid=(t // tm,),
        in_specs=[pl.BlockSpec((1, tm, d), lambda i: (i // nt, i % nt, 0)),
                  pl.BlockSpec((1, SUBLANES, d), lambda i: (i // nt, 0, 0)),
                  _full(norm_ffn.shape), _full(wrt.shape), _full(rbias.shape), _full(tri.shape),
                  _full(lower.shape)],
        out_specs=[pl.BlockSpec((1, tm, d), lambda i: (i // nt, i % nt, 0)), tok, tok,
                   pl.BlockSpec((tiles_per_step, N_EXPERTS, LANES), lambda i: (i, 0, 0)),
                   pl.BlockSpec((tiles_per_step, SUBLANES, N_EXPERTS), lambda i: (i, 0, 0))],
        out_shape=[jax.ShapeDtypeStruct((bsz, s, d), BF16), jax.ShapeDtypeStruct((N_EXPERTS, t), F32),
                   jax.ShapeDtypeStruct((N_EXPERTS, t), BF16),
                   jax.ShapeDtypeStruct((t // ts, N_EXPERTS, LANES), F32),
                   jax.ShapeDtypeStruct((t // ts, SUBLANES, N_EXPERTS), F32)],
        compiler_params=_params("arbitrary"),
        name="route",
    )(xm, mod, norm_ffn, wrt, rbias, tri, lower)


def _pack(x):
    w = x.shape[1] // 2
    lo = lax.bitcast_convert_type(x[:, :w].astype(BF16).astype(F32), jnp.uint32)
    hi = lax.bitcast_convert_type(x[:, w:].astype(BF16).astype(F32), jnp.uint32)
    return hi | (lo >> 16)


def _unpack(u):
    lo = lax.bitcast_convert_type(u << 16, F32).astype(BF16)
    hi = lax.bitcast_convert_type(u & jnp.uint32(0xFFFF0000), F32).astype(BF16)
    return lo, hi


def _pow2_pieces(units, limit):
    bit = 1
    while bit * 2 <= limit:
        bit *= 2
    while bit:
        yield (units & bit) != 0, units & ~(2 * bit - 1), bit
        bit //= 2


def _rows_copy(vm_ref, hbm_ref, sem, vm_row, hbm_row, rows, to_hbm):
    v = vm_ref.at[pl.ds(pl.multiple_of(vm_row, RUN_ALIGN), rows), :]
    h = hbm_ref.at[pl.ds(pl.multiple_of(hbm_row, RUN_ALIGN), rows), :]
    return pltpu.make_async_copy(v, h, sem) if to_hbm else pltpu.make_async_copy(h, v, sem)


def _run_copies(vm_ref, hbm_ref, sem, n8, vm_row, hbm_row, limit, to_hbm, act):
    def emit(pieces):
        for on, off, size in pieces:
            @pl.when(on)
            def _():
                act(_rows_copy(vm_ref, hbm_ref, sem, vm_row + RUN_ALIGN * off, hbm_row + RUN_ALIGN * off,
                               RUN_ALIGN * size, to_hbm))

    pieces = list(_pow2_pieces(n8, limit))
    long_pieces = [p for p in pieces if p[2] >= LONG_RUN]
    if long_pieces:
        pl.when(n8 >= LONG_RUN)(lambda: emit(long_pieces))
    emit([p for p in pieces if p[2] < LONG_RUN])


def _wait_rows(vm_ref, hbm_ref, sem, units, limit, to_hbm):
    for on, _, size in _pow2_pieces(units, limit):
        @pl.when(on)
        def _():
            _rows_copy(vm_ref, hbm_ref, sem, 0, 0, RUN_ALIGN * size, to_hbm).wait()


def _dispatch_kernel(n8_ref, ls_ref, gs_ref, ts_ref, t8_ref, nu_ref, pos_ref, ext_ref, h_ref, xs_out, srt2, zbuf,
                     sems):
    step = pl.program_id(0)
    tm = h_ref.shape[0]
    rows = srt2.shape[1]
    slot = step % 2
    srt, sem = srt2.at[slot], sems.at[slot]
    rid = lax.broadcasted_iota(jnp.int32, (rows, 1), 0).astype(F32)
    start = ext_ref[0, 0:1, :]
    member = jnp.where((rid >= start) & (rid < ext_ref[0, 1:2, :]), 1.0, 0.0)
    offset = rid - jnp.sum(member * start, axis=1, keepdims=True)
    pos = jnp.dot(member.astype(BF16), pos_ref[...], preferred_element_type=F32)
    sel = jnp.where(pos == offset, 1.0, 0.0).astype(BF16)
    srt[...] = _pack(jnp.dot(sel, h_ref[...], preferred_element_type=F32))

    def send(e, c):
        i = step * N_EXPERTS + e
        _run_copies(srt, xs_out, sem, n8_ref[i], ls_ref[i], gs_ref[i], tm // RUN_ALIGN, True, lambda cp: cp.start())
        return c

    lax.fori_loop(0, N_EXPERTS, send, 0)

    def wait_tile(tile, s):
        last = tile * N_EXPERTS + N_EXPERTS - 1
        _wait_rows(srt2.at[s], xs_out, sems.at[s], ls_ref[last] // RUN_ALIGN + n8_ref[last], rows // RUN_ALIGN, True)

    pl.when(step > 0)(lambda: wait_tile(step - 1, 1 - slot))

    @pl.when(step == pl.num_programs(0) - 1)
    def _():
        wait_tile(step, slot)
        zbuf[...] = jnp.zeros(zbuf.shape, zbuf.dtype)
        nblk = xs_out.shape[0] // EXPERT_BLOCK

        def fill(act):
            def tails(e, c):
                _run_copies(zbuf, xs_out, sem, t8_ref[e], 0, ts_ref[e], EXPERT_BLOCK // RUN_ALIGN - 1, True, act)
                return c

            def blocks(b, c):
                act(pltpu.make_async_copy(
                    zbuf, xs_out.at[pl.ds(pl.multiple_of(b * EXPERT_BLOCK, EXPERT_BLOCK), EXPERT_BLOCK), :], sem))
                return c

            lax.fori_loop(0, N_EXPERTS, tails, 0)
            lax.fori_loop(nu_ref[0], nblk, blocks, 0)

        fill(lambda cp: cp.start())
        fill(lambda cp: cp.wait())


def _dispatch_call(tables, pos_et, ext_rows, h2, nblk, tm):
    t, d = h2.shape
    lrows = TOP_K * tm + N_EXPERTS * RUN_ALIGN
    return pl.pallas_call(
        _dispatch_kernel,
        grid_spec=pltpu.PrefetchScalarGridSpec(
            num_scalar_prefetch=len(tables), grid=(t // tm,),
            in_specs=[pl.BlockSpec((N_EXPERTS, tm), lambda i, *_: (0, i)),
                      pl.BlockSpec((1,) + ext_rows.shape[1:], lambda i, *_: (i, 0, 0)),
                      pl.BlockSpec((tm, d), lambda i, *_: (i, 0))],
            out_specs=pl.BlockSpec(memory_space=pl.ANY),
            scratch_shapes=[pltpu.VMEM((2, lrows, d // 2), jnp.uint32),
                            pltpu.VMEM((EXPERT_BLOCK, d // 2), jnp.uint32), pltpu.SemaphoreType.DMA((2,))]),
        out_shape=jax.ShapeDtypeStruct((nblk * EXPERT_BLOCK, d // 2), jnp.uint32),
        compiler_params=_params("arbitrary"),
        name="dispatch",
    )(*tables, pos_et, ext_rows, h2)


def _expert_kernel(blk_ref, nused_ref, x_ref, wgu_ref, wd_ref, y_ref):
    used = pl.program_id(0) < nused_ref[0]

    @pl.when(used)
    def _():
        lo, hi = _unpack(x_ref[...])
        half = lo.shape[1]
        gu = (jnp.dot(lo, wgu_ref[0, :half, :], preferred_element_type=F32)
              + jnp.dot(hi, wgu_ref[0, half:, :], preferred_element_type=F32))
        a = _silu(gu[:, :EXPERT_FF]) * gu[:, EXPERT_FF:]
        y_ref[...] = _pack(_dot(a, wd_ref[0]))

    @pl.when(jnp.logical_not(used))
    def _():
        y_ref[...] = jnp.zeros(y_ref.shape, y_ref.dtype)


def _expert_call(blk_e, nused, xs, wgu, wd):
    rows, d = xs.shape
    nblk = rows // EXPERT_BLOCK

    def row_map(i, blk, nu):
        return (jnp.minimum(i, nu[0] - 1), 0)

    return pl.pallas_call(
        _expert_kernel,
        grid_spec=pltpu.PrefetchScalarGridSpec(
            num_scalar_prefetch=2, grid=(nblk,),
            in_specs=[pl.BlockSpec((EXPERT_BLOCK, d), row_map),
                      pl.BlockSpec((1,) + wgu.shape[1:], lambda i, blk, nu: (blk[i], 0, 0)),
                      pl.BlockSpec((1,) + wd.shape[1:], lambda i, blk, nu: (blk[i], 0, 0))],
            out_specs=pl.BlockSpec((EXPERT_BLOCK, d), lambda i, blk, nu: (i, 0))),
        out_shape=jax.ShapeDtypeStruct((rows, d), jnp.uint32),
        compiler_params=_params("arbitrary"),
        name="expert",
    )(blk_e, nused, xs, wgu, wd)


def _combine_kernel(n8_ref, ls_ref, gs_ref, ys_hbm, pos_ref, w_ref, ext_ref, xm_ref, h_ref, mod_ref, wsgu_ref,
                    wsd_ref, fn_ref, o_ref, ybuf2, sems):
    step = pl.program_id(0)
    tm = xm_ref.shape[0]
    rows = ybuf2.shape[1]
    slot = step % 2
    ybuf, sem = ybuf2.at[slot], sems.at[slot]

    def fetch(tile, s):
        def body(e, c):
            i = tile * N_EXPERTS + e
            _run_copies(ybuf2.at[s], ys_hbm, sems.at[s], n8_ref[i], ls_ref[i], gs_ref[i], tm // RUN_ALIGN, False,
                        lambda cp: cp.start())
            return c
        lax.fori_loop(0, N_EXPERTS, body, 0)

    pl.when(step == 0)(lambda: fetch(step, slot))
    pl.when(step + 1 < pl.num_programs(0))(lambda: fetch(step + 1, 1 - slot))
    gu = _dot(h_ref[...], wsgu_ref[...])
    ff = gu.shape[1] // 2
    shared = _dot(_silu(gu[:, :ff]) * gu[:, ff:], wsd_ref[...])
    cid = lax.broadcasted_iota(jnp.int32, (1, rows), 1).astype(F32)
    start = ext_ref[0, :, 1:2]
    member = jnp.where((cid >= start) & (cid < ext_ref[0, :, 2:3]), 1.0, 0.0)
    offset = cid - jnp.sum(member * start, axis=0, keepdims=True)
    member = member.astype(BF16)
    pos = jnp.dot(pos_ref[...], member, preferred_element_type=F32)
    mix = jnp.where(pos == offset, jnp.dot(w_ref[...].astype(BF16), member, preferred_element_type=F32), 0.0)
    mix = mix.astype(BF16)
    last = step * N_EXPERTS + N_EXPERTS - 1
    filled = ls_ref[last] + RUN_ALIGN * n8_ref[last]
    _wait_rows(ybuf, ys_hbm, sem, filled // RUN_ALIGN, rows // RUN_ALIGN, False)
    rid = lax.broadcasted_iota(jnp.int32, (rows, 1), 0)
    lo, hi = _unpack(jnp.where(rid < filled, ybuf[...], jnp.uint32(0)))
    routed = jnp.concatenate([jnp.dot(mix, lo, preferred_element_type=F32),
                              jnp.dot(mix, hi, preferred_element_type=F32)], axis=1)
    x = xm_ref[...] + mod_ref[0, 5:6, :] * (routed + shared)
    o_ref[...] = _rms(x, fn_ref[...])


def _combine_call(tables, ys, pos_te, w_te, ext_cols, xm, h2, mod, wsgu, wsd, final_norm, tm, tiles_per_batch):
    t, d = xm.shape
    lrows = TOP_K * tm + N_EXPERTS * RUN_ALIGN
    tok = pl.BlockSpec((tm, d), lambda i, *_: (i, 0))
    per_e = pl.BlockSpec((tm, N_EXPERTS), lambda i, *_: (i, 0))
    return pl.pallas_call(
        _combine_kernel,
        grid_spec=pltpu.PrefetchScalarGridSpec(
            num_scalar_prefetch=len(tables), grid=(t // tm,),
            in_specs=[pl.BlockSpec(memory_space=pl.ANY), per_e, per_e,
                      pl.BlockSpec((1,) + ext_cols.shape[1:], lambda i, *_: (i, 0, 0)), tok, tok,
                      pl.BlockSpec((1, SUBLANES, d), lambda i, *_: (i // tiles_per_batch, 0, 0)),
                      _full(wsgu.shape), _full(wsd.shape), _full(final_norm.shape)],
            out_specs=tok,
            scratch_shapes=[pltpu.VMEM((2, lrows, d // 2), jnp.uint32), pltpu.SemaphoreType.DMA((2,))]),
        out_shape=jax.ShapeDtypeStruct((t, d), F32),
        compiler_params=_params("arbitrary"),
        name="combine",
    )(*tables, ys, pos_te, w_te, ext_cols, xm, h2, mod, wsgu, wsd, final_norm)


def _moe(xm, mod, norm_ffn, w_router, router_bias, wg, wu, wd, wsg, wsu, wsd, final_norm, tm):
    bsz, s, d = xm.shape
    t = bsz * s
    nt = t // tm
    perm = (np.arange(N_EXPERTS) % N_GROUPS) * GROUP_SIZE + np.arange(N_EXPERTS) // N_GROUPS
    wrt = w_router.T[perm]
    rbias = router_bias[perm][:, None]
    lower = jnp.asarray(perm[None, :] < perm[:, None], BF16)
    h2, w_et, pos_et, ext_cols, ext_rows = _route_call(xm, mod, norm_ffn, wrt, rbias, lower, tm, ROUTE_TILES)

    inv = np.argsort(perm)
    n8 = (ext_cols[:, :, 0].astype(jnp.int32)[:, inv] + (RUN_ALIGN - 1)) // RUN_ALIGN
    run = RUN_ALIGN * n8
    ls = jnp.cumsum(run, axis=1) - run
    tot = jnp.sum(run, axis=0)
    padded = (tot + EXPERT_BLOCK - 1) // EXPERT_BLOCK * EXPERT_BLOCK
    pad_end = jnp.cumsum(padded)
    gs = (pad_end - padded)[None, :] + jnp.cumsum(run, axis=0) - run
    nblk = -(-(t * TOP_K + nt * N_EXPERTS * (RUN_ALIGN - 1)) // EXPERT_BLOCK) + N_EXPERTS
    blk_first = jnp.arange(nblk, dtype=jnp.int32)[:, None] * EXPERT_BLOCK
    blk_e = jnp.minimum(jnp.sum((pad_end[None, :] <= blk_first).astype(jnp.int32), axis=1), N_EXPERTS - 1)
    nused = (pad_end[-1:] // EXPERT_BLOCK).astype(jnp.int32)
    tables = [a.reshape(-1).astype(jnp.int32) for a in (n8, ls, gs)]
    tails = [(pad_end - padded + tot).astype(jnp.int32), ((padded - tot) // RUN_ALIGN).astype(jnp.int32), nused]

    h2f = h2.reshape(t, d)
    xs = _dispatch_call(tables + tails, pos_et, ext_rows, h2f, nblk, tm)
    wgu = jnp.concatenate([wg, wu], axis=2).astype(BF16)
    ys = _expert_call(blk_e, nused, xs, wgu, wd.astype(BF16))
    wsgu = jnp.concatenate([wsg, wsu], axis=1).astype(BF16)
    out = _combine_call(tables, ys, pos_et.T, w_et.T, ext_cols, xm.reshape(t, d), h2f, mod, wsgu, wsd.astype(BF16),
                        final_norm, tm, s // tm)
    return out.reshape(bsz, s, d)


def _rope_tables(s):
    rows = s // GRID_W
    row = jnp.broadcast_to(jnp.arange(rows, dtype=F32)[:, None], (rows, GRID_W)).reshape(-1)
    col = jnp.broadcast_to(jnp.arange(GRID_W, dtype=F32)[None, :], (rows, GRID_W)).reshape(-1)
    half = QK_ROPE // 2
    inv_freq = ROPE_THETA ** (-jnp.arange(0, half, 2, dtype=F32) / half)
    ar, ac = row[:, None] * inv_freq, col[:, None] * inv_freq
    ones = jnp.ones((s, QK_NOPE), F32)
    tail = HEAD_PAD - QK_NOPE - QK_ROPE
    cos_t = jnp.concatenate([ones, jnp.cos(ar), jnp.cos(ar), jnp.cos(ac), jnp.cos(ac), jnp.ones((s, tail), F32)], 1)
    sin_t = jnp.concatenate([0 * ones, -jnp.sin(ar), jnp.sin(ar), -jnp.sin(ac), jnp.sin(ac),
                             jnp.zeros((s, tail), F32)], 1)
    return cos_t, sin_t


_Q4 = QK_ROPE // 4
ROPE_SWAP = np.concatenate([np.arange(_Q4, 2 * _Q4), np.arange(0, _Q4), np.arange(3 * _Q4, 4 * _Q4),
                            np.arange(2 * _Q4, 3 * _Q4)])


def _rope_slot(w, swap):
    if swap:
        w = w[..., ROPE_SWAP]
    pad = [(0, 0)] * (w.ndim - 1) + [(QK_NOPE, HEAD_PAD - QK_NOPE - QK_ROPE)]
    return jnp.pad(w, pad)


def kernel(x, c, ctx, c_ctx, w_mod, b_mod, norm_mix, norm_ffn, w_in, b_in, q_norm, w_uq, kv_norm, w_ukv, w_branch_attn, hy_conv_w, hy_conv_b, hy_filt_w1, hy_filt_b1, hy_filt_w2, hy_filt_b2, hy_filt_w3, hy_filt_freq, hy_skip, w_branch_hyena, w_out, w_router, router_bias, w_exp_gate, w_exp_up, w_exp_down, w_sh_gate, w_sh_up, w_sh_down, final_norm,
           tiles=None):
    bsz, s, d = x.shape
    tl = dict(inproj=512, tq=2048, tk=1408, fft_kb=8, merge=512, moe=256)
    tl.update(tiles or {})
    assert w_mod.shape[0] == 1, "single-layer trunk"
    i = 0

    rows = -(-(bsz + 1) // SUBLANES) * SUBLANES
    c_rows = jnp.pad(jnp.concatenate([c, c_ctx[None]], axis=0), ((0, rows - bsz - 1), (0, 0)))
    mod_all = _mod_call(c_rows, w_mod[i], b_mod[i])
    mod_all = jnp.pad(mod_all.reshape(rows, 6, d), ((0, 0), (0, SUBLANES - 6), (0, 0)))
    mod, modc = mod_all[:bsz], mod_all[bsz:bsz + 1]

    cuts = np.cumsum([Q_LORA, KV_LORA, QK_ROPE, 3 * HY_WIDTH])
    wi, bi = w_in[i], b_in[i][None]
    w_q, w_kv, w_pe, w_hy, w_g = jnp.split(wi, cuts, axis=1)
    b_q, b_kv, b_pe, b_hy, b_g = jnp.split(bi, cuts, axis=1)
    wa = jnp.concatenate([w_q, w_kv, _rope_slot(w_pe, False), _rope_slot(w_pe, True)], axis=1).astype(BF16)
    ba = jnp.concatenate([b_q, b_kv, _rope_slot(b_pe, False), _rope_slot(b_pe, True)], axis=1)
    wq3 = w_uq[i].reshape(Q_LORA, N_HEADS, QK_NOPE + QK_ROPE) * (ATTN_SCALE * math.log2(math.e))
    tail = ((0, 0), (0, 0), (0, HEAD_PAD - QK_NOPE))
    wuq = (jnp.pad(wq3[..., :QK_NOPE], tail) + _rope_slot(wq3[..., QK_NOPE:], False)).reshape(Q_LORA, -1).astype(BF16)
    wuqs = _rope_slot(wq3[..., QK_NOPE:], True).reshape(Q_LORA, -1).astype(BF16)
    wkv3 = w_ukv[i].reshape(KV_LORA, N_HEADS, QK_NOPE + V_HEAD)
    wuk = jnp.pad(wkv3[..., :QK_NOPE], tail).reshape(KV_LORA, -1).astype(BF16)
    wuvt = wkv3[..., QK_NOPE:].reshape(KV_LORA, -1).T.astype(BF16)
    nm, qn, kvn = norm_mix[i][None], q_norm[i][None], kv_norm[i][None]

    w_c = jnp.concatenate([w_kv, _rope_slot(w_pe, False)], axis=1).astype(BF16)
    b_c = jnp.concatenate([b_kv, _rope_slot(b_pe, False)], axis=1)
    ck, cvt = _ctx_call(ctx, modc, nm, w_c, b_c, kvn, wuk, wuvt)

    cos_t, sin_t = _rope_tables(s)
    q, k, vt, hv, hx1, hx2, gate = _inproj_call(
        x, mod, nm, wa, ba, w_hy.astype(BF16), b_hy, w_g.astype(BF16), b_g, qn, wuq, wuqs, kvn, wuk, wuvt,
        cos_t, sin_t, hy_conv_w[i], hy_conv_b[i][None], tl["inproj"])

    attn = _attn_call(q, jnp.concatenate([ck, k], axis=2), jnp.concatenate([cvt, vt], axis=3), tl["tq"], tl["tk"])
    hy = _hyena(hv, hx1, hx2, hy_filt_w1[i], hy_filt_b1[i], hy_filt_w2[i], hy_filt_b2[i], hy_filt_w3[i],
                hy_filt_freq[i], hy_skip[i], tl["fft_kb"])
    xm = _merge_call(x, attn, hy, gate, mod, w_branch_attn[i].astype(BF16), w_branch_hyena[i].astype(BF16),
                     w_out[i].astype(BF16), tl["merge"])
    return _moe(xm, mod, norm_ffn[i][None], w_router[i], router_bias[i], w_exp_gate[i], w_exp_up[i], w_exp_down[i],
                w_sh_gate[i], w_sh_up[i], w_sh_down[i], final_norm[None], tl["moe"])
```

```python
import functools
import math

import numpy as np
import jax
import jax.numpy as jnp
from jax import lax
from jax.experimental import pallas as pl
from jax.experimental.pallas import tpu as pltpu

GRID_W = 64
N_HEADS = 8
QK_NOPE = 64
QK_ROPE = 32
V_HEAD = 64
Q_LORA = 256
KV_LORA = 128
ROPE_THETA = 10000.0
ATTN_SCALE = 1.0 / math.sqrt(QK_NOPE + QK_ROPE)
HY_WIDTH = 512
HY_ORDER = 2
HY_SHORT = 3
HY_BANDS = 8
HY_EMB = 1 + 2 * HY_BANDS
HY_EMB_PAD = 32
HY_FAST_DECAY = 0.3
HY_SLOW_DECAY = 1.5
HY_DECAY_TARGET = 1e-2
N_EXPERTS = 64
N_GROUPS = 8
GROUP_SIZE = N_EXPERTS // N_GROUPS
TOPK_GROUPS = 4
TOP_K = 8
EXPERT_FF = 256
ROUTE_SCALE = 2.5
EXPERT_BLOCK = 1024
RUN_ALIGN = 8
ROUTE_TILES = 8
LONG_RUN = 8
NORM_EPS = 1e-6

HEAD_PAD = 128
Q_CHUNK = 512
AHEAD = 2
LANES = 128
SUBLANES = 8
VMEM_LIMIT = 48 * 1024 * 1024

F32 = jnp.float32
BF16 = jnp.bfloat16
NT_DIMS = (((1,), (1,)), ((), ()))
NN_DIMS = (((1,), (0,)), ((), ()))


def _params(*sem):
    return pltpu.CompilerParams(dimension_semantics=sem, vmem_limit_bytes=VMEM_LIMIT)


def _dot(a, b):
    return jnp.dot(a.astype(BF16), b.astype(BF16), preferred_element_type=F32)


def _split(a):
    hi = a.astype(BF16)
    lo = (a - hi.astype(F32)).astype(BF16)
    return hi, lo


def _dot3(a, b, dims=NN_DIMS):
    ah, al = _split(a)
    bh, bl = _split(b)
    d = functools.partial(lax.dot_general, dimension_numbers=dims, preferred_element_type=F32)
    return d(ah, bh) + (d(ah, bl) + d(al, bh))


def _rms(x, g):
    return x * lax.rsqrt(jnp.mean(x * x, axis=-1, keepdims=True) + NORM_EPS) * g


def _silu(x):
    return x * jax.nn.sigmoid(x)


def _full(shape):
    nd = len(shape)
    return pl.BlockSpec(shape, lambda *_: (0,) * nd)


def _mod_kernel(c_ref, w_ref, b_ref, o_ref):
    o_ref[...] = _dot3(_silu(c_ref[...]), w_ref[...]) + b_ref[...]


def _mod_call(c_rows, w_mod, b_mod):
    r, d = c_rows.shape
    n = w_mod.shape[1]
    bn = 1024
    return pl.pallas_call(
        _mod_kernel,
        grid=(n // bn,),
        in_specs=[_full((r, d)), pl.BlockSpec((d, bn), lambda j: (0, j)), pl.BlockSpec((1, bn), lambda j: (0, j))],
        out_specs=pl.BlockSpec((r, bn), lambda j: (0, j)),
        out_shape=jax.ShapeDtypeStruct((r, n), F32),
        compiler_params=_params("arbitrary"),
        name="mod",
    )(c_rows, w_mod, b_mod.reshape(1, n))


def _prenorm(x, mod_ref, row, g):
    shift = mod_ref[0, row:row + 1, :]
    scale = mod_ref[0, row + 1:row + 2, :]
    return _rms(x, g) * (1.0 + scale) + shift


def _kv_heads(kv_lat, kpe, kvn_ref, wuk_ref, wuvt_ref, k_out, vt_out):
    kvn = _rms(kv_lat, kvn_ref[...]).astype(BF16)
    kk = _dot(kvn, wuk_ref[...])
    vt = lax.dot_general(wuvt_ref[...], kvn, NT_DIMS, preferred_element_type=F32)
    ones = jnp.ones((HEAD_PAD - V_HEAD, vt.shape[1]), F32)
    for h in range(N_HEADS):
        k_out[0, h] = (kk[:, HEAD_PAD * h:HEAD_PAD * (h + 1)] + kpe).astype(BF16)
        vt_out[0, h] = jnp.concatenate([vt[V_HEAD * h:V_HEAD * (h + 1)], ones], axis=0).astype(BF16)


def _ctx_kernel(c_ref, mod_ref, nm_ref, w_ref, b_ref, kvn_ref, wuk_ref, wuv_ref, k_out, v_out):
    h = _prenorm(c_ref[0], mod_ref, 0, nm_ref[...]).astype(BF16)
    a = _dot(h, w_ref[...]) + b_ref[...]
    _kv_heads(a[:, :KV_LORA], a[:, KV_LORA:], kvn_ref, wuk_ref, wuv_ref, k_out, v_out)


def _ctx_call(ctx, modc, norm_mix, w_c, b_c, kv_norm, w_uk, w_uv):
    bsz, n, d = ctx.shape
    return pl.pallas_call(
        _ctx_kernel,
        grid=(bsz,),
        in_specs=[pl.BlockSpec((1, n, d), lambda b: (b, 0, 0)), _full(modc.shape), _full(norm_mix.shape),
                  _full(w_c.shape), _full(b_c.shape), _full(kv_norm.shape), _full(w_uk.shape), _full(w_uv.shape)],
        out_specs=[pl.BlockSpec((1, N_HEADS, n, HEAD_PAD), lambda b: (b, 0, 0, 0)),
                   pl.BlockSpec((1, N_HEADS, HEAD_PAD, n), lambda b: (b, 0, 0, 0))],
        out_shape=[jax.ShapeDtypeStruct((bsz, N_HEADS, n, HEAD_PAD), BF16),
                   jax.ShapeDtypeStruct((bsz, N_HEADS, HEAD_PAD, n), BF16)],
        compiler_params=_params("arbitrary"),
        name="ctx",
    )(ctx, modc, norm_mix, w_c, b_c, kv_norm, w_uk, w_uv)


def _inproj_kernel(x_ref, xp_ref, xn_ref, mod_ref, nm_ref, wa_ref, ba_ref, why_ref, bhy_ref, wg_ref, bg_ref,
                   qn_ref, wuq_ref, wuqs_ref, kvn_ref, wuk_ref, wuv_ref, cos_ref, sin_ref, cw_ref, cb_ref,
                   q_out, k_out, v_out, hv_out, hx1_out, hx2_out, g_out):
    i = pl.program_id(0)
    tm = x_ref.shape[1]
    nm = nm_ref[...]
    h = _prenorm(x_ref[0], mod_ref, 0, nm).astype(BF16)
    a = _dot(h, wa_ref[...]) + ba_ref[...]
    q_lat = a[:, :Q_LORA]
    kv_lat = a[:, Q_LORA:Q_LORA + KV_LORA]
    kpe_m = a[:, Q_LORA + KV_LORA:Q_LORA + KV_LORA + HEAD_PAD]
    kpe_s = a[:, Q_LORA + KV_LORA + HEAD_PAD:]
    cos = cos_ref[...]
    sin = sin_ref[...]
    qn = _rms(q_lat, qn_ref[...]).astype(BF16)
    qa = _dot(qn, wuq_ref[...])
    qs = _dot(qn, wuqs_ref[...])
    for hh in range(N_HEADS):
        sl = slice(HEAD_PAD * hh, HEAD_PAD * (hh + 1))
        q_out[0, hh] = (qa[:, sl] * cos + qs[:, sl] * sin).astype(BF16)
    _kv_heads(kv_lat, kpe_m * cos + kpe_s * sin, kvn_ref, wuk_ref, wuv_ref, k_out, v_out)
    g_out[0] = (_dot(h, wg_ref[...]) + bg_ref[...]).astype(BF16)

    why = why_ref[...]
    bhy = bhy_ref[...]
    halo = jnp.concatenate([_prenorm(xp_ref[0], mod_ref, 0, nm), _prenorm(xn_ref[0], mod_ref, 0, nm)], axis=0)
    hy_all = _dot(jnp.concatenate([h, halo.astype(BF16)], axis=0), why) + bhy
    hy = hy_all[:tm]
    prev = jnp.where(i == 0, 0.0, hy_all[tm + SUBLANES - 1:tm + SUBLANES])
    nxt = jnp.where(i == pl.num_programs(0) - 1, 0.0, hy_all[tm + SUBLANES:tm + SUBLANES + 1])
    rid = lax.broadcasted_iota(jnp.int32, (tm, 1), 0)
    up = jnp.where(rid == 0, prev, pltpu.roll(hy, 1, 0))
    dn = jnp.where(rid == tm - 1, nxt, pltpu.roll(hy, tm - 1, 0))
    u = up * cw_ref[0:1, :] + hy * cw_ref[1:2, :] + dn * cw_ref[2:3, :] + cb_ref[...]
    hv_out[0] = u[:, :HY_WIDTH]
    hx1_out[0] = u[:, HY_WIDTH:2 * HY_WIDTH]
    hx2_out[0] = u[:, 2 * HY_WIDTH:]


def _inproj_call(x, mod, norm_mix, wa, ba, why, bhy, wg, bg, q_norm, wuq, wuqs, kv_norm, wuk, wuvt, cos_t, sin_t, cw,
                 cb, tm):
    bsz, s, d = x.shape
    nt = s // tm
    rb = tm // SUBLANES
    last_rb = s // SUBLANES - 1
    consts = [norm_mix, wa, ba, why, bhy, wg, bg, q_norm, wuq, wuqs, kv_norm, wuk, wuvt]
    in_specs = [
        pl.BlockSpec((1, tm, d), lambda i, b: (b, i, 0)),
        pl.BlockSpec((1, SUBLANES, d), lambda i, b: (b, jnp.maximum(i * rb - 1, 0), 0)),
        pl.BlockSpec((1, SUBLANES, d), lambda i, b: (b, jnp.minimum((i + 1) * rb, last_rb), 0)),
        pl.BlockSpec((1, SUBLANES, d), lambda i, b: (b, 0, 0)),
    ] + [_full(c.shape) for c in consts] + [
        pl.BlockSpec((tm, HEAD_PAD), lambda i, b: (i, 0)),
        pl.BlockSpec((tm, HEAD_PAD), lambda i, b: (i, 0)),
        _full(cw.shape), _full(cb.shape),
    ]
    hw = HY_WIDTH
    out_specs = [
        pl.BlockSpec((1, N_HEADS, tm, HEAD_PAD), lambda i, b: (b, 0, i, 0)),
        pl.BlockSpec((1, N_HEADS, tm, HEAD_PAD), lambda i, b: (b, 0, i, 0)),
        pl.BlockSpec((1, N_HEADS, HEAD_PAD, tm), lambda i, b: (b, 0, 0, i)),
        pl.BlockSpec((1, tm, hw), lambda i, b: (b, i, 0)),
        pl.BlockSpec((1, tm, hw), lambda i, b: (b, i, 0)),
        pl.BlockSpec((1, tm, hw), lambda i, b: (b, i, 0)),
        pl.BlockSpec((1, tm, 2 * d), lambda i, b: (b, i, 0)),
    ]
    out_shape = [
        jax.ShapeDtypeStruct((bsz, N_HEADS, s, HEAD_PAD), BF16),
        jax.ShapeDtypeStruct((bsz, N_HEADS, s, HEAD_PAD), BF16),
        jax.ShapeDtypeStruct((bsz, N_HEADS, HEAD_PAD, s), BF16),
        jax.ShapeDtypeStruct((bsz, s, hw), F32),
        jax.ShapeDtypeStruct((bsz, s, hw), F32),
        jax.ShapeDtypeStruct((bsz, s, hw), F32),
        jax.ShapeDtypeStruct((bsz, s, 2 * d), BF16),
    ]
    return pl.pallas_call(
        _inproj_kernel,
        grid=(nt, bsz),
        in_specs=in_specs,
        out_specs=out_specs,
        out_shape=out_shape,
        compiler_params=_params("arbitrary", "arbitrary"),
        name="inproj",
    )(x, x, x, mod, *consts, cos_t, sin_t, cw, cb)


def _attn_kernel(q_ref, k_ref, vt_ref, o_ref, m_sc, acc_sc):
    j = pl.program_id(2)

    @pl.when(j == 0)
    def _():
        m_sc[...] = jnp.full(m_sc.shape, -jnp.inf, F32)
        acc_sc[...] = jnp.zeros(acc_sc.shape, F32)

    tq = q_ref.shape[2]
    qw = min(tq, Q_CHUNK)
    units = [(h, c) for h in range(N_HEADS) for c in range(0, tq, qw)]

    def scores(u):
        h, c = units[u]
        return lax.dot_general(k_ref[0, h], q_ref[0, h, c:c + qw, :], NT_DIMS,
                               preferred_element_type=F32)

    pending = [scores(u) for u in range(AHEAD)]
    for u, (h, c) in enumerate(units):
        if u + AHEAD < len(units):
            pending.append(scores(u + AHEAD))
        st = pending.pop(0)
        m_prev = m_sc[h, :, c:c + qw]
        m_new = jnp.maximum(m_prev, jnp.max(st, axis=0, keepdims=True))
        pt = jnp.exp2(st - m_new).astype(BF16)
        acc_sc[h, :, c:c + qw] = (jnp.exp2(m_prev - m_new) * acc_sc[h, :, c:c + qw]
                                  + jnp.dot(vt_ref[0, h], pt, preferred_element_type=F32))
        m_sc[h, :, c:c + qw] = m_new

    @pl.when(j == pl.num_programs(2) - 1)
    def _():
        ot = jnp.concatenate([acc_sc[h, :V_HEAD] / acc_sc[h, V_HEAD:V_HEAD + 1] for h in range(N_HEADS)], axis=0)
        o_ref[0] = ot.T.astype(o_ref.dtype)


def _attn_call(q, k, vt, tq, tk):
    bsz, nh, s, dh = q.shape
    nk = k.shape[2]
    dv = nh * V_HEAD
    return pl.pallas_call(
        _attn_kernel,
        grid=(bsz, s // tq, nk // tk),
        in_specs=[
            pl.BlockSpec((1, nh, tq, dh), lambda b, i, j: (b, 0, i, 0)),
            pl.BlockSpec((1, nh, tk, dh), lambda b, i, j: (b, 0, j, 0)),
            pl.BlockSpec((1, nh, dh, tk), lambda b, i, j: (b, 0, 0, j)),
        ],
        out_specs=pl.BlockSpec((1, tq, dv), lambda b, i, j: (b, i, 0)),
        out_shape=jax.ShapeDtypeStruct((bsz, s, dv), BF16),
        scratch_shapes=[pltpu.VMEM((nh, 1, tq), F32), pltpu.VMEM((nh, dh, tq), F32)],
        compiler_params=_params("arbitrary", "arbitrary", "arbitrary"),
        name="attn",
    )(q, k, vt)


def _filter_kernel(emb_ref, w1_ref, b1_ref, w2_ref, b2_ref, w3_ref, fr_ref, dl_ref, full_out, asum_out, *, seq):
    r = pl.program_id(0)
    rb = emb_ref.shape[0]
    emb = emb_ref[...]
    fr = fr_ref[...]
    h = jnp.sin(fr * (_dot3(emb, w1_ref[...]) + b1_ref[...]))
    h = jnp.sin(fr * (_dot3(h, w2_ref[...]) + b2_ref[...]))
    k = _dot3(h, w3_ref[0]) * jnp.exp(-emb[:, 0:1] * dl_ref[...])
    row = r * rb + lax.broadcasted_iota(jnp.int32, (rb, 1), 0)
    k = jnp.where(row == seq, 0.0, k)
    full_out[...] = k

    @pl.when(r == 0)
    def _():
        asum_out[...] = jnp.zeros(asum_out.shape, F32)

    asum_out[...] += jnp.sum(jnp.abs(k), axis=0, keepdims=True)


def _filter_call(emb, w1, b1, w2, b2, w3sel, freq, deltas2, seq, rb):
    n2 = emb.shape[0]
    half_blocks = seq // rb
    width = w3sel.shape[2]
    return pl.pallas_call(
        functools.partial(_filter_kernel, seq=seq),
        grid=(n2 // rb,),
        in_specs=[pl.BlockSpec((rb, HY_EMB_PAD), lambda r: (r, 0)), _full(w1.shape), _full(b1.shape),
                  _full(w2.shape), _full(b2.shape),
                  pl.BlockSpec((1,) + w3sel.shape[1:], lambda r: (r // half_blocks, 0, 0)),
                  _full(freq.shape), _full(deltas2.shape)],
        out_specs=[pl.BlockSpec((rb, width), lambda r: (r, 0)), pl.BlockSpec((1, width), lambda r: (0, 0))],
        out_shape=[jax.ShapeDtypeStruct((n2, width), F32), jax.ShapeDtypeStruct((1, width), F32)],
        compiler_params=_params("arbitrary"),
        name="filt",
    )(emb, w1, b1, w2, b2, w3sel, freq, deltas2)


def _fa_kernel(u_ref, f_ref, a_out):
    two, _, hn, g, c = u_ref.shape
    a = _dot(f_ref[...], u_ref[...].reshape(two * hn * g, c))
    a_out[...] = (_pack(a) if a_out.dtype == jnp.uint32 else a).reshape(a_out.shape)


def _fa_call(u5, fmat, packed):
    _, p, hn, n, c = u5.shape
    g = SUBLANES
    co, dt = (c // 2, jnp.uint32) if packed else (c, F32)
    return pl.pallas_call(
        _fa_kernel,
        grid=(p, n // g),
        in_specs=[pl.BlockSpec((2, 1, hn, g, c), lambda q, j: (0, q, 0, j, 0)), _full(fmat.shape)],
        out_specs=pl.BlockSpec((1, 2, n, g, co), lambda q, j: (q, 0, 0, j, 0)),
        out_shape=jax.ShapeDtypeStruct((p, 2, n, n, co), dt),
        compiler_params=_params("arbitrary", "arbitrary"),
        name="fa",
    )(u5, fmat)


def _dot_packed(w, u):
    lo, hi = _unpack(u)
    return jnp.concatenate([jnp.dot(w, lo, preferred_element_type=F32), jnp.dot(w, hi, preferred_element_type=F32)],
                           axis=1)


def _fb_kernel(a_ref, g_ref, asum_ref, kf_out):
    _, two, kb, n, c = a_ref.shape
    scale = 1.0 / (asum_ref[...] + 1e-6)
    for kk in range(kb):
        x = _dot(g_ref[kk], a_ref[0, :, kk].reshape(two * n, c)) * scale
        kf_out[kk] = x.reshape(two, n, c)


def _fb_call(a5, gmat, asum, kb):
    _, _, n, _, c = a5.shape
    return pl.pallas_call(
        _fb_kernel,
        grid=(n // kb,),
        in_specs=[pl.BlockSpec((1, 2, kb, n, c), lambda k: (0, 0, k, 0, 0)),
                  pl.BlockSpec((kb, 2 * n, 2 * n), lambda k: (k, 0, 0)), _full(asum.shape)],
        out_specs=pl.BlockSpec((kb, 2, n, c), lambda k: (k, 0, 0, 0)),
        out_shape=jax.ShapeDtypeStruct((n, 2, n, c), F32),
        compiler_params=_params("arbitrary"),
        name="fb",
    )(a5, gmat, asum)


def _mid_kernel(a_ref, g_ref, h_ref, kf_ref, b_out):
    _, two, kb, n, c = a_ref.shape
    for kk in range(kb):
        x = _dot_packed(g_ref[kk], a_ref[0, :, kk].reshape(two * n, c))
        xr, xi = x[:n], x[n:]
        kr, ki = kf_ref[kk, 0], kf_ref[kk, 1]
        y = jnp.concatenate([xr * kr - xi * ki, xr * ki + xi * kr], axis=0)
        b_out[0, :, kk] = _pack(_dot(h_ref[kk], y)).reshape(two, n, c)


def _mid_call(a5, gmat, hmat, kf, order, kb):
    p, _, n, _, c = a5.shape
    return pl.pallas_call(
        _mid_kernel,
        grid=(n // kb, p),
        in_specs=[pl.BlockSpec((1, 2, kb, n, c), lambda k, q: (q, 0, k, 0, 0)),
                  pl.BlockSpec((kb, 2 * n, 2 * n), lambda k, q: (k, 0, 0)),
                  pl.BlockSpec((kb, 2 * n, 2 * n), lambda k, q: (k, 0, 0)),
                  pl.BlockSpec((kb, 2, n, 2 * c), lambda k, q: (k, 0, 0, order))],
        out_specs=pl.BlockSpec((1, 2, kb, n, c), lambda k, q: (q, 0, k, 0, 0)),
        out_shape=jax.ShapeDtypeStruct(a5.shape, jnp.uint32),
        compiler_params=_params("arbitrary", "arbitrary"),
        name="mid",
    )(a5, gmat, hmat, kf)


def _fc_kernel(b_ref, f_ref, u_ref, m_ref, skip_ref, o_out):
    _, two, n, g, c = b_ref.shape
    y = _dot_packed(f_ref[...], b_ref[...].reshape(two * n * g, c)).reshape(u_ref.shape)
    o_out[...] = m_ref[...] * (y + u_ref[...] * skip_ref[...])


def _fc_call(b5, finv, u5, m5, skip_row):
    _, p, hn, n, c = u5.shape
    g = SUBLANES
    blk = pl.BlockSpec((2, 1, hn, g, c), lambda q, j: (0, q, 0, j, 0))
    return pl.pallas_call(
        _fc_kernel,
        grid=(p, n // g),
        in_specs=[pl.BlockSpec((1, 2, n, g, c // 2), lambda q, j: (q, 0, 0, j, 0)), _full(finv.shape), blk, blk,
                  _full(skip_row.shape)],
        out_specs=blk,
        out_shape=jax.ShapeDtypeStruct(u5.shape, F32),
        compiler_params=_params("arbitrary", "arbitrary"),
        name="fc",
    )(b5, finv, u5, m5, skip_row)


def _dft_tables(n):
    hn = n // 2
    k = np.arange(n)[:, None]
    ang = -2.0 * np.pi * (k * np.arange(n)[None, :] % n) / n
    fr, fi = np.cos(ang), np.sin(ang)
    f_data = np.block([[fr[:, :hn], -fi[:, :hn]], [fi[:, :hn], fr[:, :hn]]])
    f_filt = np.concatenate([fr, fi], axis=0)
    er, ei = fr[:hn], -fi[:hn]
    f_inv = np.block([[er, -ei], [ei, er]]) / float(n * n)
    k1 = jnp.arange(n, dtype=jnp.int32)[:, None, None]
    k2 = jnp.arange(n, dtype=jnp.int32)[None, :, None]
    m2 = jnp.arange(n, dtype=jnp.int32)[None, None, :]
    idx = (m2 * (k1 + n * k2)) % (n * n)
    ang2 = idx.astype(F32) * (-2.0 * math.pi / (n * n))
    gr, gi = jnp.cos(ang2), jnp.sin(ang2)
    g = jnp.concatenate([jnp.concatenate([gr, -gi], axis=2), jnp.concatenate([gi, gr], axis=2)], axis=1)
    h = jnp.swapaxes(g, 1, 2)

    def widen(f):
        return jnp.asarray(np.kron(f, np.eye(SUBLANES)), BF16)

    return widen(f_data), widen(f_filt), widen(f_inv), g.astype(BF16), h.astype(BF16)


def _hyena_filter_tables(seq):
    t = jnp.linspace(0.0, 1.0, seq, dtype=F32)[:, None]
    w = 2.0 * math.pi * jnp.arange(seq, dtype=F32)[:, None] / seq
    f = jnp.linspace(1e-4, HY_BANDS - 1, HY_BANDS, dtype=F32)[None, :]
    emb = jnp.concatenate([t, jnp.cos(f * w), -jnp.sin(f * w)], axis=-1)
    emb = jnp.concatenate([emb, emb[:1], emb[:0:-1]], axis=0)
    emb = jnp.pad(emb, ((0, 0), (0, HY_EMB_PAD - HY_EMB)))
    deltas = jnp.abs(jnp.linspace(math.log(HY_DECAY_TARGET) / HY_SLOW_DECAY,
                                  math.log(HY_DECAY_TARGET) / HY_FAST_DECAY, HY_WIDTH, dtype=F32))
    return emb, jnp.tile(deltas, HY_ORDER)[None, :]


def _hyena(hv, hx1, hx2, w1, b1, w2, b2, w3, freq, skip, kb):
    bsz, seq, c = hv.shape
    n = int(round(math.sqrt(2 * seq)))
    assert n * n == 2 * seq and bsz % 2 == 0
    hn, p = n // 2, bsz // 2
    f_data, f_filt, f_inv, gmat, hmat = _dft_tables(n)

    emb, deltas2 = _hyena_filter_tables(seq)
    w1p = jnp.pad(w1, ((0, HY_EMB_PAD - HY_EMB), (0, 0)))
    w3r = w3.reshape(w3.shape[0], HY_ORDER, 2, c)
    w3sel = jnp.stack([w3r[:, :, 0, :].reshape(-1, HY_ORDER * c), w3r[:, :, 1, :].reshape(-1, HY_ORDER * c)])
    full, asum = _filter_call(emb, w1p, b1[None], w2, b2[None], w3sel, freq[None], deltas2, seq, min(512, seq))
    c2 = HY_ORDER * c
    kf = _fb_call(_fa_call(full.reshape(2, 1, hn, n, c2), f_filt, False), gmat, asum, kb // HY_ORDER)

    def view(t):
        return t.reshape(2, p, hn, n, c)

    def long_conv(u5, m5, order):
        bm = _mid_call(_fa_call(u5, f_data, True), gmat, hmat, kf, order, kb)
        return _fc_call(bm, f_inv, u5, m5, skip[order][None, :])

    z = long_conv(view(hv), view(hx1), 0)
    return long_conv(z, view(hx2), 1).reshape(bsz, seq, c)


def _merge_kernel(x_ref, at_ref, hy_ref, g_ref, mod_ref, wba_ref, wbh_ref, wo_ref, o_ref):
    d = x_ref.shape[2]
    g = g_ref[0].astype(F32)
    y = (jax.nn.sigmoid(g[:, :d]) * _dot(at_ref[0], wba_ref[...])
         + jax.nn.sigmoid(g[:, d:]) * _dot(hy_ref[0], wbh_ref[...]))
    o_ref[0] = x_ref[0] + mod_ref[0, 2:3, :] * _dot(y, wo_ref[...])


def _merge_call(x, attn, hy, gate, mod, wba, wbh, wo, tm):
    bsz, s, d = x.shape

    def tok(w):
        return pl.BlockSpec((1, tm, w), lambda b, i: (b, i, 0))

    return pl.pallas_call(
        _merge_kernel,
        grid=(bsz, s // tm),
        in_specs=[tok(d), tok(attn.shape[2]), tok(hy.shape[2]), tok(2 * d),
                  pl.BlockSpec((1, SUBLANES, d), lambda b, i: (b, 0, 0)),
                  _full(wba.shape), _full(wbh.shape), _full(wo.shape)],
        out_specs=tok(d),
        out_shape=jax.ShapeDtypeStruct((bsz, s, d), F32),
        compiler_params=_params("arbitrary", "arbitrary"),
        name="merge",
    )(x, attn, hy, gate, mod, wba, wbh, wo)


def _route_kernel(xm_ref, mod_ref, nf_ref, wrt_ref, rb_ref, tri_ref, lt_ref, h2_out, w_out, p_out, col_out, row_out):
    tm = xm_ref.shape[1]
    ng, gs = N_GROUPS, GROUP_SIZE

    h2 = _prenorm(xm_ref[0], mod_ref, 3, nf_ref[...])
    h2_out[0] = h2.astype(h2_out.dtype)
    scores = jax.nn.sigmoid(_dot3(wrt_ref[...], h2, NT_DIMS))
    sel = scores + rb_ref[...]
    slabs = [sel[ng * j:ng * (j + 1)] for j in range(gs)]

    top1 = jnp.full((ng, tm), -jnp.inf, F32)
    top2 = top1
    for x in slabs:
        top2 = jnp.maximum(top2, jnp.minimum(top1, x))
        top1 = jnp.maximum(top1, x)
    gscore = top1 + top2
    gid = lax.broadcasted_iota(jnp.int32, (ng, 1), 0)
    rank = jnp.zeros((ng, tm), jnp.int32)
    for g2 in range(ng):
        row = gscore[g2:g2 + 1]
        beats = (row > gscore) | ((row == gscore) & (g2 < gid))
        rank = rank + beats.astype(jnp.int32)
    gmask = rank < TOPK_GROUPS

    cand = [jnp.where(gmask, x, -jnp.inf) for x in slabs]
    eid = [gid * gs + j for j in range(gs)]
    chosen = []
    for _ in range(TOP_K):
        best = functools.reduce(jnp.maximum, cand)
        best = jnp.max(best, axis=0, keepdims=True)
        idx = functools.reduce(jnp.minimum, [jnp.where(cand[j] == best, eid[j], N_EXPERTS) for j in range(gs)])
        idx = jnp.min(idx, axis=0, keepdims=True)
        chosen.append(idx)
        cand = [jnp.where(eid[j] == idx, -jnp.inf, cand[j]) for j in range(gs)]

    mask = [functools.reduce(jnp.logical_or, [eid[j] == idx for idx in chosen]) for j in range(gs)]
    maskb = jnp.concatenate(mask, axis=0)
    wsel = jnp.where(maskb, scores, 0.0)
    w_out[...] = wsel / jnp.sum(wsel, axis=0, keepdims=True) * ROUTE_SCALE

    def extents(cnt, lower_sum):
        units = jnp.floor((cnt + (RUN_ALIGN - 1)) * (1.0 / RUN_ALIGN))
        start = RUN_ALIGN * lower_sum(units.astype(BF16))
        return start, start + RUN_ALIGN * units

    lane = lax.broadcasted_iota(jnp.int32, (N_EXPERTS, LANES), 1)
    sub = lax.broadcasted_iota(jnp.int32, (SUBLANES, N_EXPERTS), 0)
    ts = tri_ref.shape[0]
    for c in range(tm // ts):
        mb = maskb[:, c * ts:(c + 1) * ts]
        maskf = jnp.where(mb, 1.0, 0.0)
        mask16 = maskf.astype(BF16)
        before = jnp.dot(mask16, tri_ref[...], preferred_element_type=F32)
        p_out[:, c * ts:(c + 1) * ts] = jnp.where(mb, before, -1.0).astype(p_out.dtype)
        cnt_c = jnp.sum(maskf, axis=1, keepdims=True)
        start_c, end_c = extents(jnp.broadcast_to(cnt_c, (N_EXPERTS, LANES)),
                                 lambda u: jnp.dot(lt_ref[...], u, preferred_element_type=F32))
        col_out[c] = jnp.where(lane == 0, cnt_c, jnp.where(lane == 1, start_c, end_c))
        cnt_r = lax.dot_general(jnp.ones((SUBLANES, ts), BF16), mask16, NT_DIMS, preferred_element_type=F32)
        start_r, end_r = extents(cnt_r,
                                 lambda u: lax.dot_general(u, lt_ref[...], NT_DIMS, preferred_element_type=F32))
        row_out[c] = jnp.where(sub == 0, start_r, end_r)


def _route_call(xm, mod, norm_ffn, wrt, rbias, lower, ts, tiles_per_step):
    bsz, s, d = xm.shape
    t = bsz * s
    tm = ts * tiles_per_step
    nt = s // tm
    tri = (jnp.arange(ts)[:, None] < jnp.arange(ts)[None, :]).astype(BF16)
    tok = pl.BlockSpec((N_EXPERTS, tm), lambda i: (0, i))
    return pl.pallas_call(
        _route_kernel,
        grid=(t // tm,),
        in_specs=[pl.BlockSpec((1, tm, d), lambda i: (i // nt, i % nt, 0)),
                  pl.BlockSpec((1, SUBLANES, d), lambda i: (i // nt, 0, 0)),
                  _full(norm_ffn.shape), _full(wrt.shape), _full(rbias.shape), _full(tri.shape),
                  _full(lower.shape)],
        out_specs=[pl.BlockSpec((1, tm, d), lambda i: (i // nt, i % nt, 0)), tok, tok,
                   pl.BlockSpec((tiles_per_step, N_EXPERTS, LANES), lambda i: (i, 0, 0)),
                   pl.BlockSpec((tiles_per_step, SUBLANES, N_EXPERTS), lambda i: (i, 0, 0))],
        out_shape=[jax.ShapeDtypeStruct((bsz, s, d), BF16), jax.ShapeDtypeStruct((N_EXPERTS, t), F32),
                   jax.ShapeDtypeStruct((N_EXPERTS, t), BF16),
                   jax.ShapeDtypeStruct((t // ts, N_EXPERTS, LANES), F32),
                   jax.ShapeDtypeStruct((t // ts, SUBLANES, N_EXPERTS), F32)],
        compiler_params=_params("arbitrary"),
        name="route",
    )(xm, mod, norm_ffn, wrt, rbias, tri, lower)


def _pack(x):
    w = x.shape[1] // 2
    lo = lax.bitcast_convert_type(x[:, :w].astype(BF16).astype(F32), jnp.uint32)
    hi = lax.bitcast_convert_type(x[:, w:].astype(BF16).astype(F32), jnp.uint32)
    return hi | (lo >> 16)


def _unpack(u):
    lo = lax.bitcast_convert_type(u << 16, F32).astype(BF16)
    hi = lax.bitcast_convert_type(u & jnp.uint32(0xFFFF0000), F32).astype(BF16)
    return lo, hi


def _pow2_pieces(units, limit):
    bit = 1
    while bit * 2 <= limit:
        bit *= 2
    while bit:
        yield (units & bit) != 0, units & ~(2 * bit - 1), bit
        bit //= 2


def _rows_copy(vm_ref, hbm_ref, sem, vm_row, hbm_row, rows, to_hbm):
    v = vm_ref.at[pl.ds(pl.multiple_of(vm_row, RUN_ALIGN), rows), :]
    h = hbm_ref.at[pl.ds(pl.multiple_of(hbm_row, RUN_ALIGN), rows), :]
    return pltpu.make_async_copy(v, h, sem) if to_hbm else pltpu.make_async_copy(h, v, sem)


def _run_copies(vm_ref, hbm_ref, sem, n8, vm_row, hbm_row, limit, to_hbm, act):
    def emit(pieces):
        for on, off, size in pieces:
            @pl.when(on)
            def _():
                act(_rows_copy(vm_ref, hbm_ref, sem, vm_row + RUN_ALIGN * off, hbm_row + RUN_ALIGN * off,
                               RUN_ALIGN * size, to_hbm))

    pieces = list(_pow2_pieces(n8, limit))
    long_pieces = [p for p in pieces if p[2] >= LONG_RUN]
    if long_pieces:
        pl.when(n8 >= LONG_RUN)(lambda: emit(long_pieces))
    emit([p for p in pieces if p[2] < LONG_RUN])


def _wait_rows(vm_ref, hbm_ref, sem, units, limit, to_hbm):
    for on, _, size in _pow2_pieces(units, limit):
        @pl.when(on)
        def _():
            _rows_copy(vm_ref, hbm_ref, sem, 0, 0, RUN_ALIGN * size, to_hbm).wait()


def _dispatch_kernel(n8_ref, ls_ref, gs_ref, ts_ref, t8_ref, nu_ref, pos_ref, ext_ref, h_ref, xs_out, srt2, zbuf,
                     sems):
    step = pl.program_id(0)
    tm = h_ref.shape[0]
    rows = srt2.shape[1]
    slot = step % 2
    srt, sem = srt2.at[slot], sems.at[slot]
    rid = lax.broadcasted_iota(jnp.int32, (rows, 1), 0).astype(F32)
    start = ext_ref[0, 0:1, :]
    member = jnp.where((rid >= start) & (rid < ext_ref[0, 1:2, :]), 1.0, 0.0)
    offset = rid - jnp.sum(member * start, axis=1, keepdims=True)
    pos = jnp.dot(member.astype(BF16), pos_ref[...], preferred_element_type=F32)
    sel = jnp.where(pos == offset, 1.0, 0.0).astype(BF16)
    srt[...] = _pack(jnp.dot(sel, h_ref[...], preferred_element_type=F32))

    def send(e, c):
        i = step * N_EXPERTS + e
        _run_copies(srt, xs_out, sem, n8_ref[i], ls_ref[i], gs_ref[i], tm // RUN_ALIGN, True, lambda cp: cp.start())
        return c

    lax.fori_loop(0, N_EXPERTS, send, 0)

    def wait_tile(tile, s):
        last = tile * N_EXPERTS + N_EXPERTS - 1
        _wait_rows(srt2.at[s], xs_out, sems.at[s], ls_ref[last] // RUN_ALIGN + n8_ref[last], rows // RUN_ALIGN, True)

    pl.when(step > 0)(lambda: wait_tile(step - 1, 1 - slot))

    @pl.when(step == pl.num_programs(0) - 1)
    def _():
        wait_tile(step, slot)
        zbuf[...] = jnp.zeros(zbuf.shape, zbuf.dtype)
        nblk = xs_out.shape[0] // EXPERT_BLOCK

        def fill(act):
            def tails(e, c):
                _run_copies(zbuf, xs_out, sem, t8_ref[e], 0, ts_ref[e], EXPERT_BLOCK // RUN_ALIGN - 1, True, act)
                return c

            def blocks(b, c):
                act(pltpu.make_async_copy(
                    zbuf, xs_out.at[pl.ds(pl.multiple_of(b * EXPERT_BLOCK, EXPERT_BLOCK), EXPERT_BLOCK), :], sem))
                return c

            lax.fori_loop(0, N_EXPERTS, tails, 0)
            lax.fori_loop(nu_ref[0], nblk, blocks, 0)

        fill(lambda cp: cp.start())
        fill(lambda cp: cp.wait())


def _dispatch_call(tables, pos_et, ext_rows, h2, nblk, tm):
    t, d = h2.shape
    lrows = TOP_K * tm + N_EXPERTS * RUN_ALIGN
    return pl.pallas_call(
        _dispatch_kernel,
        grid_spec=pltpu.PrefetchScalarGridSpec(
            num_scalar_prefetch=len(tables), grid=(t // tm,),
            in_specs=[pl.BlockSpec((N_EXPERTS, tm), lambda i, *_: (0, i)),
                      pl.BlockSpec((1,) + ext_rows.shape[1:], lambda i, *_: (i, 0, 0)),
                      pl.BlockSpec((tm, d), lambda i, *_: (i, 0))],
            out_specs=pl.BlockSpec(memory_space=pl.ANY),
            scratch_shapes=[pltpu.VMEM((2, lrows, d // 2), jnp.uint32),
                            pltpu.VMEM((EXPERT_BLOCK, d // 2), jnp.uint32), pltpu.SemaphoreType.DMA((2,))]),
        out_shape=jax.ShapeDtypeStruct((nblk * EXPERT_BLOCK, d // 2), jnp.uint32),
        compiler_params=_params("arbitrary"),
        name="dispatch",
    )(*tables, pos_et, ext_rows, h2)


def _expert_kernel(blk_ref, nused_ref, x_ref, wgu_ref, wd_ref, y_ref):
    used = pl.program_id(0) < nused_ref[0]

    @pl.when(used)
    def _():
        lo, hi = _unpack(x_ref[...])
        half = lo.shape[1]
        gu = (jnp.dot(lo, wgu_ref[0, :half, :], preferred_element_type=F32)
              + jnp.dot(hi, wgu_ref[0, half:, :], preferred_element_type=F32))
        a = _silu(gu[:, :EXPERT_FF]) * gu[:, EXPERT_FF:]
        y_ref[...] = _pack(_dot(a, wd_ref[0]))

    @pl.when(jnp.logical_not(used))
    def _():
        y_ref[...] = jnp.zeros(y_ref.shape, y_ref.dtype)


def _expert_call(blk_e, nused, xs, wgu, wd):
    rows, d = xs.shape
    nblk = rows // EXPERT_BLOCK

    def row_map(i, blk, nu):
        return (jnp.minimum(i, nu[0] - 1), 0)

    return pl.pallas_call(
        _expert_kernel,
        grid_spec=pltpu.PrefetchScalarGridSpec(
            num_scalar_prefetch=2, grid=(nblk,),
            in_specs=[pl.BlockSpec((EXPERT_BLOCK, d), row_map),
                      pl.BlockSpec((1,) + wgu.shape[1:], lambda i, blk, nu: (blk[i], 0, 0)),
                      pl.BlockSpec((1,) + wd.shape[1:], lambda i, blk, nu: (blk[i], 0, 0))],
            out_specs=pl.BlockSpec((EXPERT_BLOCK, d), lambda i, blk, nu: (i, 0))),
        out_shape=jax.ShapeDtypeStruct((rows, d), jnp.uint32),
        compiler_params=_params("arbitrary"),
        name="expert",
    )(blk_e, nused, xs, wgu, wd)


def _combine_kernel(n8_ref, ls_ref, gs_ref, ys_hbm, pos_ref, w_ref, ext_ref, xm_ref, h_ref, mod_ref, wsgu_ref,
                    wsd_ref, fn_ref, o_ref, ybuf2, sems):
    step = pl.program_id(0)
    tm = xm_ref.shape[0]
    rows = ybuf2.shape[1]
    slot = step % 2
    ybuf, sem = ybuf2.at[slot], sems.at[slot]

    def fetch(tile, s):
        def body(e, c):
            i = tile * N_EXPERTS + e
            _run_copies(ybuf2.at[s], ys_hbm, sems.at[s], n8_ref[i], ls_ref[i], gs_ref[i], tm // RUN_ALIGN, False,
                        lambda cp: cp.start())
            return c
        lax.fori_loop(0, N_EXPERTS, body, 0)

    pl.when(step == 0)(lambda: fetch(step, slot))
    pl.when(step + 1 < pl.num_programs(0))(lambda: fetch(step + 1, 1 - slot))
    gu = _dot(h_ref[...], wsgu_ref[...])
    ff = gu.shape[1] // 2
    shared = _dot(_silu(gu[:, :ff]) * gu[:, ff:], wsd_ref[...])
    cid = lax.broadcasted_iota(jnp.int32, (1, rows), 1).astype(F32)
    start = ext_ref[0, :, 1:2]
    member = jnp.where((cid >= start) & (cid < ext_ref[0, :, 2:3]), 1.0, 0.0)
    offset = cid - jnp.sum(member * start, axis=0, keepdims=True)
    member = member.astype(BF16)
    pos = jnp.dot(pos_ref[...], member, preferred_element_type=F32)
    mix = jnp.where(pos == offset, jnp.dot(w_ref[...].astype(BF16), member, preferred_element_type=F32), 0.0)
    mix = mix.astype(BF16)
    last = step * N_EXPERTS + N_EXPERTS - 1
    filled = ls_ref[last] + RUN_ALIGN * n8_ref[last]
    _wait_rows(ybuf, ys_hbm, sem, filled // RUN_ALIGN, rows // RUN_ALIGN, False)
    rid = lax.broadcasted_iota(jnp.int32, (rows, 1), 0)
    lo, hi = _unpack(jnp.where(rid < filled, ybuf[...], jnp.uint32(0)))
    routed = jnp.concatenate([jnp.dot(mix, lo, preferred_element_type=F32),
                              jnp.dot(mix, hi, preferred_element_type=F32)], axis=1)
    x = xm_ref[...] + mod_ref[0, 5:6, :] * (routed + shared)
    o_ref[...] = _rms(x, fn_ref[...])


def _combine_call(tables, ys, pos_te, w_te, ext_cols, xm, h2, mod, wsgu, wsd, final_norm, tm, tiles_per_batch):
    t, d = xm.shape
    lrows = TOP_K * tm + N_EXPERTS * RUN_ALIGN
    tok = pl.BlockSpec((tm, d), lambda i, *_: (i, 0))
    per_e = pl.BlockSpec((tm, N_EXPERTS), lambda i, *_: (i, 0))
    return pl.pallas_call(
        _combine_kernel,
        grid_spec=pltpu.PrefetchScalarGridSpec(
            num_scalar_prefetch=len(tables), grid=(t // tm,),
            in_specs=[pl.BlockSpec(memory_space=pl.ANY), per_e, per_e,
                      pl.BlockSpec((1,) + ext_cols.shape[1:], lambda i, *_: (i, 0, 0)), tok, tok,
                      pl.BlockSpec((1, SUBLANES, d), lambda i, *_: (i // tiles_per_batch, 0, 0)),
                      _full(wsgu.shape), _full(wsd.shape), _full(final_norm.shape)],
            out_specs=tok,
            scratch_shapes=[pltpu.VMEM((2, lrows, d // 2), jnp.uint32), pltpu.SemaphoreType.DMA((2,))]),
        out_shape=jax.ShapeDtypeStruct((t, d), F32),
        compiler_params=_params("arbitrary"),
        name="combine",
    )(*tables, ys, pos_te, w_te, ext_cols, xm, h2, mod, wsgu, wsd, final_norm)


def _moe(xm, mod, norm_ffn, w_router, router_bias, wg, wu, wd, wsg, wsu, wsd, final_norm, tm):
    bsz, s, d = xm.shape
    t = bsz * s
    nt = t // tm
    perm = (np.arange(N_EXPERTS) % N_GROUPS) * GROUP_SIZE + np.arange(N_EXPERTS) // N_GROUPS
    wrt = w_router.T[perm]
    rbias = router_bias[perm][:, None]
    lower = jnp.asarray(perm[None, :] < perm[:, None], BF16)
    h2, w_et, pos_et, ext_cols, ext_rows = _route_call(xm, mod, norm_ffn, wrt, rbias, lower, tm, ROUTE_TILES)

    inv = np.argsort(perm)
    n8 = (ext_cols[:, :, 0].astype(jnp.int32)[:, inv] + (RUN_ALIGN - 1)) // RUN_ALIGN
    run = RUN_ALIGN * n8
    ls = jnp.cumsum(run, axis=1) - run
    tot = jnp.sum(run, axis=0)
    padded = (tot + EXPERT_BLOCK - 1) // EXPERT_BLOCK * EXPERT_BLOCK
    pad_end = jnp.cumsum(padded)
    gs = (pad_end - padded)[None, :] + jnp.cumsum(run, axis=0) - run
    nblk = -(-(t * TOP_K + nt * N_EXPERTS * (RUN_ALIGN - 1)) // EXPERT_BLOCK) + N_EXPERTS
    blk_first = jnp.arange(nblk, dtype=jnp.int32)[:, None] * EXPERT_BLOCK
    blk_e = jnp.minimum(jnp.sum((pad_end[None, :] <= blk_first).astype(jnp.int32), axis=1), N_EXPERTS - 1)
    nused = (pad_end[-1:] // EXPERT_BLOCK).astype(jnp.int32)
    tables = [a.reshape(-1).astype(jnp.int32) for a in (n8, ls, gs)]
    tails = [(pad_end - padded + tot).astype(jnp.int32), ((padded - tot) // RUN_ALIGN).astype(jnp.int32), nused]

    h2f = h2.reshape(t, d)
    xs = _dispatch_call(tables + tails, pos_et, ext_rows, h2f, nblk, tm)
    wgu = jnp.concatenate([wg, wu], axis=2).astype(BF16)
    ys = _expert_call(blk_e, nused, xs, wgu, wd.astype(BF16))
    wsgu = jnp.concatenate([wsg, wsu], axis=1).astype(BF16)
    out = _combine_call(tables, ys, pos_et.T, w_et.T, ext_cols, xm.reshape(t, d), h2f, mod, wsgu, wsd.astype(BF16),
                        final_norm, tm, s // tm)
    return out.reshape(bsz, s, d)


def _rope_tables(s):
    rows = s // GRID_W
    row = jnp.broadcast_to(jnp.arange(rows, dtype=F32)[:, None], (rows, GRID_W)).reshape(-1)
    col = jnp.broadcast_to(jnp.arange(GRID_W, dtype=F32)[None, :], (rows, GRID_W)).reshape(-1)
    half = QK_ROPE // 2
    inv_freq = ROPE_THETA ** (-jnp.arange(0, half, 2, dtype=F32) / half)
    ar, ac = row[:, None] * inv_freq, col[:, None] * inv_freq
    ones = jnp.ones((s, QK_NOPE), F32)
    tail = HEAD_PAD - QK_NOPE - QK_ROPE
    cos_t = jnp.concatenate([ones, jnp.cos(ar), jnp.cos(ar), jnp.cos(ac), jnp.cos(ac), jnp.ones((s, tail), F32)], 1)
    sin_t = jnp.concatenate([0 * ones, -jnp.sin(ar), jnp.sin(ar), -jnp.sin(ac), jnp.sin(ac),
                             jnp.zeros((s, tail), F32)], 1)
    return cos_t, sin_t


_Q4 = QK_ROPE // 4
ROPE_SWAP = np.concatenate([np.arange(_Q4, 2 * _Q4), np.arange(0, _Q4), np.arange(3 * _Q4, 4 * _Q4),
                            np.arange(2 * _Q4, 3 * _Q4)])


def _rope_slot(w, swap):
    if swap:
        w = w[..., ROPE_SWAP]
    pad = [(0, 0)] * (w.ndim - 1) + [(QK_NOPE, HEAD_PAD - QK_NOPE - QK_ROPE)]
    return jnp.pad(w, pad)


def kernel(x, c, ctx, c_ctx, w_mod, b_mod, norm_mix, norm_ffn, w_in, b_in, q_norm, w_uq, kv_norm, w_ukv, w_branch_attn, hy_conv_w, hy_conv_b, hy_filt_w1, hy_filt_b1, hy_filt_w2, hy_filt_b2, hy_filt_w3, hy_filt_freq, hy_skip, w_branch_hyena, w_out, w_router, router_bias, w_exp_gate, w_exp_up, w_exp_down, w_sh_gate, w_sh_up, w_sh_down, final_norm,
           tiles=None):
    bsz, s, d = x.shape
    tl = dict(inproj=512, tq=2048, tk=1408, fft_kb=8, merge=512, moe=256)
    tl.update(tiles or {})
    assert w_mod.shape[0] == 1, "single-layer trunk"
    i = 0

    rows = -(-(bsz + 1) // SUBLANES) * SUBLANES
    c_rows = jnp.pad(jnp.concatenate([c, c_ctx[None]], axis=0), ((0, rows - bsz - 1), (0, 0)))
    mod_all = _mod_call(c_rows, w_mod[i], b_mod[i])
    mod_all = jnp.pad(mod_all.reshape(rows, 6, d), ((0, 0), (0, SUBLANES - 6), (0, 0)))
    mod, modc = mod_all[:bsz], mod_all[bsz:bsz + 1]

    cuts = np.cumsum([Q_LORA, KV_LORA, QK_ROPE, 3 * HY_WIDTH])
    wi, bi = w_in[i], b_in[i][None]
    w_q, w_kv, w_pe, w_hy, w_g = jnp.split(wi, cuts, axis=1)
    b_q, b_kv, b_pe, b_hy, b_g = jnp.split(bi, cuts, axis=1)
    wa = jnp.concatenate([w_q, w_kv, _rope_slot(w_pe, False), _rope_slot(w_pe, True)], axis=1).astype(BF16)
    ba = jnp.concatenate([b_q, b_kv, _rope_slot(b_pe, False), _rope_slot(b_pe, True)], axis=1)
    wq3 = w_uq[i].reshape(Q_LORA, N_HEADS, QK_NOPE + QK_ROPE) * (ATTN_SCALE * math.log2(math.e))
    tail = ((0, 0), (0, 0), (0, HEAD_PAD - QK_NOPE))
    wuq = (jnp.pad(wq3[..., :QK_NOPE], tail) + _rope_slot(wq3[..., QK_NOPE:], False)).reshape(Q_LORA, -1).astype(BF16)
    wuqs = _rope_slot(wq3[..., QK_NOPE:], True).reshape(Q_LORA, -1).astype(BF16)
    wkv3 = w_ukv[i].reshape(KV_LORA, N_HEADS, QK_NOPE + V_HEAD)
    wuk = jnp.pad(wkv3[..., :QK_NOPE], tail).reshape(KV_LORA, -1).astype(BF16)
    wuvt = wkv3[..., QK_NOPE:].reshape(KV_LORA, -1).T.astype(BF16)
    nm, qn, kvn = norm_mix[i][None], q_norm[i][None], kv_norm[i][None]

    w_c = jnp.concatenate([w_kv, _rope_slot(w_pe, False)], axis=1).astype(BF16)
    b_c = jnp.concatenate([b_kv, _rope_slot(b_pe, False)], axis=1)
    ck, cvt = _ctx_call(ctx, modc, nm, w_c, b_c, kvn, wuk, wuvt)

    cos_t, sin_t = _rope_tables(s)
    q, k, vt, hv, hx1, hx2, gate = _inproj_call(
        x, mod, nm, wa, ba, w_hy.astype(BF16), b_hy, w_g.astype(BF16), b_g, qn, wuq, wuqs, kvn, wuk, wuvt,
        cos_t, sin_t, hy_conv_w[i], hy_conv_b[i][None], tl["inproj"])

    attn = _attn_call(q, jnp.concatenate([ck, k], axis=2), jnp.concatenate([cvt, vt], axis=3), tl["tq"], tl["tk"])
    hy = _hyena(hv, hx1, hx2, hy_filt_w1[i], hy_filt_b1[i], hy_filt_w2[i], hy_filt_b2[i], hy_filt_w3[i],
                hy_filt_freq[i], hy_skip[i], tl["fft_kb"])
    xm = _merge_call(x, attn, hy, gate, mod, w_branch_attn[i].astype(BF16), w_branch_hyena[i].astype(BF16),
                     w_out[i].astype(BF16), tl["merge"])
    return _moe(xm, mod, norm_ffn[i][None], w_router[i], router_bias[i], w_exp_gate[i], w_exp_up[i], w_exp_down[i],
                w_sh_gate[i], w_sh_up[i], w_sh_down[i], final_norm[None], tl["moe"])
```

```python
import functools
import math

import numpy as np
import jax
import jax.numpy as jnp
from jax import lax
from jax.experimental import pallas as pl
from jax.experimental.pallas import tpu as pltpu

GRID_W = 64
N_HEADS = 8
QK_NOPE = 64
QK_ROPE = 32
V_HEAD = 64
Q_LORA = 256
KV_LORA = 128
ROPE_THETA = 10000.0
ATTN_SCALE = 1.0 / math.sqrt(QK_NOPE + QK_ROPE)
HY_WIDTH = 512
HY_ORDER = 2
HY_SHORT = 3
HY_BANDS = 8
HY_EMB = 1 + 2 * HY_BANDS
HY_EMB_PAD = 32
HY_FAST_DECAY = 0.3
HY_SLOW_DECAY = 1.5
HY_DECAY_TARGET = 1e-2
N_EXPERTS = 64
N_GROUPS = 8
GROUP_SIZE = N_EXPERTS // N_GROUPS
TOPK_GROUPS = 4
TOP_K = 8
EXPERT_FF = 256
ROUTE_SCALE = 2.5
EXPERT_BLOCK = 1024
RUN_ALIGN = 8
ROUTE_TILES = 8
LONG_RUN = 8
NORM_EPS = 1e-6

HEAD_PAD = 128
Q_CHUNK = 512
AHEAD = 2
LANES = 128
SUBLANES = 8
VMEM_LIMIT = 48 * 1024 * 1024

F32 = jnp.float32
BF16 = jnp.bfloat16
NT_DIMS = (((1,), (1,)), ((), ()))
NN_DIMS = (((1,), (0,)), ((), ()))


def _params(*sem):
    return pltpu.CompilerParams(dimension_semantics=sem, vmem_limit_bytes=VMEM_LIMIT)


def _dot(a, b):
    return jnp.dot(a.astype(BF16), b.astype(BF16), preferred_element_type=F32)


def _split(a):
    hi = a.astype(BF16)
    lo = (a - hi.astype(F32)).astype(BF16)
    return hi, lo


def _dot3(a, b, dims=NN_DIMS):
    ah, al = _split(a)
    bh, bl = _split(b)
    d = functools.partial(lax.dot_general, dimension_numbers=dims, preferred_element_type=F32)
    return d(ah, bh) + (d(ah, bl) + d(al, bh))


def _rms(x, g):
    return x * lax.rsqrt(jnp.mean(x * x, axis=-1, keepdims=True) + NORM_EPS) * g


def _silu(x):
    return x * jax.nn.sigmoid(x)


def _full(shape):
    nd = len(shape)
    return pl.BlockSpec(shape, lambda *_: (0,) * nd)


def _mod_kernel(c_ref, w_ref, b_ref, o_ref):
    o_ref[...] = _dot3(_silu(c_ref[...]), w_ref[...]) + b_ref[...]


def _mod_call(c_rows, w_mod, b_mod):
    r, d = c_rows.shape
    n = w_mod.shape[1]
    bn = 1024
    return pl.pallas_call(
        _mod_kernel,
        grid=(n // bn,),
        in_specs=[_full((r, d)), pl.BlockSpec((d, bn), lambda j: (0, j)), pl.BlockSpec((1, bn), lambda j: (0, j))],
        out_specs=pl.BlockSpec((r, bn), lambda j: (0, j)),
        out_shape=jax.ShapeDtypeStruct((r, n), F32),
        compiler_params=_params("arbitrary"),
        name="mod",
    )(c_rows, w_mod, b_mod.reshape(1, n))


def _prenorm(x, mod_ref, row, g):
    shift = mod_ref[0, row:row + 1, :]
    scale = mod_ref[0, row + 1:row + 2, :]
    return _rms(x, g) * (1.0 + scale) + shift


def _kv_heads(kv_lat, kpe, kvn_ref, wuk_ref, wuvt_ref, k_out, vt_out):
    kvn = _rms(kv_lat, kvn_ref[...]).astype(BF16)
    kk = _dot(kvn, wuk_ref[...])
    vt = lax.dot_general(wuvt_ref[...], kvn, NT_DIMS, preferred_element_type=F32)
    ones = jnp.ones((HEAD_PAD - V_HEAD, vt.shape[1]), F32)
    for h in range(N_HEADS):
        k_out[0, h] = (kk[:, HEAD_PAD * h:HEAD_PAD * (h + 1)] + kpe).astype(BF16)
        vt_out[0, h] = jnp.concatenate([vt[V_HEAD * h:V_HEAD * (h + 1)], ones], axis=0).astype(BF16)


def _ctx_kernel(c_ref, mod_ref, nm_ref, w_ref, b_ref, kvn_ref, wuk_ref, wuv_ref, k_out, v_out):
    h = _prenorm(c_ref[0], mod_ref, 0, nm_ref[...]).astype(BF16)
    a = _dot(h, w_ref[...]) + b_ref[...]
    _kv_heads(a[:, :KV_LORA], a[:, KV_LORA:], kvn_ref, wuk_ref, wuv_ref, k_out, v_out)


def _ctx_call(ctx, modc, norm_mix, w_c, b_c, kv_norm, w_uk, w_uv):
    bsz, n, d = ctx.shape
    return pl.pallas_call(
        _ctx_kernel,
        grid=(bsz,),
        in_specs=[pl.BlockSpec((1, n, d), lambda b: (b, 0, 0)), _full(modc.shape), _full(norm_mix.shape),
                  _full(w_c.shape), _full(b_c.shape), _full(kv_norm.shape), _full(w_uk.shape), _full(w_uv.shape)],
        out_specs=[pl.BlockSpec((1, N_HEADS, n, HEAD_PAD), lambda b: (b, 0, 0, 0)),
                   pl.BlockSpec((1, N_HEADS, HEAD_PAD, n), lambda b: (b, 0, 0, 0))],
        out_shape=[jax.ShapeDtypeStruct((bsz, N_HEADS, n, HEAD_PAD), BF16),
                   jax.ShapeDtypeStruct((bsz, N_HEADS, HEAD_PAD, n), BF16)],
        compiler_params=_params("arbitrary"),
        name="ctx",
    )(ctx, modc, norm_mix, w_c, b_c, kv_norm, w_uk, w_uv)


def _inproj_kernel(x_ref, xp_ref, xn_ref, mod_ref, nm_ref, wa_ref, ba_ref, why_ref, bhy_ref, wg_ref, bg_ref,
                   qn_ref, wuq_ref, wuqs_ref, kvn_ref, wuk_ref, wuv_ref, cos_ref, sin_ref, cw_ref, cb_ref,
                   q_out, k_out, v_out, hv_out, hx1_out, hx2_out, g_out):
    i = pl.program_id(0)
    tm = x_ref.shape[1]
    nm = nm_ref[...]
    h = _prenorm(x_ref[0], mod_ref, 0, nm).astype(BF16)
    a = _dot(h, wa_ref[...]) + ba_ref[...]
    q_lat = a[:, :Q_LORA]
    kv_lat = a[:, Q_LORA:Q_LORA + KV_LORA]
    kpe_m = a[:, Q_LORA + KV_LORA:Q_LORA + KV_LORA + HEAD_PAD]
    kpe_s = a[:, Q_LORA + KV_LORA + HEAD_PAD:]
    cos = cos_ref[...]
    sin = sin_ref[...]
    qn = _rms(q_lat, qn_ref[...]).astype(BF16)
    qa = _dot(qn, wuq_ref[...])
    qs = _dot(qn, wuqs_ref[...])
    for hh in range(N_HEADS):
        sl = slice(HEAD_PAD * hh, HEAD_PAD * (hh + 1))
        q_out[0, hh] = (qa[:, sl] * cos + qs[:, sl] * sin).astype(BF16)
    _kv_heads(kv_lat, kpe_m * cos + kpe_s * sin, kvn_ref, wuk_ref, wuv_ref, k_out, v_out)
    g_out[0] = (_dot(h, wg_ref[...]) + bg_ref[...]).astype(BF16)

    why = why_ref[...]
    bhy = bhy_ref[...]
    halo = jnp.concatenate([_prenorm(xp_ref[0], mod_ref, 0, nm), _prenorm(xn_ref[0], mod_ref, 0, nm)], axis=0)
    hy_all = _dot(jnp.concatenate([h, halo.astype(BF16)], axis=0), why) + bhy
    hy = hy_all[:tm]
    prev = jnp.where(i == 0, 0.0, hy_all[tm + SUBLANES - 1:tm + SUBLANES])
    nxt = jnp.where(i == pl.num_programs(0) - 1, 0.0, hy_all[tm + SUBLANES:tm + SUBLANES + 1])
    rid = lax.broadcasted_iota(jnp.int32, (tm, 1), 0)
    up = jnp.where(rid == 0, prev, pltpu.roll(hy, 1, 0))
    dn = jnp.where(rid == tm - 1, nxt, pltpu.roll(hy, tm - 1, 0))
    u = up * cw_ref[0:1, :] + hy * cw_ref[1:2, :] + dn * cw_ref[2:3, :] + cb_ref[...]
    hv_out[0] = u[:, :HY_WIDTH]
    hx1_out[0] = u[:, HY_WIDTH:2 * HY_WIDTH]
    hx2_out[0] = u[:, 2 * HY_WIDTH:]


def _inproj_call(x, mod, norm_mix, wa, ba, why, bhy, wg, bg, q_norm, wuq, wuqs, kv_norm, wuk, wuvt, cos_t, sin_t, cw,
                 cb, tm):
    bsz, s, d = x.shape
    nt = s // tm
    rb = tm // SUBLANES
    last_rb = s // SUBLANES - 1
    consts = [norm_mix, wa, ba, why, bhy, wg, bg, q_norm, wuq, wuqs, kv_norm, wuk, wuvt]
    in_specs = [
        pl.BlockSpec((1, tm, d), lambda i, b: (b, i, 0)),
        pl.BlockSpec((1, SUBLANES, d), lambda i, b: (b, jnp.maximum(i * rb - 1, 0), 0)),
        pl.BlockSpec((1, SUBLANES, d), lambda i, b: (b, jnp.minimum((i + 1) * rb, last_rb), 0)),
        pl.BlockSpec((1, SUBLANES, d), lambda i, b: (b, 0, 0)),
    ] + [_full(c.shape) for c in consts] + [
        pl.BlockSpec((tm, HEAD_PAD), lambda i, b: (i, 0)),
        pl.BlockSpec((tm, HEAD_PAD), lambda i, b: (i, 0)),
        _full(cw.shape), _full(cb.shape),
    ]
    hw = HY_WIDTH
    out_specs = [
        pl.BlockSpec((1, N_HEADS, tm, HEAD_PAD), lambda i, b: (b, 0, i, 0)),
        pl.BlockSpec((1, N_HEADS, tm, HEAD_PAD), lambda i, b: (b, 0, i, 0)),
        pl.BlockSpec((1, N_HEADS, HEAD_PAD, tm), lambda i, b: (b, 0, 0, i)),
        pl.BlockSpec((1, tm, hw), lambda i, b: (b, i, 0)),
        pl.BlockSpec((1, tm, hw), lambda i, b: (b, i, 0)),
        pl.BlockSpec((1, tm, hw), lambda i, b: (b, i, 0)),
        pl.BlockSpec((1, tm, 2 * d), lambda i, b: (b, i, 0)),
    ]
    out_shape = [
        jax.ShapeDtypeStruct((bsz, N_HEADS, s, HEAD_PAD), BF16),
        jax.ShapeDtypeStruct((bsz, N_HEADS, s, HEAD_PAD), BF16),
        jax.ShapeDtypeStruct((bsz, N_HEADS, HEAD_PAD, s), BF16),
        jax.ShapeDtypeStruct((bsz, s, hw), F32),
        jax.ShapeDtypeStruct((bsz, s, hw), F32),
        jax.ShapeDtypeStruct((bsz, s, hw), F32),
        jax.ShapeDtypeStruct((bsz, s, 2 * d), BF16),
    ]
    return pl.pallas_call(
        _inproj_kernel,
        grid=(nt, bsz),
        in_specs=in_specs,
        out_specs=out_specs,
        out_shape=out_shape,
        compiler_params=_params("arbitrary", "arbitrary"),
        name="inproj",
    )(x, x, x, mod, *consts, cos_t, sin_t, cw, cb)


def _attn_kernel(q_ref, k_ref, vt_ref, o_ref, m_sc, acc_sc):
    j = pl.program_id(2)

    @pl.when(j == 0)
    def _():
        m_sc[...] = jnp.full(m_sc.shape, -jnp.inf, F32)
        acc_sc[...] = jnp.zeros(acc_sc.shape, F32)

    tq = q_ref.shape[2]
    qw = min(tq, Q_CHUNK)
    units = [(h, c) for h in range(N_HEADS) for c in range(0, tq, qw)]

    def scores(u):
        h, c = units[u]
        return lax.dot_general(k_ref[0, h], q_ref[0, h, c:c + qw, :], NT_DIMS,
                               preferred_element_type=F32)

    pending = [scores(u) for u in range(AHEAD)]
    for u, (h, c) in enumerate(units):
        if u + AHEAD < len(units):
            pending.append(scores(u + AHEAD))
        st = pending.pop(0)
        m_prev = m_sc[h, :, c:c + qw]
        m_new = jnp.maximum(m_prev, jnp.max(st, axis=0, keepdims=True))
        pt = jnp.exp2(st - m_new).astype(BF16)
        acc_sc[h, :, c:c + qw] = (jnp.exp2(m_prev - m_new) * acc_sc[h, :, c:c + qw]
                                  + jnp.dot(vt_ref[0, h], pt, preferred_element_type=F32))
        m_sc[h, :, c:c + qw] = m_new

    @pl.when(j == pl.num_programs(2) - 1)
    def _():
        ot = jnp.concatenate([acc_sc[h, :V_HEAD] / acc_sc[h, V_HEAD:V_HEAD + 1] for h in range(N_HEADS)], axis=0)
        o_ref[0] = ot.T.astype(o_ref.dtype)


def _attn_call(q, k, vt, tq, tk):
    bsz, nh, s, dh = q.shape
    nk = k.shape[2]
    dv = nh * V_HEAD
    return pl.pallas_call(
        _attn_kernel,
        grid=(bsz, s // tq, nk // tk),
        in_specs=[
            pl.BlockSpec((1, nh, tq, dh), lambda b, i, j: (b, 0, i, 0)),
            pl.BlockSpec((1, nh, tk, dh), lambda b, i, j: (b, 0, j, 0)),
            pl.BlockSpec((1, nh, dh, tk), lambda b, i, j: (b, 0, 0, j)),
        ],
        out_specs=pl.BlockSpec((1, tq, dv), lambda b, i, j: (b, i, 0)),
        out_shape=jax.ShapeDtypeStruct((bsz, s, dv), BF16),
        scratch_shapes=[pltpu.VMEM((nh, 1, tq), F32), pltpu.VMEM((nh, dh, tq), F32)],
        compiler_params=_params("arbitrary", "arbitrary", "arbitrary"),
        name="attn",
    )(q, k, vt)


def _filter_kernel(emb_ref, w1_ref, b1_ref, w2_ref, b2_ref, w3_ref, fr_ref, dl_ref, full_out, asum_out, *, seq):
    r = pl.program_id(0)
    rb = emb_ref.shape[0]
    emb = emb_ref[...]
    fr = fr_ref[...]
    h = jnp.sin(fr * (_dot3(emb, w1_ref[...]) + b1_ref[...]))
    h = jnp.sin(fr * (_dot3(h, w2_ref[...]) + b2_ref[...]))
    k = _dot3(h, w3_ref[0]) * jnp.exp(-emb[:, 0:1] * dl_ref[...])
    row = r * rb + lax.broadcasted_iota(jnp.int32, (rb, 1), 0)
    k = jnp.where(row == seq, 0.0, k)
    full_out[...] = k

    @pl.when(r == 0)
    def _():
        asum_out[...] = jnp.zeros(asum_out.shape, F32)

    asum_out[...] += jnp.sum(jnp.abs(k), axis=0, keepdims=True)


def _filter_call(emb, w1, b1, w2, b2, w3sel, freq, deltas2, seq, rb):
    n2 = emb.shape[0]
    half_blocks = seq // rb
    width = w3sel.shape[2]
    return pl.pallas_call(
        functools.partial(_filter_kernel, seq=seq),
        grid=(n2 // rb,),
        in_specs=[pl.BlockSpec((rb, HY_EMB_PAD), lambda r: (r, 0)), _full(w1.shape), _full(b1.shape),
                  _full(w2.shape), _full(b2.shape),
                  pl.BlockSpec((1,) + w3sel.shape[1:], lambda r: (r // half_blocks, 0, 0)),
                  _full(freq.shape), _full(deltas2.shape)],
        out_specs=[pl.BlockSpec((rb, width), lambda r: (r, 0)), pl.BlockSpec((1, width), lambda r: (0, 0))],
        out_shape=[jax.ShapeDtypeStruct((n2, width), F32), jax.ShapeDtypeStruct((1, width), F32)],
        compiler_params=_params("arbitrary"),
        name="filt",
    )(emb, w1, b1, w2, b2, w3sel, freq, deltas2)


def _fa_kernel(u_ref, f_ref, a_out):
    two, _, hn, g, c = u_ref.shape
    a = _dot(f_ref[...], u_ref[...].reshape(two * hn * g, c))
    a_out[...] = (_pack(a) if a_out.dtype == jnp.uint32 else a).reshape(a_out.shape)


def _fa_call(u5, fmat, packed):
    _, p, hn, n, c = u5.shape
    g = SUBLANES
    co, dt = (c // 2, jnp.uint32) if packed else (c, F32)
    return pl.pallas_call(
        _fa_kernel,
        grid=(p, n // g),
        in_specs=[pl.BlockSpec((2, 1, hn, g, c), lambda q, j: (0, q, 0, j, 0)), _full(fmat.shape)],
        out_specs=pl.BlockSpec((1, 2, n, g, co), lambda q, j: (q, 0, 0, j, 0)),
        out_shape=jax.ShapeDtypeStruct((p, 2, n, n, co), dt),
        compiler_params=_params("arbitrary", "arbitrary"),
        name="fa",
    )(u5, fmat)


def _dot_packed(w, u):
    lo, hi = _unpack(u)
    return jnp.concatenate([jnp.dot(w, lo, preferred_element_type=F32), jnp.dot(w, hi, preferred_element_type=F32)],
                           axis=1)


def _fb_kernel(a_ref, g_ref, asum_ref, kf_out):
    _, two, kb, n, c = a_ref.shape
    scale = 1.0 / (asum_ref[...] + 1e-6)
    for kk in range(kb):
        x = _dot(g_ref[kk], a_ref[0, :, kk].reshape(two * n, c)) * scale
        kf_out[kk] = x.reshape(two, n, c)


def _fb_call(a5, gmat, asum, kb):
    _, _, n, _, c = a5.shape
    return pl.pallas_call(
        _fb_kernel,
        grid=(n // kb,),
        in_specs=[pl.BlockSpec((1, 2, kb, n, c), lambda k: (0, 0, k, 0, 0)),
                  pl.BlockSpec((kb, 2 * n, 2 * n), lambda k: (k, 0, 0)), _full(asum.shape)],
        out_specs=pl.BlockSpec((kb, 2, n, c), lambda k: (k, 0, 0, 0)),
        out_shape=jax.ShapeDtypeStruct((n, 2, n, c), F32),
        compiler_params=_params("arbitrary"),
        name="fb",
    )(a5, gmat, asum)


def _mid_kernel(a_ref, g_ref, h_ref, kf_ref, b_out):
    _, two, kb, n, c = a_ref.shape
    for kk in range(kb):
        x = _dot_packed(g_ref[kk], a_ref[0, :, kk].reshape(two * n, c))
        xr, xi = x[:n], x[n:]
        kr, ki = kf_ref[kk, 0], kf_ref[kk, 1]
        y = jnp.concatenate([xr * kr - xi * ki, xr * ki + xi * kr], axis=0)
        b_out[0, :, kk] = _pack(_dot(h_ref[kk], y)).reshape(two, n, c)


def _mid_call(a5, gmat, hmat, kf, order, kb):
    p, _, n, _, c = a5.shape
    return pl.pallas_call(
        _mid_kernel,
        grid=(n // kb, p),
        in_specs=[pl.BlockSpec((1, 2, kb, n, c), lambda k, q: (q, 0, k, 0, 0)),
                  pl.BlockSpec((kb, 2 * n, 2 * n), lambda k, q: (k, 0, 0)),
                  pl.BlockSpec((kb, 2 * n, 2 * n), lambda k, q: (k, 0, 0)),
                  pl.BlockSpec((kb, 2, n, 2 * c), lambda k, q: (k, 0, 0, order))],
        out_specs=pl.BlockSpec((1, 2, kb, n, c), lambda k, q: (q, 0, k, 0, 0)),
        out_shape=jax.ShapeDtypeStruct(a5.shape, jnp.uint32),
        compiler_params=_params("arbitrary", "arbitrary"),
        name="mid",
    )(a5, gmat, hmat, kf)


def _fc_kernel(b_ref, f_ref, u_ref, m_ref, skip_ref, o_out):
    _, two, n, g, c = b_ref.shape
    y = _dot_packed(f_ref[...], b_ref[...].reshape(two * n * g, c)).reshape(u_ref.shape)
    o_out[...] = m_ref[...] * (y + u_ref[...] * skip_ref[...])


def _fc_call(b5, finv, u5, m5, skip_row):
    _, p, hn, n, c = u5.shape
    g = SUBLANES
    blk = pl.BlockSpec((2, 1, hn, g, c), lambda q, j: (0, q, 0, j, 0))
    return pl.pallas_call(
        _fc_kernel,
        grid=(p, n // g),
        in_specs=[pl.BlockSpec((1, 2, n, g, c // 2), lambda q, j: (q, 0, 0, j, 0)), _full(finv.shape), blk, blk,
                  _full(skip_row.shape)],
        out_specs=blk,
        out_shape=jax.ShapeDtypeStruct(u5.shape, F32),
        compiler_params=_params("arbitrary", "arbitrary"),
        name="fc",
    )(b5, finv, u5, m5, skip_row)


def _dft_tables(n):
    hn = n // 2
    k = np.arange(n)[:, None]
    ang = -2.0 * np.pi * (k * np.arange(n)[None, :] % n) / n
    fr, fi = np.cos(ang), np.sin(ang)
    f_data = np.block([[fr[:, :hn], -fi[:, :hn]], [fi[:, :hn], fr[:, :hn]]])
    f_filt = np.concatenate([fr, fi], axis=0)
    er, ei = fr[:hn], -fi[:hn]
    f_inv = np.block([[er, -ei], [ei, er]]) / float(n * n)
    k1 = jnp.arange(n, dtype=jnp.int32)[:, None, None]
    k2 = jnp.arange(n, dtype=jnp.int32)[None, :, None]
    m2 = jnp.arange(n, dtype=jnp.int32)[None, None, :]
    idx = (m2 * (k1 + n * k2)) % (n * n)
    ang2 = idx.astype(F32) * (-2.0 * math.pi / (n * n))
    gr, gi = jnp.cos(ang2), jnp.sin(ang2)
    g = jnp.concatenate([jnp.concatenate([gr, -gi], axis=2), jnp.concatenate([gi, gr], axis=2)], axis=1)
    h = jnp.swapaxes(g, 1, 2)

    def widen(f):
        return jnp.asarray(np.kron(f, np.eye(SUBLANES)), BF16)

    return widen(f_data), widen(f_filt), widen(f_inv), g.astype(BF16), h.astype(BF16)


def _hyena_filter_tables(seq):
    t = jnp.linspace(0.0, 1.0, seq, dtype=F32)[:, None]
    w = 2.0 * math.pi * jnp.arange(seq, dtype=F32)[:, None] / seq
    f = jnp.linspace(1e-4, HY_BANDS - 1, HY_BANDS, dtype=F32)[None, :]
    emb = jnp.concatenate([t, jnp.cos(f * w), -jnp.sin(f * w)], axis=-1)
    emb = jnp.concatenate([emb, emb[:1], emb[:0:-1]], axis=0)
    emb = jnp.pad(emb, ((0, 0), (0, HY_EMB_PAD - HY_EMB)))
    deltas = jnp.abs(jnp.linspace(math.log(HY_DECAY_TARGET) / HY_SLOW_DECAY,
                                  math.log(HY_DECAY_TARGET) / HY_FAST_DECAY, HY_WIDTH, dtype=F32))
    return emb, jnp.tile(deltas, HY_ORDER)[None, :]


def _hyena(hv, hx1, hx2, w1, b1, w2, b2, w3, freq, skip, kb):
    bsz, seq, c = hv.shape
    n = int(round(math.sqrt(2 * seq)))
    assert n * n == 2 * seq and bsz % 2 == 0
    hn, p = n // 2, bsz // 2
    f_data, f_filt, f_inv, gmat, hmat = _dft_tables(n)

    emb, deltas2 = _hyena_filter_tables(seq)
    w1p = jnp.pad(w1, ((0, HY_EMB_PAD - HY_EMB), (0, 0)))
    w3r = w3.reshape(w3.shape[0], HY_ORDER, 2, c)
    w3sel = jnp.stack([w3r[:, :, 0, :].reshape(-1, HY_ORDER * c), w3r[:, :, 1, :].reshape(-1, HY_ORDER * c)])
    full, asum = _filter_call(emb, w1p, b1[None], w2, b2[None], w3sel, freq[None], deltas2, seq, min(512, seq))
    c2 = HY_ORDER * c
    kf = _fb_call(_fa_call(full.reshape(2, 1, hn, n, c2), f_filt, False), gmat, asum, kb // HY_ORDER)

    def view(t):
        return t.reshape(2, p, hn, n, c)

    def long_conv(u5, m5, order):
        bm = _mid_call(_fa_call(u5, f_data, True), gmat, hmat, kf, order, kb)
        return _fc_call(bm, f_inv, u5, m5, skip[order][None, :])

    z = long_conv(view(hv), view(hx1), 0)
    return long_conv(z, view(hx2), 1).reshape(bsz, seq, c)


def _merge_kernel(x_ref, at_ref, hy_ref, g_ref, mod_ref, wba_ref, wbh_ref, wo_ref, o_ref):
    d = x_ref.shape[2]
    g = g_ref[0].astype(F32)
    y = (jax.nn.sigmoid(g[:, :d]) * _dot(at_ref[0], wba_ref[...])
         + jax.nn.sigmoid(g[:, d:]) * _dot(hy_ref[0], wbh_ref[...]))
    o_ref[0] = x_ref[0] + mod_ref[0, 2:3, :] * _dot(y, wo_ref[...])


def _merge_call(x, attn, hy, gate, mod, wba, wbh, wo, tm):
    bsz, s, d = x.shape

    def tok(w):
        return pl.BlockSpec((1, tm, w), lambda b, i: (b, i, 0))

    return pl.pallas_call(
        _merge_kernel,
        grid=(bsz, s // tm),
        in_specs=[tok(d), tok(attn.shape[2]), tok(hy.shape[2]), tok(2 * d),
                  pl.BlockSpec((1, SUBLANES, d), lambda b, i: (b, 0, 0)),
                  _full(wba.shape), _full(wbh.shape), _full(wo.shape)],
        out_specs=tok(d),
        out_shape=jax.ShapeDtypeStruct((bsz, s, d), F32),
        compiler_params=_params("arbitrary", "arbitrary"),
        name="merge",
    )(x, attn, hy, gate, mod, wba, wbh, wo)


def _route_kernel(xm_ref, mod_ref, nf_ref, wrt_ref, rb_ref, tri_ref, lt_ref, h2_out, w_out, p_out, col_out, row_out):
    tm = xm_ref.shape[1]
    ng, gs = N_GROUPS, GROUP_SIZE

    h2 = _prenorm(xm_ref[0], mod_ref, 3, nf_ref[...])
    h2_out[0] = h2.astype(h2_out.dtype)
    scores = jax.nn.sigmoid(_dot3(wrt_ref[...], h2, NT_DIMS))
    sel = scores + rb_ref[...]
    slabs = [sel[ng * j:ng * (j + 1)] for j in range(gs)]

    top1 = jnp.full((ng, tm), -jnp.inf, F32)
    top2 = top1
    for x in slabs:
        top2 = jnp.maximum(top2, jnp.minimum(top1, x))
        top1 = jnp.maximum(top1, x)
    gscore = top1 + top2
    gid = lax.broadcasted_iota(jnp.int32, (ng, 1), 0)
    rank = jnp.zeros((ng, tm), jnp.int32)
    for g2 in range(ng):
        row = gscore[g2:g2 + 1]
        beats = (row > gscore) | ((row == gscore) & (g2 < gid))
        rank = rank + beats.astype(jnp.int32)
    gmask = rank < TOPK_GROUPS

    cand = [jnp.where(gmask, x, -jnp.inf) for x in slabs]
    eid = [gid * gs + j for j in range(gs)]
    chosen = []
    for _ in range(TOP_K):
        best = functools.reduce(jnp.maximum, cand)
        best = jnp.max(best, axis=0, keepdims=True)
        idx = functools.reduce(jnp.minimum, [jnp.where(cand[j] == best, eid[j], N_EXPERTS) for j in range(gs)])
        idx = jnp.min(idx, axis=0, keepdims=True)
        chosen.append(idx)
        cand = [jnp.where(eid[j] == idx, -jnp.inf, cand[j]) for j in range(gs)]

    mask = [functools.reduce(jnp.logical_or, [eid[j] == idx for idx in chosen]) for j in range(gs)]
    maskb = jnp.concatenate(mask, axis=0)
    wsel = jnp.where(maskb, scores, 0.0)
    w_out[...] = wsel / jnp.sum(wsel, axis=0, keepdims=True) * ROUTE_SCALE

    def extents(cnt, lower_sum):
        units = jnp.floor((cnt + (RUN_ALIGN - 1)) * (1.0 / RUN_ALIGN))
        start = RUN_ALIGN * lower_sum(units.astype(BF16))
        return start, start + RUN_ALIGN * units

    lane = lax.broadcasted_iota(jnp.int32, (N_EXPERTS, LANES), 1)
    sub = lax.broadcasted_iota(jnp.int32, (SUBLANES, N_EXPERTS), 0)
    ts = tri_ref.shape[0]
    for c in range(tm // ts):
        mb = maskb[:, c * ts:(c + 1) * ts]
        maskf = jnp.where(mb, 1.0, 0.0)
        mask16 = maskf.astype(BF16)
        before = jnp.dot(mask16, tri_ref[...], preferred_element_type=F32)
        p_out[:, c * ts:(c + 1) * ts] = jnp.where(mb, before, -1.0).astype(p_out.dtype)
        cnt_c = jnp.sum(maskf, axis=1, keepdims=True)
        start_c, end_c = extents(jnp.broadcast_to(cnt_c, (N_EXPERTS, LANES)),
                                 lambda u: jnp.dot(lt_ref[...], u, preferred_element_type=F32))
        col_out[c] = jnp.where(lane == 0, cnt_c, jnp.where(lane == 1, start_c, end_c))
        cnt_r = lax.dot_general(jnp.ones((SUBLANES, ts), BF16), mask16, NT_DIMS, preferred_element_type=F32)
        start_r, end_r = extents(cnt_r,
                                 lambda u: lax.dot_general(u, lt_ref[...], NT_DIMS, preferred_element_type=F32))
        row_out[c] = jnp.where(sub == 0, start_r, end_r)


def _route_call(xm, mod, norm_ffn, wrt, rbias, lower, ts, tiles_per_step):
    bsz, s, d = xm.shape
    t = bsz * s
    tm = ts * tiles_per_step
    nt = s // tm
    tri = (jnp.arange(ts)[:, None] < jnp.arange(ts)[None, :]).astype(BF16)
    tok = pl.BlockSpec((N_EXPERTS, tm), lambda i: (0, i))
    return pl.pallas_call(
        _route_kernel,
        grid=(t // tm,),
        in_specs=[pl.BlockSpec((1, tm, d), lambda i: (i // nt, i % nt, 0)),
                  pl.BlockSpec((1, SUBLANES, d), lambda i: (i // nt, 0, 0)),
                  _full(norm_ffn.shape), _full(wrt.shape), _full(rbias.shape), _full(tri.shape),
                  _full(lower.shape)],
        out_specs=[pl.BlockSpec((1, tm, d), lambda i: (i // nt, i % nt, 0)), tok, tok,
                   pl.BlockSpec((tiles_per_step, N_EXPERTS, LANES), lambda i: (i, 0, 0)),
                   pl.BlockSpec((tiles_per_step, SUBLANES, N_EXPERTS), lambda i: (i, 0, 0))],
        out_shape=[jax.ShapeDtypeStruct((bsz, s, d), BF16), jax.ShapeDtypeStruct((N_EXPERTS, t), F32),
                   jax.ShapeDtypeStruct((N_EXPERTS, t), BF16),
                   jax.ShapeDtypeStruct((t // ts, N_EXPERTS, LANES), F32),
                   jax.ShapeDtypeStruct((t // ts, SUBLANES, N_EXPERTS), F32)],
        compiler_params=_params("arbitrary"),
        name="route",
    )(xm, mod, norm_ffn, wrt, rbias, tri, lower)


def _pack(x):
    w = x.shape[1] // 2
    lo = lax.bitcast_convert_type(x[:, :w].astype(BF16).astype(F32), jnp.uint32)
    hi = lax.bitcast_convert_type(x[:, w:].astype(BF16).astype(F32), jnp.uint32)
    return hi | (lo >> 16)


def _unpack(u):
    lo = lax.bitcast_convert_type(u << 16, F32).astype(BF16)
    hi = lax.bitcast_convert_type(u & jnp.uint32(0xFFFF0000), F32).astype(BF16)
    return lo, hi


def _pow2_pieces(units, limit):
    bit = 1
    while bit * 2 <= limit:
        bit *= 2
    while bit:
        yield (units & bit) != 0, units & ~(2 * bit - 1), bit
        bit //= 2


def _rows_copy(vm_ref, hbm_ref, sem, vm_row, hbm_row, rows, to_hbm):
    def aligned(r):
        return r if isinstance(r, int) else pl.multiple_of(r, RUN_ALIGN)

    v = vm_ref.at[pl.ds(aligned(vm_row), rows), :]
    h = hbm_ref.at[pl.ds(aligned(hbm_row), rows), :]
    return pltpu.make_async_copy(v, h, sem) if to_hbm else pltpu.make_async_copy(h, v, sem)


def _run_copies(vm_ref, hbm_ref, sem, n8, vm_row, hbm_row, limit, to_hbm, act):
    for size in range(1, min(limit, LONG_RUN - 1) + 1):
        @pl.when(n8 == size)
        def _():
            act(_rows_copy(vm_ref, hbm_ref, sem, vm_row, hbm_row, RUN_ALIGN * size, to_hbm))

    if limit >= LONG_RUN:
        @pl.when(n8 >= LONG_RUN)
        def _():
            for on, off, size in _pow2_pieces(n8, limit):
                @pl.when(on)
                def _():
                    act(_rows_copy(vm_ref, hbm_ref, sem, vm_row + RUN_ALIGN * off, hbm_row + RUN_ALIGN * off,
                                   RUN_ALIGN * size, to_hbm))


def _wait_rows(vm_ref, hbm_ref, sem, units, limit, to_hbm):
    for on, _, size in _pow2_pieces(units, limit):
        @pl.when(on)
        def _():
            _rows_copy(vm_ref, hbm_ref, sem, 0, 0, RUN_ALIGN * size, to_hbm).wait()


def _dispatch_kernel(n8_ref, ls_ref, gs_ref, ts_ref, t8_ref, nu_ref, pos_ref, ext_ref, h_ref, xs_out, srt2, zbuf,
                     sems):
    step = pl.program_id(0)
    tm = h_ref.shape[0]
    rows = srt2.shape[1]
    slot = step % 2
    srt, sem = srt2.at[slot], sems.at[slot]
    rid = lax.broadcasted_iota(jnp.int32, (rows, 1), 0).astype(F32)
    start = ext_ref[0, 0:1, :]
    member = jnp.where((rid >= start) & (rid < ext_ref[0, 1:2, :]), 1.0, 0.0)
    offset = rid - jnp.sum(member * start, axis=1, keepdims=True)
    pos = jnp.dot(member.astype(BF16), pos_ref[...], preferred_element_type=F32)
    sel = jnp.where(pos == offset, 1.0, 0.0).astype(BF16)
    srt[...] = _pack(jnp.dot(sel, h_ref[...], preferred_element_type=F32))

    def send(e, c):
        i = step * N_EXPERTS + e
        _run_copies(srt, xs_out, sem, n8_ref[i], ls_ref[i], gs_ref[i], tm // RUN_ALIGN, True, lambda cp: cp.start())
        return c

    lax.fori_loop(0, N_EXPERTS, send, 0)

    def wait_tile(tile, s):
        last = tile * N_EXPERTS + N_EXPERTS - 1
        _wait_rows(srt2.at[s], xs_out, sems.at[s], ls_ref[last] // RUN_ALIGN + n8_ref[last], rows // RUN_ALIGN, True)

    pl.when(step > 0)(lambda: wait_tile(step - 1, 1 - slot))

    @pl.when(step == pl.num_programs(0) - 1)
    def _():
        wait_tile(step, slot)
        zbuf[...] = jnp.zeros(zbuf.shape, zbuf.dtype)
        nblk = xs_out.shape[0] // EXPERT_BLOCK

        def fill(act):
            def tails(e, c):
                _run_copies(zbuf, xs_out, sem, t8_ref[e], 0, ts_ref[e], EXPERT_BLOCK // RUN_ALIGN - 1, True, act)
                return c

            def blocks(b, c):
                act(pltpu.make_async_copy(
                    zbuf, xs_out.at[pl.ds(pl.multiple_of(b * EXPERT_BLOCK, EXPERT_BLOCK), EXPERT_BLOCK), :], sem))
                return c

            lax.fori_loop(0, N_EXPERTS, tails, 0)
            lax.fori_loop(nu_ref[0], nblk, blocks, 0)

        fill(lambda cp: cp.start())
        fill(lambda cp: cp.wait())


def _dispatch_call(tables, pos_et, ext_rows, h2, nblk, tm):
    t, d = h2.shape
    lrows = TOP_K * tm + N_EXPERTS * RUN_ALIGN
    return pl.pallas_call(
        _dispatch_kernel,
        grid_spec=pltpu.PrefetchScalarGridSpec(
            num_scalar_prefetch=len(tables), grid=(t // tm,),
            in_specs=[pl.BlockSpec((N_EXPERTS, tm), lambda i, *_: (0, i)),
                      pl.BlockSpec((1,) + ext_rows.shape[1:], lambda i, *_: (i, 0, 0)),
                      pl.BlockSpec((tm, d), lambda i, *_: (i, 0))],
            out_specs=pl.BlockSpec(memory_space=pl.ANY),
            scratch_shapes=[pltpu.VMEM((2, lrows, d // 2), jnp.uint32),
                            pltpu.VMEM((EXPERT_BLOCK, d // 2), jnp.uint32), pltpu.SemaphoreType.DMA((2,))]),
        out_shape=jax.ShapeDtypeStruct((nblk * EXPERT_BLOCK, d // 2), jnp.uint32),
        compiler_params=_params("arbitrary"),
        name="dispatch",
    )(*tables, pos_et, ext_rows, h2)


def _expert_kernel(blk_ref, nused_ref, x_ref, wgu_ref, wd_ref, y_ref):
    used = pl.program_id(0) < nused_ref[0]

    @pl.when(used)
    def _():
        lo, hi = _unpack(x_ref[...])
        half = lo.shape[1]
        gu = (jnp.dot(lo, wgu_ref[0, :half, :], preferred_element_type=F32)
              + jnp.dot(hi, wgu_ref[0, half:, :], preferred_element_type=F32))
        a = _silu(gu[:, :EXPERT_FF]) * gu[:, EXPERT_FF:]
        y_ref[...] = _pack(_dot(a, wd_ref[0]))

    @pl.when(jnp.logical_not(used))
    def _():
        y_ref[...] = jnp.zeros(y_ref.shape, y_ref.dtype)


def _expert_call(blk_e, nused, xs, wgu, wd):
    rows, d = xs.shape
    nblk = rows // EXPERT_BLOCK

    def row_map(i, blk, nu):
        return (jnp.minimum(i, nu[0] - 1), 0)

    return pl.pallas_call(
        _expert_kernel,
        grid_spec=pltpu.PrefetchScalarGridSpec(
            num_scalar_prefetch=2, grid=(nblk,),
            in_specs=[pl.BlockSpec((EXPERT_BLOCK, d), row_map),
                      pl.BlockSpec((1,) + wgu.shape[1:], lambda i, blk, nu: (blk[i], 0, 0)),
                      pl.BlockSpec((1,) + wd.shape[1:], lambda i, blk, nu: (blk[i], 0, 0))],
            out_specs=pl.BlockSpec((EXPERT_BLOCK, d), lambda i, blk, nu: (i, 0))),
        out_shape=jax.ShapeDtypeStruct((rows, d), jnp.uint32),
        compiler_params=_params("arbitrary"),
        name="expert",
    )(blk_e, nused, xs, wgu, wd)


def _combine_kernel(n8_ref, ls_ref, gs_ref, ys_hbm, pos_ref, w_ref, ext_ref, xm_ref, h_ref, mod_ref, wsgu_ref,
                    wsd_ref, fn_ref, o_ref, ybuf2, sems):
    step = pl.program_id(0)
    tm = xm_ref.shape[0]
    rows = ybuf2.shape[1]
    slot = step % 2
    ybuf, sem = ybuf2.at[slot], sems.at[slot]

    def fetch(tile, s):
        def body(e, c):
            i = tile * N_EXPERTS + e
            _run_copies(ybuf2.at[s], ys_hbm, sems.at[s], n8_ref[i], ls_ref[i], gs_ref[i], tm // RUN_ALIGN, False,
                        lambda cp: cp.start())
            return c
        lax.fori_loop(0, N_EXPERTS, body, 0)

    pl.when(step == 0)(lambda: fetch(step, slot))
    pl.when(step + 1 < pl.num_programs(0))(lambda: fetch(step + 1, 1 - slot))
    gu = _dot(h_ref[...], wsgu_ref[...])
    ff = gu.shape[1] // 2
    shared = _dot(_silu(gu[:, :ff]) * gu[:, ff:], wsd_ref[...])
    cid = lax.broadcasted_iota(jnp.int32, (1, rows), 1).astype(F32)
    start = ext_ref[0, :, 1:2]
    member = jnp.where((cid >= start) & (cid < ext_ref[0, :, 2:3]), 1.0, 0.0)
    offset = cid - jnp.sum(member * start, axis=0, keepdims=True)
    member = member.astype(BF16)
    pos = jnp.dot(pos_ref[...], member, preferred_element_type=F32)
    mix = jnp.where(pos == offset, jnp.dot(w_ref[...].astype(BF16), member, preferred_element_type=F32), 0.0)
    mix = mix.astype(BF16)
    last = step * N_EXPERTS + N_EXPERTS - 1
    filled = ls_ref[last] + RUN_ALIGN * n8_ref[last]
    _wait_rows(ybuf, ys_hbm, sem, filled // RUN_ALIGN, rows // RUN_ALIGN, False)
    rid = lax.broadcasted_iota(jnp.int32, (rows, 1), 0)
    lo, hi = _unpack(jnp.where(rid < filled, ybuf[...], jnp.uint32(0)))
    routed = jnp.concatenate([jnp.dot(mix, lo, preferred_element_type=F32),
                              jnp.dot(mix, hi, preferred_element_type=F32)], axis=1)
    x = xm_ref[...] + mod_ref[0, 5:6, :] * (routed + shared)
    o_ref[...] = _rms(x, fn_ref[...])


def _combine_call(tables, ys, pos_te, w_te, ext_cols, xm, h2, mod, wsgu, wsd, final_norm, tm, tiles_per_batch):
    t, d = xm.shape
    lrows = TOP_K * tm + N_EXPERTS * RUN_ALIGN
    tok = pl.BlockSpec((tm, d), lambda i, *_: (i, 0))
    per_e = pl.BlockSpec((tm, N_EXPERTS), lambda i, *_: (i, 0))
    return pl.pallas_call(
        _combine_kernel,
        grid_spec=pltpu.PrefetchScalarGridSpec(
            num_scalar_prefetch=len(tables), grid=(t // tm,),
            in_specs=[pl.BlockSpec(memory_space=pl.ANY), per_e, per_e,
                      pl.BlockSpec((1,) + ext_cols.shape[1:], lambda i, *_: (i, 0, 0)), tok, tok,
                      pl.BlockSpec((1, SUBLANES, d), lambda i, *_: (i // tiles_per_batch, 0, 0)),
                      _full(wsgu.shape), _full(wsd.shape), _full(final_norm.shape)],
            out_specs=tok,
            scratch_shapes=[pltpu.VMEM((2, lrows, d // 2), jnp.uint32), pltpu.SemaphoreType.DMA((2,))]),
        out_shape=jax.ShapeDtypeStruct((t, d), F32),
        compiler_params=_params("arbitrary"),
        name="combine",
    )(*tables, ys, pos_te, w_te, ext_cols, xm, h2, mod, wsgu, wsd, final_norm)


def _moe(xm, mod, norm_ffn, w_router, router_bias, wg, wu, wd, wsg, wsu, wsd, final_norm, tm):
    bsz, s, d = xm.shape
    t = bsz * s
    nt = t // tm
    perm = (np.arange(N_EXPERTS) % N_GROUPS) * GROUP_SIZE + np.arange(N_EXPERTS) // N_GROUPS
    wrt = w_router.T[perm]
    rbias = router_bias[perm][:, None]
    lower = jnp.asarray(perm[None, :] < perm[:, None], BF16)
    h2, w_et, pos_et, ext_cols, ext_rows = _route_call(xm, mod, norm_ffn, wrt, rbias, lower, tm, ROUTE_TILES)

    inv = np.argsort(perm)
    n8 = (ext_cols[:, :, 0].astype(jnp.int32)[:, inv] + (RUN_ALIGN - 1)) // RUN_ALIGN
    run = RUN_ALIGN * n8
    ls = jnp.cumsum(run, axis=1) - run
    tot = jnp.sum(run, axis=0)
    padded = (tot + EXPERT_BLOCK - 1) // EXPERT_BLOCK * EXPERT_BLOCK
    pad_end = jnp.cumsum(padded)
    gs = (pad_end - padded)[None, :] + jnp.cumsum(run, axis=0) - run
    nblk = -(-(t * TOP_K + nt * N_EXPERTS * (RUN_ALIGN - 1)) // EXPERT_BLOCK) + N_EXPERTS
    blk_first = jnp.arange(nblk, dtype=jnp.int32)[:, None] * EXPERT_BLOCK
    blk_e = jnp.minimum(jnp.sum((pad_end[None, :] <= blk_first).astype(jnp.int32), axis=1), N_EXPERTS - 1)
    nused = (pad_end[-1:] // EXPERT_BLOCK).astype(jnp.int32)
    tables = [a.reshape(-1).astype(jnp.int32) for a in (n8, ls, gs)]
    tails = [(pad_end - padded + tot).astype(jnp.int32), ((padded - tot) // RUN_ALIGN).astype(jnp.int32), nused]

    h2f = h2.reshape(t, d)
    xs = _dispatch_call(tables + tails, pos_et, ext_rows, h2f, nblk, tm)
    wgu = jnp.concatenate([wg, wu], axis=2).astype(BF16)
    ys = _expert_call(blk_e, nused, xs, wgu, wd.astype(BF16))
    wsgu = jnp.concatenate([wsg, wsu], axis=1).astype(BF16)
    out = _combine_call(tables, ys, pos_et.T, w_et.T, ext_cols, xm.reshape(t, d), h2f, mod, wsgu, wsd.astype(BF16),
                        final_norm, tm, s // tm)
    return out.reshape(bsz, s, d)


def _rope_tables(s):
    rows = s // GRID_W
    row = jnp.broadcast_to(jnp.arange(rows, dtype=F32)[:, None], (rows, GRID_W)).reshape(-1)
    col = jnp.broadcast_to(jnp.arange(GRID_W, dtype=F32)[None, :], (rows, GRID_W)).reshape(-1)
    half = QK_ROPE // 2
    inv_freq = ROPE_THETA ** (-jnp.arange(0, half, 2, dtype=F32) / half)
    ar, ac = row[:, None] * inv_freq, col[:, None] * inv_freq
    ones = jnp.ones((s, QK_NOPE), F32)
    tail = HEAD_PAD - QK_NOPE - QK_ROPE
    cos_t = jnp.concatenate([ones, jnp.cos(ar), jnp.cos(ar), jnp.cos(ac), jnp.cos(ac), jnp.ones((s, tail), F32)], 1)
    sin_t = jnp.concatenate([0 * ones, -jnp.sin(ar), jnp.sin(ar), -jnp.sin(ac), jnp.sin(ac),
                             jnp.zeros((s, tail), F32)], 1)
    return cos_t, sin_t


_Q4 = QK_ROPE // 4
ROPE_SWAP = np.concatenate([np.arange(_Q4, 2 * _Q4), np.arange(0, _Q4), np.arange(3 * _Q4, 4 * _Q4),
                            np.arange(2 * _Q4, 3 * _Q4)])


def _rope_slot(w, swap):
    if swap:
        w = w[..., ROPE_SWAP]
    pad = [(0, 0)] * (w.ndim - 1) + [(QK_NOPE, HEAD_PAD - QK_NOPE - QK_ROPE)]
    return jnp.pad(w, pad)


def kernel(x, c, ctx, c_ctx, w_mod, b_mod, norm_mix, norm_ffn, w_in, b_in, q_norm, w_uq, kv_norm, w_ukv, w_branch_attn, hy_conv_w, hy_conv_b, hy_filt_w1, hy_filt_b1, hy_filt_w2, hy_filt_b2, hy_filt_w3, hy_filt_freq, hy_skip, w_branch_hyena, w_out, w_router, router_bias, w_exp_gate, w_exp_up, w_exp_down, w_sh_gate, w_sh_up, w_sh_down, final_norm,
           tiles=None):
    bsz, s, d = x.shape
    tl = dict(inproj=512, tq=2048, tk=1408, fft_kb=8, merge=512, moe=256)
    tl.update(tiles or {})
    assert w_mod.shape[0] == 1, "single-layer trunk"
    i = 0

    rows = -(-(bsz + 1) // SUBLANES) * SUBLANES
    c_rows = jnp.pad(jnp.concatenate([c, c_ctx[None]], axis=0), ((0, rows - bsz - 1), (0, 0)))
    mod_all = _mod_call(c_rows, w_mod[i], b_mod[i])
    mod_all = jnp.pad(mod_all.reshape(rows, 6, d), ((0, 0), (0, SUBLANES - 6), (0, 0)))
    mod, modc = mod_all[:bsz], mod_all[bsz:bsz + 1]

    cuts = np.cumsum([Q_LORA, KV_LORA, QK_ROPE, 3 * HY_WIDTH])
    wi, bi = w_in[i], b_in[i][None]
    w_q, w_kv, w_pe, w_hy, w_g = jnp.split(wi, cuts, axis=1)
    b_q, b_kv, b_pe, b_hy, b_g = jnp.split(bi, cuts, axis=1)
    wa = jnp.concatenate([w_q, w_kv, _rope_slot(w_pe, False), _rope_slot(w_pe, True)], axis=1).astype(BF16)
    ba = jnp.concatenate([b_q, b_kv, _rope_slot(b_pe, False), _rope_slot(b_pe, True)], axis=1)
    wq3 = w_uq[i].reshape(Q_LORA, N_HEADS, QK_NOPE + QK_ROPE) * (ATTN_SCALE * math.log2(math.e))
    tail = ((0, 0), (0, 0), (0, HEAD_PAD - QK_NOPE))
    wuq = (jnp.pad(wq3[..., :QK_NOPE], tail) + _rope_slot(wq3[..., QK_NOPE:], False)).reshape(Q_LORA, -1).astype(BF16)
    wuqs = _rope_slot(wq3[..., QK_NOPE:], True).reshape(Q_LORA, -1).astype(BF16)
    wkv3 = w_ukv[i].reshape(KV_LORA, N_HEADS, QK_NOPE + V_HEAD)
    wuk = jnp.pad(wkv3[..., :QK_NOPE], tail).reshape(KV_LORA, -1).astype(BF16)
    wuvt = wkv3[..., QK_NOPE:].reshape(KV_LORA, -1).T.astype(BF16)
    nm, qn, kvn = norm_mix[i][None], q_norm[i][None], kv_norm[i][None]

    w_c = jnp.concatenate([w_kv, _rope_slot(w_pe, False)], axis=1).astype(BF16)
    b_c = jnp.concatenate([b_kv, _rope_slot(b_pe, False)], axis=1)
    ck, cvt = _ctx_call(ctx, modc, nm, w_c, b_c, kvn, wuk, wuvt)

    cos_t, sin_t = _rope_tables(s)
    q, k, vt, hv, hx1, hx2, gate = _inproj_call(
        x, mod, nm, wa, ba, w_hy.astype(BF16), b_hy, w_g.astype(BF16), b_g, qn, wuq, wuqs, kvn, wuk, wuvt,
        cos_t, sin_t, hy_conv_w[i], hy_conv_b[i][None], tl["inproj"])

    attn = _attn_call(q, jnp.concatenate([ck, k], axis=2), jnp.concatenate([cvt, vt], axis=3), tl["tq"], tl["tk"])
    hy = _hyena(hv, hx1, hx2, hy_filt_w1[i], hy_filt_b1[i], hy_filt_w2[i], hy_filt_b2[i], hy_filt_w3[i],
                hy_filt_freq[i], hy_skip[i], tl["fft_kb"])
    xm = _merge_call(x, attn, hy, gate, mod, w_branch_attn[i].astype(BF16), w_branch_hyena[i].astype(BF16),
                     w_out[i].astype(BF16), tl["merge"])
    return _moe(xm, mod, norm_ffn[i][None], w_router[i], router_bias[i], w_exp_gate[i], w_exp_up[i], w_exp_down[i],
                w_sh_gate[i], w_sh_up[i], w_sh_down[i], final_norm[None], tl["moe"])
```

```python
import functools
import math

import numpy as np
import jax
import jax.numpy as jnp
from jax import lax
from jax.experimental import pallas as pl
from jax.experimental.pallas import tpu as pltpu

GRID_W = 64
N_HEADS = 8
QK_NOPE = 64
QK_ROPE = 32
V_HEAD = 64
Q_LORA = 256
KV_LORA = 128
ROPE_THETA = 10000.0
ATTN_SCALE = 1.0 / math.sqrt(QK_NOPE + QK_ROPE)
HY_WIDTH = 512
HY_ORDER = 2
HY_SHORT = 3
HY_BANDS = 8
HY_EMB = 1 + 2 * HY_BANDS
HY_EMB_PAD = 32
HY_FAST_DECAY = 0.3
HY_SLOW_DECAY = 1.5
HY_DECAY_TARGET = 1e-2
N_EXPERTS = 64
N_GROUPS = 8
GROUP_SIZE = N_EXPERTS // N_GROUPS
TOPK_GROUPS = 4
TOP_K = 8
EXPERT_FF = 256
ROUTE_SCALE = 2.5
EXPERT_BLOCK = 1024
RUN_ALIGN = 8
ROUTE_TILES = 8
LONG_RUN = 8
NORM_EPS = 1e-6

HEAD_PAD = 128
Q_CHUNK = 512
AHEAD = 2
LANES = 128
SUBLANES = 8
VMEM_LIMIT = 48 * 1024 * 1024

F32 = jnp.float32
BF16 = jnp.bfloat16
NT_DIMS = (((1,), (1,)), ((), ()))
NN_DIMS = (((1,), (0,)), ((), ()))


def _params(*sem):
    return pltpu.CompilerParams(dimension_semantics=sem, vmem_limit_bytes=VMEM_LIMIT)


def _dot(a, b):
    return jnp.dot(a.astype(BF16), b.astype(BF16), preferred_element_type=F32)


def _split(a):
    hi = a.astype(BF16)
    lo = (a - hi.astype(F32)).astype(BF16)
    return hi, lo


def _dot3(a, b, dims=NN_DIMS):
    ah, al = _split(a)
    bh, bl = _split(b)
    d = functools.partial(lax.dot_general, dimension_numbers=dims, preferred_element_type=F32)
    return d(ah, bh) + (d(ah, bl) + d(al, bh))


def _rms(x, g):
    return x * lax.rsqrt(jnp.mean(x * x, axis=-1, keepdims=True) + NORM_EPS) * g


def _silu(x):
    return x * jax.nn.sigmoid(x)


def _full(shape):
    nd = len(shape)
    return pl.BlockSpec(shape, lambda *_: (0,) * nd)


def _mod_kernel(c_ref, w_ref, b_ref, o_ref):
    o_ref[...] = _dot3(_silu(c_ref[...]), w_ref[...]) + b_ref[...]


def _mod_call(c_rows, w_mod, b_mod):
    r, d = c_rows.shape
    n = w_mod.shape[1]
    bn = 1024
    return pl.pallas_call(
        _mod_kernel,
        grid=(n // bn,),
        in_specs=[_full((r, d)), pl.BlockSpec((d, bn), lambda j: (0, j)), pl.BlockSpec((1, bn), lambda j: (0, j))],
        out_specs=pl.BlockSpec((r, bn), lambda j: (0, j)),
        out_shape=jax.ShapeDtypeStruct((r, n), F32),
        compiler_params=_params("arbitrary"),
        name="mod",
    )(c_rows, w_mod, b_mod.reshape(1, n))


def _prenorm(x, mod_ref, row, g):
    shift = mod_ref[0, row:row + 1, :]
    scale = mod_ref[0, row + 1:row + 2, :]
    return _rms(x, g) * (1.0 + scale) + shift


def _kv_heads(kv_lat, kpe, kvn_ref, wuk_ref, wuvt_ref, k_out, vt_out):
    kvn = _rms(kv_lat, kvn_ref[...]).astype(BF16)
    kk = _dot(kvn, wuk_ref[...])
    vt = lax.dot_general(wuvt_ref[...], kvn, NT_DIMS, preferred_element_type=F32)
    ones = jnp.ones((HEAD_PAD - V_HEAD, vt.shape[1]), F32)
    for h in range(N_HEADS):
        k_out[0, h] = (kk[:, HEAD_PAD * h:HEAD_PAD * (h + 1)] + kpe).astype(BF16)
        vt_out[0, h] = jnp.concatenate([vt[V_HEAD * h:V_HEAD * (h + 1)], ones], axis=0).astype(BF16)


def _ctx_kernel(c_ref, mod_ref, nm_ref, w_ref, b_ref, kvn_ref, wuk_ref, wuv_ref, k_out, v_out):
    h = _prenorm(c_ref[0], mod_ref, 0, nm_ref[...]).astype(BF16)
    a = _dot(h, w_ref[...]) + b_ref[...]
    _kv_heads(a[:, :KV_LORA], a[:, KV_LORA:], kvn_ref, wuk_ref, wuv_ref, k_out, v_out)


def _ctx_call(ctx, modc, norm_mix, w_c, b_c, kv_norm, w_uk, w_uv):
    bsz, n, d = ctx.shape
    return pl.pallas_call(
        _ctx_kernel,
        grid=(bsz,),
        in_specs=[pl.BlockSpec((1, n, d), lambda b: (b, 0, 0)), _full(modc.shape), _full(norm_mix.shape),
                  _full(w_c.shape), _full(b_c.shape), _full(kv_norm.shape), _full(w_uk.shape), _full(w_uv.shape)],
        out_specs=[pl.BlockSpec((1, N_HEADS, n, HEAD_PAD), lambda b: (b, 0, 0, 0)),
                   pl.BlockSpec((1, N_HEADS, HEAD_PAD, n), lambda b: (b, 0, 0, 0))],
        out_shape=[jax.ShapeDtypeStruct((bsz, N_HEADS, n, HEAD_PAD), BF16),
                   jax.ShapeDtypeStruct((bsz, N_HEADS, HEAD_PAD, n), BF16)],
        compiler_params=_params("arbitrary"),
        name="ctx",
    )(ctx, modc, norm_mix, w_c, b_c, kv_norm, w_uk, w_uv)


def _inproj_kernel(x_ref, xp_ref, xn_ref, mod_ref, nm_ref, wa_ref, ba_ref, why_ref, bhy_ref, wg_ref, bg_ref,
                   qn_ref, wuq_ref, wuqs_ref, kvn_ref, wuk_ref, wuv_ref, cos_ref, sin_ref, cw_ref, cb_ref,
                   q_out, k_out, v_out, hv_out, hx1_out, hx2_out, g_out):
    i = pl.program_id(0)
    tm = x_ref.shape[1]
    nm = nm_ref[...]
    h = _prenorm(x_ref[0], mod_ref, 0, nm).astype(BF16)
    a = _dot(h, wa_ref[...]) + ba_ref[...]
    q_lat = a[:, :Q_LORA]
    kv_lat = a[:, Q_LORA:Q_LORA + KV_LORA]
    kpe_m = a[:, Q_LORA + KV_LORA:Q_LORA + KV_LORA + HEAD_PAD]
    kpe_s = a[:, Q_LORA + KV_LORA + HEAD_PAD:]
    cos = cos_ref[...]
    sin = sin_ref[...]
    qn = _rms(q_lat, qn_ref[...]).astype(BF16)
    qa = _dot(qn, wuq_ref[...])
    qs = _dot(qn, wuqs_ref[...])
    for hh in range(N_HEADS):
        sl = slice(HEAD_PAD * hh, HEAD_PAD * (hh + 1))
        q_out[0, hh] = (qa[:, sl] * cos + qs[:, sl] * sin).astype(BF16)
    _kv_heads(kv_lat, kpe_m * cos + kpe_s * sin, kvn_ref, wuk_ref, wuv_ref, k_out, v_out)
    g_out[0] = (_dot(h, wg_ref[...]) + bg_ref[...]).astype(BF16)

    why = why_ref[...]
    bhy = bhy_ref[...]
    halo = jnp.concatenate([_prenorm(xp_ref[0], mod_ref, 0, nm), _prenorm(xn_ref[0], mod_ref, 0, nm)], axis=0)
    hy_all = _dot(jnp.concatenate([h, halo.astype(BF16)], axis=0), why) + bhy
    hy = hy_all[:tm]
    prev = jnp.where(i == 0, 0.0, hy_all[tm + SUBLANES - 1:tm + SUBLANES])
    nxt = jnp.where(i == pl.num_programs(0) - 1, 0.0, hy_all[tm + SUBLANES:tm + SUBLANES + 1])
    rid = lax.broadcasted_iota(jnp.int32, (tm, 1), 0)
    up = jnp.where(rid == 0, prev, pltpu.roll(hy, 1, 0))
    dn = jnp.where(rid == tm - 1, nxt, pltpu.roll(hy, tm - 1, 0))
    u = up * cw_ref[0:1, :] + hy * cw_ref[1:2, :] + dn * cw_ref[2:3, :] + cb_ref[...]
    hv_out[0] = u[:, :HY_WIDTH]
    hx1_out[0] = u[:, HY_WIDTH:2 * HY_WIDTH]
    hx2_out[0] = u[:, 2 * HY_WIDTH:]


def _inproj_call(x, mod, norm_mix, wa, ba, why, bhy, wg, bg, q_norm, wuq, wuqs, kv_norm, wuk, wuvt, cos_t, sin_t, cw,
                 cb, tm):
    bsz, s, d = x.shape
    nt = s // tm
    rb = tm // SUBLANES
    last_rb = s // SUBLANES - 1
    consts = [norm_mix, wa, ba, why, bhy, wg, bg, q_norm, wuq, wuqs, kv_norm, wuk, wuvt]
    in_specs = [
        pl.BlockSpec((1, tm, d), lambda i, b: (b, i, 0)),
        pl.BlockSpec((1, SUBLANES, d), lambda i, b: (b, jnp.maximum(i * rb - 1, 0), 0)),
        pl.BlockSpec((1, SUBLANES, d), lambda i, b: (b, jnp.minimum((i + 1) * rb, last_rb), 0)),
        pl.BlockSpec((1, SUBLANES, d), lambda i, b: (b, 0, 0)),
    ] + [_full(c.shape) for c in consts] + [
        pl.BlockSpec((tm, HEAD_PAD), lambda i, b: (i, 0)),
        pl.BlockSpec((tm, HEAD_PAD), lambda i, b: (i, 0)),
        _full(cw.shape), _full(cb.shape),
    ]
    hw = HY_WIDTH
    out_specs = [
        pl.BlockSpec((1, N_HEADS, tm, HEAD_PAD), lambda i, b: (b, 0, i, 0)),
        pl.BlockSpec((1, N_HEADS, tm, HEAD_PAD), lambda i, b: (b, 0, i, 0)),
        pl.BlockSpec((1, N_HEADS, HEAD_PAD, tm), lambda i, b: (b, 0, 0, i)),
        pl.BlockSpec((1, tm, hw), lambda i, b: (b, i, 0)),
        pl.BlockSpec((1, tm, hw), lambda i, b: (b, i, 0)),
        pl.BlockSpec((1, tm, hw), lambda i, b: (b, i, 0)),
        pl.BlockSpec((1, tm, 2 * d), lambda i, b: (b, i, 0)),
    ]
    out_shape = [
        jax.ShapeDtypeStruct((bsz, N_HEADS, s, HEAD_PAD), BF16),
        jax.ShapeDtypeStruct((bsz, N_HEADS, s, HEAD_PAD), BF16),
        jax.ShapeDtypeStruct((bsz, N_HEADS, HEAD_PAD, s), BF16),
        jax.ShapeDtypeStruct((bsz, s, hw), F32),
        jax.ShapeDtypeStruct((bsz, s, hw), F32),
        jax.ShapeDtypeStruct((bsz, s, hw), F32),
        jax.ShapeDtypeStruct((bsz, s, 2 * d), BF16),
    ]
    return pl.pallas_call(
        _inproj_kernel,
        grid=(nt, bsz),
        in_specs=in_specs,
        out_specs=out_specs,
        out_shape=out_shape,
        compiler_params=_params("arbitrary", "arbitrary"),
        name="inproj",
    )(x, x, x, mod, *consts, cos_t, sin_t, cw, cb)


def _attn_kernel(q_ref, k_ref, vt_ref, o_ref, m_sc, acc_sc):
    j = pl.program_id(2)

    @pl.when(j == 0)
    def _():
        m_sc[...] = jnp.full(m_sc.shape, -jnp.inf, F32)
        acc_sc[...] = jnp.zeros(acc_sc.shape, F32)

    tq = q_ref.shape[2]
    qw = min(tq, Q_CHUNK)
    units = [(h, c) for h in range(N_HEADS) for c in range(0, tq, qw)]

    def scores(u):
        h, c = units[u]
        return lax.dot_general(k_ref[0, h], q_ref[0, h, c:c + qw, :], NT_DIMS,
                               preferred_element_type=F32)

    pending = [scores(u) for u in range(AHEAD)]
    for u, (h, c) in enumerate(units):
        if u + AHEAD < len(units):
            pending.append(scores(u + AHEAD))
        st = pending.pop(0)
        m_prev = m_sc[h, :, c:c + qw]
        m_new = jnp.maximum(m_prev, jnp.max(st, axis=0, keepdims=True))
        pt = jnp.exp2(st - m_new).astype(BF16)
        acc_sc[h, :, c:c + qw] = (jnp.exp2(m_prev - m_new) * acc_sc[h, :, c:c + qw]
                                  + jnp.dot(vt_ref[0, h], pt, preferred_element_type=F32))
        m_sc[h, :, c:c + qw] = m_new

    @pl.when(j == pl.num_programs(2) - 1)
    def _():
        ot = jnp.concatenate([acc_sc[h, :V_HEAD] / acc_sc[h, V_HEAD:V_HEAD + 1] for h in range(N_HEADS)], axis=0)
        o_ref[0] = ot.T.astype(o_ref.dtype)


def _attn_call(q, k, vt, tq, tk):
    bsz, nh, s, dh = q.shape
    nk = k.shape[2]
    dv = nh * V_HEAD
    return pl.pallas_call(
        _attn_kernel,
        grid=(bsz, s // tq, nk // tk),
        in_specs=[
            pl.BlockSpec((1, nh, tq, dh), lambda b, i, j: (b, 0, i, 0)),
            pl.BlockSpec((1, nh, tk, dh), lambda b, i, j: (b, 0, j, 0)),
            pl.BlockSpec((1, nh, dh, tk), lambda b, i, j: (b, 0, 0, j)),
        ],
        out_specs=pl.BlockSpec((1, tq, dv), lambda b, i, j: (b, i, 0)),
        out_shape=jax.ShapeDtypeStruct((bsz, s, dv), BF16),
        scratch_shapes=[pltpu.VMEM((nh, 1, tq), F32), pltpu.VMEM((nh, dh, tq), F32)],
        compiler_params=_params("arbitrary", "arbitrary", "arbitrary"),
        name="attn",
    )(q, k, vt)


def _filter_kernel(emb_ref, w1_ref, b1_ref, w2_ref, b2_ref, w3_ref, fr_ref, dl_ref, full_out, asum_out, *, seq):
    r = pl.program_id(0)
    rb = emb_ref.shape[0]
    emb = emb_ref[...]
    fr = fr_ref[...]
    h = jnp.sin(fr * (_dot3(emb, w1_ref[...]) + b1_ref[...]))
    h = jnp.sin(fr * (_dot3(h, w2_ref[...]) + b2_ref[...]))
    k = _dot3(h, w3_ref[0]) * jnp.exp(-emb[:, 0:1] * dl_ref[...])
    row = r * rb + lax.broadcasted_iota(jnp.int32, (rb, 1), 0)
    k = jnp.where(row == seq, 0.0, k)
    full_out[...] = k

    @pl.when(r == 0)
    def _():
        asum_out[...] = jnp.zeros(asum_out.shape, F32)

    asum_out[...] += jnp.sum(jnp.abs(k), axis=0, keepdims=True)


def _filter_call(emb, w1, b1, w2, b2, w3sel, freq, deltas2, seq, rb):
    n2 = emb.shape[0]
    half_blocks = seq // rb
    width = w3sel.shape[2]
    return pl.pallas_call(
        functools.partial(_filter_kernel, seq=seq),
        grid=(n2 // rb,),
        in_specs=[pl.BlockSpec((rb, HY_EMB_PAD), lambda r: (r, 0)), _full(w1.shape), _full(b1.shape),
                  _full(w2.shape), _full(b2.shape),
                  pl.BlockSpec((1,) + w3sel.shape[1:], lambda r: (r // half_blocks, 0, 0)),
                  _full(freq.shape), _full(deltas2.shape)],
        out_specs=[pl.BlockSpec((rb, width), lambda r: (r, 0)), pl.BlockSpec((1, width), lambda r: (0, 0))],
        out_shape=[jax.ShapeDtypeStruct((n2, width), F32), jax.ShapeDtypeStruct((1, width), F32)],
        compiler_params=_params("arbitrary"),
        name="filt",
    )(emb, w1, b1, w2, b2, w3sel, freq, deltas2)


def _fa_kernel(u_ref, f_ref, a_out):
    two, _, hn, g, c = u_ref.shape
    a = _dot(f_ref[...], u_ref[...].reshape(two * hn * g, c))
    a_out[...] = (_pack(a) if a_out.dtype == jnp.uint32 else a).reshape(a_out.shape)


def _fa_call(u5, fmat, packed):
    _, p, hn, n, c = u5.shape
    g = SUBLANES
    co, dt = (c // 2, jnp.uint32) if packed else (c, F32)
    return pl.pallas_call(
        _fa_kernel,
        grid=(p, n // g),
        in_specs=[pl.BlockSpec((2, 1, hn, g, c), lambda q, j: (0, q, 0, j, 0)), _full(fmat.shape)],
        out_specs=pl.BlockSpec((1, 2, n, g, co), lambda q, j: (q, 0, 0, j, 0)),
        out_shape=jax.ShapeDtypeStruct((p, 2, n, n, co), dt),
        compiler_params=_params("arbitrary", "arbitrary"),
        name="fa",
    )(u5, fmat)


def _dot_packed(w, u):
    lo, hi = _unpack(u)
    return jnp.concatenate([jnp.dot(w, lo, preferred_element_type=F32), jnp.dot(w, hi, preferred_element_type=F32)],
                           axis=1)


def _fb_kernel(a_ref, g_ref, asum_ref, kf_out):
    _, two, kb, n, c = a_ref.shape
    scale = 1.0 / (asum_ref[...] + 1e-6)
    for kk in range(kb):
        x = _dot(g_ref[kk], a_ref[0, :, kk].reshape(two * n, c)) * scale
        kf_out[kk] = x.reshape(two, n, c)


def _fb_call(a5, gmat, asum, kb):
    _, _, n, _, c = a5.shape
    return pl.pallas_call(
        _fb_kernel,
        grid=(n // kb,),
        in_specs=[pl.BlockSpec((1, 2, kb, n, c), lambda k: (0, 0, k, 0, 0)),
                  pl.BlockSpec((kb, 2 * n, 2 * n), lambda k: (k, 0, 0)), _full(asum.shape)],
        out_specs=pl.BlockSpec((kb, 2, n, c), lambda k: (k, 0, 0, 0)),
        out_shape=jax.ShapeDtypeStruct((n, 2, n, c), F32),
        compiler_params=_params("arbitrary"),
        name="fb",
    )(a5, gmat, asum)


def _mid_kernel(a_ref, g_ref, h_ref, kf_ref, b_out):
    _, two, kb, n, c = a_ref.shape
    for kk in range(kb):
        x = _dot_packed(g_ref[kk], a_ref[0, :, kk].reshape(two * n, c))
        xr, xi = x[:n], x[n:]
        kr, ki = kf_ref[kk, 0], kf_ref[kk, 1]
        y = jnp.concatenate([xr * kr - xi * ki, xr * ki + xi * kr], axis=0)
        b_out[0, :, kk] = _pack(_dot(h_ref[kk], y)).reshape(two, n, c)


def _mid_call(a5, gmat, hmat, kf, order, kb):
    p, _, n, _, c = a5.shape
    return pl.pallas_call(
        _mid_kernel,
        grid=(n // kb, p),
        in_specs=[pl.BlockSpec((1, 2, kb, n, c), lambda k, q: (q, 0, k, 0, 0)),
                  pl.BlockSpec((kb, 2 * n, 2 * n), lambda k, q: (k, 0, 0)),
                  pl.BlockSpec((kb, 2 * n, 2 * n), lambda k, q: (k, 0, 0)),
                  pl.BlockSpec((kb, 2, n, 2 * c), lambda k, q: (k, 0, 0, order))],
        out_specs=pl.BlockSpec((1, 2, kb, n, c), lambda k, q: (q, 0, k, 0, 0)),
        out_shape=jax.ShapeDtypeStruct(a5.shape, jnp.uint32),
        compiler_params=_params("arbitrary", "arbitrary"),
        name="mid",
    )(a5, gmat, hmat, kf)


def _fc_kernel(b_ref, f_ref, u_ref, m_ref, skip_ref, o_out):
    _, two, n, g, c = b_ref.shape
    y = _dot_packed(f_ref[...], b_ref[...].reshape(two * n * g, c)).reshape(u_ref.shape)
    o_out[...] = m_ref[...] * (y + u_ref[...] * skip_ref[...])


def _fc_call(b5, finv, u5, m5, skip_row):
    _, p, hn, n, c = u5.shape
    g = SUBLANES
    blk = pl.BlockSpec((2, 1, hn, g, c), lambda q, j: (0, q, 0, j, 0))
    return pl.pallas_call(
        _fc_kernel,
        grid=(p, n // g),
        in_specs=[pl.BlockSpec((1, 2, n, g, c // 2), lambda q, j: (q, 0, 0, j, 0)), _full(finv.shape), blk, blk,
                  _full(skip_row.shape)],
        out_specs=blk,
        out_shape=jax.ShapeDtypeStruct(u5.shape, F32),
        compiler_params=_params("arbitrary", "arbitrary"),
        name="fc",
    )(b5, finv, u5, m5, skip_row)


def _dft_tables(n):
    hn = n // 2
    k = np.arange(n)[:, None]
    ang = -2.0 * np.pi * (k * np.arange(n)[None, :] % n) / n
    fr, fi = np.cos(ang), np.sin(ang)
    f_data = np.block([[fr[:, :hn], -fi[:, :hn]], [fi[:, :hn], fr[:, :hn]]])
    f_filt = np.concatenate([fr, fi], axis=0)
    er, ei = fr[:hn], -fi[:hn]
    f_inv = np.block([[er, -ei], [ei, er]]) / float(n * n)
    k1 = jnp.arange(n, dtype=jnp.int32)[:, None, None]
    k2 = jnp.arange(n, dtype=jnp.int32)[None, :, None]
    m2 = jnp.arange(n, dtype=jnp.int32)[None, None, :]
    idx = (m2 * (k1 + n * k2)) % (n * n)
    ang2 = idx.astype(F32) * (-2.0 * math.pi / (n * n))
    gr, gi = jnp.cos(ang2), jnp.sin(ang2)
    g = jnp.concatenate([jnp.concatenate([gr, -gi], axis=2), jnp.concatenate([gi, gr], axis=2)], axis=1)
    h = jnp.swapaxes(g, 1, 2)

    def widen(f):
        return jnp.asarray(np.kron(f, np.eye(SUBLANES)), BF16)

    return widen(f_data), widen(f_filt), widen(f_inv), g.astype(BF16), h.astype(BF16)


def _hyena_filter_tables(seq):
    t = jnp.linspace(0.0, 1.0, seq, dtype=F32)[:, None]
    w = 2.0 * math.pi * jnp.arange(seq, dtype=F32)[:, None] / seq
    f = jnp.linspace(1e-4, HY_BANDS - 1, HY_BANDS, dtype=F32)[None, :]
    emb = jnp.concatenate([t, jnp.cos(f * w), -jnp.sin(f * w)], axis=-1)
    emb = jnp.concatenate([emb, emb[:1], emb[:0:-1]], axis=0)
    emb = jnp.pad(emb, ((0, 0), (0, HY_EMB_PAD - HY_EMB)))
    deltas = jnp.abs(jnp.linspace(math.log(HY_DECAY_TARGET) / HY_SLOW_DECAY,
                                  math.log(HY_DECAY_TARGET) / HY_FAST_DECAY, HY_WIDTH, dtype=F32))
    return emb, jnp.tile(deltas, HY_ORDER)[None, :]


def _hyena(hv, hx1, hx2, w1, b1, w2, b2, w3, freq, skip, kb):
    bsz, seq, c = hv.shape
    n = int(round(math.sqrt(2 * seq)))
    assert n * n == 2 * seq and bsz % 2 == 0
    hn, p = n // 2, bsz // 2
    f_data, f_filt, f_inv, gmat, hmat = _dft_tables(n)

    emb, deltas2 = _hyena_filter_tables(seq)
    w1p = jnp.pad(w1, ((0, HY_EMB_PAD - HY_EMB), (0, 0)))
    w3r = w3.reshape(w3.shape[0], HY_ORDER, 2, c)
    w3sel = jnp.stack([w3r[:, :, 0, :].reshape(-1, HY_ORDER * c), w3r[:, :, 1, :].reshape(-1, HY_ORDER * c)])
    full, asum = _filter_call(emb, w1p, b1[None], w2, b2[None], w3sel, freq[None], deltas2, seq, min(512, seq))
    c2 = HY_ORDER * c
    kf = _fb_call(_fa_call(full.reshape(2, 1, hn, n, c2), f_filt, False), gmat, asum, kb // HY_ORDER)

    def view(t):
        return t.reshape(2, p, hn, n, c)

    def long_conv(u5, m5, order):
        bm = _mid_call(_fa_call(u5, f_data, True), gmat, hmat, kf, order, kb)
        return _fc_call(bm, f_inv, u5, m5, skip[order][None, :])

    z = long_conv(view(hv), view(hx1), 0)
    return long_conv(z, view(hx2), 1).reshape(bsz, seq, c)


def _merge_kernel(x_ref, at_ref, hy_ref, g_ref, mod_ref, wba_ref, wbh_ref, wo_ref, o_ref):
    d = x_ref.shape[2]
    g = g_ref[0].astype(F32)
    y = (jax.nn.sigmoid(g[:, :d]) * _dot(at_ref[0], wba_ref[...])
         + jax.nn.sigmoid(g[:, d:]) * _dot(hy_ref[0], wbh_ref[...]))
    o_ref[0] = x_ref[0] + mod_ref[0, 2:3, :] * _dot(y, wo_ref[...])


def _merge_call(x, attn, hy, gate, mod, wba, wbh, wo, tm):
    bsz, s, d = x.shape

    def tok(w):
        return pl.BlockSpec((1, tm, w), lambda b, i: (b, i, 0))

    return pl.pallas_call(
        _merge_kernel,
        grid=(bsz, s // tm),
        in_specs=[tok(d), tok(attn.shape[2]), tok(hy.shape[2]), tok(2 * d),
                  pl.BlockSpec((1, SUBLANES, d), lambda b, i: (b, 0, 0)),
                  _full(wba.shape), _full(wbh.shape), _full(wo.shape)],
        out_specs=tok(d),
        out_shape=jax.ShapeDtypeStruct((bsz, s, d), F32),
        compiler_params=_params("arbitrary", "arbitrary"),
        name="merge",
    )(x, attn, hy, gate, mod, wba, wbh, wo)


def _route_kernel(xm_ref, mod_ref, nf_ref, wrt_ref, rb_ref, tri_ref, lt_ref, h2_out, w_out, p_out, col_out, row_out):
    tm = xm_ref.shape[1]
    ng, gs = N_GROUPS, GROUP_SIZE

    h2 = _prenorm(xm_ref[0], mod_ref, 3, nf_ref[...])
    h2_out[0] = h2.astype(h2_out.dtype)
    scores = jax.nn.sigmoid(_dot3(wrt_ref[...], h2, NT_DIMS))
    sel = scores + rb_ref[...]
    slabs = [sel[ng * j:ng * (j + 1)] for j in range(gs)]

    top1 = jnp.full((ng, tm), -jnp.inf, F32)
    top2 = top1
    for x in slabs:
        top2 = jnp.maximum(top2, jnp.minimum(top1, x))
        top1 = jnp.maximum(top1, x)
    gscore = top1 + top2
    gid = lax.broadcasted_iota(jnp.int32, (ng, 1), 0)
    rank = jnp.zeros((ng, tm), jnp.int32)
    for g2 in range(ng):
        row = gscore[g2:g2 + 1]
        beats = (row > gscore) | ((row == gscore) & (g2 < gid))
        rank = rank + beats.astype(jnp.int32)
    gmask = rank < TOPK_GROUPS

    cand = [jnp.where(gmask, x, -jnp.inf) for x in slabs]
    eid = [gid * gs + j for j in range(gs)]
    chosen = []
    for _ in range(TOP_K):
        best = functools.reduce(jnp.maximum, cand)
        best = jnp.max(best, axis=0, keepdims=True)
        idx = functools.reduce(jnp.minimum, [jnp.where(cand[j] == best, eid[j], N_EXPERTS) for j in range(gs)])
        idx = jnp.min(idx, axis=0, keepdims=True)
        chosen.append(idx)
        cand = [jnp.where(eid[j] == idx, -jnp.inf, cand[j]) for j in range(gs)]

    mask = [functools.reduce(jnp.logical_or, [eid[j] == idx for idx in chosen]) for j in range(gs)]
    maskb = jnp.concatenate(mask, axis=0)
    wsel = jnp.where(maskb, scores, 0.0)
    w_out[...] = wsel / jnp.sum(wsel, axis=0, keepdims=True) * ROUTE_SCALE

    def extents(cnt, lower_sum):
        units = jnp.floor((cnt + (RUN_ALIGN - 1)) * (1.0 / RUN_ALIGN))
        start = RUN_ALIGN * lower_sum(units.astype(BF16))
        return start, start + RUN_ALIGN * units

    lane = lax.broadcasted_iota(jnp.int32, (N_EXPERTS, LANES), 1)
    sub = lax.broadcasted_iota(jnp.int32, (SUBLANES, N_EXPERTS), 0)
    ts = tri_ref.shape[0]
    for c in range(tm // ts):
        mb = maskb[:, c * ts:(c + 1) * ts]
        maskf = jnp.where(mb, 1.0, 0.0)
        mask16 = maskf.astype(BF16)
        before = jnp.dot(mask16, tri_ref[...], preferred_element_type=F32)
        p_out[:, c * ts:(c + 1) * ts] = jnp.where(mb, before, -1.0).astype(p_out.dtype)
        cnt_c = jnp.sum(maskf, axis=1, keepdims=True)
        start_c, end_c = extents(jnp.broadcast_to(cnt_c, (N_EXPERTS, LANES)),
                                 lambda u: jnp.dot(lt_ref[...], u, preferred_element_type=F32))
        col_out[c] = jnp.where(lane == 0, cnt_c, jnp.where(lane == 1, start_c, end_c))
        cnt_r = lax.dot_general(jnp.ones((SUBLANES, ts), BF16), mask16, NT_DIMS, preferred_element_type=F32)
        start_r, end_r = extents(cnt_r,
                                 lambda u: lax.dot_general(u, lt_ref[...], NT_DIMS, preferred_element_type=F32))
        row_out[c] = jnp.where(sub == 0, start_r, end_r)


def _route_call(xm, mod, norm_ffn, wrt, rbias, lower, ts, tiles_per_step):
    bsz, s, d = xm.shape
    t = bsz * s
    tm = ts * tiles_per_step
    nt = s // tm
    tri = (jnp.arange(ts)[:, None] < jnp.arange(ts)[None, :]).astype(BF16)
    tok = pl.BlockSpec((N_EXPERTS, tm), lambda i: (0, i))
    return pl.pallas_call(
        _route_kernel,
        grid=(t // tm,),
        in_specs=[pl.BlockSpec((1, tm, d), lambda i: (i // nt, i % nt, 0)),
                  pl.BlockSpec((1, SUBLANES, d), lambda i: (i // nt, 0, 0)),
                  _full(norm_ffn.shape), _full(wrt.shape), _full(rbias.shape), _full(tri.shape),
                  _full(lower.shape)],
        out_specs=[pl.BlockSpec((1, tm, d), lambda i: (i // nt, i % nt, 0)), tok, tok,
                   pl.BlockSpec((tiles_per_step, N_EXPERTS, LANES), lambda i: (i, 0, 0)),
                   pl.BlockSpec((tiles_per_step, SUBLANES, N_EXPERTS), lambda i: (i, 0, 0))],
        out_shape=[jax.ShapeDtypeStruct((bsz, s, d), BF16), jax.ShapeDtypeStruct((N_EXPERTS, t), F32),
                   jax.ShapeDtypeStruct((N_EXPERTS, t), BF16),
                   jax.ShapeDtypeStruct((t // ts, N_EXPERTS, LANES), F32),
                   jax.ShapeDtypeStruct((t // ts, SUBLANES, N_EXPERTS), F32)],
        compiler_params=_params("arbitrary"),
        name="route",
    )(xm, mod, norm_ffn, wrt, rbias, tri, lower)


def _pack(x):
    w = x.shape[1] // 2
    lo = lax.bitcast_convert_type(x[:, :w].astype(BF16).astype(F32), jnp.uint32)
    hi = lax.bitcast_convert_type(x[:, w:].astype(BF16).astype(F32), jnp.uint32)
    return hi | (lo >> 16)


def _unpack(u):
    lo = lax.bitcast_convert_type(u << 16, F32).astype(BF16)
    hi = lax.bitcast_convert_type(u & jnp.uint32(0xFFFF0000), F32).astype(BF16)
    return lo, hi


def _pow2_pieces(units, limit):
    bit = 1
    while bit * 2 <= limit:
        bit *= 2
    while bit:
        yield (units & bit) != 0, units & ~(2 * bit - 1), bit
        bit //= 2


def _rows_copy(vm_ref, hbm_ref, sem, vm_row, hbm_row, rows, to_hbm):
    v = vm_ref.at[pl.ds(pl.multiple_of(vm_row, RUN_ALIGN), rows), :]
    h = hbm_ref.at[pl.ds(pl.multiple_of(hbm_row, RUN_ALIGN), rows), :]
    return pltpu.make_async_copy(v, h, sem) if to_hbm else pltpu.make_async_copy(h, v, sem)


def _run_copies(vm_ref, hbm_ref, sem, n8, vm_row, hbm_row, limit, to_hbm, act, keep=lambda size: True):
    for on, off, size in _pow2_pieces(n8, limit):
        if keep(size):
            @pl.when(on)
            def _():
                act(_rows_copy(vm_ref, hbm_ref, sem, vm_row + RUN_ALIGN * off, hbm_row + RUN_ALIGN * off,
                               RUN_ALIGN * size, to_hbm))


def _tile_runs(tables, tile, vm_ref, hbm_ref, sem, tm, to_hbm):
    n8_ref, ls_ref, gs_ref, long_ref = tables

    def loop(keep):
        def body(e, c):
            i = tile * N_EXPERTS + e
            _run_copies(vm_ref, hbm_ref, sem, n8_ref[i], ls_ref[i], gs_ref[i], tm // RUN_ALIGN, to_hbm,
                        lambda cp: cp.start(), keep)
            return c
        lax.fori_loop(0, N_EXPERTS, body, 0)

    loop(lambda size: size < LONG_RUN)
    if tm // RUN_ALIGN >= LONG_RUN:
        pl.when(long_ref[tile] != 0)(lambda: loop(lambda size: size >= LONG_RUN))


def _wait_rows(vm_ref, hbm_ref, sem, units, limit, to_hbm):
    for on, _, size in _pow2_pieces(units, limit):
        @pl.when(on)
        def _():
            _rows_copy(vm_ref, hbm_ref, sem, 0, 0, RUN_ALIGN * size, to_hbm).wait()


def _dispatch_kernel(n8_ref, ls_ref, gs_ref, long_ref, ts_ref, t8_ref, nu_ref, pos_ref, ext_ref, h_ref, xs_out, srt2,
                     zbuf, sems):
    step = pl.program_id(0)
    tm = h_ref.shape[0]
    rows = srt2.shape[1]
    slot = step % 2
    srt, sem = srt2.at[slot], sems.at[slot]
    rid = lax.broadcasted_iota(jnp.int32, (rows, 1), 0).astype(F32)
    start = ext_ref[0, 0:1, :]
    member = jnp.where((rid >= start) & (rid < ext_ref[0, 1:2, :]), 1.0, 0.0)
    offset = rid - jnp.sum(member * start, axis=1, keepdims=True)
    pos = jnp.dot(member.astype(BF16), pos_ref[...], preferred_element_type=F32)
    sel = jnp.where(pos == offset, 1.0, 0.0).astype(BF16)
    srt[...] = _pack(jnp.dot(sel, h_ref[...], preferred_element_type=F32))

    _tile_runs((n8_ref, ls_ref, gs_ref, long_ref), step, srt, xs_out, sem, tm, True)

    def wait_tile(tile, s):
        last = tile * N_EXPERTS + N_EXPERTS - 1
        _wait_rows(srt2.at[s], xs_out, sems.at[s], ls_ref[last] // RUN_ALIGN + n8_ref[last], rows // RUN_ALIGN, True)

    pl.when(step > 0)(lambda: wait_tile(step - 1, 1 - slot))

    @pl.when(step == pl.num_programs(0) - 1)
    def _():
        wait_tile(step, slot)
        zbuf[...] = jnp.zeros(zbuf.shape, zbuf.dtype)
        nblk = xs_out.shape[0] // EXPERT_BLOCK

        def fill(act):
            def tails(e, c):
                _run_copies(zbuf, xs_out, sem, t8_ref[e], 0, ts_ref[e], EXPERT_BLOCK // RUN_ALIGN - 1, True, act)
                return c

            def blocks(b, c):
                act(pltpu.make_async_copy(
                    zbuf, xs_out.at[pl.ds(pl.multiple_of(b * EXPERT_BLOCK, EXPERT_BLOCK), EXPERT_BLOCK), :], sem))
                return c

            lax.fori_loop(0, N_EXPERTS, tails, 0)
            lax.fori_loop(nu_ref[0], nblk, blocks, 0)

        fill(lambda cp: cp.start())
        fill(lambda cp: cp.wait())


def _dispatch_call(tables, pos_et, ext_rows, h2, nblk, tm):
    t, d = h2.shape
    lrows = TOP_K * tm + N_EXPERTS * RUN_ALIGN
    return pl.pallas_call(
        _dispatch_kernel,
        grid_spec=pltpu.PrefetchScalarGridSpec(
            num_scalar_prefetch=len(tables), grid=(t // tm,),
            in_specs=[pl.BlockSpec((N_EXPERTS, tm), lambda i, *_: (0, i)),
                      pl.BlockSpec((1,) + ext_rows.shape[1:], lambda i, *_: (i, 0, 0)),
                      pl.BlockSpec((tm, d), lambda i, *_: (i, 0))],
            out_specs=pl.BlockSpec(memory_space=pl.ANY),
            scratch_shapes=[pltpu.VMEM((2, lrows, d // 2), jnp.uint32),
                            pltpu.VMEM((EXPERT_BLOCK, d // 2), jnp.uint32), pltpu.SemaphoreType.DMA((2,))]),
        out_shape=jax.ShapeDtypeStruct((nblk * EXPERT_BLOCK, d // 2), jnp.uint32),
        compiler_params=_params("arbitrary"),
        name="dispatch",
    )(*tables, pos_et, ext_rows, h2)


def _expert_kernel(blk_ref, nused_ref, x_ref, wgu_ref, wd_ref, y_ref):
    used = pl.program_id(0) < nused_ref[0]

    @pl.when(used)
    def _():
        lo, hi = _unpack(x_ref[...])
        half = lo.shape[1]
        gu = (jnp.dot(lo, wgu_ref[0, :half, :], preferred_element_type=F32)
              + jnp.dot(hi, wgu_ref[0, half:, :], preferred_element_type=F32))
        a = _silu(gu[:, :EXPERT_FF]) * gu[:, EXPERT_FF:]
        y_ref[...] = _pack(_dot(a, wd_ref[0]))

    @pl.when(jnp.logical_not(used))
    def _():
        y_ref[...] = jnp.zeros(y_ref.shape, y_ref.dtype)


def _expert_call(blk_e, nused, xs, wgu, wd):
    rows, d = xs.shape
    nblk = rows // EXPERT_BLOCK

    def row_map(i, blk, nu):
        return (jnp.minimum(i, nu[0] - 1), 0)

    return pl.pallas_call(
        _expert_kernel,
        grid_spec=pltpu.PrefetchScalarGridSpec(
            num_scalar_prefetch=2, grid=(nblk,),
            in_specs=[pl.BlockSpec((EXPERT_BLOCK, d), row_map),
                      pl.BlockSpec((1,) + wgu.shape[1:], lambda i, blk, nu: (blk[i], 0, 0)),
                      pl.BlockSpec((1,) + wd.shape[1:], lambda i, blk, nu: (blk[i], 0, 0))],
            out_specs=pl.BlockSpec((EXPERT_BLOCK, d), lambda i, blk, nu: (i, 0))),
        out_shape=jax.ShapeDtypeStruct((rows, d), jnp.uint32),
        compiler_params=_params("arbitrary"),
        name="expert",
    )(blk_e, nused, xs, wgu, wd)


def _combine_kernel(n8_ref, ls_ref, gs_ref, long_ref, ys_hbm, pos_ref, w_ref, ext_ref, xm_ref, h_ref, mod_ref,
                    wsgu_ref, wsd_ref, fn_ref, o_ref, ybuf2, sems):
    step = pl.program_id(0)
    tm = xm_ref.shape[0]
    rows = ybuf2.shape[1]
    slot = step % 2
    ybuf, sem = ybuf2.at[slot], sems.at[slot]

    def fetch(tile, s):
        _tile_runs((n8_ref, ls_ref, gs_ref, long_ref), tile, ybuf2.at[s], ys_hbm, sems.at[s], tm, False)

    pl.when(step == 0)(lambda: fetch(step, slot))
    pl.when(step + 1 < pl.num_programs(0))(lambda: fetch(step + 1, 1 - slot))
    gu = _dot(h_ref[...], wsgu_ref[...])
    ff = gu.shape[1] // 2
    shared = _dot(_silu(gu[:, :ff]) * gu[:, ff:], wsd_ref[...])
    cid = lax.broadcasted_iota(jnp.int32, (1, rows), 1).astype(F32)
    start = ext_ref[0, :, 1:2]
    member = jnp.where((cid >= start) & (cid < ext_ref[0, :, 2:3]), 1.0, 0.0)
    offset = cid - jnp.sum(member * start, axis=0, keepdims=True)
    member = member.astype(BF16)
    pos = jnp.dot(pos_ref[...], member, preferred_element_type=F32)
    mix = jnp.where(pos == offset, jnp.dot(w_ref[...].astype(BF16), member, preferred_element_type=F32), 0.0)
    mix = mix.astype(BF16)
    last = step * N_EXPERTS + N_EXPERTS - 1
    filled = ls_ref[last] + RUN_ALIGN * n8_ref[last]
    _wait_rows(ybuf, ys_hbm, sem, filled // RUN_ALIGN, rows // RUN_ALIGN, False)
    rid = lax.broadcasted_iota(jnp.int32, (rows, 1), 0)
    lo, hi = _unpack(jnp.where(rid < filled, ybuf[...], jnp.uint32(0)))
    routed = jnp.concatenate([jnp.dot(mix, lo, preferred_element_type=F32),
                              jnp.dot(mix, hi, preferred_element_type=F32)], axis=1)
    x = xm_ref[...] + mod_ref[0, 5:6, :] * (routed + shared)
    o_ref[...] = _rms(x, fn_ref[...])


def _combine_call(tables, ys, pos_te, w_te, ext_cols, xm, h2, mod, wsgu, wsd, final_norm, tm, tiles_per_batch):
    t, d = xm.shape
    lrows = TOP_K * tm + N_EXPERTS * RUN_ALIGN
    tok = pl.BlockSpec((tm, d), lambda i, *_: (i, 0))
    per_e = pl.BlockSpec((tm, N_EXPERTS), lambda i, *_: (i, 0))
    return pl.pallas_call(
        _combine_kernel,
        grid_spec=pltpu.PrefetchScalarGridSpec(
            num_scalar_prefetch=len(tables), grid=(t // tm,),
            in_specs=[pl.BlockSpec(memory_space=pl.ANY), per_e, per_e,
                      pl.BlockSpec((1,) + ext_cols.shape[1:], lambda i, *_: (i, 0, 0)), tok, tok,
                      pl.BlockSpec((1, SUBLANES, d), lambda i, *_: (i // tiles_per_batch, 0, 0)),
                      _full(wsgu.shape), _full(wsd.shape), _full(final_norm.shape)],
            out_specs=tok,
            scratch_shapes=[pltpu.VMEM((2, lrows, d // 2), jnp.uint32), pltpu.SemaphoreType.DMA((2,))]),
        out_shape=jax.ShapeDtypeStruct((t, d), F32),
        compiler_params=_params("arbitrary"),
        name="combine",
    )(*tables, ys, pos_te, w_te, ext_cols, xm, h2, mod, wsgu, wsd, final_norm)


def _moe(xm, mod, norm_ffn, w_router, router_bias, wg, wu, wd, wsg, wsu, wsd, final_norm, tm):
    bsz, s, d = xm.shape
    t = bsz * s
    nt = t // tm
    perm = (np.arange(N_EXPERTS) % N_GROUPS) * GROUP_SIZE + np.arange(N_EXPERTS) // N_GROUPS
    wrt = w_router.T[perm]
    rbias = router_bias[perm][:, None]
    lower = jnp.asarray(perm[None, :] < perm[:, None], BF16)
    h2, w_et, pos_et, ext_cols, ext_rows = _route_call(xm, mod, norm_ffn, wrt, rbias, lower, tm, ROUTE_TILES)

    inv = np.argsort(perm)
    n8 = (ext_cols[:, :, 0].astype(jnp.int32)[:, inv] + (RUN_ALIGN - 1)) // RUN_ALIGN
    run = RUN_ALIGN * n8
    ls = jnp.cumsum(run, axis=1) - run
    tot = jnp.sum(run, axis=0)
    padded = (tot + EXPERT_BLOCK - 1) // EXPERT_BLOCK * EXPERT_BLOCK
    pad_end = jnp.cumsum(padded)
    gs = (pad_end - padded)[None, :] + jnp.cumsum(run, axis=0) - run
    nblk = -(-(t * TOP_K + nt * N_EXPERTS * (RUN_ALIGN - 1)) // EXPERT_BLOCK) + N_EXPERTS
    blk_first = jnp.arange(nblk, dtype=jnp.int32)[:, None] * EXPERT_BLOCK
    blk_e = jnp.minimum(jnp.sum((pad_end[None, :] <= blk_first).astype(jnp.int32), axis=1), N_EXPERTS - 1)
    nused = (pad_end[-1:] // EXPERT_BLOCK).astype(jnp.int32)
    tables = [a.reshape(-1).astype(jnp.int32) for a in (n8, ls, gs, jnp.any(n8 >= LONG_RUN, axis=1))]
    tails = [(pad_end - padded + tot).astype(jnp.int32), ((padded - tot) // RUN_ALIGN).astype(jnp.int32), nused]

    h2f = h2.reshape(t, d)
    xs = _dispatch_call(tables + tails, pos_et, ext_rows, h2f, nblk, tm)
    wgu = jnp.concatenate([wg, wu], axis=2).astype(BF16)
    ys = _expert_call(blk_e, nused, xs, wgu, wd.astype(BF16))
    wsgu = jnp.concatenate([wsg, wsu], axis=1).astype(BF16)
    out = _combine_call(tables, ys, pos_et.T, w_et.T, ext_cols, xm.reshape(t, d), h2f, mod, wsgu, wsd.astype(BF16),
                        final_norm, tm, s // tm)
    return out.reshape(bsz, s, d)


def _rope_tables(s):
    rows = s // GRID_W
    row = jnp.broadcast_to(jnp.arange(rows, dtype=F32)[:, None], (rows, GRID_W)).reshape(-1)
    col = jnp.broadcast_to(jnp.arange(GRID_W, dtype=F32)[None, :], (rows, GRID_W)).reshape(-1)
    half = QK_ROPE // 2
    inv_freq = ROPE_THETA ** (-jnp.arange(0, half, 2, dtype=F32) / half)
    ar, ac = row[:, None] * inv_freq, col[:, None] * inv_freq
    ones = jnp.ones((s, QK_NOPE), F32)
    tail = HEAD_PAD - QK_NOPE - QK_ROPE
    cos_t = jnp.concatenate([ones, jnp.cos(ar), jnp.cos(ar), jnp.cos(ac), jnp.cos(ac), jnp.ones((s, tail), F32)], 1)
    sin_t = jnp.concatenate([0 * ones, -jnp.sin(ar), jnp.sin(ar), -jnp.sin(ac), jnp.sin(ac),
                             jnp.zeros((s, tail), F32)], 1)
    return cos_t, sin_t


_Q4 = QK_ROPE // 4
ROPE_SWAP = np.concatenate([np.arange(_Q4, 2 * _Q4), np.arange(0, _Q4), np.arange(3 * _Q4, 4 * _Q4),
                            np.arange(2 * _Q4, 3 * _Q4)])


def _rope_slot(w, swap):
    if swap:
        w = w[..., ROPE_SWAP]
    pad = [(0, 0)] * (w.ndim - 1) + [(QK_NOPE, HEAD_PAD - QK_NOPE - QK_ROPE)]
    return jnp.pad(w, pad)


def kernel(x, c, ctx, c_ctx, w_mod, b_mod, norm_mix, norm_ffn, w_in, b_in, q_norm, w_uq, kv_norm, w_ukv, w_branch_attn, hy_conv_w, hy_conv_b, hy_filt_w1, hy_filt_b1, hy_filt_w2, hy_filt_b2, hy_filt_w3, hy_filt_freq, hy_skip, w_branch_hyena, w_out, w_router, router_bias, w_exp_gate, w_exp_up, w_exp_down, w_sh_gate, w_sh_up, w_sh_down, final_norm,
           tiles=None):
    bsz, s, d = x.shape
    tl = dict(inproj=512, tq=2048, tk=1408, fft_kb=8, merge=512, moe=256)
    tl.update(tiles or {})
    assert w_mod.shape[0] == 1, "single-layer trunk"
    i = 0

    rows = -(-(bsz + 1) // SUBLANES) * SUBLANES
    c_rows = jnp.pad(jnp.concatenate([c, c_ctx[None]], axis=0), ((0, rows - bsz - 1), (0, 0)))
    mod_all = _mod_call(c_rows, w_mod[i], b_mod[i])
    mod_all = jnp.pad(mod_all.reshape(rows, 6, d), ((0, 0), (0, SUBLANES - 6), (0, 0)))
    mod, modc = mod_all[:bsz], mod_all[bsz:bsz + 1]

    cuts = np.cumsum([Q_LORA, KV_LORA, QK_ROPE, 3 * HY_WIDTH])
    wi, bi = w_in[i], b_in[i][None]
    w_q, w_kv, w_pe, w_hy, w_g = jnp.split(wi, cuts, axis=1)
    b_q, b_kv, b_pe, b_hy, b_g = jnp.split(bi, cuts, axis=1)
    wa = jnp.concatenate([w_q, w_kv, _rope_slot(w_pe, False), _rope_slot(w_pe, True)], axis=1).astype(BF16)
    ba = jnp.concatenate([b_q, b_kv, _rope_slot(b_pe, False), _rope_slot(b_pe, True)], axis=1)
    wq3 = w_uq[i].reshape(Q_LORA, N_HEADS, QK_NOPE + QK_ROPE) * (ATTN_SCALE * math.log2(math.e))
    tail = ((0, 0), (0, 0), (0, HEAD_PAD - QK_NOPE))
    wuq = (jnp.pad(wq3[..., :QK_NOPE], tail) + _rope_slot(wq3[..., QK_NOPE:], False)).reshape(Q_LORA, -1).astype(BF16)
    wuqs = _rope_slot(wq3[..., QK_NOPE:], True).reshape(Q_LORA, -1).astype(BF16)
    wkv3 = w_ukv[i].reshape(KV_LORA, N_HEADS, QK_NOPE + V_HEAD)
    wuk = jnp.pad(wkv3[..., :QK_NOPE], tail).reshape(KV_LORA, -1).astype(BF16)
    wuvt = wkv3[..., QK_NOPE:].reshape(KV_LORA, -1).T.astype(BF16)
    nm, qn, kvn = norm_mix[i][None], q_norm[i][None], kv_norm[i][None]

    w_c = jnp.concatenate([w_kv, _rope_slot(w_pe, False)], axis=1).astype(BF16)
    b_c = jnp.concatenate([b_kv, _rope_slot(b_pe, False)], axis=1)
    ck, cvt = _ctx_call(ctx, modc, nm, w_c, b_c, kvn, wuk, wuvt)

    cos_t, sin_t = _rope_tables(s)
    q, k, vt, hv, hx1, hx2, gate = _inproj_call(
        x, mod, nm, wa, ba, w_hy.astype(BF16), b_hy, w_g.astype(BF16), b_g, qn, wuq, wuqs, kvn, wuk, wuvt,
        cos_t, sin_t, hy_conv_w[i], hy_conv_b[i][None], tl["inproj"])

    attn = _attn_call(q, jnp.concatenate([ck, k], axis=2), jnp.concatenate([cvt, vt], axis=3), tl["tq"], tl["tk"])
    hy = _hyena(hv, hx1, hx2, hy_filt_w1[i], hy_filt_b1[i], hy_filt_w2[i], hy_filt_b2[i], hy_filt_w3[i],
                hy_filt_freq[i], hy_skip[i], tl["fft_kb"])
    xm = _merge_call(x, attn, hy, gate, mod, w_branch_attn[i].astype(BF16), w_branch_hyena[i].astype(BF16),
                     w_out[i].astype(BF16), tl["merge"])
    return _moe(xm, mod, norm_ffn[i][None], w_router[i], router_bias[i], w_exp_gate[i], w_exp_up[i], w_exp_down[i],
                w_sh_gate[i], w_sh_up[i], w_sh_down[i], final_norm[None], tl["moe"])
```

```python
import functools
import math

import numpy as np
import jax
import jax.numpy as jnp
from jax import lax
from jax.experimental import pallas as pl
from jax.experimental.pallas import tpu as pltpu

GRID_W = 64
N_HEADS = 8
QK_NOPE = 64
QK_ROPE = 32
V_HEAD = 64
Q_LORA = 256
KV_LORA = 128
ROPE_THETA = 10000.0
ATTN_SCALE = 1.0 / math.sqrt(QK_NOPE + QK_ROPE)
HY_WIDTH = 512
HY_ORDER = 2
HY_SHORT = 3
HY_BANDS = 8
HY_EMB = 1 + 2 * HY_BANDS
HY_EMB_PAD = 32
HY_FAST_DECAY = 0.3
HY_SLOW_DECAY = 1.5
HY_DECAY_TARGET = 1e-2
N_EXPERTS = 64
N_GROUPS = 8
GROUP_SIZE = N_EXPERTS // N_GROUPS
TOPK_GROUPS = 4
TOP_K = 8
EXPERT_FF = 256
ROUTE_SCALE = 2.5
EXPERT_BLOCK = 1024
RUN_ALIGN = 8
ROUTE_TILES = 8
LONG_RUN = 8
NORM_EPS = 1e-6

HEAD_PAD = 128
Q_CHUNK = 512
AHEAD = 2
LANES = 128
SUBLANES = 8
VMEM_LIMIT = 48 * 1024 * 1024

F32 = jnp.float32
BF16 = jnp.bfloat16
NT_DIMS = (((1,), (1,)), ((), ()))
NN_DIMS = (((1,), (0,)), ((), ()))


def _params(*sem):
    return pltpu.CompilerParams(dimension_semantics=sem, vmem_limit_bytes=VMEM_LIMIT)


def _dot(a, b):
    return jnp.dot(a.astype(BF16), b.astype(BF16), preferred_element_type=F32)


def _split(a):
    hi = a.astype(BF16)
    lo = (a - hi.astype(F32)).astype(BF16)
    return hi, lo


def _dot3(a, b, dims=NN_DIMS):
    ah, al = _split(a)
    bh, bl = _split(b)
    d = functools.partial(lax.dot_general, dimension_numbers=dims, preferred_element_type=F32)
    return d(ah, bh) + (d(ah, bl) + d(al, bh))


def _rms(x, g):
    return x * lax.rsqrt(jnp.mean(x * x, axis=-1, keepdims=True) + NORM_EPS) * g


def _silu(x):
    return x * jax.nn.sigmoid(x)


def _full(shape):
    nd = len(shape)
    return pl.BlockSpec(shape, lambda *_: (0,) * nd)


def _mod_kernel(c_ref, w_ref, b_ref, o_ref):
    o_ref[...] = _dot3(_silu(c_ref[...]), w_ref[...]) + b_ref[...]


def _mod_call(c_rows, w_mod, b_mod):
    r, d = c_rows.shape
    n = w_mod.shape[1]
    bn = 1024
    return pl.pallas_call(
        _mod_kernel,
        grid=(n // bn,),
        in_specs=[_full((r, d)), pl.BlockSpec((d, bn), lambda j: (0, j)), pl.BlockSpec((1, bn), lambda j: (0, j))],
        out_specs=pl.BlockSpec((r, bn), lambda j: (0, j)),
        out_shape=jax.ShapeDtypeStruct((r, n), F32),
        compiler_params=_params("arbitrary"),
        name="mod",
    )(c_rows, w_mod, b_mod.reshape(1, n))


def _prenorm(x, mod_ref, row, g):
    shift = mod_ref[0, row:row + 1, :]
    scale = mod_ref[0, row + 1:row + 2, :]
    return _rms(x, g) * (1.0 + scale) + shift


def _kv_heads(kv_lat, kpe, kvn_ref, wuk_ref, wuvt_ref, k_out, vt_out):
    kvn = _rms(kv_lat, kvn_ref[...]).astype(BF16)
    kk = _dot(kvn, wuk_ref[...])
    vt = lax.dot_general(wuvt_ref[...], kvn, NT_DIMS, preferred_element_type=F32)
    ones = jnp.ones((HEAD_PAD - V_HEAD, vt.shape[1]), F32)
    for h in range(N_HEADS):
        k_out[0, h] = (kk[:, HEAD_PAD * h:HEAD_PAD * (h + 1)] + kpe).astype(BF16)
        vt_out[0, h] = jnp.concatenate([vt[V_HEAD * h:V_HEAD * (h + 1)], ones], axis=0).astype(BF16)


def _ctx_kernel(c_ref, mod_ref, nm_ref, w_ref, b_ref, kvn_ref, wuk_ref, wuv_ref, k_out, v_out):
    h = _prenorm(c_ref[0], mod_ref, 0, nm_ref[...]).astype(BF16)
    a = _dot(h, w_ref[...]) + b_ref[...]
    _kv_heads(a[:, :KV_LORA], a[:, KV_LORA:], kvn_ref, wuk_ref, wuv_ref, k_out, v_out)


def _ctx_call(ctx, modc, norm_mix, w_c, b_c, kv_norm, w_uk, w_uv):
    bsz, n, d = ctx.shape
    return pl.pallas_call(
        _ctx_kernel,
        grid=(bsz,),
        in_specs=[pl.BlockSpec((1, n, d), lambda b: (b, 0, 0)), _full(modc.shape), _full(norm_mix.shape),
                  _full(w_c.shape), _full(b_c.shape), _full(kv_norm.shape), _full(w_uk.shape), _full(w_uv.shape)],
        out_specs=[pl.BlockSpec((1, N_HEADS, n, HEAD_PAD), lambda b: (b, 0, 0, 0)),
                   pl.BlockSpec((1, N_HEADS, HEAD_PAD, n), lambda b: (b, 0, 0, 0))],
        out_shape=[jax.ShapeDtypeStruct((bsz, N_HEADS, n, HEAD_PAD), BF16),
                   jax.ShapeDtypeStruct((bsz, N_HEADS, HEAD_PAD, n), BF16)],
        compiler_params=_params("arbitrary"),
        name="ctx",
    )(ctx, modc, norm_mix, w_c, b_c, kv_norm, w_uk, w_uv)


def _inproj_kernel(x_ref, xp_ref, xn_ref, mod_ref, nm_ref, wa_ref, ba_ref, why_ref, bhy_ref, wg_ref, bg_ref,
                   qn_ref, wuq_ref, wuqs_ref, kvn_ref, wuk_ref, wuv_ref, cos_ref, sin_ref, cw_ref, cb_ref,
                   q_out, k_out, v_out, hv_out, hx1_out, hx2_out, g_out):
    i = pl.program_id(0)
    tm = x_ref.shape[1]
    nm = nm_ref[...]
    h = _prenorm(x_ref[0], mod_ref, 0, nm).astype(BF16)
    a = _dot(h, wa_ref[...]) + ba_ref[...]
    q_lat = a[:, :Q_LORA]
    kv_lat = a[:, Q_LORA:Q_LORA + KV_LORA]
    kpe_m = a[:, Q_LORA + KV_LORA:Q_LORA + KV_LORA + HEAD_PAD]
    kpe_s = a[:, Q_LORA + KV_LORA + HEAD_PAD:]
    cos = cos_ref[...]
    sin = sin_ref[...]
    qn = _rms(q_lat, qn_ref[...]).astype(BF16)
    qa = _dot(qn, wuq_ref[...])
    qs = _dot(qn, wuqs_ref[...])
    for hh in range(N_HEADS):
        sl = slice(HEAD_PAD * hh, HEAD_PAD * (hh + 1))
        q_out[0, hh] = (qa[:, sl] * cos + qs[:, sl] * sin).astype(BF16)
    _kv_heads(kv_lat, kpe_m * cos + kpe_s * sin, kvn_ref, wuk_ref, wuv_ref, k_out, v_out)
    g_out[0] = (_dot(h, wg_ref[...]) + bg_ref[...]).astype(BF16)

    why = why_ref[...]
    bhy = bhy_ref[...]
    halo = jnp.concatenate([_prenorm(xp_ref[0], mod_ref, 0, nm), _prenorm(xn_ref[0], mod_ref, 0, nm)], axis=0)
    hy_all = _dot(jnp.concatenate([h, halo.astype(BF16)], axis=0), why) + bhy
    hy = hy_all[:tm]
    prev = jnp.where(i == 0, 0.0, hy_all[tm + SUBLANES - 1:tm + SUBLANES])
    nxt = jnp.where(i == pl.num_programs(0) - 1, 0.0, hy_all[tm + SUBLANES:tm + SUBLANES + 1])
    rid = lax.broadcasted_iota(jnp.int32, (tm, 1), 0)
    up = jnp.where(rid == 0, prev, pltpu.roll(hy, 1, 0))
    dn = jnp.where(rid == tm - 1, nxt, pltpu.roll(hy, tm - 1, 0))
    u = up * cw_ref[0:1, :] + hy * cw_ref[1:2, :] + dn * cw_ref[2:3, :] + cb_ref[...]
    hv_out[0] = u[:, :HY_WIDTH]
    hx1_out[0] = u[:, HY_WIDTH:2 * HY_WIDTH]
    hx2_out[0] = u[:, 2 * HY_WIDTH:]


def _inproj_call(x, mod, norm_mix, wa, ba, why, bhy, wg, bg, q_norm, wuq, wuqs, kv_norm, wuk, wuvt, cos_t, sin_t, cw,
                 cb, tm):
    bsz, s, d = x.shape
    nt = s // tm
    rb = tm // SUBLANES
    last_rb = s // SUBLANES - 1
    consts = [norm_mix, wa, ba, why, bhy, wg, bg, q_norm, wuq, wuqs, kv_norm, wuk, wuvt]
    in_specs = [
        pl.BlockSpec((1, tm, d), lambda i, b: (b, i, 0)),
        pl.BlockSpec((1, SUBLANES, d), lambda i, b: (b, jnp.maximum(i * rb - 1, 0), 0)),
        pl.BlockSpec((1, SUBLANES, d), lambda i, b: (b, jnp.minimum((i + 1) * rb, last_rb), 0)),
        pl.BlockSpec((1, SUBLANES, d), lambda i, b: (b, 0, 0)),
    ] + [_full(c.shape) for c in consts] + [
        pl.BlockSpec((tm, HEAD_PAD), lambda i, b: (i, 0)),
        pl.BlockSpec((tm, HEAD_PAD), lambda i, b: (i, 0)),
        _full(cw.shape), _full(cb.shape),
    ]
    hw = HY_WIDTH
    out_specs = [
        pl.BlockSpec((1, N_HEADS, tm, HEAD_PAD), lambda i, b: (b, 0, i, 0)),
        pl.BlockSpec((1, N_HEADS, tm, HEAD_PAD), lambda i, b: (b, 0, i, 0)),
        pl.BlockSpec((1, N_HEADS, HEAD_PAD, tm), lambda i, b: (b, 0, 0, i)),
        pl.BlockSpec((1, tm, hw), lambda i, b: (b, i, 0)),
        pl.BlockSpec((1, tm, hw), lambda i, b: (b, i, 0)),
        pl.BlockSpec((1, tm, hw), lambda i, b: (b, i, 0)),
        pl.BlockSpec((1, tm, 2 * d), lambda i, b: (b, i, 0)),
    ]
    out_shape = [
        jax.ShapeDtypeStruct((bsz, N_HEADS, s, HEAD_PAD), BF16),
        jax.ShapeDtypeStruct((bsz, N_HEADS, s, HEAD_PAD), BF16),
        jax.ShapeDtypeStruct((bsz, N_HEADS, HEAD_PAD, s), BF16),
        jax.ShapeDtypeStruct((bsz, s, hw), F32),
        jax.ShapeDtypeStruct((bsz, s, hw), F32),
        jax.ShapeDtypeStruct((bsz, s, hw), F32),
        jax.ShapeDtypeStruct((bsz, s, 2 * d), BF16),
    ]
    return pl.pallas_call(
        _inproj_kernel,
        grid=(nt, bsz),
        in_specs=in_specs,
        out_specs=out_specs,
        out_shape=out_shape,
        compiler_params=_params("arbitrary", "arbitrary"),
        name="inproj",
    )(x, x, x, mod, *consts, cos_t, sin_t, cw, cb)


def _attn_kernel(q_ref, k_ref, vt_ref, o_ref, m_sc, acc_sc):
    j = pl.program_id(2)

    @pl.when(j == 0)
    def _():
        m_sc[...] = jnp.full(m_sc.shape, -jnp.inf, F32)
        acc_sc[...] = jnp.zeros(acc_sc.shape, F32)

    tq = q_ref.shape[2]
    qw = min(tq, Q_CHUNK)
    units = [(h, c) for h in range(N_HEADS) for c in range(0, tq, qw)]

    def scores(u):
        h, c = units[u]
        return lax.dot_general(k_ref[0, h], q_ref[0, h, c:c + qw, :], NT_DIMS,
                               preferred_element_type=F32)

    pending = [scores(u) for u in range(AHEAD)]
    for u, (h, c) in enumerate(units):
        if u + AHEAD < len(units):
            pending.append(scores(u + AHEAD))
        st = pending.pop(0)
        m_prev = m_sc[h, :, c:c + qw]
        m_new = jnp.maximum(m_prev, jnp.max(st, axis=0, keepdims=True))
        pt = jnp.exp2(st - m_new).astype(BF16)
        acc_sc[h, :, c:c + qw] = (jnp.exp2(m_prev - m_new) * acc_sc[h, :, c:c + qw]
                                  + jnp.dot(vt_ref[0, h], pt, preferred_element_type=F32))
        m_sc[h, :, c:c + qw] = m_new

    @pl.when(j == pl.num_programs(2) - 1)
    def _():
        ot = jnp.concatenate([acc_sc[h, :V_HEAD] / acc_sc[h, V_HEAD:V_HEAD + 1] for h in range(N_HEADS)], axis=0)
        o_ref[0] = ot.T.astype(o_ref.dtype)


def _attn_call(q, k, vt, tq, tk):
    bsz, nh, s, dh = q.shape
    nk = k.shape[2]
    dv = nh * V_HEAD
    return pl.pallas_call(
        _attn_kernel,
        grid=(bsz, s // tq, nk // tk),
        in_specs=[
            pl.BlockSpec((1, nh, tq, dh), lambda b, i, j: (b, 0, i, 0)),
            pl.BlockSpec((1, nh, tk, dh), lambda b, i, j: (b, 0, j, 0)),
            pl.BlockSpec((1, nh, dh, tk), lambda b, i, j: (b, 0, 0, j)),
        ],
        out_specs=pl.BlockSpec((1, tq, dv), lambda b, i, j: (b, i, 0)),
        out_shape=jax.ShapeDtypeStruct((bsz, s, dv), BF16),
        scratch_shapes=[pltpu.VMEM((nh, 1, tq), F32), pltpu.VMEM((nh, dh, tq), F32)],
        compiler_params=_params("arbitrary", "arbitrary", "arbitrary"),
        name="attn",
    )(q, k, vt)


def _filter_kernel(emb_ref, w1_ref, b1_ref, w2_ref, b2_ref, w3_ref, fr_ref, dl_ref, full_out, asum_out, *, seq):
    r = pl.program_id(0)
    rb = emb_ref.shape[0]
    emb = emb_ref[...]
    fr = fr_ref[...]
    h = jnp.sin(fr * (_dot3(emb, w1_ref[...]) + b1_ref[...]))
    h = jnp.sin(fr * (_dot3(h, w2_ref[...]) + b2_ref[...]))
    k = _dot3(h, w3_ref[0]) * jnp.exp(-emb[:, 0:1] * dl_ref[...])
    row = r * rb + lax.broadcasted_iota(jnp.int32, (rb, 1), 0)
    k = jnp.where(row == seq, 0.0, k)
    full_out[...] = k

    @pl.when(r == 0)
    def _():
        asum_out[...] = jnp.zeros(asum_out.shape, F32)

    asum_out[...] += jnp.sum(jnp.abs(k), axis=0, keepdims=True)


def _filter_call(emb, w1, b1, w2, b2, w3sel, freq, deltas2, seq, rb):
    n2 = emb.shape[0]
    half_blocks = seq // rb
    width = w3sel.shape[2]
    return pl.pallas_call(
        functools.partial(_filter_kernel, seq=seq),
        grid=(n2 // rb,),
        in_specs=[pl.BlockSpec((rb, HY_EMB_PAD), lambda r: (r, 0)), _full(w1.shape), _full(b1.shape),
                  _full(w2.shape), _full(b2.shape),
                  pl.BlockSpec((1,) + w3sel.shape[1:], lambda r: (r // half_blocks, 0, 0)),
                  _full(freq.shape), _full(deltas2.shape)],
        out_specs=[pl.BlockSpec((rb, width), lambda r: (r, 0)), pl.BlockSpec((1, width), lambda r: (0, 0))],
        out_shape=[jax.ShapeDtypeStruct((n2, width), F32), jax.ShapeDtypeStruct((1, width), F32)],
        compiler_params=_params("arbitrary"),
        name="filt",
    )(emb, w1, b1, w2, b2, w3sel, freq, deltas2)


def _fa_kernel(u_ref, f_ref, a_out):
    two, _, hn, g, c = u_ref.shape
    a = _dot(f_ref[...], u_ref[...].reshape(two * hn * g, c))
    a_out[...] = (_pack(a) if a_out.dtype == jnp.uint32 else a).reshape(a_out.shape)


def _fa_call(u5, fmat, packed):
    _, p, hn, n, c = u5.shape
    g = SUBLANES
    co, dt = (c // 2, jnp.uint32) if packed else (c, F32)
    return pl.pallas_call(
        _fa_kernel,
        grid=(p, n // g),
        in_specs=[pl.BlockSpec((2, 1, hn, g, c), lambda q, j: (0, q, 0, j, 0)), _full(fmat.shape)],
        out_specs=pl.BlockSpec((1, 2, n, g, co), lambda q, j: (q, 0, 0, j, 0)),
        out_shape=jax.ShapeDtypeStruct((p, 2, n, n, co), dt),
        compiler_params=_params("arbitrary", "arbitrary"),
        name="fa",
    )(u5, fmat)


def _dot_packed(w, u):
    lo, hi = _unpack(u)
    return jnp.concatenate([jnp.dot(w, lo, preferred_element_type=F32), jnp.dot(w, hi, preferred_element_type=F32)],
                           axis=1)


def _fb_kernel(a_ref, g_ref, asum_ref, kf_out):
    _, two, kb, n, c = a_ref.shape
    scale = 1.0 / (asum_ref[...] + 1e-6)
    for kk in range(kb):
        x = _dot(g_ref[kk], a_ref[0, :, kk].reshape(two * n, c)) * scale
        kf_out[kk] = x.reshape(two, n, c)


def _fb_call(a5, gmat, asum, kb):
    _, _, n, _, c = a5.shape
    return pl.pallas_call(
        _fb_kernel,
        grid=(n // kb,),
        in_specs=[pl.BlockSpec((1, 2, kb, n, c), lambda k: (0, 0, k, 0, 0)),
                  pl.BlockSpec((kb, 2 * n, 2 * n), lambda k: (k, 0, 0)), _full(asum.shape)],
        out_specs=pl.BlockSpec((kb, 2, n, c), lambda k: (k, 0, 0, 0)),
        out_shape=jax.ShapeDtypeStruct((n, 2, n, c), F32),
        compiler_params=_params("arbitrary"),
        name="fb",
    )(a5, gmat, asum)


def _mid_kernel(a_ref, g_ref, h_ref, kf_ref, b_out):
    _, two, kb, n, c = a_ref.shape
    for kk in range(kb):
        x = _dot_packed(g_ref[kk], a_ref[0, :, kk].reshape(two * n, c))
        xr, xi = x[:n], x[n:]
        kr, ki = kf_ref[kk, 0], kf_ref[kk, 1]
        y = jnp.concatenate([xr * kr - xi * ki, xr * ki + xi * kr], axis=0)
        b_out[0, :, kk] = _pack(_dot(h_ref[kk], y)).reshape(two, n, c)


def _mid_call(a5, gmat, hmat, kf, order, kb):
    p, _, n, _, c = a5.shape
    return pl.pallas_call(
        _mid_kernel,
        grid=(n // kb, p),
        in_specs=[pl.BlockSpec((1, 2, kb, n, c), lambda k, q: (q, 0, k, 0, 0)),
                  pl.BlockSpec((kb, 2 * n, 2 * n), lambda k, q: (k, 0, 0)),
                  pl.BlockSpec((kb, 2 * n, 2 * n), lambda k, q: (k, 0, 0)),
                  pl.BlockSpec((kb, 2, n, 2 * c), lambda k, q: (k, 0, 0, order))],
        out_specs=pl.BlockSpec((1, 2, kb, n, c), lambda k, q: (q, 0, k, 0, 0)),
        out_shape=jax.ShapeDtypeStruct(a5.shape, jnp.uint32),
        compiler_params=_params("arbitrary", "arbitrary"),
        name="mid",
    )(a5, gmat, hmat, kf)


def _fc_kernel(b_ref, f_ref, u_ref, m_ref, skip_ref, o_out):
    _, two, n, g, c = b_ref.shape
    y = _dot_packed(f_ref[...], b_ref[...].reshape(two * n * g, c)).reshape(u_ref.shape)
    o_out[...] = m_ref[...] * (y + u_ref[...] * skip_ref[...])


def _fc_call(b5, finv, u5, m5, skip_row):
    _, p, hn, n, c = u5.shape
    g = SUBLANES
    blk = pl.BlockSpec((2, 1, hn, g, c), lambda q, j: (0, q, 0, j, 0))
    return pl.pallas_call(
        _fc_kernel,
        grid=(p, n // g),
        in_specs=[pl.BlockSpec((1, 2, n, g, c // 2), lambda q, j: (q, 0, 0, j, 0)), _full(finv.shape), blk, blk,
                  _full(skip_row.shape)],
        out_specs=blk,
        out_shape=jax.ShapeDtypeStruct(u5.shape, F32),
        compiler_params=_params("arbitrary", "arbitrary"),
        name="fc",
    )(b5, finv, u5, m5, skip_row)


def _dft_tables(n):
    hn = n // 2
    k = np.arange(n)[:, None]
    ang = -2.0 * np.pi * (k * np.arange(n)[None, :] % n) / n
    fr, fi = np.cos(ang), np.sin(ang)
    f_data = np.block([[fr[:, :hn], -fi[:, :hn]], [fi[:, :hn], fr[:, :hn]]])
    f_filt = np.concatenate([fr, fi], axis=0)
    er, ei = fr[:hn], -fi[:hn]
    f_inv = np.block([[er, -ei], [ei, er]]) / float(n * n)
    k1 = jnp.arange(n, dtype=jnp.int32)[:, None, None]
    k2 = jnp.arange(n, dtype=jnp.int32)[None, :, None]
    m2 = jnp.arange(n, dtype=jnp.int32)[None, None, :]
    idx = (m2 * (k1 + n * k2)) % (n * n)
    ang2 = idx.astype(F32) * (-2.0 * math.pi / (n * n))
    gr, gi = jnp.cos(ang2), jnp.sin(ang2)
    g = jnp.concatenate([jnp.concatenate([gr, -gi], axis=2), jnp.concatenate([gi, gr], axis=2)], axis=1)
    h = jnp.swapaxes(g, 1, 2)

    def widen(f):
        return jnp.asarray(np.kron(f, np.eye(SUBLANES)), BF16)

    return widen(f_data), widen(f_filt), widen(f_inv), g.astype(BF16), h.astype(BF16)


def _hyena_filter_tables(seq):
    t = jnp.linspace(0.0, 1.0, seq, dtype=F32)[:, None]
    w = 2.0 * math.pi * jnp.arange(seq, dtype=F32)[:, None] / seq
    f = jnp.linspace(1e-4, HY_BANDS - 1, HY_BANDS, dtype=F32)[None, :]
    emb = jnp.concatenate([t, jnp.cos(f * w), -jnp.sin(f * w)], axis=-1)
    emb = jnp.concatenate([emb, emb[:1], emb[:0:-1]], axis=0)
    emb = jnp.pad(emb, ((0, 0), (0, HY_EMB_PAD - HY_EMB)))
    deltas = jnp.abs(jnp.linspace(math.log(HY_DECAY_TARGET) / HY_SLOW_DECAY,
                                  math.log(HY_DECAY_TARGET) / HY_FAST_DECAY, HY_WIDTH, dtype=F32))
    return emb, jnp.tile(deltas, HY_ORDER)[None, :]


def _hyena(hv, hx1, hx2, w1, b1, w2, b2, w3, freq, skip, kb):
    bsz, seq, c = hv.shape
    n = int(round(math.sqrt(2 * seq)))
    assert n * n == 2 * seq and bsz % 2 == 0
    hn, p = n // 2, bsz // 2
    f_data, f_filt, f_inv, gmat, hmat = _dft_tables(n)

    emb, deltas2 = _hyena_filter_tables(seq)
    w1p = jnp.pad(w1, ((0, HY_EMB_PAD - HY_EMB), (0, 0)))
    w3r = w3.reshape(w3.shape[0], HY_ORDER, 2, c)
    w3sel = jnp.stack([w3r[:, :, 0, :].reshape(-1, HY_ORDER * c), w3r[:, :, 1, :].reshape(-1, HY_ORDER * c)])
    full, asum = _filter_call(emb, w1p, b1[None], w2, b2[None], w3sel, freq[None], deltas2, seq, min(512, seq))
    c2 = HY_ORDER * c
    kf = _fb_call(_fa_call(full.reshape(2, 1, hn, n, c2), f_filt, False), gmat, asum, kb // HY_ORDER)

    def view(t):
        return t.reshape(2, p, hn, n, c)

    def long_conv(u5, m5, order):
        bm = _mid_call(_fa_call(u5, f_data, True), gmat, hmat, kf, order, kb)
        return _fc_call(bm, f_inv, u5, m5, skip[order][None, :])

    z = long_conv(view(hv), view(hx1), 0)
    return long_conv(z, view(hx2), 1).reshape(bsz, seq, c)


def _merge_kernel(x_ref, at_ref, hy_ref, g_ref, mod_ref, wba_ref, wbh_ref, wo_ref, o_ref):
    d = x_ref.shape[2]
    g = g_ref[0].astype(F32)
    y = (jax.nn.sigmoid(g[:, :d]) * _dot(at_ref[0], wba_ref[...])
         + jax.nn.sigmoid(g[:, d:]) * _dot(hy_ref[0], wbh_ref[...]))
    o_ref[0] = x_ref[0] + mod_ref[0, 2:3, :] * _dot(y, wo_ref[...])


def _merge_call(x, attn, hy, gate, mod, wba, wbh, wo, tm):
    bsz, s, d = x.shape

    def tok(w):
        return pl.BlockSpec((1, tm, w), lambda b, i: (b, i, 0))

    return pl.pallas_call(
        _merge_kernel,
        grid=(bsz, s // tm),
        in_specs=[tok(d), tok(attn.shape[2]), tok(hy.shape[2]), tok(2 * d),
                  pl.BlockSpec((1, SUBLANES, d), lambda b, i: (b, 0, 0)),
                  _full(wba.shape), _full(wbh.shape), _full(wo.shape)],
        out_specs=tok(d),
        out_shape=jax.ShapeDtypeStruct((bsz, s, d), F32),
        compiler_params=_params("arbitrary", "arbitrary"),
        name="merge",
    )(x, attn, hy, gate, mod, wba, wbh, wo)


def _route_kernel(xm_ref, mod_ref, nf_ref, wrt_ref, rb_ref, tri_ref, lt_ref, h2_out, w_out, p_out, col_out, row_out):
    tm = xm_ref.shape[1]
    ng, gs = N_GROUPS, GROUP_SIZE

    h2 = _prenorm(xm_ref[0], mod_ref, 3, nf_ref[...])
    h2_out[0] = h2.astype(h2_out.dtype)
    scores = jax.nn.sigmoid(_dot3(wrt_ref[...], h2, NT_DIMS))
    sel = scores + rb_ref[...]
    slabs = [sel[ng * j:ng * (j + 1)] for j in range(gs)]

    top1 = jnp.full((ng, tm), -jnp.inf, F32)
    top2 = top1
    for x in slabs:
        top2 = jnp.maximum(top2, jnp.minimum(top1, x))
        top1 = jnp.maximum(top1, x)
    gscore = top1 + top2
    gid = lax.broadcasted_iota(jnp.int32, (ng, 1), 0)
    rank = jnp.zeros((ng, tm), jnp.int32)
    for g2 in range(ng):
        row = gscore[g2:g2 + 1]
        beats = (row > gscore) | ((row == gscore) & (g2 < gid))
        rank = rank + beats.astype(jnp.int32)
    gmask = rank < TOPK_GROUPS

    cand = [jnp.where(gmask, x, -jnp.inf) for x in slabs]
    eid = [gid * gs + j for j in range(gs)]
    chosen = []
    for _ in range(TOP_K):
        best = functools.reduce(jnp.maximum, cand)
        best = jnp.max(best, axis=0, keepdims=True)
        idx = functools.reduce(jnp.minimum, [jnp.where(cand[j] == best, eid[j], N_EXPERTS) for j in range(gs)])
        idx = jnp.min(idx, axis=0, keepdims=True)
        chosen.append(idx)
        cand = [jnp.where(eid[j] == idx, -jnp.inf, cand[j]) for j in range(gs)]

    mask = [functools.reduce(jnp.logical_or, [eid[j] == idx for idx in chosen]) for j in range(gs)]
    maskb = jnp.concatenate(mask, axis=0)
    wsel = jnp.where(maskb, scores, 0.0)
    w_out[...] = wsel / jnp.sum(wsel, axis=0, keepdims=True) * ROUTE_SCALE

    def extents(cnt, lower_sum):
        units = jnp.floor((cnt + (RUN_ALIGN - 1)) * (1.0 / RUN_ALIGN))
        start = RUN_ALIGN * lower_sum(units.astype(BF16))
        return start, start + RUN_ALIGN * units

    lane = lax.broadcasted_iota(jnp.int32, (N_EXPERTS, LANES), 1)
    sub = lax.broadcasted_iota(jnp.int32, (SUBLANES, N_EXPERTS), 0)
    ts = tri_ref.shape[0]
    for c in range(tm // ts):
        mb = maskb[:, c * ts:(c + 1) * ts]
        maskf = jnp.where(mb, 1.0, 0.0)
        mask16 = maskf.astype(BF16)
        before = jnp.dot(mask16, tri_ref[...], preferred_element_type=F32)
        p_out[:, c * ts:(c + 1) * ts] = jnp.where(mb, before, -1.0).astype(p_out.dtype)
        cnt_c = jnp.sum(maskf, axis=1, keepdims=True)
        start_c, end_c = extents(jnp.broadcast_to(cnt_c, (N_EXPERTS, LANES)),
                                 lambda u: jnp.dot(lt_ref[...], u, preferred_element_type=F32))
        col_out[c] = jnp.where(lane == 0, cnt_c, jnp.where(lane == 1, start_c, end_c))
        cnt_r = lax.dot_general(jnp.ones((SUBLANES, ts), BF16), mask16, NT_DIMS, preferred_element_type=F32)
        start_r, end_r = extents(cnt_r,
                                 lambda u: lax.dot_general(u, lt_ref[...], NT_DIMS, preferred_element_type=F32))
        row_out[c] = jnp.where(sub == 0, start_r, end_r)


def _route_call(xm, mod, norm_ffn, wrt, rbias, lower, ts, tiles_per_step):
    bsz, s, d = xm.shape
    t = bsz * s
    tm = ts * tiles_per_step
    nt = s // tm
    tri = (jnp.arange(ts)[:, None] < jnp.arange(ts)[None, :]).astype(BF16)
    tok = pl.BlockSpec((N_EXPERTS, tm), lambda i: (0, i))
    return pl.pallas_call(
        _route_kernel,
        grid=(t // tm,),
        in_specs=[pl.BlockSpec((1, tm, d), lambda i: (i // nt, i % nt, 0)),
                  pl.BlockSpec((1, SUBLANES, d), lambda i: (i // nt, 0, 0)),
                  _full(norm_ffn.shape), _full(wrt.shape), _full(rbias.shape), _full(tri.shape),
                  _full(lower.shape)],
        out_specs=[pl.BlockSpec((1, tm, d), lambda i: (i // nt, i % nt, 0)), tok, tok,
                   pl.BlockSpec((tiles_per_step, N_EXPERTS, LANES), lambda i: (i, 0, 0)),
                   pl.BlockSpec((tiles_per_step, SUBLANES, N_EXPERTS), lambda i: (i, 0, 0))],
        out_shape=[jax.ShapeDtypeStruct((bsz, s, d), BF16), jax.ShapeDtypeStruct((N_EXPERTS, t), F32),
                   jax.ShapeDtypeStruct((N_EXPERTS, t), BF16),
                   jax.ShapeDtypeStruct((t // ts, N_EXPERTS, LANES), F32),
                   jax.ShapeDtypeStruct((t // ts, SUBLANES, N_EXPERTS), F32)],
        compiler_params=_params("arbitrary"),
        name="route",
    )(xm, mod, norm_ffn, wrt, rbias, tri, lower)


def _pack(x):
    w = x.shape[1] // 2
    lo = lax.bitcast_convert_type(x[:, :w].astype(BF16).astype(F32), jnp.uint32)
    hi = lax.bitcast_convert_type(x[:, w:].astype(BF16).astype(F32), jnp.uint32)
    return hi | (lo >> 16)


def _unpack(u):
    lo = lax.bitcast_convert_type(u << 16, F32).astype(BF16)
    hi = lax.bitcast_convert_type(u & jnp.uint32(0xFFFF0000), F32).astype(BF16)
    return lo, hi


def _pow2_pieces(units, limit):
    bit = 1
    while bit * 2 <= limit:
        bit *= 2
    while bit:
        yield (units & bit) != 0, units & ~(2 * bit - 1), bit
        bit //= 2


def _rows_copy(vm_ref, hbm_ref, sem, vm_row, hbm_row, rows, to_hbm):
    v = vm_ref.at[pl.ds(pl.multiple_of(vm_row, RUN_ALIGN), rows), :]
    h = hbm_ref.at[pl.ds(pl.multiple_of(hbm_row, RUN_ALIGN), rows), :]
    return pltpu.make_async_copy(v, h, sem) if to_hbm else pltpu.make_async_copy(h, v, sem)


def _run_copies(vm_ref, hbm_ref, sem, n8, vm_row, hbm_row, limit, to_hbm, act):
    def emit(pieces):
        for on, off, size in pieces:
            @pl.when(on)
            def _():
                act(_rows_copy(vm_ref, hbm_ref, sem, vm_row + RUN_ALIGN * off, hbm_row + RUN_ALIGN * off,
                               RUN_ALIGN * size, to_hbm))

    pieces = list(_pow2_pieces(n8, limit))
    long_pieces = [p for p in pieces if p[2] >= LONG_RUN]
    if long_pieces:
        pl.when(n8 >= LONG_RUN)(lambda: emit(long_pieces))
    emit([p for p in pieces if p[2] < LONG_RUN])


def _wait_rows(vm_ref, hbm_ref, sem, units, limit, to_hbm):
    for on, _, size in _pow2_pieces(units, limit):
        @pl.when(on)
        def _():
            _rows_copy(vm_ref, hbm_ref, sem, 0, 0, RUN_ALIGN * size, to_hbm).wait()


def _dispatch_kernel(n8_ref, ls_ref, gs_ref, ts_ref, t8_ref, nu_ref, pos_ref, ext_ref, h_ref, xs_out, srt2, zbuf,
                     sems):
    step = pl.program_id(0)
    tm = h_ref.shape[0]
    rows = srt2.shape[1]
    slot = step % 2
    srt, sem = srt2.at[slot], sems.at[slot]
    rid = lax.broadcasted_iota(jnp.int32, (rows, 1), 0).astype(F32)
    start = ext_ref[0, 0:1, :]
    member = jnp.where((rid >= start) & (rid < ext_ref[0, 1:2, :]), 1.0, 0.0)
    offset = rid - jnp.sum(member * start, axis=1, keepdims=True)
    pos = jnp.dot(member.astype(BF16), pos_ref[...], preferred_element_type=F32)
    sel = jnp.where(pos == offset, 1.0, 0.0).astype(BF16)
    srt[...] = _pack(jnp.dot(sel, h_ref[...], preferred_element_type=F32))

    def send(e, c):
        i = step * N_EXPERTS + e
        _run_copies(srt, xs_out, sem, n8_ref[i], ls_ref[i], gs_ref[i], tm // RUN_ALIGN, True, lambda cp: cp.start())
        return c

    lax.fori_loop(0, N_EXPERTS, send, 0)

    def wait_tile(tile, s):
        last = tile * N_EXPERTS + N_EXPERTS - 1
        _wait_rows(srt2.at[s], xs_out, sems.at[s], ls_ref[last] // RUN_ALIGN + n8_ref[last], rows // RUN_ALIGN, True)

    pl.when(step > 0)(lambda: wait_tile(step - 1, 1 - slot))

    @pl.when(step == pl.num_programs(0) - 1)
    def _():
        wait_tile(step, slot)
        zbuf[...] = jnp.zeros(zbuf.shape, zbuf.dtype)
        nblk = xs_out.shape[0] // EXPERT_BLOCK

        def fill(act):
            def tails(e, c):
                _run_copies(zbuf, xs_out, sem, t8_ref[e], 0, ts_ref[e], EXPERT_BLOCK // RUN_ALIGN - 1, True, act)
                return c

            def blocks(b, c):
                act(pltpu.make_async_copy(
                    zbuf, xs_out.at[pl.ds(pl.multiple_of(b * EXPERT_BLOCK, EXPERT_BLOCK), EXPERT_BLOCK), :], sem))
                return c

            lax.fori_loop(0, N_EXPERTS, tails, 0)
            lax.fori_loop(nu_ref[0], nblk, blocks, 0)

        fill(lambda cp: cp.start())
        fill(lambda cp: cp.wait())


def _dispatch_call(tables, pos_et, ext_rows, h2, nblk, tm):
    t, d = h2.shape
    lrows = TOP_K * tm + N_EXPERTS * RUN_ALIGN
    return pl.pallas_call(
        _dispatch_kernel,
        grid_spec=pltpu.PrefetchScalarGridSpec(
            num_scalar_prefetch=len(tables), grid=(t // tm,),
            in_specs=[pl.BlockSpec((N_EXPERTS, tm), lambda i, *_: (0, i)),
                      pl.BlockSpec((1,) + ext_rows.shape[1:], lambda i, *_: (i, 0, 0)),
                      pl.BlockSpec((tm, d), lambda i, *_: (i, 0))],
            out_specs=pl.BlockSpec(memory_space=pl.ANY),
            scratch_shapes=[pltpu.VMEM((2, lrows, d // 2), jnp.uint32),
                            pltpu.VMEM((EXPERT_BLOCK, d // 2), jnp.uint32), pltpu.SemaphoreType.DMA((2,))]),
        out_shape=jax.ShapeDtypeStruct((nblk * EXPERT_BLOCK, d // 2), jnp.uint32),
        compiler_params=_params("arbitrary"),
        name="dispatch",
    )(*tables, pos_et, ext_rows, h2)


def _expert_kernel(blk_ref, nused_ref, x_ref, wgu_ref, wd_ref, y_ref):
    used = pl.program_id(0) < nused_ref[0]

    @pl.when(used)
    def _():
        lo, hi = _unpack(x_ref[...])
        half = lo.shape[1]
        gu = (jnp.dot(lo, wgu_ref[0, :half, :], preferred_element_type=F32)
              + jnp.dot(hi, wgu_ref[0, half:, :], preferred_element_type=F32))
        a = _silu(gu[:, :EXPERT_FF]) * gu[:, EXPERT_FF:]
        y_ref[...] = _pack(_dot(a, wd_ref[0]))

    @pl.when(jnp.logical_not(used))
    def _():
        y_ref[...] = jnp.zeros(y_ref.shape, y_ref.dtype)


def _expert_call(blk_e, nused, xs, wgu, wd):
    rows, d = xs.shape
    nblk = rows // EXPERT_BLOCK

    def row_map(i, blk, nu):
        return (jnp.minimum(i, nu[0] - 1), 0)

    return pl.pallas_call(
        _expert_kernel,
        grid_spec=pltpu.PrefetchScalarGridSpec(
            num_scalar_prefetch=2, grid=(nblk,),
            in_specs=[pl.BlockSpec((EXPERT_BLOCK, d), row_map),
                      pl.BlockSpec((1,) + wgu.shape[1:], lambda i, blk, nu: (blk[i], 0, 0)),
                      pl.BlockSpec((1,) + wd.shape[1:], lambda i, blk, nu: (blk[i], 0, 0))],
            out_specs=pl.BlockSpec((EXPERT_BLOCK, d), lambda i, blk, nu: (i, 0))),
        out_shape=jax.ShapeDtypeStruct((rows, d), jnp.uint32),
        compiler_params=_params("arbitrary"),
        name="expert",
    )(blk_e, nused, xs, wgu, wd)


def _combine_kernel(n8_ref, ls_ref, gs_ref, ys_hbm, pos_ref, w_ref, ext_ref, xm_ref, h_ref, mod_ref, wsgu_ref,
                    wsd_ref, fn_ref, o_ref, ybuf2, sems):
    step = pl.program_id(0)
    tm = xm_ref.shape[0]
    rows = ybuf2.shape[1]
    slot = step % 2
    ybuf, sem = ybuf2.at[slot], sems.at[slot]

    def fetch(tile, s):
        def body(e, c):
            i = tile * N_EXPERTS + e
            _run_copies(ybuf2.at[s], ys_hbm, sems.at[s], n8_ref[i], ls_ref[i], gs_ref[i], tm // RUN_ALIGN, False,
                        lambda cp: cp.start())
            return c
        lax.fori_loop(0, N_EXPERTS, body, 0)

    pl.when(step == 0)(lambda: fetch(step, slot))
    pl.when(step + 1 < pl.num_programs(0))(lambda: fetch(step + 1, 1 - slot))
    gu = _dot(h_ref[...], wsgu_ref[...])
    ff = gu.shape[1] // 2
    shared = _dot(_silu(gu[:, :ff]) * gu[:, ff:], wsd_ref[...])
    cid = lax.broadcasted_iota(jnp.int32, (1, rows), 1).astype(F32)
    start = ext_ref[0, :, 1:2]
    member = jnp.where((cid >= start) & (cid < ext_ref[0, :, 2:3]), 1.0, 0.0)
    offset = cid - jnp.sum(member * start, axis=0, keepdims=True)
    member = member.astype(BF16)
    pos = jnp.dot(pos_ref[...], member, preferred_element_type=F32)
    mix = jnp.where(pos == offset, jnp.dot(w_ref[...].astype(BF16), member, preferred_element_type=F32), 0.0)
    mix = mix.astype(BF16)
    last = step * N_EXPERTS + N_EXPERTS - 1
    filled = ls_ref[last] + RUN_ALIGN * n8_ref[last]
    _wait_rows(ybuf, ys_hbm, sem, filled // RUN_ALIGN, rows // RUN_ALIGN, False)
    rid = lax.broadcasted_iota(jnp.int32, (rows, 1), 0)
    lo, hi = _unpack(jnp.where(rid < filled, ybuf[...], jnp.uint32(0)))
    routed = jnp.concatenate([jnp.dot(mix, lo, preferred_element_type=F32),
                              jnp.dot(mix, hi, preferred_element_type=F32)], axis=1)
    x = xm_ref[...] + mod_ref[0, 5:6, :] * (routed + shared)
    o_ref[...] = _rms(x, fn_ref[...])


def _combine_call(tables, ys, pos_te, w_te, ext_cols, xm, h2, mod, wsgu, wsd, final_norm, tm, tiles_per_batch):
    t, d = xm.shape
    lrows = TOP_K * tm + N_EXPERTS * RUN_ALIGN
    tok = pl.BlockSpec((tm, d), lambda i, *_: (i, 0))
    per_e = pl.BlockSpec((tm, N_EXPERTS), lambda i, *_: (i, 0))
    return pl.pallas_call(
        _combine_kernel,
        grid_spec=pltpu.PrefetchScalarGridSpec(
            num_scalar_prefetch=len(tables), grid=(t // tm,),
            in_specs=[pl.BlockSpec(memory_space=pl.ANY), per_e, per_e,
                      pl.BlockSpec((1,) + ext_cols.shape[1:], lambda i, *_: (i, 0, 0)), tok, tok,
                      pl.BlockSpec((1, SUBLANES, d), lambda i, *_: (i // tiles_per_batch, 0, 0)),
                      _full(wsgu.shape), _full(wsd.shape), _full(final_norm.shape)],
            out_specs=tok,
            scratch_shapes=[pltpu.VMEM((2, lrows, d // 2), jnp.uint32), pltpu.SemaphoreType.DMA((2,))]),
        out_shape=jax.ShapeDtypeStruct((t, d), F32),
        compiler_params=_params("arbitrary"),
        name="combine",
    )(*tables, ys, pos_te, w_te, ext_cols, xm, h2, mod, wsgu, wsd, final_norm)


def _moe(xm, mod, norm_ffn, w_router, router_bias, wg, wu, wd, wsg, wsu, wsd, final_norm, tm):
    bsz, s, d = xm.shape
    t = bsz * s
    nt = t // tm
    perm = (np.arange(N_EXPERTS) % N_GROUPS) * GROUP_SIZE + np.arange(N_EXPERTS) // N_GROUPS
    wrt = w_router.T[perm]
    rbias = router_bias[perm][:, None]
    lower = jnp.asarray(perm[None, :] < perm[:, None], BF16)
    h2, w_et, pos_et, ext_cols, ext_rows = _route_call(xm, mod, norm_ffn, wrt, rbias, lower, tm, ROUTE_TILES)

    inv = np.argsort(perm)
    n8 = (ext_cols[:, :, 0].astype(jnp.int32)[:, inv] + (RUN_ALIGN - 1)) // RUN_ALIGN
    run = RUN_ALIGN * n8
    ls = jnp.cumsum(run, axis=1) - run
    tot = jnp.sum(run, axis=0)
    padded = (tot + EXPERT_BLOCK - 1) // EXPERT_BLOCK * EXPERT_BLOCK
    pad_end = jnp.cumsum(padded)
    gs = (pad_end - padded)[None, :] + jnp.cumsum(run, axis=0) - run
    nblk = -(-(t * TOP_K + nt * N_EXPERTS * (RUN_ALIGN - 1)) // EXPERT_BLOCK) + N_EXPERTS
    blk_first = jnp.arange(nblk, dtype=jnp.int32)[:, None] * EXPERT_BLOCK
    blk_e = jnp.minimum(jnp.sum((pad_end[None, :] <= blk_first).astype(jnp.int32), axis=1), N_EXPERTS - 1)
    nused = (pad_end[-1:] // EXPERT_BLOCK).astype(jnp.int32)
    tables = [a.reshape(-1).astype(jnp.int32) for a in (n8, ls, gs)]
    tails = [(pad_end - padded + tot).astype(jnp.int32), ((padded - tot) // RUN_ALIGN).astype(jnp.int32), nused]

    h2f = h2.reshape(t, d)
    xs = _dispatch_call(tables + tails, pos_et, ext_rows, h2f, nblk, tm)
    wgu = jnp.concatenate([wg, wu], axis=2).astype(BF16)
    ys = _expert_call(blk_e, nused, xs, wgu, wd.astype(BF16))
    wsgu = jnp.concatenate([wsg, wsu], axis=1).astype(BF16)
    out = _combine_call(tables, ys, pos_et.T, w_et.T, ext_cols, xm.reshape(t, d), h2f, mod, wsgu, wsd.astype(BF16),
                        final_norm, tm, s // tm)
    return out.reshape(bsz, s, d)


def _rope_tables(s):
    rows = s // GRID_W
    row = jnp.broadcast_to(jnp.arange(rows, dtype=F32)[:, None], (rows, GRID_W)).reshape(-1)
    col = jnp.broadcast_to(jnp.arange(GRID_W, dtype=F32)[None, :], (rows, GRID_W)).reshape(-1)
    half = QK_ROPE // 2
    inv_freq = ROPE_THETA ** (-jnp.arange(0, half, 2, dtype=F32) / half)
    ar, ac = row[:, None] * inv_freq, col[:, None] * inv_freq
    ones = jnp.ones((s, QK_NOPE), F32)
    tail = HEAD_PAD - QK_NOPE - QK_ROPE
    cos_t = jnp.concatenate([ones, jnp.cos(ar), jnp.cos(ar), jnp.cos(ac), jnp.cos(ac), jnp.ones((s, tail), F32)], 1)
    sin_t = jnp.concatenate([0 * ones, -jnp.sin(ar), jnp.sin(ar), -jnp.sin(ac), jnp.sin(ac),
                             jnp.zeros((s, tail), F32)], 1)
    return cos_t, sin_t


_Q4 = QK_ROPE // 4
ROPE_SWAP = np.concatenate([np.arange(_Q4, 2 * _Q4), np.arange(0, _Q4), np.arange(3 * _Q4, 4 * _Q4),
                            np.arange(2 * _Q4, 3 * _Q4)])


def _rope_slot(w, swap):
    if swap:
        w = w[..., ROPE_SWAP]
    pad = [(0, 0)] * (w.ndim - 1) + [(QK_NOPE, HEAD_PAD - QK_NOPE - QK_ROPE)]
    return jnp.pad(w, pad)


TILES = dict(inproj=512, tq=2048, tk=1408, fft_kb=8, merge=512, moe=256)


def kernel(x, c, ctx, c_ctx, w_mod, b_mod, norm_mix, norm_ffn, w_in, b_in, q_norm, w_uq, kv_norm, w_ukv, w_branch_attn, hy_conv_w, hy_conv_b, hy_filt_w1, hy_filt_b1, hy_filt_w2, hy_filt_b2, hy_filt_w3, hy_filt_freq, hy_skip, w_branch_hyena, w_out, w_router, router_bias, w_exp_gate, w_exp_up, w_exp_down, w_sh_gate, w_sh_up, w_sh_down, final_norm):
    bsz, s, d = x.shape
    tl = TILES
    assert w_mod.shape[0] == 1, "single-layer trunk"
    i = 0

    rows = -(-(bsz + 1) // SUBLANES) * SUBLANES
    c_rows = jnp.pad(jnp.concatenate([c, c_ctx[None]], axis=0), ((0, rows - bsz - 1), (0, 0)))
    mod_all = _mod_call(c_rows, w_mod[i], b_mod[i])
    mod_all = jnp.pad(mod_all.reshape(rows, 6, d), ((0, 0), (0, SUBLANES - 6), (0, 0)))
    mod, modc = mod_all[:bsz], mod_all[bsz:bsz + 1]

    cuts = np.cumsum([Q_LORA, KV_LORA, QK_ROPE, 3 * HY_WIDTH])
    wi, bi = w_in[i], b_in[i][None]
    w_q, w_kv, w_pe, w_hy, w_g = jnp.split(wi, cuts, axis=1)
    b_q, b_kv, b_pe, b_hy, b_g = jnp.split(bi, cuts, axis=1)
    wa = jnp.concatenate([w_q, w_kv, _rope_slot(w_pe, False), _rope_slot(w_pe, True)], axis=1).astype(BF16)
    ba = jnp.concatenate([b_q, b_kv, _rope_slot(b_pe, False), _rope_slot(b_pe, True)], axis=1)
    wq3 = w_uq[i].reshape(Q_LORA, N_HEADS, QK_NOPE + QK_ROPE) * (ATTN_SCALE * math.log2(math.e))
    tail = ((0, 0), (0, 0), (0, HEAD_PAD - QK_NOPE))
    wuq = (jnp.pad(wq3[..., :QK_NOPE], tail) + _rope_slot(wq3[..., QK_NOPE:], False)).reshape(Q_LORA, -1).astype(BF16)
    wuqs = _rope_slot(wq3[..., QK_NOPE:], True).reshape(Q_LORA, -1).astype(BF16)
    wkv3 = w_ukv[i].reshape(KV_LORA, N_HEADS, QK_NOPE + V_HEAD)
    wuk = jnp.pad(wkv3[..., :QK_NOPE], tail).reshape(KV_LORA, -1).astype(BF16)
    wuvt = wkv3[..., QK_NOPE:].reshape(KV_LORA, -1).T.astype(BF16)
    nm, qn, kvn = norm_mix[i][None], q_norm[i][None], kv_norm[i][None]

    w_c = jnp.concatenate([w_kv, _rope_slot(w_pe, False)], axis=1).astype(BF16)
    b_c = jnp.concatenate([b_kv, _rope_slot(b_pe, False)], axis=1)
    ck, cvt = _ctx_call(ctx, modc, nm, w_c, b_c, kvn, wuk, wuvt)

    cos_t, sin_t = _rope_tables(s)
    q, k, vt, hv, hx1, hx2, gate = _inproj_call(
        x, mod, nm, wa, ba, w_hy.astype(BF16), b_hy, w_g.astype(BF16), b_g, qn, wuq, wuqs, kvn, wuk, wuvt,
        cos_t, sin_t, hy_conv_w[i], hy_conv_b[i][None], tl["inproj"])

    attn = _attn_call(q, jnp.concatenate([ck, k], axis=2), jnp.concatenate([cvt, vt], axis=3), tl["tq"], tl["tk"])
    hy = _hyena(hv, hx1, hx2, hy_filt_w1[i], hy_filt_b1[i], hy_filt_w2[i], hy_filt_b2[i], hy_filt_w3[i],
                hy_filt_freq[i], hy_skip[i], tl["fft_kb"])
    xm = _merge_call(x, attn, hy, gate, mod, w_branch_attn[i].astype(BF16), w_branch_hyena[i].astype(BF16),
                     w_out[i].astype(BF16), tl["merge"])
    return _moe(xm, mod, norm_ffn[i][None], w_router[i], router_bias[i], w_exp_gate[i], w_exp_up[i], w_exp_down[i],
                w_sh_gate[i], w_sh_up[i], w_sh_down[i], final_norm[None], tl["moe"])
```

```python
import functools
import math

import numpy as np
import jax
import jax.numpy as jnp
from jax import lax
from jax.experimental import pallas as pl
from jax.experimental.pallas import tpu as pltpu

GRID_W = 64
N_HEADS = 8
QK_NOPE = 64
QK_ROPE = 32
V_HEAD = 64
Q_LORA = 256
KV_LORA = 128
ROPE_THETA = 10000.0
ATTN_SCALE = 1.0 / math.sqrt(QK_NOPE + QK_ROPE)
HY_WIDTH = 512
HY_ORDER = 2
HY_SHORT = 3
HY_BANDS = 8
HY_EMB = 1 + 2 * HY_BANDS
HY_EMB_PAD = 32
HY_FAST_DECAY = 0.3
HY_SLOW_DECAY = 1.5
HY_DECAY_TARGET = 1e-2
N_EXPERTS = 64
N_GROUPS = 8
GROUP_SIZE = N_EXPERTS // N_GROUPS
TOPK_GROUPS = 4
TOP_K = 8
EXPERT_FF = 256
ROUTE_SCALE = 2.5
EXPERT_BLOCK = 1024
RUN_ALIGN = 8
ROUTE_TILES = 8
LONG_RUN = 8
NORM_EPS = 1e-6

HEAD_PAD = 128
Q_CHUNK = 512
AHEAD = 2
LANES = 128
SUBLANES = 8
VMEM_LIMIT = 48 * 1024 * 1024

F32 = jnp.float32
BF16 = jnp.bfloat16
NT_DIMS = (((1,), (1,)), ((), ()))
NN_DIMS = (((1,), (0,)), ((), ()))


def _params(*sem):
    return pltpu.CompilerParams(dimension_semantics=sem, vmem_limit_bytes=VMEM_LIMIT)


def _dot(a, b):
    return jnp.dot(a.astype(BF16), b.astype(BF16), preferred_element_type=F32)


def _split(a):
    hi = a.astype(BF16)
    lo = (a - hi.astype(F32)).astype(BF16)
    return hi, lo


def _dot3(a, b, dims=NN_DIMS):
    ah, al = _split(a)
    bh, bl = _split(b)
    d = functools.partial(lax.dot_general, dimension_numbers=dims, preferred_element_type=F32)
    return d(ah, bh) + (d(ah, bl) + d(al, bh))


def _rms(x, g):
    return x * lax.rsqrt(jnp.mean(x * x, axis=-1, keepdims=True) + NORM_EPS) * g


def _silu(x):
    return x * jax.nn.sigmoid(x)


def _full(shape):
    nd = len(shape)
    return pl.BlockSpec(shape, lambda *_: (0,) * nd)


def _mod_kernel(c_ref, w_ref, b_ref, o_ref):
    o_ref[...] = _dot3(_silu(c_ref[...]), w_ref[...]) + b_ref[...]


def _mod_call(c_rows, w_mod, b_mod):
    r, d = c_rows.shape
    n = w_mod.shape[1]
    bn = 1024
    return pl.pallas_call(
        _mod_kernel,
        grid=(n // bn,),
        in_specs=[_full((r, d)), pl.BlockSpec((d, bn), lambda j: (0, j)), pl.BlockSpec((1, bn), lambda j: (0, j))],
        out_specs=pl.BlockSpec((r, bn), lambda j: (0, j)),
        out_shape=jax.ShapeDtypeStruct((r, n), F32),
        compiler_params=_params("arbitrary"),
        name="mod",
    )(c_rows, w_mod, b_mod.reshape(1, n))


def _prenorm(x, mod_ref, row, g):
    shift = mod_ref[0, row:row + 1, :]
    scale = mod_ref[0, row + 1:row + 2, :]
    return _rms(x, g) * (1.0 + scale) + shift


def _kv_heads(kv_lat, kpe, kvn_ref, wuk_ref, wuvt_ref, k_out, vt_out):
    kvn = _rms(kv_lat, kvn_ref[...]).astype(BF16)
    kk = _dot(kvn, wuk_ref[...])
    vt = lax.dot_general(wuvt_ref[...], kvn, NT_DIMS, preferred_element_type=F32)
    ones = jnp.ones((HEAD_PAD - V_HEAD, vt.shape[1]), F32)
    for h in range(N_HEADS):
        k_out[0, h] = (kk[:, HEAD_PAD * h:HEAD_PAD * (h + 1)] + kpe).astype(BF16)
        vt_out[0, h] = jnp.concatenate([vt[V_HEAD * h:V_HEAD * (h + 1)], ones], axis=0).astype(BF16)


def _ctx_kernel(c_ref, mod_ref, nm_ref, w_ref, b_ref, kvn_ref, wuk_ref, wuv_ref, k_out, v_out):
    h = _prenorm(c_ref[0], mod_ref, 0, nm_ref[...]).astype(BF16)
    a = _dot(h, w_ref[...]) + b_ref[...]
    _kv_heads(a[:, :KV_LORA], a[:, KV_LORA:], kvn_ref, wuk_ref, wuv_ref, k_out, v_out)


def _ctx_call(ctx, modc, norm_mix, w_c, b_c, kv_norm, w_uk, w_uv):
    bsz, n, d = ctx.shape
    return pl.pallas_call(
        _ctx_kernel,
        grid=(bsz,),
        in_specs=[pl.BlockSpec((1, n, d), lambda b: (b, 0, 0)), _full(modc.shape), _full(norm_mix.shape),
                  _full(w_c.shape), _full(b_c.shape), _full(kv_norm.shape), _full(w_uk.shape), _full(w_uv.shape)],
        out_specs=[pl.BlockSpec((1, N_HEADS, n, HEAD_PAD), lambda b: (b, 0, 0, 0)),
                   pl.BlockSpec((1, N_HEADS, HEAD_PAD, n), lambda b: (b, 0, 0, 0))],
        out_shape=[jax.ShapeDtypeStruct((bsz, N_HEADS, n, HEAD_PAD), BF16),
                   jax.ShapeDtypeStruct((bsz, N_HEADS, HEAD_PAD, n), BF16)],
        compiler_params=_params("arbitrary"),
        name="ctx",
    )(ctx, modc, norm_mix, w_c, b_c, kv_norm, w_uk, w_uv)


def _inproj_kernel(x_ref, xp_ref, xn_ref, mod_ref, nm_ref, wa_ref, ba_ref, why_ref, bhy_ref, wg_ref, bg_ref,
                   qn_ref, wuq_ref, wuqs_ref, kvn_ref, wuk_ref, wuv_ref, cos_ref, sin_ref, cw_ref, cb_ref,
                   q_out, k_out, v_out, hv_out, hx1_out, hx2_out, g_out):
    i = pl.program_id(0)
    tm = x_ref.shape[1]
    nm = nm_ref[...]
    h = _prenorm(x_ref[0], mod_ref, 0, nm).astype(BF16)
    a = _dot(h, wa_ref[...]) + ba_ref[...]
    q_lat = a[:, :Q_LORA]
    kv_lat = a[:, Q_LORA:Q_LORA + KV_LORA]
    kpe_m = a[:, Q_LORA + KV_LORA:Q_LORA + KV_LORA + HEAD_PAD]
    kpe_s = a[:, Q_LORA + KV_LORA + HEAD_PAD:]
    cos = cos_ref[...]
    sin = sin_ref[...]
    qn = _rms(q_lat, qn_ref[...]).astype(BF16)
    qa = _dot(qn, wuq_ref[...])
    qs = _dot(qn, wuqs_ref[...])
    for hh in range(N_HEADS):
        sl = slice(HEAD_PAD * hh, HEAD_PAD * (hh + 1))
        q_out[0, hh] = (qa[:, sl] * cos + qs[:, sl] * sin).astype(BF16)
    _kv_heads(kv_lat, kpe_m * cos + kpe_s * sin, kvn_ref, wuk_ref, wuv_ref, k_out, v_out)
    g_out[0] = (_dot(h, wg_ref[...]) + bg_ref[...]).astype(BF16)

    why = why_ref[...]
    bhy = bhy_ref[...]
    halo = jnp.concatenate([_prenorm(xp_ref[0], mod_ref, 0, nm), _prenorm(xn_ref[0], mod_ref, 0, nm)], axis=0)
    hy_all = _dot(jnp.concatenate([h, halo.astype(BF16)], axis=0), why) + bhy
    hy = hy_all[:tm]
    prev = jnp.where(i == 0, 0.0, hy_all[tm + SUBLANES - 1:tm + SUBLANES])
    nxt = jnp.where(i == pl.num_programs(0) - 1, 0.0, hy_all[tm + SUBLANES:tm + SUBLANES + 1])
    rid = lax.broadcasted_iota(jnp.int32, (tm, 1), 0)
    up = jnp.where(rid == 0, prev, pltpu.roll(hy, 1, 0))
    dn = jnp.where(rid == tm - 1, nxt, pltpu.roll(hy, tm - 1, 0))
    u = up * cw_ref[0:1, :] + hy * cw_ref[1:2, :] + dn * cw_ref[2:3, :] + cb_ref[...]
    hv_out[0] = u[:, :HY_WIDTH]
    hx1_out[0] = u[:, HY_WIDTH:2 * HY_WIDTH]
    hx2_out[0] = u[:, 2 * HY_WIDTH:]


def _inproj_call(x, mod, norm_mix, wa, ba, why, bhy, wg, bg, q_norm, wuq, wuqs, kv_norm, wuk, wuvt, cos_t, sin_t, cw,
                 cb, tm):
    bsz, s, d = x.shape
    nt = s // tm
    rb = tm // SUBLANES
    last_rb = s // SUBLANES - 1
    consts = [norm_mix, wa, ba, why, bhy, wg, bg, q_norm, wuq, wuqs, kv_norm, wuk, wuvt]
    in_specs = [
        pl.BlockSpec((1, tm, d), lambda i, b: (b, i, 0)),
        pl.BlockSpec((1, SUBLANES, d), lambda i, b: (b, jnp.maximum(i * rb - 1, 0), 0)),
        pl.BlockSpec((1, SUBLANES, d), lambda i, b: (b, jnp.minimum((i + 1) * rb, last_rb), 0)),
        pl.BlockSpec((1, SUBLANES, d), lambda i, b: (b, 0, 0)),
    ] + [_full(c.shape) for c in consts] + [
        pl.BlockSpec((tm, HEAD_PAD), lambda i, b: (i, 0)),
        pl.BlockSpec((tm, HEAD_PAD), lambda i, b: (i, 0)),
        _full(cw.shape), _full(cb.shape),
    ]
    hw = HY_WIDTH
    out_specs = [
        pl.BlockSpec((1, N_HEADS, tm, HEAD_PAD), lambda i, b: (b, 0, i, 0)),
        pl.BlockSpec((1, N_HEADS, tm, HEAD_PAD), lambda i, b: (b, 0, i, 0)),
        pl.BlockSpec((1, N_HEADS, HEAD_PAD, tm), lambda i, b: (b, 0, 0, i)),
        pl.BlockSpec((1, tm, hw), lambda i, b: (b, i, 0)),
        pl.BlockSpec((1, tm, hw), lambda i, b: (b, i, 0)),
        pl.BlockSpec((1, tm, hw), lambda i, b: (b, i, 0)),
        pl.BlockSpec((1, tm, 2 * d), lambda i, b: (b, i, 0)),
    ]
    out_shape = [
        jax.ShapeDtypeStruct((bsz, N_HEADS, s, HEAD_PAD), BF16),
        jax.ShapeDtypeStruct((bsz, N_HEADS, s, HEAD_PAD), BF16),
        jax.ShapeDtypeStruct((bsz, N_HEADS, HEAD_PAD, s), BF16),
        jax.ShapeDtypeStruct((bsz, s, hw), F32),
        jax.ShapeDtypeStruct((bsz, s, hw), F32),
        jax.ShapeDtypeStruct((bsz, s, hw), F32),
        jax.ShapeDtypeStruct((bsz, s, 2 * d), BF16),
    ]
    return pl.pallas_call(
        _inproj_kernel,
        grid=(nt, bsz),
        in_specs=in_specs,
        out_specs=out_specs,
        out_shape=out_shape,
        compiler_params=_params("arbitrary", "arbitrary"),
        name="inproj",
    )(x, x, x, mod, *consts, cos_t, sin_t, cw, cb)


def _attn_kernel(q_ref, k_ref, vt_ref, o_ref, m_sc, acc_sc):
    j = pl.program_id(2)

    @pl.when(j == 0)
    def _():
        m_sc[...] = jnp.full(m_sc.shape, -jnp.inf, F32)
        acc_sc[...] = jnp.zeros(acc_sc.shape, F32)

    tq = q_ref.shape[2]
    qw = min(tq, Q_CHUNK)
    units = [(h, c) for h in range(N_HEADS) for c in range(0, tq, qw)]

    def scores(u):
        h, c = units[u]
        return lax.dot_general(k_ref[0, h], q_ref[0, h, c:c + qw, :], NT_DIMS,
                               preferred_element_type=F32)

    pending = [scores(u) for u in range(AHEAD)]
    for u, (h, c) in enumerate(units):
        if u + AHEAD < len(units):
            pending.append(scores(u + AHEAD))
        st = pending.pop(0)
        m_prev = m_sc[h, :, c:c + qw]
        m_new = jnp.maximum(m_prev, jnp.max(st, axis=0, keepdims=True))
        pt = jnp.exp2(st - m_new).astype(BF16)
        acc_sc[h, :, c:c + qw] = (jnp.exp2(m_prev - m_new) * acc_sc[h, :, c:c + qw]
                                  + jnp.dot(vt_ref[0, h], pt, preferred_element_type=F32))
        m_sc[h, :, c:c + qw] = m_new

    @pl.when(j == pl.num_programs(2) - 1)
    def _():
        ot = jnp.concatenate([acc_sc[h, :V_HEAD] / acc_sc[h, V_HEAD:V_HEAD + 1] for h in range(N_HEADS)], axis=0)
        o_ref[0] = ot.T.astype(o_ref.dtype)


def _attn_call(q, k, vt, tq, tk):
    bsz, nh, s, dh = q.shape
    nk = k.shape[2]
    dv = nh * V_HEAD
    return pl.pallas_call(
        _attn_kernel,
        grid=(bsz, s // tq, nk // tk),
        in_specs=[
            pl.BlockSpec((1, nh, tq, dh), lambda b, i, j: (b, 0, i, 0)),
            pl.BlockSpec((1, nh, tk, dh), lambda b, i, j: (b, 0, j, 0)),
            pl.BlockSpec((1, nh, dh, tk), lambda b, i, j: (b, 0, 0, j)),
        ],
        out_specs=pl.BlockSpec((1, tq, dv), lambda b, i, j: (b, i, 0)),
        out_shape=jax.ShapeDtypeStruct((bsz, s, dv), BF16),
        scratch_shapes=[pltpu.VMEM((nh, 1, tq), F32), pltpu.VMEM((nh, dh, tq), F32)],
        compiler_params=_params("arbitrary", "arbitrary", "arbitrary"),
        name="attn",
    )(q, k, vt)


def _filter_kernel(emb_ref, w1_ref, b1_ref, w2_ref, b2_ref, w3_ref, fr_ref, dl_ref, full_out, asum_out, *, seq):
    r = pl.program_id(0)
    rb = emb_ref.shape[0]
    emb = emb_ref[...]
    fr = fr_ref[...]
    h = jnp.sin(fr * (_dot3(emb, w1_ref[...]) + b1_ref[...]))
    h = jnp.sin(fr * (_dot3(h, w2_ref[...]) + b2_ref[...]))
    k = _dot3(h, w3_ref[0]) * jnp.exp(-emb[:, 0:1] * dl_ref[...])
    row = r * rb + lax.broadcasted_iota(jnp.int32, (rb, 1), 0)
    k = jnp.where(row == seq, 0.0, k)
    full_out[...] = k

    @pl.when(r == 0)
    def _():
        asum_out[...] = jnp.zeros(asum_out.shape, F32)

    asum_out[...] += jnp.sum(jnp.abs(k), axis=0, keepdims=True)


def _filter_call(emb, w1, b1, w2, b2, w3sel, freq, deltas2, seq, rb):
    n2 = emb.shape[0]
    half_blocks = seq // rb
    width = w3sel.shape[2]
    return pl.pallas_call(
        functools.partial(_filter_kernel, seq=seq),
        grid=(n2 // rb,),
        in_specs=[pl.BlockSpec((rb, HY_EMB_PAD), lambda r: (r, 0)), _full(w1.shape), _full(b1.shape),
                  _full(w2.shape), _full(b2.shape),
                  pl.BlockSpec((1,) + w3sel.shape[1:], lambda r: (r // half_blocks, 0, 0)),
                  _full(freq.shape), _full(deltas2.shape)],
        out_specs=[pl.BlockSpec((rb, width), lambda r: (r, 0)), pl.BlockSpec((1, width), lambda r: (0, 0))],
        out_shape=[jax.ShapeDtypeStruct((n2, width), F32), jax.ShapeDtypeStruct((1, width), F32)],
        compiler_params=_params("arbitrary"),
        name="filt",
    )(emb, w1, b1, w2, b2, w3sel, freq, deltas2)


def _fa_kernel(u_ref, f_ref, a_out):
    two, _, hn, g, c = u_ref.shape
    a = _dot(f_ref[...], u_ref[...].reshape(two * hn * g, c))
    a_out[...] = (_pack(a) if a_out.dtype == jnp.uint32 else a).reshape(a_out.shape)


def _fa_call(u5, fmat, packed):
    _, p, hn, n, c = u5.shape
    g = SUBLANES
    co, dt = (c // 2, jnp.uint32) if packed else (c, F32)
    return pl.pallas_call(
        _fa_kernel,
        grid=(p, n // g),
        in_specs=[pl.BlockSpec((2, 1, hn, g, c), lambda q, j: (0, q, 0, j, 0)), _full(fmat.shape)],
        out_specs=pl.BlockSpec((1, 2, n, g, co), lambda q, j: (q, 0, 0, j, 0)),
        out_shape=jax.ShapeDtypeStruct((p, 2, n, n, co), dt),
        compiler_params=_params("arbitrary", "arbitrary"),
        name="fa",
    )(u5, fmat)


def _dot_packed(w, u):
    lo, hi = _unpack(u)
    return jnp.concatenate([jnp.dot(w, lo, preferred_element_type=F32), jnp.dot(w, hi, preferred_element_type=F32)],
                           axis=1)


def _fb_kernel(a_ref, g_ref, asum_ref, kf_out):
    _, two, kb, n, c = a_ref.shape
    scale = 1.0 / (asum_ref[...] + 1e-6)
    for kk in range(kb):
        x = _dot(g_ref[kk], a_ref[0, :, kk].reshape(two * n, c)) * scale
        kf_out[kk] = x.reshape(two, n, c)


def _fb_call(a5, gmat, asum, kb):
    _, _, n, _, c = a5.shape
    return pl.pallas_call(
        _fb_kernel,
        grid=(n // kb,),
        in_specs=[pl.BlockSpec((1, 2, kb, n, c), lambda k: (0, 0, k, 0, 0)),
                  pl.BlockSpec((kb, 2 * n, 2 * n), lambda k: (k, 0, 0)), _full(asum.shape)],
        out_specs=pl.BlockSpec((kb, 2, n, c), lambda k: (k, 0, 0, 0)),
        out_shape=jax.ShapeDtypeStruct((n, 2, n, c), F32),
        compiler_params=_params("arbitrary"),
        name="fb",
    )(a5, gmat, asum)


def _mid_kernel(a_ref, g_ref, h_ref, kf_ref, b_out):
    _, two, kb, n, c = a_ref.shape
    for kk in range(kb):
        x = _dot_packed(g_ref[kk], a_ref[0, :, kk].reshape(two * n, c))
        xr, xi = x[:n], x[n:]
        kr, ki = kf_ref[kk, 0], kf_ref[kk, 1]
        y = jnp.concatenate([xr * kr - xi * ki, xr * ki + xi * kr], axis=0)
        b_out[0, :, kk] = _pack(_dot(h_ref[kk], y)).reshape(two, n, c)


def _mid_call(a5, gmat, hmat, kf, order, kb):
    p, _, n, _, c = a5.shape
    return pl.pallas_call(
        _mid_kernel,
        grid=(n // kb, p),
        in_specs=[pl.BlockSpec((1, 2, kb, n, c), lambda k, q: (q, 0, k, 0, 0)),
                  pl.BlockSpec((kb, 2 * n, 2 * n), lambda k, q: (k, 0, 0)),
                  pl.BlockSpec((kb, 2 * n, 2 * n), lambda k, q: (k, 0, 0)),
                  pl.BlockSpec((kb, 2, n, 2 * c), lambda k, q: (k, 0, 0, order))],
        out_specs=pl.BlockSpec((1, 2, kb, n, c), lambda k, q: (q, 0, k, 0, 0)),
        out_shape=jax.ShapeDtypeStruct(a5.shape, jnp.uint32),
        compiler_params=_params("arbitrary", "arbitrary"),
        name="mid",
    )(a5, gmat, hmat, kf)


def _fc_kernel(b_ref, f_ref, u_ref, m_ref, skip_ref, o_out):
    _, two, n, g, c = b_ref.shape
    y = _dot_packed(f_ref[...], b_ref[...].reshape(two * n * g, c)).reshape(u_ref.shape)
    o_out[...] = m_ref[...] * (y + u_ref[...] * skip_ref[...])


def _fc_call(b5, finv, u5, m5, skip_row):
    _, p, hn, n, c = u5.shape
    g = SUBLANES
    blk = pl.BlockSpec((2, 1, hn, g, c), lambda q, j: (0, q, 0, j, 0))
    return pl.pallas_call(
        _fc_kernel,
        grid=(p, n // g),
        in_specs=[pl.BlockSpec((1, 2, n, g, c // 2), lambda q, j: (q, 0, 0, j, 0)), _full(finv.shape), blk, blk,
                  _full(skip_row.shape)],
        out_specs=blk,
        out_shape=jax.ShapeDtypeStruct(u5.shape, F32),
        compiler_params=_params("arbitrary", "arbitrary"),
        name="fc",
    )(b5, finv, u5, m5, skip_row)


def _dft_tables(n):
    hn = n // 2
    k = np.arange(n)[:, None]
    ang = -2.0 * np.pi * (k * np.arange(n)[None, :] % n) / n
    fr, fi = np.cos(ang), np.sin(ang)
    f_data = np.block([[fr[:, :hn], -fi[:, :hn]], [fi[:, :hn], fr[:, :hn]]])
    f_filt = np.concatenate([fr, fi], axis=0)
    er, ei = fr[:hn], -fi[:hn]
    f_inv = np.block([[er, -ei], [ei, er]]) / float(n * n)
    k1 = jnp.arange(n, dtype=jnp.int32)[:, None, None]
    k2 = jnp.arange(n, dtype=jnp.int32)[None, :, None]
    m2 = jnp.arange(n, dtype=jnp.int32)[None, None, :]
    idx = (m2 * (k1 + n * k2)) % (n * n)
    ang2 = idx.astype(F32) * (-2.0 * math.pi / (n * n))
    gr, gi = jnp.cos(ang2), jnp.sin(ang2)
    g = jnp.concatenate([jnp.concatenate([gr, -gi], axis=2), jnp.concatenate([gi, gr], axis=2)], axis=1)
    h = jnp.swapaxes(g, 1, 2)

    def widen(f):
        return jnp.asarray(np.kron(f, np.eye(SUBLANES)), BF16)

    return widen(f_data), widen(f_filt), widen(f_inv), g.astype(BF16), h.astype(BF16)


def _hyena_filter_tables(seq):
    t = jnp.linspace(0.0, 1.0, seq, dtype=F32)[:, None]
    w = 2.0 * math.pi * jnp.arange(seq, dtype=F32)[:, None] / seq
    f = jnp.linspace(1e-4, HY_BANDS - 1, HY_BANDS, dtype=F32)[None, :]
    emb = jnp.concatenate([t, jnp.cos(f * w), -jnp.sin(f * w)], axis=-1)
    emb = jnp.concatenate([emb, emb[:1], emb[:0:-1]], axis=0)
    emb = jnp.pad(emb, ((0, 0), (0, HY_EMB_PAD - HY_EMB)))
    deltas = jnp.abs(jnp.linspace(math.log(HY_DECAY_TARGET) / HY_SLOW_DECAY,
                                  math.log(HY_DECAY_TARGET) / HY_FAST_DECAY, HY_WIDTH, dtype=F32))
    return emb, jnp.tile(deltas, HY_ORDER)[None, :]


def _hyena(hv, hx1, hx2, w1, b1, w2, b2, w3, freq, skip, kb):
    bsz, seq, c = hv.shape
    n = int(round(math.sqrt(2 * seq)))
    assert n * n == 2 * seq and bsz % 2 == 0
    hn, p = n // 2, bsz // 2
    f_data, f_filt, f_inv, gmat, hmat = _dft_tables(n)

    emb, deltas2 = _hyena_filter_tables(seq)
    w1p = jnp.pad(w1, ((0, HY_EMB_PAD - HY_EMB), (0, 0)))
    w3r = w3.reshape(w3.shape[0], HY_ORDER, 2, c)
    w3sel = jnp.stack([w3r[:, :, 0, :].reshape(-1, HY_ORDER * c), w3r[:, :, 1, :].reshape(-1, HY_ORDER * c)])
    full, asum = _filter_call(emb, w1p, b1[None], w2, b2[None], w3sel, freq[None], deltas2, seq, min(512, seq))
    c2 = HY_ORDER * c
    kf = _fb_call(_fa_call(full.reshape(2, 1, hn, n, c2), f_filt, False), gmat, asum, kb // HY_ORDER)

    def view(t):
        return t.reshape(2, p, hn, n, c)

    def long_conv(u5, m5, order):
        bm = _mid_call(_fa_call(u5, f_data, True), gmat, hmat, kf, order, kb)
        return _fc_call(bm, f_inv, u5, m5, skip[order][None, :])

    z = long_conv(view(hv), view(hx1), 0)
    return long_conv(z, view(hx2), 1).reshape(bsz, seq, c)


def _merge_kernel(x_ref, at_ref, hy_ref, g_ref, mod_ref, wba_ref, wbh_ref, wo_ref, o_ref):
    d = x_ref.shape[2]
    g = g_ref[0].astype(F32)
    y = (jax.nn.sigmoid(g[:, :d]) * _dot(at_ref[0], wba_ref[...])
         + jax.nn.sigmoid(g[:, d:]) * _dot(hy_ref[0], wbh_ref[...]))
    o_ref[0] = x_ref[0] + mod_ref[0, 2:3, :] * _dot(y, wo_ref[...])


def _merge_call(x, attn, hy, gate, mod, wba, wbh, wo, tm):
    bsz, s, d = x.shape

    def tok(w):
        return pl.BlockSpec((1, tm, w), lambda b, i: (b, i, 0))

    return pl.pallas_call(
        _merge_kernel,
        grid=(bsz, s // tm),
        in_specs=[tok(d), tok(attn.shape[2]), tok(hy.shape[2]), tok(2 * d),
                  pl.BlockSpec((1, SUBLANES, d), lambda b, i: (b, 0, 0)),
                  _full(wba.shape), _full(wbh.shape), _full(wo.shape)],
        out_specs=tok(d),
        out_shape=jax.ShapeDtypeStruct((bsz, s, d), F32),
        compiler_params=_params("arbitrary", "arbitrary"),
        name="merge",
    )(x, attn, hy, gate, mod, wba, wbh, wo)


def _route_kernel(xm_ref, mod_ref, nf_ref, wrt_ref, rb_ref, tri_ref, lt_ref, h2_out, w_out, p_out, col_out, row_out):
    tm = xm_ref.shape[1]
    ng, gs = N_GROUPS, GROUP_SIZE

    h2 = _prenorm(xm_ref[0], mod_ref, 3, nf_ref[...])
    h2_out[0] = h2.astype(h2_out.dtype)
    scores = jax.nn.sigmoid(_dot3(wrt_ref[...], h2, NT_DIMS))
    sel = scores + rb_ref[...]
    slabs = [sel[ng * j:ng * (j + 1)] for j in range(gs)]

    top1 = jnp.full((ng, tm), -jnp.inf, F32)
    top2 = top1
    for x in slabs:
        top2 = jnp.maximum(top2, jnp.minimum(top1, x))
        top1 = jnp.maximum(top1, x)
    gscore = top1 + top2
    gid = lax.broadcasted_iota(jnp.int32, (ng, 1), 0)
    rank = jnp.zeros((ng, tm), jnp.int32)
    for g2 in range(ng):
        row = gscore[g2:g2 + 1]
        beats = (row > gscore) | ((row == gscore) & (g2 < gid))
        rank = rank + beats.astype(jnp.int32)
    gmask = rank < TOPK_GROUPS

    cand = [jnp.where(gmask, x, -jnp.inf) for x in slabs]
    eid = [gid * gs + j for j in range(gs)]
    chosen = []
    for _ in range(TOP_K):
        best = functools.reduce(jnp.maximum, cand)
        best = jnp.max(best, axis=0, keepdims=True)
        idx = functools.reduce(jnp.minimum, [jnp.where(cand[j] == best, eid[j], N_EXPERTS) for j in range(gs)])
        idx = jnp.min(idx, axis=0, keepdims=True)
        chosen.append(idx)
        cand = [jnp.where(eid[j] == idx, -jnp.inf, cand[j]) for j in range(gs)]

    mask = [functools.reduce(jnp.logical_or, [eid[j] == idx for idx in chosen]) for j in range(gs)]
    maskb = jnp.concatenate(mask, axis=0)
    wsel = jnp.where(maskb, scores, 0.0)
    w_out[...] = wsel / jnp.sum(wsel, axis=0, keepdims=True) * ROUTE_SCALE

    def extents(cnt, lower_sum):
        units = jnp.floor((cnt + (RUN_ALIGN - 1)) * (1.0 / RUN_ALIGN))
        start = RUN_ALIGN * lower_sum(units.astype(BF16))
        return start, start + RUN_ALIGN * units

    lane = lax.broadcasted_iota(jnp.int32, (N_EXPERTS, LANES), 1)
    sub = lax.broadcasted_iota(jnp.int32, (SUBLANES, N_EXPERTS), 0)
    ts = tri_ref.shape[0]
    for c in range(tm // ts):
        mb = maskb[:, c * ts:(c + 1) * ts]
        maskf = jnp.where(mb, 1.0, 0.0)
        mask16 = maskf.astype(BF16)
        before = jnp.dot(mask16, tri_ref[...], preferred_element_type=F32)
        p_out[:, c * ts:(c + 1) * ts] = jnp.where(mb, before, -1.0).astype(p_out.dtype)
        cnt_c = jnp.sum(maskf, axis=1, keepdims=True)
        start_c, end_c = extents(jnp.broadcast_to(cnt_c, (N_EXPERTS, LANES)),
                                 lambda u: jnp.dot(lt_ref[...], u, preferred_element_type=F32))
        col_out[c] = jnp.where(lane == 0, cnt_c, jnp.where(lane == 1, start_c, end_c))
        cnt_r = lax.dot_general(jnp.ones((SUBLANES, ts), BF16), mask16, NT_DIMS, preferred_element_type=F32)
        start_r, end_r = extents(cnt_r,
                                 lambda u: lax.dot_general(u, lt_ref[...], NT_DIMS, preferred_element_type=F32))
        row_out[c] = jnp.where(sub == 0, start_r, end_r)


def _route_call(xm, mod, norm_ffn, wrt, rbias, lower, ts, tiles_per_step):
    bsz, s, d = xm.shape
    t = bsz * s
    tm = ts * tiles_per_step
    nt = s // tm
    tri = (jnp.arange(ts)[:, None] < jnp.arange(ts)[None, :]).astype(BF16)
    tok = pl.BlockSpec((N_EXPERTS, tm), lambda i: (0, i))
    return pl.pallas_call(
        _route_kernel,
        grid=(t // tm,),
        in_specs=[pl.BlockSpec((1, tm, d), lambda i: (i // nt, i % nt, 0)),
                  pl.BlockSpec((1, SUBLANES, d), lambda i: (i // nt, 0, 0)),
                  _full(norm_ffn.shape), _full(wrt.shape), _full(rbias.shape), _full(tri.shape),
                  _full(lower.shape)],
        out_specs=[pl.BlockSpec((1, tm, d), lambda i: (i // nt, i % nt, 0)), tok, tok,
                   pl.BlockSpec((tiles_per_step, N_EXPERTS, LANES), lambda i: (i, 0, 0)),
                   pl.BlockSpec((tiles_per_step, SUBLANES, N_EXPERTS), lambda i: (i, 0, 0))],
        out_shape=[jax.ShapeDtypeStruct((bsz, s, d), BF16), jax.ShapeDtypeStruct((N_EXPERTS, t), F32),
                   jax.ShapeDtypeStruct((N_EXPERTS, t), BF16),
                   jax.ShapeDtypeStruct((t // ts, N_EXPERTS, LANES), F32),
                   jax.ShapeDtypeStruct((t // ts, SUBLANES, N_EXPERTS), F32)],
        compiler_params=_params("arbitrary"),
        name="route",
    )(xm, mod, norm_ffn, wrt, rbias, tri, lower)


def _pack(x):
    w = x.shape[1] // 2
    lo = lax.bitcast_convert_type(x[:, :w].astype(BF16).astype(F32), jnp.uint32)
    hi = lax.bitcast_convert_type(x[:, w:].astype(BF16).astype(F32), jnp.uint32)
    return hi | (lo >> 16)


def _unpack(u):
    lo = lax.bitcast_convert_type(u << 16, F32).astype(BF16)
    hi = lax.bitcast_convert_type(u & jnp.uint32(0xFFFF0000), F32).astype(BF16)
    return lo, hi


def _pow2_pieces(units, limit):
    bit = 1
    while bit * 2 <= limit:
        bit *= 2
    while bit:
        yield (units & bit) != 0, units & ~(2 * bit - 1), bit
        bit //= 2


def _rows_copy(vm_ref, hbm_ref, sem, vm_row, hbm_row, rows, to_hbm):
    v = vm_ref.at[pl.ds(pl.multiple_of(vm_row, RUN_ALIGN), rows), :]
    h = hbm_ref.at[pl.ds(pl.multiple_of(hbm_row, RUN_ALIGN), rows), :]
    return pltpu.make_async_copy(v, h, sem) if to_hbm else pltpu.make_async_copy(h, v, sem)


def _run_copies(vm_ref, hbm_ref, sem, n8, vm_row, hbm_row, limit, to_hbm, act):
    def emit(pieces):
        for on, off, size in pieces:
            @pl.when(on)
            def _():
                act(_rows_copy(vm_ref, hbm_ref, sem, vm_row + RUN_ALIGN * off, hbm_row + RUN_ALIGN * off,
                               RUN_ALIGN * size, to_hbm))

    pieces = list(_pow2_pieces(n8, limit))
    long_pieces = [p for p in pieces if p[2] >= LONG_RUN]
    if long_pieces:
        pl.when(n8 >= LONG_RUN)(lambda: emit(long_pieces))
    emit([p for p in pieces if p[2] < LONG_RUN])


def _wait_rows(vm_ref, hbm_ref, sem, units, limit, to_hbm):
    for on, _, size in _pow2_pieces(units, limit):
        @pl.when(on)
        def _():
            _rows_copy(vm_ref, hbm_ref, sem, 0, 0, RUN_ALIGN * size, to_hbm).wait()


def _dispatch_kernel(n8_ref, ls_ref, gs_ref, ts_ref, t8_ref, nu_ref, pos_ref, ext_ref, h_ref, xs_out, srt2, zbuf,
                     sems):
    step = pl.program_id(0)
    tm = h_ref.shape[0]
    rows = srt2.shape[1]
    slot = step % 2
    srt, sem = srt2.at[slot], sems.at[slot]
    rid = lax.broadcasted_iota(jnp.int32, (rows, 1), 0).astype(F32)
    start = ext_ref[0, 0:1, :]
    member = jnp.where((rid >= start) & (rid < ext_ref[0, 1:2, :]), 1.0, 0.0)
    offset = rid - jnp.sum(member * start, axis=1, keepdims=True)
    pos = jnp.dot(member.astype(BF16), pos_ref[...], preferred_element_type=F32)
    sel = jnp.where(pos == offset, 1.0, 0.0).astype(BF16)
    srt[...] = _pack(jnp.dot(sel, h_ref[...], preferred_element_type=F32))

    def send(e, c):
        i = step * N_EXPERTS + e
        _run_copies(srt, xs_out, sem, n8_ref[i], ls_ref[i], gs_ref[i], tm // RUN_ALIGN, True, lambda cp: cp.start())
        return c

    lax.fori_loop(0, N_EXPERTS, send, 0)

    def wait_tile(tile, s):
        last = tile * N_EXPERTS + N_EXPERTS - 1
        _wait_rows(srt2.at[s], xs_out, sems.at[s], ls_ref[last] // RUN_ALIGN + n8_ref[last], rows // RUN_ALIGN, True)

    pl.when(step > 0)(lambda: wait_tile(step - 1, 1 - slot))

    @pl.when(step == pl.num_programs(0) - 1)
    def _():
        wait_tile(step, slot)
        zbuf[...] = jnp.zeros(zbuf.shape, zbuf.dtype)
        nblk = xs_out.shape[0] // EXPERT_BLOCK

        def fill(act):
            def tails(e, c):
                _run_copies(zbuf, xs_out, sem, t8_ref[e], 0, ts_ref[e], EXPERT_BLOCK // RUN_ALIGN - 1, True, act)
                return c

            def blocks(b, c):
                act(pltpu.make_async_copy(
                    zbuf, xs_out.at[pl.ds(pl.multiple_of(b * EXPERT_BLOCK, EXPERT_BLOCK), EXPERT_BLOCK), :], sem))
                return c

            lax.fori_loop(0, N_EXPERTS, tails, 0)
            lax.fori_loop(nu_ref[0], nblk, blocks, 0)

        fill(lambda cp: cp.start())
        fill(lambda cp: cp.wait())


def _dispatch_call(tables, pos_et, ext_rows, h2, nblk, tm):
    t, d = h2.shape
    lrows = TOP_K * tm + N_EXPERTS * RUN_ALIGN
    return pl.pallas_call(
        _dispatch_kernel,
        grid_spec=pltpu.PrefetchScalarGridSpec(
            num_scalar_prefetch=len(tables), grid=(t // tm,),
            in_specs=[pl.BlockSpec((N_EXPERTS, tm), lambda i, *_: (0, i)),
                      pl.BlockSpec((1,) + ext_rows.shape[1:], lambda i, *_: (i, 0, 0)),
                      pl.BlockSpec((tm, d), lambda i, *_: (i, 0))],
            out_specs=pl.BlockSpec(memory_space=pl.ANY),
            scratch_shapes=[pltpu.VMEM((2, lrows, d // 2), jnp.uint32),
                            pltpu.VMEM((EXPERT_BLOCK, d // 2), jnp.uint32), pltpu.SemaphoreType.DMA((2,))]),
        out_shape=jax.ShapeDtypeStruct((nblk * EXPERT_BLOCK, d // 2), jnp.uint32),
        compiler_params=_params("arbitrary"),
        name="dispatch",
    )(*tables, pos_et, ext_rows, h2)


def _expert_kernel(blk_ref, nused_ref, x_ref, wg_ref, wu_ref, wd_ref, y_ref, wgu_sc, wd_sc):
    i = pl.program_id(0)
    used = i < nused_ref[0]

    @pl.when(used & ((i == 0) | (blk_ref[i] != blk_ref[jnp.maximum(i - 1, 0)])))
    def _():
        wgu_sc[:, :EXPERT_FF] = wg_ref[0].astype(BF16)
        wgu_sc[:, EXPERT_FF:] = wu_ref[0].astype(BF16)
        wd_sc[...] = wd_ref[0].astype(BF16)

    @pl.when(used)
    def _():
        lo, hi = _unpack(x_ref[...])
        half = lo.shape[1]
        gu = (jnp.dot(lo, wgu_sc[:half, :], preferred_element_type=F32)
              + jnp.dot(hi, wgu_sc[half:, :], preferred_element_type=F32))
        a = _silu(gu[:, :EXPERT_FF]) * gu[:, EXPERT_FF:]
        y_ref[...] = _pack(_dot(a, wd_sc[...]))

    @pl.when(jnp.logical_not(used))
    def _():
        y_ref[...] = jnp.zeros(y_ref.shape, y_ref.dtype)


def _expert_call(blk_e, nused, xs, wg, wu, wd):
    rows, d = xs.shape
    nblk = rows // EXPERT_BLOCK

    def row_map(i, blk, nu):
        return (jnp.minimum(i, nu[0] - 1), 0)

    def of_expert(w):
        return pl.BlockSpec((1,) + w.shape[1:], lambda i, blk, nu: (blk[i], 0, 0))

    return pl.pallas_call(
        _expert_kernel,
        grid_spec=pltpu.PrefetchScalarGridSpec(
            num_scalar_prefetch=2, grid=(nblk,),
            in_specs=[pl.BlockSpec((EXPERT_BLOCK, d), row_map), of_expert(wg), of_expert(wu), of_expert(wd)],
            out_specs=pl.BlockSpec((EXPERT_BLOCK, d), lambda i, blk, nu: (i, 0)),
            scratch_shapes=[pltpu.VMEM((wg.shape[1], 2 * EXPERT_FF), BF16), pltpu.VMEM(wd.shape[1:], BF16)]),
        out_shape=jax.ShapeDtypeStruct((rows, d), jnp.uint32),
        compiler_params=_params("arbitrary"),
        name="expert",
    )(blk_e, nused, xs, wg, wu, wd)


def _combine_kernel(n8_ref, ls_ref, gs_ref, ys_hbm, pos_ref, w_ref, ext_ref, xm_ref, h_ref, mod_ref, wsgu_ref,
                    wsd_ref, fn_ref, o_ref, ybuf2, sems):
    step = pl.program_id(0)
    tm = xm_ref.shape[0]
    rows = ybuf2.shape[1]
    slot = step % 2
    ybuf, sem = ybuf2.at[slot], sems.at[slot]

    def fetch(tile, s):
        def body(e, c):
            i = tile * N_EXPERTS + e
            _run_copies(ybuf2.at[s], ys_hbm, sems.at[s], n8_ref[i], ls_ref[i], gs_ref[i], tm // RUN_ALIGN, False,
                        lambda cp: cp.start())
            return c
        lax.fori_loop(0, N_EXPERTS, body, 0)

    pl.when(step == 0)(lambda: fetch(step, slot))
    pl.when(step + 1 < pl.num_programs(0))(lambda: fetch(step + 1, 1 - slot))
    gu = _dot(h_ref[...], wsgu_ref[...])
    ff = gu.shape[1] // 2
    shared = _dot(_silu(gu[:, :ff]) * gu[:, ff:], wsd_ref[...])
    cid = lax.broadcasted_iota(jnp.int32, (1, rows), 1).astype(F32)
    start = ext_ref[0, :, 1:2]
    member = jnp.where((cid >= start) & (cid < ext_ref[0, :, 2:3]), 1.0, 0.0)
    offset = cid - jnp.sum(member * start, axis=0, keepdims=True)
    member = member.astype(BF16)
    pos = jnp.dot(pos_ref[...], member, preferred_element_type=F32)
    mix = jnp.where(pos == offset, jnp.dot(w_ref[...].astype(BF16), member, preferred_element_type=F32), 0.0)
    mix = mix.astype(BF16)
    last = step * N_EXPERTS + N_EXPERTS - 1
    filled = ls_ref[last] + RUN_ALIGN * n8_ref[last]
    _wait_rows(ybuf, ys_hbm, sem, filled // RUN_ALIGN, rows // RUN_ALIGN, False)
    rid = lax.broadcasted_iota(jnp.int32, (rows, 1), 0)
    lo, hi = _unpack(jnp.where(rid < filled, ybuf[...], jnp.uint32(0)))
    routed = jnp.concatenate([jnp.dot(mix, lo, preferred_element_type=F32),
                              jnp.dot(mix, hi, preferred_element_type=F32)], axis=1)
    x = xm_ref[...] + mod_ref[0, 5:6, :] * (routed + shared)
    o_ref[...] = _rms(x, fn_ref[...])


def _combine_call(tables, ys, pos_te, w_te, ext_cols, xm, h2, mod, wsgu, wsd, final_norm, tm, tiles_per_batch):
    t, d = xm.shape
    lrows = TOP_K * tm + N_EXPERTS * RUN_ALIGN
    tok = pl.BlockSpec((tm, d), lambda i, *_: (i, 0))
    per_e = pl.BlockSpec((tm, N_EXPERTS), lambda i, *_: (i, 0))
    return pl.pallas_call(
        _combine_kernel,
        grid_spec=pltpu.PrefetchScalarGridSpec(
            num_scalar_prefetch=len(tables), grid=(t // tm,),
            in_specs=[pl.BlockSpec(memory_space=pl.ANY), per_e, per_e,
                      pl.BlockSpec((1,) + ext_cols.shape[1:], lambda i, *_: (i, 0, 0)), tok, tok,
                      pl.BlockSpec((1, SUBLANES, d), lambda i, *_: (i // tiles_per_batch, 0, 0)),
                      _full(wsgu.shape), _full(wsd.shape), _full(final_norm.shape)],
            out_specs=tok,
            scratch_shapes=[pltpu.VMEM((2, lrows, d // 2), jnp.uint32), pltpu.SemaphoreType.DMA((2,))]),
        out_shape=jax.ShapeDtypeStruct((t, d), F32),
        compiler_params=_params("arbitrary"),
        name="combine",
    )(*tables, ys, pos_te, w_te, ext_cols, xm, h2, mod, wsgu, wsd, final_norm)


def _moe(xm, mod, norm_ffn, w_router, router_bias, wg, wu, wd, wsg, wsu, wsd, final_norm, tm):
    bsz, s, d = xm.shape
    t = bsz * s
    nt = t // tm
    perm = (np.arange(N_EXPERTS) % N_GROUPS) * GROUP_SIZE + np.arange(N_EXPERTS) // N_GROUPS
    wrt = w_router.T[perm]
    rbias = router_bias[perm][:, None]
    lower = jnp.asarray(perm[None, :] < perm[:, None], BF16)
    h2, w_et, pos_et, ext_cols, ext_rows = _route_call(xm, mod, norm_ffn, wrt, rbias, lower, tm, ROUTE_TILES)

    inv = np.argsort(perm)
    n8 = (ext_cols[:, :, 0].astype(jnp.int32)[:, inv] + (RUN_ALIGN - 1)) // RUN_ALIGN
    run = RUN_ALIGN * n8
    ls = jnp.cumsum(run, axis=1) - run
    tot = jnp.sum(run, axis=0)
    padded = (tot + EXPERT_BLOCK - 1) // EXPERT_BLOCK * EXPERT_BLOCK
    pad_end = jnp.cumsum(padded)
    gs = (pad_end - padded)[None, :] + jnp.cumsum(run, axis=0) - run
    nblk = -(-(t * TOP_K + nt * N_EXPERTS * (RUN_ALIGN - 1)) // EXPERT_BLOCK) + N_EXPERTS
    blk_first = jnp.arange(nblk, dtype=jnp.int32)[:, None] * EXPERT_BLOCK
    blk_e = jnp.minimum(jnp.sum((pad_end[None, :] <= blk_first).astype(jnp.int32), axis=1), N_EXPERTS - 1)
    nused = (pad_end[-1:] // EXPERT_BLOCK).astype(jnp.int32)
    tables = [a.reshape(-1).astype(jnp.int32) for a in (n8, ls, gs)]
    tails = [(pad_end - padded + tot).astype(jnp.int32), ((padded - tot) // RUN_ALIGN).astype(jnp.int32), nused]

    h2f = h2.reshape(t, d)
    xs = _dispatch_call(tables + tails, pos_et, ext_rows, h2f, nblk, tm)
    ys = _expert_call(blk_e, nused, xs, wg, wu, wd)
    wsgu = jnp.concatenate([wsg, wsu], axis=1).astype(BF16)
    out = _combine_call(tables, ys, pos_et.T, w_et.T, ext_cols, xm.reshape(t, d), h2f, mod, wsgu, wsd.astype(BF16),
                        final_norm, tm, s // tm)
    return out.reshape(bsz, s, d)


def _rope_tables(s):
    rows = s // GRID_W
    row = jnp.broadcast_to(jnp.arange(rows, dtype=F32)[:, None], (rows, GRID_W)).reshape(-1)
    col = jnp.broadcast_to(jnp.arange(GRID_W, dtype=F32)[None, :], (rows, GRID_W)).reshape(-1)
    half = QK_ROPE // 2
    inv_freq = ROPE_THETA ** (-jnp.arange(0, half, 2, dtype=F32) / half)
    ar, ac = row[:, None] * inv_freq, col[:, None] * inv_freq
    ones = jnp.ones((s, QK_NOPE), F32)
    tail = HEAD_PAD - QK_NOPE - QK_ROPE
    cos_t = jnp.concatenate([ones, jnp.cos(ar), jnp.cos(ar), jnp.cos(ac), jnp.cos(ac), jnp.ones((s, tail), F32)], 1)
    sin_t = jnp.concatenate([0 * ones, -jnp.sin(ar), jnp.sin(ar), -jnp.sin(ac), jnp.sin(ac),
                             jnp.zeros((s, tail), F32)], 1)
    return cos_t, sin_t


_Q4 = QK_ROPE // 4
ROPE_SWAP = np.concatenate([np.arange(_Q4, 2 * _Q4), np.arange(0, _Q4), np.arange(3 * _Q4, 4 * _Q4),
                            np.arange(2 * _Q4, 3 * _Q4)])


def _rope_slot(w, swap):
    if swap:
        w = w[..., ROPE_SWAP]
    pad = [(0, 0)] * (w.ndim - 1) + [(QK_NOPE, HEAD_PAD - QK_NOPE - QK_ROPE)]
    return jnp.pad(w, pad)


TILES = dict(inproj=512, tq=2048, tk=1408, fft_kb=8, merge=512, moe=256)


def kernel(x, c, ctx, c_ctx, w_mod, b_mod, norm_mix, norm_ffn, w_in, b_in, q_norm, w_uq, kv_norm, w_ukv, w_branch_attn, hy_conv_w, hy_conv_b, hy_filt_w1, hy_filt_b1, hy_filt_w2, hy_filt_b2, hy_filt_w3, hy_filt_freq, hy_skip, w_branch_hyena, w_out, w_router, router_bias, w_exp_gate, w_exp_up, w_exp_down, w_sh_gate, w_sh_up, w_sh_down, final_norm):
    bsz, s, d = x.shape
    tl = TILES
    assert w_mod.shape[0] == 1, "single-layer trunk"
    i = 0

    rows = -(-(bsz + 1) // SUBLANES) * SUBLANES
    c_rows = jnp.pad(jnp.concatenate([c, c_ctx[None]], axis=0), ((0, rows - bsz - 1), (0, 0)))
    mod_all = _mod_call(c_rows, w_mod[i], b_mod[i])
    mod_all = jnp.pad(mod_all.reshape(rows, 6, d), ((0, 0), (0, SUBLANES - 6), (0, 0)))
    mod, modc = mod_all[:bsz], mod_all[bsz:bsz + 1]

    cuts = np.cumsum([Q_LORA, KV_LORA, QK_ROPE, 3 * HY_WIDTH])
    wi, bi = w_in[i], b_in[i][None]
    w_q, w_kv, w_pe, w_hy, w_g = jnp.split(wi, cuts, axis=1)
    b_q, b_kv, b_pe, b_hy, b_g = jnp.split(bi, cuts, axis=1)
    wa = jnp.concatenate([w_q, w_kv, _rope_slot(w_pe, False), _rope_slot(w_pe, True)], axis=1).astype(BF16)
    ba = jnp.concatenate([b_q, b_kv, _rope_slot(b_pe, False), _rope_slot(b_pe, True)], axis=1)
    wq3 = w_uq[i].reshape(Q_LORA, N_HEADS, QK_NOPE + QK_ROPE) * (ATTN_SCALE * math.log2(math.e))
    tail = ((0, 0), (0, 0), (0, HEAD_PAD - QK_NOPE))
    wuq = (jnp.pad(wq3[..., :QK_NOPE], tail) + _rope_slot(wq3[..., QK_NOPE:], False)).reshape(Q_LORA, -1).astype(BF16)
    wuqs = _rope_slot(wq3[..., QK_NOPE:], True).reshape(Q_LORA, -1).astype(BF16)
    wkv3 = w_ukv[i].reshape(KV_LORA, N_HEADS, QK_NOPE + V_HEAD)
    wuk = jnp.pad(wkv3[..., :QK_NOPE], tail).reshape(KV_LORA, -1).astype(BF16)
    wuvt = wkv3[..., QK_NOPE:].reshape(KV_LORA, -1).T.astype(BF16)
    nm, qn, kvn = norm_mix[i][None], q_norm[i][None], kv_norm[i][None]

    w_c = jnp.concatenate([w_kv, _rope_slot(w_pe, False)], axis=1).astype(BF16)
    b_c = jnp.concatenate([b_kv, _rope_slot(b_pe, False)], axis=1)
    ck, cvt = _ctx_call(ctx, modc, nm, w_c, b_c, kvn, wuk, wuvt)

    cos_t, sin_t = _rope_tables(s)
    q, k, vt, hv, hx1, hx2, gate = _inproj_call(
        x, mod, nm, wa, ba, w_hy.astype(BF16), b_hy, w_g.astype(BF16), b_g, qn, wuq, wuqs, kvn, wuk, wuvt,
        cos_t, sin_t, hy_conv_w[i], hy_conv_b[i][None], tl["inproj"])

    attn = _attn_call(q, jnp.concatenate([ck, k], axis=2), jnp.concatenate([cvt, vt], axis=3), tl["tq"], tl["tk"])
    hy = _hyena(hv, hx1, hx2, hy_filt_w1[i], hy_filt_b1[i], hy_filt_w2[i], hy_filt_b2[i], hy_filt_w3[i],
                hy_filt_freq[i], hy_skip[i], tl["fft_kb"])
    xm = _merge_call(x, attn, hy, gate, mod, w_branch_attn[i].astype(BF16), w_branch_hyena[i].astype(BF16),
                     w_out[i].astype(BF16), tl["merge"])
    return _moe(xm, mod, norm_ffn[i][None], w_router[i], router_bias[i], w_exp_gate[i], w_exp_up[i], w_exp_down[i],
                w_sh_gate[i], w_sh_up[i], w_sh_down[i], final_norm[None], tl["moe"])
```

```python
import functools
import math

import numpy as np
import jax
import jax.numpy as jnp
from jax import lax
from jax.experimental import pallas as pl
from jax.experimental.pallas import tpu as pltpu

GRID_W = 64
N_HEADS = 8
QK_NOPE = 64
QK_ROPE = 32
V_HEAD = 64
Q_LORA = 256
KV_LORA = 128
ROPE_THETA = 10000.0
ATTN_SCALE = 1.0 / math.sqrt(QK_NOPE + QK_ROPE)
HY_WIDTH = 512
HY_ORDER = 2
HY_SHORT = 3
HY_BANDS = 8
HY_EMB = 1 + 2 * HY_BANDS
HY_EMB_PAD = 32
HY_FAST_DECAY = 0.3
HY_SLOW_DECAY = 1.5
HY_DECAY_TARGET = 1e-2
N_EXPERTS = 64
N_GROUPS = 8
GROUP_SIZE = N_EXPERTS // N_GROUPS
TOPK_GROUPS = 4
TOP_K = 8
EXPERT_FF = 256
ROUTE_SCALE = 2.5
EXPERT_BLOCK = 1024
RUN_ALIGN = 8
ROUTE_TILES = 8
LONG_RUN = 8
NORM_EPS = 1e-6

HEAD_PAD = 128
Q_CHUNK = 512
AHEAD = 2
LANES = 128
SUBLANES = 8
VMEM_LIMIT = 48 * 1024 * 1024

F32 = jnp.float32
BF16 = jnp.bfloat16
NT_DIMS = (((1,), (1,)), ((), ()))
NN_DIMS = (((1,), (0,)), ((), ()))


def _params(*sem):
    return pltpu.CompilerParams(dimension_semantics=sem, vmem_limit_bytes=VMEM_LIMIT)


def _dot(a, b):
    return jnp.dot(a.astype(BF16), b.astype(BF16), preferred_element_type=F32)


def _split(a):
    hi = a.astype(BF16)
    lo = (a - hi.astype(F32)).astype(BF16)
    return hi, lo


def _dot3(a, b, dims=NN_DIMS):
    ah, al = _split(a)
    bh, bl = _split(b)
    d = functools.partial(lax.dot_general, dimension_numbers=dims, preferred_element_type=F32)
    return d(ah, bh) + (d(ah, bl) + d(al, bh))


def _rms(x, g):
    return x * lax.rsqrt(jnp.mean(x * x, axis=-1, keepdims=True) + NORM_EPS) * g


def _silu(x):
    return x * jax.nn.sigmoid(x)


def _full(shape):
    nd = len(shape)
    return pl.BlockSpec(shape, lambda *_: (0,) * nd)


def _mod_kernel(c_ref, w_ref, b_ref, o_ref):
    o_ref[...] = _dot3(_silu(c_ref[...]), w_ref[...]) + b_ref[...]


def _mod_call(c_rows, w_mod, b_mod):
    r, d = c_rows.shape
    n = w_mod.shape[1]
    bn = 1024
    return pl.pallas_call(
        _mod_kernel,
        grid=(n // bn,),
        in_specs=[_full((r, d)), pl.BlockSpec((d, bn), lambda j: (0, j)), pl.BlockSpec((1, bn), lambda j: (0, j))],
        out_specs=pl.BlockSpec((r, bn), lambda j: (0, j)),
        out_shape=jax.ShapeDtypeStruct((r, n), F32),
        compiler_params=_params("arbitrary"),
        name="mod",
    )(c_rows, w_mod, b_mod.reshape(1, n))


def _prenorm(x, mod_ref, row, g):
    shift = mod_ref[0, row:row + 1, :]
    scale = mod_ref[0, row + 1:row + 2, :]
    return _rms(x, g) * (1.0 + scale) + shift


def _kv_heads(kv_lat, kpe, kvn_ref, wuk_ref, wuvt_ref, k_out, vt_out):
    kvn = _rms(kv_lat, kvn_ref[...]).astype(BF16)
    kk = _dot(kvn, wuk_ref[...])
    vt = lax.dot_general(wuvt_ref[...], kvn, NT_DIMS, preferred_element_type=F32)
    ones = jnp.ones((HEAD_PAD - V_HEAD, vt.shape[1]), F32)
    for h in range(N_HEADS):
        k_out[0, h] = (kk[:, HEAD_PAD * h:HEAD_PAD * (h + 1)] + kpe).astype(BF16)
        vt_out[0, h] = jnp.concatenate([vt[V_HEAD * h:V_HEAD * (h + 1)], ones], axis=0).astype(BF16)


def _ctx_kernel(c_ref, mod_ref, nm_ref, w_ref, b_ref, kvn_ref, wuk_ref, wuv_ref, k_out, v_out):
    h = _prenorm(c_ref[0], mod_ref, 0, nm_ref[...]).astype(BF16)
    a = _dot(h, w_ref[...]) + b_ref[...]
    _kv_heads(a[:, :KV_LORA], a[:, KV_LORA:], kvn_ref, wuk_ref, wuv_ref, k_out, v_out)


def _ctx_call(ctx, modc, norm_mix, w_c, b_c, kv_norm, w_uk, w_uv):
    bsz, n, d = ctx.shape
    return pl.pallas_call(
        _ctx_kernel,
        grid=(bsz,),
        in_specs=[pl.BlockSpec((1, n, d), lambda b: (b, 0, 0)), _full(modc.shape), _full(norm_mix.shape),
                  _full(w_c.shape), _full(b_c.shape), _full(kv_norm.shape), _full(w_uk.shape), _full(w_uv.shape)],
        out_specs=[pl.BlockSpec((1, N_HEADS, n, HEAD_PAD), lambda b: (b, 0, 0, 0)),
                   pl.BlockSpec((1, N_HEADS, HEAD_PAD, n), lambda b: (b, 0, 0, 0))],
        out_shape=[jax.ShapeDtypeStruct((bsz, N_HEADS, n, HEAD_PAD), BF16),
                   jax.ShapeDtypeStruct((bsz, N_HEADS, HEAD_PAD, n), BF16)],
        compiler_params=_params("arbitrary"),
        name="ctx",
    )(ctx, modc, norm_mix, w_c, b_c, kv_norm, w_uk, w_uv)


def _inproj_kernel(x_ref, xp_ref, xn_ref, mod_ref, nm_ref, wa_ref, ba_ref, why_ref, bhy_ref, wg_ref, bg_ref,
                   qn_ref, wuq_ref, wuqs_ref, kvn_ref, wuk_ref, wuv_ref, cos_ref, sin_ref, cw_ref, cb_ref,
                   q_out, k_out, v_out, hv_out, hx1_out, hx2_out, g_out):
    i = pl.program_id(0)
    tm = x_ref.shape[1]
    nm = nm_ref[...]
    h = _prenorm(x_ref[0], mod_ref, 0, nm).astype(BF16)
    a = _dot(h, wa_ref[...]) + ba_ref[...]
    q_lat = a[:, :Q_LORA]
    kv_lat = a[:, Q_LORA:Q_LORA + KV_LORA]
    kpe_m = a[:, Q_LORA + KV_LORA:Q_LORA + KV_LORA + HEAD_PAD]
    kpe_s = a[:, Q_LORA + KV_LORA + HEAD_PAD:]
    cos = cos_ref[...]
    sin = sin_ref[...]
    qn = _rms(q_lat, qn_ref[...]).astype(BF16)
    qa = _dot(qn, wuq_ref[...])
    qs = _dot(qn, wuqs_ref[...])
    for hh in range(N_HEADS):
        sl = slice(HEAD_PAD * hh, HEAD_PAD * (hh + 1))
        q_out[0, hh] = (qa[:, sl] * cos + qs[:, sl] * sin).astype(BF16)
    _kv_heads(kv_lat, kpe_m * cos + kpe_s * sin, kvn_ref, wuk_ref, wuv_ref, k_out, v_out)
    g_out[0] = (_dot(h, wg_ref[...]) + bg_ref[...]).astype(BF16)

    why = why_ref[...]
    bhy = bhy_ref[...]
    halo = jnp.concatenate([_prenorm(xp_ref[0], mod_ref, 0, nm), _prenorm(xn_ref[0], mod_ref, 0, nm)], axis=0)
    hy_all = _dot(jnp.concatenate([h, halo.astype(BF16)], axis=0), why) + bhy
    hy = hy_all[:tm]
    prev = jnp.where(i == 0, 0.0, hy_all[tm + SUBLANES - 1:tm + SUBLANES])
    nxt = jnp.where(i == pl.num_programs(0) - 1, 0.0, hy_all[tm + SUBLANES:tm + SUBLANES + 1])
    rid = lax.broadcasted_iota(jnp.int32, (tm, 1), 0)
    up = jnp.where(rid == 0, prev, pltpu.roll(hy, 1, 0))
    dn = jnp.where(rid == tm - 1, nxt, pltpu.roll(hy, tm - 1, 0))
    u = up * cw_ref[0:1, :] + hy * cw_ref[1:2, :] + dn * cw_ref[2:3, :] + cb_ref[...]
    hv_out[0] = u[:, :HY_WIDTH]
    hx1_out[0] = u[:, HY_WIDTH:2 * HY_WIDTH]
    hx2_out[0] = u[:, 2 * HY_WIDTH:]


def _inproj_call(x, mod, norm_mix, wa, ba, why, bhy, wg, bg, q_norm, wuq, wuqs, kv_norm, wuk, wuvt, cos_t, sin_t, cw,
                 cb, tm):
    bsz, s, d = x.shape
    nt = s // tm
    rb = tm // SUBLANES
    last_rb = s // SUBLANES - 1
    consts = [norm_mix, wa, ba, why, bhy, wg, bg, q_norm, wuq, wuqs, kv_norm, wuk, wuvt]
    in_specs = [
        pl.BlockSpec((1, tm, d), lambda i, b: (b, i, 0)),
        pl.BlockSpec((1, SUBLANES, d), lambda i, b: (b, jnp.maximum(i * rb - 1, 0), 0)),
        pl.BlockSpec((1, SUBLANES, d), lambda i, b: (b, jnp.minimum((i + 1) * rb, last_rb), 0)),
        pl.BlockSpec((1, SUBLANES, d), lambda i, b: (b, 0, 0)),
    ] + [_full(c.shape) for c in consts] + [
        pl.BlockSpec((tm, HEAD_PAD), lambda i, b: (i, 0)),
        pl.BlockSpec((tm, HEAD_PAD), lambda i, b: (i, 0)),
        _full(cw.shape), _full(cb.shape),
    ]
    hw = HY_WIDTH
    out_specs = [
        pl.BlockSpec((1, N_HEADS, tm, HEAD_PAD), lambda i, b: (b, 0, i, 0)),
        pl.BlockSpec((1, N_HEADS, tm, HEAD_PAD), lambda i, b: (b, 0, i, 0)),
        pl.BlockSpec((1, N_HEADS, HEAD_PAD, tm), lambda i, b: (b, 0, 0, i)),
        pl.BlockSpec((1, tm, hw), lambda i, b: (b, i, 0)),
        pl.BlockSpec((1, tm, hw), lambda i, b: (b, i, 0)),
        pl.BlockSpec((1, tm, hw), lambda i, b: (b, i, 0)),
        pl.BlockSpec((1, tm, 2 * d), lambda i, b: (b, i, 0)),
    ]
    out_shape = [
        jax.ShapeDtypeStruct((bsz, N_HEADS, s, HEAD_PAD), BF16),
        jax.ShapeDtypeStruct((bsz, N_HEADS, s, HEAD_PAD), BF16),
        jax.ShapeDtypeStruct((bsz, N_HEADS, HEAD_PAD, s), BF16),
        jax.ShapeDtypeStruct((bsz, s, hw), F32),
        jax.ShapeDtypeStruct((bsz, s, hw), F32),
        jax.ShapeDtypeStruct((bsz, s, hw), F32),
        jax.ShapeDtypeStruct((bsz, s, 2 * d), BF16),
    ]
    return pl.pallas_call(
        _inproj_kernel,
        grid=(nt, bsz),
        in_specs=in_specs,
        out_specs=out_specs,
        out_shape=out_shape,
        compiler_params=_params("arbitrary", "arbitrary"),
        name="inproj",
    )(x, x, x, mod, *consts, cos_t, sin_t, cw, cb)


def _attn_kernel(q_ref, k_ref, vt_ref, o_ref, m_sc, acc_sc):
    j = pl.program_id(2)

    @pl.when(j == 0)
    def _():
        m_sc[...] = jnp.full(m_sc.shape, -jnp.inf, F32)
        acc_sc[...] = jnp.zeros(acc_sc.shape, F32)

    tq = q_ref.shape[2]
    qw = min(tq, Q_CHUNK)
    units = [(h, c) for h in range(N_HEADS) for c in range(0, tq, qw)]

    def scores(u):
        h, c = units[u]
        return lax.dot_general(k_ref[0, h], q_ref[0, h, c:c + qw, :], NT_DIMS,
                               preferred_element_type=F32)

    pending = [scores(u) for u in range(AHEAD)]
    for u, (h, c) in enumerate(units):
        if u + AHEAD < len(units):
            pending.append(scores(u + AHEAD))
        st = pending.pop(0)
        m_prev = m_sc[h, :, c:c + qw]
        m_new = jnp.maximum(m_prev, jnp.max(st, axis=0, keepdims=True))
        pt = jnp.exp2(st - m_new).astype(BF16)
        acc_sc[h, :, c:c + qw] = (jnp.exp2(m_prev - m_new) * acc_sc[h, :, c:c + qw]
                                  + jnp.dot(vt_ref[0, h], pt, preferred_element_type=F32))
        m_sc[h, :, c:c + qw] = m_new

    @pl.when(j == pl.num_programs(2) - 1)
    def _():
        ot = jnp.concatenate([acc_sc[h, :V_HEAD] / acc_sc[h, V_HEAD:V_HEAD + 1] for h in range(N_HEADS)], axis=0)
        o_ref[0] = ot.T.astype(o_ref.dtype)


def _attn_call(q, k, vt, tq, tk):
    bsz, nh, s, dh = q.shape
    nk = k.shape[2]
    dv = nh * V_HEAD
    return pl.pallas_call(
        _attn_kernel,
        grid=(bsz, s // tq, nk // tk),
        in_specs=[
            pl.BlockSpec((1, nh, tq, dh), lambda b, i, j: (b, 0, i, 0)),
            pl.BlockSpec((1, nh, tk, dh), lambda b, i, j: (b, 0, j, 0)),
            pl.BlockSpec((1, nh, dh, tk), lambda b, i, j: (b, 0, 0, j)),
        ],
        out_specs=pl.BlockSpec((1, tq, dv), lambda b, i, j: (b, i, 0)),
        out_shape=jax.ShapeDtypeStruct((bsz, s, dv), BF16),
        scratch_shapes=[pltpu.VMEM((nh, 1, tq), F32), pltpu.VMEM((nh, dh, tq), F32)],
        compiler_params=_params("arbitrary", "arbitrary", "arbitrary"),
        name="attn",
    )(q, k, vt)


def _filter_kernel(emb_ref, w1_ref, b1_ref, w2_ref, b2_ref, w3_ref, fr_ref, dl_ref, full_out, asum_out, *, seq):
    r = pl.program_id(0)
    rb = emb_ref.shape[0]
    emb = emb_ref[...]
    fr = fr_ref[...]
    h = jnp.sin(fr * (_dot3(emb, w1_ref[...]) + b1_ref[...]))
    h = jnp.sin(fr * (_dot3(h, w2_ref[...]) + b2_ref[...]))
    k = _dot3(h, w3_ref[0]) * jnp.exp(-emb[:, 0:1] * dl_ref[...])
    row = r * rb + lax.broadcasted_iota(jnp.int32, (rb, 1), 0)
    k = jnp.where(row == seq, 0.0, k)
    full_out[...] = k

    @pl.when(r == 0)
    def _():
        asum_out[...] = jnp.zeros(asum_out.shape, F32)

    asum_out[...] += jnp.sum(jnp.abs(k), axis=0, keepdims=True)


def _filter_call(emb, w1, b1, w2, b2, w3sel, freq, deltas2, seq, rb):
    n2 = emb.shape[0]
    half_blocks = seq // rb
    width = w3sel.shape[2]
    return pl.pallas_call(
        functools.partial(_filter_kernel, seq=seq),
        grid=(n2 // rb,),
        in_specs=[pl.BlockSpec((rb, HY_EMB_PAD), lambda r: (r, 0)), _full(w1.shape), _full(b1.shape),
                  _full(w2.shape), _full(b2.shape),
                  pl.BlockSpec((1,) + w3sel.shape[1:], lambda r: (r // half_blocks, 0, 0)),
                  _full(freq.shape), _full(deltas2.shape)],
        out_specs=[pl.BlockSpec((rb, width), lambda r: (r, 0)), pl.BlockSpec((1, width), lambda r: (0, 0))],
        out_shape=[jax.ShapeDtypeStruct((n2, width), F32), jax.ShapeDtypeStruct((1, width), F32)],
        compiler_params=_params("arbitrary"),
        name="filt",
    )(emb, w1, b1, w2, b2, w3sel, freq, deltas2)


def _fa_kernel(u_ref, f_ref, a_out):
    two, _, hn, g, c = u_ref.shape
    a = _dot(f_ref[...], u_ref[...].reshape(two * hn * g, c))
    a_out[...] = (_pack(a) if a_out.dtype == jnp.uint32 else a).reshape(a_out.shape)


def _fa_call(u5, fmat, packed):
    _, p, hn, n, c = u5.shape
    g = SUBLANES
    co, dt = (c // 2, jnp.uint32) if packed else (c, F32)
    return pl.pallas_call(
        _fa_kernel,
        grid=(p, n // g),
        in_specs=[pl.BlockSpec((2, 1, hn, g, c), lambda q, j: (0, q, 0, j, 0)), _full(fmat.shape)],
        out_specs=pl.BlockSpec((1, 2, n, g, co), lambda q, j: (q, 0, 0, j, 0)),
        out_shape=jax.ShapeDtypeStruct((p, 2, n, n, co), dt),
        compiler_params=_params("arbitrary", "arbitrary"),
        name="fa",
    )(u5, fmat)


def _dot_packed(w, u):
    lo, hi = _unpack(u)
    return jnp.concatenate([jnp.dot(w, lo, preferred_element_type=F32), jnp.dot(w, hi, preferred_element_type=F32)],
                           axis=1)


def _fb_kernel(a_ref, g_ref, asum_ref, kf_out):
    _, two, kb, n, c = a_ref.shape
    scale = 1.0 / (asum_ref[...] + 1e-6)
    for kk in range(kb):
        x = _dot(g_ref[kk], a_ref[0, :, kk].reshape(two * n, c)) * scale
        kf_out[kk] = x.reshape(two, n, c)


def _fb_call(a5, gmat, asum, kb):
    _, _, n, _, c = a5.shape
    return pl.pallas_call(
        _fb_kernel,
        grid=(n // kb,),
        in_specs=[pl.BlockSpec((1, 2, kb, n, c), lambda k: (0, 0, k, 0, 0)),
                  pl.BlockSpec((kb, 2 * n, 2 * n), lambda k: (k, 0, 0)), _full(asum.shape)],
        out_specs=pl.BlockSpec((kb, 2, n, c), lambda k: (k, 0, 0, 0)),
        out_shape=jax.ShapeDtypeStruct((n, 2, n, c), F32),
        compiler_params=_params("arbitrary"),
        name="fb",
    )(a5, gmat, asum)


def _mid_kernel(a_ref, g_ref, h_ref, kf_ref, b_out):
    _, two, kb, n, c = a_ref.shape
    for kk in range(kb):
        x = _dot_packed(g_ref[kk], a_ref[0, :, kk].reshape(two * n, c))
        xr, xi = x[:n], x[n:]
        kr, ki = kf_ref[kk, 0], kf_ref[kk, 1]
        y = jnp.concatenate([xr * kr - xi * ki, xr * ki + xi * kr], axis=0)
        b_out[0, :, kk] = _pack(_dot(h_ref[kk], y)).reshape(two, n, c)


def _mid_call(a5, gmat, hmat, kf, order, kb):
    p, _, n, _, c = a5.shape
    return pl.pallas_call(
        _mid_kernel,
        grid=(n // kb, p),
        in_specs=[pl.BlockSpec((1, 2, kb, n, c), lambda k, q: (q, 0, k, 0, 0)),
                  pl.BlockSpec((kb, 2 * n, 2 * n), lambda k, q: (k, 0, 0)),
                  pl.BlockSpec((kb, 2 * n, 2 * n), lambda k, q: (k, 0, 0)),
                  pl.BlockSpec((kb, 2, n, 2 * c), lambda k, q: (k, 0, 0, order))],
        out_specs=pl.BlockSpec((1, 2, kb, n, c), lambda k, q: (q, 0, k, 0, 0)),
        out_shape=jax.ShapeDtypeStruct(a5.shape, jnp.uint32),
        compiler_params=_params("arbitrary", "arbitrary"),
        name="mid",
    )(a5, gmat, hmat, kf)


def _fc_kernel(b_ref, f_ref, u_ref, m_ref, skip_ref, o_out):
    _, two, n, g, c = b_ref.shape
    y = _dot_packed(f_ref[...], b_ref[...].reshape(two * n * g, c)).reshape(u_ref.shape)
    o_out[...] = m_ref[...] * (y + u_ref[...] * skip_ref[...])


def _fc_call(b5, finv, u5, m5, skip_row):
    _, p, hn, n, c = u5.shape
    g = SUBLANES
    blk = pl.BlockSpec((2, 1, hn, g, c), lambda q, j: (0, q, 0, j, 0))
    return pl.pallas_call(
        _fc_kernel,
        grid=(p, n // g),
        in_specs=[pl.BlockSpec((1, 2, n, g, c // 2), lambda q, j: (q, 0, 0, j, 0)), _full(finv.shape), blk, blk,
                  _full(skip_row.shape)],
        out_specs=blk,
        out_shape=jax.ShapeDtypeStruct(u5.shape, F32),
        compiler_params=_params("arbitrary", "arbitrary"),
        name="fc",
    )(b5, finv, u5, m5, skip_row)


def _dft_tables(n):
    hn = n // 2
    k = np.arange(n)[:, None]
    ang = -2.0 * np.pi * (k * np.arange(n)[None, :] % n) / n
    fr, fi = np.cos(ang), np.sin(ang)
    f_data = np.block([[fr[:, :hn], -fi[:, :hn]], [fi[:, :hn], fr[:, :hn]]])
    f_filt = np.concatenate([fr, fi], axis=0)
    er, ei = fr[:hn], -fi[:hn]
    f_inv = np.block([[er, -ei], [ei, er]]) / float(n * n)
    k1 = jnp.arange(n, dtype=jnp.int32)[:, None, None]
    k2 = jnp.arange(n, dtype=jnp.int32)[None, :, None]
    m2 = jnp.arange(n, dtype=jnp.int32)[None, None, :]
    idx = (m2 * (k1 + n * k2)) % (n * n)
    ang2 = idx.astype(F32) * (-2.0 * math.pi / (n * n))
    gr, gi = jnp.cos(ang2), jnp.sin(ang2)
    g = jnp.concatenate([jnp.concatenate([gr, -gi], axis=2), jnp.concatenate([gi, gr], axis=2)], axis=1)
    h = jnp.swapaxes(g, 1, 2)

    def widen(f):
        return jnp.asarray(np.kron(f, np.eye(SUBLANES)), BF16)

    return widen(f_data), widen(f_filt), widen(f_inv), g.astype(BF16), h.astype(BF16)


def _hyena_filter_tables(seq):
    f32 = np.float32
    t = np.linspace(0.0, 1.0, seq, dtype=f32)[:, None]
    w = (f32(2.0 * math.pi) * np.arange(seq, dtype=f32)[:, None] / f32(seq)).astype(f32)
    f = np.linspace(1e-4, HY_BANDS - 1, HY_BANDS, dtype=f32)[None, :]
    emb = np.concatenate([t, np.cos(f * w), -np.sin(f * w)], axis=-1).astype(f32)
    emb = np.concatenate([emb, emb[:1], emb[:0:-1]], axis=0)
    emb = np.pad(emb, ((0, 0), (0, HY_EMB_PAD - HY_EMB)))
    deltas = np.abs(np.linspace(math.log(HY_DECAY_TARGET) / HY_SLOW_DECAY,
                                math.log(HY_DECAY_TARGET) / HY_FAST_DECAY, HY_WIDTH, dtype=f32))
    return jnp.asarray(emb), jnp.asarray(np.tile(deltas, HY_ORDER)[None, :])


def _hyena(hv, hx1, hx2, w1, b1, w2, b2, w3, freq, skip, kb):
    bsz, seq, c = hv.shape
    n = int(round(math.sqrt(2 * seq)))
    assert n * n == 2 * seq and bsz % 2 == 0
    hn, p = n // 2, bsz // 2
    f_data, f_filt, f_inv, gmat, hmat = _dft_tables(n)

    emb, deltas2 = _hyena_filter_tables(seq)
    w1p = jnp.pad(w1, ((0, HY_EMB_PAD - HY_EMB), (0, 0)))
    w3r = w3.reshape(w3.shape[0], HY_ORDER, 2, c)
    w3sel = jnp.stack([w3r[:, :, 0, :].reshape(-1, HY_ORDER * c), w3r[:, :, 1, :].reshape(-1, HY_ORDER * c)])
    full, asum = _filter_call(emb, w1p, b1[None], w2, b2[None], w3sel, freq[None], deltas2, seq, min(512, seq))
    c2 = HY_ORDER * c
    kf = _fb_call(_fa_call(full.reshape(2, 1, hn, n, c2), f_filt, False), gmat, asum, kb // HY_ORDER)

    def view(t):
        return t.reshape(2, p, hn, n, c)

    def long_conv(u5, m5, order):
        bm = _mid_call(_fa_call(u5, f_data, True), gmat, hmat, kf, order, kb)
        return _fc_call(bm, f_inv, u5, m5, skip[order][None, :])

    z = long_conv(view(hv), view(hx1), 0)
    return long_conv(z, view(hx2), 1).reshape(bsz, seq, c)


def _merge_kernel(x_ref, at_ref, hy_ref, g_ref, mod_ref, wba_ref, wbh_ref, wo_ref, o_ref):
    d = x_ref.shape[2]
    g = g_ref[0].astype(F32)
    y = (jax.nn.sigmoid(g[:, :d]) * _dot(at_ref[0], wba_ref[...])
         + jax.nn.sigmoid(g[:, d:]) * _dot(hy_ref[0], wbh_ref[...]))
    o_ref[0] = x_ref[0] + mod_ref[0, 2:3, :] * _dot(y, wo_ref[...])


def _merge_call(x, attn, hy, gate, mod, wba, wbh, wo, tm):
    bsz, s, d = x.shape

    def tok(w):
        return pl.BlockSpec((1, tm, w), lambda b, i: (b, i, 0))

    return pl.pallas_call(
        _merge_kernel,
        grid=(bsz, s // tm),
        in_specs=[tok(d), tok(attn.shape[2]), tok(hy.shape[2]), tok(2 * d),
                  pl.BlockSpec((1, SUBLANES, d), lambda b, i: (b, 0, 0)),
                  _full(wba.shape), _full(wbh.shape), _full(wo.shape)],
        out_specs=tok(d),
        out_shape=jax.ShapeDtypeStruct((bsz, s, d), F32),
        compiler_params=_params("arbitrary", "arbitrary"),
        name="merge",
    )(x, attn, hy, gate, mod, wba, wbh, wo)


def _route_kernel(xm_ref, mod_ref, nf_ref, wrt_ref, rb_ref, tri_ref, lt_ref, h2_out, w_out, p_out, col_out, row_out):
    tm = xm_ref.shape[1]
    ng, gs = N_GROUPS, GROUP_SIZE

    h2 = _prenorm(xm_ref[0], mod_ref, 3, nf_ref[...])
    h2_out[0] = h2.astype(h2_out.dtype)
    scores = jax.nn.sigmoid(_dot3(wrt_ref[...], h2, NT_DIMS))
    sel = scores + rb_ref[...]
    slabs = [sel[ng * j:ng * (j + 1)] for j in range(gs)]

    top1 = jnp.full((ng, tm), -jnp.inf, F32)
    top2 = top1
    for x in slabs:
        top2 = jnp.maximum(top2, jnp.minimum(top1, x))
        top1 = jnp.maximum(top1, x)
    gscore = top1 + top2
    gid = lax.broadcasted_iota(jnp.int32, (ng, 1), 0)
    rank = jnp.zeros((ng, tm), jnp.int32)
    for g2 in range(ng):
        row = gscore[g2:g2 + 1]
        beats = (row > gscore) | ((row == gscore) & (g2 < gid))
        rank = rank + beats.astype(jnp.int32)
    gmask = rank < TOPK_GROUPS

    cand = [jnp.where(gmask, x, -jnp.inf) for x in slabs]
    eid = [gid * gs + j for j in range(gs)]
    chosen = []
    for _ in range(TOP_K):
        best = functools.reduce(jnp.maximum, cand)
        best = jnp.max(best, axis=0, keepdims=True)
        idx = functools.reduce(jnp.minimum, [jnp.where(cand[j] == best, eid[j], N_EXPERTS) for j in range(gs)])
        idx = jnp.min(idx, axis=0, keepdims=True)
        chosen.append(idx)
        cand = [jnp.where(eid[j] == idx, -jnp.inf, cand[j]) for j in range(gs)]

    mask = [functools.reduce(jnp.logical_or, [eid[j] == idx for idx in chosen]) for j in range(gs)]
    maskb = jnp.concatenate(mask, axis=0)
    wsel = jnp.where(maskb, scores, 0.0)
    w_out[...] = wsel / jnp.sum(wsel, axis=0, keepdims=True) * ROUTE_SCALE

    def extents(cnt, lower_sum):
        units = jnp.floor((cnt + (RUN_ALIGN - 1)) * (1.0 / RUN_ALIGN))
        start = RUN_ALIGN * lower_sum(units.astype(BF16))
        return start, start + RUN_ALIGN * units

    lane = lax.broadcasted_iota(jnp.int32, (N_EXPERTS, LANES), 1)
    sub = lax.broadcasted_iota(jnp.int32, (SUBLANES, N_EXPERTS), 0)
    ts = tri_ref.shape[0]
    for c in range(tm // ts):
        mb = maskb[:, c * ts:(c + 1) * ts]
        maskf = jnp.where(mb, 1.0, 0.0)
        mask16 = maskf.astype(BF16)
        before = jnp.dot(mask16, tri_ref[...], preferred_element_type=F32)
        p_out[:, c * ts:(c + 1) * ts] = jnp.where(mb, before, -1.0).astype(p_out.dtype)
        cnt_c = jnp.sum(maskf, axis=1, keepdims=True)
        start_c, end_c = extents(jnp.broadcast_to(cnt_c, (N_EXPERTS, LANES)),
                                 lambda u: jnp.dot(lt_ref[...], u, preferred_element_type=F32))
        col_out[c] = jnp.where(lane == 0, cnt_c, jnp.where(lane == 1, start_c, end_c))
        cnt_r = lax.dot_general(jnp.ones((SUBLANES, ts), BF16), mask16, NT_DIMS, preferred_element_type=F32)
        start_r, end_r = extents(cnt_r,
                                 lambda u: lax.dot_general(u, lt_ref[...], NT_DIMS, preferred_element_type=F32))
        row_out[c] = jnp.where(sub == 0, start_r, end_r)


def _route_call(xm, mod, norm_ffn, wrt, rbias, lower, ts, tiles_per_step):
    bsz, s, d = xm.shape
    t = bsz * s
    tm = ts * tiles_per_step
    nt = s // tm
    tri = (jnp.arange(ts)[:, None] < jnp.arange(ts)[None, :]).astype(BF16)
    tok = pl.BlockSpec((N_EXPERTS, tm), lambda i: (0, i))
    return pl.pallas_call(
        _route_kernel,
        grid=(t // tm,),
        in_specs=[pl.BlockSpec((1, tm, d), lambda i: (i // nt, i % nt, 0)),
                  pl.BlockSpec((1, SUBLANES, d), lambda i: (i // nt, 0, 0)),
                  _full(norm_ffn.shape), _full(wrt.shape), _full(rbias.shape), _full(tri.shape),
                  _full(lower.shape)],
        out_specs=[pl.BlockSpec((1, tm, d), lambda i: (i // nt, i % nt, 0)), tok, tok,
                   pl.BlockSpec((tiles_per_step, N_EXPERTS, LANES), lambda i: (i, 0, 0)),
                   pl.BlockSpec((tiles_per_step, SUBLANES, N_EXPERTS), lambda i: (i, 0, 0))],
        out_shape=[jax.ShapeDtypeStruct((bsz, s, d), BF16), jax.ShapeDtypeStruct((N_EXPERTS, t), F32),
                   jax.ShapeDtypeStruct((N_EXPERTS, t), BF16),
                   jax.ShapeDtypeStruct((t // ts, N_EXPERTS, LANES), F32),
                   jax.ShapeDtypeStruct((t // ts, SUBLANES, N_EXPERTS), F32)],
        compiler_params=_params("arbitrary"),
        name="route",
    )(xm, mod, norm_ffn, wrt, rbias, tri, lower)


def _pack(x):
    w = x.shape[1] // 2
    lo = lax.bitcast_convert_type(x[:, :w].astype(BF16).astype(F32), jnp.uint32)
    hi = lax.bitcast_convert_type(x[:, w:].astype(BF16).astype(F32), jnp.uint32)
    return hi | (lo >> 16)


def _unpack(u):
    lo = lax.bitcast_convert_type(u << 16, F32).astype(BF16)
    hi = lax.bitcast_convert_type(u & jnp.uint32(0xFFFF0000), F32).astype(BF16)
    return lo, hi


def _pow2_pieces(units, limit):
    bit = 1
    while bit * 2 <= limit:
        bit *= 2
    while bit:
        yield (units & bit) != 0, units & ~(2 * bit - 1), bit
        bit //= 2


def _rows_copy(vm_ref, hbm_ref, sem, vm_row, hbm_row, rows, to_hbm):
    v = vm_ref.at[pl.ds(pl.multiple_of(vm_row, RUN_ALIGN), rows), :]
    h = hbm_ref.at[pl.ds(pl.multiple_of(hbm_row, RUN_ALIGN), rows), :]
    return pltpu.make_async_copy(v, h, sem) if to_hbm else pltpu.make_async_copy(h, v, sem)


def _run_copies(vm_ref, hbm_ref, sem, n8, vm_row, hbm_row, limit, to_hbm, act):
    def emit(pieces):
        for on, off, size in pieces:
            @pl.when(on)
            def _():
                act(_rows_copy(vm_ref, hbm_ref, sem, vm_row + RUN_ALIGN * off, hbm_row + RUN_ALIGN * off,
                               RUN_ALIGN * size, to_hbm))

    pieces = list(_pow2_pieces(n8, limit))
    long_pieces = [p for p in pieces if p[2] >= LONG_RUN]
    if long_pieces:
        pl.when(n8 >= LONG_RUN)(lambda: emit(long_pieces))
    emit([p for p in pieces if p[2] < LONG_RUN])


def _wait_rows(vm_ref, hbm_ref, sem, units, limit, to_hbm):
    for on, _, size in _pow2_pieces(units, limit):
        @pl.when(on)
        def _():
            _rows_copy(vm_ref, hbm_ref, sem, 0, 0, RUN_ALIGN * size, to_hbm).wait()


def _dispatch_kernel(n8_ref, ls_ref, gs_ref, ts_ref, t8_ref, nu_ref, pos_ref, ext_ref, h_ref, xs_out, srt2, zbuf,
                     sems):
    step = pl.program_id(0)
    tm = h_ref.shape[0]
    rows = srt2.shape[1]
    slot = step % 2
    srt, sem = srt2.at[slot], sems.at[slot]
    rid = lax.broadcasted_iota(jnp.int32, (rows, 1), 0).astype(F32)
    start = ext_ref[0, 0:1, :]
    member = jnp.where((rid >= start) & (rid < ext_ref[0, 1:2, :]), 1.0, 0.0)
    offset = rid - jnp.sum(member * start, axis=1, keepdims=True)
    pos = jnp.dot(member.astype(BF16), pos_ref[...], preferred_element_type=F32)
    sel = jnp.where(pos == offset, 1.0, 0.0).astype(BF16)
    srt[...] = _pack(jnp.dot(sel, h_ref[...], preferred_element_type=F32))

    def send(e, c):
        i = step * N_EXPERTS + e
        _run_copies(srt, xs_out, sem, n8_ref[i], ls_ref[i], gs_ref[i], tm // RUN_ALIGN, True, lambda cp: cp.start())
        return c

    lax.fori_loop(0, N_EXPERTS, send, 0)

    def wait_tile(tile, s):
        last = tile * N_EXPERTS + N_EXPERTS - 1
        _wait_rows(srt2.at[s], xs_out, sems.at[s], ls_ref[last] // RUN_ALIGN + n8_ref[last], rows // RUN_ALIGN, True)

    pl.when(step > 0)(lambda: wait_tile(step - 1, 1 - slot))

    @pl.when(step == pl.num_programs(0) - 1)
    def _():
        wait_tile(step, slot)
        zbuf[...] = jnp.zeros(zbuf.shape, zbuf.dtype)
        nblk = xs_out.shape[0] // EXPERT_BLOCK

        def fill(act):
            def tails(e, c):
                _run_copies(zbuf, xs_out, sem, t8_ref[e], 0, ts_ref[e], EXPERT_BLOCK // RUN_ALIGN - 1, True, act)
                return c

            def blocks(b, c):
                act(pltpu.make_async_copy(
                    zbuf, xs_out.at[pl.ds(pl.multiple_of(b * EXPERT_BLOCK, EXPERT_BLOCK), EXPERT_BLOCK), :], sem))
                return c

            lax.fori_loop(0, N_EXPERTS, tails, 0)
            lax.fori_loop(nu_ref[0], nblk, blocks, 0)

        fill(lambda cp: cp.start())
        fill(lambda cp: cp.wait())


def _dispatch_call(tables, pos_et, ext_rows, h2, nblk, tm):
    t, d = h2.shape
    lrows = TOP_K * tm + N_EXPERTS * RUN_ALIGN
    return pl.pallas_call(
        _dispatch_kernel,
        grid_spec=pltpu.PrefetchScalarGridSpec(
            num_scalar_prefetch=len(tables), grid=(t // tm,),
            in_specs=[pl.BlockSpec((N_EXPERTS, tm), lambda i, *_: (0, i)),
                      pl.BlockSpec((1,) + ext_rows.shape[1:], lambda i, *_: (i, 0, 0)),
                      pl.BlockSpec((tm, d), lambda i, *_: (i, 0))],
            out_specs=pl.BlockSpec(memory_space=pl.ANY),
            scratch_shapes=[pltpu.VMEM((2, lrows, d // 2), jnp.uint32),
                            pltpu.VMEM((EXPERT_BLOCK, d // 2), jnp.uint32), pltpu.SemaphoreType.DMA((2,))]),
        out_shape=jax.ShapeDtypeStruct((nblk * EXPERT_BLOCK, d // 2), jnp.uint32),
        compiler_params=_params("arbitrary"),
        name="dispatch",
    )(*tables, pos_et, ext_rows, h2)


def _expert_kernel(blk_ref, nused_ref, x_ref, wg_ref, wu_ref, wd_ref, y_ref, wgu_sc, wd_sc):
    i = pl.program_id(0)
    used = i < nused_ref[0]

    @pl.when(used & ((i == 0) | (blk_ref[i] != blk_ref[jnp.maximum(i - 1, 0)])))
    def _():
        wgu_sc[:, :EXPERT_FF] = wg_ref[0].astype(BF16)
        wgu_sc[:, EXPERT_FF:] = wu_ref[0].astype(BF16)
        wd_sc[...] = wd_ref[0].astype(BF16)

    @pl.when(used)
    def _():
        lo, hi = _unpack(x_ref[...])
        half = lo.shape[1]
        gu = (jnp.dot(lo, wgu_sc[:half, :], preferred_element_type=F32)
              + jnp.dot(hi, wgu_sc[half:, :], preferred_element_type=F32))
        a = _silu(gu[:, :EXPERT_FF]) * gu[:, EXPERT_FF:]
        y_ref[...] = _pack(_dot(a, wd_sc[...]))

    @pl.when(jnp.logical_not(used))
    def _():
        y_ref[...] = jnp.zeros(y_ref.shape, y_ref.dtype)


def _expert_call(blk_e, nused, xs, wg, wu, wd):
    rows, d = xs.shape
    nblk = rows // EXPERT_BLOCK

    def row_map(i, blk, nu):
        return (jnp.minimum(i, nu[0] - 1), 0)

    def of_expert(w):
        return pl.BlockSpec((1,) + w.shape[1:], lambda i, blk, nu: (blk[i], 0, 0))

    return pl.pallas_call(
        _expert_kernel,
        grid_spec=pltpu.PrefetchScalarGridSpec(
            num_scalar_prefetch=2, grid=(nblk,),
            in_specs=[pl.BlockSpec((EXPERT_BLOCK, d), row_map), of_expert(wg), of_expert(wu), of_expert(wd)],
            out_specs=pl.BlockSpec((EXPERT_BLOCK, d), lambda i, blk, nu: (i, 0)),
            scratch_shapes=[pltpu.VMEM((wg.shape[1], 2 * EXPERT_FF), BF16), pltpu.VMEM(wd.shape[1:], BF16)]),
        out_shape=jax.ShapeDtypeStruct((rows, d), jnp.uint32),
        compiler_params=_params("arbitrary"),
        name="expert",
    )(blk_e, nused, xs, wg, wu, wd)


def _combine_kernel(n8_ref, ls_ref, gs_ref, ys_hbm, pos_ref, w_ref, ext_ref, xm_ref, h_ref, mod_ref, wsgu_ref,
                    wsd_ref, fn_ref, o_ref, ybuf2, sems):
    step = pl.program_id(0)
    tm = xm_ref.shape[0]
    rows = ybuf2.shape[1]
    slot = step % 2
    ybuf, sem = ybuf2.at[slot], sems.at[slot]

    def fetch(tile, s):
        def body(e, c):
            i = tile * N_EXPERTS + e
            _run_copies(ybuf2.at[s], ys_hbm, sems.at[s], n8_ref[i], ls_ref[i], gs_ref[i], tm // RUN_ALIGN, False,
                        lambda cp: cp.start())
            return c
        lax.fori_loop(0, N_EXPERTS, body, 0)

    pl.when(step == 0)(lambda: fetch(step, slot))
    pl.when(step + 1 < pl.num_programs(0))(lambda: fetch(step + 1, 1 - slot))
    gu = _dot(h_ref[...], wsgu_ref[...])
    ff = gu.shape[1] // 2
    shared = _dot(_silu(gu[:, :ff]) * gu[:, ff:], wsd_ref[...])
    cid = lax.broadcasted_iota(jnp.int32, (1, rows), 1).astype(F32)
    start = ext_ref[0, :, 1:2]
    member = jnp.where((cid >= start) & (cid < ext_ref[0, :, 2:3]), 1.0, 0.0)
    offset = cid - jnp.sum(member * start, axis=0, keepdims=True)
    member = member.astype(BF16)
    pos = jnp.dot(pos_ref[...], member, preferred_element_type=F32)
    mix = jnp.where(pos == offset, jnp.dot(w_ref[...].astype(BF16), member, preferred_element_type=F32), 0.0)
    mix = mix.astype(BF16)
    last = step * N_EXPERTS + N_EXPERTS - 1
    filled = ls_ref[last] + RUN_ALIGN * n8_ref[last]
    _wait_rows(ybuf, ys_hbm, sem, filled // RUN_ALIGN, rows // RUN_ALIGN, False)
    rid = lax.broadcasted_iota(jnp.int32, (rows, 1), 0)
    lo, hi = _unpack(jnp.where(rid < filled, ybuf[...], jnp.uint32(0)))
    routed = jnp.concatenate([jnp.dot(mix, lo, preferred_element_type=F32),
                              jnp.dot(mix, hi, preferred_element_type=F32)], axis=1)
    x = xm_ref[...] + mod_ref[0, 5:6, :] * (routed + shared)
    o_ref[...] = _rms(x, fn_ref[...])


def _combine_call(tables, ys, pos_te, w_te, ext_cols, xm, h2, mod, wsgu, wsd, final_norm, tm, tiles_per_batch):
    t, d = xm.shape
    lrows = TOP_K * tm + N_EXPERTS * RUN_ALIGN
    tok = pl.BlockSpec((tm, d), lambda i, *_: (i, 0))
    per_e = pl.BlockSpec((tm, N_EXPERTS), lambda i, *_: (i, 0))
    return pl.pallas_call(
        _combine_kernel,
        grid_spec=pltpu.PrefetchScalarGridSpec(
            num_scalar_prefetch=len(tables), grid=(t // tm,),
            in_specs=[pl.BlockSpec(memory_space=pl.ANY), per_e, per_e,
                      pl.BlockSpec((1,) + ext_cols.shape[1:], lambda i, *_: (i, 0, 0)), tok, tok,
                      pl.BlockSpec((1, SUBLANES, d), lambda i, *_: (i // tiles_per_batch, 0, 0)),
                      _full(wsgu.shape), _full(wsd.shape), _full(final_norm.shape)],
            out_specs=tok,
            scratch_shapes=[pltpu.VMEM((2, lrows, d // 2), jnp.uint32), pltpu.SemaphoreType.DMA((2,))]),
        out_shape=jax.ShapeDtypeStruct((t, d), F32),
        compiler_params=_params("arbitrary"),
        name="combine",
    )(*tables, ys, pos_te, w_te, ext_cols, xm, h2, mod, wsgu, wsd, final_norm)


def _moe(xm, mod, norm_ffn, w_router, router_bias, wg, wu, wd, wsg, wsu, wsd, final_norm, tm):
    bsz, s, d = xm.shape
    t = bsz * s
    nt = t // tm
    perm = (np.arange(N_EXPERTS) % N_GROUPS) * GROUP_SIZE + np.arange(N_EXPERTS) // N_GROUPS
    wrt = w_router.T[perm]
    rbias = router_bias[perm][:, None]
    lower = jnp.asarray(perm[None, :] < perm[:, None], BF16)
    h2, w_et, pos_et, ext_cols, ext_rows = _route_call(xm, mod, norm_ffn, wrt, rbias, lower, tm, ROUTE_TILES)

    inv = np.argsort(perm)
    n8 = (ext_cols[:, :, 0].astype(jnp.int32)[:, inv] + (RUN_ALIGN - 1)) // RUN_ALIGN
    run = RUN_ALIGN * n8
    ls = jnp.cumsum(run, axis=1) - run
    tot = jnp.sum(run, axis=0)
    padded = (tot + EXPERT_BLOCK - 1) // EXPERT_BLOCK * EXPERT_BLOCK
    pad_end = jnp.cumsum(padded)
    gs = (pad_end - padded)[None, :] + jnp.cumsum(run, axis=0) - run
    nblk = -(-(t * TOP_K + nt * N_EXPERTS * (RUN_ALIGN - 1)) // EXPERT_BLOCK) + N_EXPERTS
    blk_first = jnp.arange(nblk, dtype=jnp.int32)[:, None] * EXPERT_BLOCK
    blk_e = jnp.minimum(jnp.sum((pad_end[None, :] <= blk_first).astype(jnp.int32), axis=1), N_EXPERTS - 1)
    nused = (pad_end[-1:] // EXPERT_BLOCK).astype(jnp.int32)
    tables = [a.reshape(-1).astype(jnp.int32) for a in (n8, ls, gs)]
    tails = [(pad_end - padded + tot).astype(jnp.int32), ((padded - tot) // RUN_ALIGN).astype(jnp.int32), nused]

    h2f = h2.reshape(t, d)
    xs = _dispatch_call(tables + tails, pos_et, ext_rows, h2f, nblk, tm)
    ys = _expert_call(blk_e, nused, xs, wg, wu, wd)
    wsgu = jnp.concatenate([wsg, wsu], axis=1).astype(BF16)
    out = _combine_call(tables, ys, pos_et.T, w_et.T, ext_cols, xm.reshape(t, d), h2f, mod, wsgu, wsd.astype(BF16),
                        final_norm, tm, s // tm)
    return out.reshape(bsz, s, d)


def _rope_tables(s):
    rows = s // GRID_W
    row = jnp.broadcast_to(jnp.arange(rows, dtype=F32)[:, None], (rows, GRID_W)).reshape(-1)
    col = jnp.broadcast_to(jnp.arange(GRID_W, dtype=F32)[None, :], (rows, GRID_W)).reshape(-1)
    half = QK_ROPE // 2
    inv_freq = ROPE_THETA ** (-jnp.arange(0, half, 2, dtype=F32) / half)
    ar, ac = row[:, None] * inv_freq, col[:, None] * inv_freq
    ones = jnp.ones((s, QK_NOPE), F32)
    tail = HEAD_PAD - QK_NOPE - QK_ROPE
    cos_t = jnp.concatenate([ones, jnp.cos(ar), jnp.cos(ar), jnp.cos(ac), jnp.cos(ac), jnp.ones((s, tail), F32)], 1)
    sin_t = jnp.concatenate([0 * ones, -jnp.sin(ar), jnp.sin(ar), -jnp.sin(ac), jnp.sin(ac),
                             jnp.zeros((s, tail), F32)], 1)
    return cos_t, sin_t


_Q4 = QK_ROPE // 4
ROPE_SWAP = np.concatenate([np.arange(_Q4, 2 * _Q4), np.arange(0, _Q4), np.arange(3 * _Q4, 4 * _Q4),
                            np.arange(2 * _Q4, 3 * _Q4)])


def _rope_slot(w, swap):
    if swap:
        w = w[..., ROPE_SWAP]
    pad = [(0, 0)] * (w.ndim - 1) + [(QK_NOPE, HEAD_PAD - QK_NOPE - QK_ROPE)]
    return jnp.pad(w, pad)


TILES = dict(inproj=512, tq=2048, tk=1408, fft_kb=8, merge=512, moe=256)


def kernel(x, c, ctx, c_ctx, w_mod, b_mod, norm_mix, norm_ffn, w_in, b_in, q_norm, w_uq, kv_norm, w_ukv, w_branch_attn, hy_conv_w, hy_conv_b, hy_filt_w1, hy_filt_b1, hy_filt_w2, hy_filt_b2, hy_filt_w3, hy_filt_freq, hy_skip, w_branch_hyena, w_out, w_router, router_bias, w_exp_gate, w_exp_up, w_exp_down, w_sh_gate, w_sh_up, w_sh_down, final_norm):
    bsz, s, d = x.shape
    tl = TILES
    assert w_mod.shape[0] == 1, "single-layer trunk"
    i = 0

    rows = -(-(bsz + 1) // SUBLANES) * SUBLANES
    c_rows = jnp.pad(jnp.concatenate([c, c_ctx[None]], axis=0), ((0, rows - bsz - 1), (0, 0)))
    mod_all = _mod_call(c_rows, w_mod[i], b_mod[i])
    mod_all = jnp.pad(mod_all.reshape(rows, 6, d), ((0, 0), (0, SUBLANES - 6), (0, 0)))
    mod, modc = mod_all[:bsz], mod_all[bsz:bsz + 1]

    cuts = np.cumsum([Q_LORA, KV_LORA, QK_ROPE, 3 * HY_WIDTH])
    wi, bi = w_in[i], b_in[i][None]
    w_q, w_kv, w_pe, w_hy, w_g = jnp.split(wi, cuts, axis=1)
    b_q, b_kv, b_pe, b_hy, b_g = jnp.split(bi, cuts, axis=1)
    wa = jnp.concatenate([w_q, w_kv, _rope_slot(w_pe, False), _rope_slot(w_pe, True)], axis=1).astype(BF16)
    ba = jnp.concatenate([b_q, b_kv, _rope_slot(b_pe, False), _rope_slot(b_pe, True)], axis=1)
    wq3 = w_uq[i].reshape(Q_LORA, N_HEADS, QK_NOPE + QK_ROPE) * (ATTN_SCALE * math.log2(math.e))
    tail = ((0, 0), (0, 0), (0, HEAD_PAD - QK_NOPE))
    wuq = (jnp.pad(wq3[..., :QK_NOPE], tail) + _rope_slot(wq3[..., QK_NOPE:], False)).reshape(Q_LORA, -1).astype(BF16)
    wuqs = _rope_slot(wq3[..., QK_NOPE:], True).reshape(Q_LORA, -1).astype(BF16)
    wkv3 = w_ukv[i].reshape(KV_LORA, N_HEADS, QK_NOPE + V_HEAD)
    wuk = jnp.pad(wkv3[..., :QK_NOPE], tail).reshape(KV_LORA, -1).astype(BF16)
    wuvt = wkv3[..., QK_NOPE:].reshape(KV_LORA, -1).T.astype(BF16)
    nm, qn, kvn = norm_mix[i][None], q_norm[i][None], kv_norm[i][None]

    w_c = jnp.concatenate([w_kv, _rope_slot(w_pe, False)], axis=1).astype(BF16)
    b_c = jnp.concatenate([b_kv, _rope_slot(b_pe, False)], axis=1)
    ck, cvt = _ctx_call(ctx, modc, nm, w_c, b_c, kvn, wuk, wuvt)

    cos_t, sin_t = _rope_tables(s)
    q, k, vt, hv, hx1, hx2, gate = _inproj_call(
        x, mod, nm, wa, ba, w_hy.astype(BF16), b_hy, w_g.astype(BF16), b_g, qn, wuq, wuqs, kvn, wuk, wuvt,
        cos_t, sin_t, hy_conv_w[i], hy_conv_b[i][None], tl["inproj"])

    attn = _attn_call(q, jnp.concatenate([ck, k], axis=2), jnp.concatenate([cvt, vt], axis=3), tl["tq"], tl["tk"])
    hy = _hyena(hv, hx1, hx2, hy_filt_w1[i], hy_filt_b1[i], hy_filt_w2[i], hy_filt_b2[i], hy_filt_w3[i],
                hy_filt_freq[i], hy_skip[i], tl["fft_kb"])
    xm = _merge_call(x, attn, hy, gate, mod, w_branch_attn[i].astype(BF16), w_branch_hyena[i].astype(BF16),
                     w_out[i].astype(BF16), tl["merge"])
    return _moe(xm, mod, norm_ffn[i][None], w_router[i], router_bias[i], w_exp_gate[i], w_exp_up[i], w_exp_down[i],
                w_sh_gate[i], w_sh_up[i], w_sh_down[i], final_norm[None], tl["moe"])
```

```python
import functools
import math

import numpy as np
import jax
import jax.numpy as jnp
from jax import lax
from jax.experimental import pallas as pl
from jax.experimental.pallas import tpu as pltpu

GRID_W = 64
N_HEADS = 8
QK_NOPE = 64
QK_ROPE = 32
V_HEAD = 64
Q_LORA = 256
KV_LORA = 128
ROPE_THETA = 10000.0
ATTN_SCALE = 1.0 / math.sqrt(QK_NOPE + QK_ROPE)
HY_WIDTH = 512
HY_ORDER = 2
HY_SHORT = 3
HY_BANDS = 8
HY_EMB = 1 + 2 * HY_BANDS
HY_EMB_PAD = 32
HY_FAST_DECAY = 0.3
HY_SLOW_DECAY = 1.5
HY_DECAY_TARGET = 1e-2
N_EXPERTS = 64
N_GROUPS = 8
GROUP_SIZE = N_EXPERTS // N_GROUPS
TOPK_GROUPS = 4
TOP_K = 8
EXPERT_FF = 256
ROUTE_SCALE = 2.5
EXPERT_BLOCK = 1024
RUN_ALIGN = 8
ROUTE_TILES = 8
LONG_RUN = 8
NORM_EPS = 1e-6

HEAD_PAD = 128
Q_CHUNK = 512
AHEAD = 2
LANES = 128
SUBLANES = 8
VMEM_LIMIT = 48 * 1024 * 1024

F32 = jnp.float32
BF16 = jnp.bfloat16
NT_DIMS = (((1,), (1,)), ((), ()))
NN_DIMS = (((1,), (0,)), ((), ()))


def _params(*sem):
    return pltpu.CompilerParams(dimension_semantics=sem, vmem_limit_bytes=VMEM_LIMIT)


def _dot(a, b):
    return jnp.dot(a.astype(BF16), b.astype(BF16), preferred_element_type=F32)


def _split(a):
    hi = a.astype(BF16)
    lo = (a - hi.astype(F32)).astype(BF16)
    return hi, lo


def _dot3(a, b, dims=NN_DIMS):
    ah, al = _split(a)
    bh, bl = _split(b)
    d = functools.partial(lax.dot_general, dimension_numbers=dims, preferred_element_type=F32)
    return d(ah, bh) + (d(ah, bl) + d(al, bh))


def _rms(x, g):
    return x * lax.rsqrt(jnp.mean(x * x, axis=-1, keepdims=True) + NORM_EPS) * g


def _silu(x):
    return x * jax.nn.sigmoid(x)


def _full(shape):
    nd = len(shape)
    return pl.BlockSpec(shape, lambda *_: (0,) * nd)


def _mod_kernel(c_ref, w_ref, b_ref, o_ref):
    o_ref[...] = _dot3(_silu(c_ref[...]), w_ref[...]) + b_ref[...]


def _mod_call(c_rows, w_mod, b_mod):
    r, d = c_rows.shape
    n = w_mod.shape[1]
    bn = 1024
    return pl.pallas_call(
        _mod_kernel,
        grid=(n // bn,),
        in_specs=[_full((r, d)), pl.BlockSpec((d, bn), lambda j: (0, j)), pl.BlockSpec((1, bn), lambda j: (0, j))],
        out_specs=pl.BlockSpec((r, bn), lambda j: (0, j)),
        out_shape=jax.ShapeDtypeStruct((r, n), F32),
        compiler_params=_params("arbitrary"),
        name="mod",
    )(c_rows, w_mod, b_mod.reshape(1, n))


def _prenorm(x, mod_ref, row, g):
    shift = mod_ref[0, row:row + 1, :]
    scale = mod_ref[0, row + 1:row + 2, :]
    return _rms(x, g) * (1.0 + scale) + shift


def _kv_heads(kv_lat, kpe, kvn_ref, wuk_ref, wuvt_ref, k_out, vt_out):
    kvn = _rms(kv_lat, kvn_ref[...]).astype(BF16)
    kk = _dot(kvn, wuk_ref[...])
    vt = lax.dot_general(wuvt_ref[...], kvn, NT_DIMS, preferred_element_type=F32)
    ones = jnp.ones((HEAD_PAD - V_HEAD, vt.shape[1]), F32)
    for h in range(N_HEADS):
        k_out[0, h] = (kk[:, HEAD_PAD * h:HEAD_PAD * (h + 1)] + kpe).astype(BF16)
        vt_out[0, h] = jnp.concatenate([vt[V_HEAD * h:V_HEAD * (h + 1)], ones], axis=0).astype(BF16)


def _ctx_kernel(c_ref, mod_ref, nm_ref, w_ref, b_ref, kvn_ref, wuk_ref, wuv_ref, k_out, v_out):
    h = _prenorm(c_ref[0], mod_ref, 0, nm_ref[...]).astype(BF16)
    a = _dot(h, w_ref[...]) + b_ref[...]
    _kv_heads(a[:, :KV_LORA], a[:, KV_LORA:], kvn_ref, wuk_ref, wuv_ref, k_out, v_out)


def _ctx_call(ctx, modc, norm_mix, w_c, b_c, kv_norm, w_uk, w_uv):
    bsz, n, d = ctx.shape
    return pl.pallas_call(
        _ctx_kernel,
        grid=(bsz,),
        in_specs=[pl.BlockSpec((1, n, d), lambda b: (b, 0, 0)), _full(modc.shape), _full(norm_mix.shape),
                  _full(w_c.shape), _full(b_c.shape), _full(kv_norm.shape), _full(w_uk.shape), _full(w_uv.shape)],
        out_specs=[pl.BlockSpec((1, N_HEADS, n, HEAD_PAD), lambda b: (b, 0, 0, 0)),
                   pl.BlockSpec((1, N_HEADS, HEAD_PAD, n), lambda b: (b, 0, 0, 0))],
        out_shape=[jax.ShapeDtypeStruct((bsz, N_HEADS, n, HEAD_PAD), BF16),
                   jax.ShapeDtypeStruct((bsz, N_HEADS, HEAD_PAD, n), BF16)],
        compiler_params=_params("arbitrary"),
        name="ctx",
    )(ctx, modc, norm_mix, w_c, b_c, kv_norm, w_uk, w_uv)


def _inproj_kernel(x_ref, xp_ref, xn_ref, mod_ref, nm_ref, wa_ref, ba_ref, why_ref, bhy_ref, wg_ref, bg_ref,
                   qn_ref, wuq_ref, wuqs_ref, kvn_ref, wuk_ref, wuv_ref, cos_ref, sin_ref, cw_ref, cb_ref,
                   q_out, k_out, v_out, hv_out, hx1_out, hx2_out, g_out):
    i = pl.program_id(0)
    tm = x_ref.shape[1]
    nm = nm_ref[...]
    h = _prenorm(x_ref[0], mod_ref, 0, nm).astype(BF16)
    a = _dot(h, wa_ref[...]) + ba_ref[...]
    q_lat = a[:, :Q_LORA]
    kv_lat = a[:, Q_LORA:Q_LORA + KV_LORA]
    kpe_m = a[:, Q_LORA + KV_LORA:Q_LORA + KV_LORA + HEAD_PAD]
    kpe_s = a[:, Q_LORA + KV_LORA + HEAD_PAD:]
    cos = cos_ref[...]
    sin = sin_ref[...]
    qn = _rms(q_lat, qn_ref[...]).astype(BF16)
    qa = _dot(qn, wuq_ref[...])
    qs = _dot(qn, wuqs_ref[...])
    for hh in range(N_HEADS):
        sl = slice(HEAD_PAD * hh, HEAD_PAD * (hh + 1))
        q_out[0, hh] = (qa[:, sl] * cos + qs[:, sl] * sin).astype(BF16)
    _kv_heads(kv_lat, kpe_m * cos + kpe_s * sin, kvn_ref, wuk_ref, wuv_ref, k_out, v_out)
    g_out[0] = (_dot(h, wg_ref[...]) + bg_ref[...]).astype(BF16)

    why = why_ref[...]
    bhy = bhy_ref[...]
    halo = jnp.concatenate([_prenorm(xp_ref[0], mod_ref, 0, nm), _prenorm(xn_ref[0], mod_ref, 0, nm)], axis=0)
    hy_all = _dot(jnp.concatenate([h, halo.astype(BF16)], axis=0), why) + bhy
    hy = hy_all[:tm]
    prev = jnp.where(i == 0, 0.0, hy_all[tm + SUBLANES - 1:tm + SUBLANES])
    nxt = jnp.where(i == pl.num_programs(0) - 1, 0.0, hy_all[tm + SUBLANES:tm + SUBLANES + 1])
    rid = lax.broadcasted_iota(jnp.int32, (tm, 1), 0)
    up = jnp.where(rid == 0, prev, pltpu.roll(hy, 1, 0))
    dn = jnp.where(rid == tm - 1, nxt, pltpu.roll(hy, tm - 1, 0))
    u = up * cw_ref[0:1, :] + hy * cw_ref[1:2, :] + dn * cw_ref[2:3, :] + cb_ref[...]
    hv_out[0] = u[:, :HY_WIDTH]
    hx1_out[0] = u[:, HY_WIDTH:2 * HY_WIDTH]
    hx2_out[0] = u[:, 2 * HY_WIDTH:]


def _inproj_call(x, mod, norm_mix, wa, ba, why, bhy, wg, bg, q_norm, wuq, wuqs, kv_norm, wuk, wuvt, cos_t, sin_t, cw,
                 cb, tm):
    bsz, s, d = x.shape
    nt = s // tm
    rb = tm // SUBLANES
    last_rb = s // SUBLANES - 1
    consts = [norm_mix, wa, ba, why, bhy, wg, bg, q_norm, wuq, wuqs, kv_norm, wuk, wuvt]
    in_specs = [
        pl.BlockSpec((1, tm, d), lambda i, b: (b, i, 0)),
        pl.BlockSpec((1, SUBLANES, d), lambda i, b: (b, jnp.maximum(i * rb - 1, 0), 0)),
        pl.BlockSpec((1, SUBLANES, d), lambda i, b: (b, jnp.minimum((i + 1) * rb, last_rb), 0)),
        pl.BlockSpec((1, SUBLANES, d), lambda i, b: (b, 0, 0)),
    ] + [_full(c.shape) for c in consts] + [
        pl.BlockSpec((tm, HEAD_PAD), lambda i, b: (i, 0)),
        pl.BlockSpec((tm, HEAD_PAD), lambda i, b: (i, 0)),
        _full(cw.shape), _full(cb.shape),
    ]
    hw = HY_WIDTH
    out_specs = [
        pl.BlockSpec((1, N_HEADS, tm, HEAD_PAD), lambda i, b: (b, 0, i, 0)),
        pl.BlockSpec((1, N_HEADS, tm, HEAD_PAD), lambda i, b: (b, 0, i, 0)),
        pl.BlockSpec((1, N_HEADS, HEAD_PAD, tm), lambda i, b: (b, 0, 0, i)),
        pl.BlockSpec((1, tm, hw), lambda i, b: (b, i, 0)),
        pl.BlockSpec((1, tm, hw), lambda i, b: (b, i, 0)),
        pl.BlockSpec((1, tm, hw), lambda i, b: (b, i, 0)),
        pl.BlockSpec((1, tm, 2 * d), lambda i, b: (b, i, 0)),
    ]
    out_shape = [
        jax.ShapeDtypeStruct((bsz, N_HEADS, s, HEAD_PAD), BF16),
        jax.ShapeDtypeStruct((bsz, N_HEADS, s, HEAD_PAD), BF16),
        jax.ShapeDtypeStruct((bsz, N_HEADS, HEAD_PAD, s), BF16),
        jax.ShapeDtypeStruct((bsz, s, hw), F32),
        jax.ShapeDtypeStruct((bsz, s, hw), F32),
        jax.ShapeDtypeStruct((bsz, s, hw), F32),
        jax.ShapeDtypeStruct((bsz, s, 2 * d), BF16),
    ]
    return pl.pallas_call(
        _inproj_kernel,
        grid=(nt, bsz),
        in_specs=in_specs,
        out_specs=out_specs,
        out_shape=out_shape,
        compiler_params=_params("arbitrary", "arbitrary"),
        name="inproj",
    )(x, x, x, mod, *consts, cos_t, sin_t, cw, cb)


def _attn_kernel(q_ref, k_ref, vt_ref, o_ref, m_sc, acc_sc):
    j = pl.program_id(2)

    @pl.when(j == 0)
    def _():
        m_sc[...] = jnp.full(m_sc.shape, -jnp.inf, F32)
        acc_sc[...] = jnp.zeros(acc_sc.shape, F32)

    tq = q_ref.shape[2]
    qw = min(tq, Q_CHUNK)
    units = [(h, c) for h in range(N_HEADS) for c in range(0, tq, qw)]

    def scores(u):
        h, c = units[u]
        return lax.dot_general(k_ref[0, h], q_ref[0, h, c:c + qw, :], NT_DIMS,
                               preferred_element_type=F32)

    pending = [scores(u) for u in range(AHEAD)]
    for u, (h, c) in enumerate(units):
        if u + AHEAD < len(units):
            pending.append(scores(u + AHEAD))
        st = pending.pop(0)
        m_prev = m_sc[h, :, c:c + qw]
        m_new = jnp.maximum(m_prev, jnp.max(st, axis=0, keepdims=True))
        pt = jnp.exp2(st - m_new).astype(BF16)
        acc_sc[h, :, c:c + qw] = (jnp.exp2(m_prev - m_new) * acc_sc[h, :, c:c + qw]
                                  + jnp.dot(vt_ref[0, h], pt, preferred_element_type=F32))
        m_sc[h, :, c:c + qw] = m_new

    @pl.when(j == pl.num_programs(2) - 1)
    def _():
        ot = jnp.concatenate([acc_sc[h, :V_HEAD] / acc_sc[h, V_HEAD:V_HEAD + 1] for h in range(N_HEADS)], axis=0)
        o_ref[0] = ot.T.astype(o_ref.dtype)


def _attn_call(q, k, vt, tq, tk):
    bsz, nh, s, dh = q.shape
    nk = k.shape[2]
    dv = nh * V_HEAD
    return pl.pallas_call(
        _attn_kernel,
        grid=(bsz, s // tq, nk // tk),
        in_specs=[
            pl.BlockSpec((1, nh, tq, dh), lambda b, i, j: (b, 0, i, 0)),
            pl.BlockSpec((1, nh, tk, dh), lambda b, i, j: (b, 0, j, 0)),
            pl.BlockSpec((1, nh, dh, tk), lambda b, i, j: (b, 0, 0, j)),
        ],
        out_specs=pl.BlockSpec((1, tq, dv), lambda b, i, j: (b, i, 0)),
        out_shape=jax.ShapeDtypeStruct((bsz, s, dv), BF16),
        scratch_shapes=[pltpu.VMEM((nh, 1, tq), F32), pltpu.VMEM((nh, dh, tq), F32)],
        compiler_params=_params("arbitrary", "arbitrary", "arbitrary"),
        name="attn",
    )(q, k, vt)


def _filter_kernel(emb_ref, w1_ref, b1_ref, w2_ref, b2_ref, w3_ref, fr_ref, dl_ref, full_out, asum_out, *, seq):
    r = pl.program_id(0)
    rb = emb_ref.shape[0]
    emb = emb_ref[...]
    fr = fr_ref[...]
    h = jnp.sin(fr * (_dot3(emb, w1_ref[...]) + b1_ref[...]))
    h = jnp.sin(fr * (_dot3(h, w2_ref[...]) + b2_ref[...]))
    k = _dot3(h, w3_ref[0]) * jnp.exp(-emb[:, 0:1] * dl_ref[...])
    row = r * rb + lax.broadcasted_iota(jnp.int32, (rb, 1), 0)
    k = jnp.where(row == seq, 0.0, k)
    full_out[...] = k

    @pl.when(r == 0)
    def _():
        asum_out[...] = jnp.zeros(asum_out.shape, F32)

    asum_out[...] += jnp.sum(jnp.abs(k), axis=0, keepdims=True)


def _filter_call(emb, w1, b1, w2, b2, w3sel, freq, deltas2, seq, rb):
    n2 = emb.shape[0]
    half_blocks = seq // rb
    width = w3sel.shape[2]
    return pl.pallas_call(
        functools.partial(_filter_kernel, seq=seq),
        grid=(n2 // rb,),
        in_specs=[pl.BlockSpec((rb, HY_EMB_PAD), lambda r: (r, 0)), _full(w1.shape), _full(b1.shape),
                  _full(w2.shape), _full(b2.shape),
                  pl.BlockSpec((1,) + w3sel.shape[1:], lambda r: (r // half_blocks, 0, 0)),
                  _full(freq.shape), _full(deltas2.shape)],
        out_specs=[pl.BlockSpec((rb, width), lambda r: (r, 0)), pl.BlockSpec((1, width), lambda r: (0, 0))],
        out_shape=[jax.ShapeDtypeStruct((n2, width), F32), jax.ShapeDtypeStruct((1, width), F32)],
        compiler_params=_params("arbitrary"),
        name="filt",
    )(emb, w1, b1, w2, b2, w3sel, freq, deltas2)


def _fa_kernel(u_ref, f_ref, a_out):
    two, _, hn, g, c = u_ref.shape
    a = _dot(f_ref[...], u_ref[...].reshape(two * hn * g, c))
    a_out[...] = (_pack(a) if a_out.dtype == jnp.uint32 else a).reshape(a_out.shape)


def _fa_call(u5, fmat, packed):
    _, p, hn, n, c = u5.shape
    g = SUBLANES
    co, dt = (c // 2, jnp.uint32) if packed else (c, F32)
    return pl.pallas_call(
        _fa_kernel,
        grid=(p, n // g),
        in_specs=[pl.BlockSpec((2, 1, hn, g, c), lambda q, j: (0, q, 0, j, 0)), _full(fmat.shape)],
        out_specs=pl.BlockSpec((1, 2, n, g, co), lambda q, j: (q, 0, 0, j, 0)),
        out_shape=jax.ShapeDtypeStruct((p, 2, n, n, co), dt),
        compiler_params=_params("arbitrary", "arbitrary"),
        name="fa",
    )(u5, fmat)


def _dot_packed(w, u):
    lo, hi = _unpack(u)
    return jnp.concatenate([jnp.dot(w, lo, preferred_element_type=F32), jnp.dot(w, hi, preferred_element_type=F32)],
                           axis=1)


def _fb_kernel(a_ref, g_ref, asum_ref, kf_out):
    _, two, kb, n, c = a_ref.shape
    scale = 1.0 / (asum_ref[...] + 1e-6)
    for kk in range(kb):
        x = _dot(g_ref[kk], a_ref[0, :, kk].reshape(two * n, c)) * scale
        kf_out[kk] = x.reshape(two, n, c)


def _fb_call(a5, gmat, asum, kb):
    _, _, n, _, c = a5.shape
    return pl.pallas_call(
        _fb_kernel,
        grid=(n // kb,),
        in_specs=[pl.BlockSpec((1, 2, kb, n, c), lambda k: (0, 0, k, 0, 0)),
                  pl.BlockSpec((kb, 2 * n, 2 * n), lambda k: (k, 0, 0)), _full(asum.shape)],
        out_specs=pl.BlockSpec((kb, 2, n, c), lambda k: (k, 0, 0, 0)),
        out_shape=jax.ShapeDtypeStruct((n, 2, n, c), F32),
        compiler_params=_params("arbitrary"),
        name="fb",
    )(a5, gmat, asum)


def _mid_kernel(a_ref, g_ref, h_ref, kf_ref, b_out):
    _, two, kb, n, c = a_ref.shape
    for kk in range(kb):
        x = _dot_packed(g_ref[kk], a_ref[0, :, kk].reshape(two * n, c))
        xr, xi = x[:n], x[n:]
        kr, ki = kf_ref[kk, 0], kf_ref[kk, 1]
        y = jnp.concatenate([xr * kr - xi * ki, xr * ki + xi * kr], axis=0)
        b_out[0, :, kk] = _pack(_dot(h_ref[kk], y)).reshape(two, n, c)


def _mid_call(a5, gmat, hmat, kf, order, kb):
    p, _, n, _, c = a5.shape
    return pl.pallas_call(
        _mid_kernel,
        grid=(n // kb, p),
        in_specs=[pl.BlockSpec((1, 2, kb, n, c), lambda k, q: (q, 0, k, 0, 0)),
                  pl.BlockSpec((kb, 2 * n, 2 * n), lambda k, q: (k, 0, 0)),
                  pl.BlockSpec((kb, 2 * n, 2 * n), lambda k, q: (k, 0, 0)),
                  pl.BlockSpec((kb, 2, n, 2 * c), lambda k, q: (k, 0, 0, order))],
        out_specs=pl.BlockSpec((1, 2, kb, n, c), lambda k, q: (q, 0, k, 0, 0)),
        out_shape=jax.ShapeDtypeStruct(a5.shape, jnp.uint32),
        compiler_params=_params("arbitrary", "arbitrary"),
        name="mid",
    )(a5, gmat, hmat, kf)


def _fc_kernel(b_ref, f_ref, u_ref, m_ref, skip_ref, o_out):
    _, two, n, g, c = b_ref.shape
    y = _dot_packed(f_ref[...], b_ref[...].reshape(two * n * g, c)).reshape(u_ref.shape)
    o_out[...] = m_ref[...] * (y + u_ref[...] * skip_ref[...])


def _fc_call(b5, finv, u5, m5, skip_row):
    _, p, hn, n, c = u5.shape
    g = SUBLANES
    blk = pl.BlockSpec((2, 1, hn, g, c), lambda q, j: (0, q, 0, j, 0))
    return pl.pallas_call(
        _fc_kernel,
        grid=(p, n // g),
        in_specs=[pl.BlockSpec((1, 2, n, g, c // 2), lambda q, j: (q, 0, 0, j, 0)), _full(finv.shape), blk, blk,
                  _full(skip_row.shape)],
        out_specs=blk,
        out_shape=jax.ShapeDtypeStruct(u5.shape, F32),
        compiler_params=_params("arbitrary", "arbitrary"),
        name="fc",
    )(b5, finv, u5, m5, skip_row)


def _dft_tables(n):
    hn = n // 2
    k = np.arange(n)[:, None]
    ang = -2.0 * np.pi * (k * np.arange(n)[None, :] % n) / n
    fr, fi = np.cos(ang), np.sin(ang)
    f_data = np.block([[fr[:, :hn], -fi[:, :hn]], [fi[:, :hn], fr[:, :hn]]])
    f_filt = np.concatenate([fr, fi], axis=0)
    er, ei = fr[:hn], -fi[:hn]
    f_inv = np.block([[er, -ei], [ei, er]]) / float(n * n)
    k1 = np.arange(n)[:, None, None]
    k2 = np.arange(n)[None, :, None]
    m2 = np.arange(n)[None, None, :]
    ang2 = -2.0 * np.pi * ((m2 * (k1 + n * k2)) % (n * n)) / (n * n)
    gr, gi = np.cos(ang2).astype(np.float32), np.sin(ang2).astype(np.float32)
    g = np.concatenate([np.concatenate([gr, -gi], axis=2), np.concatenate([gi, gr], axis=2)], axis=1)
    h = np.swapaxes(g, 1, 2)

    def widen(f):
        return np.kron(f, np.eye(SUBLANES))

    return tuple(jnp.asarray(a, BF16) for a in (widen(f_data), widen(f_filt), widen(f_inv), g, h))


def _hyena_filter_tables(seq):
    f32 = np.float32
    t = np.linspace(0.0, 1.0, seq, dtype=f32)[:, None]
    w = (f32(2.0 * math.pi) * np.arange(seq, dtype=f32)[:, None] / f32(seq)).astype(f32)
    f = np.linspace(1e-4, HY_BANDS - 1, HY_BANDS, dtype=f32)[None, :]
    emb = np.concatenate([t, np.cos(f * w), -np.sin(f * w)], axis=-1).astype(f32)
    emb = np.concatenate([emb, emb[:1], emb[:0:-1]], axis=0)
    emb = np.pad(emb, ((0, 0), (0, HY_EMB_PAD - HY_EMB)))
    deltas = np.abs(np.linspace(math.log(HY_DECAY_TARGET) / HY_SLOW_DECAY,
                                math.log(HY_DECAY_TARGET) / HY_FAST_DECAY, HY_WIDTH, dtype=f32))
    return jnp.asarray(emb), jnp.asarray(np.tile(deltas, HY_ORDER)[None, :])


def _hyena(hv, hx1, hx2, w1, b1, w2, b2, w3, freq, skip, kb):
    bsz, seq, c = hv.shape
    n = int(round(math.sqrt(2 * seq)))
    assert n * n == 2 * seq and bsz % 2 == 0
    hn, p = n // 2, bsz // 2
    f_data, f_filt, f_inv, gmat, hmat = _dft_tables(n)

    emb, deltas2 = _hyena_filter_tables(seq)
    w1p = jnp.pad(w1, ((0, HY_EMB_PAD - HY_EMB), (0, 0)))
    w3r = w3.reshape(w3.shape[0], HY_ORDER, 2, c)
    w3sel = jnp.stack([w3r[:, :, 0, :].reshape(-1, HY_ORDER * c), w3r[:, :, 1, :].reshape(-1, HY_ORDER * c)])
    full, asum = _filter_call(emb, w1p, b1[None], w2, b2[None], w3sel, freq[None], deltas2, seq, min(512, seq))
    c2 = HY_ORDER * c
    kf = _fb_call(_fa_call(full.reshape(2, 1, hn, n, c2), f_filt, False), gmat, asum, kb // HY_ORDER)

    def view(t):
        return t.reshape(2, p, hn, n, c)

    def long_conv(u5, m5, order):
        bm = _mid_call(_fa_call(u5, f_data, True), gmat, hmat, kf, order, kb)
        return _fc_call(bm, f_inv, u5, m5, skip[order][None, :])

    z = long_conv(view(hv), view(hx1), 0)
    return long_conv(z, view(hx2), 1).reshape(bsz, seq, c)


def _merge_kernel(x_ref, at_ref, hy_ref, g_ref, mod_ref, wba_ref, wbh_ref, wo_ref, o_ref):
    d = x_ref.shape[2]
    g = g_ref[0].astype(F32)
    y = (jax.nn.sigmoid(g[:, :d]) * _dot(at_ref[0], wba_ref[...])
         + jax.nn.sigmoid(g[:, d:]) * _dot(hy_ref[0], wbh_ref[...]))
    o_ref[0] = x_ref[0] + mod_ref[0, 2:3, :] * _dot(y, wo_ref[...])


def _merge_call(x, attn, hy, gate, mod, wba, wbh, wo, tm):
    bsz, s, d = x.shape

    def tok(w):
        return pl.BlockSpec((1, tm, w), lambda b, i: (b, i, 0))

    return pl.pallas_call(
        _merge_kernel,
        grid=(bsz, s // tm),
        in_specs=[tok(d), tok(attn.shape[2]), tok(hy.shape[2]), tok(2 * d),
                  pl.BlockSpec((1, SUBLANES, d), lambda b, i: (b, 0, 0)),
                  _full(wba.shape), _full(wbh.shape), _full(wo.shape)],
        out_specs=tok(d),
        out_shape=jax.ShapeDtypeStruct((bsz, s, d), F32),
        compiler_params=_params("arbitrary", "arbitrary"),
        name="merge",
    )(x, attn, hy, gate, mod, wba, wbh, wo)


def _route_kernel(xm_ref, mod_ref, nf_ref, wrt_ref, rb_ref, tri_ref, lt_ref, h2_out, w_out, p_out, col_out, row_out):
    tm = xm_ref.shape[1]
    ng, gs = N_GROUPS, GROUP_SIZE

    h2 = _prenorm(xm_ref[0], mod_ref, 3, nf_ref[...])
    h2_out[0] = h2.astype(h2_out.dtype)
    scores = jax.nn.sigmoid(_dot3(wrt_ref[...], h2, NT_DIMS))
    sel = scores + rb_ref[...]
    slabs = [sel[ng * j:ng * (j + 1)] for j in range(gs)]

    top1 = jnp.full((ng, tm), -jnp.inf, F32)
    top2 = top1
    for x in slabs:
        top2 = jnp.maximum(top2, jnp.minimum(top1, x))
        top1 = jnp.maximum(top1, x)
    gscore = top1 + top2
    gid = lax.broadcasted_iota(jnp.int32, (ng, 1), 0)
    rank = jnp.zeros((ng, tm), jnp.int32)
    for g2 in range(ng):
        row = gscore[g2:g2 + 1]
        beats = (row > gscore) | ((row == gscore) & (g2 < gid))
        rank = rank + beats.astype(jnp.int32)
    gmask = rank < TOPK_GROUPS

    cand = [jnp.where(gmask, x, -jnp.inf) for x in slabs]
    eid = [gid * gs + j for j in range(gs)]
    chosen = []
    for _ in range(TOP_K):
        best = functools.reduce(jnp.maximum, cand)
        best = jnp.max(best, axis=0, keepdims=True)
        idx = functools.reduce(jnp.minimum, [jnp.where(cand[j] == best, eid[j], N_EXPERTS) for j in range(gs)])
        idx = jnp.min(idx, axis=0, keepdims=True)
        chosen.append(idx)
        cand = [jnp.where(eid[j] == idx, -jnp.inf, cand[j]) for j in range(gs)]

    mask = [functools.reduce(jnp.logical_or, [eid[j] == idx for idx in chosen]) for j in range(gs)]
    maskb = jnp.concatenate(mask, axis=0)
    wsel = jnp.where(maskb, scores, 0.0)
    w_out[...] = wsel / jnp.sum(wsel, axis=0, keepdims=True) * ROUTE_SCALE

    def extents(cnt, lower_sum):
        units = jnp.floor((cnt + (RUN_ALIGN - 1)) * (1.0 / RUN_ALIGN))
        start = RUN_ALIGN * lower_sum(units.astype(BF16))
        return start, start + RUN_ALIGN * units

    lane = lax.broadcasted_iota(jnp.int32, (N_EXPERTS, LANES), 1)
    sub = lax.broadcasted_iota(jnp.int32, (SUBLANES, N_EXPERTS), 0)
    ts = tri_ref.shape[0]
    for c in range(tm // ts):
        mb = maskb[:, c * ts:(c + 1) * ts]
        maskf = jnp.where(mb, 1.0, 0.0)
        mask16 = maskf.astype(BF16)
        before = jnp.dot(mask16, tri_ref[...], preferred_element_type=F32)
        p_out[:, c * ts:(c + 1) * ts] = jnp.where(mb, before, -1.0).astype(p_out.dtype)
        cnt_c = jnp.sum(maskf, axis=1, keepdims=True)
        start_c, end_c = extents(jnp.broadcast_to(cnt_c, (N_EXPERTS, LANES)),
                                 lambda u: jnp.dot(lt_ref[...], u, preferred_element_type=F32))
        col_out[c] = jnp.where(lane == 0, cnt_c, jnp.where(lane == 1, start_c, end_c))
        cnt_r = lax.dot_general(jnp.ones((SUBLANES, ts), BF16), mask16, NT_DIMS, preferred_element_type=F32)
        start_r, end_r = extents(cnt_r,
                                 lambda u: lax.dot_general(u, lt_ref[...], NT_DIMS, preferred_element_type=F32))
        row_out[c] = jnp.where(sub == 0, start_r, end_r)


def _route_call(xm, mod, norm_ffn, wrt, rbias, lower, ts, tiles_per_step):
    bsz, s, d = xm.shape
    t = bsz * s
    tm = ts * tiles_per_step
    nt = s // tm
    tri = (jnp.arange(ts)[:, None] < jnp.arange(ts)[None, :]).astype(BF16)
    tok = pl.BlockSpec((N_EXPERTS, tm), lambda i: (0, i))
    return pl.pallas_call(
        _route_kernel,
        grid=(t // tm,),
        in_specs=[pl.BlockSpec((1, tm, d), lambda i: (i // nt, i % nt, 0)),
                  pl.BlockSpec((1, SUBLANES, d), lambda i: (i // nt, 0, 0)),
                  _full(norm_ffn.shape), _full(wrt.shape), _full(rbias.shape), _full(tri.shape),
                  _full(lower.shape)],
        out_specs=[pl.BlockSpec((1, tm, d), lambda i: (i // nt, i % nt, 0)), tok, tok,
                   pl.BlockSpec((tiles_per_step, N_EXPERTS, LANES), lambda i: (i, 0, 0)),
                   pl.BlockSpec((tiles_per_step, SUBLANES, N_EXPERTS), lambda i: (i, 0, 0))],
        out_shape=[jax.ShapeDtypeStruct((bsz, s, d), BF16), jax.ShapeDtypeStruct((N_EXPERTS, t), F32),
                   jax.ShapeDtypeStruct((N_EXPERTS, t), BF16),
                   jax.ShapeDtypeStruct((t // ts, N_EXPERTS, LANES), F32),
                   jax.ShapeDtypeStruct((t // ts, SUBLANES, N_EXPERTS), F32)],
        compiler_params=_params("arbitrary"),
        name="route",
    )(xm, mod, norm_ffn, wrt, rbias, tri, lower)


def _pack(x):
    w = x.shape[1] // 2
    lo = lax.bitcast_convert_type(x[:, :w].astype(BF16).astype(F32), jnp.uint32)
    hi = lax.bitcast_convert_type(x[:, w:].astype(BF16).astype(F32), jnp.uint32)
    return hi | (lo >> 16)


def _unpack(u):
    lo = lax.bitcast_convert_type(u << 16, F32).astype(BF16)
    hi = lax.bitcast_convert_type(u & jnp.uint32(0xFFFF0000), F32).astype(BF16)
    return lo, hi


def _pow2_pieces(units, limit):
    bit = 1
    while bit * 2 <= limit:
        bit *= 2
    while bit:
        yield (units & bit) != 0, units & ~(2 * bit - 1), bit
        bit //= 2


def _rows_copy(vm_ref, hbm_ref, sem, vm_row, hbm_row, rows, to_hbm):
    v = vm_ref.at[pl.ds(pl.multiple_of(vm_row, RUN_ALIGN), rows), :]
    h = hbm_ref.at[pl.ds(pl.multiple_of(hbm_row, RUN_ALIGN), rows), :]
    return pltpu.make_async_copy(v, h, sem) if to_hbm else pltpu.make_async_copy(h, v, sem)


def _run_copies(vm_ref, hbm_ref, sem, n8, vm_row, hbm_row, limit, to_hbm, act):
    def emit(pieces):
        for on, off, size in pieces:
            @pl.when(on)
            def _():
                act(_rows_copy(vm_ref, hbm_ref, sem, vm_row + RUN_ALIGN * off, hbm_row + RUN_ALIGN * off,
                               RUN_ALIGN * size, to_hbm))

    pieces = list(_pow2_pieces(n8, limit))
    long_pieces = [p for p in pieces if p[2] >= LONG_RUN]
    if long_pieces:
        pl.when(n8 >= LONG_RUN)(lambda: emit(long_pieces))
    emit([p for p in pieces if p[2] < LONG_RUN])


def _wait_rows(vm_ref, hbm_ref, sem, units, limit, to_hbm):
    for on, _, size in _pow2_pieces(units, limit):
        @pl.when(on)
        def _():
            _rows_copy(vm_ref, hbm_ref, sem, 0, 0, RUN_ALIGN * size, to_hbm).wait()


def _dispatch_kernel(n8_ref, ls_ref, gs_ref, ts_ref, t8_ref, nu_ref, pos_ref, ext_ref, h_ref, xs_out, srt2, zbuf,
                     sems):
    step = pl.program_id(0)
    tm = h_ref.shape[0]
    rows = srt2.shape[1]
    slot = step % 2
    srt, sem = srt2.at[slot], sems.at[slot]
    rid = lax.broadcasted_iota(jnp.int32, (rows, 1), 0).astype(F32)
    start = ext_ref[0, 0:1, :]
    member = jnp.where((rid >= start) & (rid < ext_ref[0, 1:2, :]), 1.0, 0.0)
    offset = rid - jnp.sum(member * start, axis=1, keepdims=True)
    pos = jnp.dot(member.astype(BF16), pos_ref[...], preferred_element_type=F32)
    sel = jnp.where(pos == offset, 1.0, 0.0).astype(BF16)
    srt[...] = _pack(jnp.dot(sel, h_ref[...], preferred_element_type=F32))

    def send(e, c):
        i = step * N_EXPERTS + e
        _run_copies(srt, xs_out, sem, n8_ref[i], ls_ref[i], gs_ref[i], tm // RUN_ALIGN, True, lambda cp: cp.start())
        return c

    lax.fori_loop(0, N_EXPERTS, send, 0)

    def wait_tile(tile, s):
        last = tile * N_EXPERTS + N_EXPERTS - 1
        _wait_rows(srt2.at[s], xs_out, sems.at[s], ls_ref[last] // RUN_ALIGN + n8_ref[last], rows // RUN_ALIGN, True)

    pl.when(step > 0)(lambda: wait_tile(step - 1, 1 - slot))

    @pl.when(step == pl.num_programs(0) - 1)
    def _():
        wait_tile(step, slot)
        zbuf[...] = jnp.zeros(zbuf.shape, zbuf.dtype)
        nblk = xs_out.shape[0] // EXPERT_BLOCK

        def fill(act):
            def tails(e, c):
                _run_copies(zbuf, xs_out, sem, t8_ref[e], 0, ts_ref[e], EXPERT_BLOCK // RUN_ALIGN - 1, True, act)
                return c

            def blocks(b, c):
                act(pltpu.make_async_copy(
                    zbuf, xs_out.at[pl.ds(pl.multiple_of(b * EXPERT_BLOCK, EXPERT_BLOCK), EXPERT_BLOCK), :], sem))
                return c

            lax.fori_loop(0, N_EXPERTS, tails, 0)
            lax.fori_loop(nu_ref[0], nblk, blocks, 0)

        fill(lambda cp: cp.start())
        fill(lambda cp: cp.wait())


def _dispatch_call(tables, pos_et, ext_rows, h2, nblk, tm):
    t, d = h2.shape
    lrows = TOP_K * tm + N_EXPERTS * RUN_ALIGN
    return pl.pallas_call(
        _dispatch_kernel,
        grid_spec=pltpu.PrefetchScalarGridSpec(
            num_scalar_prefetch=len(tables), grid=(t // tm,),
            in_specs=[pl.BlockSpec((N_EXPERTS, tm), lambda i, *_: (0, i)),
                      pl.BlockSpec((1,) + ext_rows.shape[1:], lambda i, *_: (i, 0, 0)),
                      pl.BlockSpec((tm, d), lambda i, *_: (i, 0))],
            out_specs=pl.BlockSpec(memory_space=pl.ANY),
            scratch_shapes=[pltpu.VMEM((2, lrows, d // 2), jnp.uint32),
                            pltpu.VMEM((EXPERT_BLOCK, d // 2), jnp.uint32), pltpu.SemaphoreType.DMA((2,))]),
        out_shape=jax.ShapeDtypeStruct((nblk * EXPERT_BLOCK, d // 2), jnp.uint32),
        compiler_params=_params("arbitrary"),
        name="dispatch",
    )(*tables, pos_et, ext_rows, h2)


def _expert_kernel(blk_ref, nused_ref, x_ref, wg_ref, wu_ref, wd_ref, y_ref, wgu_sc, wd_sc):
    i = pl.program_id(0)
    used = i < nused_ref[0]

    @pl.when(used & ((i == 0) | (blk_ref[i] != blk_ref[jnp.maximum(i - 1, 0)])))
    def _():
        wgu_sc[:, :EXPERT_FF] = wg_ref[0].astype(BF16)
        wgu_sc[:, EXPERT_FF:] = wu_ref[0].astype(BF16)
        wd_sc[...] = wd_ref[0].astype(BF16)

    @pl.when(used)
    def _():
        lo, hi = _unpack(x_ref[...])
        half = lo.shape[1]
        gu = (jnp.dot(lo, wgu_sc[:half, :], preferred_element_type=F32)
              + jnp.dot(hi, wgu_sc[half:, :], preferred_element_type=F32))
        a = _silu(gu[:, :EXPERT_FF]) * gu[:, EXPERT_FF:]
        y_ref[...] = _pack(_dot(a, wd_sc[...]))

    @pl.when(jnp.logical_not(used))
    def _():
        y_ref[...] = jnp.zeros(y_ref.shape, y_ref.dtype)


def _expert_call(blk_e, nused, xs, wg, wu, wd):
    rows, d = xs.shape
    nblk = rows // EXPERT_BLOCK

    def row_map(i, blk, nu):
        return (jnp.minimum(i, nu[0] - 1), 0)

    def of_expert(w):
        return pl.BlockSpec((1,) + w.shape[1:], lambda i, blk, nu: (blk[i], 0, 0))

    return pl.pallas_call(
        _expert_kernel,
        grid_spec=pltpu.PrefetchScalarGridSpec(
            num_scalar_prefetch=2, grid=(nblk,),
            in_specs=[pl.BlockSpec((EXPERT_BLOCK, d), row_map), of_expert(wg), of_expert(wu), of_expert(wd)],
            out_specs=pl.BlockSpec((EXPERT_BLOCK, d), lambda i, blk, nu: (i, 0)),
            scratch_shapes=[pltpu.VMEM((wg.shape[1], 2 * EXPERT_FF), BF16), pltpu.VMEM(wd.shape[1:], BF16)]),
        out_shape=jax.ShapeDtypeStruct((rows, d), jnp.uint32),
        compiler_params=_params("arbitrary"),
        name="expert",
    )(blk_e, nused, xs, wg, wu, wd)


def _combine_kernel(n8_ref, ls_ref, gs_ref, ys_hbm, pos_ref, w_ref, ext_ref, xm_ref, h_ref, mod_ref, wsgu_ref,
                    wsd_ref, fn_ref, o_ref, ybuf2, sems):
    step = pl.program_id(0)
    tm = xm_ref.shape[0]
    rows = ybuf2.shape[1]
    slot = step % 2
    ybuf, sem = ybuf2.at[slot], sems.at[slot]

    def fetch(tile, s):
        def body(e, c):
            i = tile * N_EXPERTS + e
            _run_copies(ybuf2.at[s], ys_hbm, sems.at[s], n8_ref[i], ls_ref[i], gs_ref[i], tm // RUN_ALIGN, False,
                        lambda cp: cp.start())
            return c
        lax.fori_loop(0, N_EXPERTS, body, 0)

    pl.when(step == 0)(lambda: fetch(step, slot))
    pl.when(step + 1 < pl.num_programs(0))(lambda: fetch(step + 1, 1 - slot))
    gu = _dot(h_ref[...], wsgu_ref[...])
    ff = gu.shape[1] // 2
    shared = _dot(_silu(gu[:, :ff]) * gu[:, ff:], wsd_ref[...])
    cid = lax.broadcasted_iota(jnp.int32, (1, rows), 1).astype(F32)
    start = ext_ref[0, :, 1:2]
    member = jnp.where((cid >= start) & (cid < ext_ref[0, :, 2:3]), 1.0, 0.0)
    offset = cid - jnp.sum(member * start, axis=0, keepdims=True)
    member = member.astype(BF16)
    pos = jnp.dot(pos_ref[...], member, preferred_element_type=F32)
    mix = jnp.where(pos == offset, jnp.dot(w_ref[...].astype(BF16), member, preferred_element_type=F32), 0.0)
    mix = mix.astype(BF16)
    last = step * N_EXPERTS + N_EXPERTS - 1
    filled = ls_ref[last] + RUN_ALIGN * n8_ref[last]
    _wait_rows(ybuf, ys_hbm, sem, filled // RUN_ALIGN, rows // RUN_ALIGN, False)
    rid = lax.broadcasted_iota(jnp.int32, (rows, 1), 0)
    lo, hi = _unpack(jnp.where(rid < filled, ybuf[...], jnp.uint32(0)))
    routed = jnp.concatenate([jnp.dot(mix, lo, preferred_element_type=F32),
                              jnp.dot(mix, hi, preferred_element_type=F32)], axis=1)
    x = xm_ref[...] + mod_ref[0, 5:6, :] * (routed + shared)
    o_ref[...] = _rms(x, fn_ref[...])


def _combine_call(tables, ys, pos_te, w_te, ext_cols, xm, h2, mod, wsgu, wsd, final_norm, tm, tiles_per_batch):
    t, d = xm.shape
    lrows = TOP_K * tm + N_EXPERTS * RUN_ALIGN
    tok = pl.BlockSpec((tm, d), lambda i, *_: (i, 0))
    per_e = pl.BlockSpec((tm, N_EXPERTS), lambda i, *_: (i, 0))
    return pl.pallas_call(
        _combine_kernel,
        grid_spec=pltpu.PrefetchScalarGridSpec(
            num_scalar_prefetch=len(tables), grid=(t // tm,),
            in_specs=[pl.BlockSpec(memory_space=pl.ANY), per_e, per_e,
                      pl.BlockSpec((1,) + ext_cols.shape[1:], lambda i, *_: (i, 0, 0)), tok, tok,
                      pl.BlockSpec((1, SUBLANES, d), lambda i, *_: (i // tiles_per_batch, 0, 0)),
                      _full(wsgu.shape), _full(wsd.shape), _full(final_norm.shape)],
            out_specs=tok,
            scratch_shapes=[pltpu.VMEM((2, lrows, d // 2), jnp.uint32), pltpu.SemaphoreType.DMA((2,))]),
        out_shape=jax.ShapeDtypeStruct((t, d), F32),
        compiler_params=_params("arbitrary"),
        name="combine",
    )(*tables, ys, pos_te, w_te, ext_cols, xm, h2, mod, wsgu, wsd, final_norm)


def _moe(xm, mod, norm_ffn, w_router, router_bias, wg, wu, wd, wsg, wsu, wsd, final_norm, tm):
    bsz, s, d = xm.shape
    t = bsz * s
    nt = t // tm
    perm = (np.arange(N_EXPERTS) % N_GROUPS) * GROUP_SIZE + np.arange(N_EXPERTS) // N_GROUPS
    wrt = w_router.T[perm]
    rbias = router_bias[perm][:, None]
    lower = jnp.asarray(perm[None, :] < perm[:, None], BF16)
    h2, w_et, pos_et, ext_cols, ext_rows = _route_call(xm, mod, norm_ffn, wrt, rbias, lower, tm, ROUTE_TILES)

    inv = np.argsort(perm)
    n8 = (ext_cols[:, :, 0].astype(jnp.int32)[:, inv] + (RUN_ALIGN - 1)) // RUN_ALIGN
    run = RUN_ALIGN * n8
    ls = jnp.cumsum(run, axis=1) - run
    tot = jnp.sum(run, axis=0)
    padded = (tot + EXPERT_BLOCK - 1) // EXPERT_BLOCK * EXPERT_BLOCK
    pad_end = jnp.cumsum(padded)
    gs = (pad_end - padded)[None, :] + jnp.cumsum(run, axis=0) - run
    nblk = -(-(t * TOP_K + nt * N_EXPERTS * (RUN_ALIGN - 1)) // EXPERT_BLOCK) + N_EXPERTS
    blk_first = jnp.arange(nblk, dtype=jnp.int32)[:, None] * EXPERT_BLOCK
    blk_e = jnp.minimum(jnp.sum((pad_end[None, :] <= blk_first).astype(jnp.int32), axis=1), N_EXPERTS - 1)
    nused = (pad_end[-1:] // EXPERT_BLOCK).astype(jnp.int32)
    tables = [a.reshape(-1).astype(jnp.int32) for a in (n8, ls, gs)]
    tails = [(pad_end - padded + tot).astype(jnp.int32), ((padded - tot) // RUN_ALIGN).astype(jnp.int32), nused]

    h2f = h2.reshape(t, d)
    xs = _dispatch_call(tables + tails, pos_et, ext_rows, h2f, nblk, tm)
    ys = _expert_call(blk_e, nused, xs, wg, wu, wd)
    wsgu = jnp.concatenate([wsg, wsu], axis=1).astype(BF16)
    out = _combine_call(tables, ys, pos_et.T, w_et.T, ext_cols, xm.reshape(t, d), h2f, mod, wsgu, wsd.astype(BF16),
                        final_norm, tm, s // tm)
    return out.reshape(bsz, s, d)


def _rope_tables(s):
    rows = s // GRID_W
    row = jnp.broadcast_to(jnp.arange(rows, dtype=F32)[:, None], (rows, GRID_W)).reshape(-1)
    col = jnp.broadcast_to(jnp.arange(GRID_W, dtype=F32)[None, :], (rows, GRID_W)).reshape(-1)
    half = QK_ROPE // 2
    inv_freq = ROPE_THETA ** (-jnp.arange(0, half, 2, dtype=F32) / half)
    ar, ac = row[:, None] * inv_freq, col[:, None] * inv_freq
    ones = jnp.ones((s, QK_NOPE), F32)
    tail = HEAD_PAD - QK_NOPE - QK_ROPE
    cos_t = jnp.concatenate([ones, jnp.cos(ar), jnp.cos(ar), jnp.cos(ac), jnp.cos(ac), jnp.ones((s, tail), F32)], 1)
    sin_t = jnp.concatenate([0 * ones, -jnp.sin(ar), jnp.sin(ar), -jnp.sin(ac), jnp.sin(ac),
                             jnp.zeros((s, tail), F32)], 1)
    return cos_t, sin_t


_Q4 = QK_ROPE // 4
ROPE_SWAP = np.concatenate([np.arange(_Q4, 2 * _Q4), np.arange(0, _Q4), np.arange(3 * _Q4, 4 * _Q4),
                            np.arange(2 * _Q4, 3 * _Q4)])


def _rope_slot(w, swap):
    if swap:
        w = w[..., ROPE_SWAP]
    pad = [(0, 0)] * (w.ndim - 1) + [(QK_NOPE, HEAD_PAD - QK_NOPE - QK_ROPE)]
    return jnp.pad(w, pad)


TILES = dict(inproj=512, tq=2048, tk=1408, fft_kb=8, merge=512, moe=256)


def kernel(x, c, ctx, c_ctx, w_mod, b_mod, norm_mix, norm_ffn, w_in, b_in, q_norm, w_uq, kv_norm, w_ukv, w_branch_attn, hy_conv_w, hy_conv_b, hy_filt_w1, hy_filt_b1, hy_filt_w2, hy_filt_b2, hy_filt_w3, hy_filt_freq, hy_skip, w_branch_hyena, w_out, w_router, router_bias, w_exp_gate, w_exp_up, w_exp_down, w_sh_gate, w_sh_up, w_sh_down, final_norm):
    bsz, s, d = x.shape
    tl = TILES
    assert w_mod.shape[0] == 1, "single-layer trunk"
    i = 0

    rows = -(-(bsz + 1) // SUBLANES) * SUBLANES
    c_rows = jnp.pad(jnp.concatenate([c, c_ctx[None]], axis=0), ((0, rows - bsz - 1), (0, 0)))
    mod_all = _mod_call(c_rows, w_mod[i], b_mod[i])
    mod_all = jnp.pad(mod_all.reshape(rows, 6, d), ((0, 0), (0, SUBLANES - 6), (0, 0)))
    mod, modc = mod_all[:bsz], mod_all[bsz:bsz + 1]

    cuts = np.cumsum([Q_LORA, KV_LORA, QK_ROPE, 3 * HY_WIDTH])
    wi, bi = w_in[i], b_in[i][None]
    w_q, w_kv, w_pe, w_hy, w_g = jnp.split(wi, cuts, axis=1)
    b_q, b_kv, b_pe, b_hy, b_g = jnp.split(bi, cuts, axis=1)
    wa = jnp.concatenate([w_q, w_kv, _rope_slot(w_pe, False), _rope_slot(w_pe, True)], axis=1).astype(BF16)
    ba = jnp.concatenate([b_q, b_kv, _rope_slot(b_pe, False), _rope_slot(b_pe, True)], axis=1)
    wq3 = w_uq[i].reshape(Q_LORA, N_HEADS, QK_NOPE + QK_ROPE) * (ATTN_SCALE * math.log2(math.e))
    tail = ((0, 0), (0, 0), (0, HEAD_PAD - QK_NOPE))
    wuq = (jnp.pad(wq3[..., :QK_NOPE], tail) + _rope_slot(wq3[..., QK_NOPE:], False)).reshape(Q_LORA, -1).astype(BF16)
    wuqs = _rope_slot(wq3[..., QK_NOPE:], True).reshape(Q_LORA, -1).astype(BF16)
    wkv3 = w_ukv[i].reshape(KV_LORA, N_HEADS, QK_NOPE + V_HEAD)
    wuk = jnp.pad(wkv3[..., :QK_NOPE], tail).reshape(KV_LORA, -1).astype(BF16)
    wuvt = wkv3[..., QK_NOPE:].reshape(KV_LORA, -1).T.astype(BF16)
    nm, qn, kvn = norm_mix[i][None], q_norm[i][None], kv_norm[i][None]

    w_c = jnp.concatenate([w_kv, _rope_slot(w_pe, False)], axis=1).astype(BF16)
    b_c = jnp.concatenate([b_kv, _rope_slot(b_pe, False)], axis=1)
    ck, cvt = _ctx_call(ctx, modc, nm, w_c, b_c, kvn, wuk, wuvt)

    cos_t, sin_t = _rope_tables(s)
    q, k, vt, hv, hx1, hx2, gate = _inproj_call(
        x, mod, nm, wa, ba, w_hy.astype(BF16), b_hy, w_g.astype(BF16), b_g, qn, wuq, wuqs, kvn, wuk, wuvt,
        cos_t, sin_t, hy_conv_w[i], hy_conv_b[i][None], tl["inproj"])

    attn = _attn_call(q, jnp.concatenate([ck, k], axis=2), jnp.concatenate([cvt, vt], axis=3), tl["tq"], tl["tk"])
    hy = _hyena(hv, hx1, hx2, hy_filt_w1[i], hy_filt_b1[i], hy_filt_w2[i], hy_filt_b2[i], hy_filt_w3[i],
                hy_filt_freq[i], hy_skip[i], tl["fft_kb"])
    xm = _merge_call(x, attn, hy, gate, mod, w_branch_attn[i].astype(BF16), w_branch_hyena[i].astype(BF16),
                     w_out[i].astype(BF16), tl["merge"])
    return _moe(xm, mod, norm_ffn[i][None], w_router[i], router_bias[i], w_exp_gate[i], w_exp_up[i], w_exp_down[i],
                w_sh_gate[i], w_sh_up[i], w_sh_down[i], final_norm[None], tl["moe"])
```

```python
import functools
import math

import numpy as np
import jax
import jax.numpy as jnp
from jax import lax
from jax.experimental import pallas as pl
from jax.experimental.pallas import tpu as pltpu

GRID_W = 64
N_HEADS = 8
QK_NOPE = 64
QK_ROPE = 32
V_HEAD = 64
Q_LORA = 256
KV_LORA = 128
ROPE_THETA = 10000.0
ATTN_SCALE = 1.0 / math.sqrt(QK_NOPE + QK_ROPE)
HY_WIDTH = 512
HY_ORDER = 2
HY_SHORT = 3
HY_BANDS = 8
HY_EMB = 1 + 2 * HY_BANDS
HY_EMB_PAD = 32
HY_FAST_DECAY = 0.3
HY_SLOW_DECAY = 1.5
HY_DECAY_TARGET = 1e-2
N_EXPERTS = 64
N_GROUPS = 8
GROUP_SIZE = N_EXPERTS // N_GROUPS
TOPK_GROUPS = 4
TOP_K = 8
EXPERT_FF = 256
ROUTE_SCALE = 2.5
EXPERT_BLOCK = 1024
RUN_ALIGN = 8
ROUTE_TILES = 8
LONG_RUN = 8
NORM_EPS = 1e-6

HEAD_PAD = 128
Q_CHUNK = 512
AHEAD = 2
LANES = 128
SUBLANES = 8
VMEM_LIMIT = 48 * 1024 * 1024

F32 = jnp.float32
BF16 = jnp.bfloat16
NT_DIMS = (((1,), (1,)), ((), ()))
NN_DIMS = (((1,), (0,)), ((), ()))


def _params(*sem):
    return pltpu.CompilerParams(dimension_semantics=sem, vmem_limit_bytes=VMEM_LIMIT)


def _dot(a, b):
    return jnp.dot(a.astype(BF16), b.astype(BF16), preferred_element_type=F32)


def _split(a):
    hi = a.astype(BF16)
    lo = (a - hi.astype(F32)).astype(BF16)
    return hi, lo


def _dot3(a, b, dims=NN_DIMS):
    ah, al = _split(a)
    bh, bl = _split(b)
    d = functools.partial(lax.dot_general, dimension_numbers=dims, preferred_element_type=F32)
    return d(ah, bh) + (d(ah, bl) + d(al, bh))


def _rms(x, g):
    return x * lax.rsqrt(jnp.mean(x * x, axis=-1, keepdims=True) + NORM_EPS) * g


def _silu(x):
    return x * jax.nn.sigmoid(x)


def _full(shape):
    nd = len(shape)
    return pl.BlockSpec(shape, lambda *_: (0,) * nd)


def _mod_kernel(c_ref, w_ref, b_ref, o_ref):
    o_ref[...] = _dot3(_silu(c_ref[...]), w_ref[...]) + b_ref[...]


def _mod_call(c_rows, w_mod, b_mod):
    r, d = c_rows.shape
    n = w_mod.shape[1]
    bn = 1024
    return pl.pallas_call(
        _mod_kernel,
        grid=(n // bn,),
        in_specs=[_full((r, d)), pl.BlockSpec((d, bn), lambda j: (0, j)), pl.BlockSpec((1, bn), lambda j: (0, j))],
        out_specs=pl.BlockSpec((r, bn), lambda j: (0, j)),
        out_shape=jax.ShapeDtypeStruct((r, n), F32),
        compiler_params=_params("arbitrary"),
        name="mod",
    )(c_rows, w_mod, b_mod.reshape(1, n))


def _prenorm(x, mod_ref, row, g):
    shift = mod_ref[0, row:row + 1, :]
    scale = mod_ref[0, row + 1:row + 2, :]
    return _rms(x, g) * (1.0 + scale) + shift


def _kv_heads(kv_lat, kpe, kvn_ref, wuk_ref, wuvt_ref, k_out, vt_out):
    kvn = _rms(kv_lat, kvn_ref[...]).astype(BF16)
    kk = _dot(kvn, wuk_ref[...])
    vt = lax.dot_general(wuvt_ref[...], kvn, NT_DIMS, preferred_element_type=F32)
    ones = jnp.ones((HEAD_PAD - V_HEAD, vt.shape[1]), F32)
    for h in range(N_HEADS):
        k_out[0, h] = (kk[:, HEAD_PAD * h:HEAD_PAD * (h + 1)] + kpe).astype(BF16)
        vt_out[0, h] = jnp.concatenate([vt[V_HEAD * h:V_HEAD * (h + 1)], ones], axis=0).astype(BF16)


def _ctx_kernel(c_ref, mod_ref, nm_ref, w_ref, b_ref, kvn_ref, wuk_ref, wuv_ref, k_out, v_out):
    h = _prenorm(c_ref[0], mod_ref, 0, nm_ref[...]).astype(BF16)
    a = _dot(h, w_ref[...]) + b_ref[...]
    _kv_heads(a[:, :KV_LORA], a[:, KV_LORA:], kvn_ref, wuk_ref, wuv_ref, k_out, v_out)


def _ctx_call(ctx, modc, norm_mix, w_c, b_c, kv_norm, w_uk, w_uv):
    bsz, n, d = ctx.shape
    return pl.pallas_call(
        _ctx_kernel,
        grid=(bsz,),
        in_specs=[pl.BlockSpec((1, n, d), lambda b: (b, 0, 0)), _full(modc.shape), _full(norm_mix.shape),
                  _full(w_c.shape), _full(b_c.shape), _full(kv_norm.shape), _full(w_uk.shape), _full(w_uv.shape)],
        out_specs=[pl.BlockSpec((1, N_HEADS, n, HEAD_PAD), lambda b: (b, 0, 0, 0)),
                   pl.BlockSpec((1, N_HEADS, HEAD_PAD, n), lambda b: (b, 0, 0, 0))],
        out_shape=[jax.ShapeDtypeStruct((bsz, N_HEADS, n, HEAD_PAD), BF16),
                   jax.ShapeDtypeStruct((bsz, N_HEADS, HEAD_PAD, n), BF16)],
        compiler_params=_params("arbitrary"),
        name="ctx",
    )(ctx, modc, norm_mix, w_c, b_c, kv_norm, w_uk, w_uv)


def _inproj_kernel(x_ref, xp_ref, xn_ref, mod_ref, nm_ref, wa_ref, ba_ref, why_ref, bhy_ref, wg_ref, bg_ref,
                   qn_ref, wuq_ref, wuqs_ref, kvn_ref, wuk_ref, wuv_ref, cos_ref, sin_ref, cw_ref, cb_ref,
                   q_out, k_out, v_out, hv_out, hx1_out, hx2_out, g_out):
    i = pl.program_id(0)
    tm = x_ref.shape[1]
    nm = nm_ref[...]
    h = _prenorm(x_ref[0], mod_ref, 0, nm).astype(BF16)
    a = _dot(h, wa_ref[...]) + ba_ref[...]
    q_lat = a[:, :Q_LORA]
    kv_lat = a[:, Q_LORA:Q_LORA + KV_LORA]
    kpe_m = a[:, Q_LORA + KV_LORA:Q_LORA + KV_LORA + HEAD_PAD]
    kpe_s = a[:, Q_LORA + KV_LORA + HEAD_PAD:]
    cos = cos_ref[...]
    sin = sin_ref[...]
    qn = _rms(q_lat, qn_ref[...]).astype(BF16)
    qa = _dot(qn, wuq_ref[...])
    qs = _dot(qn, wuqs_ref[...])
    for hh in range(N_HEADS):
        sl = slice(HEAD_PAD * hh, HEAD_PAD * (hh + 1))
        q_out[0, hh] = (qa[:, sl] * cos + qs[:, sl] * sin).astype(BF16)
    _kv_heads(kv_lat, kpe_m * cos + kpe_s * sin, kvn_ref, wuk_ref, wuv_ref, k_out, v_out)
    g_out[0] = (_dot(h, wg_ref[...]) + bg_ref[...]).astype(BF16)

    why = why_ref[...]
    bhy = bhy_ref[...]
    halo = jnp.concatenate([_prenorm(xp_ref[0], mod_ref, 0, nm), _prenorm(xn_ref[0], mod_ref, 0, nm)], axis=0)
    hy_all = _dot(jnp.concatenate([h, halo.astype(BF16)], axis=0), why) + bhy
    hy = hy_all[:tm]
    prev = jnp.where(i == 0, 0.0, hy_all[tm + SUBLANES - 1:tm + SUBLANES])
    nxt = jnp.where(i == pl.num_programs(0) - 1, 0.0, hy_all[tm + SUBLANES:tm + SUBLANES + 1])
    rid = lax.broadcasted_iota(jnp.int32, (tm, 1), 0)
    up = jnp.where(rid == 0, prev, pltpu.roll(hy, 1, 0))
    dn = jnp.where(rid == tm - 1, nxt, pltpu.roll(hy, tm - 1, 0))
    u = up * cw_ref[0:1, :] + hy * cw_ref[1:2, :] + dn * cw_ref[2:3, :] + cb_ref[...]
    hv_out[0] = u[:, :HY_WIDTH]
    hx1_out[0] = u[:, HY_WIDTH:2 * HY_WIDTH]
    hx2_out[0] = u[:, 2 * HY_WIDTH:]


def _inproj_call(x, mod, norm_mix, wa, ba, why, bhy, wg, bg, q_norm, wuq, wuqs, kv_norm, wuk, wuvt, cos_t, sin_t, cw,
                 cb, tm):
    bsz, s, d = x.shape
    nt = s // tm
    rb = tm // SUBLANES
    last_rb = s // SUBLANES - 1
    consts = [norm_mix, wa, ba, why, bhy, wg, bg, q_norm, wuq, wuqs, kv_norm, wuk, wuvt]
    in_specs = [
        pl.BlockSpec((1, tm, d), lambda i, b: (b, i, 0)),
        pl.BlockSpec((1, SUBLANES, d), lambda i, b: (b, jnp.maximum(i * rb - 1, 0), 0)),
        pl.BlockSpec((1, SUBLANES, d), lambda i, b: (b, jnp.minimum((i + 1) * rb, last_rb), 0)),
        pl.BlockSpec((1, SUBLANES, d), lambda i, b: (b, 0, 0)),
    ] + [_full(c.shape) for c in consts] + [
        pl.BlockSpec((tm, HEAD_PAD), lambda i, b: (i, 0)),
        pl.BlockSpec((tm, HEAD_PAD), lambda i, b: (i, 0)),
        _full(cw.shape), _full(cb.shape),
    ]
    hw = HY_WIDTH
    out_specs = [
        pl.BlockSpec((1, N_HEADS, tm, HEAD_PAD), lambda i, b: (b, 0, i, 0)),
        pl.BlockSpec((1, N_HEADS, tm, HEAD_PAD), lambda i, b: (b, 0, i, 0)),
        pl.BlockSpec((1, N_HEADS, HEAD_PAD, tm), lambda i, b: (b, 0, 0, i)),
        pl.BlockSpec((1, tm, hw), lambda i, b: (b, i, 0)),
        pl.BlockSpec((1, tm, hw), lambda i, b: (b, i, 0)),
        pl.BlockSpec((1, tm, hw), lambda i, b: (b, i, 0)),
        pl.BlockSpec((1, tm, 2 * d), lambda i, b: (b, i, 0)),
    ]
    out_shape = [
        jax.ShapeDtypeStruct((bsz, N_HEADS, s, HEAD_PAD), BF16),
        jax.ShapeDtypeStruct((bsz, N_HEADS, s, HEAD_PAD), BF16),
        jax.ShapeDtypeStruct((bsz, N_HEADS, HEAD_PAD, s), BF16),
        jax.ShapeDtypeStruct((bsz, s, hw), F32),
        jax.ShapeDtypeStruct((bsz, s, hw), F32),
        jax.ShapeDtypeStruct((bsz, s, hw), F32),
        jax.ShapeDtypeStruct((bsz, s, 2 * d), BF16),
    ]
    return pl.pallas_call(
        _inproj_kernel,
        grid=(nt, bsz),
        in_specs=in_specs,
        out_specs=out_specs,
        out_shape=out_shape,
        compiler_params=_params("arbitrary", "arbitrary"),
        name="inproj",
    )(x, x, x, mod, *consts, cos_t, sin_t, cw, cb)


def _attn_kernel(q_ref, k_ref, vt_ref, o_ref, m_sc, acc_sc):
    j = pl.program_id(2)

    @pl.when(j == 0)
    def _():
        m_sc[...] = jnp.full(m_sc.shape, -jnp.inf, F32)
        acc_sc[...] = jnp.zeros(acc_sc.shape, F32)

    tq = q_ref.shape[2]
    qw = min(tq, Q_CHUNK)
    units = [(h, c) for h in range(N_HEADS) for c in range(0, tq, qw)]

    def scores(u):
        h, c = units[u]
        return lax.dot_general(k_ref[0, h], q_ref[0, h, c:c + qw, :], NT_DIMS,
                               preferred_element_type=F32)

    pending = [scores(u) for u in range(AHEAD)]
    for u, (h, c) in enumerate(units):
        if u + AHEAD < len(units):
            pending.append(scores(u + AHEAD))
        st = pending.pop(0)
        m_prev = m_sc[h, :, c:c + qw]
        m_new = jnp.maximum(m_prev, jnp.max(st, axis=0, keepdims=True))
        pt = jnp.exp2(st - m_new).astype(BF16)
        acc_sc[h, :, c:c + qw] = (jnp.exp2(m_prev - m_new) * acc_sc[h, :, c:c + qw]
                                  + jnp.dot(vt_ref[0, h], pt, preferred_element_type=F32))
        m_sc[h, :, c:c + qw] = m_new

    @pl.when(j == pl.num_programs(2) - 1)
    def _():
        ot = jnp.concatenate([acc_sc[h, :V_HEAD] / acc_sc[h, V_HEAD:V_HEAD + 1] for h in range(N_HEADS)], axis=0)
        o_ref[0] = ot.T.astype(o_ref.dtype)


def _attn_call(q, k, vt, tq, tk):
    bsz, nh, s, dh = q.shape
    nk = k.shape[2]
    dv = nh * V_HEAD
    return pl.pallas_call(
        _attn_kernel,
        grid=(bsz, s // tq, nk // tk),
        in_specs=[
            pl.BlockSpec((1, nh, tq, dh), lambda b, i, j: (b, 0, i, 0)),
            pl.BlockSpec((1, nh, tk, dh), lambda b, i, j: (b, 0, j, 0)),
            pl.BlockSpec((1, nh, dh, tk), lambda b, i, j: (b, 0, 0, j)),
        ],
        out_specs=pl.BlockSpec((1, tq, dv), lambda b, i, j: (b, i, 0)),
        out_shape=jax.ShapeDtypeStruct((bsz, s, dv), BF16),
        scratch_shapes=[pltpu.VMEM((nh, 1, tq), F32), pltpu.VMEM((nh, dh, tq), F32)],
        compiler_params=_params("arbitrary", "arbitrary", "arbitrary"),
        name="attn",
    )(q, k, vt)


def _filter_kernel(emb_ref, w1_ref, b1_ref, w2_ref, b2_ref, w3_ref, fr_ref, dl_ref, full_out, asum_out, *, seq):
    r = pl.program_id(0)
    rb = emb_ref.shape[0]
    emb = emb_ref[...]
    fr = fr_ref[...]
    h = jnp.sin(fr * (_dot3(emb, w1_ref[...]) + b1_ref[...]))
    h = jnp.sin(fr * (_dot3(h, w2_ref[...]) + b2_ref[...]))
    k = _dot3(h, w3_ref[0]) * jnp.exp(-emb[:, 0:1] * dl_ref[...])
    row = r * rb + lax.broadcasted_iota(jnp.int32, (rb, 1), 0)
    k = jnp.where(row == seq, 0.0, k)
    full_out[...] = k

    @pl.when(r == 0)
    def _():
        asum_out[...] = jnp.zeros(asum_out.shape, F32)

    asum_out[...] += jnp.sum(jnp.abs(k), axis=0, keepdims=True)


def _filter_call(emb, w1, b1, w2, b2, w3sel, freq, deltas2, seq, rb):
    n2 = emb.shape[0]
    half_blocks = seq // rb
    width = w3sel.shape[2]
    return pl.pallas_call(
        functools.partial(_filter_kernel, seq=seq),
        grid=(n2 // rb,),
        in_specs=[pl.BlockSpec((rb, HY_EMB_PAD), lambda r: (r, 0)), _full(w1.shape), _full(b1.shape),
                  _full(w2.shape), _full(b2.shape),
                  pl.BlockSpec((1,) + w3sel.shape[1:], lambda r: (r // half_blocks, 0, 0)),
                  _full(freq.shape), _full(deltas2.shape)],
        out_specs=[pl.BlockSpec((rb, width), lambda r: (r, 0)), pl.BlockSpec((1, width), lambda r: (0, 0))],
        out_shape=[jax.ShapeDtypeStruct((n2, width), F32), jax.ShapeDtypeStruct((1, width), F32)],
        compiler_params=_params("arbitrary"),
        name="filt",
    )(emb, w1, b1, w2, b2, w3sel, freq, deltas2)


def _fa_kernel(u_ref, f_ref, a_out):
    two, _, hn, g, c = u_ref.shape
    a = _dot(f_ref[...], u_ref[...].reshape(two * hn * g, c))
    a_out[...] = (_pack(a) if a_out.dtype == jnp.uint32 else a).reshape(a_out.shape)


def _fa_call(u5, fmat, packed):
    _, p, hn, n, c = u5.shape
    g = SUBLANES
    co, dt = (c // 2, jnp.uint32) if packed else (c, F32)
    return pl.pallas_call(
        _fa_kernel,
        grid=(p, n // g),
        in_specs=[pl.BlockSpec((2, 1, hn, g, c), lambda q, j: (0, q, 0, j, 0)), _full(fmat.shape)],
        out_specs=pl.BlockSpec((1, 2, n, g, co), lambda q, j: (q, 0, 0, j, 0)),
        out_shape=jax.ShapeDtypeStruct((p, 2, n, n, co), dt),
        compiler_params=_params("arbitrary", "arbitrary"),
        name="fa",
    )(u5, fmat)


def _dot_packed(w, u):
    lo, hi = _unpack(u)
    return jnp.concatenate([jnp.dot(w, lo, preferred_element_type=F32), jnp.dot(w, hi, preferred_element_type=F32)],
                           axis=1)


def _fb_kernel(a_ref, g_ref, asum_ref, kf_out):
    _, two, kb, n, c = a_ref.shape
    scale = 1.0 / (asum_ref[...] + 1e-6)
    for kk in range(kb):
        x = _dot(g_ref[kk], a_ref[0, :, kk].reshape(two * n, c)) * scale
        kf_out[kk] = x.reshape(two, n, c)


def _fb_call(a5, gmat, asum, kb):
    _, _, n, _, c = a5.shape
    return pl.pallas_call(
        _fb_kernel,
        grid=(n // kb,),
        in_specs=[pl.BlockSpec((1, 2, kb, n, c), lambda k: (0, 0, k, 0, 0)),
                  pl.BlockSpec((kb, 2 * n, 2 * n), lambda k: (k, 0, 0)), _full(asum.shape)],
        out_specs=pl.BlockSpec((kb, 2, n, c), lambda k: (k, 0, 0, 0)),
        out_shape=jax.ShapeDtypeStruct((n, 2, n, c), F32),
        compiler_params=_params("arbitrary"),
        name="fb",
    )(a5, gmat, asum)


def _mid_kernel(a_ref, g_ref, h_ref, kf_ref, b_out):
    _, two, kb, n, c = a_ref.shape
    for kk in range(kb):
        x = _dot_packed(g_ref[kk], a_ref[0, :, kk].reshape(two * n, c))
        xr, xi = x[:n], x[n:]
        kr, ki = kf_ref[kk, 0], kf_ref[kk, 1]
        y = jnp.concatenate([xr * kr - xi * ki, xr * ki + xi * kr], axis=0)
        b_out[0, :, kk] = _pack(_dot(h_ref[kk], y)).reshape(two, n, c)


def _mid_call(a5, gmat, hmat, kf, order, kb):
    p, _, n, _, c = a5.shape
    return pl.pallas_call(
        _mid_kernel,
        grid=(n // kb, p),
        in_specs=[pl.BlockSpec((1, 2, kb, n, c), lambda k, q: (q, 0, k, 0, 0)),
                  pl.BlockSpec((kb, 2 * n, 2 * n), lambda k, q: (k, 0, 0)),
                  pl.BlockSpec((kb, 2 * n, 2 * n), lambda k, q: (k, 0, 0)),
                  pl.BlockSpec((kb, 2, n, 2 * c), lambda k, q: (k, 0, 0, order))],
        out_specs=pl.BlockSpec((1, 2, kb, n, c), lambda k, q: (q, 0, k, 0, 0)),
        out_shape=jax.ShapeDtypeStruct(a5.shape, jnp.uint32),
        compiler_params=_params("arbitrary", "arbitrary"),
        name="mid",
    )(a5, gmat, hmat, kf)


def _fc_kernel(b_ref, f_ref, u_ref, m_ref, skip_ref, o_out):
    _, two, n, g, c = b_ref.shape
    y = _dot_packed(f_ref[...], b_ref[...].reshape(two * n * g, c)).reshape(u_ref.shape)
    o_out[...] = m_ref[...] * (y + u_ref[...] * skip_ref[...])


def _fc_call(b5, finv, u5, m5, skip_row):
    _, p, hn, n, c = u5.shape
    g = SUBLANES
    blk = pl.BlockSpec((2, 1, hn, g, c), lambda q, j: (0, q, 0, j, 0))
    return pl.pallas_call(
        _fc_kernel,
        grid=(p, n // g),
        in_specs=[pl.BlockSpec((1, 2, n, g, c // 2), lambda q, j: (q, 0, 0, j, 0)), _full(finv.shape), blk, blk,
                  _full(skip_row.shape)],
        out_specs=blk,
        out_shape=jax.ShapeDtypeStruct(u5.shape, F32),
        compiler_params=_params("arbitrary", "arbitrary"),
        name="fc",
    )(b5, finv, u5, m5, skip_row)


def _dft_tables(n):
    hn = n // 2
    k = np.arange(n)[:, None]
    ang = -2.0 * np.pi * (k * np.arange(n)[None, :] % n) / n
    fr, fi = np.cos(ang), np.sin(ang)
    f_data = np.block([[fr[:, :hn], -fi[:, :hn]], [fi[:, :hn], fr[:, :hn]]])
    f_filt = np.concatenate([fr, fi], axis=0)
    er, ei = fr[:hn], -fi[:hn]
    f_inv = np.block([[er, -ei], [ei, er]]) / float(n * n)
    k1 = np.arange(n)[:, None, None]
    k2 = np.arange(n)[None, :, None]
    m2 = np.arange(n)[None, None, :]
    ang2 = -2.0 * np.pi * ((m2 * (k1 + n * k2)) % (n * n)) / (n * n)
    gr, gi = np.cos(ang2).astype(np.float32), np.sin(ang2).astype(np.float32)
    g = np.concatenate([np.concatenate([gr, -gi], axis=2), np.concatenate([gi, gr], axis=2)], axis=1)
    h = np.swapaxes(g, 1, 2)

    def widen(f):
        return np.kron(f, np.eye(SUBLANES))

    return tuple(jnp.asarray(a, BF16) for a in (widen(f_data), widen(f_filt), widen(f_inv), g, h))


def _hyena_filter_tables(seq):
    f32 = np.float32
    t = np.linspace(0.0, 1.0, seq, dtype=f32)[:, None]
    w = (f32(2.0 * math.pi) * np.arange(seq, dtype=f32)[:, None] / f32(seq)).astype(f32)
    f = np.linspace(1e-4, HY_BANDS - 1, HY_BANDS, dtype=f32)[None, :]
    emb = np.concatenate([t, np.cos(f * w), -np.sin(f * w)], axis=-1).astype(f32)
    emb = np.concatenate([emb, emb[:1], emb[:0:-1]], axis=0)
    emb = np.pad(emb, ((0, 0), (0, HY_EMB_PAD - HY_EMB)))
    deltas = np.abs(np.linspace(math.log(HY_DECAY_TARGET) / HY_SLOW_DECAY,
                                math.log(HY_DECAY_TARGET) / HY_FAST_DECAY, HY_WIDTH, dtype=f32))
    return jnp.asarray(emb), jnp.asarray(np.tile(deltas, HY_ORDER)[None, :])


def _hyena(hv, hx1, hx2, w1, b1, w2, b2, w3, freq, skip, kb):
    bsz, seq, c = hv.shape
    n = int(round(math.sqrt(2 * seq)))
    assert n * n == 2 * seq and bsz % 2 == 0
    hn, p = n // 2, bsz // 2
    f_data, f_filt, f_inv, gmat, hmat = _dft_tables(n)

    emb, deltas2 = _hyena_filter_tables(seq)
    w1p = jnp.pad(w1, ((0, HY_EMB_PAD - HY_EMB), (0, 0)))
    w3r = w3.reshape(w3.shape[0], HY_ORDER, 2, c)
    w3sel = jnp.stack([w3r[:, :, 0, :].reshape(-1, HY_ORDER * c), w3r[:, :, 1, :].reshape(-1, HY_ORDER * c)])
    full, asum = _filter_call(emb, w1p, b1[None], w2, b2[None], w3sel, freq[None], deltas2, seq, min(512, seq))
    c2 = HY_ORDER * c
    kf = _fb_call(_fa_call(full.reshape(2, 1, hn, n, c2), f_filt, False), gmat, asum, kb // HY_ORDER)

    def view(t):
        return t.reshape(2, p, hn, n, c)

    def long_conv(u5, m5, order):
        bm = _mid_call(_fa_call(u5, f_data, True), gmat, hmat, kf, order, kb)
        return _fc_call(bm, f_inv, u5, m5, skip[order][None, :])

    z = long_conv(view(hv), view(hx1), 0)
    return long_conv(z, view(hx2), 1).reshape(bsz, seq, c)


def _merge_kernel(x_ref, at_ref, hy_ref, g_ref, mod_ref, wba_ref, wbh_ref, wo_ref, o_ref):
    d = x_ref.shape[2]
    g = g_ref[0].astype(F32)
    y = (jax.nn.sigmoid(g[:, :d]) * _dot(at_ref[0], wba_ref[...])
         + jax.nn.sigmoid(g[:, d:]) * _dot(hy_ref[0], wbh_ref[...]))
    o_ref[0] = x_ref[0] + mod_ref[0, 2:3, :] * _dot(y, wo_ref[...])


def _merge_call(x, attn, hy, gate, mod, wba, wbh, wo, tm):
    bsz, s, d = x.shape

    def tok(w):
        return pl.BlockSpec((1, tm, w), lambda b, i: (b, i, 0))

    return pl.pallas_call(
        _merge_kernel,
        grid=(bsz, s // tm),
        in_specs=[tok(d), tok(attn.shape[2]), tok(hy.shape[2]), tok(2 * d),
                  pl.BlockSpec((1, SUBLANES, d), lambda b, i: (b, 0, 0)),
                  _full(wba.shape), _full(wbh.shape), _full(wo.shape)],
        out_specs=tok(d),
        out_shape=jax.ShapeDtypeStruct((bsz, s, d), F32),
        compiler_params=_params("arbitrary", "arbitrary"),
        name="merge",
    )(x, attn, hy, gate, mod, wba, wbh, wo)


def _route_kernel(xm_ref, mod_ref, nf_ref, wrt_ref, rb_ref, tri_ref, lt_ref, h2_out, w_out, p_out, col_out, row_out):
    tm = xm_ref.shape[1]
    ng, gs = N_GROUPS, GROUP_SIZE

    h2 = _prenorm(xm_ref[0], mod_ref, 3, nf_ref[...])
    h2_out[0] = h2.astype(h2_out.dtype)
    scores = jax.nn.sigmoid(_dot3(wrt_ref[...], h2, NT_DIMS))
    sel = scores + rb_ref[...]
    slabs = [sel[ng * j:ng * (j + 1)] for j in range(gs)]

    top1 = jnp.full((ng, tm), -jnp.inf, F32)
    top2 = top1
    for x in slabs:
        top2 = jnp.maximum(top2, jnp.minimum(top1, x))
        top1 = jnp.maximum(top1, x)
    gscore = top1 + top2
    gid = lax.broadcasted_iota(jnp.int32, (ng, 1), 0)
    rank = jnp.zeros((ng, tm), jnp.int32)
    for g2 in range(ng):
        row = gscore[g2:g2 + 1]
        beats = (row > gscore) | ((row == gscore) & (g2 < gid))
        rank = rank + beats.astype(jnp.int32)
    gmask = rank < TOPK_GROUPS

    cand = [jnp.where(gmask, x, -jnp.inf) for x in slabs]
    eid = [gid * gs + j for j in range(gs)]
    chosen = []
    for _ in range(TOP_K):
        best = functools.reduce(jnp.maximum, cand)
        best = jnp.max(best, axis=0, keepdims=True)
        idx = functools.reduce(jnp.minimum, [jnp.where(cand[j] == best, eid[j], N_EXPERTS) for j in range(gs)])
        idx = jnp.min(idx, axis=0, keepdims=True)
        chosen.append(idx)
        cand = [jnp.where(eid[j] == idx, -jnp.inf, cand[j]) for j in range(gs)]

    mask = [functools.reduce(jnp.logical_or, [eid[j] == idx for idx in chosen]) for j in range(gs)]
    maskb = jnp.concatenate(mask, axis=0)
    wsel = jnp.where(maskb, scores, 0.0)
    w_out[...] = wsel / jnp.sum(wsel, axis=0, keepdims=True) * ROUTE_SCALE

    def extents(cnt, lower_sum):
        units = jnp.floor((cnt + (RUN_ALIGN - 1)) * (1.0 / RUN_ALIGN))
        start = RUN_ALIGN * lower_sum(units.astype(BF16))
        return start, start + RUN_ALIGN * units

    lane = lax.broadcasted_iota(jnp.int32, (N_EXPERTS, LANES), 1)
    sub = lax.broadcasted_iota(jnp.int32, (SUBLANES, N_EXPERTS), 0)
    ts = tri_ref.shape[0]
    for c in range(tm // ts):
        mb = maskb[:, c * ts:(c + 1) * ts]
        maskf = jnp.where(mb, 1.0, 0.0)
        mask16 = maskf.astype(BF16)
        before = jnp.dot(mask16, tri_ref[...], preferred_element_type=F32)
        p_out[:, c * ts:(c + 1) * ts] = jnp.where(mb, before, -1.0).astype(p_out.dtype)
        cnt_c = jnp.sum(maskf, axis=1, keepdims=True)
        start_c, end_c = extents(jnp.broadcast_to(cnt_c, (N_EXPERTS, LANES)),
                                 lambda u: jnp.dot(lt_ref[...], u, preferred_element_type=F32))
        col_out[c] = jnp.where(lane == 0, cnt_c, jnp.where(lane == 1, start_c, end_c))
        cnt_r = lax.dot_general(jnp.ones((SUBLANES, ts), BF16), mask16, NT_DIMS, preferred_element_type=F32)
        start_r, end_r = extents(cnt_r,
                                 lambda u: lax.dot_general(u, lt_ref[...], NT_DIMS, preferred_element_type=F32))
        row_out[c] = jnp.where(sub == 0, start_r, end_r)


def _route_call(xm, mod, norm_ffn, wrt, rbias, lower, ts, tiles_per_step):
    bsz, s, d = xm.shape
    t = bsz * s
    tm = ts * tiles_per_step
    nt = s // tm
    tri = (jnp.arange(ts)[:, None] < jnp.arange(ts)[None, :]).astype(BF16)
    tok = pl.BlockSpec((N_EXPERTS, tm), lambda i: (0, i))
    return pl.pallas_call(
        _route_kernel,
        grid=(t // tm,),
        in_specs=[pl.BlockSpec((1, tm, d), lambda i: (i // nt, i % nt, 0)),
                  pl.BlockSpec((1, SUBLANES, d), lambda i: (i // nt, 0, 0)),
                  _full(norm_ffn.shape), _full(wrt.shape), _full(rbias.shape), _full(tri.shape),
                  _full(lower.shape)],
        out_specs=[pl.BlockSpec((1, tm, d), lambda i: (i // nt, i % nt, 0)), tok, tok,
                   pl.BlockSpec((tiles_per_step, N_EXPERTS, LANES), lambda i: (i, 0, 0)),
                   pl.BlockSpec((tiles_per_step, SUBLANES, N_EXPERTS), lambda i: (i, 0, 0))],
        out_shape=[jax.ShapeDtypeStruct((bsz, s, d), BF16), jax.ShapeDtypeStruct((N_EXPERTS, t), F32),
                   jax.ShapeDtypeStruct((N_EXPERTS, t), BF16),
                   jax.ShapeDtypeStruct((t // ts, N_EXPERTS, LANES), F32),
                   jax.ShapeDtypeStruct((t // ts, SUBLANES, N_EXPERTS), F32)],
        compiler_params=_params("arbitrary"),
        name="route",
    )(xm, mod, norm_ffn, wrt, rbias, tri, lower)


def _pack(x):
    w = x.shape[1] // 2
    lo = lax.bitcast_convert_type(x[:, :w].astype(BF16).astype(F32), jnp.uint32)
    hi = lax.bitcast_convert_type(x[:, w:].astype(BF16).astype(F32), jnp.uint32)
    return hi | (lo >> 16)


def _unpack(u):
    lo = lax.bitcast_convert_type(u << 16, F32).astype(BF16)
    hi = lax.bitcast_convert_type(u & jnp.uint32(0xFFFF0000), F32).astype(BF16)
    return lo, hi


def _pow2_pieces(units, limit):
    bit = 1
    while bit * 2 <= limit:
        bit *= 2
    while bit:
        yield (units & bit) != 0, units & ~(2 * bit - 1), bit
        bit //= 2


def _rows_copy(vm_ref, hbm_ref, sem, vm_row, hbm_row, rows, to_hbm):
    v = vm_ref.at[pl.ds(pl.multiple_of(vm_row, RUN_ALIGN), rows), :]
    h = hbm_ref.at[pl.ds(pl.multiple_of(hbm_row, RUN_ALIGN), rows), :]
    return pltpu.make_async_copy(v, h, sem) if to_hbm else pltpu.make_async_copy(h, v, sem)


def _run_copies(vm_ref, hbm_ref, sem, n8, vm_row, hbm_row, limit, to_hbm, act):
    def emit(pieces):
        for on, off, size in pieces:
            @pl.when(on)
            def _():
                act(_rows_copy(vm_ref, hbm_ref, sem, vm_row + RUN_ALIGN * off, hbm_row + RUN_ALIGN * off,
                               RUN_ALIGN * size, to_hbm))

    pieces = list(_pow2_pieces(n8, limit))
    long_pieces = [p for p in pieces if p[2] >= LONG_RUN]
    if long_pieces:
        pl.when(n8 >= LONG_RUN)(lambda: emit(long_pieces))
    emit([p for p in pieces if p[2] < LONG_RUN])


def _wait_rows(vm_ref, hbm_ref, sem, units, limit, to_hbm):
    for on, _, size in _pow2_pieces(units, limit):
        @pl.when(on)
        def _():
            _rows_copy(vm_ref, hbm_ref, sem, 0, 0, RUN_ALIGN * size, to_hbm).wait()


def _dispatch_kernel(n8_ref, ls_ref, gs_ref, ts_ref, t8_ref, nu_ref, pos_ref, ext_ref, h_ref, xs_out, srt2, zbuf,
                     sems):
    step = pl.program_id(0)
    tm = h_ref.shape[0]
    rows = srt2.shape[1]
    slot = step % 2
    srt, sem = srt2.at[slot], sems.at[slot]
    rid = lax.broadcasted_iota(jnp.int32, (rows, 1), 0).astype(F32)
    start = ext_ref[0, 0:1, :]
    member = jnp.where((rid >= start) & (rid < ext_ref[0, 1:2, :]), 1.0, 0.0)
    offset = rid - jnp.sum(member * start, axis=1, keepdims=True)
    pos = jnp.dot(member.astype(BF16), pos_ref[...], preferred_element_type=F32)
    sel = jnp.where(pos == offset, 1.0, 0.0).astype(BF16)
    srt[...] = _pack(jnp.dot(sel, h_ref[...], preferred_element_type=F32))

    def send(e, c):
        i = step * N_EXPERTS + e
        _run_copies(srt, xs_out, sem, n8_ref[i], ls_ref[i], gs_ref[i], tm // RUN_ALIGN, True, lambda cp: cp.start())
        return c

    lax.fori_loop(0, N_EXPERTS, send, 0)

    def wait_tile(tile, s):
        last = tile * N_EXPERTS + N_EXPERTS - 1
        _wait_rows(srt2.at[s], xs_out, sems.at[s], ls_ref[last] // RUN_ALIGN + n8_ref[last], rows // RUN_ALIGN, True)

    pl.when(step > 0)(lambda: wait_tile(step - 1, 1 - slot))

    @pl.when(step == pl.num_programs(0) - 1)
    def _():
        wait_tile(step, slot)
        zbuf[...] = jnp.zeros(zbuf.shape, zbuf.dtype)
        nblk = xs_out.shape[0] // EXPERT_BLOCK

        def fill(act):
            def tails(e, c):
                _run_copies(zbuf, xs_out, sem, t8_ref[e], 0, ts_ref[e], EXPERT_BLOCK // RUN_ALIGN - 1, True, act)
                return c

            def blocks(b, c):
                act(pltpu.make_async_copy(
                    zbuf, xs_out.at[pl.ds(pl.multiple_of(b * EXPERT_BLOCK, EXPERT_BLOCK), EXPERT_BLOCK), :], sem))
                return c

            lax.fori_loop(0, N_EXPERTS, tails, 0)
            lax.fori_loop(nu_ref[0], nblk, blocks, 0)

        fill(lambda cp: cp.start())
        fill(lambda cp: cp.wait())


def _dispatch_call(tables, pos_et, ext_rows, h2, nblk, tm):
    t, d = h2.shape
    lrows = TOP_K * tm + N_EXPERTS * RUN_ALIGN
    return pl.pallas_call(
        _dispatch_kernel,
        grid_spec=pltpu.PrefetchScalarGridSpec(
            num_scalar_prefetch=len(tables), grid=(t // tm,),
            in_specs=[pl.BlockSpec((N_EXPERTS, tm), lambda i, *_: (0, i)),
                      pl.BlockSpec((1,) + ext_rows.shape[1:], lambda i, *_: (i, 0, 0)),
                      pl.BlockSpec((tm, d), lambda i, *_: (i, 0))],
            out_specs=pl.BlockSpec(memory_space=pl.ANY),
            scratch_shapes=[pltpu.VMEM((2, lrows, d // 2), jnp.uint32),
                            pltpu.VMEM((EXPERT_BLOCK, d // 2), jnp.uint32), pltpu.SemaphoreType.DMA((2,))]),
        out_shape=jax.ShapeDtypeStruct((nblk * EXPERT_BLOCK, d // 2), jnp.uint32),
        compiler_params=_params("arbitrary"),
        name="dispatch",
    )(*tables, pos_et, ext_rows, h2)


def _expert_kernel(blk_ref, nused_ref, x_ref, wg_ref, wu_ref, wd_ref, y_ref, wgu_sc, wd_sc):
    i = pl.program_id(0)
    used = i < nused_ref[0]

    @pl.when(used & ((i == 0) | (blk_ref[i] != blk_ref[jnp.maximum(i - 1, 0)])))
    def _():
        wgu_sc[:, :EXPERT_FF] = wg_ref[0].astype(BF16)
        wgu_sc[:, EXPERT_FF:] = wu_ref[0].astype(BF16)
        wd_sc[...] = wd_ref[0].astype(BF16)

    @pl.when(used)
    def _():
        lo, hi = _unpack(x_ref[...])
        half = lo.shape[1]
        gu = (jnp.dot(lo, wgu_sc[:half, :], preferred_element_type=F32)
              + jnp.dot(hi, wgu_sc[half:, :], preferred_element_type=F32))
        a = _silu(gu[:, :EXPERT_FF]) * gu[:, EXPERT_FF:]
        y_ref[...] = _pack(_dot(a, wd_sc[...]))

    @pl.when(jnp.logical_not(used))
    def _():
        y_ref[...] = jnp.zeros(y_ref.shape, y_ref.dtype)


def _expert_call(blk_e, nused, xs, wg, wu, wd):
    rows, d = xs.shape
    nblk = rows // EXPERT_BLOCK

    def row_map(i, blk, nu):
        return (jnp.minimum(i, nu[0] - 1), 0)

    def of_expert(w):
        return pl.BlockSpec((1,) + w.shape[1:], lambda i, blk, nu: (blk[i], 0, 0))

    return pl.pallas_call(
        _expert_kernel,
        grid_spec=pltpu.PrefetchScalarGridSpec(
            num_scalar_prefetch=2, grid=(nblk,),
            in_specs=[pl.BlockSpec((EXPERT_BLOCK, d), row_map), of_expert(wg), of_expert(wu), of_expert(wd)],
            out_specs=pl.BlockSpec((EXPERT_BLOCK, d), lambda i, blk, nu: (i, 0)),
            scratch_shapes=[pltpu.VMEM((wg.shape[1], 2 * EXPERT_FF), BF16), pltpu.VMEM(wd.shape[1:], BF16)]),
        out_shape=jax.ShapeDtypeStruct((rows, d), jnp.uint32),
        compiler_params=_params("arbitrary"),
        name="expert",
    )(blk_e, nused, xs, wg, wu, wd)


def _combine_kernel(n8_ref, ls_ref, gs_ref, ys_hbm, pos_ref, w_ref, ext_ref, xm_ref, h_ref, mod_ref, wsgu_ref,
                    wsd_ref, fn_ref, o_ref, ybuf2, sems):
    step = pl.program_id(0)
    tm = xm_ref.shape[0]
    rows = ybuf2.shape[1]
    slot = step % 2
    ybuf, sem = ybuf2.at[slot], sems.at[slot]

    def fetch(tile, s):
        def body(e, c):
            i = tile * N_EXPERTS + e
            _run_copies(ybuf2.at[s], ys_hbm, sems.at[s], n8_ref[i], ls_ref[i], gs_ref[i], tm // RUN_ALIGN, False,
                        lambda cp: cp.start())
            return c
        lax.fori_loop(0, N_EXPERTS, body, 0)

    pl.when(step == 0)(lambda: fetch(step, slot))
    pl.when(step + 1 < pl.num_programs(0))(lambda: fetch(step + 1, 1 - slot))
    gu = _dot(h_ref[...], wsgu_ref[...])
    ff = gu.shape[1] // 2
    shared = _dot(_silu(gu[:, :ff]) * gu[:, ff:], wsd_ref[...])
    cid = lax.broadcasted_iota(jnp.int32, (1, rows), 1).astype(F32)
    start = ext_ref[0, :, 1:2]
    member = jnp.where((cid >= start) & (cid < ext_ref[0, :, 2:3]), 1.0, 0.0)
    offset = cid - jnp.sum(member * start, axis=0, keepdims=True)
    member = member.astype(BF16)
    pos = jnp.dot(pos_ref[...], member, preferred_element_type=F32)
    mix = jnp.where(pos == offset, jnp.dot(w_ref[...].astype(BF16), member, preferred_element_type=F32), 0.0)
    mix = mix.astype(BF16)
    last = step * N_EXPERTS + N_EXPERTS - 1
    filled = ls_ref[last] + RUN_ALIGN * n8_ref[last]
    _wait_rows(ybuf, ys_hbm, sem, filled // RUN_ALIGN, rows // RUN_ALIGN, False)
    rid = lax.broadcasted_iota(jnp.int32, (rows, 1), 0)
    lo, hi = _unpack(jnp.where(rid < filled, ybuf[...], jnp.uint32(0)))
    routed = jnp.concatenate([jnp.dot(mix, lo, preferred_element_type=F32),
                              jnp.dot(mix, hi, preferred_element_type=F32)], axis=1)
    x = xm_ref[...] + mod_ref[0, 5:6, :] * (routed + shared)
    o_ref[...] = _rms(x, fn_ref[...])


def _combine_call(tables, ys, pos_te, w_te, ext_cols, xm, h2, mod, wsgu, wsd, final_norm, tm, tiles_per_batch):
    t, d = xm.shape
    lrows = TOP_K * tm + N_EXPERTS * RUN_ALIGN
    tok = pl.BlockSpec((tm, d), lambda i, *_: (i, 0))
    per_e = pl.BlockSpec((tm, N_EXPERTS), lambda i, *_: (i, 0))
    return pl.pallas_call(
        _combine_kernel,
        grid_spec=pltpu.PrefetchScalarGridSpec(
            num_scalar_prefetch=len(tables), grid=(t // tm,),
            in_specs=[pl.BlockSpec(memory_space=pl.ANY), per_e, per_e,
                      pl.BlockSpec((1,) + ext_cols.shape[1:], lambda i, *_: (i, 0, 0)), tok, tok,
                      pl.BlockSpec((1, SUBLANES, d), lambda i, *_: (i // tiles_per_batch, 0, 0)),
                      _full(wsgu.shape), _full(wsd.shape), _full(final_norm.shape)],
            out_specs=tok,
            scratch_shapes=[pltpu.VMEM((2, lrows, d // 2), jnp.uint32), pltpu.SemaphoreType.DMA((2,))]),
        out_shape=jax.ShapeDtypeStruct((t, d), F32),
        compiler_params=_params("arbitrary"),
        name="combine",
    )(*tables, ys, pos_te, w_te, ext_cols, xm, h2, mod, wsgu, wsd, final_norm)


def _moe(xm, mod, norm_ffn, w_router, router_bias, wg, wu, wd, wsg, wsu, wsd, final_norm, tm):
    bsz, s, d = xm.shape
    t = bsz * s
    nt = t // tm
    perm = (np.arange(N_EXPERTS) % N_GROUPS) * GROUP_SIZE + np.arange(N_EXPERTS) // N_GROUPS
    wrt = w_router.T[perm]
    rbias = router_bias[perm][:, None]
    lower = jnp.asarray(perm[None, :] < perm[:, None], BF16)
    h2, w_et, pos_et, ext_cols, ext_rows = _route_call(xm, mod, norm_ffn, wrt, rbias, lower, tm, ROUTE_TILES)

    inv = np.argsort(perm)
    n8 = (ext_cols[:, :, 0].astype(jnp.int32)[:, inv] + (RUN_ALIGN - 1)) // RUN_ALIGN
    run = RUN_ALIGN * n8
    ls = jnp.cumsum(run, axis=1) - run
    tot = jnp.sum(run, axis=0)
    padded = (tot + EXPERT_BLOCK - 1) // EXPERT_BLOCK * EXPERT_BLOCK
    pad_end = jnp.cumsum(padded)
    gs = (pad_end - padded)[None, :] + jnp.cumsum(run, axis=0) - run
    nblk = -(-(t * TOP_K + nt * N_EXPERTS * (RUN_ALIGN - 1)) // EXPERT_BLOCK) + N_EXPERTS
    blk_first = jnp.arange(nblk, dtype=jnp.int32)[:, None] * EXPERT_BLOCK
    blk_e = jnp.minimum(jnp.sum((pad_end[None, :] <= blk_first).astype(jnp.int32), axis=1), N_EXPERTS - 1)
    nused = (pad_end[-1:] // EXPERT_BLOCK).astype(jnp.int32)
    tables = [a.reshape(-1).astype(jnp.int32) for a in (n8, ls, gs)]
    tails = [(pad_end - padded + tot).astype(jnp.int32), ((padded - tot) // RUN_ALIGN).astype(jnp.int32), nused]

    h2f = h2.reshape(t, d)
    xs = _dispatch_call(tables + tails, pos_et, ext_rows, h2f, nblk, tm)
    ys = _expert_call(blk_e, nused, xs, wg, wu, wd)
    wsgu = jnp.concatenate([wsg, wsu], axis=1).astype(BF16)
    out = _combine_call(tables, ys, pos_et.T, w_et.T, ext_cols, xm.reshape(t, d), h2f, mod, wsgu, wsd.astype(BF16),
                        final_norm, tm, s // tm)
    return out.reshape(bsz, s, d)


def _rope_tables(s):
    f32 = np.float32
    rows = s // GRID_W
    row = np.broadcast_to(np.arange(rows, dtype=f32)[:, None], (rows, GRID_W)).reshape(-1)
    col = np.broadcast_to(np.arange(GRID_W, dtype=f32)[None, :], (rows, GRID_W)).reshape(-1)
    half = QK_ROPE // 2
    inv_freq = (f32(ROPE_THETA) ** (-np.arange(0, half, 2, dtype=f32) / f32(half))).astype(f32)
    ar, ac = row[:, None] * inv_freq, col[:, None] * inv_freq
    ones = np.ones((s, QK_NOPE), f32)
    tail = HEAD_PAD - QK_NOPE - QK_ROPE
    cos_t = np.concatenate([ones, np.cos(ar), np.cos(ar), np.cos(ac), np.cos(ac), np.ones((s, tail), f32)], 1)
    sin_t = np.concatenate([0 * ones, -np.sin(ar), np.sin(ar), -np.sin(ac), np.sin(ac), np.zeros((s, tail), f32)], 1)
    return jnp.asarray(cos_t, F32), jnp.asarray(sin_t, F32)


_Q4 = QK_ROPE // 4
ROPE_SWAP = np.concatenate([np.arange(_Q4, 2 * _Q4), np.arange(0, _Q4), np.arange(3 * _Q4, 4 * _Q4),
                            np.arange(2 * _Q4, 3 * _Q4)])


def _rope_slot(w, swap):
    if swap:
        w = w[..., ROPE_SWAP]
    pad = [(0, 0)] * (w.ndim - 1) + [(QK_NOPE, HEAD_PAD - QK_NOPE - QK_ROPE)]
    return jnp.pad(w, pad)


TILES = dict(inproj=512, tq=2048, tk=1408, fft_kb=8, merge=512, moe=256)


def kernel(x, c, ctx, c_ctx, w_mod, b_mod, norm_mix, norm_ffn, w_in, b_in, q_norm, w_uq, kv_norm, w_ukv, w_branch_attn, hy_conv_w, hy_conv_b, hy_filt_w1, hy_filt_b1, hy_filt_w2, hy_filt_b2, hy_filt_w3, hy_filt_freq, hy_skip, w_branch_hyena, w_out, w_router, router_bias, w_exp_gate, w_exp_up, w_exp_down, w_sh_gate, w_sh_up, w_sh_down, final_norm):
    bsz, s, d = x.shape
    tl = TILES
    assert w_mod.shape[0] == 1, "single-layer trunk"
    i = 0

    rows = -(-(bsz + 1) // SUBLANES) * SUBLANES
    c_rows = jnp.pad(jnp.concatenate([c, c_ctx[None]], axis=0), ((0, rows - bsz - 1), (0, 0)))
    mod_all = _mod_call(c_rows, w_mod[i], b_mod[i])
    mod_all = jnp.pad(mod_all.reshape(rows, 6, d), ((0, 0), (0, SUBLANES - 6), (0, 0)))
    mod, modc = mod_all[:bsz], mod_all[bsz:bsz + 1]

    cuts = np.cumsum([Q_LORA, KV_LORA, QK_ROPE, 3 * HY_WIDTH])
    wi, bi = w_in[i], b_in[i][None]
    w_q, w_kv, w_pe, w_hy, w_g = jnp.split(wi, cuts, axis=1)
    b_q, b_kv, b_pe, b_hy, b_g = jnp.split(bi, cuts, axis=1)
    wa = jnp.concatenate([w_q, w_kv, _rope_slot(w_pe, False), _rope_slot(w_pe, True)], axis=1).astype(BF16)
    ba = jnp.concatenate([b_q, b_kv, _rope_slot(b_pe, False), _rope_slot(b_pe, True)], axis=1)
    wq3 = w_uq[i].reshape(Q_LORA, N_HEADS, QK_NOPE + QK_ROPE) * (ATTN_SCALE * math.log2(math.e))
    tail = ((0, 0), (0, 0), (0, HEAD_PAD - QK_NOPE))
    wuq = (jnp.pad(wq3[..., :QK_NOPE], tail) + _rope_slot(wq3[..., QK_NOPE:], False)).reshape(Q_LORA, -1).astype(BF16)
    wuqs = _rope_slot(wq3[..., QK_NOPE:], True).reshape(Q_LORA, -1).astype(BF16)
    wkv3 = w_ukv[i].reshape(KV_LORA, N_HEADS, QK_NOPE + V_HEAD)
    wuk = jnp.pad(wkv3[..., :QK_NOPE], tail).reshape(KV_LORA, -1).astype(BF16)
    wuvt = wkv3[..., QK_NOPE:].reshape(KV_LORA, -1).T.astype(BF16)
    nm, qn, kvn = norm_mix[i][None], q_norm[i][None], kv_norm[i][None]

    w_c = jnp.concatenate([w_kv, _rope_slot(w_pe, False)], axis=1).astype(BF16)
    b_c = jnp.concatenate([b_kv, _rope_slot(b_pe, False)], axis=1)
    ck, cvt = _ctx_call(ctx, modc, nm, w_c, b_c, kvn, wuk, wuvt)

    cos_t, sin_t = _rope_tables(s)
    q, k, vt, hv, hx1, hx2, gate = _inproj_call(
        x, mod, nm, wa, ba, w_hy.astype(BF16), b_hy, w_g.astype(BF16), b_g, qn, wuq, wuqs, kvn, wuk, wuvt,
        cos_t, sin_t, hy_conv_w[i], hy_conv_b[i][None], tl["inproj"])

    attn = _attn_call(q, jnp.concatenate([ck, k], axis=2), jnp.concatenate([cvt, vt], axis=3), tl["tq"], tl["tk"])
    hy = _hyena(hv, hx1, hx2, hy_filt_w1[i], hy_filt_b1[i], hy_filt_w2[i], hy_filt_b2[i], hy_filt_w3[i],
                hy_filt_freq[i], hy_skip[i], tl["fft_kb"])
    xm = _merge_call(x, attn, hy, gate, mod, w_branch_attn[i].astype(BF16), w_branch_hyena[i].astype(BF16),
                     w_out[i].astype(BF16), tl["merge"])
    return _moe(xm, mod, norm_ffn[i][None], w_router[i], router_bias[i], w_exp_gate[i], w_exp_up[i], w_exp_down[i],
                w_sh_gate[i], w_sh_up[i], w_sh_down[i], final_norm[None], tl["moe"])
```

```python
import functools
import math

import numpy as np
import jax
import jax.numpy as jnp
from jax import lax
from jax.experimental import pallas as pl
from jax.experimental.pallas import tpu as pltpu

GRID_W = 64
N_HEADS = 8
QK_NOPE = 64
QK_ROPE = 32
V_HEAD = 64
Q_LORA = 256
KV_LORA = 128
ROPE_THETA = 10000.0
ATTN_SCALE = 1.0 / math.sqrt(QK_NOPE + QK_ROPE)
HY_WIDTH = 512
HY_ORDER = 2
HY_SHORT = 3
HY_BANDS = 8
HY_EMB = 1 + 2 * HY_BANDS
HY_EMB_PAD = 32
HY_FAST_DECAY = 0.3
HY_SLOW_DECAY = 1.5
HY_DECAY_TARGET = 1e-2
N_EXPERTS = 64
N_GROUPS = 8
GROUP_SIZE = N_EXPERTS // N_GROUPS
TOPK_GROUPS = 4
TOP_K = 8
EXPERT_FF = 256
ROUTE_SCALE = 2.5
EXPERT_BLOCK = 1024
RUN_ALIGN = 8
ROUTE_TILES = 8
LONG_RUN = 8
NORM_EPS = 1e-6

HEAD_PAD = 128
Q_CHUNK = 512
AHEAD = 2
LANES = 128
SUBLANES = 8
VMEM_LIMIT = 48 * 1024 * 1024

F32 = jnp.float32
BF16 = jnp.bfloat16
NT_DIMS = (((1,), (1,)), ((), ()))
NN_DIMS = (((1,), (0,)), ((), ()))


def _params(*sem):
    return pltpu.CompilerParams(dimension_semantics=sem, vmem_limit_bytes=VMEM_LIMIT)


def _dot(a, b):
    return jnp.dot(a.astype(BF16), b.astype(BF16), preferred_element_type=F32)


def _split(a):
    hi = a.astype(BF16)
    lo = (a - hi.astype(F32)).astype(BF16)
    return hi, lo


def _dot3(a, b, dims=NN_DIMS):
    ah, al = _split(a)
    bh, bl = _split(b)
    d = functools.partial(lax.dot_general, dimension_numbers=dims, preferred_element_type=F32)
    return d(ah, bh) + (d(ah, bl) + d(al, bh))


def _rms(x, g):
    return x * lax.rsqrt(jnp.mean(x * x, axis=-1, keepdims=True) + NORM_EPS) * g


def _silu(x):
    return x * jax.nn.sigmoid(x)


def _full(shape):
    nd = len(shape)
    return pl.BlockSpec(shape, lambda *_: (0,) * nd)


def _mod_kernel(c_ref, w_ref, b_ref, o_ref):
    o_ref[...] = _dot3(_silu(c_ref[...]), w_ref[...]) + b_ref[...]


def _mod_call(c_rows, w_mod, b_mod):
    r, d = c_rows.shape
    n = w_mod.shape[1]
    bn = 1024
    return pl.pallas_call(
        _mod_kernel,
        grid=(n // bn,),
        in_specs=[_full((r, d)), pl.BlockSpec((d, bn), lambda j: (0, j)), pl.BlockSpec((1, bn), lambda j: (0, j))],
        out_specs=pl.BlockSpec((r, bn), lambda j: (0, j)),
        out_shape=jax.ShapeDtypeStruct((r, n), F32),
        compiler_params=_params("arbitrary"),
        name="mod",
    )(c_rows, w_mod, b_mod.reshape(1, n))


def _prenorm(x, mod_ref, row, g):
    shift = mod_ref[0, row:row + 1, :]
    scale = mod_ref[0, row + 1:row + 2, :]
    return _rms(x, g) * (1.0 + scale) + shift


def _kv_heads(kv_lat, kpe, kvn_ref, wuk_ref, wuvt_ref, k_out, vt_out):
    kvn = _rms(kv_lat, kvn_ref[...]).astype(BF16)
    kk = _dot(kvn, wuk_ref[...])
    vt = lax.dot_general(wuvt_ref[...], kvn, NT_DIMS, preferred_element_type=F32)
    ones = jnp.ones((HEAD_PAD - V_HEAD, vt.shape[1]), F32)
    for h in range(N_HEADS):
        k_out[0, h] = (kk[:, HEAD_PAD * h:HEAD_PAD * (h + 1)] + kpe).astype(BF16)
        vt_out[0, h] = jnp.concatenate([vt[V_HEAD * h:V_HEAD * (h + 1)], ones], axis=0).astype(BF16)


def _ctx_kernel(c_ref, mod_ref, nm_ref, w_ref, b_ref, kvn_ref, wuk_ref, wuv_ref, k_out, v_out):
    j = pl.program_id(1)
    is_ctx = j == pl.num_programs(1) - 1

    @pl.when(is_ctx)
    def _():
        h = _prenorm(c_ref[0], mod_ref, 0, nm_ref[...]).astype(BF16)
        a = _dot(h, w_ref[...]) + b_ref[...]
        _kv_heads(a[:, :KV_LORA], a[:, KV_LORA:], kvn_ref, wuk_ref, wuv_ref, k_out, v_out)

    @pl.when(jnp.logical_not(is_ctx))
    def _():
        k_out[...] = jnp.zeros(k_out.shape, k_out.dtype)
        v_out[...] = jnp.zeros(v_out.shape, v_out.dtype)


def _ctx_call(ctx, modc, norm_mix, w_c, b_c, kv_norm, w_uk, w_uv, seq):
    bsz, n, d = ctx.shape
    nk = seq + n
    return pl.pallas_call(
        _ctx_kernel,
        grid=(bsz, nk // n),
        in_specs=[pl.BlockSpec((1, n, d), lambda b, j: (b, 0, 0)), _full(modc.shape), _full(norm_mix.shape),
                  _full(w_c.shape), _full(b_c.shape), _full(kv_norm.shape), _full(w_uk.shape), _full(w_uv.shape)],
        out_specs=[pl.BlockSpec((1, N_HEADS, n, HEAD_PAD), lambda b, j: (b, 0, j, 0)),
                   pl.BlockSpec((1, N_HEADS, HEAD_PAD, n), lambda b, j: (b, 0, 0, j))],
        out_shape=[jax.ShapeDtypeStruct((bsz, N_HEADS, nk, HEAD_PAD), BF16),
                   jax.ShapeDtypeStruct((bsz, N_HEADS, HEAD_PAD, nk), BF16)],
        compiler_params=_params("arbitrary", "arbitrary"),
        name="ctx",
    )(ctx, modc, norm_mix, w_c, b_c, kv_norm, w_uk, w_uv)


def _inproj_kernel(x_ref, xp_ref, xn_ref, mod_ref, nm_ref, wa_ref, ba_ref, why_ref, bhy_ref, wg_ref, bg_ref,
                   qn_ref, wuq_ref, wuqs_ref, kvn_ref, wuk_ref, wuv_ref, cos_ref, sin_ref, cw_ref, cb_ref,
                   k_buf, v_buf, q_out, k_out, v_out, hv_out, hx1_out, hx2_out, g_out):
    del k_buf, v_buf
    i = pl.program_id(0)
    tm = x_ref.shape[1]
    nm = nm_ref[...]
    h = _prenorm(x_ref[0], mod_ref, 0, nm).astype(BF16)
    a = _dot(h, wa_ref[...]) + ba_ref[...]
    q_lat = a[:, :Q_LORA]
    kv_lat = a[:, Q_LORA:Q_LORA + KV_LORA]
    kpe_m = a[:, Q_LORA + KV_LORA:Q_LORA + KV_LORA + HEAD_PAD]
    kpe_s = a[:, Q_LORA + KV_LORA + HEAD_PAD:]
    cos = cos_ref[...]
    sin = sin_ref[...]
    qn = _rms(q_lat, qn_ref[...]).astype(BF16)
    qa = _dot(qn, wuq_ref[...])
    qs = _dot(qn, wuqs_ref[...])
    for hh in range(N_HEADS):
        sl = slice(HEAD_PAD * hh, HEAD_PAD * (hh + 1))
        q_out[0, hh] = (qa[:, sl] * cos + qs[:, sl] * sin).astype(BF16)
    _kv_heads(kv_lat, kpe_m * cos + kpe_s * sin, kvn_ref, wuk_ref, wuv_ref, k_out, v_out)
    g_out[0] = (_dot(h, wg_ref[...]) + bg_ref[...]).astype(BF16)

    why = why_ref[...]
    bhy = bhy_ref[...]
    halo = jnp.concatenate([_prenorm(xp_ref[0], mod_ref, 0, nm), _prenorm(xn_ref[0], mod_ref, 0, nm)], axis=0)
    hy_all = _dot(jnp.concatenate([h, halo.astype(BF16)], axis=0), why) + bhy
    hy = hy_all[:tm]
    prev = jnp.where(i == 0, 0.0, hy_all[tm + SUBLANES - 1:tm + SUBLANES])
    nxt = jnp.where(i == pl.num_programs(0) - 1, 0.0, hy_all[tm + SUBLANES:tm + SUBLANES + 1])
    rid = lax.broadcasted_iota(jnp.int32, (tm, 1), 0)
    up = jnp.where(rid == 0, prev, pltpu.roll(hy, 1, 0))
    dn = jnp.where(rid == tm - 1, nxt, pltpu.roll(hy, tm - 1, 0))
    u = up * cw_ref[0:1, :] + hy * cw_ref[1:2, :] + dn * cw_ref[2:3, :] + cb_ref[...]
    hv_out[0] = u[:, :HY_WIDTH]
    hx1_out[0] = u[:, HY_WIDTH:2 * HY_WIDTH]
    hx2_out[0] = u[:, 2 * HY_WIDTH:]


def _inproj_call(x, mod, norm_mix, wa, ba, why, bhy, wg, bg, q_norm, wuq, wuqs, kv_norm, wuk, wuvt, cos_t, sin_t, cw,
                 cb, k_buf, vt_buf, tm):
    bsz, s, d = x.shape
    nt = s // tm
    rb = tm // SUBLANES
    last_rb = s // SUBLANES - 1
    consts = [norm_mix, wa, ba, why, bhy, wg, bg, q_norm, wuq, wuqs, kv_norm, wuk, wuvt]
    in_specs = [
        pl.BlockSpec((1, tm, d), lambda i, b: (b, i, 0)),
        pl.BlockSpec((1, SUBLANES, d), lambda i, b: (b, jnp.maximum(i * rb - 1, 0), 0)),
        pl.BlockSpec((1, SUBLANES, d), lambda i, b: (b, jnp.minimum((i + 1) * rb, last_rb), 0)),
        pl.BlockSpec((1, SUBLANES, d), lambda i, b: (b, 0, 0)),
    ] + [_full(c.shape) for c in consts] + [
        pl.BlockSpec((tm, HEAD_PAD), lambda i, b: (i, 0)),
        pl.BlockSpec((tm, HEAD_PAD), lambda i, b: (i, 0)),
        _full(cw.shape), _full(cb.shape),
        pl.BlockSpec(memory_space=pl.ANY), pl.BlockSpec(memory_space=pl.ANY),
    ]
    hw = HY_WIDTH
    out_specs = [
        pl.BlockSpec((1, N_HEADS, tm, HEAD_PAD), lambda i, b: (b, 0, i, 0)),
        pl.BlockSpec((1, N_HEADS, tm, HEAD_PAD), lambda i, b: (b, 0, i, 0)),
        pl.BlockSpec((1, N_HEADS, HEAD_PAD, tm), lambda i, b: (b, 0, 0, i)),
        pl.BlockSpec((1, tm, hw), lambda i, b: (b, i, 0)),
        pl.BlockSpec((1, tm, hw), lambda i, b: (b, i, 0)),
        pl.BlockSpec((1, tm, hw), lambda i, b: (b, i, 0)),
        pl.BlockSpec((1, tm, 2 * d), lambda i, b: (b, i, 0)),
    ]
    out_shape = [
        jax.ShapeDtypeStruct((bsz, N_HEADS, s, HEAD_PAD), BF16),
        jax.ShapeDtypeStruct(k_buf.shape, k_buf.dtype),
        jax.ShapeDtypeStruct(vt_buf.shape, vt_buf.dtype),
        jax.ShapeDtypeStruct((bsz, s, hw), F32),
        jax.ShapeDtypeStruct((bsz, s, hw), F32),
        jax.ShapeDtypeStruct((bsz, s, hw), F32),
        jax.ShapeDtypeStruct((bsz, s, 2 * d), BF16),
    ]
    return pl.pallas_call(
        _inproj_kernel,
        grid=(nt, bsz),
        in_specs=in_specs,
        out_specs=out_specs,
        out_shape=out_shape,
        input_output_aliases={len(in_specs) - 2: 1, len(in_specs) - 1: 2},
        compiler_params=_params("arbitrary", "arbitrary"),
        name="inproj",
    )(x, x, x, mod, *consts, cos_t, sin_t, cw, cb, k_buf, vt_buf)


def _attn_kernel(q_ref, k_ref, vt_ref, o_ref, m_sc, acc_sc):
    j = pl.program_id(2)

    @pl.when(j == 0)
    def _():
        m_sc[...] = jnp.full(m_sc.shape, -jnp.inf, F32)
        acc_sc[...] = jnp.zeros(acc_sc.shape, F32)

    tq = q_ref.shape[2]
    qw = min(tq, Q_CHUNK)
    units = [(h, c) for h in range(N_HEADS) for c in range(0, tq, qw)]

    def scores(u):
        h, c = units[u]
        return lax.dot_general(k_ref[0, h], q_ref[0, h, c:c + qw, :], NT_DIMS,
                               preferred_element_type=F32)

    pending = [scores(u) for u in range(AHEAD)]
    for u, (h, c) in enumerate(units):
        if u + AHEAD < len(units):
            pending.append(scores(u + AHEAD))
        st = pending.pop(0)
        m_prev = m_sc[h, :, c:c + qw]
        m_new = jnp.maximum(m_prev, jnp.max(st, axis=0, keepdims=True))
        pt = jnp.exp2(st - m_new).astype(BF16)
        acc_sc[h, :, c:c + qw] = (jnp.exp2(m_prev - m_new) * acc_sc[h, :, c:c + qw]
                                  + jnp.dot(vt_ref[0, h], pt, preferred_element_type=F32))
        m_sc[h, :, c:c + qw] = m_new

    @pl.when(j == pl.num_programs(2) - 1)
    def _():
        ot = jnp.concatenate([acc_sc[h, :V_HEAD] / acc_sc[h, V_HEAD:V_HEAD + 1] for h in range(N_HEADS)], axis=0)
        o_ref[0] = ot.T.astype(o_ref.dtype)


def _attn_call(q, k, vt, tq, tk):
    bsz, nh, s, dh = q.shape
    nk = k.shape[2]
    dv = nh * V_HEAD
    return pl.pallas_call(
        _attn_kernel,
        grid=(bsz, s // tq, nk // tk),
        in_specs=[
            pl.BlockSpec((1, nh, tq, dh), lambda b, i, j: (b, 0, i, 0)),
            pl.BlockSpec((1, nh, tk, dh), lambda b, i, j: (b, 0, j, 0)),
            pl.BlockSpec((1, nh, dh, tk), lambda b, i, j: (b, 0, 0, j)),
        ],
        out_specs=pl.BlockSpec((1, tq, dv), lambda b, i, j: (b, i, 0)),
        out_shape=jax.ShapeDtypeStruct((bsz, s, dv), BF16),
        scratch_shapes=[pltpu.VMEM((nh, 1, tq), F32), pltpu.VMEM((nh, dh, tq), F32)],
        compiler_params=_params("arbitrary", "arbitrary", "arbitrary"),
        name="attn",
    )(q, k, vt)


def _filter_kernel(emb_ref, w1_ref, b1_ref, w2_ref, b2_ref, w3_ref, fr_ref, dl_ref, full_out, asum_out, *, seq):
    r = pl.program_id(0)
    rb = emb_ref.shape[0]
    emb = emb_ref[...]
    fr = fr_ref[...]
    h = jnp.sin(fr * (_dot3(emb, w1_ref[...]) + b1_ref[...]))
    h = jnp.sin(fr * (_dot3(h, w2_ref[...]) + b2_ref[...]))
    k = _dot3(h, w3_ref[0]) * jnp.exp(-emb[:, 0:1] * dl_ref[...])
    row = r * rb + lax.broadcasted_iota(jnp.int32, (rb, 1), 0)
    k = jnp.where(row == seq, 0.0, k)
    full_out[...] = k

    @pl.when(r == 0)
    def _():
        asum_out[...] = jnp.zeros(asum_out.shape, F32)

    asum_out[...] += jnp.sum(jnp.abs(k), axis=0, keepdims=True)


def _filter_call(emb, w1, b1, w2, b2, w3sel, freq, deltas2, seq, rb):
    n2 = emb.shape[0]
    half_blocks = seq // rb
    width = w3sel.shape[2]
    return pl.pallas_call(
        functools.partial(_filter_kernel, seq=seq),
        grid=(n2 // rb,),
        in_specs=[pl.BlockSpec((rb, HY_EMB_PAD), lambda r: (r, 0)), _full(w1.shape), _full(b1.shape),
                  _full(w2.shape), _full(b2.shape),
                  pl.BlockSpec((1,) + w3sel.shape[1:], lambda r: (r // half_blocks, 0, 0)),
                  _full(freq.shape), _full(deltas2.shape)],
        out_specs=[pl.BlockSpec((rb, width), lambda r: (r, 0)), pl.BlockSpec((1, width), lambda r: (0, 0))],
        out_shape=[jax.ShapeDtypeStruct((n2, width), F32), jax.ShapeDtypeStruct((1, width), F32)],
        compiler_params=_params("arbitrary"),
        name="filt",
    )(emb, w1, b1, w2, b2, w3sel, freq, deltas2)


def _fa_kernel(u_ref, f_ref, a_out):
    two, _, hn, g, c = u_ref.shape
    a = _dot(f_ref[...], u_ref[...].reshape(two * hn * g, c))
    a_out[...] = (_pack(a) if a_out.dtype == jnp.uint32 else a).reshape(a_out.shape)


def _fa_call(u5, fmat, packed):
    _, p, hn, n, c = u5.shape
    g = SUBLANES
    co, dt = (c // 2, jnp.uint32) if packed else (c, F32)
    return pl.pallas_call(
        _fa_kernel,
        grid=(p, n // g),
        in_specs=[pl.BlockSpec((2, 1, hn, g, c), lambda q, j: (0, q, 0, j, 0)), _full(fmat.shape)],
        out_specs=pl.BlockSpec((1, 2, n, g, co), lambda q, j: (q, 0, 0, j, 0)),
        out_shape=jax.ShapeDtypeStruct((p, 2, n, n, co), dt),
        compiler_params=_params("arbitrary", "arbitrary"),
        name="fa",
    )(u5, fmat)


def _dot_packed(w, u):
    lo, hi = _unpack(u)
    return jnp.concatenate([jnp.dot(w, lo, preferred_element_type=F32), jnp.dot(w, hi, preferred_element_type=F32)],
                           axis=1)


def _fb_kernel(a_ref, g_ref, asum_ref, kf_out):
    _, two, kb, n, c = a_ref.shape
    scale = 1.0 / (asum_ref[...] + 1e-6)
    for kk in range(kb):
        x = _dot(g_ref[kk], a_ref[0, :, kk].reshape(two * n, c)) * scale
        kf_out[kk] = x.reshape(two, n, c)


def _fb_call(a5, gmat, asum, kb):
    _, _, n, _, c = a5.shape
    return pl.pallas_call(
        _fb_kernel,
        grid=(n // kb,),
        in_specs=[pl.BlockSpec((1, 2, kb, n, c), lambda k: (0, 0, k, 0, 0)),
                  pl.BlockSpec((kb, 2 * n, 2 * n), lambda k: (k, 0, 0)), _full(asum.shape)],
        out_specs=pl.BlockSpec((kb, 2, n, c), lambda k: (k, 0, 0, 0)),
        out_shape=jax.ShapeDtypeStruct((n, 2, n, c), F32),
        compiler_params=_params("arbitrary"),
        name="fb",
    )(a5, gmat, asum)


def _mid_kernel(a_ref, g_ref, h_ref, kf_ref, b_out):
    _, two, kb, n, c = a_ref.shape
    for kk in range(kb):
        x = _dot_packed(g_ref[kk], a_ref[0, :, kk].reshape(two * n, c))
        xr, xi = x[:n], x[n:]
        kr, ki = kf_ref[kk, 0], kf_ref[kk, 1]
        y = jnp.concatenate([xr * kr - xi * ki, xr * ki + xi * kr], axis=0)
        b_out[0, :, kk] = _pack(_dot(h_ref[kk], y)).reshape(two, n, c)


def _mid_call(a5, gmat, hmat, kf, order, kb):
    p, _, n, _, c = a5.shape
    return pl.pallas_call(
        _mid_kernel,
        grid=(n // kb, p),
        in_specs=[pl.BlockSpec((1, 2, kb, n, c), lambda k, q: (q, 0, k, 0, 0)),
                  pl.BlockSpec((kb, 2 * n, 2 * n), lambda k, q: (k, 0, 0)),
                  pl.BlockSpec((kb, 2 * n, 2 * n), lambda k, q: (k, 0, 0)),
                  pl.BlockSpec((kb, 2, n, 2 * c), lambda k, q: (k, 0, 0, order))],
        out_specs=pl.BlockSpec((1, 2, kb, n, c), lambda k, q: (q, 0, k, 0, 0)),
        out_shape=jax.ShapeDtypeStruct(a5.shape, jnp.uint32),
        compiler_params=_params("arbitrary", "arbitrary"),
        name="mid",
    )(a5, gmat, hmat, kf)


def _fc_kernel(b_ref, f_ref, u_ref, m_ref, skip_ref, o_out):
    _, two, n, g, c = b_ref.shape
    y = _dot_packed(f_ref[...], b_ref[...].reshape(two * n * g, c)).reshape(u_ref.shape)
    o_out[...] = m_ref[...] * (y + u_ref[...] * skip_ref[...])


def _fc_call(b5, finv, u5, m5, skip_row):
    _, p, hn, n, c = u5.shape
    g = SUBLANES
    blk = pl.BlockSpec((2, 1, hn, g, c), lambda q, j: (0, q, 0, j, 0))
    return pl.pallas_call(
        _fc_kernel,
        grid=(p, n // g),
        in_specs=[pl.BlockSpec((1, 2, n, g, c // 2), lambda q, j: (q, 0, 0, j, 0)), _full(finv.shape), blk, blk,
                  _full(skip_row.shape)],
        out_specs=blk,
        out_shape=jax.ShapeDtypeStruct(u5.shape, F32),
        compiler_params=_params("arbitrary", "arbitrary"),
        name="fc",
    )(b5, finv, u5, m5, skip_row)


def _dft_tables(n):
    hn = n // 2
    k = np.arange(n)[:, None]
    ang = -2.0 * np.pi * (k * np.arange(n)[None, :] % n) / n
    fr, fi = np.cos(ang), np.sin(ang)
    f_data = np.block([[fr[:, :hn], -fi[:, :hn]], [fi[:, :hn], fr[:, :hn]]])
    f_filt = np.concatenate([fr, fi], axis=0)
    er, ei = fr[:hn], -fi[:hn]
    f_inv = np.block([[er, -ei], [ei, er]]) / float(n * n)
    k1 = np.arange(n)[:, None, None]
    k2 = np.arange(n)[None, :, None]
    m2 = np.arange(n)[None, None, :]
    ang2 = -2.0 * np.pi * ((m2 * (k1 + n * k2)) % (n * n)) / (n * n)
    gr, gi = np.cos(ang2).astype(np.float32), np.sin(ang2).astype(np.float32)
    g = np.concatenate([np.concatenate([gr, -gi], axis=2), np.concatenate([gi, gr], axis=2)], axis=1)
    h = np.swapaxes(g, 1, 2)

    def widen(f):
        return np.kron(f, np.eye(SUBLANES))

    return tuple(jnp.asarray(a, BF16) for a in (widen(f_data), widen(f_filt), widen(f_inv), g, h))


def _hyena_filter_tables(seq):
    f32 = np.float32
    t = np.linspace(0.0, 1.0, seq, dtype=f32)[:, None]
    w = (f32(2.0 * math.pi) * np.arange(seq, dtype=f32)[:, None] / f32(seq)).astype(f32)
    f = np.linspace(1e-4, HY_BANDS - 1, HY_BANDS, dtype=f32)[None, :]
    emb = np.concatenate([t, np.cos(f * w), -np.sin(f * w)], axis=-1).astype(f32)
    emb = np.concatenate([emb, emb[:1], emb[:0:-1]], axis=0)
    emb = np.pad(emb, ((0, 0), (0, HY_EMB_PAD - HY_EMB)))
    deltas = np.abs(np.linspace(math.log(HY_DECAY_TARGET) / HY_SLOW_DECAY,
                                math.log(HY_DECAY_TARGET) / HY_FAST_DECAY, HY_WIDTH, dtype=f32))
    return jnp.asarray(emb), jnp.asarray(np.tile(deltas, HY_ORDER)[None, :])


def _hyena(hv, hx1, hx2, w1, b1, w2, b2, w3, freq, skip, kb):
    bsz, seq, c = hv.shape
    n = int(round(math.sqrt(2 * seq)))
    assert n * n == 2 * seq and bsz % 2 == 0
    hn, p = n // 2, bsz // 2
    f_data, f_filt, f_inv, gmat, hmat = _dft_tables(n)

    emb, deltas2 = _hyena_filter_tables(seq)
    w1p = jnp.pad(w1, ((0, HY_EMB_PAD - HY_EMB), (0, 0)))
    w3r = w3.reshape(w3.shape[0], HY_ORDER, 2, c)
    w3sel = jnp.stack([w3r[:, :, 0, :].reshape(-1, HY_ORDER * c), w3r[:, :, 1, :].reshape(-1, HY_ORDER * c)])
    full, asum = _filter_call(emb, w1p, b1[None], w2, b2[None], w3sel, freq[None], deltas2, seq, min(512, seq))
    c2 = HY_ORDER * c
    kf = _fb_call(_fa_call(full.reshape(2, 1, hn, n, c2), f_filt, False), gmat, asum, kb // HY_ORDER)

    def view(t):
        return t.reshape(2, p, hn, n, c)

    def long_conv(u5, m5, order):
        bm = _mid_call(_fa_call(u5, f_data, True), gmat, hmat, kf, order, kb)
        return _fc_call(bm, f_inv, u5, m5, skip[order][None, :])

    z = long_conv(view(hv), view(hx1), 0)
    return long_conv(z, view(hx2), 1).reshape(bsz, seq, c)


def _merge_kernel(x_ref, at_ref, hy_ref, g_ref, mod_ref, wba_ref, wbh_ref, wo_ref, o_ref):
    d = x_ref.shape[2]
    g = g_ref[0].astype(F32)
    y = (jax.nn.sigmoid(g[:, :d]) * _dot(at_ref[0], wba_ref[...])
         + jax.nn.sigmoid(g[:, d:]) * _dot(hy_ref[0], wbh_ref[...]))
    o_ref[0] = x_ref[0] + mod_ref[0, 2:3, :] * _dot(y, wo_ref[...])


def _merge_call(x, attn, hy, gate, mod, wba, wbh, wo, tm):
    bsz, s, d = x.shape

    def tok(w):
        return pl.BlockSpec((1, tm, w), lambda b, i: (b, i, 0))

    return pl.pallas_call(
        _merge_kernel,
        grid=(bsz, s // tm),
        in_specs=[tok(d), tok(attn.shape[2]), tok(hy.shape[2]), tok(2 * d),
                  pl.BlockSpec((1, SUBLANES, d), lambda b, i: (b, 0, 0)),
                  _full(wba.shape), _full(wbh.shape), _full(wo.shape)],
        out_specs=tok(d),
        out_shape=jax.ShapeDtypeStruct((bsz, s, d), F32),
        compiler_params=_params("arbitrary", "arbitrary"),
        name="merge",
    )(x, attn, hy, gate, mod, wba, wbh, wo)


def _route_kernel(xm_ref, mod_ref, nf_ref, wrt_ref, rb_ref, tri_ref, lt_ref, h2_out, w_out, p_out, col_out, row_out):
    tm = xm_ref.shape[1]
    ng, gs = N_GROUPS, GROUP_SIZE

    h2 = _prenorm(xm_ref[0], mod_ref, 3, nf_ref[...])
    h2_out[0] = h2.astype(h2_out.dtype)
    scores = jax.nn.sigmoid(_dot3(wrt_ref[...], h2, NT_DIMS))
    sel = scores + rb_ref[...]
    slabs = [sel[ng * j:ng * (j + 1)] for j in range(gs)]

    top1 = jnp.full((ng, tm), -jnp.inf, F32)
    top2 = top1
    for x in slabs:
        top2 = jnp.maximum(top2, jnp.minimum(top1, x))
        top1 = jnp.maximum(top1, x)
    gscore = top1 + top2
    gid = lax.broadcasted_iota(jnp.int32, (ng, 1), 0)
    rank = jnp.zeros((ng, tm), jnp.int32)
    for g2 in range(ng):
        row = gscore[g2:g2 + 1]
        beats = (row > gscore) | ((row == gscore) & (g2 < gid))
        rank = rank + beats.astype(jnp.int32)
    gmask = rank < TOPK_GROUPS

    cand = [jnp.where(gmask, x, -jnp.inf) for x in slabs]
    eid = [gid * gs + j for j in range(gs)]
    chosen = []
    for _ in range(TOP_K):
        best = functools.reduce(jnp.maximum, cand)
        best = jnp.max(best, axis=0, keepdims=True)
        idx = functools.reduce(jnp.minimum, [jnp.where(cand[j] == best, eid[j], N_EXPERTS) for j in range(gs)])
        idx = jnp.min(idx, axis=0, keepdims=True)
        chosen.append(idx)
        cand = [jnp.where(eid[j] == idx, -jnp.inf, cand[j]) for j in range(gs)]

    mask = [functools.reduce(jnp.logical_or, [eid[j] == idx for idx in chosen]) for j in range(gs)]
    maskb = jnp.concatenate(mask, axis=0)
    wsel = jnp.where(maskb, scores, 0.0)
    w_out[...] = wsel / jnp.sum(wsel, axis=0, keepdims=True) * ROUTE_SCALE

    def extents(cnt, lower_sum):
        units = jnp.floor((cnt + (RUN_ALIGN - 1)) * (1.0 / RUN_ALIGN))
        start = RUN_ALIGN * lower_sum(units.astype(BF16))
        return start, start + RUN_ALIGN * units

    lane = lax.broadcasted_iota(jnp.int32, (N_EXPERTS, LANES), 1)
    sub = lax.broadcasted_iota(jnp.int32, (SUBLANES, N_EXPERTS), 0)
    ts = tri_ref.shape[0]
    for c in range(tm // ts):
        mb = maskb[:, c * ts:(c + 1) * ts]
        maskf = jnp.where(mb, 1.0, 0.0)
        mask16 = maskf.astype(BF16)
        before = jnp.dot(mask16, tri_ref[...], preferred_element_type=F32)
        p_out[:, c * ts:(c + 1) * ts] = jnp.where(mb, before, -1.0).astype(p_out.dtype)
        cnt_c = jnp.sum(maskf, axis=1, keepdims=True)
        start_c, end_c = extents(jnp.broadcast_to(cnt_c, (N_EXPERTS, LANES)),
                                 lambda u: jnp.dot(lt_ref[...], u, preferred_element_type=F32))
        col_out[c] = jnp.where(lane == 0, cnt_c, jnp.where(lane == 1, start_c, end_c))
        cnt_r = lax.dot_general(jnp.ones((SUBLANES, ts), BF16), mask16, NT_DIMS, preferred_element_type=F32)
        start_r, end_r = extents(cnt_r,
                                 lambda u: lax.dot_general(u, lt_ref[...], NT_DIMS, preferred_element_type=F32))
        row_out[c] = jnp.where(sub == 0, start_r, end_r)


def _route_call(xm, mod, norm_ffn, wrt, rbias, lower, ts, tiles_per_step):
    bsz, s, d = xm.shape
    t = bsz * s
    tm = ts * tiles_per_step
    nt = s // tm
    tri = (jnp.arange(ts)[:, None] < jnp.arange(ts)[None, :]).astype(BF16)
    tok = pl.BlockSpec((N_EXPERTS, tm), lambda i: (0, i))
    return pl.pallas_call(
        _route_kernel,
        grid=(t // tm,),
        in_specs=[pl.BlockSpec((1, tm, d), lambda i: (i // nt, i % nt, 0)),
                  pl.BlockSpec((1, SUBLANES, d), lambda i: (i // nt, 0, 0)),
                  _full(norm_ffn.shape), _full(wrt.shape), _full(rbias.shape), _full(tri.shape),
                  _full(lower.shape)],
        out_specs=[pl.BlockSpec((1, tm, d), lambda i: (i // nt, i % nt, 0)), tok, tok,
                   pl.BlockSpec((tiles_per_step, N_EXPERTS, LANES), lambda i: (i, 0, 0)),
                   pl.BlockSpec((tiles_per_step, SUBLANES, N_EXPERTS), lambda i: (i, 0, 0))],
        out_shape=[jax.ShapeDtypeStruct((bsz, s, d), BF16), jax.ShapeDtypeStruct((N_EXPERTS, t), F32),
                   jax.ShapeDtypeStruct((N_EXPERTS, t), BF16),
                   jax.ShapeDtypeStruct((t // ts, N_EXPERTS, LANES), F32),
                   jax.ShapeDtypeStruct((t // ts, SUBLANES, N_EXPERTS), F32)],
        compiler_params=_params("arbitrary"),
        name="route",
    )(xm, mod, norm_ffn, wrt, rbias, tri, lower)


def _pack(x):
    w = x.shape[1] // 2
    lo = lax.bitcast_convert_type(x[:, :w].astype(BF16).astype(F32), jnp.uint32)
    hi = lax.bitcast_convert_type(x[:, w:].astype(BF16).astype(F32), jnp.uint32)
    return hi | (lo >> 16)


def _unpack(u):
    lo = lax.bitcast_convert_type(u << 16, F32).astype(BF16)
    hi = lax.bitcast_convert_type(u & jnp.uint32(0xFFFF0000), F32).astype(BF16)
    return lo, hi


def _pow2_pieces(units, limit):
    bit = 1
    while bit * 2 <= limit:
        bit *= 2
    while bit:
        yield (units & bit) != 0, units & ~(2 * bit - 1), bit
        bit //= 2


def _rows_copy(vm_ref, hbm_ref, sem, vm_row, hbm_row, rows, to_hbm):
    v = vm_ref.at[pl.ds(pl.multiple_of(vm_row, RUN_ALIGN), rows), :]
    h = hbm_ref.at[pl.ds(pl.multiple_of(hbm_row, RUN_ALIGN), rows), :]
    return pltpu.make_async_copy(v, h, sem) if to_hbm else pltpu.make_async_copy(h, v, sem)


def _run_copies(vm_ref, hbm_ref, sem, n8, vm_row, hbm_row, limit, to_hbm, act):
    def emit(pieces):
        for on, off, size in pieces:
            @pl.when(on)
            def _():
                act(_rows_copy(vm_ref, hbm_ref, sem, vm_row + RUN_ALIGN * off, hbm_row + RUN_ALIGN * off,
                               RUN_ALIGN * size, to_hbm))

    pieces = list(_pow2_pieces(n8, limit))
    long_pieces = [p for p in pieces if p[2] >= LONG_RUN]
    if long_pieces:
        pl.when(n8 >= LONG_RUN)(lambda: emit(long_pieces))
    emit([p for p in pieces if p[2] < LONG_RUN])


def _wait_rows(vm_ref, hbm_ref, sem, units, limit, to_hbm):
    for on, _, size in _pow2_pieces(units, limit):
        @pl.when(on)
        def _():
            _rows_copy(vm_ref, hbm_ref, sem, 0, 0, RUN_ALIGN * size, to_hbm).wait()


def _dispatch_kernel(n8_ref, ls_ref, gs_ref, ts_ref, t8_ref, nu_ref, pos_ref, ext_ref, h_ref, xs_out, srt2, zbuf,
                     sems):
    step = pl.program_id(0)
    tm = h_ref.shape[0]
    rows = srt2.shape[1]
    slot = step % 2
    srt, sem = srt2.at[slot], sems.at[slot]
    rid = lax.broadcasted_iota(jnp.int32, (rows, 1), 0).astype(F32)
    start = ext_ref[0, 0:1, :]
    member = jnp.where((rid >= start) & (rid < ext_ref[0, 1:2, :]), 1.0, 0.0)
    offset = rid - jnp.sum(member * start, axis=1, keepdims=True)
    pos = jnp.dot(member.astype(BF16), pos_ref[...], preferred_element_type=F32)
    sel = jnp.where(pos == offset, 1.0, 0.0).astype(BF16)
    srt[...] = _pack(jnp.dot(sel, h_ref[...], preferred_element_type=F32))

    def send(e, c):
        i = step * N_EXPERTS + e
        _run_copies(srt, xs_out, sem, n8_ref[i], ls_ref[i], gs_ref[i], tm // RUN_ALIGN, True, lambda cp: cp.start())
        return c

    lax.fori_loop(0, N_EXPERTS, send, 0)

    def wait_tile(tile, s):
        last = tile * N_EXPERTS + N_EXPERTS - 1
        _wait_rows(srt2.at[s], xs_out, sems.at[s], ls_ref[last] // RUN_ALIGN + n8_ref[last], rows // RUN_ALIGN, True)

    pl.when(step > 0)(lambda: wait_tile(step - 1, 1 - slot))

    @pl.when(step == pl.num_programs(0) - 1)
    def _():
        wait_tile(step, slot)
        zbuf[...] = jnp.zeros(zbuf.shape, zbuf.dtype)
        nblk = xs_out.shape[0] // EXPERT_BLOCK

        def fill(act):
            def tails(e, c):
                _run_copies(zbuf, xs_out, sem, t8_ref[e], 0, ts_ref[e], EXPERT_BLOCK // RUN_ALIGN - 1, True, act)
                return c

            def blocks(b, c):
                act(pltpu.make_async_copy(
                    zbuf, xs_out.at[pl.ds(pl.multiple_of(b * EXPERT_BLOCK, EXPERT_BLOCK), EXPERT_BLOCK), :], sem))
                return c

            lax.fori_loop(0, N_EXPERTS, tails, 0)
            lax.fori_loop(nu_ref[0], nblk, blocks, 0)

        fill(lambda cp: cp.start())
        fill(lambda cp: cp.wait())


def _dispatch_call(tables, pos_et, ext_rows, h2, nblk, tm):
    t, d = h2.shape
    lrows = TOP_K * tm + N_EXPERTS * RUN_ALIGN
    return pl.pallas_call(
        _dispatch_kernel,
        grid_spec=pltpu.PrefetchScalarGridSpec(
            num_scalar_prefetch=len(tables), grid=(t // tm,),
            in_specs=[pl.BlockSpec((N_EXPERTS, tm), lambda i, *_: (0, i)),
                      pl.BlockSpec((1,) + ext_rows.shape[1:], lambda i, *_: (i, 0, 0)),
                      pl.BlockSpec((tm, d), lambda i, *_: (i, 0))],
            out_specs=pl.BlockSpec(memory_space=pl.ANY),
            scratch_shapes=[pltpu.VMEM((2, lrows, d // 2), jnp.uint32),
                            pltpu.VMEM((EXPERT_BLOCK, d // 2), jnp.uint32), pltpu.SemaphoreType.DMA((2,))]),
        out_shape=jax.ShapeDtypeStruct((nblk * EXPERT_BLOCK, d // 2), jnp.uint32),
        compiler_params=_params("arbitrary"),
        name="dispatch",
    )(*tables, pos_et, ext_rows, h2)


def _expert_kernel(blk_ref, nused_ref, x_ref, wg_ref, wu_ref, wd_ref, y_ref, wgu_sc, wd_sc):
    i = pl.program_id(0)
    used = i < nused_ref[0]

    @pl.when(used & ((i == 0) | (blk_ref[i] != blk_ref[jnp.maximum(i - 1, 0)])))
    def _():
        wgu_sc[:, :EXPERT_FF] = wg_ref[0].astype(BF16)
        wgu_sc[:, EXPERT_FF:] = wu_ref[0].astype(BF16)
        wd_sc[...] = wd_ref[0].astype(BF16)

    @pl.when(used)
    def _():
        lo, hi = _unpack(x_ref[...])
        half = lo.shape[1]
        gu = (jnp.dot(lo, wgu_sc[:half, :], preferred_element_type=F32)
              + jnp.dot(hi, wgu_sc[half:, :], preferred_element_type=F32))
        a = _silu(gu[:, :EXPERT_FF]) * gu[:, EXPERT_FF:]
        y_ref[...] = _pack(_dot(a, wd_sc[...]))

    @pl.when(jnp.logical_not(used))
    def _():
        y_ref[...] = jnp.zeros(y_ref.shape, y_ref.dtype)


def _expert_call(blk_e, nused, xs, wg, wu, wd):
    rows, d = xs.shape
    nblk = rows // EXPERT_BLOCK

    def row_map(i, blk, nu):
        return (jnp.minimum(i, nu[0] - 1), 0)

    def of_expert(w):
        return pl.BlockSpec((1,) + w.shape[1:], lambda i, blk, nu: (blk[i], 0, 0))

    return pl.pallas_call(
        _expert_kernel,
        grid_spec=pltpu.PrefetchScalarGridSpec(
            num_scalar_prefetch=2, grid=(nblk,),
            in_specs=[pl.BlockSpec((EXPERT_BLOCK, d), row_map), of_expert(wg), of_expert(wu), of_expert(wd)],
            out_specs=pl.BlockSpec((EXPERT_BLOCK, d), lambda i, blk, nu: (i, 0)),
            scratch_shapes=[pltpu.VMEM((wg.shape[1], 2 * EXPERT_FF), BF16), pltpu.VMEM(wd.shape[1:], BF16)]),
        out_shape=jax.ShapeDtypeStruct((rows, d), jnp.uint32),
        compiler_params=_params("arbitrary"),
        name="expert",
    )(blk_e, nused, xs, wg, wu, wd)


def _combine_kernel(n8_ref, ls_ref, gs_ref, ys_hbm, pos_ref, w_ref, ext_ref, xm_ref, h_ref, mod_ref, wsgu_ref,
                    wsd_ref, fn_ref, o_ref, ybuf2, sems):
    step = pl.program_id(0)
    tm = xm_ref.shape[0]
    rows = ybuf2.shape[1]
    slot = step % 2
    ybuf, sem = ybuf2.at[slot], sems.at[slot]

    def fetch(tile, s):
        def body(e, c):
            i = tile * N_EXPERTS + e
            _run_copies(ybuf2.at[s], ys_hbm, sems.at[s], n8_ref[i], ls_ref[i], gs_ref[i], tm // RUN_ALIGN, False,
                        lambda cp: cp.start())
            return c
        lax.fori_loop(0, N_EXPERTS, body, 0)

    pl.when(step == 0)(lambda: fetch(step, slot))
    pl.when(step + 1 < pl.num_programs(0))(lambda: fetch(step + 1, 1 - slot))
    gu = _dot(h_ref[...], wsgu_ref[...])
    ff = gu.shape[1] // 2
    shared = _dot(_silu(gu[:, :ff]) * gu[:, ff:], wsd_ref[...])
    cid = lax.broadcasted_iota(jnp.int32, (1, rows), 1).astype(F32)
    start = ext_ref[0, :, 1:2]
    member = jnp.where((cid >= start) & (cid < ext_ref[0, :, 2:3]), 1.0, 0.0)
    offset = cid - jnp.sum(member * start, axis=0, keepdims=True)
    member = member.astype(BF16)
    pos = jnp.dot(pos_ref[...], member, preferred_element_type=F32)
    mix = jnp.where(pos == offset, jnp.dot(w_ref[...].astype(BF16), member, preferred_element_type=F32), 0.0)
    mix = mix.astype(BF16)
    last = step * N_EXPERTS + N_EXPERTS - 1
    filled = ls_ref[last] + RUN_ALIGN * n8_ref[last]
    _wait_rows(ybuf, ys_hbm, sem, filled // RUN_ALIGN, rows // RUN_ALIGN, False)
    rid = lax.broadcasted_iota(jnp.int32, (rows, 1), 0)
    lo, hi = _unpack(jnp.where(rid < filled, ybuf[...], jnp.uint32(0)))
    routed = jnp.concatenate([jnp.dot(mix, lo, preferred_element_type=F32),
                              jnp.dot(mix, hi, preferred_element_type=F32)], axis=1)
    x = xm_ref[...] + mod_ref[0, 5:6, :] * (routed + shared)
    o_ref[...] = _rms(x, fn_ref[...])


def _combine_call(tables, ys, pos_te, w_te, ext_cols, xm, h2, mod, wsgu, wsd, final_norm, tm, tiles_per_batch):
    t, d = xm.shape
    lrows = TOP_K * tm + N_EXPERTS * RUN_ALIGN
    tok = pl.BlockSpec((tm, d), lambda i, *_: (i, 0))
    per_e = pl.BlockSpec((tm, N_EXPERTS), lambda i, *_: (i, 0))
    return pl.pallas_call(
        _combine_kernel,
        grid_spec=pltpu.PrefetchScalarGridSpec(
            num_scalar_prefetch=len(tables), grid=(t // tm,),
            in_specs=[pl.BlockSpec(memory_space=pl.ANY), per_e, per_e,
                      pl.BlockSpec((1,) + ext_cols.shape[1:], lambda i, *_: (i, 0, 0)), tok, tok,
                      pl.BlockSpec((1, SUBLANES, d), lambda i, *_: (i // tiles_per_batch, 0, 0)),
                      _full(wsgu.shape), _full(wsd.shape), _full(final_norm.shape)],
            out_specs=tok,
            scratch_shapes=[pltpu.VMEM((2, lrows, d // 2), jnp.uint32), pltpu.SemaphoreType.DMA((2,))]),
        out_shape=jax.ShapeDtypeStruct((t, d), F32),
        compiler_params=_params("arbitrary"),
        name="combine",
    )(*tables, ys, pos_te, w_te, ext_cols, xm, h2, mod, wsgu, wsd, final_norm)


def _moe(xm, mod, norm_ffn, w_router, router_bias, wg, wu, wd, wsg, wsu, wsd, final_norm, tm):
    bsz, s, d = xm.shape
    t = bsz * s
    nt = t // tm
    perm = (np.arange(N_EXPERTS) % N_GROUPS) * GROUP_SIZE + np.arange(N_EXPERTS) // N_GROUPS
    wrt = w_router.T[perm]
    rbias = router_bias[perm][:, None]
    lower = jnp.asarray(perm[None, :] < perm[:, None], BF16)
    h2, w_et, pos_et, ext_cols, ext_rows = _route_call(xm, mod, norm_ffn, wrt, rbias, lower, tm, ROUTE_TILES)

    inv = np.argsort(perm)
    n8 = (ext_cols[:, :, 0].astype(jnp.int32)[:, inv] + (RUN_ALIGN - 1)) // RUN_ALIGN
    run = RUN_ALIGN * n8
    ls = jnp.cumsum(run, axis=1) - run
    tot = jnp.sum(run, axis=0)
    padded = (tot + EXPERT_BLOCK - 1) // EXPERT_BLOCK * EXPERT_BLOCK
    pad_end = jnp.cumsum(padded)
    gs = (pad_end - padded)[None, :] + jnp.cumsum(run, axis=0) - run
    nblk = -(-(t * TOP_K + nt * N_EXPERTS * (RUN_ALIGN - 1)) // EXPERT_BLOCK) + N_EXPERTS
    blk_first = jnp.arange(nblk, dtype=jnp.int32)[:, None] * EXPERT_BLOCK
    blk_e = jnp.minimum(jnp.sum((pad_end[None, :] <= blk_first).astype(jnp.int32), axis=1), N_EXPERTS - 1)
    nused = (pad_end[-1:] // EXPERT_BLOCK).astype(jnp.int32)
    tables = [a.reshape(-1).astype(jnp.int32) for a in (n8, ls, gs)]
    tails = [(pad_end - padded + tot).astype(jnp.int32), ((padded - tot) // RUN_ALIGN).astype(jnp.int32), nused]

    h2f = h2.reshape(t, d)
    xs = _dispatch_call(tables + tails, pos_et, ext_rows, h2f, nblk, tm)
    ys = _expert_call(blk_e, nused, xs, wg, wu, wd)
    wsgu = jnp.concatenate([wsg, wsu], axis=1).astype(BF16)
    out = _combine_call(tables, ys, pos_et.T, w_et.T, ext_cols, xm.reshape(t, d), h2f, mod, wsgu, wsd.astype(BF16),
                        final_norm, tm, s // tm)
    return out.reshape(bsz, s, d)


def _rope_tables(s):
    f32 = np.float32
    rows = s // GRID_W
    row = np.broadcast_to(np.arange(rows, dtype=f32)[:, None], (rows, GRID_W)).reshape(-1)
    col = np.broadcast_to(np.arange(GRID_W, dtype=f32)[None, :], (rows, GRID_W)).reshape(-1)
    half = QK_ROPE // 2
    inv_freq = (f32(ROPE_THETA) ** (-np.arange(0, half, 2, dtype=f32) / f32(half))).astype(f32)
    ar, ac = row[:, None] * inv_freq, col[:, None] * inv_freq
    ones = np.ones((s, QK_NOPE), f32)
    tail = HEAD_PAD - QK_NOPE - QK_ROPE
    cos_t = np.concatenate([ones, np.cos(ar), np.cos(ar), np.cos(ac), np.cos(ac), np.ones((s, tail), f32)], 1)
    sin_t = np.concatenate([0 * ones, -np.sin(ar), np.sin(ar), -np.sin(ac), np.sin(ac), np.zeros((s, tail), f32)], 1)
    return jnp.asarray(cos_t, F32), jnp.asarray(sin_t, F32)


_Q4 = QK_ROPE // 4
ROPE_SWAP = np.concatenate([np.arange(_Q4, 2 * _Q4), np.arange(0, _Q4), np.arange(3 * _Q4, 4 * _Q4),
                            np.arange(2 * _Q4, 3 * _Q4)])


def _rope_slot(w, swap):
    if swap:
        w = w[..., ROPE_SWAP]
    pad = [(0, 0)] * (w.ndim - 1) + [(QK_NOPE, HEAD_PAD - QK_NOPE - QK_ROPE)]
    return jnp.pad(w, pad)


TILES = dict(inproj=512, tq=2048, tk=1408, fft_kb=8, merge=512, moe=256)


def kernel(x, c, ctx, c_ctx, w_mod, b_mod, norm_mix, norm_ffn, w_in, b_in, q_norm, w_uq, kv_norm, w_ukv, w_branch_attn, hy_conv_w, hy_conv_b, hy_filt_w1, hy_filt_b1, hy_filt_w2, hy_filt_b2, hy_filt_w3, hy_filt_freq, hy_skip, w_branch_hyena, w_out, w_router, router_bias, w_exp_gate, w_exp_up, w_exp_down, w_sh_gate, w_sh_up, w_sh_down, final_norm):
    bsz, s, d = x.shape
    tl = TILES
    assert w_mod.shape[0] == 1, "single-layer trunk"
    i = 0

    rows = -(-(bsz + 1) // SUBLANES) * SUBLANES
    c_rows = jnp.pad(jnp.concatenate([c, c_ctx[None]], axis=0), ((0, rows - bsz - 1), (0, 0)))
    mod_all = _mod_call(c_rows, w_mod[i], b_mod[i])
    mod_all = jnp.pad(mod_all.reshape(rows, 6, d), ((0, 0), (0, SUBLANES - 6), (0, 0)))
    mod, modc = mod_all[:bsz], mod_all[bsz:bsz + 1]

    cuts = np.cumsum([Q_LORA, KV_LORA, QK_ROPE, 3 * HY_WIDTH])
    wi, bi = w_in[i], b_in[i][None]
    w_q, w_kv, w_pe, w_hy, w_g = jnp.split(wi, cuts, axis=1)
    b_q, b_kv, b_pe, b_hy, b_g = jnp.split(bi, cuts, axis=1)
    wa = jnp.concatenate([w_q, w_kv, _rope_slot(w_pe, False), _rope_slot(w_pe, True)], axis=1).astype(BF16)
    ba = jnp.concatenate([b_q, b_kv, _rope_slot(b_pe, False), _rope_slot(b_pe, True)], axis=1)
    wq3 = w_uq[i].reshape(Q_LORA, N_HEADS, QK_NOPE + QK_ROPE) * (ATTN_SCALE * math.log2(math.e))
    tail = ((0, 0), (0, 0), (0, HEAD_PAD - QK_NOPE))
    wuq = (jnp.pad(wq3[..., :QK_NOPE], tail) + _rope_slot(wq3[..., QK_NOPE:], False)).reshape(Q_LORA, -1).astype(BF16)
    wuqs = _rope_slot(wq3[..., QK_NOPE:], True).reshape(Q_LORA, -1).astype(BF16)
    wkv3 = w_ukv[i].reshape(KV_LORA, N_HEADS, QK_NOPE + V_HEAD)
    wuk = jnp.pad(wkv3[..., :QK_NOPE], tail).reshape(KV_LORA, -1).astype(BF16)
    wuvt = wkv3[..., QK_NOPE:].reshape(KV_LORA, -1).T.astype(BF16)
    nm, qn, kvn = norm_mix[i][None], q_norm[i][None], kv_norm[i][None]

    w_c = jnp.concatenate([w_kv, _rope_slot(w_pe, False)], axis=1).astype(BF16)
    b_c = jnp.concatenate([b_kv, _rope_slot(b_pe, False)], axis=1)
    k_buf, vt_buf = _ctx_call(ctx, modc, nm, w_c, b_c, kvn, wuk, wuvt, s)

    cos_t, sin_t = _rope_tables(s)
    q, k, vt, hv, hx1, hx2, gate = _inproj_call(
        x, mod, nm, wa, ba, w_hy.astype(BF16), b_hy, w_g.astype(BF16), b_g, qn, wuq, wuqs, kvn, wuk, wuvt,
        cos_t, sin_t, hy_conv_w[i], hy_conv_b[i][None], k_buf, vt_buf, tl["inproj"])

    attn = _attn_call(q, k, vt, tl["tq"], tl["tk"])
    hy = _hyena(hv, hx1, hx2, hy_filt_w1[i], hy_filt_b1[i], hy_filt_w2[i], hy_filt_b2[i], hy_filt_w3[i],
                hy_filt_freq[i], hy_skip[i], tl["fft_kb"])
    xm = _merge_call(x, attn, hy, gate, mod, w_branch_attn[i].astype(BF16), w_branch_hyena[i].astype(BF16),
                     w_out[i].astype(BF16), tl["merge"])
    return _moe(xm, mod, norm_ffn[i][None], w_router[i], router_bias[i], w_exp_gate[i], w_exp_up[i], w_exp_down[i],
                w_sh_gate[i], w_sh_up[i], w_sh_down[i], final_norm[None], tl["moe"])
```

```python
import functools
import math

import numpy as np
import jax
import jax.numpy as jnp
from jax import lax
from jax.experimental import pallas as pl
from jax.experimental.pallas import tpu as pltpu

GRID_W = 64
N_HEADS = 8
QK_NOPE = 64
QK_ROPE = 32
V_HEAD = 64
Q_LORA = 256
KV_LORA = 128
ROPE_THETA = 10000.0
ATTN_SCALE = 1.0 / math.sqrt(QK_NOPE + QK_ROPE)
HY_WIDTH = 512
HY_ORDER = 2
HY_SHORT = 3
HY_BANDS = 8
HY_EMB = 1 + 2 * HY_BANDS
HY_EMB_PAD = 32
HY_FAST_DECAY = 0.3
HY_SLOW_DECAY = 1.5
HY_DECAY_TARGET = 1e-2
N_EXPERTS = 64
N_GROUPS = 8
GROUP_SIZE = N_EXPERTS // N_GROUPS
TOPK_GROUPS = 4
TOP_K = 8
EXPERT_FF = 256
ROUTE_SCALE = 2.5
EXPERT_BLOCK = 1024
RUN_ALIGN = 8
ROUTE_TILES = 8
LONG_RUN = 8
NORM_EPS = 1e-6

HEAD_PAD = 128
Q_CHUNK = 512
AHEAD = 2
LANES = 128
SUBLANES = 8
VMEM_LIMIT = 48 * 1024 * 1024

F32 = jnp.float32
BF16 = jnp.bfloat16
NT_DIMS = (((1,), (1,)), ((), ()))
NN_DIMS = (((1,), (0,)), ((), ()))


def _params(*sem):
    return pltpu.CompilerParams(dimension_semantics=sem, vmem_limit_bytes=VMEM_LIMIT)


def _dot(a, b):
    return jnp.dot(a.astype(BF16), b.astype(BF16), preferred_element_type=F32)


def _split(a):
    hi = a.astype(BF16)
    lo = (a - hi.astype(F32)).astype(BF16)
    return hi, lo


def _dot3(a, b, dims=NN_DIMS):
    ah, al = _split(a)
    bh, bl = _split(b)
    d = functools.partial(lax.dot_general, dimension_numbers=dims, preferred_element_type=F32)
    return d(ah, bh) + (d(ah, bl) + d(al, bh))


def _rms(x, g):
    return x * lax.rsqrt(jnp.mean(x * x, axis=-1, keepdims=True) + NORM_EPS) * g


def _silu(x):
    return x * jax.nn.sigmoid(x)


def _full(shape):
    nd = len(shape)
    return pl.BlockSpec(shape, lambda *_: (0,) * nd)


def _mod_kernel(c_ref, w_ref, b_ref, o_ref):
    o_ref[...] = _dot3(_silu(c_ref[...]), w_ref[...]) + b_ref[...]


def _mod_call(c_rows, w_mod, b_mod):
    r, d = c_rows.shape
    n = w_mod.shape[1]
    bn = 1024
    return pl.pallas_call(
        _mod_kernel,
        grid=(n // bn,),
        in_specs=[_full((r, d)), pl.BlockSpec((d, bn), lambda j: (0, j)), pl.BlockSpec((1, bn), lambda j: (0, j))],
        out_specs=pl.BlockSpec((r, bn), lambda j: (0, j)),
        out_shape=jax.ShapeDtypeStruct((r, n), F32),
        compiler_params=_params("arbitrary"),
        name="mod",
    )(c_rows, w_mod, b_mod.reshape(1, n))


def _prenorm(x, mod_ref, row, g):
    shift = mod_ref[0, row:row + 1, :]
    scale = mod_ref[0, row + 1:row + 2, :]
    return _rms(x, g) * (1.0 + scale) + shift


def _kv_heads(kv_lat, kpe, kvn_ref, wuk_ref, wuvt_ref, k_out, vt_out):
    kvn = _rms(kv_lat, kvn_ref[...]).astype(BF16)
    kk = _dot(kvn, wuk_ref[...])
    vt = lax.dot_general(wuvt_ref[...], kvn, NT_DIMS, preferred_element_type=F32)
    ones = jnp.ones((HEAD_PAD - V_HEAD, vt.shape[1]), F32)
    for h in range(N_HEADS):
        k_out[0, h] = (kk[:, HEAD_PAD * h:HEAD_PAD * (h + 1)] + kpe).astype(BF16)
        vt_out[0, h] = jnp.concatenate([vt[V_HEAD * h:V_HEAD * (h + 1)], ones], axis=0).astype(BF16)


def _ctx_kernel(c_ref, mod_ref, nm_ref, w_ref, b_ref, kvn_ref, wuk_ref, wuv_ref, k_out, v_out):
    j = pl.program_id(1)
    is_ctx = j == pl.num_programs(1) - 1

    @pl.when(is_ctx)
    def _():
        h = _prenorm(c_ref[0], mod_ref, 0, nm_ref[...]).astype(BF16)
        a = _dot(h, w_ref[...]) + b_ref[...]
        _kv_heads(a[:, :KV_LORA], a[:, KV_LORA:], kvn_ref, wuk_ref, wuv_ref, k_out, v_out)

    @pl.when(jnp.logical_not(is_ctx))
    def _():
        k_out[...] = jnp.zeros(k_out.shape, k_out.dtype)
        v_out[...] = jnp.zeros(v_out.shape, v_out.dtype)


def _ctx_call(ctx, modc, norm_mix, w_c, b_c, kv_norm, w_uk, w_uv, seq):
    bsz, n, d = ctx.shape
    nk = seq + n
    return pl.pallas_call(
        _ctx_kernel,
        grid=(bsz, nk // n),
        in_specs=[pl.BlockSpec((1, n, d), lambda b, j: (b, 0, 0)), _full(modc.shape), _full(norm_mix.shape),
                  _full(w_c.shape), _full(b_c.shape), _full(kv_norm.shape), _full(w_uk.shape), _full(w_uv.shape)],
        out_specs=[pl.BlockSpec((1, N_HEADS, n, HEAD_PAD), lambda b, j: (b, 0, j, 0)),
                   pl.BlockSpec((1, N_HEADS, HEAD_PAD, n), lambda b, j: (b, 0, 0, j))],
        out_shape=[jax.ShapeDtypeStruct((bsz, N_HEADS, nk, HEAD_PAD), BF16),
                   jax.ShapeDtypeStruct((bsz, N_HEADS, HEAD_PAD, nk), BF16)],
        compiler_params=_params("arbitrary", "arbitrary"),
        name="ctx",
    )(ctx, modc, norm_mix, w_c, b_c, kv_norm, w_uk, w_uv)


def _inproj_kernel(x_ref, xp_ref, xn_ref, mod_ref, nm_ref, wa_ref, ba_ref, why_ref, bhy_ref, wg_ref, bg_ref,
                   qn_ref, wuq_ref, wuqs_ref, kvn_ref, wuk_ref, wuv_ref, cos_ref, sin_ref, cw_ref, cb_ref,
                   k_buf, v_buf, q_out, k_out, v_out, hv_out, hx1_out, hx2_out, g_out):
    del k_buf, v_buf
    i = pl.program_id(0)
    tm = x_ref.shape[1]
    nm = nm_ref[...]
    h = _prenorm(x_ref[0], mod_ref, 0, nm).astype(BF16)
    a = _dot(h, wa_ref[...]) + ba_ref[...]
    q_lat = a[:, :Q_LORA]
    kv_lat = a[:, Q_LORA:Q_LORA + KV_LORA]
    kpe_m = a[:, Q_LORA + KV_LORA:Q_LORA + KV_LORA + HEAD_PAD]
    kpe_s = a[:, Q_LORA + KV_LORA + HEAD_PAD:]
    cos = cos_ref[...]
    sin = sin_ref[...]
    qn = _rms(q_lat, qn_ref[...]).astype(BF16)
    qa = _dot(qn, wuq_ref[...])
    qs = _dot(qn, wuqs_ref[...])
    for hh in range(N_HEADS):
        sl = slice(HEAD_PAD * hh, HEAD_PAD * (hh + 1))
        q_out[0, hh] = (qa[:, sl] * cos + qs[:, sl] * sin).astype(BF16)
    _kv_heads(kv_lat, kpe_m * cos + kpe_s * sin, kvn_ref, wuk_ref, wuv_ref, k_out, v_out)
    g_out[0] = (_dot(h, wg_ref[...]) + bg_ref[...]).astype(BF16)

    why = why_ref[...]
    bhy = bhy_ref[...]
    halo = jnp.concatenate([_prenorm(xp_ref[0], mod_ref, 0, nm), _prenorm(xn_ref[0], mod_ref, 0, nm)], axis=0)
    hy_all = _dot(jnp.concatenate([h, halo.astype(BF16)], axis=0), why) + bhy
    hy = hy_all[:tm]
    prev = jnp.where(i == 0, 0.0, hy_all[tm + SUBLANES - 1:tm + SUBLANES])
    nxt = jnp.where(i == pl.num_programs(0) - 1, 0.0, hy_all[tm + SUBLANES:tm + SUBLANES + 1])
    rid = lax.broadcasted_iota(jnp.int32, (tm, 1), 0)
    up = jnp.where(rid == 0, prev, pltpu.roll(hy, 1, 0))
    dn = jnp.where(rid == tm - 1, nxt, pltpu.roll(hy, tm - 1, 0))
    u = up * cw_ref[0:1, :] + hy * cw_ref[1:2, :] + dn * cw_ref[2:3, :] + cb_ref[...]
    hv_out[0] = u[:, :HY_WIDTH]
    hx1_out[0] = u[:, HY_WIDTH:2 * HY_WIDTH]
    hx2_out[0] = u[:, 2 * HY_WIDTH:]


def _inproj_call(x, mod, norm_mix, wa, ba, why, bhy, wg, bg, q_norm, wuq, wuqs, kv_norm, wuk, wuvt, cos_t, sin_t, cw,
                 cb, k_buf, vt_buf, tm):
    bsz, s, d = x.shape
    assert cw.shape[0] == HY_SHORT == 3, "the kernel applies a centred width-3 depthwise convolution"
    nt = s // tm
    rb = tm // SUBLANES
    last_rb = s // SUBLANES - 1
    consts = [norm_mix, wa, ba, why, bhy, wg, bg, q_norm, wuq, wuqs, kv_norm, wuk, wuvt]
    in_specs = [
        pl.BlockSpec((1, tm, d), lambda i, b: (b, i, 0)),
        pl.BlockSpec((1, SUBLANES, d), lambda i, b: (b, jnp.maximum(i * rb - 1, 0), 0)),
        pl.BlockSpec((1, SUBLANES, d), lambda i, b: (b, jnp.minimum((i + 1) * rb, last_rb), 0)),
        pl.BlockSpec((1, SUBLANES, d), lambda i, b: (b, 0, 0)),
    ] + [_full(c.shape) for c in consts] + [
        pl.BlockSpec((tm, HEAD_PAD), lambda i, b: (i, 0)),
        pl.BlockSpec((tm, HEAD_PAD), lambda i, b: (i, 0)),
        _full(cw.shape), _full(cb.shape),
        pl.BlockSpec(memory_space=pl.ANY), pl.BlockSpec(memory_space=pl.ANY),
    ]
    hw = HY_WIDTH
    out_specs = [
        pl.BlockSpec((1, N_HEADS, tm, HEAD_PAD), lambda i, b: (b, 0, i, 0)),
        pl.BlockSpec((1, N_HEADS, tm, HEAD_PAD), lambda i, b: (b, 0, i, 0)),
        pl.BlockSpec((1, N_HEADS, HEAD_PAD, tm), lambda i, b: (b, 0, 0, i)),
        pl.BlockSpec((1, tm, hw), lambda i, b: (b, i, 0)),
        pl.BlockSpec((1, tm, hw), lambda i, b: (b, i, 0)),
        pl.BlockSpec((1, tm, hw), lambda i, b: (b, i, 0)),
        pl.BlockSpec((1, tm, 2 * d), lambda i, b: (b, i, 0)),
    ]
    out_shape = [
        jax.ShapeDtypeStruct((bsz, N_HEADS, s, HEAD_PAD), BF16),
        jax.ShapeDtypeStruct(k_buf.shape, k_buf.dtype),
        jax.ShapeDtypeStruct(vt_buf.shape, vt_buf.dtype),
        jax.ShapeDtypeStruct((bsz, s, hw), F32),
        jax.ShapeDtypeStruct((bsz, s, hw), F32),
        jax.ShapeDtypeStruct((bsz, s, hw), F32),
        jax.ShapeDtypeStruct((bsz, s, 2 * d), BF16),
    ]
    return pl.pallas_call(
        _inproj_kernel,
        grid=(nt, bsz),
        in_specs=in_specs,
        out_specs=out_specs,
        out_shape=out_shape,
        input_output_aliases={len(in_specs) - 2: 1, len(in_specs) - 1: 2},
        compiler_params=_params("arbitrary", "arbitrary"),
        name="inproj",
    )(x, x, x, mod, *consts, cos_t, sin_t, cw, cb, k_buf, vt_buf)


def _attn_kernel(q_ref, k_ref, vt_ref, o_ref, m_sc, acc_sc):
    j = pl.program_id(2)

    @pl.when(j == 0)
    def _():
        m_sc[...] = jnp.full(m_sc.shape, -jnp.inf, F32)
        acc_sc[...] = jnp.zeros(acc_sc.shape, F32)

    tq = q_ref.shape[2]
    qw = min(tq, Q_CHUNK)
    units = [(h, c) for h in range(N_HEADS) for c in range(0, tq, qw)]

    def scores(u):
        h, c = units[u]
        return lax.dot_general(k_ref[0, h], q_ref[0, h, c:c + qw, :], NT_DIMS,
                               preferred_element_type=F32)

    pending = [scores(u) for u in range(AHEAD)]
    for u, (h, c) in enumerate(units):
        if u + AHEAD < len(units):
            pending.append(scores(u + AHEAD))
        st = pending.pop(0)
        m_prev = m_sc[h, :, c:c + qw]
        m_new = jnp.maximum(m_prev, jnp.max(st, axis=0, keepdims=True))
        pt = jnp.exp2(st - m_new).astype(BF16)
        acc_sc[h, :, c:c + qw] = (jnp.exp2(m_prev - m_new) * acc_sc[h, :, c:c + qw]
                                  + jnp.dot(vt_ref[0, h], pt, preferred_element_type=F32))
        m_sc[h, :, c:c + qw] = m_new

    @pl.when(j == pl.num_programs(2) - 1)
    def _():
        ot = jnp.concatenate([acc_sc[h, :V_HEAD] / acc_sc[h, V_HEAD:V_HEAD + 1] for h in range(N_HEADS)], axis=0)
        o_ref[0] = ot.T.astype(o_ref.dtype)


def _attn_call(q, k, vt, tq, tk):
    bsz, nh, s, dh = q.shape
    nk = k.shape[2]
    dv = nh * V_HEAD
    return pl.pallas_call(
        _attn_kernel,
        grid=(bsz, s // tq, nk // tk),
        in_specs=[
            pl.BlockSpec((1, nh, tq, dh), lambda b, i, j: (b, 0, i, 0)),
            pl.BlockSpec((1, nh, tk, dh), lambda b, i, j: (b, 0, j, 0)),
            pl.BlockSpec((1, nh, dh, tk), lambda b, i, j: (b, 0, 0, j)),
        ],
        out_specs=pl.BlockSpec((1, tq, dv), lambda b, i, j: (b, i, 0)),
        out_shape=jax.ShapeDtypeStruct((bsz, s, dv), BF16),
        scratch_shapes=[pltpu.VMEM((nh, 1, tq), F32), pltpu.VMEM((nh, dh, tq), F32)],
        compiler_params=_params("arbitrary", "arbitrary", "arbitrary"),
        name="attn",
    )(q, k, vt)


def _filter_kernel(emb_ref, w1_ref, b1_ref, w2_ref, b2_ref, w3_ref, fr_ref, dl_ref, full_out, asum_out, *, seq):
    r = pl.program_id(0)
    rb = emb_ref.shape[0]
    emb = emb_ref[...]
    fr = fr_ref[...]
    h = jnp.sin(fr * (_dot3(emb, w1_ref[...]) + b1_ref[...]))
    h = jnp.sin(fr * (_dot3(h, w2_ref[...]) + b2_ref[...]))
    k = _dot3(h, w3_ref[0]) * jnp.exp(-emb[:, 0:1] * dl_ref[...])
    row = r * rb + lax.broadcasted_iota(jnp.int32, (rb, 1), 0)
    k = jnp.where(row == seq, 0.0, k)
    full_out[...] = k

    @pl.when(r == 0)
    def _():
        asum_out[...] = jnp.zeros(asum_out.shape, F32)

    asum_out[...] += jnp.sum(jnp.abs(k), axis=0, keepdims=True)


def _filter_call(emb, w1, b1, w2, b2, w3sel, freq, deltas2, seq, rb):
    n2 = emb.shape[0]
    half_blocks = seq // rb
    width = w3sel.shape[2]
    return pl.pallas_call(
        functools.partial(_filter_kernel, seq=seq),
        grid=(n2 // rb,),
        in_specs=[pl.BlockSpec((rb, HY_EMB_PAD), lambda r: (r, 0)), _full(w1.shape), _full(b1.shape),
                  _full(w2.shape), _full(b2.shape),
                  pl.BlockSpec((1,) + w3sel.shape[1:], lambda r: (r // half_blocks, 0, 0)),
                  _full(freq.shape), _full(deltas2.shape)],
        out_specs=[pl.BlockSpec((rb, width), lambda r: (r, 0)), pl.BlockSpec((1, width), lambda r: (0, 0))],
        out_shape=[jax.ShapeDtypeStruct((n2, width), F32), jax.ShapeDtypeStruct((1, width), F32)],
        compiler_params=_params("arbitrary"),
        name="filt",
    )(emb, w1, b1, w2, b2, w3sel, freq, deltas2)


def _fa_kernel(u_ref, f_ref, a_out):
    two, _, hn, g, c = u_ref.shape
    a = _dot(f_ref[...], u_ref[...].reshape(two * hn * g, c))
    a_out[...] = (_pack(a) if a_out.dtype == jnp.uint32 else a).reshape(a_out.shape)


def _fa_call(u5, fmat, packed):
    _, p, hn, n, c = u5.shape
    g = SUBLANES
    co, dt = (c // 2, jnp.uint32) if packed else (c, F32)
    return pl.pallas_call(
        _fa_kernel,
        grid=(p, n // g),
        in_specs=[pl.BlockSpec((2, 1, hn, g, c), lambda q, j: (0, q, 0, j, 0)), _full(fmat.shape)],
        out_specs=pl.BlockSpec((1, 2, n, g, co), lambda q, j: (q, 0, 0, j, 0)),
        out_shape=jax.ShapeDtypeStruct((p, 2, n, n, co), dt),
        compiler_params=_params("arbitrary", "arbitrary"),
        name="fa",
    )(u5, fmat)


def _dot_packed(w, u):
    lo, hi = _unpack(u)
    return jnp.concatenate([jnp.dot(w, lo, preferred_element_type=F32), jnp.dot(w, hi, preferred_element_type=F32)],
                           axis=1)


def _fb_kernel(a_ref, g_ref, asum_ref, kf_out):
    _, two, kb, n, c = a_ref.shape
    scale = 1.0 / (asum_ref[...] + 1e-6)
    for kk in range(kb):
        x = _dot(g_ref[kk], a_ref[0, :, kk].reshape(two * n, c)) * scale
        kf_out[kk] = x.reshape(two, n, c)


def _fb_call(a5, gmat, asum, kb):
    _, _, n, _, c = a5.shape
    return pl.pallas_call(
        _fb_kernel,
        grid=(n // kb,),
        in_specs=[pl.BlockSpec((1, 2, kb, n, c), lambda k: (0, 0, k, 0, 0)),
                  pl.BlockSpec((kb, 2 * n, 2 * n), lambda k: (k, 0, 0)), _full(asum.shape)],
        out_specs=pl.BlockSpec((kb, 2, n, c), lambda k: (k, 0, 0, 0)),
        out_shape=jax.ShapeDtypeStruct((n, 2, n, c), F32),
        compiler_params=_params("arbitrary"),
        name="fb",
    )(a5, gmat, asum)


def _mid_kernel(a_ref, g_ref, h_ref, kf_ref, b_out):
    _, two, kb, n, c = a_ref.shape
    for kk in range(kb):
        x = _dot_packed(g_ref[kk], a_ref[0, :, kk].reshape(two * n, c))
        xr, xi = x[:n], x[n:]
        kr, ki = kf_ref[kk, 0], kf_ref[kk, 1]
        y = jnp.concatenate([xr * kr - xi * ki, xr * ki + xi * kr], axis=0)
        b_out[0, :, kk] = _pack(_dot(h_ref[kk], y)).reshape(two, n, c)


def _mid_call(a5, gmat, hmat, kf, order, kb):
    p, _, n, _, c = a5.shape
    return pl.pallas_call(
        _mid_kernel,
        grid=(n // kb, p),
        in_specs=[pl.BlockSpec((1, 2, kb, n, c), lambda k, q: (q, 0, k, 0, 0)),
                  pl.BlockSpec((kb, 2 * n, 2 * n), lambda k, q: (k, 0, 0)),
                  pl.BlockSpec((kb, 2 * n, 2 * n), lambda k, q: (k, 0, 0)),
                  pl.BlockSpec((kb, 2, n, 2 * c), lambda k, q: (k, 0, 0, order))],
        out_specs=pl.BlockSpec((1, 2, kb, n, c), lambda k, q: (q, 0, k, 0, 0)),
        out_shape=jax.ShapeDtypeStruct(a5.shape, jnp.uint32),
        compiler_params=_params("arbitrary", "arbitrary"),
        name="mid",
    )(a5, gmat, hmat, kf)


def _fc_kernel(b_ref, f_ref, u_ref, m_ref, skip_ref, o_out):
    _, two, n, g, c = b_ref.shape
    y = _dot_packed(f_ref[...], b_ref[...].reshape(two * n * g, c)).reshape(u_ref.shape)
    o_out[...] = m_ref[...] * (y + u_ref[...] * skip_ref[...])


def _fc_call(b5, finv, u5, m5, skip_row):
    _, p, hn, n, c = u5.shape
    g = SUBLANES
    blk = pl.BlockSpec((2, 1, hn, g, c), lambda q, j: (0, q, 0, j, 0))
    return pl.pallas_call(
        _fc_kernel,
        grid=(p, n // g),
        in_specs=[pl.BlockSpec((1, 2, n, g, c // 2), lambda q, j: (q, 0, 0, j, 0)), _full(finv.shape), blk, blk,
                  _full(skip_row.shape)],
        out_specs=blk,
        out_shape=jax.ShapeDtypeStruct(u5.shape, F32),
        compiler_params=_params("arbitrary", "arbitrary"),
        name="fc",
    )(b5, finv, u5, m5, skip_row)


def _dft_tables(n):
    hn = n // 2
    k = np.arange(n)[:, None]
    ang = -2.0 * np.pi * (k * np.arange(n)[None, :] % n) / n
    fr, fi = np.cos(ang), np.sin(ang)
    f_data = np.block([[fr[:, :hn], -fi[:, :hn]], [fi[:, :hn], fr[:, :hn]]])
    f_filt = np.concatenate([fr, fi], axis=0)
    er, ei = fr[:hn], -fi[:hn]
    f_inv = np.block([[er, -ei], [ei, er]]) / float(n * n)
    k1 = np.arange(n)[:, None, None]
    k2 = np.arange(n)[None, :, None]
    m2 = np.arange(n)[None, None, :]
    ang2 = -2.0 * np.pi * ((m2 * (k1 + n * k2)) % (n * n)) / (n * n)
    gr, gi = np.cos(ang2).astype(np.float32), np.sin(ang2).astype(np.float32)
    g = np.concatenate([np.concatenate([gr, -gi], axis=2), np.concatenate([gi, gr], axis=2)], axis=1)
    h = np.swapaxes(g, 1, 2)

    def widen(f):
        return np.kron(f, np.eye(SUBLANES))

    return tuple(jnp.asarray(a, BF16) for a in (widen(f_data), widen(f_filt), widen(f_inv), g, h))


def _hyena_filter_tables(seq):
    f32 = np.float32
    t = np.linspace(0.0, 1.0, seq, dtype=f32)[:, None]
    w = (f32(2.0 * math.pi) * np.arange(seq, dtype=f32)[:, None] / f32(seq)).astype(f32)
    f = np.linspace(1e-4, HY_BANDS - 1, HY_BANDS, dtype=f32)[None, :]
    emb = np.concatenate([t, np.cos(f * w), -np.sin(f * w)], axis=-1).astype(f32)
    emb = np.concatenate([emb, emb[:1], emb[:0:-1]], axis=0)
    emb = np.pad(emb, ((0, 0), (0, HY_EMB_PAD - HY_EMB)))
    deltas = np.abs(np.linspace(math.log(HY_DECAY_TARGET) / HY_SLOW_DECAY,
                                math.log(HY_DECAY_TARGET) / HY_FAST_DECAY, HY_WIDTH, dtype=f32))
    return jnp.asarray(emb), jnp.asarray(np.tile(deltas, HY_ORDER)[None, :])


def _hyena(hv, hx1, hx2, w1, b1, w2, b2, w3, freq, skip, kb):
    bsz, seq, c = hv.shape
    n = int(round(math.sqrt(2 * seq)))
    assert n * n == 2 * seq and bsz % 2 == 0
    hn, p = n // 2, bsz // 2
    f_data, f_filt, f_inv, gmat, hmat = _dft_tables(n)

    emb, deltas2 = _hyena_filter_tables(seq)
    w1p = jnp.pad(w1, ((0, HY_EMB_PAD - HY_EMB), (0, 0)))
    w3r = w3.reshape(w3.shape[0], HY_ORDER, 2, c)
    w3sel = jnp.stack([w3r[:, :, 0, :].reshape(-1, HY_ORDER * c), w3r[:, :, 1, :].reshape(-1, HY_ORDER * c)])
    full, asum = _filter_call(emb, w1p, b1[None], w2, b2[None], w3sel, freq[None], deltas2, seq, min(512, seq))
    c2 = HY_ORDER * c
    kf = _fb_call(_fa_call(full.reshape(2, 1, hn, n, c2), f_filt, False), gmat, asum, kb // HY_ORDER)

    def view(t):
        return t.reshape(2, p, hn, n, c)

    def long_conv(u5, m5, order):
        bm = _mid_call(_fa_call(u5, f_data, True), gmat, hmat, kf, order, kb)
        return _fc_call(bm, f_inv, u5, m5, skip[order][None, :])

    z = long_conv(view(hv), view(hx1), 0)
    return long_conv(z, view(hx2), 1).reshape(bsz, seq, c)


def _merge_kernel(x_ref, at_ref, hy_ref, g_ref, mod_ref, wba_ref, wbh_ref, wo_ref, o_ref):
    d = x_ref.shape[2]
    g = g_ref[0].astype(F32)
    y = (jax.nn.sigmoid(g[:, :d]) * _dot(at_ref[0], wba_ref[...])
         + jax.nn.sigmoid(g[:, d:]) * _dot(hy_ref[0], wbh_ref[...]))
    o_ref[0] = x_ref[0] + mod_ref[0, 2:3, :] * _dot(y, wo_ref[...])


def _merge_call(x, attn, hy, gate, mod, wba, wbh, wo, tm):
    bsz, s, d = x.shape

    def tok(w):
        return pl.BlockSpec((1, tm, w), lambda b, i: (b, i, 0))

    return pl.pallas_call(
        _merge_kernel,
        grid=(bsz, s // tm),
        in_specs=[tok(d), tok(attn.shape[2]), tok(hy.shape[2]), tok(2 * d),
                  pl.BlockSpec((1, SUBLANES, d), lambda b, i: (b, 0, 0)),
                  _full(wba.shape), _full(wbh.shape), _full(wo.shape)],
        out_specs=tok(d),
        out_shape=jax.ShapeDtypeStruct((bsz, s, d), F32),
        compiler_params=_params("arbitrary", "arbitrary"),
        name="merge",
    )(x, attn, hy, gate, mod, wba, wbh, wo)


def _route_kernel(xm_ref, mod_ref, nf_ref, wrt_ref, rb_ref, tri_ref, lt_ref, h2_out, w_out, p_out, col_out, row_out):
    tm = xm_ref.shape[1]
    ng, gs = N_GROUPS, GROUP_SIZE

    h2 = _prenorm(xm_ref[0], mod_ref, 3, nf_ref[...])
    h2_out[0] = h2.astype(h2_out.dtype)
    scores = jax.nn.sigmoid(_dot3(wrt_ref[...], h2, NT_DIMS))
    sel = scores + rb_ref[...]
    slabs = [sel[ng * j:ng * (j + 1)] for j in range(gs)]

    top1 = jnp.full((ng, tm), -jnp.inf, F32)
    top2 = top1
    for x in slabs:
        top2 = jnp.maximum(top2, jnp.minimum(top1, x))
        top1 = jnp.maximum(top1, x)
    gscore = top1 + top2
    gid = lax.broadcasted_iota(jnp.int32, (ng, 1), 0)
    rank = jnp.zeros((ng, tm), jnp.int32)
    for g2 in range(ng):
        row = gscore[g2:g2 + 1]
        beats = (row > gscore) | ((row == gscore) & (g2 < gid))
        rank = rank + beats.astype(jnp.int32)
    gmask = rank < TOPK_GROUPS

    cand = [jnp.where(gmask, x, -jnp.inf) for x in slabs]
    eid = [gid * gs + j for j in range(gs)]
    chosen = []
    for _ in range(TOP_K):
        best = functools.reduce(jnp.maximum, cand)
        best = jnp.max(best, axis=0, keepdims=True)
        idx = functools.reduce(jnp.minimum, [jnp.where(cand[j] == best, eid[j], N_EXPERTS) for j in range(gs)])
        idx = jnp.min(idx, axis=0, keepdims=True)
        chosen.append(idx)
        cand = [jnp.where(eid[j] == idx, -jnp.inf, cand[j]) for j in range(gs)]

    mask = [functools.reduce(jnp.logical_or, [eid[j] == idx for idx in chosen]) for j in range(gs)]
    maskb = jnp.concatenate(mask, axis=0)
    wsel = jnp.where(maskb, scores, 0.0)
    w_out[...] = wsel / jnp.sum(wsel, axis=0, keepdims=True) * ROUTE_SCALE

    def extents(cnt, lower_sum):
        units = jnp.floor((cnt + (RUN_ALIGN - 1)) * (1.0 / RUN_ALIGN))
        start = RUN_ALIGN * lower_sum(units.astype(BF16))
        return start, start + RUN_ALIGN * units

    lane = lax.broadcasted_iota(jnp.int32, (N_EXPERTS, LANES), 1)
    sub = lax.broadcasted_iota(jnp.int32, (SUBLANES, N_EXPERTS), 0)
    ts = tri_ref.shape[0]
    for c in range(tm // ts):
        mb = maskb[:, c * ts:(c + 1) * ts]
        maskf = jnp.where(mb, 1.0, 0.0)
        mask16 = maskf.astype(BF16)
        before = jnp.dot(mask16, tri_ref[...], preferred_element_type=F32)
        p_out[:, c * ts:(c + 1) * ts] = jnp.where(mb, before, -1.0).astype(p_out.dtype)
        cnt_c = jnp.sum(maskf, axis=1, keepdims=True)
        start_c, end_c = extents(jnp.broadcast_to(cnt_c, (N_EXPERTS, LANES)),
                                 lambda u: jnp.dot(lt_ref[...], u, preferred_element_type=F32))
        col_out[c] = jnp.where(lane == 0, cnt_c, jnp.where(lane == 1, start_c, end_c))
        cnt_r = lax.dot_general(jnp.ones((SUBLANES, ts), BF16), mask16, NT_DIMS, preferred_element_type=F32)
        start_r, end_r = extents(cnt_r,
                                 lambda u: lax.dot_general(u, lt_ref[...], NT_DIMS, preferred_element_type=F32))
        row_out[c] = jnp.where(sub == 0, start_r, end_r)


def _route_call(xm, mod, norm_ffn, wrt, rbias, lower, ts, tiles_per_step):
    bsz, s, d = xm.shape
    t = bsz * s
    tm = ts * tiles_per_step
    nt = s // tm
    tri = (jnp.arange(ts)[:, None] < jnp.arange(ts)[None, :]).astype(BF16)
    tok = pl.BlockSpec((N_EXPERTS, tm), lambda i: (0, i))
    return pl.pallas_call(
        _route_kernel,
        grid=(t // tm,),
        in_specs=[pl.BlockSpec((1, tm, d), lambda i: (i // nt, i % nt, 0)),
                  pl.BlockSpec((1, SUBLANES, d), lambda i: (i // nt, 0, 0)),
                  _full(norm_ffn.shape), _full(wrt.shape), _full(rbias.shape), _full(tri.shape),
                  _full(lower.shape)],
        out_specs=[pl.BlockSpec((1, tm, d), lambda i: (i // nt, i % nt, 0)), tok, tok,
                   pl.BlockSpec((tiles_per_step, N_EXPERTS, LANES), lambda i: (i, 0, 0)),
                   pl.BlockSpec((tiles_per_step, SUBLANES, N_EXPERTS), lambda i: (i, 0, 0))],
        out_shape=[jax.ShapeDtypeStruct((bsz, s, d), BF16), jax.ShapeDtypeStruct((N_EXPERTS, t), F32),
                   jax.ShapeDtypeStruct((N_EXPERTS, t), BF16),
                   jax.ShapeDtypeStruct((t // ts, N_EXPERTS, LANES), F32),
                   jax.ShapeDtypeStruct((t // ts, SUBLANES, N_EXPERTS), F32)],
        compiler_params=_params("arbitrary"),
        name="route",
    )(xm, mod, norm_ffn, wrt, rbias, tri, lower)


def _pack(x):
    w = x.shape[1] // 2
    lo = lax.bitcast_convert_type(x[:, :w].astype(BF16).astype(F32), jnp.uint32)
    hi = lax.bitcast_convert_type(x[:, w:].astype(BF16).astype(F32), jnp.uint32)
    return hi | (lo >> 16)


def _unpack(u):
    lo = lax.bitcast_convert_type(u << 16, F32).astype(BF16)
    hi = lax.bitcast_convert_type(u & jnp.uint32(0xFFFF0000), F32).astype(BF16)
    return lo, hi


def _pow2_pieces(units, limit):
    bit = 1
    while bit * 2 <= limit:
        bit *= 2
    while bit:
        yield (units & bit) != 0, units & ~(2 * bit - 1), bit
        bit //= 2


def _rows_copy(vm_ref, hbm_ref, sem, vm_row, hbm_row, rows, to_hbm):
    v = vm_ref.at[pl.ds(pl.multiple_of(vm_row, RUN_ALIGN), rows), :]
    h = hbm_ref.at[pl.ds(pl.multiple_of(hbm_row, RUN_ALIGN), rows), :]
    return pltpu.make_async_copy(v, h, sem) if to_hbm else pltpu.make_async_copy(h, v, sem)


def _run_copies(vm_ref, hbm_ref, sem, n8, vm_row, hbm_row, limit, to_hbm, act):
    def emit(pieces):
        for on, off, size in pieces:
            @pl.when(on)
            def _():
                act(_rows_copy(vm_ref, hbm_ref, sem, vm_row + RUN_ALIGN * off, hbm_row + RUN_ALIGN * off,
                               RUN_ALIGN * size, to_hbm))

    pieces = list(_pow2_pieces(n8, limit))
    long_pieces = [p for p in pieces if p[2] >= LONG_RUN]
    if long_pieces:
        pl.when(n8 >= LONG_RUN)(lambda: emit(long_pieces))
    emit([p for p in pieces if p[2] < LONG_RUN])


def _wait_rows(vm_ref, hbm_ref, sem, units, limit, to_hbm):
    for on, _, size in _pow2_pieces(units, limit):
        @pl.when(on)
        def _():
            _rows_copy(vm_ref, hbm_ref, sem, 0, 0, RUN_ALIGN * size, to_hbm).wait()


def _dispatch_kernel(n8_ref, ls_ref, gs_ref, ts_ref, t8_ref, nu_ref, pos_ref, ext_ref, h_ref, xs_out, srt2, zbuf,
                     sems):
    step = pl.program_id(0)
    tm = h_ref.shape[0]
    rows = srt2.shape[1]
    slot = step % 2
    srt, sem = srt2.at[slot], sems.at[slot]
    rid = lax.broadcasted_iota(jnp.int32, (rows, 1), 0).astype(F32)
    start = ext_ref[0, 0:1, :]
    member = jnp.where((rid >= start) & (rid < ext_ref[0, 1:2, :]), 1.0, 0.0)
    offset = rid - jnp.sum(member * start, axis=1, keepdims=True)
    pos = jnp.dot(member.astype(BF16), pos_ref[...], preferred_element_type=F32)
    sel = jnp.where(pos == offset, 1.0, 0.0).astype(BF16)
    srt[...] = _pack(jnp.dot(sel, h_ref[...], preferred_element_type=F32))

    def send(e, c):
        i = step * N_EXPERTS + e
        _run_copies(srt, xs_out, sem, n8_ref[i], ls_ref[i], gs_ref[i], tm // RUN_ALIGN, True, lambda cp: cp.start())
        return c

    lax.fori_loop(0, N_EXPERTS, send, 0)

    def wait_tile(tile, s):
        last = tile * N_EXPERTS + N_EXPERTS - 1
        _wait_rows(srt2.at[s], xs_out, sems.at[s], ls_ref[last] // RUN_ALIGN + n8_ref[last], rows // RUN_ALIGN, True)

    pl.when(step > 0)(lambda: wait_tile(step - 1, 1 - slot))

    @pl.when(step == pl.num_programs(0) - 1)
    def _():
        wait_tile(step, slot)
        zbuf[...] = jnp.zeros(zbuf.shape, zbuf.dtype)
        nblk = xs_out.shape[0] // EXPERT_BLOCK

        def fill(act):
            def tails(e, c):
                _run_copies(zbuf, xs_out, sem, t8_ref[e], 0, ts_ref[e], EXPERT_BLOCK // RUN_ALIGN - 1, True, act)
                return c

            def blocks(b, c):
                act(pltpu.make_async_copy(
                    zbuf, xs_out.at[pl.ds(pl.multiple_of(b * EXPERT_BLOCK, EXPERT_BLOCK), EXPERT_BLOCK), :], sem))
                return c

            lax.fori_loop(0, N_EXPERTS, tails, 0)
            lax.fori_loop(nu_ref[0], nblk, blocks, 0)

        fill(lambda cp: cp.start())
        fill(lambda cp: cp.wait())


def _dispatch_call(tables, pos_et, ext_rows, h2, nblk, tm):
    t, d = h2.shape
    lrows = TOP_K * tm + N_EXPERTS * RUN_ALIGN
    return pl.pallas_call(
        _dispatch_kernel,
        grid_spec=pltpu.PrefetchScalarGridSpec(
            num_scalar_prefetch=len(tables), grid=(t // tm,),
            in_specs=[pl.BlockSpec((N_EXPERTS, tm), lambda i, *_: (0, i)),
                      pl.BlockSpec((1,) + ext_rows.shape[1:], lambda i, *_: (i, 0, 0)),
                      pl.BlockSpec((tm, d), lambda i, *_: (i, 0))],
            out_specs=pl.BlockSpec(memory_space=pl.ANY),
            scratch_shapes=[pltpu.VMEM((2, lrows, d // 2), jnp.uint32),
                            pltpu.VMEM((EXPERT_BLOCK, d // 2), jnp.uint32), pltpu.SemaphoreType.DMA((2,))]),
        out_shape=jax.ShapeDtypeStruct((nblk * EXPERT_BLOCK, d // 2), jnp.uint32),
        compiler_params=_params("arbitrary"),
        name="dispatch",
    )(*tables, pos_et, ext_rows, h2)


def _expert_kernel(blk_ref, nused_ref, x_ref, wg_ref, wu_ref, wd_ref, y_ref, wgu_sc, wd_sc):
    i = pl.program_id(0)
    used = i < nused_ref[0]

    @pl.when(used & ((i == 0) | (blk_ref[i] != blk_ref[jnp.maximum(i - 1, 0)])))
    def _():
        wgu_sc[:, :EXPERT_FF] = wg_ref[0].astype(BF16)
        wgu_sc[:, EXPERT_FF:] = wu_ref[0].astype(BF16)
        wd_sc[...] = wd_ref[0].astype(BF16)

    @pl.when(used)
    def _():
        lo, hi = _unpack(x_ref[...])
        half = lo.shape[1]
        gu = (jnp.dot(lo, wgu_sc[:half, :], preferred_element_type=F32)
              + jnp.dot(hi, wgu_sc[half:, :], preferred_element_type=F32))
        a = _silu(gu[:, :EXPERT_FF]) * gu[:, EXPERT_FF:]
        y_ref[...] = _pack(_dot(a, wd_sc[...]))

    @pl.when(jnp.logical_not(used))
    def _():
        y_ref[...] = jnp.zeros(y_ref.shape, y_ref.dtype)


def _expert_call(blk_e, nused, xs, wg, wu, wd):
    rows, d = xs.shape
    nblk = rows // EXPERT_BLOCK

    def row_map(i, blk, nu):
        return (jnp.minimum(i, nu[0] - 1), 0)

    def of_expert(w):
        return pl.BlockSpec((1,) + w.shape[1:], lambda i, blk, nu: (blk[i], 0, 0))

    return pl.pallas_call(
        _expert_kernel,
        grid_spec=pltpu.PrefetchScalarGridSpec(
            num_scalar_prefetch=2, grid=(nblk,),
            in_specs=[pl.BlockSpec((EXPERT_BLOCK, d), row_map), of_expert(wg), of_expert(wu), of_expert(wd)],
            out_specs=pl.BlockSpec((EXPERT_BLOCK, d), lambda i, blk, nu: (i, 0)),
            scratch_shapes=[pltpu.VMEM((wg.shape[1], 2 * EXPERT_FF), BF16), pltpu.VMEM(wd.shape[1:], BF16)]),
        out_shape=jax.ShapeDtypeStruct((rows, d), jnp.uint32),
        compiler_params=_params("arbitrary"),
        name="expert",
    )(blk_e, nused, xs, wg, wu, wd)


def _combine_kernel(n8_ref, ls_ref, gs_ref, ys_hbm, pos_ref, w_ref, ext_ref, xm_ref, h_ref, mod_ref, wsgu_ref,
                    wsd_ref, fn_ref, o_ref, ybuf2, sems):
    step = pl.program_id(0)
    tm = xm_ref.shape[0]
    rows = ybuf2.shape[1]
    slot = step % 2
    ybuf, sem = ybuf2.at[slot], sems.at[slot]

    def fetch(tile, s):
        def body(e, c):
            i = tile * N_EXPERTS + e
            _run_copies(ybuf2.at[s], ys_hbm, sems.at[s], n8_ref[i], ls_ref[i], gs_ref[i], tm // RUN_ALIGN, False,
                        lambda cp: cp.start())
            return c
        lax.fori_loop(0, N_EXPERTS, body, 0)

    pl.when(step == 0)(lambda: fetch(step, slot))
    pl.when(step + 1 < pl.num_programs(0))(lambda: fetch(step + 1, 1 - slot))
    gu = _dot(h_ref[...], wsgu_ref[...])
    ff = gu.shape[1] // 2
    shared = _dot(_silu(gu[:, :ff]) * gu[:, ff:], wsd_ref[...])
    cid = lax.broadcasted_iota(jnp.int32, (1, rows), 1).astype(F32)
    start = ext_ref[0, :, 1:2]
    member = jnp.where((cid >= start) & (cid < ext_ref[0, :, 2:3]), 1.0, 0.0)
    offset = cid - jnp.sum(member * start, axis=0, keepdims=True)
    member = member.astype(BF16)
    pos = jnp.dot(pos_ref[...], member, preferred_element_type=F32)
    mix = jnp.where(pos == offset, jnp.dot(w_ref[...].astype(BF16), member, preferred_element_type=F32), 0.0)
    mix = mix.astype(BF16)
    last = step * N_EXPERTS + N_EXPERTS - 1
    filled = ls_ref[last] + RUN_ALIGN * n8_ref[last]
    _wait_rows(ybuf, ys_hbm, sem, filled // RUN_ALIGN, rows // RUN_ALIGN, False)
    rid = lax.broadcasted_iota(jnp.int32, (rows, 1), 0)
    lo, hi = _unpack(jnp.where(rid < filled, ybuf[...], jnp.uint32(0)))
    routed = jnp.concatenate([jnp.dot(mix, lo, preferred_element_type=F32),
                              jnp.dot(mix, hi, preferred_element_type=F32)], axis=1)
    x = xm_ref[...] + mod_ref[0, 5:6, :] * (routed + shared)
    o_ref[...] = _rms(x, fn_ref[...])


def _combine_call(tables, ys, pos_te, w_te, ext_cols, xm, h2, mod, wsgu, wsd, final_norm, tm, tiles_per_batch):
    t, d = xm.shape
    lrows = TOP_K * tm + N_EXPERTS * RUN_ALIGN
    tok = pl.BlockSpec((tm, d), lambda i, *_: (i, 0))
    per_e = pl.BlockSpec((tm, N_EXPERTS), lambda i, *_: (i, 0))
    return pl.pallas_call(
        _combine_kernel,
        grid_spec=pltpu.PrefetchScalarGridSpec(
            num_scalar_prefetch=len(tables), grid=(t // tm,),
            in_specs=[pl.BlockSpec(memory_space=pl.ANY), per_e, per_e,
                      pl.BlockSpec((1,) + ext_cols.shape[1:], lambda i, *_: (i, 0, 0)), tok, tok,
                      pl.BlockSpec((1, SUBLANES, d), lambda i, *_: (i // tiles_per_batch, 0, 0)),
                      _full(wsgu.shape), _full(wsd.shape), _full(final_norm.shape)],
            out_specs=tok,
            scratch_shapes=[pltpu.VMEM((2, lrows, d // 2), jnp.uint32), pltpu.SemaphoreType.DMA((2,))]),
        out_shape=jax.ShapeDtypeStruct((t, d), F32),
        compiler_params=_params("arbitrary"),
        name="combine",
    )(*tables, ys, pos_te, w_te, ext_cols, xm, h2, mod, wsgu, wsd, final_norm)


def _moe(xm, mod, norm_ffn, w_router, router_bias, wg, wu, wd, wsg, wsu, wsd, final_norm, tm):
    bsz, s, d = xm.shape
    t = bsz * s
    nt = t // tm
    perm = (np.arange(N_EXPERTS) % N_GROUPS) * GROUP_SIZE + np.arange(N_EXPERTS) // N_GROUPS
    wrt = w_router.T[perm]
    rbias = router_bias[perm][:, None]
    lower = jnp.asarray(perm[None, :] < perm[:, None], BF16)
    h2, w_et, pos_et, ext_cols, ext_rows = _route_call(xm, mod, norm_ffn, wrt, rbias, lower, tm, ROUTE_TILES)

    inv = np.argsort(perm)
    n8 = (ext_cols[:, :, 0].astype(jnp.int32)[:, inv] + (RUN_ALIGN - 1)) // RUN_ALIGN
    run = RUN_ALIGN * n8
    ls = jnp.cumsum(run, axis=1) - run
    tot = jnp.sum(run, axis=0)
    padded = (tot + EXPERT_BLOCK - 1) // EXPERT_BLOCK * EXPERT_BLOCK
    pad_end = jnp.cumsum(padded)
    gs = (pad_end - padded)[None, :] + jnp.cumsum(run, axis=0) - run
    nblk = -(-(t * TOP_K + nt * N_EXPERTS * (RUN_ALIGN - 1)) // EXPERT_BLOCK) + N_EXPERTS
    blk_first = jnp.arange(nblk, dtype=jnp.int32)[:, None] * EXPERT_BLOCK
    blk_e = jnp.minimum(jnp.sum((pad_end[None, :] <= blk_first).astype(jnp.int32), axis=1), N_EXPERTS - 1)
    nused = (pad_end[-1:] // EXPERT_BLOCK).astype(jnp.int32)
    tables = [a.reshape(-1).astype(jnp.int32) for a in (n8, ls, gs)]
    tails = [(pad_end - padded + tot).astype(jnp.int32), ((padded - tot) // RUN_ALIGN).astype(jnp.int32), nused]

    h2f = h2.reshape(t, d)
    xs = _dispatch_call(tables + tails, pos_et, ext_rows, h2f, nblk, tm)
    ys = _expert_call(blk_e, nused, xs, wg, wu, wd)
    wsgu = jnp.concatenate([wsg, wsu], axis=1).astype(BF16)
    out = _combine_call(tables, ys, pos_et.T, w_et.T, ext_cols, xm.reshape(t, d), h2f, mod, wsgu, wsd.astype(BF16),
                        final_norm, tm, s // tm)
    return out.reshape(bsz, s, d)


def _rope_tables(s):
    f32 = np.float32
    rows = s // GRID_W
    row = np.broadcast_to(np.arange(rows, dtype=f32)[:, None], (rows, GRID_W)).reshape(-1)
    col = np.broadcast_to(np.arange(GRID_W, dtype=f32)[None, :], (rows, GRID_W)).reshape(-1)
    half = QK_ROPE // 2
    inv_freq = (f32(ROPE_THETA) ** (-np.arange(0, half, 2, dtype=f32) / f32(half))).astype(f32)
    ar, ac = row[:, None] * inv_freq, col[:, None] * inv_freq
    ones = np.ones((s, QK_NOPE), f32)
    tail = HEAD_PAD - QK_NOPE - QK_ROPE
    cos_t = np.concatenate([ones, np.cos(ar), np.cos(ar), np.cos(ac), np.cos(ac), np.ones((s, tail), f32)], 1)
    sin_t = np.concatenate([0 * ones, -np.sin(ar), np.sin(ar), -np.sin(ac), np.sin(ac), np.zeros((s, tail), f32)], 1)
    return jnp.asarray(cos_t, F32), jnp.asarray(sin_t, F32)


_Q4 = QK_ROPE // 4
ROPE_SWAP = np.concatenate([np.arange(_Q4, 2 * _Q4), np.arange(0, _Q4), np.arange(3 * _Q4, 4 * _Q4),
                            np.arange(2 * _Q4, 3 * _Q4)])


def _rope_slot(w, swap):
    if swap:
        w = w[..., ROPE_SWAP]
    pad = [(0, 0)] * (w.ndim - 1) + [(QK_NOPE, HEAD_PAD - QK_NOPE - QK_ROPE)]
    return jnp.pad(w, pad)


TILES = dict(inproj=512, tq=2048, tk=1408, fft_kb=8, merge=512, moe=256)


def kernel(x, c, ctx, c_ctx, w_mod, b_mod, norm_mix, norm_ffn, w_in, b_in, q_norm, w_uq, kv_norm, w_ukv, w_branch_attn, hy_conv_w, hy_conv_b, hy_filt_w1, hy_filt_b1, hy_filt_w2, hy_filt_b2, hy_filt_w3, hy_filt_freq, hy_skip, w_branch_hyena, w_out, w_router, router_bias, w_exp_gate, w_exp_up, w_exp_down, w_sh_gate, w_sh_up, w_sh_down, final_norm):
    bsz, s, d = x.shape
    tl = TILES
    assert w_mod.shape[0] == 1, "single-layer trunk"
    i = 0

    rows = -(-(bsz + 1) // SUBLANES) * SUBLANES
    c_rows = jnp.pad(jnp.concatenate([c, c_ctx[None]], axis=0), ((0, rows - bsz - 1), (0, 0)))
    mod_all = _mod_call(c_rows, w_mod[i], b_mod[i])
    mod_all = jnp.pad(mod_all.reshape(rows, 6, d), ((0, 0), (0, SUBLANES - 6), (0, 0)))
    mod, modc = mod_all[:bsz], mod_all[bsz:bsz + 1]

    cuts = np.cumsum([Q_LORA, KV_LORA, QK_ROPE, 3 * HY_WIDTH])
    wi, bi = w_in[i], b_in[i][None]
    w_q, w_kv, w_pe, w_hy, w_g = jnp.split(wi, cuts, axis=1)
    b_q, b_kv, b_pe, b_hy, b_g = jnp.split(bi, cuts, axis=1)
    wa = jnp.concatenate([w_q, w_kv, _rope_slot(w_pe, False), _rope_slot(w_pe, True)], axis=1).astype(BF16)
    ba = jnp.concatenate([b_q, b_kv, _rope_slot(b_pe, False), _rope_slot(b_pe, True)], axis=1)
    wq3 = w_uq[i].reshape(Q_LORA, N_HEADS, QK_NOPE + QK_ROPE) * (ATTN_SCALE * math.log2(math.e))
    tail = ((0, 0), (0, 0), (0, HEAD_PAD - QK_NOPE))
    wuq = (jnp.pad(wq3[..., :QK_NOPE], tail) + _rope_slot(wq3[..., QK_NOPE:], False)).reshape(Q_LORA, -1).astype(BF16)
    wuqs = _rope_slot(wq3[..., QK_NOPE:], True).reshape(Q_LORA, -1).astype(BF16)
    wkv3 = w_ukv[i].reshape(KV_LORA, N_HEADS, QK_NOPE + V_HEAD)
    wuk = jnp.pad(wkv3[..., :QK_NOPE], tail).reshape(KV_LORA, -1).astype(BF16)
    wuvt = wkv3[..., QK_NOPE:].reshape(KV_LORA, -1).T.astype(BF16)
    nm, qn, kvn = norm_mix[i][None], q_norm[i][None], kv_norm[i][None]

    w_c = jnp.concatenate([w_kv, _rope_slot(w_pe, False)], axis=1).astype(BF16)
    b_c = jnp.concatenate([b_kv, _rope_slot(b_pe, False)], axis=1)
    k_buf, vt_buf = _ctx_call(ctx, modc, nm, w_c, b_c, kvn, wuk, wuvt, s)

    cos_t, sin_t = _rope_tables(s)
    q, k, vt, hv, hx1, hx2, gate = _inproj_call(
        x, mod, nm, wa, ba, w_hy.astype(BF16), b_hy, w_g.astype(BF16), b_g, qn, wuq, wuqs, kvn, wuk, wuvt,
        cos_t, sin_t, hy_conv_w[i], hy_conv_b[i][None], k_buf, vt_buf, tl["inproj"])

    attn = _attn_call(q, k, vt, tl["tq"], tl["tk"])
    hy = _hyena(hv, hx1, hx2, hy_filt_w1[i], hy_filt_b1[i], hy_filt_w2[i], hy_filt_b2[i], hy_filt_w3[i],
                hy_filt_freq[i], hy_skip[i], tl["fft_kb"])
    xm = _merge_call(x, attn, hy, gate, mod, w_branch_attn[i].astype(BF16), w_branch_hyena[i].astype(BF16),
                     w_out[i].astype(BF16), tl["merge"])
    return _moe(xm, mod, norm_ffn[i][None], w_router[i], router_bias[i], w_exp_gate[i], w_exp_up[i], w_exp_down[i],
                w_sh_gate[i], w_sh_up[i], w_sh_down[i], final_norm[None], tl["moe"])
```
